```python
import jax, jax.numpy as jnp
from jax import lax
import numpy as np

D_MODEL = 1024
BATCH = 8
SEQ = 8192
DEPTH = 2

CHUNK = 64
Q_BLOCK = 128
D_FF = 2816
BRANCH_WIDTH = 512
N_BRANCH = 3
LRU_WIDTH = BRANCH_WIDTH
LRU_BLOCKS = 8
LRU_BLOCK = LRU_WIDTH // LRU_BLOCKS
CONV_WIDTH = 4
LRU_C = 8.0
GLA_HEADS = 4
GLA_DV = BRANCH_WIDTH // GLA_HEADS
GLA_DK = GLA_DV // 2
GLA_LOWRANK = 16
GLA_TAU = 16.0
FOX_HEADS = 8
FOX_DH = BRANCH_WIDTH // FOX_HEADS
PLE_DIM = 256
LN_EPS = 1e-5
RMS_EPS = 1e-6
DEEPNORM_ALPHA = (2 * DEPTH) ** 0.25
DEEPNORM_BETA = (8 * DEPTH) ** -0.25

SPLIT_SIZES = (
    LRU_WIDTH,
    LRU_WIDTH,
    GLA_HEADS * GLA_DK,
    GLA_HEADS * GLA_DK,
    GLA_HEADS * GLA_DV,
    GLA_LOWRANK,
    GLA_HEADS * GLA_DV,
    FOX_HEADS * FOX_DH,
    FOX_HEADS * FOX_DH,
    FOX_HEADS * FOX_DH,
    FOX_HEADS,
    N_BRANCH * D_MODEL,
)
SPLIT_POINTS = tuple(int(v) for v in np.cumsum(SPLIT_SIZES)[:-1])
D_IN = int(sum(SPLIT_SIZES))

kernel_name = 'hybrid_rglru_gla_fox_macaron_deepnorm'


def layer_norm(x, g, b):
    xf = x.astype(jnp.float32)
    mu = jnp.mean(xf, axis=-1, keepdims=True)
    var = jnp.mean(jnp.square(xf - mu), axis=-1, keepdims=True)
    y = (xf - mu) * lax.rsqrt(var + LN_EPS) * g.astype(jnp.float32) + b.astype(jnp.float32)
    return y.astype(x.dtype)


def swiglu(x, w_up, w_down):
    gate, up = jnp.split(x @ w_up, 2, axis=-1)
    return (jax.nn.silu(gate) * up) @ w_down


def linear_scan(a, b):
    def combine(left, right):
        return (left[0] * right[0], right[0] * left[1] + right[1])
    return lax.associative_scan(combine, (a, b), axis=1)[1]


def causal_depthwise_conv(x, w, b):
    y = lax.conv_general_dilated(
        x, w[:, None, :], window_strides=(1,), padding=[(CONV_WIDTH - 1, 0)],
        dimension_numbers=('NWC', 'WIO', 'NWC'), feature_group_count=x.shape[-1])
    return y + b


def rg_lru(x, wa, ba, wx, bx, lam):
    bsz, seq, width = x.shape
    f32 = jnp.float32
    xf = x.astype(f32)
    xb = xf.reshape(bsz, seq, LRU_BLOCKS, LRU_BLOCK)
    r = jax.nn.sigmoid(jnp.einsum('bsnc,ncd->bsnd', xb, wa.astype(f32)).reshape(bsz, seq, width) + ba.astype(f32))
    i = jax.nn.sigmoid(jnp.einsum('bsnc,ncd->bsnd', xb, wx.astype(f32)).reshape(bsz, seq, width) + bx.astype(f32))
    log_a = -LRU_C * r * jax.nn.softplus(-lam.astype(f32))
    a = jnp.exp(log_a)
    mult = jnp.sqrt(-jnp.expm1(2.0 * log_a))
    h = linear_scan(a, mult * (i * xf))
    return h.astype(x.dtype)


def gla(q, k, v, g_low, w_g2, b_g, norm_g, out_gate):
    bsz, seq, _ = q.shape
    nc = seq // CHUNK
    f32 = jnp.float32
    qc = q.astype(f32).reshape(bsz, nc, CHUNK, GLA_HEADS, GLA_DK) * (GLA_DK ** -0.5)
    kc = k.astype(f32).reshape(bsz, nc, CHUNK, GLA_HEADS, GLA_DK)
    vc = v.astype(f32).reshape(bsz, nc, CHUNK, GLA_HEADS, GLA_DV)
    log_alpha = jax.nn.log_sigmoid((g_low @ w_g2 + b_g).astype(f32)) / GLA_TAU
    log_alpha = log_alpha.reshape(bsz, nc, CHUNK, GLA_HEADS, GLA_DK)
    g_cum = jnp.cumsum(log_alpha, axis=2)
    g_tot = g_cum[:, :, -1]
    k_dec = kc * jnp.exp(g_tot[:, :, None] - g_cum)
    delta = jnp.einsum('bnchk,bnchv->bnhkv', k_dec, vc)
    state = linear_scan(jnp.exp(g_tot)[..., None], delta)
    o = jnp.einsum('bnchk,bnhkv->bnchv', qc, state)
    o = o * lax.rsqrt(jnp.mean(jnp.square(o), axis=-1, keepdims=True) + RMS_EPS)
    o = o.reshape(bsz, seq, GLA_HEADS * GLA_DV) * norm_g.astype(f32)
    return o.astype(q.dtype) * jax.nn.silu(out_gate)


def forgetting_attention(q, k, v, f_logit, b_f):
    bsz, seq, _ = q.shape
    nb = seq // Q_BLOCK
    f32 = jnp.float32
    scale = FOX_DH ** -0.5
    qh = q.reshape(bsz, seq, FOX_HEADS, FOX_DH)
    kh = k.reshape(bsz, seq, FOX_HEADS, FOX_DH).transpose(0, 2, 1, 3)
    vh = v.reshape(bsz, seq, FOX_HEADS, FOX_DH).transpose(0, 2, 1, 3)
    log_f = jax.nn.log_sigmoid((f_logit + b_f).astype(f32))
    f_cum = jnp.cumsum(log_f, axis=1).transpose(0, 2, 1)
    q_blk = qh.reshape(bsz, nb, Q_BLOCK, FOX_HEADS, FOX_DH).transpose(1, 0, 3, 2, 4)
    f_blk = f_cum.reshape(bsz, FOX_HEADS, nb, Q_BLOCK).transpose(2, 0, 1, 3)
    kpos = jnp.arange(seq)

    def attend(args):
        qb, fb, n = args
        logits = jnp.einsum('bhqd,bhkd->bhqk', qb, kh).astype(f32) * scale
        logits = logits + (fb[..., None] - f_cum[:, :, None, :])
        qpos = n * Q_BLOCK + jnp.arange(Q_BLOCK)
        mask = kpos[None, :] <= qpos[:, None]
        probs = jax.nn.softmax(jnp.where(mask, logits, -jnp.inf), axis=-1)
        return jnp.einsum('bhqk,bhkd->bhqd', probs.astype(vh.dtype), vh)

    out = lax.map(attend, (q_blk, f_blk, jnp.arange(nb)))
    return out.transpose(1, 0, 3, 2, 4).reshape(bsz, seq, FOX_HEADS * FOX_DH)


def hybrid_mixer(x, w_in, conv_w, conv_b, lru_wa, lru_ba, lru_wx, lru_bx, lru_lambda,
                 gla_w_g2, gla_b_g, gla_norm_g, fox_b_f, w_branch, w_out):
    bsz, seq, _ = x.shape
    (a_x, a_y, b_q, b_k, b_v, b_low, b_r,
     c_q, c_k, c_v, c_f, gate_logits) = jnp.split(x @ w_in, SPLIT_POINTS, axis=-1)
    y_a = jax.nn.gelu(a_y) * rg_lru(causal_depthwise_conv(a_x, conv_w, conv_b),
                                    lru_wa, lru_ba, lru_wx, lru_bx, lru_lambda)
    y_b = gla(b_q, b_k, b_v, b_low, gla_w_g2, gla_b_g, gla_norm_g, b_r)
    y_c = forgetting_attention(c_q, c_k, c_v, c_f, fox_b_f)
    gates = jax.nn.sigmoid(gate_logits).reshape(bsz, seq, N_BRANCH, D_MODEL)
    merged = (gates[:, :, 0] * (y_a @ w_branch[0])
              + gates[:, :, 1] * (y_b @ w_branch[1])
              + gates[:, :, 2] * (y_c @ w_branch[2]))
    return merged @ w_out


def _fwd_setup_inputs(seed: int = 0) -> dict:
    key = jax.random.key(seed)
    ks = jax.random.split(key, 40)
    f32 = jnp.float32

    def nrm(k, shape, scale):
        return jax.random.normal(k, shape, f32) * scale

    def gain(k, shape):
        return 1.0 + 0.02 * jax.random.normal(k, shape, f32)

    u = jax.random.uniform(ks[12], (DEPTH, LRU_WIDTH), f32, minval=0.9, maxval=0.999)
    a0 = u ** (1.0 / LRU_C)
    lru_lambda = jnp.log(a0) - jnp.log1p(-a0)

    return {
        'x': nrm(ks[0], (BATCH, SEQ, D_MODEL), 1.0),
        'p': nrm(ks[1], (DEPTH, BATCH, SEQ, PLE_DIM), 1.0),
        'ffn1_w_up': nrm(ks[2], (DEPTH, D_MODEL, 2 * D_FF), D_MODEL ** -0.5),
        'ffn1_w_down': nrm(ks[3], (DEPTH, D_FF, D_MODEL), D_FF ** -0.5 * DEEPNORM_BETA),
        'ln1_g': gain(ks[4], (DEPTH, D_MODEL)),
        'ln1_b': nrm(ks[5], (DEPTH, D_MODEL), 0.02),
        'w_in': nrm(ks[6], (DEPTH, D_MODEL, D_IN), D_MODEL ** -0.5),
        'conv_w': nrm(ks[7], (DEPTH, CONV_WIDTH, LRU_WIDTH), CONV_WIDTH ** -0.5),
        'conv_b': nrm(ks[8], (DEPTH, LRU_WIDTH), 0.02),
        'lru_wa': nrm(ks[9], (DEPTH, LRU_BLOCKS, LRU_BLOCK, LRU_BLOCK), LRU_BLOCK ** -0.5),
        'lru_ba': nrm(ks[10], (DEPTH, LRU_WIDTH), 0.02),
        'lru_wx': nrm(ks[11], (DEPTH, LRU_BLOCKS, LRU_BLOCK, LRU_BLOCK), LRU_BLOCK ** -0.5),
        'lru_bx': nrm(ks[13], (DEPTH, LRU_WIDTH), 0.02),
        'lru_lambda': lru_lambda,
        'gla_w_g2': nrm(ks[14], (DEPTH, GLA_LOWRANK, GLA_HEADS * GLA_DK), GLA_LOWRANK ** -0.5),
        'gla_b_g': nrm(ks[15], (DEPTH, GLA_HEADS * GLA_DK), 0.02),
        'gla_norm_g': gain(ks[16], (DEPTH, GLA_HEADS * GLA_DV)),
        'fox_b_f': jax.random.uniform(ks[17], (DEPTH, FOX_HEADS), f32, minval=1.0, maxval=4.0),
        'w_branch': nrm(ks[18], (DEPTH, N_BRANCH, BRANCH_WIDTH, D_MODEL), BRANCH_WIDTH ** -0.5),
        'w_out': nrm(ks[19], (DEPTH, D_MODEL, D_MODEL), D_MODEL ** -0.5 * DEEPNORM_BETA),
        'ln2_g': gain(ks[20], (DEPTH, D_MODEL)),
        'ln2_b': nrm(ks[21], (DEPTH, D_MODEL), 0.02),
        'ffn2_w_up': nrm(ks[22], (DEPTH, D_MODEL, 2 * D_FF), D_MODEL ** -0.5),
        'ffn2_w_down': nrm(ks[23], (DEPTH, D_FF, D_MODEL), D_FF ** -0.5 * DEEPNORM_BETA),
        'ln3_g': gain(ks[24], (DEPTH, D_MODEL)),
        'ln3_b': nrm(ks[25], (DEPTH, D_MODEL), 0.02),
        'ple_w_proj': nrm(ks[26], (DEPTH, PLE_DIM, D_MODEL), PLE_DIM ** -0.5 * DEEPNORM_BETA),
        'ple_w_gate': nrm(ks[27], (DEPTH, D_MODEL, D_MODEL), D_MODEL ** -0.5),
        'ple_b_gate': nrm(ks[28], (DEPTH, D_MODEL), 0.02),
        'ln4_g': gain(ks[29], (DEPTH, D_MODEL)),
        'ln4_b': nrm(ks[30], (DEPTH, D_MODEL), 0.02),
    }


def _fwd_reference(x, p, ffn1_w_up, ffn1_w_down, ln1_g, ln1_b, w_in, conv_w, conv_b,
              lru_wa, lru_ba, lru_wx, lru_bx, lru_lambda, gla_w_g2, gla_b_g, gla_norm_g,
              fox_b_f, w_branch, w_out, ln2_g, ln2_b, ffn2_w_up, ffn2_w_down, ln3_g, ln3_b,
              ple_w_proj, ple_w_gate, ple_b_gate, ln4_g, ln4_b):
    for i in range(DEPTH):
        x = layer_norm(DEEPNORM_ALPHA * x + 0.5 * swiglu(x, ffn1_w_up[i], ffn1_w_down[i]),
                       ln1_g[i], ln1_b[i])
        mix = hybrid_mixer(x, w_in[i], conv_w[i], conv_b[i], lru_wa[i], lru_ba[i], lru_wx[i],
                           lru_bx[i], lru_lambda[i], gla_w_g2[i], gla_b_g[i], gla_norm_g[i],
                           fox_b_f[i], w_branch[i], w_out[i])
        x = layer_norm(DEEPNORM_ALPHA * x + mix, ln2_g[i], ln2_b[i])
        x = layer_norm(DEEPNORM_ALPHA * x + 0.5 * swiglu(x, ffn2_w_up[i], ffn2_w_down[i]),
                       ln3_g[i], ln3_b[i])
        pe = p[i] @ ple_w_proj[i]
        x = layer_norm(DEEPNORM_ALPHA * x + jax.nn.sigmoid(x @ ple_w_gate[i] + ple_b_gate[i]) * pe,
                       ln4_g[i], ln4_b[i])
    return x


import jax as _jax
import jax.numpy as _jnp

TWIN_FORMAT = 'train_step'
FWD_PARAMS = ['x', 'p', 'ffn1_w_up', 'ffn1_w_down', 'ln1_g', 'ln1_b', 'w_in', 'conv_w', 'conv_b', 'lru_wa', 'lru_ba', 'lru_wx', 'lru_bx', 'lru_lambda', 'gla_w_g2', 'gla_b_g', 'gla_norm_g', 'fox_b_f', 'w_branch', 'w_out', 'ln2_g', 'ln2_b', 'ffn2_w_up', 'ffn2_w_down', 'ln3_g', 'ln3_b', 'ple_w_proj', 'ple_w_gate', 'ple_b_gate', 'ln4_g', 'ln4_b']
TWIN_WEIGHTS = ['ffn1_w_up', 'ffn1_w_down', 'ln1_g', 'ln1_b', 'w_in', 'conv_w', 'conv_b', 'lru_wa', 'lru_ba', 'lru_wx', 'lru_bx', 'lru_lambda', 'gla_w_g2', 'gla_b_g', 'gla_norm_g', 'fox_b_f', 'w_branch', 'w_out', 'ln2_g', 'ln2_b', 'ffn2_w_up', 'ffn2_w_down', 'ln3_g', 'ln3_b', 'ple_w_proj', 'ple_w_gate', 'ple_b_gate', 'ln4_g', 'ln4_b']
TWIN_DIFF_INPUT = 'x'
TWIN_INPUTS = ['x', 'p', 'ffn1_w_up', 'ffn1_w_down', 'ln1_g', 'ln1_b', 'w_in', 'conv_w', 'conv_b', 'lru_wa', 'lru_ba', 'lru_wx', 'lru_bx', 'lru_lambda', 'gla_w_g2', 'gla_b_g', 'gla_norm_g', 'fox_b_f', 'w_branch', 'w_out', 'ln2_g', 'ln2_b', 'ffn2_w_up', 'ffn2_w_down', 'ln3_g', 'ln3_b', 'ple_w_proj', 'ple_w_gate', 'ple_b_gate', 'ln4_g', 'ln4_b', 'loss_target', 'm_ffn1_w_up', 'm_ffn1_w_down', 'm_ln1_g', 'm_ln1_b', 'm_w_in', 'm_conv_w', 'm_conv_b', 'm_lru_wa', 'm_lru_ba', 'm_lru_wx', 'm_lru_bx', 'm_lru_lambda', 'm_gla_w_g2', 'm_gla_b_g', 'm_gla_norm_g', 'm_fox_b_f', 'm_w_branch', 'm_w_out', 'm_ln2_g', 'm_ln2_b', 'm_ffn2_w_up', 'm_ffn2_w_down', 'm_ln3_g', 'm_ln3_b', 'm_ple_w_proj', 'm_ple_w_gate', 'm_ple_b_gate', 'm_ln4_g', 'm_ln4_b', 'v_ffn1_w_up', 'v_ffn1_w_down', 'v_ln1_g', 'v_ln1_b', 'v_w_in', 'v_conv_w', 'v_conv_b', 'v_lru_wa', 'v_lru_ba', 'v_lru_wx', 'v_lru_bx', 'v_lru_lambda', 'v_gla_w_g2', 'v_gla_b_g', 'v_gla_norm_g', 'v_fox_b_f', 'v_w_branch', 'v_w_out', 'v_ln2_g', 'v_ln2_b', 'v_ffn2_w_up', 'v_ffn2_w_down', 'v_ln3_g', 'v_ln3_b', 'v_ple_w_proj', 'v_ple_w_gate', 'v_ple_b_gate', 'v_ln4_g', 'v_ln4_b']
TWIN_OUTPUTS = ['loss', 'grad_x', 'grad_ffn1_w_up', 'grad_ffn1_w_down', 'grad_ln1_g', 'grad_ln1_b', 'grad_w_in', 'grad_conv_w', 'grad_conv_b', 'grad_lru_wa', 'grad_lru_ba', 'grad_lru_wx', 'grad_lru_bx', 'grad_lru_lambda', 'grad_gla_w_g2', 'grad_gla_b_g', 'grad_gla_norm_g', 'grad_fox_b_f', 'grad_w_branch', 'grad_w_out', 'grad_ln2_g', 'grad_ln2_b', 'grad_ffn2_w_up', 'grad_ffn2_w_down', 'grad_ln3_g', 'grad_ln3_b', 'grad_ple_w_proj', 'grad_ple_w_gate', 'grad_ple_b_gate', 'grad_ln4_g', 'grad_ln4_b', 'delta_ffn1_w_up', 'delta_ffn1_w_down', 'delta_ln1_g', 'delta_ln1_b', 'delta_w_in', 'delta_conv_w', 'delta_conv_b', 'delta_lru_wa', 'delta_lru_ba', 'delta_lru_wx', 'delta_lru_bx', 'delta_lru_lambda', 'delta_gla_w_g2', 'delta_gla_b_g', 'delta_gla_norm_g', 'delta_fox_b_f', 'delta_w_branch', 'delta_w_out', 'delta_ln2_g', 'delta_ln2_b', 'delta_ffn2_w_up', 'delta_ffn2_w_down', 'delta_ln3_g', 'delta_ln3_b', 'delta_ple_w_proj', 'delta_ple_w_gate', 'delta_ple_b_gate', 'delta_ln4_g', 'delta_ln4_b', 'new_m_ffn1_w_up', 'new_m_ffn1_w_down', 'new_m_ln1_g', 'new_m_ln1_b', 'new_m_w_in', 'new_m_conv_w', 'new_m_conv_b', 'new_m_lru_wa', 'new_m_lru_ba', 'new_m_lru_wx', 'new_m_lru_bx', 'new_m_lru_lambda', 'new_m_gla_w_g2', 'new_m_gla_b_g', 'new_m_gla_norm_g', 'new_m_fox_b_f', 'new_m_w_branch', 'new_m_w_out', 'new_m_ln2_g', 'new_m_ln2_b', 'new_m_ffn2_w_up', 'new_m_ffn2_w_down', 'new_m_ln3_g', 'new_m_ln3_b', 'new_m_ple_w_proj', 'new_m_ple_w_gate', 'new_m_ple_b_gate', 'new_m_ln4_g', 'new_m_ln4_b', 'new_v_ffn1_w_up', 'new_v_ffn1_w_down', 'new_v_ln1_g', 'new_v_ln1_b', 'new_v_w_in', 'new_v_conv_w', 'new_v_conv_b', 'new_v_lru_wa', 'new_v_lru_ba', 'new_v_lru_wx', 'new_v_lru_bx', 'new_v_lru_lambda', 'new_v_gla_w_g2', 'new_v_gla_b_g', 'new_v_gla_norm_g', 'new_v_fox_b_f', 'new_v_w_branch', 'new_v_w_out', 'new_v_ln2_g', 'new_v_ln2_b', 'new_v_ffn2_w_up', 'new_v_ffn2_w_down', 'new_v_ln3_g', 'new_v_ln3_b', 'new_v_ple_w_proj', 'new_v_ple_w_gate', 'new_v_ple_b_gate', 'new_v_ln4_g', 'new_v_ln4_b']
TWIN_LEAF_KINDS = {'loss': 'loss', 'grad_x': 'grad_x', 'grad_ffn1_w_up': 'grad_w', 'grad_ffn1_w_down': 'grad_w', 'grad_ln1_g': 'grad_w', 'grad_ln1_b': 'grad_w', 'grad_w_in': 'grad_w', 'grad_conv_w': 'grad_w', 'grad_conv_b': 'grad_w', 'grad_lru_wa': 'grad_w', 'grad_lru_ba': 'grad_w', 'grad_lru_wx': 'grad_w', 'grad_lru_bx': 'grad_w', 'grad_lru_lambda': 'grad_w', 'grad_gla_w_g2': 'grad_w', 'grad_gla_b_g': 'grad_w', 'grad_gla_norm_g': 'grad_w', 'grad_fox_b_f': 'grad_w', 'grad_w_branch': 'grad_w', 'grad_w_out': 'grad_w', 'grad_ln2_g': 'grad_w', 'grad_ln2_b': 'grad_w', 'grad_ffn2_w_up': 'grad_w', 'grad_ffn2_w_down': 'grad_w', 'grad_ln3_g': 'grad_w', 'grad_ln3_b': 'grad_w', 'grad_ple_w_proj': 'grad_w', 'grad_ple_w_gate': 'grad_w', 'grad_ple_b_gate': 'grad_w', 'grad_ln4_g': 'grad_w', 'grad_ln4_b': 'grad_w', 'delta_ffn1_w_up': 'delta_w', 'delta_ffn1_w_down': 'delta_w', 'delta_ln1_g': 'delta_w', 'delta_ln1_b': 'delta_w', 'delta_w_in': 'delta_w', 'delta_conv_w': 'delta_w', 'delta_conv_b': 'delta_w', 'delta_lru_wa': 'delta_w', 'delta_lru_ba': 'delta_w', 'delta_lru_wx': 'delta_w', 'delta_lru_bx': 'delta_w', 'delta_lru_lambda': 'delta_w', 'delta_gla_w_g2': 'delta_w', 'delta_gla_b_g': 'delta_w', 'delta_gla_norm_g': 'delta_w', 'delta_fox_b_f': 'delta_w', 'delta_w_branch': 'delta_w', 'delta_w_out': 'delta_w', 'delta_ln2_g': 'delta_w', 'delta_ln2_b': 'delta_w', 'delta_ffn2_w_up': 'delta_w', 'delta_ffn2_w_down': 'delta_w', 'delta_ln3_g': 'delta_w', 'delta_ln3_b': 'delta_w', 'delta_ple_w_proj': 'delta_w', 'delta_ple_w_gate': 'delta_w', 'delta_ple_b_gate': 'delta_w', 'delta_ln4_g': 'delta_w', 'delta_ln4_b': 'delta_w', 'new_m_ffn1_w_up': 'new_m', 'new_m_ffn1_w_down': 'new_m', 'new_m_ln1_g': 'new_m', 'new_m_ln1_b': 'new_m', 'new_m_w_in': 'new_m', 'new_m_conv_w': 'new_m', 'new_m_conv_b': 'new_m', 'new_m_lru_wa': 'new_m', 'new_m_lru_ba': 'new_m', 'new_m_lru_wx': 'new_m', 'new_m_lru_bx': 'new_m', 'new_m_lru_lambda': 'new_m', 'new_m_gla_w_g2': 'new_m', 'new_m_gla_b_g': 'new_m', 'new_m_gla_norm_g': 'new_m', 'new_m_fox_b_f': 'new_m', 'new_m_w_branch': 'new_m', 'new_m_w_out': 'new_m', 'new_m_ln2_g': 'new_m', 'new_m_ln2_b': 'new_m', 'new_m_ffn2_w_up': 'new_m', 'new_m_ffn2_w_down': 'new_m', 'new_m_ln3_g': 'new_m', 'new_m_ln3_b': 'new_m', 'new_m_ple_w_proj': 'new_m', 'new_m_ple_w_gate': 'new_m', 'new_m_ple_b_gate': 'new_m', 'new_m_ln4_g': 'new_m', 'new_m_ln4_b': 'new_m', 'new_v_ffn1_w_up': 'new_v', 'new_v_ffn1_w_down': 'new_v', 'new_v_ln1_g': 'new_v', 'new_v_ln1_b': 'new_v', 'new_v_w_in': 'new_v', 'new_v_conv_w': 'new_v', 'new_v_conv_b': 'new_v', 'new_v_lru_wa': 'new_v', 'new_v_lru_ba': 'new_v', 'new_v_lru_wx': 'new_v', 'new_v_lru_bx': 'new_v', 'new_v_lru_lambda': 'new_v', 'new_v_gla_w_g2': 'new_v', 'new_v_gla_b_g': 'new_v', 'new_v_gla_norm_g': 'new_v', 'new_v_fox_b_f': 'new_v', 'new_v_w_branch': 'new_v', 'new_v_w_out': 'new_v', 'new_v_ln2_g': 'new_v', 'new_v_ln2_b': 'new_v', 'new_v_ffn2_w_up': 'new_v', 'new_v_ffn2_w_down': 'new_v', 'new_v_ln3_g': 'new_v', 'new_v_ln3_b': 'new_v', 'new_v_ple_w_proj': 'new_v', 'new_v_ple_w_gate': 'new_v', 'new_v_ple_b_gate': 'new_v', 'new_v_ln4_g': 'new_v', 'new_v_ln4_b': 'new_v'}


def _forward(args):
    return _fwd_reference(*[args[k] for k in FWD_PARAMS])


def _output_shape():
    def fwd():
        inp = _fwd_setup_inputs(0)
        return _fwd_reference(*[inp[k] for k in FWD_PARAMS])
    out = _jax.eval_shape(fwd)
    return out.shape, out.dtype

N_MICROBATCH = 1
ADAM_LR = 0.001
ADAM_B1 = 0.9
ADAM_B2 = 0.999
ADAM_EPS = 1e-08
ADAM_WD = 0.01
ADAM_STEP = 10
PER_EXAMPLE_BATCH_AXIS = {'x': 0, 'p': 1, 'loss_target': 0}
SHARED_INPUTS = []
_WEIGHT_DTYPES = {'ffn1_w_up': _jnp.float32, 'ffn1_w_down': _jnp.float32, 'ln1_g': _jnp.float32, 'ln1_b': _jnp.float32, 'w_in': _jnp.float32, 'conv_w': _jnp.float32, 'conv_b': _jnp.float32, 'lru_wa': _jnp.float32, 'lru_ba': _jnp.float32, 'lru_wx': _jnp.float32, 'lru_bx': _jnp.float32, 'lru_lambda': _jnp.float32, 'gla_w_g2': _jnp.float32, 'gla_b_g': _jnp.float32, 'gla_norm_g': _jnp.float32, 'fox_b_f': _jnp.float32, 'w_branch': _jnp.float32, 'w_out': _jnp.float32, 'ln2_g': _jnp.float32, 'ln2_b': _jnp.float32, 'ffn2_w_up': _jnp.float32, 'ffn2_w_down': _jnp.float32, 'ln3_g': _jnp.float32, 'ln3_b': _jnp.float32, 'ple_w_proj': _jnp.float32, 'ple_w_gate': _jnp.float32, 'ple_b_gate': _jnp.float32, 'ln4_g': _jnp.float32, 'ln4_b': _jnp.float32}
MOMENT_SCALE = {'ffn1_w_up': 1.659065e-02, 'ffn1_w_down': 5.415292e-02, 'ln1_g': 2.065251e+00, 'ln1_b': 1.135416e+00, 'w_in': 2.768289e-02, 'conv_w': 4.529209e-02, 'conv_b': 5.733411e-01, 'lru_wa': 1.654038e-02, 'lru_ba': 1.470422e-02, 'lru_wx': 3.042605e-02, 'lru_bx': 1.349917e-02, 'lru_lambda': 2.325981e-02, 'gla_w_g2': 6.702110e-03, 'gla_b_g': 2.592947e-02, 'gla_norm_g': 4.016829e-02, 'fox_b_f': 2.256098e-01, 'w_branch': 2.540995e-02, 'w_out': 8.794451e-02, 'ln2_g': 2.113140e+00, 'ln2_b': 1.022316e+00, 'ffn2_w_up': 1.607055e-02, 'ffn2_w_down': 5.259768e-02, 'ln3_g': 2.149061e+00, 'ln3_b': 1.032670e+00, 'ple_w_proj': 9.507511e-02, 'ple_w_gate': 1.862648e-02, 'ple_b_gate': 2.499947e-02, 'ln4_g': 4.547828e+01, 'ln4_b': 2.750209e+00}


def _to_microbatches(a, axis):
    t = _jnp.moveaxis(a, axis, 0)
    t = t.reshape((N_MICROBATCH, t.shape[0] // N_MICROBATCH) + t.shape[1:])
    return _jnp.moveaxis(t, 1, axis + 1)


def setup_inputs(seed: int = 0) -> dict:
    inp = _fwd_setup_inputs(seed)
    key = _jax.random.fold_in(_jax.random.key(seed), 7919)
    shape, _ = _output_shape()
    out = dict(inp)
    out["loss_target"] = _jax.random.normal(_jax.random.fold_in(key, 0), shape, _jnp.float32)
    for i, name in enumerate(TWIN_WEIGHTS):
        w = inp[name].astype(_jnp.float32)
        if MOMENT_SCALE is None:
            s = _jnp.sqrt(_jnp.mean(_jnp.square(w)) + 1e-30)
        else:
            s = MOMENT_SCALE[name]
        km, kv = _jax.random.split(_jax.random.fold_in(key, i + 1))
        out[name] = w
        out["m_" + name] = s * _jax.random.normal(km, w.shape, _jnp.float32)
        out["v_" + name] = (s * s) * _jax.random.uniform(kv, w.shape, _jnp.float32, 0.5, 1.5)
    if N_MICROBATCH > 1:
        for name, axis in PER_EXAMPLE_BATCH_AXIS.items():
            out[name] = _to_microbatches(out[name], axis)
    return {'x': out['x'], 'p': out['p'], 'ffn1_w_up': out['ffn1_w_up'], 'ffn1_w_down': out['ffn1_w_down'], 'ln1_g': out['ln1_g'], 'ln1_b': out['ln1_b'], 'w_in': out['w_in'], 'conv_w': out['conv_w'], 'conv_b': out['conv_b'], 'lru_wa': out['lru_wa'], 'lru_ba': out['lru_ba'], 'lru_wx': out['lru_wx'], 'lru_bx': out['lru_bx'], 'lru_lambda': out['lru_lambda'], 'gla_w_g2': out['gla_w_g2'], 'gla_b_g': out['gla_b_g'], 'gla_norm_g': out['gla_norm_g'], 'fox_b_f': out['fox_b_f'], 'w_branch': out['w_branch'], 'w_out': out['w_out'], 'ln2_g': out['ln2_g'], 'ln2_b': out['ln2_b'], 'ffn2_w_up': out['ffn2_w_up'], 'ffn2_w_down': out['ffn2_w_down'], 'ln3_g': out['ln3_g'], 'ln3_b': out['ln3_b'], 'ple_w_proj': out['ple_w_proj'], 'ple_w_gate': out['ple_w_gate'], 'ple_b_gate': out['ple_b_gate'], 'ln4_g': out['ln4_g'], 'ln4_b': out['ln4_b'], 'loss_target': out['loss_target'], 'm_ffn1_w_up': out['m_ffn1_w_up'], 'm_ffn1_w_down': out['m_ffn1_w_down'], 'm_ln1_g': out['m_ln1_g'], 'm_ln1_b': out['m_ln1_b'], 'm_w_in': out['m_w_in'], 'm_conv_w': out['m_conv_w'], 'm_conv_b': out['m_conv_b'], 'm_lru_wa': out['m_lru_wa'], 'm_lru_ba': out['m_lru_ba'], 'm_lru_wx': out['m_lru_wx'], 'm_lru_bx': out['m_lru_bx'], 'm_lru_lambda': out['m_lru_lambda'], 'm_gla_w_g2': out['m_gla_w_g2'], 'm_gla_b_g': out['m_gla_b_g'], 'm_gla_norm_g': out['m_gla_norm_g'], 'm_fox_b_f': out['m_fox_b_f'], 'm_w_branch': out['m_w_branch'], 'm_w_out': out['m_w_out'], 'm_ln2_g': out['m_ln2_g'], 'm_ln2_b': out['m_ln2_b'], 'm_ffn2_w_up': out['m_ffn2_w_up'], 'm_ffn2_w_down': out['m_ffn2_w_down'], 'm_ln3_g': out['m_ln3_g'], 'm_ln3_b': out['m_ln3_b'], 'm_ple_w_proj': out['m_ple_w_proj'], 'm_ple_w_gate': out['m_ple_w_gate'], 'm_ple_b_gate': out['m_ple_b_gate'], 'm_ln4_g': out['m_ln4_g'], 'm_ln4_b': out['m_ln4_b'], 'v_ffn1_w_up': out['v_ffn1_w_up'], 'v_ffn1_w_down': out['v_ffn1_w_down'], 'v_ln1_g': out['v_ln1_g'], 'v_ln1_b': out['v_ln1_b'], 'v_w_in': out['v_w_in'], 'v_conv_w': out['v_conv_w'], 'v_conv_b': out['v_conv_b'], 'v_lru_wa': out['v_lru_wa'], 'v_lru_ba': out['v_lru_ba'], 'v_lru_wx': out['v_lru_wx'], 'v_lru_bx': out['v_lru_bx'], 'v_lru_lambda': out['v_lru_lambda'], 'v_gla_w_g2': out['v_gla_w_g2'], 'v_gla_b_g': out['v_gla_b_g'], 'v_gla_norm_g': out['v_gla_norm_g'], 'v_fox_b_f': out['v_fox_b_f'], 'v_w_branch': out['v_w_branch'], 'v_w_out': out['v_w_out'], 'v_ln2_g': out['v_ln2_g'], 'v_ln2_b': out['v_ln2_b'], 'v_ffn2_w_up': out['v_ffn2_w_up'], 'v_ffn2_w_down': out['v_ffn2_w_down'], 'v_ln3_g': out['v_ln3_g'], 'v_ln3_b': out['v_ln3_b'], 'v_ple_w_proj': out['v_ple_w_proj'], 'v_ple_w_gate': out['v_ple_w_gate'], 'v_ple_b_gate': out['v_ple_b_gate'], 'v_ln4_g': out['v_ln4_g'], 'v_ln4_b': out['v_ln4_b']}


def _loss(weights, diff, rest, loss_target):
    with _jax.named_scope("forward"):
        args = {**rest, TWIN_DIFF_INPUT: diff, **{k: w.astype(_WEIGHT_DTYPES[k]) for k, w in weights.items()}}
        y = _forward(args)
    with _jax.named_scope("loss_head"):
        err = _jnp.square(y.astype(_jnp.float32) - loss_target)
        return 0.5 * _jnp.sum(_jnp.mean(err, axis=-1)) if err.ndim else 0.5 * err


def _adamw(w, g, m, v):
    m = ADAM_B1 * m + (1.0 - ADAM_B1) * g
    v = ADAM_B2 * v + (1.0 - ADAM_B2) * _jnp.square(g)
    m_hat = m / (1.0 - ADAM_B1 ** ADAM_STEP)
    v_hat = v / (1.0 - ADAM_B2 ** ADAM_STEP)
    delta = -ADAM_LR * (m_hat / (_jnp.sqrt(v_hat) + ADAM_EPS) + ADAM_WD * w)
    return delta, m, v


def reference(x, p, ffn1_w_up, ffn1_w_down, ln1_g, ln1_b, w_in, conv_w, conv_b, lru_wa, lru_ba, lru_wx, lru_bx, lru_lambda, gla_w_g2, gla_b_g, gla_norm_g, fox_b_f, w_branch, w_out, ln2_g, ln2_b, ffn2_w_up, ffn2_w_down, ln3_g, ln3_b, ple_w_proj, ple_w_gate, ple_b_gate, ln4_g, ln4_b, loss_target, m_ffn1_w_up, m_ffn1_w_down, m_ln1_g, m_ln1_b, m_w_in, m_conv_w, m_conv_b, m_lru_wa, m_lru_ba, m_lru_wx, m_lru_bx, m_lru_lambda, m_gla_w_g2, m_gla_b_g, m_gla_norm_g, m_fox_b_f, m_w_branch, m_w_out, m_ln2_g, m_ln2_b, m_ffn2_w_up, m_ffn2_w_down, m_ln3_g, m_ln3_b, m_ple_w_proj, m_ple_w_gate, m_ple_b_gate, m_ln4_g, m_ln4_b, v_ffn1_w_up, v_ffn1_w_down, v_ln1_g, v_ln1_b, v_w_in, v_conv_w, v_conv_b, v_lru_wa, v_lru_ba, v_lru_wx, v_lru_bx, v_lru_lambda, v_gla_w_g2, v_gla_b_g, v_gla_norm_g, v_fox_b_f, v_w_branch, v_w_out, v_ln2_g, v_ln2_b, v_ffn2_w_up, v_ffn2_w_down, v_ln3_g, v_ln3_b, v_ple_w_proj, v_ple_w_gate, v_ple_b_gate, v_ln4_g, v_ln4_b):
    given = dict(x=x, p=p, ffn1_w_up=ffn1_w_up, ffn1_w_down=ffn1_w_down, ln1_g=ln1_g, ln1_b=ln1_b, w_in=w_in, conv_w=conv_w, conv_b=conv_b, lru_wa=lru_wa, lru_ba=lru_ba, lru_wx=lru_wx, lru_bx=lru_bx, lru_lambda=lru_lambda, gla_w_g2=gla_w_g2, gla_b_g=gla_b_g, gla_norm_g=gla_norm_g, fox_b_f=fox_b_f, w_branch=w_branch, w_out=w_out, ln2_g=ln2_g, ln2_b=ln2_b, ffn2_w_up=ffn2_w_up, ffn2_w_down=ffn2_w_down, ln3_g=ln3_g, ln3_b=ln3_b, ple_w_proj=ple_w_proj, ple_w_gate=ple_w_gate, ple_b_gate=ple_b_gate, ln4_g=ln4_g, ln4_b=ln4_b, loss_target=loss_target, m_ffn1_w_up=m_ffn1_w_up, m_ffn1_w_down=m_ffn1_w_down, m_ln1_g=m_ln1_g, m_ln1_b=m_ln1_b, m_w_in=m_w_in, m_conv_w=m_conv_w, m_conv_b=m_conv_b, m_lru_wa=m_lru_wa, m_lru_ba=m_lru_ba, m_lru_wx=m_lru_wx, m_lru_bx=m_lru_bx, m_lru_lambda=m_lru_lambda, m_gla_w_g2=m_gla_w_g2, m_gla_b_g=m_gla_b_g, m_gla_norm_g=m_gla_norm_g, m_fox_b_f=m_fox_b_f, m_w_branch=m_w_branch, m_w_out=m_w_out, m_ln2_g=m_ln2_g, m_ln2_b=m_ln2_b, m_ffn2_w_up=m_ffn2_w_up, m_ffn2_w_down=m_ffn2_w_down, m_ln3_g=m_ln3_g, m_ln3_b=m_ln3_b, m_ple_w_proj=m_ple_w_proj, m_ple_w_gate=m_ple_w_gate, m_ple_b_gate=m_ple_b_gate, m_ln4_g=m_ln4_g, m_ln4_b=m_ln4_b, v_ffn1_w_up=v_ffn1_w_up, v_ffn1_w_down=v_ffn1_w_down, v_ln1_g=v_ln1_g, v_ln1_b=v_ln1_b, v_w_in=v_w_in, v_conv_w=v_conv_w, v_conv_b=v_conv_b, v_lru_wa=v_lru_wa, v_lru_ba=v_lru_ba, v_lru_wx=v_lru_wx, v_lru_bx=v_lru_bx, v_lru_lambda=v_lru_lambda, v_gla_w_g2=v_gla_w_g2, v_gla_b_g=v_gla_b_g, v_gla_norm_g=v_gla_norm_g, v_fox_b_f=v_fox_b_f, v_w_branch=v_w_branch, v_w_out=v_w_out, v_ln2_g=v_ln2_g, v_ln2_b=v_ln2_b, v_ffn2_w_up=v_ffn2_w_up, v_ffn2_w_down=v_ffn2_w_down, v_ln3_g=v_ln3_g, v_ln3_b=v_ln3_b, v_ple_w_proj=v_ple_w_proj, v_ple_w_gate=v_ple_w_gate, v_ple_b_gate=v_ple_b_gate, v_ln4_g=v_ln4_g, v_ln4_b=v_ln4_b)
    weights = {n: given[n] for n in TWIN_WEIGHTS}
    shared = {n: given[n] for n in SHARED_INPUTS}
    per_example = {n: given[n] for n in ['x', 'p']}
    grad_fn = _jax.value_and_grad(_loss, argnums=(0, 1))

    def one_microbatch(ex, loss_target):
        ex = dict(ex)
        diff = ex.pop(TWIN_DIFF_INPUT)
        return grad_fn(weights, diff, {**shared, **ex}, loss_target)

    if N_MICROBATCH == 1:
        loss, (grad_w, grad_x) = one_microbatch(per_example, given["loss_target"])
    else:
        def body(carry, xs):
            loss_sum, grad_sum = carry
            l_k, (gw_k, gx_k) = one_microbatch(xs[0], xs[1])
            with _jax.named_scope("update"):
                return (loss_sum + l_k, _jax.tree.map(_jnp.add, grad_sum, gw_k)), gx_k

        init = (_jnp.zeros((), _jnp.float32), _jax.tree.map(_jnp.zeros_like, weights))
        (loss, grad_w), grad_x = _jax.lax.scan(body, init, (per_example, given["loss_target"]))
    with _jax.named_scope("update"):
        delta_w, new_m, new_v = {}, {}, {}
        for n in TWIN_WEIGHTS:
            delta_w[n], new_m[n], new_v[n] = _adamw(weights[n], grad_w[n], given["m_" + n], given["v_" + n])
    return (loss, grad_x, *[grad_w[n] for n in TWIN_WEIGHTS], *[delta_w[n] for n in TWIN_WEIGHTS],
            *[new_m[n] for n in TWIN_WEIGHTS], *[new_v[n] for n in TWIN_WEIGHTS])
```

```python
import functools
import math

import jax
import jax.numpy as jnp
from jax import lax
from jax.experimental import pallas as pl
from jax.experimental.pallas import tpu as pltpu

F32 = jnp.float32
BF16 = jnp.bfloat16

N_DEV = 8
MESH_AXES = ("x", "y", "c")
DEPTH = 2
D_MODEL = 1024
D_FF = 2816
BRANCH = 512
CHUNK = 64
GLA_HEADS = 4
GLA_DK = 64
GLA_DV = 128
GLA_LOWRANK = 16
GLA_TAU = 16.0
FOX_HEADS = 8
FOX_DH = 64
PLE_DIM = 256
LRU_C = 8.0
LRU_BLOCKS = 8
LN_EPS = 1e-5
RMS_EPS = 1e-6
ALPHA = (2 * DEPTH) ** 0.25
LANES = 128
NEG_BIG = -1e30

ADAM_LR = 0.001
ADAM_B1 = 0.9
ADAM_B2 = 0.999
ADAM_EPS = 1e-08
ADAM_WD = 0.01
ADAM_STEP = 10

VMEM_LIMIT_BYTES = 56 * 1024 * 1024

U_GATES = 0
U_AX = 3072
U_AY = 3584
U_BQ = 4096
U_BK = 4352
U_BV = 4608
U_BR = 5120
U_CQ = 5632
U_CK = 6144
U_CV = 6656
U_BLOW = 7168
U_CF = 7296
U_WIDTH = 7680
W_IN_SEGMENTS = (
    (0, 512, U_AX), (512, 512, U_AY), (1024, 256, U_BQ), (1280, 256, U_BK), (1536, 512, U_BV),
    (2048, 16, U_BLOW), (2064, 512, U_BR), (2576, 512, U_CQ), (3088, 512, U_CK), (3600, 512, U_CV),
    (4112, 8, U_CF), (4120, 3072, U_GATES),
)

SHARDED = (
    ("ffn1_w_up", 2), ("ffn1_w_down", 1), ("w_in", 2), ("conv_w", 2), ("gla_w_g2", 2), ("w_branch", 3),
    ("w_out", 1), ("ffn2_w_up", 2), ("ffn2_w_down", 1), ("ple_w_proj", 2), ("ple_w_gate", 1),
)
SHARDED_F32_GATHER = ("conv_w", "gla_w_g2")
REPLICATED = ("ln1_g", "ln1_b", "conv_b", "lru_wa", "lru_ba", "lru_wx", "lru_bx", "lru_lambda", "gla_b_g",
              "gla_norm_g", "fox_b_f", "ln2_g", "ln2_b", "ln3_g", "ln3_b", "ple_b_gate", "ln4_g", "ln4_b")
WEIGHTS = ("ffn1_w_up", "ffn1_w_down", "ln1_g", "ln1_b", "w_in", "conv_w", "conv_b", "lru_wa", "lru_ba", "lru_wx",
           "lru_bx", "lru_lambda", "gla_w_g2", "gla_b_g", "gla_norm_g", "fox_b_f", "w_branch", "w_out", "ln2_g",
           "ln2_b", "ffn2_w_up", "ffn2_w_down", "ln3_g", "ln3_b", "ple_w_proj", "ple_w_gate", "ple_b_gate", "ln4_g",
           "ln4_b")


def _sigmoid(x):
    return 1.0 / (1.0 + jnp.exp(-x))


def _log1p_pos(e):
    return jnp.where(e < 1e-4, e * (1.0 - 0.5 * e), jnp.log(1.0 + e))


def _softplus(x):
    return jnp.maximum(x, 0.0) + _log1p_pos(jnp.exp(-jnp.abs(x)))


def _log_sigmoid(x):
    return -_softplus(-x)


def _neg_expm1(y):
    series = -y * (1.0 + y * (0.5 + y * (1.0 / 6.0 + y * (1.0 / 24.0 + y * (1.0 / 120.0)))))
    return jnp.where(y > -0.1, series, 1.0 - jnp.exp(y))


def _silu_and_grad(x):
    s = _sigmoid(x)
    return x * s, s * (1.0 + x * (1.0 - s))


_GELU_C = math.sqrt(2.0 / math.pi)


def _gelu_and_grad(x):
    inner = _GELU_C * (x + 0.044715 * x * x * x)
    t = jnp.tanh(inner)
    g = 0.5 * x * (1.0 + t)
    dg = 0.5 * (1.0 + t) + 0.5 * x * (1.0 - t * t) * _GELU_C * (1.0 + 3.0 * 0.044715 * x * x)
    return g, dg


def _ln_stats(z):
    mu = jnp.mean(z, axis=-1, keepdims=True)
    zc = z - mu
    var = jnp.mean(zc * zc, axis=-1, keepdims=True)
    rstd = lax.rsqrt(var + LN_EPS)
    return zc * rstd, rstd


def _ln_fwd(z, g, b):
    xhat, _ = _ln_stats(z)
    return xhat * g + b


def _ln_bwd(dy, z, g):
    xhat, rstd = _ln_stats(z)
    dxh = dy * g
    m1 = jnp.mean(dxh, axis=-1, keepdims=True)
    m2 = jnp.mean(dxh * xhat, axis=-1, keepdims=True)
    return rstd * (dxh - m1 - xhat * m2), xhat


def _colsum(x):
    return jnp.sum(x, axis=0, keepdims=True)


def _dot(a, b, dims):
    dn = {"nn": (((1,), (0,)), ((), ())), "nt": (((1,), (1,)), ((), ())), "tn": (((0,), (0,)), ((), ()))}[dims]
    return lax.dot_general(a.astype(BF16), b.astype(BF16), dn, preferred_element_type=F32)


def _scan_rows(a, b, length, reverse=False, seg=None):
    rows = lax.broadcasted_iota(jnp.int32, b.shape, 0)
    span = seg if seg else length
    pos = rows % span if seg else rows
    d = 1
    while d < span:
        shift = (length - d) if reverse else d
        valid = (pos < span - d) if reverse else (pos >= d)
        sb = jnp.where(valid, pltpu.roll(b, shift, 0), 0.0)
        if a is None:
            b = b + sb
        else:
            b = b + a * sb
            a = a * jnp.where(valid, pltpu.roll(a, shift, 0), 1.0)
        d *= 2
    return a, b


def _tile(dim, pref):
    if dim <= pref:
        return dim
    best = None
    t = LANES
    while t <= pref:
        if dim % t == 0:
            best = t
        t += LANES
    assert best is not None, (dim, pref)
    return best


def _full_spec(arr):
    nd = arr.ndim
    return pl.BlockSpec(arr.shape, lambda *_: (0,) * nd)


def _mm(name, dims, a_ops, b_ops, terms, n_acc, epilogue, extras, out_dtypes, M, N, K, tm=512, tn=1024, tk=1024):
    tm, tn, tk = _tile(M, tm), _tile(N, tn), _tile(K, tk)
    gm, gn, gk = M // tm, N // tn, K // tk
    if dims == "tn":
        a_spec = pl.BlockSpec((tk, tm), lambda i, j, k: (k, i))
    else:
        a_spec = pl.BlockSpec((tm, tk), lambda i, j, k: (i, k))
    if dims == "nt":
        b_spec = pl.BlockSpec((tn, tk), lambda i, j, k: (j, k))
    else:
        b_spec = pl.BlockSpec((tk, tn), lambda i, j, k: (k, j))
    e_specs, e_arrays = [], []
    for ex in extras:
        if ex[1] == "mn":
            off = ex[2]
            e_specs.append(pl.BlockSpec((tm, tn), functools.partial(lambda i, j, k, off: (i, j + off), off=off)))
        else:
            e_specs.append(pl.BlockSpec((1, tn), lambda i, j, k: (0, j)))
        e_arrays.append(ex[0])
    na, nb, ne, no = len(a_ops), len(b_ops), len(extras), len(out_dtypes)

    def body(*refs):
        a_refs = refs[:na]
        b_refs = refs[na:na + nb]
        e_refs = refs[na + nb:na + nb + ne]
        o_refs = refs[na + nb + ne:na + nb + ne + no]
        acc_refs = refs[na + nb + ne + no:]
        k = pl.program_id(2)

        @pl.when(k == 0)
        def _():
            for acc in acc_refs:
                acc[...] = jnp.zeros_like(acc)

        for r, ai, bi in terms:
            acc_refs[r][...] += _dot(a_refs[ai][...], b_refs[bi][...], dims)

        @pl.when(k == gk - 1)
        def _():
            res = epilogue([acc[...] for acc in acc_refs], *[e[...] for e in e_refs])
            for o, val in zip(o_refs, res):
                o[...] = val.astype(o.dtype)

    outs = pl.pallas_call(
        body,
        name=name,
        grid=(gm, gn, gk),
        in_specs=[a_spec] * na + [b_spec] * nb + e_specs,
        out_specs=[pl.BlockSpec((tm, tn), lambda i, j, k: (i, j))] * no,
        out_shape=[jax.ShapeDtypeStruct((M, N), dt) for dt in out_dtypes],
        scratch_shapes=[pltpu.VMEM((tm, tn), F32)] * n_acc,
        compiler_params=pltpu.CompilerParams(
            dimension_semantics=("parallel", "parallel", "arbitrary"), vmem_limit_bytes=VMEM_LIMIT_BYTES),
    )(*a_ops, *b_ops, *e_arrays)
    return outs


def _mm1(name, dims, a, b, M, N, K, out_dtype=F32, scale=None, **kw):
    def epi(accs):
        return [accs[0] if scale is None else accs[0] * scale]
    return _mm(name, dims, [a], [b], [(0, 0, 0)], 1, epi, [], [out_dtype], M, N, K, **kw)[0]


def _rowwise(name, fn, row_ins, vec_ins, row_outs, sum_outs, S, tr=256, reverse=False):
    tr = min(tr, S)
    g = S // tr
    rmap = (lambda i: (g - 1 - i)) if reverse else (lambda i: i)
    in_specs, arrays = [], []
    for r in row_ins:
        if isinstance(r, tuple):
            arr, width, blk = r
            in_specs.append(pl.BlockSpec((tr, width), functools.partial(lambda i, blk: (rmap(i), blk), blk=blk)))
        else:
            arr = r
            in_specs.append(pl.BlockSpec((tr, arr.shape[1]), lambda i: (rmap(i), 0)))
        arrays.append(arr)
    for v in vec_ins:
        in_specs.append(_full_spec(v))
        arrays.append(v)
    nr, nv, no, ns = len(row_ins), len(vec_ins), len(row_outs), len(sum_outs)

    def body(*refs):
        ins = [r[...] for r in refs[:nr + nv]]
        o_refs = refs[nr + nv:nr + nv + no]
        s_refs = refs[nr + nv + no:]
        outs, sums = fn(*ins)
        for o, val in zip(o_refs, outs):
            o[...] = val.astype(o.dtype)
        if ns:
            i = pl.program_id(0)

            @pl.when(i == 0)
            def _():
                for s, val in zip(s_refs, sums):
                    s[...] = val

            @pl.when(i > 0)
            def _():
                for s, val in zip(s_refs, sums):
                    s[...] += val

    res = pl.pallas_call(
        body,
        name=name,
        grid=(g,),
        in_specs=in_specs,
        out_specs=[pl.BlockSpec((tr, c), lambda i: (rmap(i), 0)) for c, _ in row_outs]
        + [pl.BlockSpec((1, c), lambda i: (0, 0)) for c in sum_outs],
        out_shape=[jax.ShapeDtypeStruct((S, c), dt) for c, dt in row_outs]
        + [jax.ShapeDtypeStruct((1, c), F32) for c in sum_outs],
        compiler_params=pltpu.CompilerParams(
            dimension_semantics=("arbitrary",), vmem_limit_bytes=VMEM_LIMIT_BYTES),
    )(*arrays)
    return res[:no], res[no:]


def _win(arr, offset, width):
    assert offset % width == 0
    return (arr, width, offset // width)


def _exchange(name, buf, all_to_all):
    chunk_shape = buf.shape[1:] if all_to_all else buf.shape

    def body(buf_ref, out_ref, send_sems, recv_sems, local_sem):
        x, y, c = lax.axis_index("x"), lax.axis_index("y"), lax.axis_index("c")
        me = 4 * x + 2 * y + c

        def src_for(dst_index):
            return buf_ref.at[dst_index] if all_to_all else buf_ref

        local = pltpu.make_async_copy(src_for(me), out_ref.at[me], local_sem)
        local.start()
        copies = []
        for k in range(1, N_DEV):
            kx, ky, kc = (k >> 2) & 1, (k >> 1) & 1, k & 1
            px, py, pc = x ^ kx, y ^ ky, c ^ kc
            peer = 4 * px + 2 * py + pc
            send = pltpu.make_async_remote_copy(
                src_ref=src_for(peer), dst_ref=out_ref.at[me], send_sem=send_sems.at[k - 1],
                recv_sem=recv_sems.at[k - 1], device_id=(px, py, pc), device_id_type=pl.DeviceIdType.MESH)
            send.start()
            recv = pltpu.make_async_remote_copy(
                src_ref=src_for(peer), dst_ref=out_ref.at[peer], send_sem=send_sems.at[k - 1],
                recv_sem=recv_sems.at[k - 1], device_id=(px, py, pc), device_id_type=pl.DeviceIdType.MESH)
            copies.append((send, recv))
        for send, recv in copies:
            recv.wait_recv()
        for send, recv in copies:
            send.wait_send()
        local.wait()

    return pl.pallas_call(
        body,
        name=name,
        in_specs=[pl.BlockSpec(memory_space=pltpu.HBM)],
        out_specs=pl.BlockSpec(memory_space=pltpu.HBM),
        out_shape=jax.ShapeDtypeStruct((N_DEV,) + tuple(chunk_shape), buf.dtype),
        scratch_shapes=[pltpu.SemaphoreType.DMA((N_DEV - 1,)), pltpu.SemaphoreType.DMA((N_DEV - 1,)),
                        pltpu.SemaphoreType.DMA],
        compiler_params=pltpu.CompilerParams(has_side_effects=True),
    )(buf)


def _adamw(name, gparts, w, m, v):
    R = w.shape[0]
    tr = _tile_rows(R, 512)
    c1 = 1.0 / (1.0 - ADAM_B1 ** ADAM_STEP)
    c2 = 1.0 / (1.0 - ADAM_B2 ** ADAM_STEP)

    def body(gp_ref, w_ref, m_ref, v_ref, g_ref, d_ref, nm_ref, nv_ref):
        g = gp_ref[0]
        for i in range(1, N_DEV):
            g = g + gp_ref[i]
        nm = ADAM_B1 * m_ref[...] + (1.0 - ADAM_B1) * g
        nv = ADAM_B2 * v_ref[...] + (1.0 - ADAM_B2) * (g * g)
        m_hat = nm * c1
        v_hat = nv * c2
        g_ref[...] = g
        nm_ref[...] = nm
        nv_ref[...] = nv
        d_ref[...] = -ADAM_LR * (m_hat / (jnp.sqrt(v_hat) + ADAM_EPS) + ADAM_WD * w_ref[...])

    row = pl.BlockSpec((tr, LANES), lambda i: (i, 0))
    return pl.pallas_call(
        body,
        name=name,
        grid=(R // tr,),
        in_specs=[pl.BlockSpec((N_DEV, tr, LANES), lambda i: (0, i, 0)), row, row, row],
        out_specs=[row] * 4,
        out_shape=[jax.ShapeDtypeStruct((R, LANES), F32)] * 4,
        compiler_params=pltpu.CompilerParams(dimension_semantics=("parallel",), vmem_limit_bytes=VMEM_LIMIT_BYTES),
    )(gparts, w, m, v)


def _tile_rows(rows, pref):
    t = min(pref, rows)
    while rows % t:
        t -= 8
    return t


PACK_ROWS = 512


def _pack(arrs, dtype):
    flat = jnp.concatenate([a.astype(dtype).reshape(-1) for a in arrs])
    quantum = PACK_ROWS * LANES
    padded = -(-flat.shape[0] // quantum) * quantum
    return jnp.pad(flat, (0, padded - flat.shape[0])).reshape(-1, LANES)


def _unpack(buf, shapes, lead=()):
    flat = buf.reshape(lead + (-1,))
    out, off = [], 0
    for shp in shapes:
        n = math.prod(shp)
        out.append(flat[..., off:off + n].reshape(lead + tuple(shp)))
        off += n
    return out


def _dest_major(full, axis):
    shp = full.shape
    n = shp[axis] // N_DEV
    return jnp.moveaxis(full.reshape(shp[:axis] + (N_DEV, n) + shp[axis + 1:]), axis, 0)


def _from_gathered(g, axis):
    r = jnp.moveaxis(g, 0, axis)
    shp = r.shape
    return r.reshape(shp[:axis] + (shp[axis] * shp[axis + 1],) + shp[axis + 2:])


def _shift_down(a, k):
    return jnp.pad(a, ((k, 0), (0, 0)))[:a.shape[0]] if k else a


def _shift_up(a, k, fill=0.0):
    return jnp.pad(a, ((0, k), (0, 0)), constant_values=fill)[k:] if k else a


def _lru_fwd(u, ax_shift, lw, S):
    T = min(256, S)
    nb = S // T
    row = pl.BlockSpec((T, BRANCH), lambda t: (t, 0))
    vecs = [lw["cw0"], lw["cw1"], lw["cw2"], lw["cw3"], lw["conv_b"], lw["wa"], lw["wx"], lw["ba"], lw["bx"],
            lw["lam"]]

    def body(ax0, ax1, ax2, ax3, ay, cw0, cw1, cw2, cw3, cb, wa, wx, ba, bx, lam, xc_o, r_o, i_o, a_o, h_o, ya_o, hc):
        t = pl.program_id(0)

        @pl.when(t == 0)
        def _():
            hc[...] = jnp.zeros_like(hc)

        xc = cw3[...] * ax0[...] + cw2[...] * ax1[...] + cw1[...] * ax2[...] + cw0[...] * ax3[...] + cb[...]
        r = _sigmoid(_dot(xc, wa[...], "nn") + ba[...])
        gi = _sigmoid(_dot(xc, wx[...], "nn") + bx[...])
        sp = _softplus(-lam[...])
        la = -LRU_C * r * sp
        a = jnp.exp(la)
        mult = jnp.sqrt(_neg_expm1(2.0 * la))
        A, B = _scan_rows(a, mult * gi * xc, T)
        h = B + A * hc[...]
        h_o[...] = h
        hc[...] = h_o[pl.ds(T - 1, 1), :]
        xc_o[...] = xc
        r_o[...] = r
        i_o[...] = gi
        a_o[...] = a
        gy, _ = _gelu_and_grad(ay[...])
        ya_o[...] = (gy * h).astype(ya_o.dtype)

    outs = pl.pallas_call(
        body,
        name="lru_fwd",
        grid=(nb,),
        in_specs=[pl.BlockSpec((T, BRANCH), lambda t: (t, U_AX // BRANCH))] + [row] * 3
        + [pl.BlockSpec((T, BRANCH), lambda t: (t, U_AY // BRANCH))] + [_full_spec(v) for v in vecs],
        out_specs=[row] * 6,
        out_shape=[jax.ShapeDtypeStruct((S, BRANCH), F32)] * 5 + [jax.ShapeDtypeStruct((S, BRANCH), BF16)],
        scratch_shapes=[pltpu.VMEM((1, BRANCH), F32)],
        compiler_params=pltpu.CompilerParams(dimension_semantics=("arbitrary",), vmem_limit_bytes=VMEM_LIMIT_BYTES),
    )(u, *ax_shift[1:], u, *vecs)
    return outs


def _lru_bwd(dya, u, sv, lw, S):
    T = min(256, S)
    nb = S // T
    rrow = pl.BlockSpec((T, BRANCH), lambda t: (nb - 1 - t, 0))
    sq = pl.BlockSpec((BRANCH, BRANCH), lambda t: (0, 0))
    vrow = pl.BlockSpec((1, BRANCH), lambda t: (0, 0))
    h_prev = _shift_down(sv["h"], 1)
    a_next = _shift_up(sv["a"], 1)

    def body(dya_r, ay, h, hp, xc_r, r_r, i_r, a_r, an, wa, wx, lam,
             day_o, dxc_o, dwa_o, dwx_o, dba_o, dbx_o, dlam_o, lcar, tmp):
        t = pl.program_id(0)

        @pl.when(t == 0)
        def _():
            lcar[...] = jnp.zeros_like(lcar)
            dwa_o[...] = jnp.zeros_like(dwa_o)
            dwx_o[...] = jnp.zeros_like(dwx_o)
            dba_o[...] = jnp.zeros_like(dba_o)
            dbx_o[...] = jnp.zeros_like(dbx_o)
            dlam_o[...] = jnp.zeros_like(dlam_o)

        gy, dgy = _gelu_and_grad(ay[...])
        dy = dya_r[...]
        day_o[...] = (dy * h[...] * dgy).astype(day_o.dtype)
        A, B = _scan_rows(an[...], dy * gy, T, reverse=True)
        lmb = B + A * lcar[...]
        tmp[...] = lmb
        lcar[...] = tmp[pl.ds(0, 1), :]
        xc, r, gi, a = xc_r[...], r_r[...], i_r[...], a_r[...]
        sp = _softplus(-lam[...])
        la = -LRU_C * r * sp
        mult = jnp.sqrt(_neg_expm1(2.0 * la))
        da = lmb * hp[...]
        dmult = lmb * gi * xc
        di = lmb * mult * xc
        dxc = lmb * mult * gi
        dla = da * a - dmult * a * a / mult
        dr = dla * (-LRU_C * sp)
        dlam_o[...] += _colsum(dla * (LRU_C * r)) * _sigmoid(-lam[...])
        dpr = dr * r * (1.0 - r)
        dpi = di * gi * (1.0 - gi)
        dba_o[...] += _colsum(dpr)
        dbx_o[...] += _colsum(dpi)
        dxc_o[...] = dxc + _dot(dpr, wa[...], "nt") + _dot(dpi, wx[...], "nt")
        dwa_o[...] += _dot(xc, dpr, "tn")
        dwx_o[...] += _dot(xc, dpi, "tn")

    outs = pl.pallas_call(
        body,
        name="lru_bwd",
        grid=(nb,),
        in_specs=[rrow, pl.BlockSpec((T, BRANCH), lambda t: (nb - 1 - t, U_AY // BRANCH))] + [rrow] * 7
        + [sq, sq, vrow],
        out_specs=[rrow, rrow, sq, sq, vrow, vrow, vrow],
        out_shape=[jax.ShapeDtypeStruct((S, BRANCH), BF16), jax.ShapeDtypeStruct((S, BRANCH), F32),
                   jax.ShapeDtypeStruct((BRANCH, BRANCH), F32), jax.ShapeDtypeStruct((BRANCH, BRANCH), F32),
                   jax.ShapeDtypeStruct((1, BRANCH), F32), jax.ShapeDtypeStruct((1, BRANCH), F32),
                   jax.ShapeDtypeStruct((1, BRANCH), F32)],
        scratch_shapes=[pltpu.VMEM((1, BRANCH), F32), pltpu.VMEM((T, BRANCH), F32)],
        compiler_params=pltpu.CompilerParams(dimension_semantics=("arbitrary",), vmem_limit_bytes=VMEM_LIMIT_BYTES),
    )(dya, u, sv["h"], h_prev, sv["xc"], sv["r"], sv["i"], sv["a"], a_next, lw["wa"], lw["wx"], lw["lam"])
    return outs


def _conv_bwd(dxc, ax_shift, lw, S):
    dxs = [_shift_up(dxc, k) for k in range(4)]

    def fn(d0, d1, d2, d3, a0, a1, a2, a3, cw0, cw1, cw2, cw3):
        dax = cw3 * d0 + cw2 * d1 + cw1 * d2 + cw0 * d3
        return [dax], [_colsum(d0 * a3), _colsum(d0 * a2), _colsum(d0 * a1), _colsum(d0 * a0), _colsum(d0)]

    (dax,), sums = _rowwise("conv_bwd", fn, dxs + list(ax_shift), [lw["cw0"], lw["cw1"], lw["cw2"], lw["cw3"]],
                            [(BRANCH, BF16)], [BRANCH] * 5, S)
    return dax, sums


GLA_QK = GLA_HEADS * GLA_DK
GLA_V = GLA_HEADS * GLA_DV
GLA_SCALE = GLA_DK ** -0.5


def _gla_specs(TB, rev_nb=None):
    def rmap(t):
        return t if rev_nb is None else rev_nb - 1 - t
    return [
        pl.BlockSpec((TB, GLA_QK), lambda t: (rmap(t), U_BQ // GLA_QK)),
        pl.BlockSpec((TB, GLA_QK), lambda t: (rmap(t), U_BK // GLA_QK)),
        pl.BlockSpec((TB, GLA_V), lambda t: (rmap(t), U_BV // GLA_V)),
        pl.BlockSpec((TB, GLA_V), lambda t: (rmap(t), U_BR // GLA_V)),
        pl.BlockSpec((TB, LANES), lambda t: (rmap(t), U_BLOW // LANES)),
    ]


def _gla_gates(gl, wg2, bg, TB):
    pre = _dot(gl, wg2, "nn") + bg
    la = _log_sigmoid(pre) * (1.0 / GLA_TAU)
    _, gc = _scan_rows(None, la, TB, seg=CHUNK)
    return pre, la, gc


def _gla_fwd(u, gw, S):
    TB = min(512, S)
    nb = S // TB
    cpb = TB // CHUNK
    vecs = [gw["wg2"], gw["bg"], gw["ng"], gw["bd"]]

    def body(q_r, k_r, v_r, br_r, gl_r, wg2, bg, ng, bd, yb_o, oraw_o, st_o, st):
        t = pl.program_id(0)

        @pl.when(t == 0)
        def _():
            st[...] = jnp.zeros_like(st)

        _, la, gc = _gla_gates(gl_r[...], wg2[...], bg[...], TB)
        for c in range(cpb):
            sl = slice(c * CHUNK, (c + 1) * CHUNK)
            gt = _colsum(la[sl])
            kdec = k_r[sl, :] * jnp.exp(gt - gc[sl])
            d_t = _dot(v_r[sl, :], kdec, "tn") * bd[...]
            s_new = st[...] * jnp.exp(gt) + d_t
            st[...] = s_new
            st_o[c] = s_new
            oraw_o[sl, :] = _dot(q_r[sl, :] * GLA_SCALE, s_new, "nt")
        for h in range(GLA_HEADS):
            hs = slice(h * GLA_DV, (h + 1) * GLA_DV)
            oh = oraw_o[:, hs]
            on = oh * lax.rsqrt(jnp.mean(oh * oh, axis=-1, keepdims=True) + RMS_EPS)
            sil, _ = _silu_and_grad(br_r[:, hs])
            yb_o[:, hs] = (on * ng[:, hs] * sil).astype(yb_o.dtype)

    return pl.pallas_call(
        body,
        name="gla_fwd",
        grid=(nb,),
        in_specs=_gla_specs(TB) + [_full_spec(v) for v in vecs],
        out_specs=[pl.BlockSpec((TB, GLA_V), lambda t: (t, 0)), pl.BlockSpec((TB, GLA_V), lambda t: (t, 0)),
                   pl.BlockSpec((cpb, GLA_V, GLA_QK), lambda t: (t, 0, 0))],
        out_shape=[jax.ShapeDtypeStruct((S, GLA_V), BF16), jax.ShapeDtypeStruct((S, GLA_V), F32),
                   jax.ShapeDtypeStruct((S // CHUNK, GLA_V, GLA_QK), F32)],
        scratch_shapes=[pltpu.VMEM((GLA_V, GLA_QK), F32)],
        compiler_params=pltpu.CompilerParams(dimension_semantics=("arbitrary",), vmem_limit_bytes=VMEM_LIMIT_BYTES),
    )(u, u, u, u, u, *vecs)


def _gla_bwd(dyb, u, oraw, states, gw, S):
    TB = min(512, S)
    nb = S // TB
    cpb = TB // CHUNK
    vecs = [gw["wg2"], gw["bg"], gw["ng"], gw["bd"]]

    def rrow(width):
        return pl.BlockSpec((TB, width), lambda t: (nb - 1 - t, 0))

    def body(dyb_r, oraw_r, q_r, k_r, v_r, br_r, gl_r, st_r, sp_r, wg2, bg, ng, bd,
             dq_o, dk_o, dv_o, dbr_o, dgl_o, dwg2_o, dbg_o, dng_o, dcar, do_buf, dla_buf):
        t = pl.program_id(0)
        blk = nb - 1 - t

        @pl.when(t == 0)
        def _():
            dcar[...] = jnp.zeros_like(dcar)
            dwg2_o[...] = jnp.zeros_like(dwg2_o)
            dbg_o[...] = jnp.zeros_like(dbg_o)
            dng_o[...] = jnp.zeros_like(dng_o)

        pre, la, gc = _gla_gates(gl_r[...], wg2[...], bg[...], TB)
        for h in range(GLA_HEADS):
            hs = slice(h * GLA_DV, (h + 1) * GLA_DV)
            oh = oraw_r[:, hs]
            rs = lax.rsqrt(jnp.mean(oh * oh, axis=-1, keepdims=True) + RMS_EPS)
            on = oh * rs
            sil, dsil = _silu_and_grad(br_r[:, hs])
            dy = dyb_r[:, hs]
            dbr_o[:, hs] = (dy * on * ng[:, hs] * dsil).astype(dbr_o.dtype)
            don = dy * ng[:, hs] * sil
            dng_o[:, hs] += _colsum(dy * on * sil)
            do_buf[:, hs] = rs * (don - on * jnp.mean(don * on, axis=-1, keepdims=True))
        first = jnp.where(blk == 0, 0.0, 1.0)
        for c in reversed(range(cpb)):
            sl = slice(c * CHUNK, (c + 1) * CHUNK)
            s_n = st_r[c]
            s_prev = st_r[c - 1] if c > 0 else sp_r[0] * first
            gt = _colsum(la[sl])
            w = jnp.exp(gt - gc[sl])
            k_c = k_r[sl, :]
            kdec = k_c * w
            qs = q_r[sl, :] * GLA_SCALE
            do_c = do_buf[sl, :]
            dq_o[sl, :] = (_dot(do_c, s_n, "nn") * GLA_SCALE).astype(dq_o.dtype)
            d_n = _dot(do_c, qs, "tn") * bd[...] + dcar[...]
            dv_o[sl, :] = _dot(kdec, d_n, "nt").astype(dv_o.dtype)
            dkdec = _dot(v_r[sl, :], d_n, "nn")
            dk_o[sl, :] = (dkdec * w).astype(dk_o.dtype)
            tt = dkdec * kdec
            e = jnp.exp(gt)
            dgt = _colsum(tt) + _colsum(d_n * s_prev) * e
            _, rc = _scan_rows(None, -tt, CHUNK, reverse=True)
            dla_buf[sl, :] = rc + dgt
            dcar[...] = d_n * e
        dpre = dla_buf[...] * _sigmoid(-pre) * (1.0 / GLA_TAU)
        dbg_o[...] += _colsum(dpre)
        dgl_o[...] = _dot(dpre, wg2[...], "nt").astype(dgl_o.dtype)
        dwg2_o[...] += _dot(gl_r[...], dpre, "tn")

    return pl.pallas_call(
        body,
        name="gla_bwd",
        grid=(nb,),
        in_specs=[rrow(GLA_V), rrow(GLA_V)] + _gla_specs(TB, rev_nb=nb)
        + [pl.BlockSpec((cpb, GLA_V, GLA_QK), lambda t: (nb - 1 - t, 0, 0)),
           pl.BlockSpec((1, GLA_V, GLA_QK), lambda t: (jnp.maximum((nb - 1 - t) * cpb - 1, 0), 0, 0))]
        + [_full_spec(v) for v in vecs],
        out_specs=[rrow(GLA_QK), rrow(GLA_QK), rrow(GLA_V), rrow(GLA_V), rrow(LANES),
                   pl.BlockSpec((LANES, GLA_QK), lambda t: (0, 0)), pl.BlockSpec((1, GLA_QK), lambda t: (0, 0)),
                   pl.BlockSpec((1, GLA_V), lambda t: (0, 0))],
        out_shape=[jax.ShapeDtypeStruct((S, GLA_QK), BF16), jax.ShapeDtypeStruct((S, GLA_QK), BF16),
                   jax.ShapeDtypeStruct((S, GLA_V), BF16), jax.ShapeDtypeStruct((S, GLA_V), BF16),
                   jax.ShapeDtypeStruct((S, LANES), BF16), jax.ShapeDtypeStruct((LANES, GLA_QK), F32),
                   jax.ShapeDtypeStruct((1, GLA_QK), F32), jax.ShapeDtypeStruct((1, GLA_V), F32)],
        scratch_shapes=[pltpu.VMEM((GLA_V, GLA_QK), F32), pltpu.VMEM((TB, GLA_V), F32),
                        pltpu.VMEM((TB, GLA_QK), F32)],
        compiler_params=pltpu.CompilerParams(dimension_semantics=("arbitrary",), vmem_limit_bytes=VMEM_LIMIT_BYTES),
    )(dyb, oraw, u, u, u, u, u, states, states, *vecs)


FOX_SCALE = FOX_DH ** -0.5


def _fox_gate_fwd(u, bfp, S):
    T = min(512, S)

    def body(f_r, b_r, fc_o, car):
        t = pl.program_id(0)

        @pl.when(t == 0)
        def _():
            car[...] = jnp.zeros_like(car)

        _, cs = _scan_rows(None, _log_sigmoid(f_r[...] + b_r[...]), T)
        fc_o[...] = cs + car[...]
        car[...] = fc_o[pl.ds(T - 1, 1), :]

    return pl.pallas_call(
        body,
        name="fox_gate_fwd",
        grid=(S // T,),
        in_specs=[pl.BlockSpec((T, LANES), lambda t: (t, U_CF // LANES)), _full_spec(bfp)],
        out_specs=pl.BlockSpec((T, LANES), lambda t: (t, 0)),
        out_shape=jax.ShapeDtypeStruct((S, LANES), F32),
        scratch_shapes=[pltpu.VMEM((1, LANES), F32)],
        compiler_params=pltpu.CompilerParams(dimension_semantics=("arbitrary",), vmem_limit_bytes=VMEM_LIMIT_BYTES),
    )(u, bfp)


def _fox_gate_bwd(dfc, u, bfp, S):
    T = min(512, S)
    nb = S // T

    def body(d_r, f_r, b_r, df_o, db_o, car, tmp):
        t = pl.program_id(0)

        @pl.when(t == 0)
        def _():
            car[...] = jnp.zeros_like(car)
            db_o[...] = jnp.zeros_like(db_o)

        _, rc = _scan_rows(None, d_r[...], T, reverse=True)
        tmp[...] = rc + car[...]
        car[...] = tmp[pl.ds(0, 1), :]
        df = tmp[...] * _sigmoid(-(f_r[...] + b_r[...]))
        df_o[...] = df.astype(df_o.dtype)
        db_o[...] += _colsum(df)

    return pl.pallas_call(
        body,
        name="fox_gate_bwd",
        grid=(nb,),
        in_specs=[pl.BlockSpec((T, LANES), lambda t: (nb - 1 - t, 0)),
                  pl.BlockSpec((T, LANES), lambda t: (nb - 1 - t, U_CF // LANES)), _full_spec(bfp)],
        out_specs=[pl.BlockSpec((T, LANES), lambda t: (nb - 1 - t, 0)), pl.BlockSpec((1, LANES), lambda t: (0, 0))],
        out_shape=[jax.ShapeDtypeStruct((S, LANES), BF16), jax.ShapeDtypeStruct((1, LANES), F32)],
        scratch_shapes=[pltpu.VMEM((1, LANES), F32), pltpu.VMEM((T, LANES), F32)],
        compiler_params=pltpu.CompilerParams(dimension_semantics=("arbitrary",), vmem_limit_bytes=VMEM_LIMIT_BYTES),
    )(dfc, u, bfp)


def _fox_scores(q, k, fq, fk, qi, ki, tq, tk):
    s = _dot(q, k, "nt") * FOX_SCALE + (fq - fk)
    rows = lax.broadcasted_iota(jnp.int32, (tq, tk), 0) + qi * tq
    cols = lax.broadcasted_iota(jnp.int32, (tq, tk), 1) + ki * tk
    return jnp.where(cols <= rows, s, NEG_BIG)


def _fox_fwd(qh, kh, vh, fq, fk, S):
    tq = tk = min(512, S)
    nq, nk = S // tq, S // tk

    def body(q_r, k_r, v_r, fq_r, fk_r, o_o, lse_o, m_s, l_s, acc):
        qi, ki = pl.program_id(1), pl.program_id(2)

        @pl.when(ki == 0)
        def _():
            m_s[...] = jnp.full_like(m_s, NEG_BIG)
            l_s[...] = jnp.zeros_like(l_s)
            acc[...] = jnp.zeros_like(acc)

        @pl.when(ki <= qi)
        def _():
            s = _fox_scores(q_r[0], k_r[0], fq_r[0], fk_r[0], qi, ki, tq, tk)
            m_new = jnp.maximum(m_s[...], jnp.max(s, axis=-1, keepdims=True))
            p = jnp.exp(s - m_new)
            alpha = jnp.exp(m_s[...] - m_new)
            l_s[...] = alpha * l_s[...] + jnp.sum(p, axis=-1, keepdims=True)
            acc[...] = alpha * acc[...] + _dot(p, v_r[0], "nn")
            m_s[...] = m_new

        @pl.when(ki == nk - 1)
        def _():
            o_o[0] = acc[...] / l_s[...]
            lse_o[0] = m_s[...] + jnp.log(l_s[...])

    kv = pl.BlockSpec((1, tk, FOX_DH), lambda h, i, j: (h, jnp.minimum(j, i), 0))
    return pl.pallas_call(
        body,
        name="fox_fwd",
        grid=(FOX_HEADS, nq, nk),
        in_specs=[pl.BlockSpec((1, tq, FOX_DH), lambda h, i, j: (h, i, 0)), kv, kv,
                  pl.BlockSpec((1, tq, 1), lambda h, i, j: (h, i, 0)),
                  pl.BlockSpec((1, 1, tk), lambda h, i, j: (h, 0, jnp.minimum(j, i)))],
        out_specs=[pl.BlockSpec((1, tq, FOX_DH), lambda h, i, j: (h, i, 0)),
                   pl.BlockSpec((1, tq, 1), lambda h, i, j: (h, i, 0))],
        out_shape=[jax.ShapeDtypeStruct((FOX_HEADS, S, FOX_DH), F32), jax.ShapeDtypeStruct((FOX_HEADS, S, 1), F32)],
        scratch_shapes=[pltpu.VMEM((tq, 1), F32), pltpu.VMEM((tq, 1), F32), pltpu.VMEM((tq, FOX_DH), F32)],
        compiler_params=pltpu.CompilerParams(
            dimension_semantics=("parallel", "parallel", "arbitrary"), vmem_limit_bytes=VMEM_LIMIT_BYTES),
    )(qh, kh, vh, fq, fk)


def _fox_bwd_dq(qh, kh, vh, fq, fk, o, do, lse, S):
    tq = tk = min(512, S)
    nq, nk = S // tq, S // tk

    def body(q_r, k_r, v_r, fq_r, fk_r, o_r, do_r, lse_r, dq_o, dfq_o, dq_acc, df_acc):
        qi, ki = pl.program_id(1), pl.program_id(2)

        @pl.when(ki == 0)
        def _():
            dq_acc[...] = jnp.zeros_like(dq_acc)
            df_acc[...] = jnp.zeros_like(df_acc)

        @pl.when(ki <= qi)
        def _():
            s = _fox_scores(q_r[0], k_r[0], fq_r[0], fk_r[0], qi, ki, tq, tk)
            p = jnp.exp(s - lse_r[0])
            do_t = do_r[0]
            delta = jnp.sum(o_r[0] * do_t, axis=-1, keepdims=True)
            ds = p * (_dot(do_t, v_r[0], "nt") - delta)
            dq_acc[...] += _dot(ds, k_r[0], "nn")
            df_acc[...] += jnp.sum(ds, axis=-1, keepdims=True)

        @pl.when(ki == nk - 1)
        def _():
            dq_o[0] = dq_acc[...] * FOX_SCALE
            dfq_o[0] = df_acc[...]

    qrow = pl.BlockSpec((1, tq, FOX_DH), lambda h, i, j: (h, i, 0))
    qcol = pl.BlockSpec((1, tq, 1), lambda h, i, j: (h, i, 0))
    kv = pl.BlockSpec((1, tk, FOX_DH), lambda h, i, j: (h, jnp.minimum(j, i), 0))
    return pl.pallas_call(
        body,
        name="fox_bwd_dq",
        grid=(FOX_HEADS, nq, nk),
        in_specs=[qrow, kv, kv, qcol, pl.BlockSpec((1, 1, tk), lambda h, i, j: (h, 0, jnp.minimum(j, i))),
                  qrow, qrow, qcol],
        out_specs=[qrow, qcol],
        out_shape=[jax.ShapeDtypeStruct((FOX_HEADS, S, FOX_DH), F32), jax.ShapeDtypeStruct((FOX_HEADS, S, 1), F32)],
        scratch_shapes=[pltpu.VMEM((tq, FOX_DH), F32), pltpu.VMEM((tq, 1), F32)],
        compiler_params=pltpu.CompilerParams(
            dimension_semantics=("parallel", "parallel", "arbitrary"), vmem_limit_bytes=VMEM_LIMIT_BYTES),
    )(qh, kh, vh, fq, fk, o, do, lse)


def _fox_bwd_dkv(qh, kh, vh, fq, fk, o, do, lse, S):
    tq = tk = min(512, S)
    nq, nk = S // tq, S // tk

    def body(q_r, k_r, v_r, fq_r, fk_r, o_r, do_r, lse_r, dk_o, dv_o, dfk_o, dk_acc, dv_acc, df_acc):
        ki, qi = pl.program_id(1), pl.program_id(2)

        @pl.when(qi == 0)
        def _():
            dk_acc[...] = jnp.zeros_like(dk_acc)
            dv_acc[...] = jnp.zeros_like(dv_acc)
            df_acc[...] = jnp.zeros_like(df_acc)

        @pl.when(qi >= ki)
        def _():
            s = _fox_scores(q_r[0], k_r[0], fq_r[0], fk_r[0], qi, ki, tq, tk)
            p = jnp.exp(s - lse_r[0])
            do_t = do_r[0]
            delta = jnp.sum(o_r[0] * do_t, axis=-1, keepdims=True)
            ds = p * (_dot(do_t, v_r[0], "nt") - delta)
            dv_acc[...] += _dot(p, do_t, "tn")
            dk_acc[...] += _dot(ds, q_r[0], "tn")
            df_acc[...] += _colsum(ds)

        @pl.when(qi == nq - 1)
        def _():
            dk_o[0] = dk_acc[...] * FOX_SCALE
            dv_o[0] = dv_acc[...]
            dfk_o[0] = df_acc[...]

    qrow = pl.BlockSpec((1, tq, FOX_DH), lambda h, j, i: (h, jnp.maximum(i, j), 0))
    qcol = pl.BlockSpec((1, tq, 1), lambda h, j, i: (h, jnp.maximum(i, j), 0))
    kv = pl.BlockSpec((1, tk, FOX_DH), lambda h, j, i: (h, j, 0))
    krow = pl.BlockSpec((1, 1, tk), lambda h, j, i: (h, 0, j))
    return pl.pallas_call(
        body,
        name="fox_bwd_dkv",
        grid=(FOX_HEADS, nk, nq),
        in_specs=[qrow, kv, kv, qcol, krow, qrow, qrow, qcol],
        out_specs=[kv, kv, krow],
        out_shape=[jax.ShapeDtypeStruct((FOX_HEADS, S, FOX_DH), F32), jax.ShapeDtypeStruct((FOX_HEADS, S, FOX_DH), F32),
                   jax.ShapeDtypeStruct((FOX_HEADS, 1, S), F32)],
        scratch_shapes=[pltpu.VMEM((tk, FOX_DH), F32), pltpu.VMEM((tk, FOX_DH), F32), pltpu.VMEM((1, tk), F32)],
        compiler_params=pltpu.CompilerParams(
            dimension_semantics=("parallel", "parallel", "arbitrary"), vmem_limit_bytes=VMEM_LIMIT_BYTES),
    )(qh, kh, vh, fq, fk, o, do, lse)


def _to_heads(x2d, S):
    return jnp.transpose(x2d.reshape(S, FOX_HEADS, FOX_DH), (1, 0, 2))


def _from_heads(xh, S):
    return jnp.transpose(xh, (1, 0, 2)).reshape(S, FOX_HEADS * FOX_DH)


def _ffn_fwd(tag, x, wg, wu, wd, g, b, S):
    def up_epi(accs):
        gate, up = accs
        sil, _ = _silu_and_grad(gate)
        return [gate, up, sil * up]

    gate, up, act = _mm(tag + "_up", "nn", [x], [wg, wu], [(0, 0, 0), (1, 0, 1)], 2, up_epi, [],
                        [F32, F32, BF16], S, D_FF, D_MODEL, tn=1408)

    def down_epi(accs, xr, gg, bb):
        z = ALPHA * xr + 0.5 * accs[0]
        return [z, _ln_fwd(z, gg, bb)]

    z, xn = _mm(tag + "_down", "nn", [act], [wd], [(0, 0, 0)], 1, down_epi, [(x, "mn", 0), (g, "n"), (b, "n")],
                [F32, F32], S, D_MODEL, D_FF, tk=1408)
    return xn, dict(x=x, gate=gate, up=up, act=act, z=z)


def _ln_bwd_call(tag, dy, z, g, S):
    def fn(dy_t, z_t, g_t):
        dz, xhat = _ln_bwd(dy_t, z_t, g_t)
        return [dz], [_colsum(dy_t * xhat), _colsum(dy_t)]

    (dz,), (dg, db) = _rowwise(tag + "_ln_bwd", fn, [dy, z], [g], [(D_MODEL, F32)], [D_MODEL, D_MODEL], S)
    return dz, dg, db


def _ffn_bwd(tag, dxn, sv, wg, wu, wd, g, S):
    dz, dg, db = _ln_bwd_call(tag, dxn, sv["z"], g, S)

    def act_epi(accs, gate, up):
        da = 0.5 * accs[0]
        sil, dsil = _silu_and_grad(gate)
        return [da * up * dsil, da * sil]

    dgate, dup = _mm(tag + "_dact", "nt", [dz], [wd], [(0, 0, 0)], 1, act_epi,
                     [(sv["gate"], "mn", 0), (sv["up"], "mn", 0)], [BF16, BF16], S, D_FF, D_MODEL, tn=1408)
    dwd = _mm1(tag + "_dwd", "tn", sv["act"], dz, D_FF, D_MODEL, S, scale=0.5, tm=1408)

    def two(accs):
        return [accs[0], accs[1]]

    dwg, dwu = _mm(tag + "_dwup", "tn", [sv["x"]], [dgate, dup], [(0, 0, 0), (1, 0, 1)], 2, two, [], [F32, F32],
                   D_MODEL, D_FF, S, tn=1408)

    def dx_epi(accs, dzr):
        return [accs[0] + ALPHA * dzr]

    (dx,) = _mm(tag + "_dx", "nt", [dgate, dup], [wg, wu], [(0, 0, 0), (0, 1, 1)], 1, dx_epi, [(dz, "mn", 0)],
                [F32], S, D_MODEL, D_FF, tk=1408)
    return dx, dict(w_up=jnp.concatenate([dwg, dwu], axis=1), w_down=dwd, ln_g=dg, ln_b=db)


def _mixer_fwd(x1, w, S):
    u = _mm1("w_in", "nn", x1, w["w_in_p"], S, U_WIDTH, D_MODEL, tn=1536)
    ax0 = u[:, U_AX:U_AX + BRANCH]
    ax_shift = [ax0] + [_shift_down(ax0, k) for k in (1, 2, 3)]
    xc, r, gi, a, h, y_a = _lru_fwd(u, ax_shift, w["lru"], S)
    y_b, oraw, states = _gla_fwd(u, w["gla"], S)
    fcum = _fox_gate_fwd(u, w["bfp"], S)
    qh = _to_heads(u[:, U_CQ:U_CQ + BRANCH].astype(BF16), S)
    kh = _to_heads(u[:, U_CK:U_CK + BRANCH].astype(BF16), S)
    vh = _to_heads(u[:, U_CV:U_CV + BRANCH].astype(BF16), S)
    fct = jnp.transpose(fcum[:, :FOX_HEADS])
    fq = fct[:, :, None]
    fk = fct[:, None, :]
    oh, lse = _fox_fwd(qh, kh, vh, fq, fk, S)
    y_c = _from_heads(oh, S).astype(BF16)

    def merge_epi(accs, g0, g1, g2):
        merged = _sigmoid(g0) * accs[0] + _sigmoid(g1) * accs[1] + _sigmoid(g2) * accs[2]
        return [accs[0], accs[1], accs[2], merged]

    wb = w["w_branch"]
    yp0, yp1, yp2, merged = _mm(
        "merge", "nn", [y_a, y_b, y_c], [wb[0], wb[1], wb[2]], [(0, 0, 0), (1, 1, 1), (2, 2, 2)], 3, merge_epi,
        [(u, "mn", 0), (u, "mn", 1), (u, "mn", 2)], [F32, F32, F32, BF16], S, D_MODEL, BRANCH, tm=256)

    def out_epi(accs, xr, gg, bb):
        z = ALPHA * xr + accs[0]
        return [z, _ln_fwd(z, gg, bb)]

    z2, x2 = _mm("w_out", "nn", [merged], [w["w_out"]], [(0, 0, 0)], 1, out_epi,
                 [(x1, "mn", 0), (w["ln2_g"], "n"), (w["ln2_b"], "n")], [F32, F32], S, D_MODEL, D_MODEL)
    sv = dict(x=x1, u=u, ax_shift=ax_shift, xc=xc, r=r, i=gi, a=a, h=h, y_a=y_a, y_b=y_b, y_c=y_c, oraw=oraw,
              states=states, qh=qh, kh=kh, vh=vh, fq=fq, fk=fk, oh=oh, lse=lse, yp=(yp0, yp1, yp2), merged=merged,
              z=z2)
    return x2, sv


def _mixer_bwd(dx2, sv, w, S):
    u = sv["u"]
    dz, dg2, db2 = _ln_bwd_call("mix", dx2, sv["z"], w["ln2_g"], S)

    def dm_epi(accs, y0, y1, y2, g0, g1, g2):
        dm = accs[0]
        outs_p, outs_g = [], []
        for yp, gl in ((y0, g0), (y1, g1), (y2, g2)):
            sg = _sigmoid(gl)
            outs_p.append(dm * sg)
            outs_g.append(dm * yp * sg * (1.0 - sg))
        return outs_p + outs_g

    yp = sv["yp"]
    dyp0, dyp1, dyp2, dgl0, dgl1, dgl2 = _mm(
        "dmerged", "nt", [dz], [w["w_out"]], [(0, 0, 0)], 1, dm_epi,
        [(yp[0], "mn", 0), (yp[1], "mn", 0), (yp[2], "mn", 0), (u, "mn", 0), (u, "mn", 1), (u, "mn", 2)],
        [BF16] * 6, S, D_MODEL, D_MODEL, tm=256)
    dw_out = _mm1("dw_out", "tn", sv["merged"], dz, D_MODEL, D_MODEL, S)
    wb = w["w_branch"]
    dys, dwbs = [], []
    for j, (yj, dyp) in enumerate(((sv["y_a"], dyp0), (sv["y_b"], dyp1), (sv["y_c"], dyp2))):
        dys.append(_mm1("dy_branch%d" % j, "nt", dyp, wb[j], S, BRANCH, D_MODEL))
        dwbs.append(_mm1("dw_branch%d" % j, "tn", yj, dyp, BRANCH, D_MODEL, S))
    day, dxc, dwa, dwx, dba, dbx, dlam = _lru_bwd(dys[0], u, sv, w["lru"], S)
    dax, (dcw0, dcw1, dcw2, dcw3, dcb) = _conv_bwd(dxc, sv["ax_shift"], w["lru"], S)
    dbq, dbk, dbv, dbr, dglow, dwg2p, dbg, dng = _gla_bwd(dys[1], u, sv["oraw"], sv["states"], w["gla"], S)
    doh = _to_heads(dys[2], S)
    args = (sv["qh"], sv["kh"], sv["vh"], sv["fq"], sv["fk"], sv["oh"], doh, sv["lse"], S)
    dqh, dfq = _fox_bwd_dq(*args)
    dkh, dvh, dfk = _fox_bwd_dkv(*args)
    dfc = jnp.transpose(dfq[:, :, 0] - dfk[:, 0, :])
    dfc = jnp.pad(dfc, ((0, 0), (0, LANES - FOX_HEADS)))
    dcf, dbf = _fox_gate_bwd(dfc, u, w["bfp"], S)
    du = jnp.concatenate(
        [dgl0, dgl1, dgl2, dax, day, dbq, dbk, dbv, dbr, _from_heads(dqh, S).astype(BF16),
         _from_heads(dkh, S).astype(BF16), _from_heads(dvh, S).astype(BF16), dglow, dcf,
         jnp.zeros((S, U_WIDTH - U_CF - LANES), BF16)], axis=1)
    dw_in_p = _mm1("dw_in", "tn", sv["x"], du, D_MODEL, U_WIDTH, S, tn=1536)

    def dx_epi(accs, dzr):
        return [accs[0] + ALPHA * dzr]

    (dx1,) = _mm("dx_mix", "nt", [du], [w["w_in_p"]], [(0, 0, 0)], 1, dx_epi, [(dz, "mn", 0)], [F32], S, D_MODEL,
                 U_WIDTH, tk=1536)
    pieces = sorted(W_IN_SEGMENTS)
    dw_in = jnp.concatenate([dw_in_p[:, dst:dst + width] for _, width, dst in pieces], axis=1)
    eye = jnp.eye(LRU_BLOCKS, dtype=F32)
    dwa_b = jnp.einsum("ncmd,nm->ncd", dwa.reshape(LRU_BLOCKS, 64, LRU_BLOCKS, 64), eye)
    dwx_b = jnp.einsum("ncmd,nm->ncd", dwx.reshape(LRU_BLOCKS, 64, LRU_BLOCKS, 64), eye)
    grads = dict(
        w_in=dw_in, w_out=dw_out, w_branch=jnp.stack(dwbs), ln2_g=dg2, ln2_b=db2,
        conv_w=jnp.concatenate([dcw0, dcw1, dcw2, dcw3], axis=0), conv_b=dcb, lru_wa=dwa_b, lru_wx=dwx_b,
        lru_ba=dba, lru_bx=dbx, lru_lambda=dlam, gla_w_g2=dwg2p[:GLA_LOWRANK], gla_b_g=dbg, gla_norm_g=dng,
        fox_b_f=dbf[:, :FOX_HEADS])
    return dx1, grads


def _ple_fwd(x3, p_i, w, S):
    pe = _mm1("ple_proj", "nn", p_i, w["ple_w_proj"], S, D_MODEL, PLE_DIM)

    def epi(accs, xr, per, bg, gg, bb):
        sg = _sigmoid(accs[0] + bg)
        z = ALPHA * xr + sg * per
        return [sg, z, _ln_fwd(z, gg, bb)]

    sg, z4, x4 = _mm("ple_gate", "nn", [x3], [w["ple_w_gate"]], [(0, 0, 0)], 1, epi,
                     [(x3, "mn", 0), (pe, "mn", 0), (w["ple_b_gate"], "n"), (w["ln4_g"], "n"), (w["ln4_b"], "n")],
                     [F32, F32, F32], S, D_MODEL, D_MODEL)
    return x4, dict(x=x3, p=p_i, pe=pe, sg=sg, z=z4)


def _ple_bwd(dx4, sv, w, S):
    def fn(dy_t, z_t, pe_t, sg_t, g_t):
        dz, xhat = _ln_bwd(dy_t, z_t, g_t)
        dgl = dz * pe_t * sg_t * (1.0 - sg_t)
        return [dz, dz * sg_t, dgl], [_colsum(dy_t * xhat), _colsum(dy_t), _colsum(dgl)]

    (dz, dpe, dgl), (dg4, db4, dbg) = _rowwise(
        "ple_bwd", fn, [dx4, sv["z"], sv["pe"], sv["sg"]], [w["ln4_g"]],
        [(D_MODEL, F32), (D_MODEL, BF16), (D_MODEL, BF16)], [D_MODEL] * 3, S)
    dwp = _mm1("dw_ple_proj", "tn", sv["p"], dpe, PLE_DIM, D_MODEL, S)
    dwg = _mm1("dw_ple_gate", "tn", sv["x"], dgl, D_MODEL, D_MODEL, S)

    def dx_epi(accs, dzr):
        return [accs[0] + ALPHA * dzr]

    (dx3,) = _mm("dx_ple", "nt", [dgl], [w["ple_w_gate"]], [(0, 0, 0)], 1, dx_epi, [(dz, "mn", 0)], [F32], S,
                 D_MODEL, D_MODEL)
    return dx3, dict(ple_w_proj=dwp, ple_w_gate=dwg, ple_b_gate=dbg, ln4_g=dg4, ln4_b=db4)


def _layer_weights(full, i):
    w = {}
    up = full["ffn1_w_up"][i]
    w["ffn1"] = (up[:, :D_FF], up[:, D_FF:], full["ffn1_w_down"][i])
    up = full["ffn2_w_up"][i]
    w["ffn2"] = (up[:, :D_FF], up[:, D_FF:], full["ffn2_w_down"][i])
    w_in = full["w_in"][i]
    placed = sorted((dst, src, width) for src, width, dst in W_IN_SEGMENTS)
    parts, pos = [], 0
    for dst, src, width in placed:
        if dst > pos:
            parts.append(jnp.zeros((D_MODEL, dst - pos), w_in.dtype))
        parts.append(w_in[:, src:src + width])
        pos = dst + width
    parts.append(jnp.zeros((D_MODEL, U_WIDTH - pos), w_in.dtype))
    w["w_in_p"] = jnp.concatenate(parts, axis=1)
    eye = jnp.eye(LRU_BLOCKS, dtype=F32)

    def dense(blocks):
        return jnp.einsum("ncd,nm->ncmd", blocks, eye).reshape(BRANCH, BRANCH).astype(BF16)

    def vec(name):
        return full[name][i].reshape(1, -1)

    cw = full["conv_w"][i]
    w["lru"] = dict(cw0=cw[0:1], cw1=cw[1:2], cw2=cw[2:3], cw3=cw[3:4], conv_b=vec("conv_b"),
                    wa=dense(full["lru_wa"][i]), wx=dense(full["lru_wx"][i]), ba=vec("lru_ba"), bx=vec("lru_bx"),
                    lam=vec("lru_lambda"))
    hq = jnp.arange(GLA_QK) // GLA_DK
    hv = jnp.arange(GLA_V) // GLA_DV
    w["gla"] = dict(wg2=jnp.pad(full["gla_w_g2"][i], ((0, LANES - GLA_LOWRANK), (0, 0))).astype(BF16),
                    bg=vec("gla_b_g"), ng=vec("gla_norm_g"), bd=(hv[:, None] == hq[None, :]).astype(F32))
    w["bfp"] = jnp.pad(vec("fox_b_f"), ((0, 0), (0, LANES - FOX_HEADS)))
    w["w_branch"] = full["w_branch"][i]
    w["w_out"] = full["w_out"][i]
    w["ple_w_proj"] = full["ple_w_proj"][i]
    w["ple_w_gate"] = full["ple_w_gate"][i]
    for name in ("ln1_g", "ln1_b", "ln2_g", "ln2_b", "ln3_g", "ln3_b", "ln4_g", "ln4_b", "ple_b_gate"):
        w[name] = vec(name)
    return w


def _layer_fwd(x0, p_i, w, S):
    x1, s1 = _ffn_fwd("ffn1", x0, *w["ffn1"], w["ln1_g"], w["ln1_b"], S)
    x2, s2 = _mixer_fwd(x1, w, S)
    x3, s3 = _ffn_fwd("ffn2", x2, *w["ffn2"], w["ln3_g"], w["ln3_b"], S)
    x4, s4 = _ple_fwd(x3, p_i, w, S)
    return x4, (s1, s2, s3, s4)


def _layer_bwd(dx4, saved, w, S):
    s1, s2, s3, s4 = saved
    dx3, g4 = _ple_bwd(dx4, s4, w, S)
    dx2, g3 = _ffn_bwd("ffn2", dx3, s3, *w["ffn2"], w["ln3_g"], S)
    dx1, g2 = _mixer_bwd(dx2, s2, w, S)
    dx0, g1 = _ffn_bwd("ffn1", dx1, s1, *w["ffn1"], w["ln1_g"], S)
    grads = dict(g2)
    grads.update(g4)
    grads.update(ffn1_w_up=g1["w_up"], ffn1_w_down=g1["w_down"], ln1_g=g1["ln_g"], ln1_b=g1["ln_b"],
                 ffn2_w_up=g3["w_up"], ffn2_w_down=g3["w_down"], ln3_g=g3["ln_g"], ln3_b=g3["ln_b"])
    return dx0, grads


def _local_step(x, p, target, full):
    S = x.shape[0]
    ws = [_layer_weights(full, i) for i in range(DEPTH)]
    saved = []
    h = x
    for i in range(DEPTH):
        h, sv = _layer_fwd(h, p[i], ws[i], S)
        saved.append(sv)

    def loss_fn(y, t):
        err = y - t
        return [err * (1.0 / D_MODEL)], [_colsum(err * err) * (0.5 / D_MODEL)]

    (dy,), (lsum,) = _rowwise("loss", loss_fn, [h, target], [], [(D_MODEL, F32)], [D_MODEL], S)
    loss = jnp.sum(lsum)
    layer_grads = [None] * DEPTH
    for i in reversed(range(DEPTH)):
        dy, layer_grads[i] = _layer_bwd(dy, saved[i], ws[i], S)
    grads = {}
    for name in WEIGHTS:
        grads[name] = jnp.stack([layer_grads[i][name] for i in range(DEPTH)])
    return loss, dy, grads


def kernel(x, p, ffn1_w_up, ffn1_w_down, ln1_g, ln1_b, w_in, conv_w, conv_b, lru_wa, lru_ba, lru_wx, lru_bx, lru_lambda, gla_w_g2, gla_b_g, gla_norm_g, fox_b_f, w_branch, w_out, ln2_g, ln2_b, ffn2_w_up, ffn2_w_down, ln3_g, ln3_b, ple_w_proj, ple_w_gate, ple_b_gate, ln4_g, ln4_b, loss_target, m_ffn1_w_up, m_ffn1_w_down, m_ln1_g, m_ln1_b, m_w_in, m_conv_w, m_conv_b, m_lru_wa, m_lru_ba, m_lru_wx, m_lru_bx, m_lru_lambda, m_gla_w_g2, m_gla_b_g, m_gla_norm_g, m_fox_b_f, m_w_branch, m_w_out, m_ln2_g, m_ln2_b, m_ffn2_w_up, m_ffn2_w_down, m_ln3_g, m_ln3_b, m_ple_w_proj, m_ple_w_gate, m_ple_b_gate, m_ln4_g, m_ln4_b, v_ffn1_w_up, v_ffn1_w_down, v_ln1_g, v_ln1_b, v_w_in, v_conv_w, v_conv_b, v_lru_wa, v_lru_ba, v_lru_wx, v_lru_bx, v_lru_lambda, v_gla_w_g2, v_gla_b_g, v_gla_norm_g, v_fox_b_f, v_w_branch, v_w_out, v_ln2_g, v_ln2_b, v_ffn2_w_up, v_ffn2_w_down, v_ln3_g, v_ln3_b, v_ple_w_proj, v_ple_w_gate, v_ple_b_gate, v_ln4_g, v_ln4_b):
    env = dict(locals())
    wts = {n: env[n] for n in WEIGHTS}
    ms = {n: env["m_" + n] for n in WEIGHTS}
    vs = {n: env["v_" + n] for n in WEIGHTS}
    shard_axis = dict(SHARDED)
    sharded = [n for n, _ in SHARDED]
    big = [n for n in sharded if n not in SHARDED_F32_GATHER]
    small = list(SHARDED_F32_GATHER)

    full = {n: wts[n] for n in REPLICATED}
    for names, dtype, tag in ((big, BF16, "gather_w_bf16"), (small, F32, "gather_w_f32")):
        gathered = _exchange(tag, _pack([wts[n] for n in names], dtype), all_to_all=False)
        parts = _unpack(gathered, [wts[n].shape for n in names], lead=(N_DEV,))
        for n, g in zip(names, parts):
            full[n] = _from_gathered(g, shard_axis[n])

    loss_part, grad_x, grads = _local_step(x[0], p[:, 0], loss_target[0], full)
    loss = lax.psum(loss_part, MESH_AXES)

    dest = [_dest_major(grads[n], shard_axis[n]) for n in sharded]
    gs = jnp.stack([_pack([d[j] for d in dest], F32) for j in range(N_DEV)])
    gs = _exchange("grad_all_to_all", gs, all_to_all=True)
    rep = list(REPLICATED)
    gr = _exchange("grad_gather_replicated", _pack([grads[n].reshape(wts[n].shape) for n in rep], F32),
                   all_to_all=False)

    out = {}
    for names, gparts, tag in ((sharded, gs, "adamw_sharded"), (rep, gr, "adamw_replicated")):
        res = _adamw(tag, gparts, _pack([wts[n] for n in names], F32), _pack([ms[n] for n in names], F32),
                     _pack([vs[n] for n in names], F32))
        shapes = [wts[n].shape for n in names]
        for kind, buf in zip(("grad", "delta", "new_m", "new_v"), res):
            for n, arr in zip(names, _unpack(buf, shapes)):
                out[kind + "_" + n] = arr
    return (loss, grad_x[None], *[out["grad_" + n] for n in WEIGHTS], *[out["delta_" + n] for n in WEIGHTS],
            *[out["new_m_" + n] for n in WEIGHTS], *[out["new_v_" + n] for n in WEIGHTS])
```

```python
import functools
import math

import jax
import jax.numpy as jnp
from jax import lax
from jax.experimental import pallas as pl
from jax.experimental.pallas import tpu as pltpu

F32 = jnp.float32
BF16 = jnp.bfloat16

N_DEV = 8
MESH_AXES = ("x", "y", "c")
DEPTH = 2
D_MODEL = 1024
D_FF = 2816
BRANCH = 512
CHUNK = 64
GLA_HEADS = 4
GLA_DK = 64
GLA_DV = 128
GLA_LOWRANK = 16
GLA_TAU = 16.0
FOX_HEADS = 8
FOX_DH = 64
PLE_DIM = 256
LRU_C = 8.0
LRU_BLOCKS = 8
LN_EPS = 1e-5
RMS_EPS = 1e-6
ALPHA = (2 * DEPTH) ** 0.25
LANES = 128
NEG_BIG = -1e30

ADAM_LR = 0.001
ADAM_B1 = 0.9
ADAM_B2 = 0.999
ADAM_EPS = 1e-08
ADAM_WD = 0.01
ADAM_STEP = 10

VMEM_LIMIT_BYTES = 56 * 1024 * 1024

U_GATES = 0
U_AX = 3072
U_AY = 3584
U_BQ = 4096
U_BK = 4352
U_BV = 4608
U_BR = 5120
U_CQ = 5632
U_CK = 6144
U_CV = 6656
U_BLOW = 7168
U_CF = 7296
U_WIDTH = 7680
W_IN_SEGMENTS = (
    (0, 512, U_AX), (512, 512, U_AY), (1024, 256, U_BQ), (1280, 256, U_BK), (1536, 512, U_BV),
    (2048, 16, U_BLOW), (2064, 512, U_BR), (2576, 512, U_CQ), (3088, 512, U_CK), (3600, 512, U_CV),
    (4112, 8, U_CF), (4120, 3072, U_GATES),
)

SHARDED = (
    ("ffn1_w_up", 2), ("ffn1_w_down", 1), ("w_in", 2), ("conv_w", 2), ("gla_w_g2", 2), ("w_branch", 3),
    ("w_out", 1), ("ffn2_w_up", 2), ("ffn2_w_down", 1), ("ple_w_proj", 2), ("ple_w_gate", 1),
)
SHARDED_F32_GATHER = ("conv_w", "gla_w_g2")
REPLICATED = ("ln1_g", "ln1_b", "conv_b", "lru_wa", "lru_ba", "lru_wx", "lru_bx", "lru_lambda", "gla_b_g",
              "gla_norm_g", "fox_b_f", "ln2_g", "ln2_b", "ln3_g", "ln3_b", "ple_b_gate", "ln4_g", "ln4_b")
WEIGHTS = ("ffn1_w_up", "ffn1_w_down", "ln1_g", "ln1_b", "w_in", "conv_w", "conv_b", "lru_wa", "lru_ba", "lru_wx",
           "lru_bx", "lru_lambda", "gla_w_g2", "gla_b_g", "gla_norm_g", "fox_b_f", "w_branch", "w_out", "ln2_g",
           "ln2_b", "ffn2_w_up", "ffn2_w_down", "ln3_g", "ln3_b", "ple_w_proj", "ple_w_gate", "ple_b_gate", "ln4_g",
           "ln4_b")


def _sigmoid(x):
    return 1.0 / (1.0 + jnp.exp(-x))


def _log1p_pos(e):
    return jnp.where(e < 1e-4, e * (1.0 - 0.5 * e), jnp.log(1.0 + e))


def _softplus(x):
    return jnp.maximum(x, 0.0) + _log1p_pos(jnp.exp(-jnp.abs(x)))


def _log_sigmoid(x):
    return -_softplus(-x)


def _neg_expm1(y):
    series = -y * (1.0 + y * (0.5 + y * (1.0 / 6.0 + y * (1.0 / 24.0 + y * (1.0 / 120.0)))))
    return jnp.where(y > -0.1, series, 1.0 - jnp.exp(y))


def _silu_and_grad(x):
    s = _sigmoid(x)
    return x * s, s * (1.0 + x * (1.0 - s))


_GELU_C = math.sqrt(2.0 / math.pi)


def _gelu_and_grad(x):
    inner = _GELU_C * (x + 0.044715 * x * x * x)
    t = jnp.tanh(inner)
    g = 0.5 * x * (1.0 + t)
    dg = 0.5 * (1.0 + t) + 0.5 * x * (1.0 - t * t) * _GELU_C * (1.0 + 3.0 * 0.044715 * x * x)
    return g, dg


def _ln_stats(z):
    mu = jnp.mean(z, axis=-1, keepdims=True)
    zc = z - mu
    var = jnp.mean(zc * zc, axis=-1, keepdims=True)
    rstd = lax.rsqrt(var + LN_EPS)
    return zc * rstd, rstd


def _ln_fwd(z, g, b):
    xhat, _ = _ln_stats(z)
    return xhat * g + b


def _ln_bwd(dy, z, g):
    xhat, rstd = _ln_stats(z)
    dxh = dy * g
    m1 = jnp.mean(dxh, axis=-1, keepdims=True)
    m2 = jnp.mean(dxh * xhat, axis=-1, keepdims=True)
    return rstd * (dxh - m1 - xhat * m2), xhat


def _colsum(x):
    return jnp.sum(x, axis=0, keepdims=True)


def _dot(a, b, dims):
    dn = {"nn": (((1,), (0,)), ((), ())), "nt": (((1,), (1,)), ((), ())), "tn": (((0,), (0,)), ((), ()))}[dims]
    return lax.dot_general(a.astype(BF16), b.astype(BF16), dn, preferred_element_type=F32)


def _scan_rows(a, b, length, reverse=False, seg=None):
    rows = lax.broadcasted_iota(jnp.int32, b.shape, 0)
    span = seg if seg else length
    pos = rows % span if seg else rows
    d = 1
    while d < span:
        shift = (length - d) if reverse else d
        valid = (pos < span - d) if reverse else (pos >= d)
        sb = jnp.where(valid, pltpu.roll(b, shift, 0), 0.0)
        if a is None:
            b = b + sb
        else:
            b = b + a * sb
            a = a * jnp.where(valid, pltpu.roll(a, shift, 0), 1.0)
        d *= 2
    return a, b


def _tile(dim, pref):
    if dim <= pref:
        return dim
    best = None
    t = LANES
    while t <= pref:
        if dim % t == 0:
            best = t
        t += LANES
    assert best is not None, (dim, pref)
    return best


def _full_spec(arr):
    nd = arr.ndim
    return pl.BlockSpec(arr.shape, lambda *_: (0,) * nd)


def _mm(name, dims, a_ops, b_ops, terms, n_acc, epilogue, extras, out_dtypes, M, N, K, tm=512, tn=1024, tk=1024):
    tm, tn, tk = _tile(M, tm), _tile(N, tn), _tile(K, tk)
    gm, gn, gk = M // tm, N // tn, K // tk
    if dims == "tn":
        a_spec = pl.BlockSpec((tk, tm), lambda i, j, k: (k, i))
    else:
        a_spec = pl.BlockSpec((tm, tk), lambda i, j, k: (i, k))
    if dims == "nt":
        b_spec = pl.BlockSpec((tn, tk), lambda i, j, k: (j, k))
    else:
        b_spec = pl.BlockSpec((tk, tn), lambda i, j, k: (k, j))
    e_specs, e_arrays = [], []
    for ex in extras:
        if ex[1] == "mn":
            off = ex[2]
            e_specs.append(pl.BlockSpec((tm, tn), functools.partial(lambda i, j, k, off: (i, j + off), off=off)))
        else:
            e_specs.append(pl.BlockSpec((1, tn), lambda i, j, k: (0, j)))
        e_arrays.append(ex[0])
    na, nb, ne, no = len(a_ops), len(b_ops), len(extras), len(out_dtypes)

    def body(*refs):
        a_refs = refs[:na]
        b_refs = refs[na:na + nb]
        e_refs = refs[na + nb:na + nb + ne]
        o_refs = refs[na + nb + ne:na + nb + ne + no]
        acc_refs = refs[na + nb + ne + no:]
        k = pl.program_id(2)

        @pl.when(k == 0)
        def _():
            for acc in acc_refs:
                acc[...] = jnp.zeros_like(acc)

        for r, ai, bi in terms:
            acc_refs[r][...] += _dot(a_refs[ai][...], b_refs[bi][...], dims)

        @pl.when(k == gk - 1)
        def _():
            res = epilogue([acc[...] for acc in acc_refs], *[e[...] for e in e_refs])
            for o, val in zip(o_refs, res):
                o[...] = val.astype(o.dtype)

    outs = pl.pallas_call(
        body,
        name=name,
        grid=(gm, gn, gk),
        in_specs=[a_spec] * na + [b_spec] * nb + e_specs,
        out_specs=[pl.BlockSpec((tm, tn), lambda i, j, k: (i, j))] * no,
        out_shape=[jax.ShapeDtypeStruct((M, N), dt) for dt in out_dtypes],
        scratch_shapes=[pltpu.VMEM((tm, tn), F32)] * n_acc,
        compiler_params=pltpu.CompilerParams(
            dimension_semantics=("parallel", "parallel", "arbitrary"), vmem_limit_bytes=VMEM_LIMIT_BYTES),
    )(*a_ops, *b_ops, *e_arrays)
    return outs


def _mm1(name, dims, a, b, M, N, K, out_dtype=F32, scale=None, **kw):
    def epi(accs):
        return [accs[0] if scale is None else accs[0] * scale]
    return _mm(name, dims, [a], [b], [(0, 0, 0)], 1, epi, [], [out_dtype], M, N, K, **kw)[0]


def _rowwise(name, fn, row_ins, vec_ins, row_outs, sum_outs, S, tr=256, reverse=False):
    tr = min(tr, S)
    g = S // tr
    rmap = (lambda i: (g - 1 - i)) if reverse else (lambda i: i)
    in_specs, arrays = [], []
    for r in row_ins:
        if isinstance(r, tuple):
            arr, width, blk = r
            in_specs.append(pl.BlockSpec((tr, width), functools.partial(lambda i, blk: (rmap(i), blk), blk=blk)))
        else:
            arr = r
            in_specs.append(pl.BlockSpec((tr, arr.shape[1]), lambda i: (rmap(i), 0)))
        arrays.append(arr)
    for v in vec_ins:
        in_specs.append(_full_spec(v))
        arrays.append(v)
    nr, nv, no, ns = len(row_ins), len(vec_ins), len(row_outs), len(sum_outs)

    def body(*refs):
        ins = [r[...] for r in refs[:nr + nv]]
        o_refs = refs[nr + nv:nr + nv + no]
        s_refs = refs[nr + nv + no:]
        outs, sums = fn(*ins)
        for o, val in zip(o_refs, outs):
            o[...] = val.astype(o.dtype)
        if ns:
            i = pl.program_id(0)

            @pl.when(i == 0)
            def _():
                for s, val in zip(s_refs, sums):
                    s[...] = val

            @pl.when(i > 0)
            def _():
                for s, val in zip(s_refs, sums):
                    s[...] += val

    res = pl.pallas_call(
        body,
        name=name,
        grid=(g,),
        in_specs=in_specs,
        out_specs=[pl.BlockSpec((tr, c), lambda i: (rmap(i), 0)) for c, _ in row_outs]
        + [pl.BlockSpec((1, c), lambda i: (0, 0)) for c in sum_outs],
        out_shape=[jax.ShapeDtypeStruct((S, c), dt) for c, dt in row_outs]
        + [jax.ShapeDtypeStruct((1, c), F32) for c in sum_outs],
        compiler_params=pltpu.CompilerParams(
            dimension_semantics=("arbitrary",), vmem_limit_bytes=VMEM_LIMIT_BYTES),
    )(*arrays)
    return res[:no], res[no:]


def _win(arr, offset, width):
    assert offset % width == 0
    return (arr, width, offset // width)


MESH_ID = pl.DeviceIdType.MESH


def _remote(src, dst, send_sem, recv_sem, to):
    return pltpu.make_async_remote_copy(src_ref=src, dst_ref=dst, send_sem=send_sem, recv_sem=recv_sem,
                                        device_id=to, device_id_type=MESH_ID)


def _hbm_call(name, body, arrs, out_shapes, n_send, n_recv, n_local):
    return pl.pallas_call(
        body,
        name=name,
        in_specs=[pl.BlockSpec(memory_space=pltpu.HBM)] * len(arrs),
        out_specs=[pl.BlockSpec(memory_space=pltpu.HBM)] * len(out_shapes),
        out_shape=out_shapes,
        scratch_shapes=[pltpu.SemaphoreType.DMA((n_send,)), pltpu.SemaphoreType.DMA((n_recv,)),
                        pltpu.SemaphoreType.DMA((n_local,))],
        compiler_params=pltpu.CompilerParams(has_side_effects=True),
    )(*arrs)


def _allgather_multi(name, arrs):
    n = len(arrs)

    def body(*refs):
        ins, outs = refs[:n], refs[n:2 * n]
        send_sems, recv_sems, local_sems = refs[2 * n:]
        x, y, c = lax.axis_index("x"), lax.axis_index("y"), lax.axis_index("c")
        me, sibling = (x, y, c), (x, y, 1 - c)
        chips = [(1 - x, y), (x, 1 - y), (1 - x, 1 - y)]

        def slot(i, dev):
            return outs[i].at[4 * dev[0] + 2 * dev[1] + dev[2]]

        def copy(i, k, block, to, src=None):
            dst = slot(i, block)
            return _remote(dst if src is None else src, dst, send_sems.at[7 * i + k], recv_sems.at[7 * i + k], to)

        mine = [pltpu.make_async_copy(ins[i], slot(i, me), local_sems.at[i]) for i in range(n)]
        for cp in mine:
            cp.start()
        first = []
        for i in range(n):
            first.append(copy(i, 0, me, sibling, src=ins[i]))
            first += [copy(i, 1 + j, me, (*chip, c), src=ins[i]) for j, chip in enumerate(chips)]
        for cp in first:
            cp.start()
        passed = []
        for j, chip in enumerate(chips):
            for i in range(n):
                copy(i, 1 + j, (*chip, c), me).wait_recv()
                cp = copy(i, 4 + j, (*chip, c), sibling)
                cp.start()
                passed.append(cp)
        for i in range(n):
            copy(i, 0, sibling, me).wait_recv()
            for j, chip in enumerate(chips):
                copy(i, 4 + j, (*chip, 1 - c), me).wait_recv()
        for cp in first + passed:
            cp.wait_send()
        for cp in mine:
            cp.wait()

    outs = [jax.ShapeDtypeStruct((N_DEV,) + a.shape, a.dtype) for a in arrs]
    return _hbm_call(name, body, arrs, outs, 7 * n, 7 * n, n)


def _sibling_swap_multi(name, arrs):
    n = len(arrs)

    def body(*refs):
        ins, got, kept = refs[:n], refs[n:2 * n], refs[2 * n:3 * n]
        send_sems, recv_sems, local_sems = refs[3 * n:]
        x, y, c = lax.axis_index("x"), lax.axis_index("y"), lax.axis_index("c")
        sibling = (x, y, 1 - c)
        local = [pltpu.make_async_copy(ins[i].at[c], kept[i], local_sems.at[i]) for i in range(n)]
        sends = [_remote(ins[i].at[1 - c], got[i], send_sems.at[i], recv_sems.at[i], sibling) for i in range(n)]
        for cp in local + sends:
            cp.start()
        for cp in sends:
            cp.wait_recv()
        for cp in sends:
            cp.wait_send()
        for cp in local:
            cp.wait()

    outs = [jax.ShapeDtypeStruct(a.shape[1:], a.dtype) for a in arrs] * 2
    res = _hbm_call(name, body, arrs, outs, n, n, n)
    return res[:n], res[n:]


def _chip_all_to_all_multi(name, arrs):
    n = len(arrs)

    def body(*refs):
        ins, outs = refs[:n], refs[n:2 * n]
        send_sems, recv_sems, local_sems = refs[2 * n:]
        x, y, c = lax.axis_index("x"), lax.axis_index("y"), lax.axis_index("c")
        mine = 2 * x + y
        chips = [(1 - x, y), (x, 1 - y), (1 - x, 1 - y)]
        local = [pltpu.make_async_copy(ins[i].at[mine], outs[i].at[mine], local_sems.at[i]) for i in range(n)]
        for cp in local:
            cp.start()
        sends, recvs = [], []
        for i in range(n):
            for j, (px, py) in enumerate(chips):
                peer = 2 * px + py
                sems = (send_sems.at[3 * i + j], recv_sems.at[3 * i + j], (px, py, c))
                sends.append(_remote(ins[i].at[peer], outs[i].at[mine], *sems))
                recvs.append(_remote(ins[i].at[peer], outs[i].at[peer], *sems))
        for cp in sends:
            cp.start()
        for cp in recvs:
            cp.wait_recv()
        for cp in sends:
            cp.wait_send()
        for cp in local:
            cp.wait()

    outs = [jax.ShapeDtypeStruct(a.shape, a.dtype) for a in arrs]
    return _hbm_call(name, body, arrs, outs, 3 * n, 3 * n, n)


def _as_rows(a, lead):
    return a.reshape(a.shape[:lead] + (-1, a.shape[-1]))


def _row_tile(rows, cols, parts):
    budget = 4 * 1024 * 1024 // (4 * max(cols, LANES) * parts)
    return _tile_rows(rows, max(8, min(512, budget // 8 * 8)))


def _pair_add(name, a, b):
    _, rows, cols = a.shape
    tr = _row_tile(rows, cols, 2)

    def body(a_ref, b_ref, o_ref):
        o_ref[...] = (a_ref[...] + b_ref[...]).astype(o_ref.dtype)

    blk = pl.BlockSpec((1, tr, cols), lambda ch, i: (ch, i, 0))
    return pl.pallas_call(
        body, name=name, grid=(4, rows // tr), in_specs=[blk, blk], out_specs=blk,
        out_shape=jax.ShapeDtypeStruct(a.shape, BF16),
        compiler_params=pltpu.CompilerParams(dimension_semantics=("parallel", "parallel"),
                                             vmem_limit_bytes=VMEM_LIMIT_BYTES),
    )(a, b)


def _adamw(name, gparts, w, m, v):
    parts, rows, cols = gparts.shape
    tr = _row_tile(rows, cols, parts)
    c1 = 1.0 / (1.0 - ADAM_B1 ** ADAM_STEP)
    c2 = 1.0 / (1.0 - ADAM_B2 ** ADAM_STEP)

    def body(gp_ref, w_ref, m_ref, v_ref, g_ref, d_ref, nm_ref, nv_ref):
        g = gp_ref[0].astype(F32)
        for i in range(1, parts):
            g = g + gp_ref[i].astype(F32)
        nm = ADAM_B1 * m_ref[...] + (1.0 - ADAM_B1) * g
        nv = ADAM_B2 * v_ref[...] + (1.0 - ADAM_B2) * (g * g)
        m_hat = nm * c1
        v_hat = nv * c2
        g_ref[...] = g
        nm_ref[...] = nm
        nv_ref[...] = nv
        d_ref[...] = -ADAM_LR * (m_hat / (jnp.sqrt(v_hat) + ADAM_EPS) + ADAM_WD * w_ref[...])

    row = pl.BlockSpec((tr, cols), lambda i: (i, 0))
    return pl.pallas_call(
        body,
        name=name,
        grid=(rows // tr,),
        in_specs=[pl.BlockSpec((parts, tr, cols), lambda i: (0, i, 0)), row, row, row],
        out_specs=[row] * 4,
        out_shape=[jax.ShapeDtypeStruct((rows, cols), F32)] * 4,
        compiler_params=pltpu.CompilerParams(dimension_semantics=("parallel",), vmem_limit_bytes=VMEM_LIMIT_BYTES),
    )(gparts, w, m, v)


def _tile_rows(rows, pref):
    t = min(pref, rows)
    while rows % t:
        t -= 8
    return t


PACK_ROWS = 512


def _pack(arrs, dtype):
    flat = jnp.concatenate([a.astype(dtype).reshape(-1) for a in arrs])
    quantum = PACK_ROWS * LANES
    padded = -(-flat.shape[0] // quantum) * quantum
    return jnp.pad(flat, (0, padded - flat.shape[0])).reshape(-1, LANES)


def _unpack(buf, shapes, lead=()):
    flat = buf.reshape(lead + (-1,))
    out, off = [], 0
    for shp in shapes:
        n = math.prod(shp)
        out.append(flat[..., off:off + n].reshape(lead + tuple(shp)))
        off += n
    return out


def _dest_core_major(full, axis):
    shp = full.shape
    n = shp[axis] // N_DEV
    r = full.reshape(shp[:axis] + (4, 2, n) + shp[axis + 1:])
    return jnp.moveaxis(r, (axis + 1, axis), (0, 1))


def _from_gathered(g, axis):
    r = jnp.moveaxis(g, 0, axis)
    shp = r.shape
    return r.reshape(shp[:axis] + (shp[axis] * shp[axis + 1],) + shp[axis + 2:])


def _shift_down(a, k):
    return jnp.pad(a, ((k, 0), (0, 0)))[:a.shape[0]] if k else a


def _shift_up(a, k, fill=0.0):
    return jnp.pad(a, ((0, k), (0, 0)), constant_values=fill)[k:] if k else a


def _lru_fwd(u, ax_shift, lw, S):
    T = min(256, S)
    nb = S // T
    row = pl.BlockSpec((T, BRANCH), lambda t: (t, 0))
    vecs = [lw["cw0"], lw["cw1"], lw["cw2"], lw["cw3"], lw["conv_b"], lw["wa"], lw["wx"], lw["ba"], lw["bx"],
            lw["lam"]]

    def body(ax0, ax1, ax2, ax3, ay, cw0, cw1, cw2, cw3, cb, wa, wx, ba, bx, lam, xc_o, r_o, i_o, a_o, h_o, ya_o, hc):
        t = pl.program_id(0)

        @pl.when(t == 0)
        def _():
            hc[...] = jnp.zeros_like(hc)

        xc = cw3[...] * ax0[...] + cw2[...] * ax1[...] + cw1[...] * ax2[...] + cw0[...] * ax3[...] + cb[...]
        r = _sigmoid(_dot(xc, wa[...], "nn") + ba[...])
        gi = _sigmoid(_dot(xc, wx[...], "nn") + bx[...])
        sp = _softplus(-lam[...])
        la = -LRU_C * r * sp
        a = jnp.exp(la)
        mult = jnp.sqrt(_neg_expm1(2.0 * la))
        A, B = _scan_rows(a, mult * gi * xc, T)
        h = B + A * hc[...]
        h_o[...] = h
        hc[...] = h_o[pl.ds(T - 1, 1), :]
        xc_o[...] = xc
        r_o[...] = r
        i_o[...] = gi
        a_o[...] = a
        gy, _ = _gelu_and_grad(ay[...])
        ya_o[...] = (gy * h).astype(ya_o.dtype)

    outs = pl.pallas_call(
        body,
        name="lru_fwd",
        grid=(nb,),
        in_specs=[pl.BlockSpec((T, BRANCH), lambda t: (t, U_AX // BRANCH))] + [row] * 3
        + [pl.BlockSpec((T, BRANCH), lambda t: (t, U_AY // BRANCH))] + [_full_spec(v) for v in vecs],
        out_specs=[row] * 6,
        out_shape=[jax.ShapeDtypeStruct((S, BRANCH), F32)] * 5 + [jax.ShapeDtypeStruct((S, BRANCH), BF16)],
        scratch_shapes=[pltpu.VMEM((1, BRANCH), F32)],
        compiler_params=pltpu.CompilerParams(dimension_semantics=("arbitrary",), vmem_limit_bytes=VMEM_LIMIT_BYTES),
    )(u, *ax_shift[1:], u, *vecs)
    return outs


def _lru_bwd(dya, u, sv, lw, S):
    T = min(256, S)
    nb = S // T
    rrow = pl.BlockSpec((T, BRANCH), lambda t: (nb - 1 - t, 0))
    sq = pl.BlockSpec((BRANCH, BRANCH), lambda t: (0, 0))
    vrow = pl.BlockSpec((1, BRANCH), lambda t: (0, 0))
    h_prev = _shift_down(sv["h"], 1)
    a_next = _shift_up(sv["a"], 1)

    def body(dya_r, ay, h, hp, xc_r, r_r, i_r, a_r, an, wa, wx, lam,
             day_o, dxc_o, dwa_o, dwx_o, dba_o, dbx_o, dlam_o, lcar, tmp):
        t = pl.program_id(0)

        @pl.when(t == 0)
        def _():
            lcar[...] = jnp.zeros_like(lcar)
            dwa_o[...] = jnp.zeros_like(dwa_o)
            dwx_o[...] = jnp.zeros_like(dwx_o)
            dba_o[...] = jnp.zeros_like(dba_o)
            dbx_o[...] = jnp.zeros_like(dbx_o)
            dlam_o[...] = jnp.zeros_like(dlam_o)

        gy, dgy = _gelu_and_grad(ay[...])
        dy = dya_r[...]
        day_o[...] = (dy * h[...] * dgy).astype(day_o.dtype)
        A, B = _scan_rows(an[...], dy * gy, T, reverse=True)
        lmb = B + A * lcar[...]
        tmp[...] = lmb
        lcar[...] = tmp[pl.ds(0, 1), :]
        xc, r, gi, a = xc_r[...], r_r[...], i_r[...], a_r[...]
        sp = _softplus(-lam[...])
        la = -LRU_C * r * sp
        mult = jnp.sqrt(_neg_expm1(2.0 * la))
        da = lmb * hp[...]
        dmult = lmb * gi * xc
        di = lmb * mult * xc
        dxc = lmb * mult * gi
        dla = da * a - dmult * a * a / mult
        dr = dla * (-LRU_C * sp)
        dlam_o[...] += _colsum(dla * (LRU_C * r)) * _sigmoid(-lam[...])
        dpr = dr * r * (1.0 - r)
        dpi = di * gi * (1.0 - gi)
        dba_o[...] += _colsum(dpr)
        dbx_o[...] += _colsum(dpi)
        dxc_o[...] = dxc + _dot(dpr, wa[...], "nt") + _dot(dpi, wx[...], "nt")
        dwa_o[...] += _dot(xc, dpr, "tn")
        dwx_o[...] += _dot(xc, dpi, "tn")

    outs = pl.pallas_call(
        body,
        name="lru_bwd",
        grid=(nb,),
        in_specs=[rrow, pl.BlockSpec((T, BRANCH), lambda t: (nb - 1 - t, U_AY // BRANCH))] + [rrow] * 7
        + [sq, sq, vrow],
        out_specs=[rrow, rrow, sq, sq, vrow, vrow, vrow],
        out_shape=[jax.ShapeDtypeStruct((S, BRANCH), BF16), jax.ShapeDtypeStruct((S, BRANCH), F32),
                   jax.ShapeDtypeStruct((BRANCH, BRANCH), F32), jax.ShapeDtypeStruct((BRANCH, BRANCH), F32),
                   jax.ShapeDtypeStruct((1, BRANCH), F32), jax.ShapeDtypeStruct((1, BRANCH), F32),
                   jax.ShapeDtypeStruct((1, BRANCH), F32)],
        scratch_shapes=[pltpu.VMEM((1, BRANCH), F32), pltpu.VMEM((T, BRANCH), F32)],
        compiler_params=pltpu.CompilerParams(dimension_semantics=("arbitrary",), vmem_limit_bytes=VMEM_LIMIT_BYTES),
    )(dya, u, sv["h"], h_prev, sv["xc"], sv["r"], sv["i"], sv["a"], a_next, lw["wa"], lw["wx"], lw["lam"])
    return outs


def _conv_bwd(dxc, ax_shift, lw, S):
    dxs = [_shift_up(dxc, k) for k in range(4)]

    def fn(d0, d1, d2, d3, a0, a1, a2, a3, cw0, cw1, cw2, cw3):
        dax = cw3 * d0 + cw2 * d1 + cw1 * d2 + cw0 * d3
        return [dax], [_colsum(d0 * a3), _colsum(d0 * a2), _colsum(d0 * a1), _colsum(d0 * a0), _colsum(d0)]

    (dax,), sums = _rowwise("conv_bwd", fn, dxs + list(ax_shift), [lw["cw0"], lw["cw1"], lw["cw2"], lw["cw3"]],
                            [(BRANCH, BF16)], [BRANCH] * 5, S)
    return dax, sums


GLA_QK = GLA_HEADS * GLA_DK
GLA_V = GLA_HEADS * GLA_DV
GLA_SCALE = GLA_DK ** -0.5


def _gla_specs(TB, rev_nb=None):
    def rmap(t):
        return t if rev_nb is None else rev_nb - 1 - t
    return [
        pl.BlockSpec((TB, GLA_QK), lambda t: (rmap(t), U_BQ // GLA_QK)),
        pl.BlockSpec((TB, GLA_QK), lambda t: (rmap(t), U_BK // GLA_QK)),
        pl.BlockSpec((TB, GLA_V), lambda t: (rmap(t), U_BV // GLA_V)),
        pl.BlockSpec((TB, GLA_V), lambda t: (rmap(t), U_BR // GLA_V)),
        pl.BlockSpec((TB, LANES), lambda t: (rmap(t), U_BLOW // LANES)),
    ]


def _gla_gates(gl, wg2, bg, TB):
    pre = _dot(gl, wg2, "nn") + bg
    la = _log_sigmoid(pre) * (1.0 / GLA_TAU)
    _, gc = _scan_rows(None, la, TB, seg=CHUNK)
    return pre, la, gc


def _gla_fwd(u, gw, S):
    TB = min(512, S)
    nb = S // TB
    cpb = TB // CHUNK
    vecs = [gw["wg2"], gw["bg"], gw["ng"], gw["bd"]]

    def body(q_r, k_r, v_r, br_r, gl_r, wg2, bg, ng, bd, yb_o, oraw_o, st_o, st):
        t = pl.program_id(0)

        @pl.when(t == 0)
        def _():
            st[...] = jnp.zeros_like(st)

        _, la, gc = _gla_gates(gl_r[...], wg2[...], bg[...], TB)
        for c in range(cpb):
            sl = slice(c * CHUNK, (c + 1) * CHUNK)
            gt = _colsum(la[sl])
            kdec = k_r[sl, :] * jnp.exp(gt - gc[sl])
            d_t = _dot(v_r[sl, :], kdec, "tn") * bd[...]
            s_new = st[...] * jnp.exp(gt) + d_t
            st[...] = s_new
            st_o[c] = s_new
            oraw_o[sl, :] = _dot(q_r[sl, :] * GLA_SCALE, s_new, "nt")
        for h in range(GLA_HEADS):
            hs = slice(h * GLA_DV, (h + 1) * GLA_DV)
            oh = oraw_o[:, hs]
            on = oh * lax.rsqrt(jnp.mean(oh * oh, axis=-1, keepdims=True) + RMS_EPS)
            sil, _ = _silu_and_grad(br_r[:, hs])
            yb_o[:, hs] = (on * ng[:, hs] * sil).astype(yb_o.dtype)

    return pl.pallas_call(
        body,
        name="gla_fwd",
        grid=(nb,),
        in_specs=_gla_specs(TB) + [_full_spec(v) for v in vecs],
        out_specs=[pl.BlockSpec((TB, GLA_V), lambda t: (t, 0)), pl.BlockSpec((TB, GLA_V), lambda t: (t, 0)),
                   pl.BlockSpec((cpb, GLA_V, GLA_QK), lambda t: (t, 0, 0))],
        out_shape=[jax.ShapeDtypeStruct((S, GLA_V), BF16), jax.ShapeDtypeStruct((S, GLA_V), F32),
                   jax.ShapeDtypeStruct((S // CHUNK, GLA_V, GLA_QK), F32)],
        scratch_shapes=[pltpu.VMEM((GLA_V, GLA_QK), F32)],
        compiler_params=pltpu.CompilerParams(dimension_semantics=("arbitrary",), vmem_limit_bytes=VMEM_LIMIT_BYTES),
    )(u, u, u, u, u, *vecs)


def _gla_bwd(dyb, u, oraw, states, gw, S):
    TB = min(512, S)
    nb = S // TB
    cpb = TB // CHUNK
    vecs = [gw["wg2"], gw["bg"], gw["ng"], gw["bd"]]

    def rrow(width):
        return pl.BlockSpec((TB, width), lambda t: (nb - 1 - t, 0))

    def body(dyb_r, oraw_r, q_r, k_r, v_r, br_r, gl_r, st_r, sp_r, wg2, bg, ng, bd,
             dq_o, dk_o, dv_o, dbr_o, dgl_o, dwg2_o, dbg_o, dng_o, dcar, do_buf, dla_buf):
        t = pl.program_id(0)
        blk = nb - 1 - t

        @pl.when(t == 0)
        def _():
            dcar[...] = jnp.zeros_like(dcar)
            dwg2_o[...] = jnp.zeros_like(dwg2_o)
            dbg_o[...] = jnp.zeros_like(dbg_o)
            dng_o[...] = jnp.zeros_like(dng_o)

        pre, la, gc = _gla_gates(gl_r[...], wg2[...], bg[...], TB)
        for h in range(GLA_HEADS):
            hs = slice(h * GLA_DV, (h + 1) * GLA_DV)
            oh = oraw_r[:, hs]
            rs = lax.rsqrt(jnp.mean(oh * oh, axis=-1, keepdims=True) + RMS_EPS)
            on = oh * rs
            sil, dsil = _silu_and_grad(br_r[:, hs])
            dy = dyb_r[:, hs]
            dbr_o[:, hs] = (dy * on * ng[:, hs] * dsil).astype(dbr_o.dtype)
            don = dy * ng[:, hs] * sil
            dng_o[:, hs] += _colsum(dy * on * sil)
            do_buf[:, hs] = rs * (don - on * jnp.mean(don * on, axis=-1, keepdims=True))
        first = jnp.where(blk == 0, 0.0, 1.0)
        for c in reversed(range(cpb)):
            sl = slice(c * CHUNK, (c + 1) * CHUNK)
            s_n = st_r[c]
            s_prev = st_r[c - 1] if c > 0 else sp_r[0] * first
            gt = _colsum(la[sl])
            w = jnp.exp(gt - gc[sl])
            k_c = k_r[sl, :]
            kdec = k_c * w
            qs = q_r[sl, :] * GLA_SCALE
            do_c = do_buf[sl, :]
            dq_o[sl, :] = (_dot(do_c, s_n, "nn") * GLA_SCALE).astype(dq_o.dtype)
            d_n = _dot(do_c, qs, "tn") * bd[...] + dcar[...]
            dv_o[sl, :] = _dot(kdec, d_n, "nt").astype(dv_o.dtype)
            dkdec = _dot(v_r[sl, :], d_n, "nn")
            dk_o[sl, :] = (dkdec * w).astype(dk_o.dtype)
            tt = dkdec * kdec
            e = jnp.exp(gt)
            dgt = _colsum(tt) + _colsum(d_n * s_prev) * e
            _, rc = _scan_rows(None, -tt, CHUNK, reverse=True)
            dla_buf[sl, :] = rc + dgt
            dcar[...] = d_n * e
        dpre = dla_buf[...] * _sigmoid(-pre) * (1.0 / GLA_TAU)
        dbg_o[...] += _colsum(dpre)
        dgl_o[...] = _dot(dpre, wg2[...], "nt").astype(dgl_o.dtype)
        dwg2_o[...] += _dot(gl_r[...], dpre, "tn")

    return pl.pallas_call(
        body,
        name="gla_bwd",
        grid=(nb,),
        in_specs=[rrow(GLA_V), rrow(GLA_V)] + _gla_specs(TB, rev_nb=nb)
        + [pl.BlockSpec((cpb, GLA_V, GLA_QK), lambda t: (nb - 1 - t, 0, 0)),
           pl.BlockSpec((1, GLA_V, GLA_QK), lambda t: (jnp.maximum((nb - 1 - t) * cpb - 1, 0), 0, 0))]
        + [_full_spec(v) for v in vecs],
        out_specs=[rrow(GLA_QK), rrow(GLA_QK), rrow(GLA_V), rrow(GLA_V), rrow(LANES),
                   pl.BlockSpec((LANES, GLA_QK), lambda t: (0, 0)), pl.BlockSpec((1, GLA_QK), lambda t: (0, 0)),
                   pl.BlockSpec((1, GLA_V), lambda t: (0, 0))],
        out_shape=[jax.ShapeDtypeStruct((S, GLA_QK), BF16), jax.ShapeDtypeStruct((S, GLA_QK), BF16),
                   jax.ShapeDtypeStruct((S, GLA_V), BF16), jax.ShapeDtypeStruct((S, GLA_V), BF16),
                   jax.ShapeDtypeStruct((S, LANES), BF16), jax.ShapeDtypeStruct((LANES, GLA_QK), F32),
                   jax.ShapeDtypeStruct((1, GLA_QK), F32), jax.ShapeDtypeStruct((1, GLA_V), F32)],
        scratch_shapes=[pltpu.VMEM((GLA_V, GLA_QK), F32), pltpu.VMEM((TB, GLA_V), F32),
                        pltpu.VMEM((TB, GLA_QK), F32)],
        compiler_params=pltpu.CompilerParams(dimension_semantics=("arbitrary",), vmem_limit_bytes=VMEM_LIMIT_BYTES),
    )(dyb, oraw, u, u, u, u, u, states, states, *vecs)


FOX_SCALE = FOX_DH ** -0.5


def _fox_gate_fwd(u, bfp, S):
    T = min(512, S)

    def body(f_r, b_r, fc_o, car):
        t = pl.program_id(0)

        @pl.when(t == 0)
        def _():
            car[...] = jnp.zeros_like(car)

        _, cs = _scan_rows(None, _log_sigmoid(f_r[...] + b_r[...]), T)
        fc_o[...] = cs + car[...]
        car[...] = fc_o[pl.ds(T - 1, 1), :]

    return pl.pallas_call(
        body,
        name="fox_gate_fwd",
        grid=(S // T,),
        in_specs=[pl.BlockSpec((T, LANES), lambda t: (t, U_CF // LANES)), _full_spec(bfp)],
        out_specs=pl.BlockSpec((T, LANES), lambda t: (t, 0)),
        out_shape=jax.ShapeDtypeStruct((S, LANES), F32),
        scratch_shapes=[pltpu.VMEM((1, LANES), F32)],
        compiler_params=pltpu.CompilerParams(dimension_semantics=("arbitrary",), vmem_limit_bytes=VMEM_LIMIT_BYTES),
    )(u, bfp)


def _fox_gate_bwd(dfc, u, bfp, S):
    T = min(512, S)
    nb = S // T

    def body(d_r, f_r, b_r, df_o, db_o, car, tmp):
        t = pl.program_id(0)

        @pl.when(t == 0)
        def _():
            car[...] = jnp.zeros_like(car)
            db_o[...] = jnp.zeros_like(db_o)

        _, rc = _scan_rows(None, d_r[...], T, reverse=True)
        tmp[...] = rc + car[...]
        car[...] = tmp[pl.ds(0, 1), :]
        df = tmp[...] * _sigmoid(-(f_r[...] + b_r[...]))
        df_o[...] = df.astype(df_o.dtype)
        db_o[...] += _colsum(df)

    return pl.pallas_call(
        body,
        name="fox_gate_bwd",
        grid=(nb,),
        in_specs=[pl.BlockSpec((T, LANES), lambda t: (nb - 1 - t, 0)),
                  pl.BlockSpec((T, LANES), lambda t: (nb - 1 - t, U_CF // LANES)), _full_spec(bfp)],
        out_specs=[pl.BlockSpec((T, LANES), lambda t: (nb - 1 - t, 0)), pl.BlockSpec((1, LANES), lambda t: (0, 0))],
        out_shape=[jax.ShapeDtypeStruct((S, LANES), BF16), jax.ShapeDtypeStruct((1, LANES), F32)],
        scratch_shapes=[pltpu.VMEM((1, LANES), F32), pltpu.VMEM((T, LANES), F32)],
        compiler_params=pltpu.CompilerParams(dimension_semantics=("arbitrary",), vmem_limit_bytes=VMEM_LIMIT_BYTES),
    )(dfc, u, bfp)


def _fox_scores(q, k, fq, fk, qi, ki, tq, tk):
    s = _dot(q, k, "nt") * FOX_SCALE + (fq - fk)
    rows = lax.broadcasted_iota(jnp.int32, (tq, tk), 0) + qi * tq
    cols = lax.broadcasted_iota(jnp.int32, (tq, tk), 1) + ki * tk
    return jnp.where(cols <= rows, s, NEG_BIG)


def _fox_fwd(qh, kh, vh, fq, fk, S):
    tq = tk = min(512, S)
    nq, nk = S // tq, S // tk

    def body(q_r, k_r, v_r, fq_r, fk_r, o_o, lse_o, m_s, l_s, acc):
        qi, ki = pl.program_id(1), pl.program_id(2)

        @pl.when(ki == 0)
        def _():
            m_s[...] = jnp.full_like(m_s, NEG_BIG)
            l_s[...] = jnp.zeros_like(l_s)
            acc[...] = jnp.zeros_like(acc)

        @pl.when(ki <= qi)
        def _():
            s = _fox_scores(q_r[0], k_r[0], fq_r[0], fk_r[0], qi, ki, tq, tk)
            m_new = jnp.maximum(m_s[...], jnp.max(s, axis=-1, keepdims=True))
            p = jnp.exp(s - m_new)
            alpha = jnp.exp(m_s[...] - m_new)
            l_s[...] = alpha * l_s[...] + jnp.sum(p, axis=-1, keepdims=True)
            acc[...] = alpha * acc[...] + _dot(p, v_r[0], "nn")
            m_s[...] = m_new

        @pl.when(ki == nk - 1)
        def _():
            o_o[0] = acc[...] / l_s[...]
            lse_o[0] = m_s[...] + jnp.log(l_s[...])

    kv = pl.BlockSpec((1, tk, FOX_DH), lambda h, i, j: (h, jnp.minimum(j, i), 0))
    return pl.pallas_call(
        body,
        name="fox_fwd",
        grid=(FOX_HEADS, nq, nk),
        in_specs=[pl.BlockSpec((1, tq, FOX_DH), lambda h, i, j: (h, i, 0)), kv, kv,
                  pl.BlockSpec((1, tq, 1), lambda h, i, j: (h, i, 0)),
                  pl.BlockSpec((1, 1, tk), lambda h, i, j: (h, 0, jnp.minimum(j, i)))],
        out_specs=[pl.BlockSpec((1, tq, FOX_DH), lambda h, i, j: (h, i, 0)),
                   pl.BlockSpec((1, tq, 1), lambda h, i, j: (h, i, 0))],
        out_shape=[jax.ShapeDtypeStruct((FOX_HEADS, S, FOX_DH), F32), jax.ShapeDtypeStruct((FOX_HEADS, S, 1), F32)],
        scratch_shapes=[pltpu.VMEM((tq, 1), F32), pltpu.VMEM((tq, 1), F32), pltpu.VMEM((tq, FOX_DH), F32)],
        compiler_params=pltpu.CompilerParams(
            dimension_semantics=("parallel", "parallel", "arbitrary"), vmem_limit_bytes=VMEM_LIMIT_BYTES),
    )(qh, kh, vh, fq, fk)


def _fox_bwd_dq(qh, kh, vh, fq, fk, o, do, lse, S):
    tq = tk = min(512, S)
    nq, nk = S // tq, S // tk

    def body(q_r, k_r, v_r, fq_r, fk_r, o_r, do_r, lse_r, dq_o, dfq_o, dq_acc, df_acc):
        qi, ki = pl.program_id(1), pl.program_id(2)

        @pl.when(ki == 0)
        def _():
            dq_acc[...] = jnp.zeros_like(dq_acc)
            df_acc[...] = jnp.zeros_like(df_acc)

        @pl.when(ki <= qi)
        def _():
            s = _fox_scores(q_r[0], k_r[0], fq_r[0], fk_r[0], qi, ki, tq, tk)
            p = jnp.exp(s - lse_r[0])
            do_t = do_r[0]
            delta = jnp.sum(o_r[0] * do_t, axis=-1, keepdims=True)
            ds = p * (_dot(do_t, v_r[0], "nt") - delta)
            dq_acc[...] += _dot(ds, k_r[0], "nn")
            df_acc[...] += jnp.sum(ds, axis=-1, keepdims=True)

        @pl.when(ki == nk - 1)
        def _():
            dq_o[0] = dq_acc[...] * FOX_SCALE
            dfq_o[0] = df_acc[...]

    qrow = pl.BlockSpec((1, tq, FOX_DH), lambda h, i, j: (h, i, 0))
    qcol = pl.BlockSpec((1, tq, 1), lambda h, i, j: (h, i, 0))
    kv = pl.BlockSpec((1, tk, FOX_DH), lambda h, i, j: (h, jnp.minimum(j, i), 0))
    return pl.pallas_call(
        body,
        name="fox_bwd_dq",
        grid=(FOX_HEADS, nq, nk),
        in_specs=[qrow, kv, kv, qcol, pl.BlockSpec((1, 1, tk), lambda h, i, j: (h, 0, jnp.minimum(j, i))),
                  qrow, qrow, qcol],
        out_specs=[qrow, qcol],
        out_shape=[jax.ShapeDtypeStruct((FOX_HEADS, S, FOX_DH), F32), jax.ShapeDtypeStruct((FOX_HEADS, S, 1), F32)],
        scratch_shapes=[pltpu.VMEM((tq, FOX_DH), F32), pltpu.VMEM((tq, 1), F32)],
        compiler_params=pltpu.CompilerParams(
            dimension_semantics=("parallel", "parallel", "arbitrary"), vmem_limit_bytes=VMEM_LIMIT_BYTES),
    )(qh, kh, vh, fq, fk, o, do, lse)


def _fox_bwd_dkv(qh, kh, vh, fq, fk, o, do, lse, S):
    tq = tk = min(512, S)
    nq, nk = S // tq, S // tk

    def body(q_r, k_r, v_r, fq_r, fk_r, o_r, do_r, lse_r, dk_o, dv_o, dfk_o, dk_acc, dv_acc, df_acc):
        ki, qi = pl.program_id(1), pl.program_id(2)

        @pl.when(qi == 0)
        def _():
            dk_acc[...] = jnp.zeros_like(dk_acc)
            dv_acc[...] = jnp.zeros_like(dv_acc)
            df_acc[...] = jnp.zeros_like(df_acc)

        @pl.when(qi >= ki)
        def _():
            s = _fox_scores(q_r[0], k_r[0], fq_r[0], fk_r[0], qi, ki, tq, tk)
            p = jnp.exp(s - lse_r[0])
            do_t = do_r[0]
            delta = jnp.sum(o_r[0] * do_t, axis=-1, keepdims=True)
            ds = p * (_dot(do_t, v_r[0], "nt") - delta)
            dv_acc[...] += _dot(p, do_t, "tn")
            dk_acc[...] += _dot(ds, q_r[0], "tn")
            df_acc[...] += _colsum(ds)

        @pl.when(qi == nq - 1)
        def _():
            dk_o[0] = dk_acc[...] * FOX_SCALE
            dv_o[0] = dv_acc[...]
            dfk_o[0] = df_acc[...]

    qrow = pl.BlockSpec((1, tq, FOX_DH), lambda h, j, i: (h, jnp.maximum(i, j), 0))
    qcol = pl.BlockSpec((1, tq, 1), lambda h, j, i: (h, jnp.maximum(i, j), 0))
    kv = pl.BlockSpec((1, tk, FOX_DH), lambda h, j, i: (h, j, 0))
    krow = pl.BlockSpec((1, 1, tk), lambda h, j, i: (h, 0, j))
    return pl.pallas_call(
        body,
        name="fox_bwd_dkv",
        grid=(FOX_HEADS, nk, nq),
        in_specs=[qrow, kv, kv, qcol, krow, qrow, qrow, qcol],
        out_specs=[kv, kv, krow],
        out_shape=[jax.ShapeDtypeStruct((FOX_HEADS, S, FOX_DH), F32), jax.ShapeDtypeStruct((FOX_HEADS, S, FOX_DH), F32),
                   jax.ShapeDtypeStruct((FOX_HEADS, 1, S), F32)],
        scratch_shapes=[pltpu.VMEM((tk, FOX_DH), F32), pltpu.VMEM((tk, FOX_DH), F32), pltpu.VMEM((1, tk), F32)],
        compiler_params=pltpu.CompilerParams(
            dimension_semantics=("parallel", "parallel", "arbitrary"), vmem_limit_bytes=VMEM_LIMIT_BYTES),
    )(qh, kh, vh, fq, fk, o, do, lse)


def _to_heads(x2d, S):
    return jnp.transpose(x2d.reshape(S, FOX_HEADS, FOX_DH), (1, 0, 2))


def _from_heads(xh, S):
    return jnp.transpose(xh, (1, 0, 2)).reshape(S, FOX_HEADS * FOX_DH)


def _ffn_fwd(tag, x, wg, wu, wd, g, b, S):
    def up_epi(accs):
        gate, up = accs
        sil, _ = _silu_and_grad(gate)
        return [gate, up, sil * up]

    gate, up, act = _mm(tag + "_up", "nn", [x], [wg, wu], [(0, 0, 0), (1, 0, 1)], 2, up_epi, [],
                        [F32, F32, BF16], S, D_FF, D_MODEL, tn=1408)

    def down_epi(accs, xr, gg, bb):
        z = ALPHA * xr + 0.5 * accs[0]
        return [z, _ln_fwd(z, gg, bb)]

    z, xn = _mm(tag + "_down", "nn", [act], [wd], [(0, 0, 0)], 1, down_epi, [(x, "mn", 0), (g, "n"), (b, "n")],
                [F32, F32], S, D_MODEL, D_FF, tk=1408)
    return xn, dict(x=x, gate=gate, up=up, act=act, z=z)


def _ln_bwd_call(tag, dy, z, g, S):
    def fn(dy_t, z_t, g_t):
        dz, xhat = _ln_bwd(dy_t, z_t, g_t)
        return [dz], [_colsum(dy_t * xhat), _colsum(dy_t)]

    (dz,), (dg, db) = _rowwise(tag + "_ln_bwd", fn, [dy, z], [g], [(D_MODEL, F32)], [D_MODEL, D_MODEL], S)
    return dz, dg, db


def _ffn_bwd(tag, dxn, sv, wg, wu, wd, g, S):
    dz, dg, db = _ln_bwd_call(tag, dxn, sv["z"], g, S)

    def act_epi(accs, gate, up):
        da = 0.5 * accs[0]
        sil, dsil = _silu_and_grad(gate)
        return [da * up * dsil, da * sil]

    dgate, dup = _mm(tag + "_dact", "nt", [dz], [wd], [(0, 0, 0)], 1, act_epi,
                     [(sv["gate"], "mn", 0), (sv["up"], "mn", 0)], [BF16, BF16], S, D_FF, D_MODEL, tn=1408)
    dwd = _mm1(tag + "_dwd", "tn", sv["act"], dz, D_FF, D_MODEL, S, scale=0.5, tm=1408)

    def two(accs):
        return [accs[0], accs[1]]

    dwg, dwu = _mm(tag + "_dwup", "tn", [sv["x"]], [dgate, dup], [(0, 0, 0), (1, 0, 1)], 2, two, [], [F32, F32],
                   D_MODEL, D_FF, S, tn=1408)

    def dx_epi(accs, dzr):
        return [accs[0] + ALPHA * dzr]

    (dx,) = _mm(tag + "_dx", "nt", [dgate, dup], [wg, wu], [(0, 0, 0), (0, 1, 1)], 1, dx_epi, [(dz, "mn", 0)],
                [F32], S, D_MODEL, D_FF, tk=1408)
    return dx, dict(w_up=jnp.concatenate([dwg, dwu], axis=1), w_down=dwd, ln_g=dg, ln_b=db)


def _mixer_fwd(x1, w, S):
    u = _mm1("w_in", "nn", x1, w["w_in_p"], S, U_WIDTH, D_MODEL, tn=1536)
    ax0 = u[:, U_AX:U_AX + BRANCH]
    ax_shift = [ax0] + [_shift_down(ax0, k) for k in (1, 2, 3)]
    xc, r, gi, a, h, y_a = _lru_fwd(u, ax_shift, w["lru"], S)
    y_b, oraw, states = _gla_fwd(u, w["gla"], S)
    fcum = _fox_gate_fwd(u, w["bfp"], S)
    qh = _to_heads(u[:, U_CQ:U_CQ + BRANCH].astype(BF16), S)
    kh = _to_heads(u[:, U_CK:U_CK + BRANCH].astype(BF16), S)
    vh = _to_heads(u[:, U_CV:U_CV + BRANCH].astype(BF16), S)
    fct = jnp.transpose(fcum[:, :FOX_HEADS])
    fq = fct[:, :, None]
    fk = fct[:, None, :]
    oh, lse = _fox_fwd(qh, kh, vh, fq, fk, S)
    y_c = _from_heads(oh, S).astype(BF16)

    def merge_epi(accs, g0, g1, g2):
        merged = _sigmoid(g0) * accs[0] + _sigmoid(g1) * accs[1] + _sigmoid(g2) * accs[2]
        return [accs[0], accs[1], accs[2], merged]

    wb = w["w_branch"]
    yp0, yp1, yp2, merged = _mm(
        "merge", "nn", [y_a, y_b, y_c], [wb[0], wb[1], wb[2]], [(0, 0, 0), (1, 1, 1), (2, 2, 2)], 3, merge_epi,
        [(u, "mn", 0), (u, "mn", 1), (u, "mn", 2)], [F32, F32, F32, BF16], S, D_MODEL, BRANCH, tm=256)

    def out_epi(accs, xr, gg, bb):
        z = ALPHA * xr + accs[0]
        return [z, _ln_fwd(z, gg, bb)]

    z2, x2 = _mm("w_out", "nn", [merged], [w["w_out"]], [(0, 0, 0)], 1, out_epi,
                 [(x1, "mn", 0), (w["ln2_g"], "n"), (w["ln2_b"], "n")], [F32, F32], S, D_MODEL, D_MODEL)
    sv = dict(x=x1, u=u, ax_shift=ax_shift, xc=xc, r=r, i=gi, a=a, h=h, y_a=y_a, y_b=y_b, y_c=y_c, oraw=oraw,
              states=states, qh=qh, kh=kh, vh=vh, fq=fq, fk=fk, oh=oh, lse=lse, yp=(yp0, yp1, yp2), merged=merged,
              z=z2)
    return x2, sv


def _mixer_bwd(dx2, sv, w, S):
    u = sv["u"]
    dz, dg2, db2 = _ln_bwd_call("mix", dx2, sv["z"], w["ln2_g"], S)

    def dm_epi(accs, y0, y1, y2, g0, g1, g2):
        dm = accs[0]
        outs_p, outs_g = [], []
        for yp, gl in ((y0, g0), (y1, g1), (y2, g2)):
            sg = _sigmoid(gl)
            outs_p.append(dm * sg)
            outs_g.append(dm * yp * sg * (1.0 - sg))
        return outs_p + outs_g

    yp = sv["yp"]
    dyp0, dyp1, dyp2, dgl0, dgl1, dgl2 = _mm(
        "dmerged", "nt", [dz], [w["w_out"]], [(0, 0, 0)], 1, dm_epi,
        [(yp[0], "mn", 0), (yp[1], "mn", 0), (yp[2], "mn", 0), (u, "mn", 0), (u, "mn", 1), (u, "mn", 2)],
        [BF16] * 6, S, D_MODEL, D_MODEL, tm=256)
    dw_out = _mm1("dw_out", "tn", sv["merged"], dz, D_MODEL, D_MODEL, S)
    wb = w["w_branch"]
    dys, dwbs = [], []
    for j, (yj, dyp) in enumerate(((sv["y_a"], dyp0), (sv["y_b"], dyp1), (sv["y_c"], dyp2))):
        dys.append(_mm1("dy_branch%d" % j, "nt", dyp, wb[j], S, BRANCH, D_MODEL))
        dwbs.append(_mm1("dw_branch%d" % j, "tn", yj, dyp, BRANCH, D_MODEL, S))
    day, dxc, dwa, dwx, dba, dbx, dlam = _lru_bwd(dys[0], u, sv, w["lru"], S)
    dax, (dcw0, dcw1, dcw2, dcw3, dcb) = _conv_bwd(dxc, sv["ax_shift"], w["lru"], S)
    dbq, dbk, dbv, dbr, dglow, dwg2p, dbg, dng = _gla_bwd(dys[1], u, sv["oraw"], sv["states"], w["gla"], S)
    doh = _to_heads(dys[2], S)
    args = (sv["qh"], sv["kh"], sv["vh"], sv["fq"], sv["fk"], sv["oh"], doh, sv["lse"], S)
    dqh, dfq = _fox_bwd_dq(*args)
    dkh, dvh, dfk = _fox_bwd_dkv(*args)
    dfc = jnp.transpose(dfq[:, :, 0] - dfk[:, 0, :])
    dfc = jnp.pad(dfc, ((0, 0), (0, LANES - FOX_HEADS)))
    dcf, dbf = _fox_gate_bwd(dfc, u, w["bfp"], S)
    du = jnp.concatenate(
        [dgl0, dgl1, dgl2, dax, day, dbq, dbk, dbv, dbr, _from_heads(dqh, S).astype(BF16),
         _from_heads(dkh, S).astype(BF16), _from_heads(dvh, S).astype(BF16), dglow, dcf,
         jnp.zeros((S, U_WIDTH - U_CF - LANES), BF16)], axis=1)
    dw_in_p = _mm1("dw_in", "tn", sv["x"], du, D_MODEL, U_WIDTH, S, tn=1536)

    def dx_epi(accs, dzr):
        return [accs[0] + ALPHA * dzr]

    (dx1,) = _mm("dx_mix", "nt", [du], [w["w_in_p"]], [(0, 0, 0)], 1, dx_epi, [(dz, "mn", 0)], [F32], S, D_MODEL,
                 U_WIDTH, tk=1536)
    pieces = sorted(W_IN_SEGMENTS)
    dw_in = jnp.concatenate([dw_in_p[:, dst:dst + width] for _, width, dst in pieces], axis=1)
    eye = jnp.eye(LRU_BLOCKS, dtype=F32)
    dwa_b = jnp.einsum("ncmd,nm->ncd", dwa.reshape(LRU_BLOCKS, 64, LRU_BLOCKS, 64), eye)
    dwx_b = jnp.einsum("ncmd,nm->ncd", dwx.reshape(LRU_BLOCKS, 64, LRU_BLOCKS, 64), eye)
    grads = dict(
        w_in=dw_in, w_out=dw_out, w_branch=jnp.stack(dwbs), ln2_g=dg2, ln2_b=db2,
        conv_w=jnp.concatenate([dcw0, dcw1, dcw2, dcw3], axis=0), conv_b=dcb, lru_wa=dwa_b, lru_wx=dwx_b,
        lru_ba=dba, lru_bx=dbx, lru_lambda=dlam, gla_w_g2=dwg2p[:GLA_LOWRANK], gla_b_g=dbg, gla_norm_g=dng,
        fox_b_f=dbf[:, :FOX_HEADS])
    return dx1, grads


def _ple_fwd(x3, p_i, w, S):
    pe = _mm1("ple_proj", "nn", p_i, w["ple_w_proj"], S, D_MODEL, PLE_DIM)

    def epi(accs, xr, per, bg, gg, bb):
        sg = _sigmoid(accs[0] + bg)
        z = ALPHA * xr + sg * per
        return [sg, z, _ln_fwd(z, gg, bb)]

    sg, z4, x4 = _mm("ple_gate", "nn", [x3], [w["ple_w_gate"]], [(0, 0, 0)], 1, epi,
                     [(x3, "mn", 0), (pe, "mn", 0), (w["ple_b_gate"], "n"), (w["ln4_g"], "n"), (w["ln4_b"], "n")],
                     [F32, F32, F32], S, D_MODEL, D_MODEL)
    return x4, dict(x=x3, p=p_i, pe=pe, sg=sg, z=z4)


def _ple_bwd(dx4, sv, w, S):
    def fn(dy_t, z_t, pe_t, sg_t, g_t):
        dz, xhat = _ln_bwd(dy_t, z_t, g_t)
        dgl = dz * pe_t * sg_t * (1.0 - sg_t)
        return [dz, dz * sg_t, dgl], [_colsum(dy_t * xhat), _colsum(dy_t), _colsum(dgl)]

    (dz, dpe, dgl), (dg4, db4, dbg) = _rowwise(
        "ple_bwd", fn, [dx4, sv["z"], sv["pe"], sv["sg"]], [w["ln4_g"]],
        [(D_MODEL, F32), (D_MODEL, BF16), (D_MODEL, BF16)], [D_MODEL] * 3, S)
    dwp = _mm1("dw_ple_proj", "tn", sv["p"], dpe, PLE_DIM, D_MODEL, S)
    dwg = _mm1("dw_ple_gate", "tn", sv["x"], dgl, D_MODEL, D_MODEL, S)

    def dx_epi(accs, dzr):
        return [accs[0] + ALPHA * dzr]

    (dx3,) = _mm("dx_ple", "nt", [dgl], [w["ple_w_gate"]], [(0, 0, 0)], 1, dx_epi, [(dz, "mn", 0)], [F32], S,
                 D_MODEL, D_MODEL)
    return dx3, dict(ple_w_proj=dwp, ple_w_gate=dwg, ple_b_gate=dbg, ln4_g=dg4, ln4_b=db4)


def _layer_weights(full, i):
    w = {}
    up = full["ffn1_w_up"][i]
    w["ffn1"] = (up[:, :D_FF], up[:, D_FF:], full["ffn1_w_down"][i])
    up = full["ffn2_w_up"][i]
    w["ffn2"] = (up[:, :D_FF], up[:, D_FF:], full["ffn2_w_down"][i])
    w_in = full["w_in"][i]
    placed = sorted((dst, src, width) for src, width, dst in W_IN_SEGMENTS)
    parts, pos = [], 0
    for dst, src, width in placed:
        if dst > pos:
            parts.append(jnp.zeros((D_MODEL, dst - pos), w_in.dtype))
        parts.append(w_in[:, src:src + width])
        pos = dst + width
    parts.append(jnp.zeros((D_MODEL, U_WIDTH - pos), w_in.dtype))
    w["w_in_p"] = jnp.concatenate(parts, axis=1)
    eye = jnp.eye(LRU_BLOCKS, dtype=F32)

    def dense(blocks):
        return jnp.einsum("ncd,nm->ncmd", blocks, eye).reshape(BRANCH, BRANCH).astype(BF16)

    def vec(name):
        return full[name][i].reshape(1, -1)

    cw = full["conv_w"][i]
    w["lru"] = dict(cw0=cw[0:1], cw1=cw[1:2], cw2=cw[2:3], cw3=cw[3:4], conv_b=vec("conv_b"),
                    wa=dense(full["lru_wa"][i]), wx=dense(full["lru_wx"][i]), ba=vec("lru_ba"), bx=vec("lru_bx"),
                    lam=vec("lru_lambda"))
    hq = jnp.arange(GLA_QK) // GLA_DK
    hv = jnp.arange(GLA_V) // GLA_DV
    w["gla"] = dict(wg2=jnp.pad(full["gla_w_g2"][i], ((0, LANES - GLA_LOWRANK), (0, 0))).astype(BF16),
                    bg=vec("gla_b_g"), ng=vec("gla_norm_g"), bd=(hv[:, None] == hq[None, :]).astype(F32))
    w["bfp"] = jnp.pad(vec("fox_b_f"), ((0, 0), (0, LANES - FOX_HEADS)))
    w["w_branch"] = full["w_branch"][i]
    w["w_out"] = full["w_out"][i]
    w["ple_w_proj"] = full["ple_w_proj"][i]
    w["ple_w_gate"] = full["ple_w_gate"][i]
    for name in ("ln1_g", "ln1_b", "ln2_g", "ln2_b", "ln3_g", "ln3_b", "ln4_g", "ln4_b", "ple_b_gate"):
        w[name] = vec(name)
    return w


def _layer_fwd(x0, p_i, w, S):
    x1, s1 = _ffn_fwd("ffn1", x0, *w["ffn1"], w["ln1_g"], w["ln1_b"], S)
    x2, s2 = _mixer_fwd(x1, w, S)
    x3, s3 = _ffn_fwd("ffn2", x2, *w["ffn2"], w["ln3_g"], w["ln3_b"], S)
    x4, s4 = _ple_fwd(x3, p_i, w, S)
    return x4, (s1, s2, s3, s4)


def _layer_bwd(dx4, saved, w, S):
    s1, s2, s3, s4 = saved
    dx3, g4 = _ple_bwd(dx4, s4, w, S)
    dx2, g3 = _ffn_bwd("ffn2", dx3, s3, *w["ffn2"], w["ln3_g"], S)
    dx1, g2 = _mixer_bwd(dx2, s2, w, S)
    dx0, g1 = _ffn_bwd("ffn1", dx1, s1, *w["ffn1"], w["ln1_g"], S)
    grads = dict(g2)
    grads.update(g4)
    grads.update(ffn1_w_up=g1["w_up"], ffn1_w_down=g1["w_down"], ln1_g=g1["ln_g"], ln1_b=g1["ln_b"],
                 ffn2_w_up=g3["w_up"], ffn2_w_down=g3["w_down"], ln3_g=g3["ln_g"], ln3_b=g3["ln_b"])
    return dx0, grads


def _local_step(x, p, target, full):
    S = x.shape[0]
    ws = [_layer_weights(full, i) for i in range(DEPTH)]
    saved = []
    h = x
    for i in range(DEPTH):
        h, sv = _layer_fwd(h, p[i], ws[i], S)
        saved.append(sv)

    def loss_fn(y, t):
        err = y - t
        return [err * (1.0 / D_MODEL)], [_colsum(err * err) * (0.5 / D_MODEL)]

    (dy,), (lsum,) = _rowwise("loss", loss_fn, [h, target], [], [(D_MODEL, F32)], [D_MODEL], S)
    loss = jnp.sum(lsum)
    layer_grads = [None] * DEPTH
    for i in reversed(range(DEPTH)):
        dy, layer_grads[i] = _layer_bwd(dy, saved[i], ws[i], S)
    grads = {}
    for name in WEIGHTS:
        grads[name] = jnp.stack([layer_grads[i][name] for i in range(DEPTH)])
    return loss, dy, grads


def kernel(x, p, ffn1_w_up, ffn1_w_down, ln1_g, ln1_b, w_in, conv_w, conv_b, lru_wa, lru_ba, lru_wx, lru_bx, lru_lambda, gla_w_g2, gla_b_g, gla_norm_g, fox_b_f, w_branch, w_out, ln2_g, ln2_b, ffn2_w_up, ffn2_w_down, ln3_g, ln3_b, ple_w_proj, ple_w_gate, ple_b_gate, ln4_g, ln4_b, loss_target, m_ffn1_w_up, m_ffn1_w_down, m_ln1_g, m_ln1_b, m_w_in, m_conv_w, m_conv_b, m_lru_wa, m_lru_ba, m_lru_wx, m_lru_bx, m_lru_lambda, m_gla_w_g2, m_gla_b_g, m_gla_norm_g, m_fox_b_f, m_w_branch, m_w_out, m_ln2_g, m_ln2_b, m_ffn2_w_up, m_ffn2_w_down, m_ln3_g, m_ln3_b, m_ple_w_proj, m_ple_w_gate, m_ple_b_gate, m_ln4_g, m_ln4_b, v_ffn1_w_up, v_ffn1_w_down, v_ln1_g, v_ln1_b, v_w_in, v_conv_w, v_conv_b, v_lru_wa, v_lru_ba, v_lru_wx, v_lru_bx, v_lru_lambda, v_gla_w_g2, v_gla_b_g, v_gla_norm_g, v_fox_b_f, v_w_branch, v_w_out, v_ln2_g, v_ln2_b, v_ffn2_w_up, v_ffn2_w_down, v_ln3_g, v_ln3_b, v_ple_w_proj, v_ple_w_gate, v_ple_b_gate, v_ln4_g, v_ln4_b):
    env = dict(locals())
    wts = {n: env[n] for n in WEIGHTS}
    ms = {n: env["m_" + n] for n in WEIGHTS}
    vs = {n: env["v_" + n] for n in WEIGHTS}
    shard_axis = dict(SHARDED)
    sharded = [n for n, _ in SHARDED]
    big = [n for n in sharded if n not in SHARDED_F32_GATHER]

    shards = [wts[n].astype(BF16) if n in big else wts[n] for n in sharded]
    gathered = _allgather_multi("gather_weights", shards)
    full = {n: wts[n] for n in REPLICATED}
    for n, g in zip(sharded, gathered):
        full[n] = _from_gathered(g, shard_axis[n])

    loss_part, grad_x, grads = _local_step(x[0], p[:, 0], loss_target[0], full)
    loss = lax.psum(loss_part, MESH_AXES)

    dest = [_dest_core_major(grads[n], shard_axis[n]) for n in sharded]
    got, kept = _sibling_swap_multi("grad_sibling_swap", dest)
    pairs = [_pair_add("grad_pair_add_" + n, _as_rows(k, 1), _as_rows(g, 1)) for n, g, k in zip(sharded, got, kept)]
    parts = _chip_all_to_all_multi("grad_chip_all_to_all", pairs)
    rep = list(REPLICATED)
    (gr,) = _allgather_multi("grad_gather_replicated", [_pack([grads[n].reshape(wts[n].shape) for n in rep], F32)])

    out = {}
    for n, gp in zip(sharded, parts):
        res = _adamw("adamw_" + n, gp, _as_rows(wts[n], 0), _as_rows(ms[n], 0), _as_rows(vs[n], 0))
        for kind, arr in zip(("grad", "delta", "new_m", "new_v"), res):
            out[kind + "_" + n] = arr.reshape(wts[n].shape)
    res = _adamw("adamw_replicated", gr, _pack([wts[n] for n in rep], F32), _pack([ms[n] for n in rep], F32),
                 _pack([vs[n] for n in rep], F32))
    shapes = [wts[n].shape for n in rep]
    for kind, buf in zip(("grad", "delta", "new_m", "new_v"), res):
        for n, arr in zip(rep, _unpack(buf, shapes)):
            out[kind + "_" + n] = arr
    return (loss, grad_x[None], *[out["grad_" + n] for n in WEIGHTS], *[out["delta_" + n] for n in WEIGHTS],
            *[out["new_m_" + n] for n in WEIGHTS], *[out["new_v_" + n] for n in WEIGHTS])
```

```python
import functools
import math

import jax
import jax.numpy as jnp
from jax import lax
from jax.experimental import pallas as pl
from jax.experimental.pallas import tpu as pltpu

F32 = jnp.float32
BF16 = jnp.bfloat16

N_DEV = 8
MESH_AXES = ("x", "y", "c")
DEPTH = 2
D_MODEL = 1024
D_FF = 2816
BRANCH = 512
CHUNK = 64
GLA_HEADS = 4
GLA_DK = 64
GLA_DV = 128
GLA_LOWRANK = 16
GLA_TAU = 16.0
FOX_HEADS = 8
FOX_DH = 64
PLE_DIM = 256
LRU_C = 8.0
LRU_BLOCKS = 8
LN_EPS = 1e-5
RMS_EPS = 1e-6
ALPHA = (2 * DEPTH) ** 0.25
LANES = 128
NEG_BIG = -1e30

ADAM_LR = 0.001
ADAM_B1 = 0.9
ADAM_B2 = 0.999
ADAM_EPS = 1e-08
ADAM_WD = 0.01
ADAM_STEP = 10

VMEM_LIMIT_BYTES = 56 * 1024 * 1024

U_GATES = 0
U_AX = 3072
U_AY = 3584
U_BQ = 4096
U_BK = 4352
U_BV = 4608
U_BR = 5120
U_CQ = 5632
U_CK = 6144
U_CV = 6656
U_BLOW = 7168
U_CF = 7296
U_WIDTH = 7680
W_IN_SEGMENTS = (
    (0, 512, U_AX), (512, 512, U_AY), (1024, 256, U_BQ), (1280, 256, U_BK), (1536, 512, U_BV),
    (2048, 16, U_BLOW), (2064, 512, U_BR), (2576, 512, U_CQ), (3088, 512, U_CK), (3600, 512, U_CV),
    (4112, 8, U_CF), (4120, 3072, U_GATES),
)

SHARDED = (
    ("ffn1_w_up", 2), ("ffn1_w_down", 1), ("w_in", 2), ("conv_w", 2), ("gla_w_g2", 2), ("w_branch", 3),
    ("w_out", 1), ("ffn2_w_up", 2), ("ffn2_w_down", 1), ("ple_w_proj", 2), ("ple_w_gate", 1),
)
SHARDED_F32_GATHER = ("conv_w", "gla_w_g2")
REPLICATED = ("ln1_g", "ln1_b", "conv_b", "lru_wa", "lru_ba", "lru_wx", "lru_bx", "lru_lambda", "gla_b_g",
              "gla_norm_g", "fox_b_f", "ln2_g", "ln2_b", "ln3_g", "ln3_b", "ple_b_gate", "ln4_g", "ln4_b")
WEIGHTS = ("ffn1_w_up", "ffn1_w_down", "ln1_g", "ln1_b", "w_in", "conv_w", "conv_b", "lru_wa", "lru_ba", "lru_wx",
           "lru_bx", "lru_lambda", "gla_w_g2", "gla_b_g", "gla_norm_g", "fox_b_f", "w_branch", "w_out", "ln2_g",
           "ln2_b", "ffn2_w_up", "ffn2_w_down", "ln3_g", "ln3_b", "ple_w_proj", "ple_w_gate", "ple_b_gate", "ln4_g",
           "ln4_b")


def _sigmoid(x):
    return 1.0 / (1.0 + jnp.exp(-x))


def _log1p_pos(e):
    return jnp.where(e < 1e-4, e * (1.0 - 0.5 * e), jnp.log(1.0 + e))


def _softplus(x):
    return jnp.maximum(x, 0.0) + _log1p_pos(jnp.exp(-jnp.abs(x)))


def _log_sigmoid(x):
    return -_softplus(-x)


def _neg_expm1(y):
    series = -y * (1.0 + y * (0.5 + y * (1.0 / 6.0 + y * (1.0 / 24.0 + y * (1.0 / 120.0)))))
    return jnp.where(y > -0.1, series, 1.0 - jnp.exp(y))


def _silu_and_grad(x):
    s = _sigmoid(x)
    return x * s, s * (1.0 + x * (1.0 - s))


_GELU_C = math.sqrt(2.0 / math.pi)


def _gelu_and_grad(x):
    inner = _GELU_C * (x + 0.044715 * x * x * x)
    t = jnp.tanh(inner)
    g = 0.5 * x * (1.0 + t)
    dg = 0.5 * (1.0 + t) + 0.5 * x * (1.0 - t * t) * _GELU_C * (1.0 + 3.0 * 0.044715 * x * x)
    return g, dg


def _ln_stats(z):
    mu = jnp.mean(z, axis=-1, keepdims=True)
    zc = z - mu
    var = jnp.mean(zc * zc, axis=-1, keepdims=True)
    rstd = lax.rsqrt(var + LN_EPS)
    return zc * rstd, rstd


def _ln_fwd(z, g, b):
    xhat, _ = _ln_stats(z)
    return xhat * g + b


def _ln_bwd(dy, z, g):
    xhat, rstd = _ln_stats(z)
    dxh = dy * g
    m1 = jnp.mean(dxh, axis=-1, keepdims=True)
    m2 = jnp.mean(dxh * xhat, axis=-1, keepdims=True)
    return rstd * (dxh - m1 - xhat * m2), xhat


def _colsum(x):
    return jnp.sum(x, axis=0, keepdims=True)


def _dot(a, b, dims):
    dn = {"nn": (((1,), (0,)), ((), ())), "nt": (((1,), (1,)), ((), ())), "tn": (((0,), (0,)), ((), ()))}[dims]
    return lax.dot_general(a.astype(BF16), b.astype(BF16), dn, preferred_element_type=F32)


def _scan_rows(a, b, length, reverse=False, seg=None):
    rows = lax.broadcasted_iota(jnp.int32, b.shape, 0)
    span = seg if seg else length
    pos = rows % span if seg else rows
    d = 1
    while d < span:
        shift = (length - d) if reverse else d
        valid = (pos < span - d) if reverse else (pos >= d)
        sb = jnp.where(valid, pltpu.roll(b, shift, 0), 0.0)
        if a is None:
            b = b + sb
        else:
            b = b + a * sb
            a = a * jnp.where(valid, pltpu.roll(a, shift, 0), 1.0)
        d *= 2
    return a, b


def _tile(dim, pref):
    if dim <= pref:
        return dim
    best = None
    t = LANES
    while t <= pref:
        if dim % t == 0:
            best = t
        t += LANES
    assert best is not None, (dim, pref)
    return best


def _full_spec(arr):
    nd = arr.ndim
    return pl.BlockSpec(arr.shape, lambda *_: (0,) * nd)


def _mm(name, dims, a_ops, b_ops, terms, n_acc, epilogue, extras, out_dtypes, M, N, K, tm=512, tn=1024, tk=1024):
    tm, tn, tk = _tile(M, tm), _tile(N, tn), _tile(K, tk)
    gm, gn, gk = M // tm, N // tn, K // tk
    if dims == "tn":
        a_spec = pl.BlockSpec((tk, tm), lambda i, j, k: (k, i))
    else:
        a_spec = pl.BlockSpec((tm, tk), lambda i, j, k: (i, k))
    if dims == "nt":
        b_spec = pl.BlockSpec((tn, tk), lambda i, j, k: (j, k))
    else:
        b_spec = pl.BlockSpec((tk, tn), lambda i, j, k: (k, j))
    e_specs, e_arrays = [], []
    for ex in extras:
        if ex[1] == "mn":
            off = ex[2]
            e_specs.append(pl.BlockSpec((tm, tn), functools.partial(lambda i, j, k, off: (i, j + off), off=off)))
        else:
            e_specs.append(pl.BlockSpec((1, tn), lambda i, j, k: (0, j)))
        e_arrays.append(ex[0])
    na, nb, ne, no = len(a_ops), len(b_ops), len(extras), len(out_dtypes)

    def body(*refs):
        a_refs = refs[:na]
        b_refs = refs[na:na + nb]
        e_refs = refs[na + nb:na + nb + ne]
        o_refs = refs[na + nb + ne:na + nb + ne + no]
        acc_refs = refs[na + nb + ne + no:]
        k = pl.program_id(2)

        @pl.when(k == 0)
        def _():
            for acc in acc_refs:
                acc[...] = jnp.zeros_like(acc)

        for r, ai, bi in terms:
            acc_refs[r][...] += _dot(a_refs[ai][...], b_refs[bi][...], dims)

        @pl.when(k == gk - 1)
        def _():
            res = epilogue([acc[...] for acc in acc_refs], *[e[...] for e in e_refs])
            for o, val in zip(o_refs, res):
                o[...] = val.astype(o.dtype)

    outs = pl.pallas_call(
        body,
        name=name,
        grid=(gm, gn, gk),
        in_specs=[a_spec] * na + [b_spec] * nb + e_specs,
        out_specs=[pl.BlockSpec((tm, tn), lambda i, j, k: (i, j))] * no,
        out_shape=[jax.ShapeDtypeStruct((M, N), dt) for dt in out_dtypes],
        scratch_shapes=[pltpu.VMEM((tm, tn), F32)] * n_acc,
        compiler_params=pltpu.CompilerParams(
            dimension_semantics=("parallel", "parallel", "arbitrary"), vmem_limit_bytes=VMEM_LIMIT_BYTES),
    )(*a_ops, *b_ops, *e_arrays)
    return outs


def _mm1(name, dims, a, b, M, N, K, out_dtype=F32, scale=None, **kw):
    def epi(accs):
        return [accs[0] if scale is None else accs[0] * scale]
    return _mm(name, dims, [a], [b], [(0, 0, 0)], 1, epi, [], [out_dtype], M, N, K, **kw)[0]


def _rowwise(name, fn, row_ins, vec_ins, row_outs, sum_outs, S, tr=256, reverse=False):
    tr = min(tr, S)
    g = S // tr
    rmap = (lambda i: (g - 1 - i)) if reverse else (lambda i: i)
    in_specs, arrays = [], []
    for r in row_ins:
        if isinstance(r, tuple):
            arr, width, blk = r
            in_specs.append(pl.BlockSpec((tr, width), functools.partial(lambda i, blk: (rmap(i), blk), blk=blk)))
        else:
            arr = r
            in_specs.append(pl.BlockSpec((tr, arr.shape[1]), lambda i: (rmap(i), 0)))
        arrays.append(arr)
    for v in vec_ins:
        in_specs.append(_full_spec(v))
        arrays.append(v)
    nr, nv, no, ns = len(row_ins), len(vec_ins), len(row_outs), len(sum_outs)

    def body(*refs):
        ins = [r[...] for r in refs[:nr + nv]]
        o_refs = refs[nr + nv:nr + nv + no]
        s_refs = refs[nr + nv + no:]
        outs, sums = fn(*ins)
        for o, val in zip(o_refs, outs):
            o[...] = val.astype(o.dtype)
        if ns:
            i = pl.program_id(0)

            @pl.when(i == 0)
            def _():
                for s, val in zip(s_refs, sums):
                    s[...] = val

            @pl.when(i > 0)
            def _():
                for s, val in zip(s_refs, sums):
                    s[...] += val

    res = pl.pallas_call(
        body,
        name=name,
        grid=(g,),
        in_specs=in_specs,
        out_specs=[pl.BlockSpec((tr, c), lambda i: (rmap(i), 0)) for c, _ in row_outs]
        + [pl.BlockSpec((1, c), lambda i: (0, 0)) for c in sum_outs],
        out_shape=[jax.ShapeDtypeStruct((S, c), dt) for c, dt in row_outs]
        + [jax.ShapeDtypeStruct((1, c), F32) for c in sum_outs],
        compiler_params=pltpu.CompilerParams(
            dimension_semantics=("arbitrary",), vmem_limit_bytes=VMEM_LIMIT_BYTES),
    )(*arrays)
    return res[:no], res[no:]


def _win(arr, offset, width):
    assert offset % width == 0
    return (arr, width, offset // width)


MESH_ID = pl.DeviceIdType.MESH


def _remote(src, dst, send_sem, recv_sem, to):
    return pltpu.make_async_remote_copy(src_ref=src, dst_ref=dst, send_sem=send_sem, recv_sem=recv_sem,
                                        device_id=to, device_id_type=MESH_ID)


def _hbm_call(name, body, arrs, out_shapes, n_send, n_recv, n_local):
    return pl.pallas_call(
        body,
        name=name,
        in_specs=[pl.BlockSpec(memory_space=pltpu.HBM)] * len(arrs),
        out_specs=[pl.BlockSpec(memory_space=pltpu.HBM)] * len(out_shapes),
        out_shape=out_shapes,
        scratch_shapes=[pltpu.SemaphoreType.DMA((n_send,)), pltpu.SemaphoreType.DMA((n_recv,)),
                        pltpu.SemaphoreType.DMA((n_local,))],
        compiler_params=pltpu.CompilerParams(has_side_effects=True),
    )(*arrs)


def _allgather_multi(name, arrs):
    n = len(arrs)

    def body(*refs):
        ins, outs = refs[:n], refs[n:2 * n]
        send_sems, recv_sems, local_sems = refs[2 * n:]
        x, y, c = lax.axis_index("x"), lax.axis_index("y"), lax.axis_index("c")
        me, sibling = (x, y, c), (x, y, 1 - c)
        chips = [(1 - x, y), (x, 1 - y), (1 - x, 1 - y)]

        def slot(i, dev):
            return outs[i].at[4 * dev[0] + 2 * dev[1] + dev[2]]

        def copy(i, k, block, to, src=None):
            dst = slot(i, block)
            return _remote(dst if src is None else src, dst, send_sems.at[7 * i + k], recv_sems.at[7 * i + k], to)

        mine = [pltpu.make_async_copy(ins[i], slot(i, me), local_sems.at[i]) for i in range(n)]
        for cp in mine:
            cp.start()
        first = []
        for i in range(n):
            first.append(copy(i, 0, me, sibling, src=ins[i]))
            first += [copy(i, 1 + j, me, (*chip, c), src=ins[i]) for j, chip in enumerate(chips)]
        for cp in first:
            cp.start()
        passed = []
        for j, chip in enumerate(chips):
            for i in range(n):
                copy(i, 1 + j, (*chip, c), me).wait_recv()
                cp = copy(i, 4 + j, (*chip, c), sibling)
                cp.start()
                passed.append(cp)
        for i in range(n):
            copy(i, 0, sibling, me).wait_recv()
            for j, chip in enumerate(chips):
                copy(i, 4 + j, (*chip, 1 - c), me).wait_recv()
        for cp in first + passed:
            cp.wait_send()
        for cp in mine:
            cp.wait()

    outs = [jax.ShapeDtypeStruct((N_DEV,) + a.shape, a.dtype) for a in arrs]
    return _hbm_call(name, body, arrs, outs, 7 * n, 7 * n, n)


def _sibling_swap_multi(name, arrs):
    n = len(arrs)
    per = 4 * DEPTH

    def body(*refs):
        ins, got = refs[:n], refs[n:2 * n]
        send_sems, recv_sems, _ = refs[2 * n:]
        x, y, c = lax.axis_index("x"), lax.axis_index("y"), lax.axis_index("c")
        sibling = (x, y, 1 - c)
        sends = []
        for i in range(n):
            for a in range(4):
                for l in range(DEPTH):
                    k = per * i + DEPTH * a + l
                    sends.append(_remote(ins[i].at[1 - c, a, l], got[i].at[a, l], send_sems.at[k], recv_sems.at[k],
                                         sibling))
        for cp in sends:
            cp.start()
        for cp in sends:
            cp.wait_recv()
        for cp in sends:
            cp.wait_send()

    outs = [jax.ShapeDtypeStruct(a.shape[1:], a.dtype) for a in arrs]
    return _hbm_call(name, body, arrs, outs, per * n, per * n, 1)


def _chip_all_to_all_multi(name, arrs):
    n = len(arrs)

    def body(*refs):
        ins, outs = refs[:n], refs[n:2 * n]
        send_sems, recv_sems, local_sems = refs[2 * n:]
        x, y, c = lax.axis_index("x"), lax.axis_index("y"), lax.axis_index("c")
        mine = 2 * x + y
        chips = [(1 - x, y), (x, 1 - y), (1 - x, 1 - y)]
        local = [pltpu.make_async_copy(ins[i].at[mine], outs[i].at[mine], local_sems.at[i]) for i in range(n)]
        for cp in local:
            cp.start()
        sends, recvs = [], []
        for i in range(n):
            for j, (px, py) in enumerate(chips):
                peer = 2 * px + py
                sems = (send_sems.at[3 * i + j], recv_sems.at[3 * i + j], (px, py, c))
                sends.append(_remote(ins[i].at[peer], outs[i].at[mine], *sems))
                recvs.append(_remote(ins[i].at[peer], outs[i].at[peer], *sems))
        for cp in sends:
            cp.start()
        for cp in recvs:
            cp.wait_recv()
        for cp in sends:
            cp.wait_send()
        for cp in local:
            cp.wait()

    outs = [jax.ShapeDtypeStruct(a.shape, a.dtype) for a in arrs]
    return _hbm_call(name, body, arrs, outs, 3 * n, 3 * n, n)


def _as_rows(a, lead):
    return a.reshape(a.shape[:lead] + (-1, a.shape[-1]))


def _row_tile(rows, cols, parts):
    budget = 4 * 1024 * 1024 // (4 * max(cols, LANES) * parts)
    return _tile_rows(rows, max(8, min(512, budget // 8 * 8)))


def _pair_add(name, core, both, got):
    _, rows, cols = got.shape
    tr = _row_tile(rows, cols, 2)

    def body(c_ref, a_ref, b_ref, o_ref):
        o_ref[...] = (a_ref[...] + b_ref[...]).astype(o_ref.dtype)

    blk = pl.BlockSpec((1, tr, cols), lambda ch, i, c_ref: (ch, i, 0))
    return pl.pallas_call(
        body, name=name,
        grid_spec=pltpu.PrefetchScalarGridSpec(
            num_scalar_prefetch=1, grid=(4, rows // tr),
            in_specs=[pl.BlockSpec((None, 1, tr, cols), lambda ch, i, c_ref: (c_ref[0], ch, i, 0)), blk],
            out_specs=blk),
        out_shape=jax.ShapeDtypeStruct(got.shape, BF16),
        compiler_params=pltpu.CompilerParams(dimension_semantics=("parallel", "parallel"),
                                             vmem_limit_bytes=VMEM_LIMIT_BYTES),
    )(core, both, got)


def _adamw(name, gparts, w, m, v):
    parts, rows, cols = gparts.shape
    tr = _row_tile(rows, cols, parts)
    c1 = 1.0 / (1.0 - ADAM_B1 ** ADAM_STEP)
    c2 = 1.0 / (1.0 - ADAM_B2 ** ADAM_STEP)

    def body(gp_ref, w_ref, m_ref, v_ref, g_ref, d_ref, nm_ref, nv_ref):
        g = gp_ref[0].astype(F32)
        for i in range(1, parts):
            g = g + gp_ref[i].astype(F32)
        nm = ADAM_B1 * m_ref[...] + (1.0 - ADAM_B1) * g
        nv = ADAM_B2 * v_ref[...] + (1.0 - ADAM_B2) * (g * g)
        m_hat = nm * c1
        v_hat = nv * c2
        g_ref[...] = g
        nm_ref[...] = nm
        nv_ref[...] = nv
        d_ref[...] = -ADAM_LR * (m_hat / (jnp.sqrt(v_hat) + ADAM_EPS) + ADAM_WD * w_ref[...])

    row = pl.BlockSpec((tr, cols), lambda i: (i, 0))
    return pl.pallas_call(
        body,
        name=name,
        grid=(rows // tr,),
        in_specs=[pl.BlockSpec((parts, tr, cols), lambda i: (0, i, 0)), row, row, row],
        out_specs=[row] * 4,
        out_shape=[jax.ShapeDtypeStruct((rows, cols), F32)] * 4,
        compiler_params=pltpu.CompilerParams(dimension_semantics=("parallel",), vmem_limit_bytes=VMEM_LIMIT_BYTES),
    )(gparts, w, m, v)


def _tile_rows(rows, pref):
    t = min(pref, rows)
    while rows % t:
        t -= 8
    return t


PACK_ROWS = 512


def _pack(arrs, dtype):
    flat = jnp.concatenate([a.astype(dtype).reshape(-1) for a in arrs])
    quantum = PACK_ROWS * LANES
    padded = -(-flat.shape[0] // quantum) * quantum
    return jnp.pad(flat, (0, padded - flat.shape[0])).reshape(-1, LANES)


def _unpack(buf, shapes, lead=()):
    flat = buf.reshape(lead + (-1,))
    out, off = [], 0
    for shp in shapes:
        n = math.prod(shp)
        out.append(flat[..., off:off + n].reshape(lead + tuple(shp)))
        off += n
    return out


def _dest_core_major(full, axis):
    shp = full.shape
    n = shp[axis] // N_DEV
    r = full.reshape(shp[:axis] + (4, 2, n) + shp[axis + 1:])
    return jnp.moveaxis(r, (axis + 1, axis), (0, 1))


def _from_gathered(g, axis):
    r = jnp.moveaxis(g, 0, axis)
    shp = r.shape
    return r.reshape(shp[:axis] + (shp[axis] * shp[axis + 1],) + shp[axis + 2:])


def _shift_down(a, k):
    return jnp.pad(a, ((k, 0), (0, 0)))[:a.shape[0]] if k else a


def _shift_up(a, k, fill=0.0):
    return jnp.pad(a, ((0, k), (0, 0)), constant_values=fill)[k:] if k else a


def _lru_fwd(u, ax_shift, lw, S):
    T = min(256, S)
    nb = S // T
    row = pl.BlockSpec((T, BRANCH), lambda t: (t, 0))
    vecs = [lw["cw0"], lw["cw1"], lw["cw2"], lw["cw3"], lw["conv_b"], lw["wa"], lw["wx"], lw["ba"], lw["bx"],
            lw["lam"]]

    def body(ax0, ax1, ax2, ax3, ay, cw0, cw1, cw2, cw3, cb, wa, wx, ba, bx, lam, xc_o, r_o, i_o, a_o, h_o, ya_o, hc):
        t = pl.program_id(0)

        @pl.when(t == 0)
        def _():
            hc[...] = jnp.zeros_like(hc)

        xc = cw3[...] * ax0[...] + cw2[...] * ax1[...] + cw1[...] * ax2[...] + cw0[...] * ax3[...] + cb[...]
        r = _sigmoid(_dot(xc, wa[...], "nn") + ba[...])
        gi = _sigmoid(_dot(xc, wx[...], "nn") + bx[...])
        sp = _softplus(-lam[...])
        la = -LRU_C * r * sp
        a = jnp.exp(la)
        mult = jnp.sqrt(_neg_expm1(2.0 * la))
        A, B = _scan_rows(a, mult * gi * xc, T)
        h = B + A * hc[...]
        h_o[...] = h
        hc[...] = h_o[pl.ds(T - 1, 1), :]
        xc_o[...] = xc
        r_o[...] = r
        i_o[...] = gi
        a_o[...] = a
        gy, _ = _gelu_and_grad(ay[...])
        ya_o[...] = (gy * h).astype(ya_o.dtype)

    outs = pl.pallas_call(
        body,
        name="lru_fwd",
        grid=(nb,),
        in_specs=[pl.BlockSpec((T, BRANCH), lambda t: (t, U_AX // BRANCH))] + [row] * 3
        + [pl.BlockSpec((T, BRANCH), lambda t: (t, U_AY // BRANCH))] + [_full_spec(v) for v in vecs],
        out_specs=[row] * 6,
        out_shape=[jax.ShapeDtypeStruct((S, BRANCH), F32)] * 5 + [jax.ShapeDtypeStruct((S, BRANCH), BF16)],
        scratch_shapes=[pltpu.VMEM((1, BRANCH), F32)],
        compiler_params=pltpu.CompilerParams(dimension_semantics=("arbitrary",), vmem_limit_bytes=VMEM_LIMIT_BYTES),
    )(u, *ax_shift[1:], u, *vecs)
    return outs


def _lru_bwd(dya, u, sv, lw, S):
    T = min(256, S)
    nb = S // T
    rrow = pl.BlockSpec((T, BRANCH), lambda t: (nb - 1 - t, 0))
    sq = pl.BlockSpec((BRANCH, BRANCH), lambda t: (0, 0))
    vrow = pl.BlockSpec((1, BRANCH), lambda t: (0, 0))
    h_prev = _shift_down(sv["h"], 1)
    a_next = _shift_up(sv["a"], 1)

    def body(dya_r, ay, h, hp, xc_r, r_r, i_r, a_r, an, wa, wx, lam,
             day_o, dxc_o, dwa_o, dwx_o, dba_o, dbx_o, dlam_o, lcar, tmp):
        t = pl.program_id(0)

        @pl.when(t == 0)
        def _():
            lcar[...] = jnp.zeros_like(lcar)
            dwa_o[...] = jnp.zeros_like(dwa_o)
            dwx_o[...] = jnp.zeros_like(dwx_o)
            dba_o[...] = jnp.zeros_like(dba_o)
            dbx_o[...] = jnp.zeros_like(dbx_o)
            dlam_o[...] = jnp.zeros_like(dlam_o)

        gy, dgy = _gelu_and_grad(ay[...])
        dy = dya_r[...]
        day_o[...] = (dy * h[...] * dgy).astype(day_o.dtype)
        A, B = _scan_rows(an[...], dy * gy, T, reverse=True)
        lmb = B + A * lcar[...]
        tmp[...] = lmb
        lcar[...] = tmp[pl.ds(0, 1), :]
        xc, r, gi, a = xc_r[...], r_r[...], i_r[...], a_r[...]
        sp = _softplus(-lam[...])
        la = -LRU_C * r * sp
        mult = jnp.sqrt(_neg_expm1(2.0 * la))
        da = lmb * hp[...]
        dmult = lmb * gi * xc
        di = lmb * mult * xc
        dxc = lmb * mult * gi
        dla = da * a - dmult * a * a / mult
        dr = dla * (-LRU_C * sp)
        dlam_o[...] += _colsum(dla * (LRU_C * r)) * _sigmoid(-lam[...])
        dpr = dr * r * (1.0 - r)
        dpi = di * gi * (1.0 - gi)
        dba_o[...] += _colsum(dpr)
        dbx_o[...] += _colsum(dpi)
        dxc_o[...] = dxc + _dot(dpr, wa[...], "nt") + _dot(dpi, wx[...], "nt")
        dwa_o[...] += _dot(xc, dpr, "tn")
        dwx_o[...] += _dot(xc, dpi, "tn")

    outs = pl.pallas_call(
        body,
        name="lru_bwd",
        grid=(nb,),
        in_specs=[rrow, pl.BlockSpec((T, BRANCH), lambda t: (nb - 1 - t, U_AY // BRANCH))] + [rrow] * 7
        + [sq, sq, vrow],
        out_specs=[rrow, rrow, sq, sq, vrow, vrow, vrow],
        out_shape=[jax.ShapeDtypeStruct((S, BRANCH), BF16), jax.ShapeDtypeStruct((S, BRANCH), F32),
                   jax.ShapeDtypeStruct((BRANCH, BRANCH), F32), jax.ShapeDtypeStruct((BRANCH, BRANCH), F32),
                   jax.ShapeDtypeStruct((1, BRANCH), F32), jax.ShapeDtypeStruct((1, BRANCH), F32),
                   jax.ShapeDtypeStruct((1, BRANCH), F32)],
        scratch_shapes=[pltpu.VMEM((1, BRANCH), F32), pltpu.VMEM((T, BRANCH), F32)],
        compiler_params=pltpu.CompilerParams(dimension_semantics=("arbitrary",), vmem_limit_bytes=VMEM_LIMIT_BYTES),
    )(dya, u, sv["h"], h_prev, sv["xc"], sv["r"], sv["i"], sv["a"], a_next, lw["wa"], lw["wx"], lw["lam"])
    return outs


def _conv_bwd(dxc, ax_shift, lw, S):
    dxs = [_shift_up(dxc, k) for k in range(4)]

    def fn(d0, d1, d2, d3, a0, a1, a2, a3, cw0, cw1, cw2, cw3):
        dax = cw3 * d0 + cw2 * d1 + cw1 * d2 + cw0 * d3
        return [dax], [_colsum(d0 * a3), _colsum(d0 * a2), _colsum(d0 * a1), _colsum(d0 * a0), _colsum(d0)]

    (dax,), sums = _rowwise("conv_bwd", fn, dxs + list(ax_shift), [lw["cw0"], lw["cw1"], lw["cw2"], lw["cw3"]],
                            [(BRANCH, BF16)], [BRANCH] * 5, S)
    return dax, sums


GLA_QK = GLA_HEADS * GLA_DK
GLA_V = GLA_HEADS * GLA_DV
GLA_SCALE = GLA_DK ** -0.5


def _gla_specs(TB, rev_nb=None):
    def rmap(t):
        return t if rev_nb is None else rev_nb - 1 - t
    return [
        pl.BlockSpec((TB, GLA_QK), lambda t: (rmap(t), U_BQ // GLA_QK)),
        pl.BlockSpec((TB, GLA_QK), lambda t: (rmap(t), U_BK // GLA_QK)),
        pl.BlockSpec((TB, GLA_V), lambda t: (rmap(t), U_BV // GLA_V)),
        pl.BlockSpec((TB, GLA_V), lambda t: (rmap(t), U_BR // GLA_V)),
        pl.BlockSpec((TB, LANES), lambda t: (rmap(t), U_BLOW // LANES)),
    ]


def _gla_gates(gl, wg2, bg, TB):
    pre = _dot(gl, wg2, "nn") + bg
    la = _log_sigmoid(pre) * (1.0 / GLA_TAU)
    _, gc = _scan_rows(None, la, TB, seg=CHUNK)
    return pre, la, gc


def _gla_fwd(u, gw, S):
    TB = min(512, S)
    nb = S // TB
    cpb = TB // CHUNK
    vecs = [gw["wg2"], gw["bg"], gw["ng"], gw["bd"]]

    def body(q_r, k_r, v_r, br_r, gl_r, wg2, bg, ng, bd, yb_o, oraw_o, st_o, st):
        t = pl.program_id(0)

        @pl.when(t == 0)
        def _():
            st[...] = jnp.zeros_like(st)

        _, la, gc = _gla_gates(gl_r[...], wg2[...], bg[...], TB)
        for c in range(cpb):
            sl = slice(c * CHUNK, (c + 1) * CHUNK)
            gt = _colsum(la[sl])
            kdec = k_r[sl, :] * jnp.exp(gt - gc[sl])
            d_t = _dot(v_r[sl, :], kdec, "tn") * bd[...]
            s_new = st[...] * jnp.exp(gt) + d_t
            st[...] = s_new
            st_o[c] = s_new
            oraw_o[sl, :] = _dot(q_r[sl, :] * GLA_SCALE, s_new, "nt")
        for h in range(GLA_HEADS):
            hs = slice(h * GLA_DV, (h + 1) * GLA_DV)
            oh = oraw_o[:, hs]
            on = oh * lax.rsqrt(jnp.mean(oh * oh, axis=-1, keepdims=True) + RMS_EPS)
            sil, _ = _silu_and_grad(br_r[:, hs])
            yb_o[:, hs] = (on * ng[:, hs] * sil).astype(yb_o.dtype)

    return pl.pallas_call(
        body,
        name="gla_fwd",
        grid=(nb,),
        in_specs=_gla_specs(TB) + [_full_spec(v) for v in vecs],
        out_specs=[pl.BlockSpec((TB, GLA_V), lambda t: (t, 0)), pl.BlockSpec((TB, GLA_V), lambda t: (t, 0)),
                   pl.BlockSpec((cpb, GLA_V, GLA_QK), lambda t: (t, 0, 0))],
        out_shape=[jax.ShapeDtypeStruct((S, GLA_V), BF16), jax.ShapeDtypeStruct((S, GLA_V), F32),
                   jax.ShapeDtypeStruct((S // CHUNK, GLA_V, GLA_QK), F32)],
        scratch_shapes=[pltpu.VMEM((GLA_V, GLA_QK), F32)],
        compiler_params=pltpu.CompilerParams(dimension_semantics=("arbitrary",), vmem_limit_bytes=VMEM_LIMIT_BYTES),
    )(u, u, u, u, u, *vecs)


def _gla_bwd(dyb, u, oraw, states, gw, S):
    TB = min(512, S)
    nb = S // TB
    cpb = TB // CHUNK
    vecs = [gw["wg2"], gw["bg"], gw["ng"], gw["bd"]]

    def rrow(width):
        return pl.BlockSpec((TB, width), lambda t: (nb - 1 - t, 0))

    def body(dyb_r, oraw_r, q_r, k_r, v_r, br_r, gl_r, st_r, sp_r, wg2, bg, ng, bd,
             dq_o, dk_o, dv_o, dbr_o, dgl_o, dwg2_o, dbg_o, dng_o, dcar, do_buf, dla_buf):
        t = pl.program_id(0)
        blk = nb - 1 - t

        @pl.when(t == 0)
        def _():
            dcar[...] = jnp.zeros_like(dcar)
            dwg2_o[...] = jnp.zeros_like(dwg2_o)
            dbg_o[...] = jnp.zeros_like(dbg_o)
            dng_o[...] = jnp.zeros_like(dng_o)

        pre, la, gc = _gla_gates(gl_r[...], wg2[...], bg[...], TB)
        for h in range(GLA_HEADS):
            hs = slice(h * GLA_DV, (h + 1) * GLA_DV)
            oh = oraw_r[:, hs]
            rs = lax.rsqrt(jnp.mean(oh * oh, axis=-1, keepdims=True) + RMS_EPS)
            on = oh * rs
            sil, dsil = _silu_and_grad(br_r[:, hs])
            dy = dyb_r[:, hs]
            dbr_o[:, hs] = (dy * on * ng[:, hs] * dsil).astype(dbr_o.dtype)
            don = dy * ng[:, hs] * sil
            dng_o[:, hs] += _colsum(dy * on * sil)
            do_buf[:, hs] = rs * (don - on * jnp.mean(don * on, axis=-1, keepdims=True))
        first = jnp.where(blk == 0, 0.0, 1.0)
        for c in reversed(range(cpb)):
            sl = slice(c * CHUNK, (c + 1) * CHUNK)
            s_n = st_r[c]
            s_prev = st_r[c - 1] if c > 0 else sp_r[0] * first
            gt = _colsum(la[sl])
            w = jnp.exp(gt - gc[sl])
            k_c = k_r[sl, :]
            kdec = k_c * w
            qs = q_r[sl, :] * GLA_SCALE
            do_c = do_buf[sl, :]
            dq_o[sl, :] = (_dot(do_c, s_n, "nn") * GLA_SCALE).astype(dq_o.dtype)
            d_n = _dot(do_c, qs, "tn") * bd[...] + dcar[...]
            dv_o[sl, :] = _dot(kdec, d_n, "nt").astype(dv_o.dtype)
            dkdec = _dot(v_r[sl, :], d_n, "nn")
            dk_o[sl, :] = (dkdec * w).astype(dk_o.dtype)
            tt = dkdec * kdec
            e = jnp.exp(gt)
            dgt = _colsum(tt) + _colsum(d_n * s_prev) * e
            _, rc = _scan_rows(None, -tt, CHUNK, reverse=True)
            dla_buf[sl, :] = rc + dgt
            dcar[...] = d_n * e
        dpre = dla_buf[...] * _sigmoid(-pre) * (1.0 / GLA_TAU)
        dbg_o[...] += _colsum(dpre)
        dgl_o[...] = _dot(dpre, wg2[...], "nt").astype(dgl_o.dtype)
        dwg2_o[...] += _dot(gl_r[...], dpre, "tn")

    return pl.pallas_call(
        body,
        name="gla_bwd",
        grid=(nb,),
        in_specs=[rrow(GLA_V), rrow(GLA_V)] + _gla_specs(TB, rev_nb=nb)
        + [pl.BlockSpec((cpb, GLA_V, GLA_QK), lambda t: (nb - 1 - t, 0, 0)),
           pl.BlockSpec((1, GLA_V, GLA_QK), lambda t: (jnp.maximum((nb - 1 - t) * cpb - 1, 0), 0, 0))]
        + [_full_spec(v) for v in vecs],
        out_specs=[rrow(GLA_QK), rrow(GLA_QK), rrow(GLA_V), rrow(GLA_V), rrow(LANES),
                   pl.BlockSpec((LANES, GLA_QK), lambda t: (0, 0)), pl.BlockSpec((1, GLA_QK), lambda t: (0, 0)),
                   pl.BlockSpec((1, GLA_V), lambda t: (0, 0))],
        out_shape=[jax.ShapeDtypeStruct((S, GLA_QK), BF16), jax.ShapeDtypeStruct((S, GLA_QK), BF16),
                   jax.ShapeDtypeStruct((S, GLA_V), BF16), jax.ShapeDtypeStruct((S, GLA_V), BF16),
                   jax.ShapeDtypeStruct((S, LANES), BF16), jax.ShapeDtypeStruct((LANES, GLA_QK), F32),
                   jax.ShapeDtypeStruct((1, GLA_QK), F32), jax.ShapeDtypeStruct((1, GLA_V), F32)],
        scratch_shapes=[pltpu.VMEM((GLA_V, GLA_QK), F32), pltpu.VMEM((TB, GLA_V), F32),
                        pltpu.VMEM((TB, GLA_QK), F32)],
        compiler_params=pltpu.CompilerParams(dimension_semantics=("arbitrary",), vmem_limit_bytes=VMEM_LIMIT_BYTES),
    )(dyb, oraw, u, u, u, u, u, states, states, *vecs)


FOX_SCALE = FOX_DH ** -0.5


def _fox_gate_fwd(u, bfp, S):
    T = min(512, S)

    def body(f_r, b_r, fc_o, car):
        t = pl.program_id(0)

        @pl.when(t == 0)
        def _():
            car[...] = jnp.zeros_like(car)

        _, cs = _scan_rows(None, _log_sigmoid(f_r[...] + b_r[...]), T)
        fc_o[...] = cs + car[...]
        car[...] = fc_o[pl.ds(T - 1, 1), :]

    return pl.pallas_call(
        body,
        name="fox_gate_fwd",
        grid=(S // T,),
        in_specs=[pl.BlockSpec((T, LANES), lambda t: (t, U_CF // LANES)), _full_spec(bfp)],
        out_specs=pl.BlockSpec((T, LANES), lambda t: (t, 0)),
        out_shape=jax.ShapeDtypeStruct((S, LANES), F32),
        scratch_shapes=[pltpu.VMEM((1, LANES), F32)],
        compiler_params=pltpu.CompilerParams(dimension_semantics=("arbitrary",), vmem_limit_bytes=VMEM_LIMIT_BYTES),
    )(u, bfp)


def _fox_gate_bwd(dfc, u, bfp, S):
    T = min(512, S)
    nb = S // T

    def body(d_r, f_r, b_r, df_o, db_o, car, tmp):
        t = pl.program_id(0)

        @pl.when(t == 0)
        def _():
            car[...] = jnp.zeros_like(car)
            db_o[...] = jnp.zeros_like(db_o)

        _, rc = _scan_rows(None, d_r[...], T, reverse=True)
        tmp[...] = rc + car[...]
        car[...] = tmp[pl.ds(0, 1), :]
        df = tmp[...] * _sigmoid(-(f_r[...] + b_r[...]))
        df_o[...] = df.astype(df_o.dtype)
        db_o[...] += _colsum(df)

    return pl.pallas_call(
        body,
        name="fox_gate_bwd",
        grid=(nb,),
        in_specs=[pl.BlockSpec((T, LANES), lambda t: (nb - 1 - t, 0)),
                  pl.BlockSpec((T, LANES), lambda t: (nb - 1 - t, U_CF // LANES)), _full_spec(bfp)],
        out_specs=[pl.BlockSpec((T, LANES), lambda t: (nb - 1 - t, 0)), pl.BlockSpec((1, LANES), lambda t: (0, 0))],
        out_shape=[jax.ShapeDtypeStruct((S, LANES), BF16), jax.ShapeDtypeStruct((1, LANES), F32)],
        scratch_shapes=[pltpu.VMEM((1, LANES), F32), pltpu.VMEM((T, LANES), F32)],
        compiler_params=pltpu.CompilerParams(dimension_semantics=("arbitrary",), vmem_limit_bytes=VMEM_LIMIT_BYTES),
    )(dfc, u, bfp)


def _fox_scores(q, k, fq, fk, qi, ki, tq, tk):
    s = _dot(q, k, "nt") * FOX_SCALE + (fq - fk)
    rows = lax.broadcasted_iota(jnp.int32, (tq, tk), 0) + qi * tq
    cols = lax.broadcasted_iota(jnp.int32, (tq, tk), 1) + ki * tk
    return jnp.where(cols <= rows, s, NEG_BIG)


def _fox_fwd(qh, kh, vh, fq, fk, S):
    tq = tk = min(512, S)
    nq, nk = S // tq, S // tk

    def body(q_r, k_r, v_r, fq_r, fk_r, o_o, lse_o, m_s, l_s, acc):
        qi, ki = pl.program_id(1), pl.program_id(2)

        @pl.when(ki == 0)
        def _():
            m_s[...] = jnp.full_like(m_s, NEG_BIG)
            l_s[...] = jnp.zeros_like(l_s)
            acc[...] = jnp.zeros_like(acc)

        @pl.when(ki <= qi)
        def _():
            s = _fox_scores(q_r[0], k_r[0], fq_r[0], fk_r[0], qi, ki, tq, tk)
            m_new = jnp.maximum(m_s[...], jnp.max(s, axis=-1, keepdims=True))
            p = jnp.exp(s - m_new)
            alpha = jnp.exp(m_s[...] - m_new)
            l_s[...] = alpha * l_s[...] + jnp.sum(p, axis=-1, keepdims=True)
            acc[...] = alpha * acc[...] + _dot(p, v_r[0], "nn")
            m_s[...] = m_new

        @pl.when(ki == nk - 1)
        def _():
            o_o[0] = acc[...] / l_s[...]
            lse_o[0] = m_s[...] + jnp.log(l_s[...])

    kv = pl.BlockSpec((1, tk, FOX_DH), lambda h, i, j: (h, jnp.minimum(j, i), 0))
    return pl.pallas_call(
        body,
        name="fox_fwd",
        grid=(FOX_HEADS, nq, nk),
        in_specs=[pl.BlockSpec((1, tq, FOX_DH), lambda h, i, j: (h, i, 0)), kv, kv,
                  pl.BlockSpec((1, tq, 1), lambda h, i, j: (h, i, 0)),
                  pl.BlockSpec((1, 1, tk), lambda h, i, j: (h, 0, jnp.minimum(j, i)))],
        out_specs=[pl.BlockSpec((1, tq, FOX_DH), lambda h, i, j: (h, i, 0)),
                   pl.BlockSpec((1, tq, 1), lambda h, i, j: (h, i, 0))],
        out_shape=[jax.ShapeDtypeStruct((FOX_HEADS, S, FOX_DH), F32), jax.ShapeDtypeStruct((FOX_HEADS, S, 1), F32)],
        scratch_shapes=[pltpu.VMEM((tq, 1), F32), pltpu.VMEM((tq, 1), F32), pltpu.VMEM((tq, FOX_DH), F32)],
        compiler_params=pltpu.CompilerParams(
            dimension_semantics=("parallel", "parallel", "arbitrary"), vmem_limit_bytes=VMEM_LIMIT_BYTES),
    )(qh, kh, vh, fq, fk)


def _fox_bwd_dq(qh, kh, vh, fq, fk, o, do, lse, S):
    tq = tk = min(512, S)
    nq, nk = S // tq, S // tk

    def body(q_r, k_r, v_r, fq_r, fk_r, o_r, do_r, lse_r, dq_o, dfq_o, dq_acc, df_acc):
        qi, ki = pl.program_id(1), pl.program_id(2)

        @pl.when(ki == 0)
        def _():
            dq_acc[...] = jnp.zeros_like(dq_acc)
            df_acc[...] = jnp.zeros_like(df_acc)

        @pl.when(ki <= qi)
        def _():
            s = _fox_scores(q_r[0], k_r[0], fq_r[0], fk_r[0], qi, ki, tq, tk)
            p = jnp.exp(s - lse_r[0])
            do_t = do_r[0]
            delta = jnp.sum(o_r[0] * do_t, axis=-1, keepdims=True)
            ds = p * (_dot(do_t, v_r[0], "nt") - delta)
            dq_acc[...] += _dot(ds, k_r[0], "nn")
            df_acc[...] += jnp.sum(ds, axis=-1, keepdims=True)

        @pl.when(ki == nk - 1)
        def _():
            dq_o[0] = dq_acc[...] * FOX_SCALE
            dfq_o[0] = df_acc[...]

    qrow = pl.BlockSpec((1, tq, FOX_DH), lambda h, i, j: (h, i, 0))
    qcol = pl.BlockSpec((1, tq, 1), lambda h, i, j: (h, i, 0))
    kv = pl.BlockSpec((1, tk, FOX_DH), lambda h, i, j: (h, jnp.minimum(j, i), 0))
    return pl.pallas_call(
        body,
        name="fox_bwd_dq",
        grid=(FOX_HEADS, nq, nk),
        in_specs=[qrow, kv, kv, qcol, pl.BlockSpec((1, 1, tk), lambda h, i, j: (h, 0, jnp.minimum(j, i))),
                  qrow, qrow, qcol],
        out_specs=[qrow, qcol],
        out_shape=[jax.ShapeDtypeStruct((FOX_HEADS, S, FOX_DH), F32), jax.ShapeDtypeStruct((FOX_HEADS, S, 1), F32)],
        scratch_shapes=[pltpu.VMEM((tq, FOX_DH), F32), pltpu.VMEM((tq, 1), F32)],
        compiler_params=pltpu.CompilerParams(
            dimension_semantics=("parallel", "parallel", "arbitrary"), vmem_limit_bytes=VMEM_LIMIT_BYTES),
    )(qh, kh, vh, fq, fk, o, do, lse)


def _fox_bwd_dkv(qh, kh, vh, fq, fk, o, do, lse, S):
    tq = tk = min(512, S)
    nq, nk = S // tq, S // tk

    def body(q_r, k_r, v_r, fq_r, fk_r, o_r, do_r, lse_r, dk_o, dv_o, dfk_o, dk_acc, dv_acc, df_acc):
        ki, qi = pl.program_id(1), pl.program_id(2)

        @pl.when(qi == 0)
        def _():
            dk_acc[...] = jnp.zeros_like(dk_acc)
            dv_acc[...] = jnp.zeros_like(dv_acc)
            df_acc[...] = jnp.zeros_like(df_acc)

        @pl.when(qi >= ki)
        def _():
            s = _fox_scores(q_r[0], k_r[0], fq_r[0], fk_r[0], qi, ki, tq, tk)
            p = jnp.exp(s - lse_r[0])
            do_t = do_r[0]
            delta = jnp.sum(o_r[0] * do_t, axis=-1, keepdims=True)
            ds = p * (_dot(do_t, v_r[0], "nt") - delta)
            dv_acc[...] += _dot(p, do_t, "tn")
            dk_acc[...] += _dot(ds, q_r[0], "tn")
            df_acc[...] += _colsum(ds)

        @pl.when(qi == nq - 1)
        def _():
            dk_o[0] = dk_acc[...] * FOX_SCALE
            dv_o[0] = dv_acc[...]
            dfk_o[0] = df_acc[...]

    qrow = pl.BlockSpec((1, tq, FOX_DH), lambda h, j, i: (h, jnp.maximum(i, j), 0))
    qcol = pl.BlockSpec((1, tq, 1), lambda h, j, i: (h, jnp.maximum(i, j), 0))
    kv = pl.BlockSpec((1, tk, FOX_DH), lambda h, j, i: (h, j, 0))
    krow = pl.BlockSpec((1, 1, tk), lambda h, j, i: (h, 0, j))
    return pl.pallas_call(
        body,
        name="fox_bwd_dkv",
        grid=(FOX_HEADS, nk, nq),
        in_specs=[qrow, kv, kv, qcol, krow, qrow, qrow, qcol],
        out_specs=[kv, kv, krow],
        out_shape=[jax.ShapeDtypeStruct((FOX_HEADS, S, FOX_DH), F32), jax.ShapeDtypeStruct((FOX_HEADS, S, FOX_DH), F32),
                   jax.ShapeDtypeStruct((FOX_HEADS, 1, S), F32)],
        scratch_shapes=[pltpu.VMEM((tk, FOX_DH), F32), pltpu.VMEM((tk, FOX_DH), F32), pltpu.VMEM((1, tk), F32)],
        compiler_params=pltpu.CompilerParams(
            dimension_semantics=("parallel", "parallel", "arbitrary"), vmem_limit_bytes=VMEM_LIMIT_BYTES),
    )(qh, kh, vh, fq, fk, o, do, lse)


FOX_TILE = 1024
FOX_AUG = 128
FOX_ONES = 3


def _fox_pairs(n, by_key):
    pairs = [(qi, ki) for qi in range(n) for ki in range(qi + 1)]
    if by_key:
        pairs.sort(key=lambda qk: (qk[1], qk[0]))
    qs = jnp.asarray([qk[0] for qk in pairs], jnp.int32)
    ks = jnp.asarray([qk[1] for qk in pairs], jnp.int32)
    return qs, ks


def _fox_augment(q, k, fcum):
    S = q.shape[0]
    def to_bf16_grid(a):
        return lax.reduce_precision(a, exponent_bits=8, mantissa_bits=7)

    hi = to_bf16_grid(fcum)
    mid = to_bf16_grid(fcum - hi)
    lo = to_bf16_grid(fcum - hi - mid)
    f3 = jnp.stack([hi, mid, lo], axis=-1).astype(BF16)
    ones = jnp.ones((S, FOX_HEADS, FOX_ONES), BF16)
    pad = jnp.zeros((S, FOX_HEADS, FOX_AUG - FOX_DH - 2 * FOX_ONES), BF16)
    q_aug = jnp.concatenate([(q * FOX_SCALE).astype(BF16), ones, f3, pad], axis=-1)
    k_aug = jnp.concatenate([k.astype(BF16), -f3, ones, pad], axis=-1)
    return jnp.transpose(q_aug, (1, 0, 2)), jnp.transpose(k_aug, (1, 0, 2))


def _fox_causal(sT):
    keys = lax.broadcasted_iota(jnp.int32, sT.shape, 0)
    queries = lax.broadcasted_iota(jnp.int32, sT.shape, 1)
    return jnp.where(keys <= queries, sT, NEG_BIG)


def _fox_fwd(qT, ka, vT, S):
    t = min(FOX_TILE, S)
    n = S // t
    qi_tab, ki_tab = _fox_pairs(n, by_key=False)

    def body(qi_ref, ki_ref, qT_r, ka_r, vT_r, oT_o, lse_o, m_s, l_s, acc):
        step = pl.program_id(1)
        qi, ki = qi_ref[step], ki_ref[step]

        @pl.when(ki == 0)
        def _():
            m_s[...] = jnp.full_like(m_s, NEG_BIG)
            l_s[...] = jnp.zeros_like(l_s)
            acc[...] = jnp.zeros_like(acc)

        def update(masked):
            sT = _dot(ka_r[0], qT_r[0], "nn")
            if masked:
                sT = _fox_causal(sT)
            m_new = jnp.maximum(m_s[...], jnp.max(sT, axis=0, keepdims=True))
            p = jnp.exp(sT - m_new)
            alpha = jnp.exp(m_s[...] - m_new)
            l_s[...] = alpha * l_s[...] + jnp.sum(p, axis=0, keepdims=True)
            acc[...] = alpha * acc[...] + _dot(vT_r[0], p, "nn")
            m_s[...] = m_new

        @pl.when(ki < qi)
        def _():
            update(False)

        @pl.when(ki == qi)
        def _():
            update(True)
            oT_o[0] = acc[...] / l_s[...]
            lse_o[0] = m_s[...] + jnp.log(l_s[...])

    return pl.pallas_call(
        body,
        name="fox_fwd",
        grid_spec=pltpu.PrefetchScalarGridSpec(
            num_scalar_prefetch=2, grid=(FOX_HEADS, int(qi_tab.shape[0])),
            in_specs=[pl.BlockSpec((1, FOX_AUG, t), lambda h, s, qt, kt: (h, 0, qt[s])),
                      pl.BlockSpec((1, t, FOX_AUG), lambda h, s, qt, kt: (h, kt[s], 0)),
                      pl.BlockSpec((1, FOX_DH, t), lambda h, s, qt, kt: (h, 0, kt[s]))],
            out_specs=[pl.BlockSpec((1, FOX_DH, t), lambda h, s, qt, kt: (h, 0, qt[s])),
                       pl.BlockSpec((1, 1, t), lambda h, s, qt, kt: (h, 0, qt[s]))],
            scratch_shapes=[pltpu.VMEM((1, t), F32), pltpu.VMEM((1, t), F32), pltpu.VMEM((FOX_DH, t), F32)]),
        out_shape=[jax.ShapeDtypeStruct((FOX_HEADS, FOX_DH, S), F32), jax.ShapeDtypeStruct((FOX_HEADS, 1, S), F32)],
        compiler_params=pltpu.CompilerParams(
            dimension_semantics=("parallel", "arbitrary"), vmem_limit_bytes=VMEM_LIMIT_BYTES),
    )(qi_tab, ki_tab, qT, ka, vT)


def _fox_bwd(qT, qa, ka, kT, v, do, doT, oT, lse, S):
    t = min(FOX_TILE, S)
    n = S // t
    qi_tab, ki_tab = _fox_pairs(n, by_key=True)

    def body(qi_ref, ki_ref, qT_r, qa_r, ka_r, kT_r, v_r, do_r, doT_r, oT_r, lse_r, dqT_o, dk_o, dv_o, dk_acc, dv_acc):
        step = pl.program_id(1)
        qi, ki = qi_ref[step], ki_ref[step]

        @pl.when(step == 0)
        def _():
            dqT_o[...] = jnp.zeros_like(dqT_o)

        @pl.when(qi == ki)
        def _():
            dk_acc[...] = jnp.zeros_like(dk_acc)
            dv_acc[...] = jnp.zeros_like(dv_acc)

        def update(masked):
            sT = _dot(ka_r[0], qT_r[0], "nn")
            if masked:
                sT = _fox_causal(sT)
            pT = jnp.exp(sT - lse_r[0])
            delta = jnp.sum(oT_r[0] * doT_r[0], axis=0, keepdims=True)
            dsT = pT * (_dot(v_r[0], doT_r[0], "nn") - delta)
            dv_acc[...] += _dot(pT, do_r[0], "nn")
            dk_acc[...] += _dot(dsT, qa_r[0], "nn")
            dqT_o[0, qi] += _dot(kT_r[0], dsT, "nn")

        @pl.when(qi > ki)
        def _():
            update(False)

        @pl.when(qi == ki)
        def _():
            update(True)

        @pl.when(qi == n - 1)
        def _():
            dk_o[0] = dk_acc[...]
            dv_o[0] = dv_acc[...]

    def qlane(rows):
        return pl.BlockSpec((1, rows, t), lambda h, s, qt, kt: (h, 0, qt[s]))

    def qrow(cols):
        return pl.BlockSpec((1, t, cols), lambda h, s, qt, kt: (h, qt[s], 0))

    def krow(cols):
        return pl.BlockSpec((1, t, cols), lambda h, s, qt, kt: (h, kt[s], 0))

    return pl.pallas_call(
        body,
        name="fox_bwd",
        grid_spec=pltpu.PrefetchScalarGridSpec(
            num_scalar_prefetch=2, grid=(FOX_HEADS, int(qi_tab.shape[0])),
            in_specs=[qlane(FOX_AUG), qrow(FOX_AUG), krow(FOX_AUG),
                      pl.BlockSpec((1, FOX_AUG, t), lambda h, s, qt, kt: (h, 0, kt[s])), krow(FOX_DH), qrow(FOX_DH),
                      qlane(FOX_DH), qlane(FOX_DH), qlane(1)],
            out_specs=[pl.BlockSpec((1, n, FOX_AUG, t), lambda h, s, qt, kt: (h, 0, 0, 0)), krow(FOX_AUG),
                       krow(FOX_DH)],
            scratch_shapes=[pltpu.VMEM((t, FOX_AUG), F32), pltpu.VMEM((t, FOX_DH), F32)]),
        out_shape=[jax.ShapeDtypeStruct((FOX_HEADS, n, FOX_AUG, t), F32),
                   jax.ShapeDtypeStruct((FOX_HEADS, S, FOX_AUG), F32),
                   jax.ShapeDtypeStruct((FOX_HEADS, S, FOX_DH), F32)],
        compiler_params=pltpu.CompilerParams(
            dimension_semantics=("parallel", "arbitrary"), vmem_limit_bytes=VMEM_LIMIT_BYTES),
    )(qi_tab, ki_tab, qT, qa, ka, kT, v, do, doT, oT, lse)


def _to_heads(x2d, S):
    return jnp.transpose(x2d.reshape(S, FOX_HEADS, FOX_DH), (1, 0, 2))


def _from_heads(xh, S):
    return jnp.transpose(xh, (1, 0, 2)).reshape(S, FOX_HEADS * FOX_DH)


def _ffn_fwd(tag, x, wg, wu, wd, g, b, S):
    def up_epi(accs):
        gate, up = accs
        sil, _ = _silu_and_grad(gate)
        return [gate, up, sil * up]

    gate, up, act = _mm(tag + "_up", "nn", [x], [wg, wu], [(0, 0, 0), (1, 0, 1)], 2, up_epi, [],
                        [F32, F32, BF16], S, D_FF, D_MODEL, tn=1408)

    def down_epi(accs, xr, gg, bb):
        z = ALPHA * xr + 0.5 * accs[0]
        return [z, _ln_fwd(z, gg, bb)]

    z, xn = _mm(tag + "_down", "nn", [act], [wd], [(0, 0, 0)], 1, down_epi, [(x, "mn", 0), (g, "n"), (b, "n")],
                [F32, F32], S, D_MODEL, D_FF, tk=1408)
    return xn, dict(x=x, gate=gate, up=up, act=act, z=z)


def _ln_bwd_call(tag, dy, z, g, S):
    def fn(dy_t, z_t, g_t):
        dz, xhat = _ln_bwd(dy_t, z_t, g_t)
        return [dz], [_colsum(dy_t * xhat), _colsum(dy_t)]

    (dz,), (dg, db) = _rowwise(tag + "_ln_bwd", fn, [dy, z], [g], [(D_MODEL, F32)], [D_MODEL, D_MODEL], S)
    return dz, dg, db


def _ffn_bwd(tag, dxn, sv, wg, wu, wd, g, S):
    dz, dg, db = _ln_bwd_call(tag, dxn, sv["z"], g, S)

    def act_epi(accs, gate, up):
        da = 0.5 * accs[0]
        sil, dsil = _silu_and_grad(gate)
        return [da * up * dsil, da * sil]

    dgate, dup = _mm(tag + "_dact", "nt", [dz], [wd], [(0, 0, 0)], 1, act_epi,
                     [(sv["gate"], "mn", 0), (sv["up"], "mn", 0)], [BF16, BF16], S, D_FF, D_MODEL, tn=1408)
    dwd = _mm1(tag + "_dwd", "tn", sv["act"], dz, D_FF, D_MODEL, S, scale=0.5, tm=1408)

    def two(accs):
        return [accs[0], accs[1]]

    dwg, dwu = _mm(tag + "_dwup", "tn", [sv["x"]], [dgate, dup], [(0, 0, 0), (1, 0, 1)], 2, two, [], [F32, F32],
                   D_MODEL, D_FF, S, tn=1408)

    def dx_epi(accs, dzr):
        return [accs[0] + ALPHA * dzr]

    (dx,) = _mm(tag + "_dx", "nt", [dgate, dup], [wg, wu], [(0, 0, 0), (0, 1, 1)], 1, dx_epi, [(dz, "mn", 0)],
                [F32], S, D_MODEL, D_FF, tk=1408)
    return dx, dict(w_up=jnp.concatenate([dwg, dwu], axis=1), w_down=dwd, ln_g=dg, ln_b=db)


def _mixer_fwd(x1, w, S):
    u = _mm1("w_in", "nn", x1, w["w_in_p"], S, U_WIDTH, D_MODEL, tn=1536)
    ax0 = u[:, U_AX:U_AX + BRANCH]
    ax_shift = [ax0] + [_shift_down(ax0, k) for k in (1, 2, 3)]
    xc, r, gi, a, h, y_a = _lru_fwd(u, ax_shift, w["lru"], S)
    y_b, oraw, states = _gla_fwd(u, w["gla"], S)
    fcum = _fox_gate_fwd(u, w["bfp"], S)
    heads = (S, FOX_HEADS, FOX_DH)
    qa, ka = _fox_augment(u[:, U_CQ:U_CQ + BRANCH].reshape(heads), u[:, U_CK:U_CK + BRANCH].reshape(heads),
                          fcum[:, :FOX_HEADS])
    vh = _to_heads(u[:, U_CV:U_CV + BRANCH].astype(BF16), S)
    qT, kT, vT = (jnp.swapaxes(a, 1, 2) for a in (qa, ka, vh))
    oT, lse = _fox_fwd(qT, ka, vT, S)
    y_c = jnp.transpose(oT, (2, 0, 1)).reshape(S, BRANCH).astype(BF16)

    def merge_epi(accs, g0, g1, g2):
        merged = _sigmoid(g0) * accs[0] + _sigmoid(g1) * accs[1] + _sigmoid(g2) * accs[2]
        return [accs[0], accs[1], accs[2], merged]

    wb = w["w_branch"]
    yp0, yp1, yp2, merged = _mm(
        "merge", "nn", [y_a, y_b, y_c], [wb[0], wb[1], wb[2]], [(0, 0, 0), (1, 1, 1), (2, 2, 2)], 3, merge_epi,
        [(u, "mn", 0), (u, "mn", 1), (u, "mn", 2)], [F32, F32, F32, BF16], S, D_MODEL, BRANCH, tm=256)

    def out_epi(accs, xr, gg, bb):
        z = ALPHA * xr + accs[0]
        return [z, _ln_fwd(z, gg, bb)]

    z2, x2 = _mm("w_out", "nn", [merged], [w["w_out"]], [(0, 0, 0)], 1, out_epi,
                 [(x1, "mn", 0), (w["ln2_g"], "n"), (w["ln2_b"], "n")], [F32, F32], S, D_MODEL, D_MODEL)
    sv = dict(x=x1, u=u, ax_shift=ax_shift, xc=xc, r=r, i=gi, a=a, h=h, y_a=y_a, y_b=y_b, y_c=y_c, oraw=oraw,
              states=states, qT=qT, qa=qa, ka=ka, kT=kT, vh=vh, oT=oT, lse=lse, yp=(yp0, yp1, yp2), merged=merged,
              z=z2)
    return x2, sv


def _mixer_bwd(dx2, sv, w, S):
    u = sv["u"]
    dz, dg2, db2 = _ln_bwd_call("mix", dx2, sv["z"], w["ln2_g"], S)

    def dm_epi(accs, y0, y1, y2, g0, g1, g2):
        dm = accs[0]
        outs_p, outs_g = [], []
        for yp, gl in ((y0, g0), (y1, g1), (y2, g2)):
            sg = _sigmoid(gl)
            outs_p.append(dm * sg)
            outs_g.append(dm * yp * sg * (1.0 - sg))
        return outs_p + outs_g

    yp = sv["yp"]
    dyp0, dyp1, dyp2, dgl0, dgl1, dgl2 = _mm(
        "dmerged", "nt", [dz], [w["w_out"]], [(0, 0, 0)], 1, dm_epi,
        [(yp[0], "mn", 0), (yp[1], "mn", 0), (yp[2], "mn", 0), (u, "mn", 0), (u, "mn", 1), (u, "mn", 2)],
        [BF16] * 6, S, D_MODEL, D_MODEL, tm=256)
    dw_out = _mm1("dw_out", "tn", sv["merged"], dz, D_MODEL, D_MODEL, S)
    wb = w["w_branch"]
    dys, dwbs = [], []
    for j, (yj, dyp) in enumerate(((sv["y_a"], dyp0), (sv["y_b"], dyp1), (sv["y_c"], dyp2))):
        dys.append(_mm1("dy_branch%d" % j, "nt", dyp, wb[j], S, BRANCH, D_MODEL))
        dwbs.append(_mm1("dw_branch%d" % j, "tn", yj, dyp, BRANCH, D_MODEL, S))
    day, dxc, dwa, dwx, dba, dbx, dlam = _lru_bwd(dys[0], u, sv, w["lru"], S)
    dax, (dcw0, dcw1, dcw2, dcw3, dcb) = _conv_bwd(dxc, sv["ax_shift"], w["lru"], S)
    dbq, dbk, dbv, dbr, dglow, dwg2p, dbg, dng = _gla_bwd(dys[1], u, sv["oraw"], sv["states"], w["gla"], S)
    doh = _to_heads(dys[2], S)
    dqT_aug, dk_aug, dvh = _fox_bwd(sv["qT"], sv["qa"], sv["ka"], sv["kT"], sv["vh"], doh, jnp.swapaxes(doh, 1, 2),
                                    sv["oT"], sv["lse"], S)
    dqT_aug = jnp.swapaxes(dqT_aug, 1, 2).reshape(FOX_HEADS, FOX_AUG, S)
    dqh = jnp.swapaxes(dqT_aug[:, :FOX_DH, :], 1, 2) * FOX_SCALE
    dkh = dk_aug[:, :, :FOX_DH]
    dfc = jnp.transpose(dqT_aug[:, FOX_DH + FOX_ONES, :] - dk_aug[:, :, FOX_DH])
    dfc = jnp.pad(dfc, ((0, 0), (0, LANES - FOX_HEADS)))
    dcf, dbf = _fox_gate_bwd(dfc, u, w["bfp"], S)
    du = jnp.concatenate(
        [dgl0, dgl1, dgl2, dax, day, dbq, dbk, dbv, dbr, _from_heads(dqh, S).astype(BF16),
         _from_heads(dkh, S).astype(BF16), _from_heads(dvh, S).astype(BF16), dglow, dcf,
         jnp.zeros((S, U_WIDTH - U_CF - LANES), BF16)], axis=1)
    dw_in_p = _mm1("dw_in", "tn", sv["x"], du, D_MODEL, U_WIDTH, S, tn=1536)

    def dx_epi(accs, dzr):
        return [accs[0] + ALPHA * dzr]

    (dx1,) = _mm("dx_mix", "nt", [du], [w["w_in_p"]], [(0, 0, 0)], 1, dx_epi, [(dz, "mn", 0)], [F32], S, D_MODEL,
                 U_WIDTH, tk=1536)
    pieces = sorted(W_IN_SEGMENTS)
    dw_in = jnp.concatenate([dw_in_p[:, dst:dst + width] for _, width, dst in pieces], axis=1)
    eye = jnp.eye(LRU_BLOCKS, dtype=F32)
    dwa_b = jnp.einsum("ncmd,nm->ncd", dwa.reshape(LRU_BLOCKS, 64, LRU_BLOCKS, 64), eye)
    dwx_b = jnp.einsum("ncmd,nm->ncd", dwx.reshape(LRU_BLOCKS, 64, LRU_BLOCKS, 64), eye)
    grads = dict(
        w_in=dw_in, w_out=dw_out, w_branch=jnp.stack(dwbs), ln2_g=dg2, ln2_b=db2,
        conv_w=jnp.concatenate([dcw0, dcw1, dcw2, dcw3], axis=0), conv_b=dcb, lru_wa=dwa_b, lru_wx=dwx_b,
        lru_ba=dba, lru_bx=dbx, lru_lambda=dlam, gla_w_g2=dwg2p[:GLA_LOWRANK], gla_b_g=dbg, gla_norm_g=dng,
        fox_b_f=dbf[:, :FOX_HEADS])
    return dx1, grads


def _ple_fwd(x3, p_i, w, S):
    pe = _mm1("ple_proj", "nn", p_i, w["ple_w_proj"], S, D_MODEL, PLE_DIM)

    def epi(accs, xr, per, bg, gg, bb):
        sg = _sigmoid(accs[0] + bg)
        z = ALPHA * xr + sg * per
        return [sg, z, _ln_fwd(z, gg, bb)]

    sg, z4, x4 = _mm("ple_gate", "nn", [x3], [w["ple_w_gate"]], [(0, 0, 0)], 1, epi,
                     [(x3, "mn", 0), (pe, "mn", 0), (w["ple_b_gate"], "n"), (w["ln4_g"], "n"), (w["ln4_b"], "n")],
                     [F32, F32, F32], S, D_MODEL, D_MODEL)
    return x4, dict(x=x3, p=p_i, pe=pe, sg=sg, z=z4)


def _ple_bwd(dx4, sv, w, S):
    def fn(dy_t, z_t, pe_t, sg_t, g_t):
        dz, xhat = _ln_bwd(dy_t, z_t, g_t)
        dgl = dz * pe_t * sg_t * (1.0 - sg_t)
        return [dz, dz * sg_t, dgl], [_colsum(dy_t * xhat), _colsum(dy_t), _colsum(dgl)]

    (dz, dpe, dgl), (dg4, db4, dbg) = _rowwise(
        "ple_bwd", fn, [dx4, sv["z"], sv["pe"], sv["sg"]], [w["ln4_g"]],
        [(D_MODEL, F32), (D_MODEL, BF16), (D_MODEL, BF16)], [D_MODEL] * 3, S)
    dwp = _mm1("dw_ple_proj", "tn", sv["p"], dpe, PLE_DIM, D_MODEL, S)
    dwg = _mm1("dw_ple_gate", "tn", sv["x"], dgl, D_MODEL, D_MODEL, S)

    def dx_epi(accs, dzr):
        return [accs[0] + ALPHA * dzr]

    (dx3,) = _mm("dx_ple", "nt", [dgl], [w["ple_w_gate"]], [(0, 0, 0)], 1, dx_epi, [(dz, "mn", 0)], [F32], S,
                 D_MODEL, D_MODEL)
    return dx3, dict(ple_w_proj=dwp, ple_w_gate=dwg, ple_b_gate=dbg, ln4_g=dg4, ln4_b=db4)


def _layer_weights(full, i):
    w = {}
    up = full["ffn1_w_up"][i]
    w["ffn1"] = (up[:, :D_FF], up[:, D_FF:], full["ffn1_w_down"][i])
    up = full["ffn2_w_up"][i]
    w["ffn2"] = (up[:, :D_FF], up[:, D_FF:], full["ffn2_w_down"][i])
    w_in = full["w_in"][i]
    placed = sorted((dst, src, width) for src, width, dst in W_IN_SEGMENTS)
    parts, pos = [], 0
    for dst, src, width in placed:
        if dst > pos:
            parts.append(jnp.zeros((D_MODEL, dst - pos), w_in.dtype))
        parts.append(w_in[:, src:src + width])
        pos = dst + width
    parts.append(jnp.zeros((D_MODEL, U_WIDTH - pos), w_in.dtype))
    w["w_in_p"] = jnp.concatenate(parts, axis=1)
    eye = jnp.eye(LRU_BLOCKS, dtype=F32)

    def dense(blocks):
        return jnp.einsum("ncd,nm->ncmd", blocks, eye).reshape(BRANCH, BRANCH).astype(BF16)

    def vec(name):
        return full[name][i].reshape(1, -1)

    cw = full["conv_w"][i]
    w["lru"] = dict(cw0=cw[0:1], cw1=cw[1:2], cw2=cw[2:3], cw3=cw[3:4], conv_b=vec("conv_b"),
                    wa=dense(full["lru_wa"][i]), wx=dense(full["lru_wx"][i]), ba=vec("lru_ba"), bx=vec("lru_bx"),
                    lam=vec("lru_lambda"))
    hq = jnp.arange(GLA_QK) // GLA_DK
    hv = jnp.arange(GLA_V) // GLA_DV
    w["gla"] = dict(wg2=jnp.pad(full["gla_w_g2"][i], ((0, LANES - GLA_LOWRANK), (0, 0))).astype(BF16),
                    bg=vec("gla_b_g"), ng=vec("gla_norm_g"), bd=(hv[:, None] == hq[None, :]).astype(F32))
    w["bfp"] = jnp.pad(vec("fox_b_f"), ((0, 0), (0, LANES - FOX_HEADS)))
    w["w_branch"] = full["w_branch"][i]
    w["w_out"] = full["w_out"][i]
    w["ple_w_proj"] = full["ple_w_proj"][i]
    w["ple_w_gate"] = full["ple_w_gate"][i]
    for name in ("ln1_g", "ln1_b", "ln2_g", "ln2_b", "ln3_g", "ln3_b", "ln4_g", "ln4_b", "ple_b_gate"):
        w[name] = vec(name)
    return w


def _layer_fwd(x0, p_i, w, S):
    x1, s1 = _ffn_fwd("ffn1", x0, *w["ffn1"], w["ln1_g"], w["ln1_b"], S)
    x2, s2 = _mixer_fwd(x1, w, S)
    x3, s3 = _ffn_fwd("ffn2", x2, *w["ffn2"], w["ln3_g"], w["ln3_b"], S)
    x4, s4 = _ple_fwd(x3, p_i, w, S)
    return x4, (s1, s2, s3, s4)


def _layer_bwd(dx4, saved, w, S):
    s1, s2, s3, s4 = saved
    dx3, g4 = _ple_bwd(dx4, s4, w, S)
    dx2, g3 = _ffn_bwd("ffn2", dx3, s3, *w["ffn2"], w["ln3_g"], S)
    dx1, g2 = _mixer_bwd(dx2, s2, w, S)
    dx0, g1 = _ffn_bwd("ffn1", dx1, s1, *w["ffn1"], w["ln1_g"], S)
    grads = dict(g2)
    grads.update(g4)
    grads.update(ffn1_w_up=g1["w_up"], ffn1_w_down=g1["w_down"], ln1_g=g1["ln_g"], ln1_b=g1["ln_b"],
                 ffn2_w_up=g3["w_up"], ffn2_w_down=g3["w_down"], ln3_g=g3["ln_g"], ln3_b=g3["ln_b"])
    return dx0, grads


def _local_step(x, p, target, full):
    S = x.shape[0]
    ws = [_layer_weights(full, i) for i in range(DEPTH)]
    saved = []
    h = x
    for i in range(DEPTH):
        h, sv = _layer_fwd(h, p[i], ws[i], S)
        saved.append(sv)

    def loss_fn(y, t):
        err = y - t
        return [err * (1.0 / D_MODEL)], [_colsum(err * err) * (0.5 / D_MODEL)]

    (dy,), (lsum,) = _rowwise("loss", loss_fn, [h, target], [], [(D_MODEL, F32)], [D_MODEL], S)
    loss = jnp.sum(lsum)
    layer_grads = [None] * DEPTH
    for i in reversed(range(DEPTH)):
        dy, layer_grads[i] = _layer_bwd(dy, saved[i], ws[i], S)
    grads = {}
    for name in WEIGHTS:
        grads[name] = jnp.stack([layer_grads[i][name] for i in range(DEPTH)])
    return loss, dy, grads


def kernel(x, p, ffn1_w_up, ffn1_w_down, ln1_g, ln1_b, w_in, conv_w, conv_b, lru_wa, lru_ba, lru_wx, lru_bx, lru_lambda, gla_w_g2, gla_b_g, gla_norm_g, fox_b_f, w_branch, w_out, ln2_g, ln2_b, ffn2_w_up, ffn2_w_down, ln3_g, ln3_b, ple_w_proj, ple_w_gate, ple_b_gate, ln4_g, ln4_b, loss_target, m_ffn1_w_up, m_ffn1_w_down, m_ln1_g, m_ln1_b, m_w_in, m_conv_w, m_conv_b, m_lru_wa, m_lru_ba, m_lru_wx, m_lru_bx, m_lru_lambda, m_gla_w_g2, m_gla_b_g, m_gla_norm_g, m_fox_b_f, m_w_branch, m_w_out, m_ln2_g, m_ln2_b, m_ffn2_w_up, m_ffn2_w_down, m_ln3_g, m_ln3_b, m_ple_w_proj, m_ple_w_gate, m_ple_b_gate, m_ln4_g, m_ln4_b, v_ffn1_w_up, v_ffn1_w_down, v_ln1_g, v_ln1_b, v_w_in, v_conv_w, v_conv_b, v_lru_wa, v_lru_ba, v_lru_wx, v_lru_bx, v_lru_lambda, v_gla_w_g2, v_gla_b_g, v_gla_norm_g, v_fox_b_f, v_w_branch, v_w_out, v_ln2_g, v_ln2_b, v_ffn2_w_up, v_ffn2_w_down, v_ln3_g, v_ln3_b, v_ple_w_proj, v_ple_w_gate, v_ple_b_gate, v_ln4_g, v_ln4_b):
    env = dict(locals())
    wts = {n: env[n] for n in WEIGHTS}
    ms = {n: env["m_" + n] for n in WEIGHTS}
    vs = {n: env["v_" + n] for n in WEIGHTS}
    shard_axis = dict(SHARDED)
    sharded = [n for n, _ in SHARDED]
    big = [n for n in sharded if n not in SHARDED_F32_GATHER]

    shards = [wts[n].astype(BF16) if n in big else wts[n] for n in sharded]
    gathered = _allgather_multi("gather_weights", shards)
    full = {n: wts[n] for n in REPLICATED}
    for n, g in zip(sharded, gathered):
        full[n] = _from_gathered(g, shard_axis[n])

    loss_part, grad_x, grads = _local_step(x[0], p[:, 0], loss_target[0], full)
    loss = lax.psum(loss_part, MESH_AXES)

    dest = [_dest_core_major(grads[n], shard_axis[n]) for n in sharded]
    got = _sibling_swap_multi("grad_sibling_swap", dest)
    core = lax.axis_index("c").astype(jnp.int32).reshape(1)
    pairs = [_pair_add("grad_pair_add_" + n, core, _as_rows(d, 2), _as_rows(g, 1))
             for n, d, g in zip(sharded, dest, got)]
    parts = _chip_all_to_all_multi("grad_chip_all_to_all", pairs)
    rep = list(REPLICATED)
    (gr,) = _allgather_multi("grad_gather_replicated", [_pack([grads[n].reshape(wts[n].shape) for n in rep], F32)])

    out = {}
    for n, gp in zip(sharded, parts):
        res = _adamw("adamw_" + n, gp, _as_rows(wts[n], 0), _as_rows(ms[n], 0), _as_rows(vs[n], 0))
        for kind, arr in zip(("grad", "delta", "new_m", "new_v"), res):
            out[kind + "_" + n] = arr.reshape(wts[n].shape)
    res = _adamw("adamw_replicated", gr, _pack([wts[n] for n in rep], F32), _pack([ms[n] for n in rep], F32),
                 _pack([vs[n] for n in rep], F32))
    shapes = [wts[n].shape for n in rep]
    for kind, buf in zip(("grad", "delta", "new_m", "new_v"), res):
        for n, arr in zip(rep, _unpack(buf, shapes)):
            out[kind + "_" + n] = arr
    return (loss, grad_x[None], *[out["grad_" + n] for n in WEIGHTS], *[out["delta_" + n] for n in WEIGHTS],
            *[out["new_m_" + n] for n in WEIGHTS], *[out["new_v_" + n] for n in WEIGHTS])
```

```python
import functools
import math

import jax
import jax.numpy as jnp
from jax import lax
from jax.experimental import pallas as pl
from jax.experimental.pallas import tpu as pltpu

F32 = jnp.float32
BF16 = jnp.bfloat16

N_DEV = 8
MESH_AXES = ("x", "y", "c")
DEPTH = 2
D_MODEL = 1024
D_FF = 2816
BRANCH = 512
CHUNK = 64
GLA_HEADS = 4
GLA_DK = 64
GLA_DV = 128
GLA_LOWRANK = 16
GLA_TAU = 16.0
FOX_HEADS = 8
FOX_DH = 64
PLE_DIM = 256
LRU_C = 8.0
LRU_BLOCKS = 8
LN_EPS = 1e-5
RMS_EPS = 1e-6
ALPHA = (2 * DEPTH) ** 0.25
LANES = 128
NEG_BIG = -1e30

ADAM_LR = 0.001
ADAM_B1 = 0.9
ADAM_B2 = 0.999
ADAM_EPS = 1e-08
ADAM_WD = 0.01
ADAM_STEP = 10

VMEM_LIMIT_BYTES = 56 * 1024 * 1024

U_GATES = 0
U_AX = 3072
U_AY = 3584
U_BQ = 4096
U_BK = 4352
U_BV = 4608
U_BR = 5120
U_CQ = 5632
U_CK = 6144
U_CV = 6656
U_BLOW = 7168
U_CF = 7296
U_WIDTH = 7680
W_IN_SEGMENTS = (
    (0, 512, U_AX), (512, 512, U_AY), (1024, 256, U_BQ), (1280, 256, U_BK), (1536, 512, U_BV),
    (2048, 16, U_BLOW), (2064, 512, U_BR), (2576, 512, U_CQ), (3088, 512, U_CK), (3600, 512, U_CV),
    (4112, 8, U_CF), (4120, 3072, U_GATES),
)

SHARDED = (
    ("ffn1_w_up", 2), ("ffn1_w_down", 1), ("w_in", 2), ("conv_w", 2), ("gla_w_g2", 2), ("w_branch", 3),
    ("w_out", 1), ("ffn2_w_up", 2), ("ffn2_w_down", 1), ("ple_w_proj", 2), ("ple_w_gate", 1),
)
SHARDED_F32_GATHER = ("conv_w", "gla_w_g2")
COLUMN_SHARDED = ("ffn1_w_up", "ffn2_w_up", "w_in", "w_branch", "ple_w_proj")
REPLICATED = ("ln1_g", "ln1_b", "conv_b", "lru_wa", "lru_ba", "lru_wx", "lru_bx", "lru_lambda", "gla_b_g",
              "gla_norm_g", "fox_b_f", "ln2_g", "ln2_b", "ln3_g", "ln3_b", "ple_b_gate", "ln4_g", "ln4_b")
WEIGHTS = ("ffn1_w_up", "ffn1_w_down", "ln1_g", "ln1_b", "w_in", "conv_w", "conv_b", "lru_wa", "lru_ba", "lru_wx",
           "lru_bx", "lru_lambda", "gla_w_g2", "gla_b_g", "gla_norm_g", "fox_b_f", "w_branch", "w_out", "ln2_g",
           "ln2_b", "ffn2_w_up", "ffn2_w_down", "ln3_g", "ln3_b", "ple_w_proj", "ple_w_gate", "ple_b_gate", "ln4_g",
           "ln4_b")


def _sigmoid(x):
    return 1.0 / (1.0 + jnp.exp(-x))


def _log1p_pos(e):
    return jnp.where(e < 1e-4, e * (1.0 - 0.5 * e), jnp.log(1.0 + e))


def _softplus(x):
    return jnp.maximum(x, 0.0) + _log1p_pos(jnp.exp(-jnp.abs(x)))


def _log_sigmoid(x):
    return -_softplus(-x)


def _neg_expm1(y):
    series = -y * (1.0 + y * (0.5 + y * (1.0 / 6.0 + y * (1.0 / 24.0 + y * (1.0 / 120.0)))))
    return jnp.where(y > -0.1, series, 1.0 - jnp.exp(y))


def _silu_and_grad(x):
    s = _sigmoid(x)
    return x * s, s * (1.0 + x * (1.0 - s))


_GELU_C = math.sqrt(2.0 / math.pi)


def _gelu_and_grad(x):
    inner = _GELU_C * (x + 0.044715 * x * x * x)
    t = jnp.tanh(inner)
    g = 0.5 * x * (1.0 + t)
    dg = 0.5 * (1.0 + t) + 0.5 * x * (1.0 - t * t) * _GELU_C * (1.0 + 3.0 * 0.044715 * x * x)
    return g, dg


def _ln_stats(z):
    mu = jnp.mean(z, axis=-1, keepdims=True)
    zc = z - mu
    var = jnp.mean(zc * zc, axis=-1, keepdims=True)
    rstd = lax.rsqrt(var + LN_EPS)
    return zc * rstd, rstd


def _ln_fwd(z, g, b):
    xhat, _ = _ln_stats(z)
    return xhat * g + b


def _ln_bwd(dy, z, g):
    xhat, rstd = _ln_stats(z)
    dxh = dy * g
    m1 = jnp.mean(dxh, axis=-1, keepdims=True)
    m2 = jnp.mean(dxh * xhat, axis=-1, keepdims=True)
    return rstd * (dxh - m1 - xhat * m2), xhat


def _colsum(x):
    return jnp.sum(x, axis=0, keepdims=True)


def _dot(a, b, dims):
    dn = {"nn": (((1,), (0,)), ((), ())), "nt": (((1,), (1,)), ((), ())), "tn": (((0,), (0,)), ((), ()))}[dims]
    return lax.dot_general(a.astype(BF16), b.astype(BF16), dn, preferred_element_type=F32)


def _scan_rows(a, b, length, reverse=False, seg=None):
    rows = lax.broadcasted_iota(jnp.int32, b.shape, 0)
    span = seg if seg else length
    pos = rows % span if seg else rows
    d = 1
    while d < span:
        shift = (length - d) if reverse else d
        valid = (pos < span - d) if reverse else (pos >= d)
        sb = jnp.where(valid, pltpu.roll(b, shift, 0), 0.0)
        if a is None:
            b = b + sb
        else:
            b = b + a * sb
            a = a * jnp.where(valid, pltpu.roll(a, shift, 0), 1.0)
        d *= 2
    return a, b


def _tile(dim, pref):
    if dim <= pref:
        return dim
    best = None
    t = LANES
    while t <= pref:
        if dim % t == 0:
            best = t
        t += LANES
    assert best is not None, (dim, pref)
    return best


def _full_spec(arr):
    nd = arr.ndim
    return pl.BlockSpec(arr.shape, lambda *_: (0,) * nd)


def _mm(name, dims, a_ops, b_ops, terms, n_acc, epilogue, extras, out_dtypes, M, N, K, tm=512, tn=1024, tk=1024):
    tm, tn, tk = _tile(M, tm), _tile(N, tn), _tile(K, tk)
    gm, gn, gk = M // tm, N // tn, K // tk
    a_bytes = sum(a.size * a.dtype.itemsize for a in a_ops)
    b_bytes = sum(b.size * b.dtype.itemsize for b in b_ops)
    n_outer = gk == 1 and b_bytes + a_bytes * gn < a_bytes + b_bytes * gm

    def spec(shape, fn):
        if n_outer:
            return pl.BlockSpec(shape, lambda j, i, k: fn(i, j, k))
        return pl.BlockSpec(shape, fn)

    if dims == "tn":
        a_spec = spec((tk, tm), lambda i, j, k: (k, i))
    else:
        a_spec = spec((tm, tk), lambda i, j, k: (i, k))
    if dims == "nt":
        b_spec = spec((tn, tk), lambda i, j, k: (j, k))
    else:
        b_spec = spec((tk, tn), lambda i, j, k: (k, j))
    e_specs, e_arrays = [], []
    for ex in extras:
        if ex[1] == "mn":
            e_specs.append(spec((tm, tn), functools.partial(lambda i, j, k, off: (i, j + off), off=ex[2])))
        else:
            e_specs.append(spec((1, tn), lambda i, j, k: (0, j)))
        e_arrays.append(ex[0])
    na, nb, ne, no = len(a_ops), len(b_ops), len(extras), len(out_dtypes)

    def body(*refs):
        a_refs = refs[:na]
        b_refs = refs[na:na + nb]
        e_refs = refs[na + nb:na + nb + ne]
        o_refs = refs[na + nb + ne:na + nb + ne + no]
        acc_refs = refs[na + nb + ne + no:]
        k = pl.program_id(2)

        @pl.when(k == 0)
        def _():
            for acc in acc_refs:
                acc[...] = jnp.zeros_like(acc)

        for r, ai, bi in terms:
            acc_refs[r][...] += _dot(a_refs[ai][...], b_refs[bi][...], dims)

        @pl.when(k == gk - 1)
        def _():
            res = epilogue([acc[...] for acc in acc_refs], *[e[...] for e in e_refs])
            for o, val in zip(o_refs, res):
                o[...] = val.astype(o.dtype)

    outs = pl.pallas_call(
        body,
        name=name,
        grid=(gn, gm, gk) if n_outer else (gm, gn, gk),
        in_specs=[a_spec] * na + [b_spec] * nb + e_specs,
        out_specs=[spec((tm, tn), lambda i, j, k: (i, j))] * no,
        out_shape=[jax.ShapeDtypeStruct((M, N), dt) for dt in out_dtypes],
        scratch_shapes=[pltpu.VMEM((tm, tn), F32)] * n_acc,
        compiler_params=pltpu.CompilerParams(
            dimension_semantics=("parallel", "parallel", "arbitrary"), vmem_limit_bytes=VMEM_LIMIT_BYTES),
    )(*a_ops, *b_ops, *e_arrays)
    return outs


def _mm1(name, dims, a, b, M, N, K, out_dtype=F32, scale=None, **kw):
    def epi(accs):
        return [accs[0] if scale is None else accs[0] * scale]
    return _mm(name, dims, [a], [b], [(0, 0, 0)], 1, epi, [], [out_dtype], M, N, K, **kw)[0]


def _rowwise(name, fn, row_ins, vec_ins, row_outs, sum_outs, S, tr=256, reverse=False):
    tr = min(tr, S)
    g = S // tr
    rmap = (lambda i: (g - 1 - i)) if reverse else (lambda i: i)
    in_specs, arrays = [], []
    for r in row_ins:
        if isinstance(r, tuple):
            arr, width, blk = r
            in_specs.append(pl.BlockSpec((tr, width), functools.partial(lambda i, blk: (rmap(i), blk), blk=blk)))
        else:
            arr = r
            in_specs.append(pl.BlockSpec((tr, arr.shape[1]), lambda i: (rmap(i), 0)))
        arrays.append(arr)
    for v in vec_ins:
        in_specs.append(_full_spec(v))
        arrays.append(v)
    nr, nv, no, ns = len(row_ins), len(vec_ins), len(row_outs), len(sum_outs)

    def body(*refs):
        ins = [r[...] for r in refs[:nr + nv]]
        o_refs = refs[nr + nv:nr + nv + no]
        s_refs = refs[nr + nv + no:]
        outs, sums = fn(*ins)
        for o, val in zip(o_refs, outs):
            o[...] = val.astype(o.dtype)
        if ns:
            i = pl.program_id(0)

            @pl.when(i == 0)
            def _():
                for s, val in zip(s_refs, sums):
                    s[...] = val

            @pl.when(i > 0)
            def _():
                for s, val in zip(s_refs, sums):
                    s[...] += val

    res = pl.pallas_call(
        body,
        name=name,
        grid=(g,),
        in_specs=in_specs,
        out_specs=[pl.BlockSpec((tr, c), lambda i: (rmap(i), 0)) for c, _ in row_outs]
        + [pl.BlockSpec((1, c), lambda i: (0, 0)) for c in sum_outs],
        out_shape=[jax.ShapeDtypeStruct((S, c), dt) for c, dt in row_outs]
        + [jax.ShapeDtypeStruct((1, c), F32) for c in sum_outs],
        compiler_params=pltpu.CompilerParams(
            dimension_semantics=("arbitrary",), vmem_limit_bytes=VMEM_LIMIT_BYTES),
    )(*arrays)
    return res[:no], res[no:]


def _win(arr, offset, width):
    assert offset % width == 0
    return (arr, width, offset // width)


MESH_ID = pl.DeviceIdType.MESH


def _remote(src, dst, send_sem, recv_sem, to):
    return pltpu.make_async_remote_copy(src_ref=src, dst_ref=dst, send_sem=send_sem, recv_sem=recv_sem,
                                        device_id=to, device_id_type=MESH_ID)


def _hbm_call(name, body, arrs, out_shapes, n_send, n_recv, n_local):
    return pl.pallas_call(
        body,
        name=name,
        in_specs=[pl.BlockSpec(memory_space=pltpu.HBM)] * len(arrs),
        out_specs=[pl.BlockSpec(memory_space=pltpu.HBM)] * len(out_shapes),
        out_shape=out_shapes,
        scratch_shapes=[pltpu.SemaphoreType.DMA((n_send,)), pltpu.SemaphoreType.DMA((n_recv,)),
                        pltpu.SemaphoreType.DMA((n_local,))],
        compiler_params=pltpu.CompilerParams(has_side_effects=True),
    )(*arrs)


def _allgather_multi(name, arrs):
    n = len(arrs)

    def body(*refs):
        ins, outs = refs[:n], refs[n:2 * n]
        send_sems, recv_sems, local_sems = refs[2 * n:]
        x, y, c = lax.axis_index("x"), lax.axis_index("y"), lax.axis_index("c")
        me, sibling = (x, y, c), (x, y, 1 - c)
        chips = [(1 - x, y), (x, 1 - y), (1 - x, 1 - y)]

        def slot(i, dev):
            return outs[i].at[4 * dev[0] + 2 * dev[1] + dev[2]]

        def copy(i, k, block, to, src=None):
            dst = slot(i, block)
            return _remote(dst if src is None else src, dst, send_sems.at[7 * i + k], recv_sems.at[7 * i + k], to)

        mine = [pltpu.make_async_copy(ins[i], slot(i, me), local_sems.at[i]) for i in range(n)]
        for cp in mine:
            cp.start()
        first = []
        for i in range(n):
            first.append(copy(i, 0, me, sibling, src=ins[i]))
            first += [copy(i, 1 + j, me, (*chip, c), src=ins[i]) for j, chip in enumerate(chips)]
        for cp in first:
            cp.start()
        passed = []
        for j, chip in enumerate(chips):
            for i in range(n):
                copy(i, 1 + j, (*chip, c), me).wait_recv()
                cp = copy(i, 4 + j, (*chip, c), sibling)
                cp.start()
                passed.append(cp)
        for i in range(n):
            copy(i, 0, sibling, me).wait_recv()
            for j, chip in enumerate(chips):
                copy(i, 4 + j, (*chip, 1 - c), me).wait_recv()
        for cp in first + passed:
            cp.wait_send()
        for cp in mine:
            cp.wait()

    outs = [jax.ShapeDtypeStruct((N_DEV,) + a.shape, a.dtype) for a in arrs]
    return _hbm_call(name, body, arrs, outs, 7 * n, 7 * n, n)


def _sibling_swap_multi(name, arrs):
    n = len(arrs)
    per = 4

    def body(*refs):
        ins, got = refs[:n], refs[n:2 * n]
        send_sems, recv_sems, _ = refs[2 * n:]
        x, y, c = lax.axis_index("x"), lax.axis_index("y"), lax.axis_index("c")
        sibling = (x, y, 1 - c)
        sends = []
        for i in range(n):
            for a in range(4):
                k = per * i + a
                sends.append(_remote(ins[i].at[a, 1 - c], got[i].at[a], send_sems.at[k], recv_sems.at[k], sibling))
        for cp in sends:
            cp.start()
        for cp in sends:
            cp.wait_recv()
        for cp in sends:
            cp.wait_send()

    outs = [jax.ShapeDtypeStruct((4,) + a.shape[2:], a.dtype) for a in arrs]
    return _hbm_call(name, body, arrs, outs, per * n, per * n, 1)


def _chip_all_to_all_multi(name, arrs):
    n = len(arrs)

    def body(*refs):
        ins, outs = refs[:n], refs[n:2 * n]
        send_sems, recv_sems, local_sems = refs[2 * n:]
        x, y, c = lax.axis_index("x"), lax.axis_index("y"), lax.axis_index("c")
        mine = 2 * x + y
        chips = [(1 - x, y), (x, 1 - y), (1 - x, 1 - y)]
        local = [pltpu.make_async_copy(ins[i].at[mine], outs[i].at[mine], local_sems.at[i]) for i in range(n)]
        for cp in local:
            cp.start()
        sends, recvs = [], []
        for i in range(n):
            for j, (px, py) in enumerate(chips):
                peer = 2 * px + py
                sems = (send_sems.at[3 * i + j], recv_sems.at[3 * i + j], (px, py, c))
                sends.append(_remote(ins[i].at[peer], outs[i].at[mine], *sems))
                recvs.append(_remote(ins[i].at[peer], outs[i].at[peer], *sems))
        for cp in sends:
            cp.start()
        for cp in recvs:
            cp.wait_recv()
        for cp in sends:
            cp.wait_send()
        for cp in local:
            cp.wait()

    outs = [jax.ShapeDtypeStruct(a.shape, a.dtype) for a in arrs]
    return _hbm_call(name, body, arrs, outs, 3 * n, 3 * n, n)


def _as_rows(a, lead):
    return a.reshape(a.shape[:lead] + (-1, a.shape[-1]))


def _row_tile(rows, cols, parts):
    budget = 4 * 1024 * 1024 // (4 * max(cols, LANES) * parts)
    return _tile_rows(rows, max(8, min(512, budget // 8 * 8)))


def _pair_add(name, core, both, got):
    _, rows, cols = got.shape
    tr = _row_tile(rows, cols, 2)

    def body(c_ref, a_ref, b_ref, o_ref):
        o_ref[...] = (a_ref[...] + b_ref[...]).astype(o_ref.dtype)

    blk = pl.BlockSpec((1, tr, cols), lambda ch, i, c_ref: (ch, i, 0))
    return pl.pallas_call(
        body, name=name,
        grid_spec=pltpu.PrefetchScalarGridSpec(
            num_scalar_prefetch=1, grid=(4, rows // tr),
            in_specs=[pl.BlockSpec((1, None, tr, cols), lambda ch, i, c_ref: (ch, c_ref[0], i, 0)), blk],
            out_specs=blk),
        out_shape=jax.ShapeDtypeStruct(got.shape, BF16),
        compiler_params=pltpu.CompilerParams(dimension_semantics=("parallel", "parallel"),
                                             vmem_limit_bytes=VMEM_LIMIT_BYTES),
    )(core, both, got)


def _adamw(name, gparts, w, m, v):
    parts, rows, cols = gparts.shape
    tr = _row_tile(rows, cols, parts)
    c1 = 1.0 / (1.0 - ADAM_B1 ** ADAM_STEP)
    c2 = 1.0 / (1.0 - ADAM_B2 ** ADAM_STEP)

    def body(gp_ref, w_ref, m_ref, v_ref, g_ref, d_ref, nm_ref, nv_ref):
        g = gp_ref[0].astype(F32)
        for i in range(1, parts):
            g = g + gp_ref[i].astype(F32)
        nm = ADAM_B1 * m_ref[...] + (1.0 - ADAM_B1) * g
        nv = ADAM_B2 * v_ref[...] + (1.0 - ADAM_B2) * (g * g)
        m_hat = nm * c1
        v_hat = nv * c2
        g_ref[...] = g
        nm_ref[...] = nm
        nv_ref[...] = nv
        d_ref[...] = -ADAM_LR * (m_hat / (jnp.sqrt(v_hat) + ADAM_EPS) + ADAM_WD * w_ref[...])

    row = pl.BlockSpec((tr, cols), lambda i: (i, 0))
    return pl.pallas_call(
        body,
        name=name,
        grid=(rows // tr,),
        in_specs=[pl.BlockSpec((parts, tr, cols), lambda i: (0, i, 0)), row, row, row],
        out_specs=[row] * 4,
        out_shape=[jax.ShapeDtypeStruct((rows, cols), F32)] * 4,
        compiler_params=pltpu.CompilerParams(dimension_semantics=("parallel",), vmem_limit_bytes=VMEM_LIMIT_BYTES),
    )(gparts, w, m, v)


def _tile_rows(rows, pref):
    t = min(pref, rows) // 8 * 8
    while t >= 8 and rows % t:
        t -= 8
    return t if t >= 8 else rows


PACK_ROWS = 512


def _pack(arrs, dtype):
    flat = jnp.concatenate([a.astype(dtype).reshape(-1) for a in arrs])
    quantum = PACK_ROWS * LANES
    padded = -(-flat.shape[0] // quantum) * quantum
    return jnp.pad(flat, (0, padded - flat.shape[0])).reshape(-1, LANES)


def _unpack(buf, shapes, lead=()):
    flat = buf.reshape(lead + (-1,))
    out, off = [], 0
    for shp in shapes:
        n = math.prod(shp)
        out.append(flat[..., off:off + n].reshape(lead + tuple(shp)))
        off += n
    return out


def _dest_pieces(name, g):
    if name == "w_branch":
        return jnp.moveaxis(g.reshape(3, 4, 2, D_MODEL // N_DEV, BRANCH), 0, 2)
    if name in SHARDED_F32_GATHER:
        return jnp.moveaxis(g.reshape(g.shape[0], 4, 2, -1), 0, 2)
    return g.reshape((4, 2, g.shape[0] // N_DEV) + g.shape[1:])


def _shift_down(a, k):
    return jnp.pad(a, ((k, 0), (0, 0)))[:a.shape[0]] if k else a


def _shift_up(a, k, fill=0.0):
    return jnp.pad(a, ((0, k), (0, 0)), constant_values=fill)[k:] if k else a


def _lru_fwd(u, ax_shift, lw, S):
    T = min(256, S)
    nb = S // T
    row = pl.BlockSpec((T, BRANCH), lambda t: (t, 0))
    vecs = [lw["cw0"], lw["cw1"], lw["cw2"], lw["cw3"], lw["conv_b"], lw["wa"], lw["wx"], lw["ba"], lw["bx"],
            lw["lam"]]

    def body(ax0, ax1, ax2, ax3, ay, cw0, cw1, cw2, cw3, cb, wa, wx, ba, bx, lam, xc_o, r_o, i_o, a_o, h_o, ya_o, hc):
        t = pl.program_id(0)

        @pl.when(t == 0)
        def _():
            hc[...] = jnp.zeros_like(hc)

        xc = cw3[...] * ax0[...] + cw2[...] * ax1[...] + cw1[...] * ax2[...] + cw0[...] * ax3[...] + cb[...]
        r = _sigmoid(_dot(xc, wa[...], "nn") + ba[...])
        gi = _sigmoid(_dot(xc, wx[...], "nn") + bx[...])
        sp = _softplus(-lam[...])
        la = -LRU_C * r * sp
        a = jnp.exp(la)
        mult = jnp.sqrt(_neg_expm1(2.0 * la))
        A, B = _scan_rows(a, mult * gi * xc, T)
        h = B + A * hc[...]
        h_o[...] = h
        hc[...] = h_o[pl.ds(T - 1, 1), :]
        xc_o[...] = xc
        r_o[...] = r
        i_o[...] = gi
        a_o[...] = a
        gy, _ = _gelu_and_grad(ay[...])
        ya_o[...] = (gy * h).astype(ya_o.dtype)

    outs = pl.pallas_call(
        body,
        name="lru_fwd",
        grid=(nb,),
        in_specs=[pl.BlockSpec((T, BRANCH), lambda t: (t, U_AX // BRANCH))] + [row] * 3
        + [pl.BlockSpec((T, BRANCH), lambda t: (t, U_AY // BRANCH))] + [_full_spec(v) for v in vecs],
        out_specs=[row] * 6,
        out_shape=[jax.ShapeDtypeStruct((S, BRANCH), F32)] * 5 + [jax.ShapeDtypeStruct((S, BRANCH), BF16)],
        scratch_shapes=[pltpu.VMEM((1, BRANCH), F32)],
        compiler_params=pltpu.CompilerParams(dimension_semantics=("arbitrary",), vmem_limit_bytes=VMEM_LIMIT_BYTES),
    )(u, *ax_shift[1:], u, *vecs)
    return outs


def _lru_bwd(dya, u, sv, lw, S):
    T = min(256, S)
    nb = S // T
    rrow = pl.BlockSpec((T, BRANCH), lambda t: (nb - 1 - t, 0))
    sq = pl.BlockSpec((BRANCH, BRANCH), lambda t: (0, 0))
    vrow = pl.BlockSpec((1, BRANCH), lambda t: (0, 0))
    h_prev = _shift_down(sv["h"], 1)
    a_next = _shift_up(sv["a"], 1)

    def body(dya_r, ay, h, hp, xc_r, r_r, i_r, a_r, an, wa, wx, lam,
             day_o, dxc_o, dwa_o, dwx_o, dba_o, dbx_o, dlam_o, lcar, tmp):
        t = pl.program_id(0)

        @pl.when(t == 0)
        def _():
            lcar[...] = jnp.zeros_like(lcar)
            dwa_o[...] = jnp.zeros_like(dwa_o)
            dwx_o[...] = jnp.zeros_like(dwx_o)
            dba_o[...] = jnp.zeros_like(dba_o)
            dbx_o[...] = jnp.zeros_like(dbx_o)
            dlam_o[...] = jnp.zeros_like(dlam_o)

        gy, dgy = _gelu_and_grad(ay[...])
        dy = dya_r[...]
        day_o[...] = (dy * h[...] * dgy).astype(day_o.dtype)
        A, B = _scan_rows(an[...], dy * gy, T, reverse=True)
        lmb = B + A * lcar[...]
        tmp[...] = lmb
        lcar[...] = tmp[pl.ds(0, 1), :]
        xc, r, gi, a = xc_r[...], r_r[...], i_r[...], a_r[...]
        sp = _softplus(-lam[...])
        la = -LRU_C * r * sp
        mult = jnp.sqrt(_neg_expm1(2.0 * la))
        da = lmb * hp[...]
        dmult = lmb * gi * xc
        di = lmb * mult * xc
        dxc = lmb * mult * gi
        dla = da * a - dmult * a * a / mult
        dr = dla * (-LRU_C * sp)
        dlam_o[...] += _colsum(dla * (LRU_C * r)) * _sigmoid(-lam[...])
        dpr = dr * r * (1.0 - r)
        dpi = di * gi * (1.0 - gi)
        dba_o[...] += _colsum(dpr)
        dbx_o[...] += _colsum(dpi)
        dxc_o[...] = dxc + _dot(dpr, wa[...], "nt") + _dot(dpi, wx[...], "nt")
        dwa_o[...] += _dot(xc, dpr, "tn")
        dwx_o[...] += _dot(xc, dpi, "tn")

    outs = pl.pallas_call(
        body,
        name="lru_bwd",
        grid=(nb,),
        in_specs=[rrow, pl.BlockSpec((T, BRANCH), lambda t: (nb - 1 - t, U_AY // BRANCH))] + [rrow] * 7
        + [sq, sq, vrow],
        out_specs=[rrow, rrow, sq, sq, vrow, vrow, vrow],
        out_shape=[jax.ShapeDtypeStruct((S, BRANCH), BF16), jax.ShapeDtypeStruct((S, BRANCH), F32),
                   jax.ShapeDtypeStruct((BRANCH, BRANCH), F32), jax.ShapeDtypeStruct((BRANCH, BRANCH), F32),
                   jax.ShapeDtypeStruct((1, BRANCH), F32), jax.ShapeDtypeStruct((1, BRANCH), F32),
                   jax.ShapeDtypeStruct((1, BRANCH), F32)],
        scratch_shapes=[pltpu.VMEM((1, BRANCH), F32), pltpu.VMEM((T, BRANCH), F32)],
        compiler_params=pltpu.CompilerParams(dimension_semantics=("arbitrary",), vmem_limit_bytes=VMEM_LIMIT_BYTES),
    )(dya, u, sv["h"], h_prev, sv["xc"], sv["r"], sv["i"], sv["a"], a_next, lw["wa"], lw["wx"], lw["lam"])
    return outs


def _conv_bwd(dxc, ax_shift, lw, S):
    dxs = [_shift_up(dxc, k) for k in range(4)]

    def fn(d0, d1, d2, d3, a0, a1, a2, a3, cw0, cw1, cw2, cw3):
        dax = cw3 * d0 + cw2 * d1 + cw1 * d2 + cw0 * d3
        return [dax], [_colsum(d0 * a3), _colsum(d0 * a2), _colsum(d0 * a1), _colsum(d0 * a0), _colsum(d0)]

    (dax,), sums = _rowwise("conv_bwd", fn, dxs + list(ax_shift), [lw["cw0"], lw["cw1"], lw["cw2"], lw["cw3"]],
                            [(BRANCH, BF16)], [BRANCH] * 5, S)
    return dax, sums


GLA_QK = GLA_HEADS * GLA_DK
GLA_V = GLA_HEADS * GLA_DV
GLA_SCALE = GLA_DK ** -0.5


def _gla_specs(TB, rev_nb=None):
    def rmap(t):
        return t if rev_nb is None else rev_nb - 1 - t
    return [
        pl.BlockSpec((TB, GLA_QK), lambda t: (rmap(t), U_BQ // GLA_QK)),
        pl.BlockSpec((TB, GLA_QK), lambda t: (rmap(t), U_BK // GLA_QK)),
        pl.BlockSpec((TB, GLA_V), lambda t: (rmap(t), U_BV // GLA_V)),
        pl.BlockSpec((TB, GLA_V), lambda t: (rmap(t), U_BR // GLA_V)),
        pl.BlockSpec((TB, LANES), lambda t: (rmap(t), U_BLOW // LANES)),
    ]


def _gla_gates(gl, wg2, bg, TB):
    pre = _dot(gl, wg2, "nn") + bg
    la = _log_sigmoid(pre) * (1.0 / GLA_TAU)
    _, gc = _scan_rows(None, la, TB, seg=CHUNK)
    return pre, la, gc


def _gla_fwd(u, gw, S):
    TB = min(512, S)
    nb = S // TB
    cpb = TB // CHUNK
    vecs = [gw["wg2"], gw["bg"], gw["ng"], gw["bd"]]

    def body(q_r, k_r, v_r, br_r, gl_r, wg2, bg, ng, bd, yb_o, oraw_o, st_o, st):
        t = pl.program_id(0)

        @pl.when(t == 0)
        def _():
            st[...] = jnp.zeros_like(st)

        _, la, gc = _gla_gates(gl_r[...], wg2[...], bg[...], TB)
        for c in range(cpb):
            sl = slice(c * CHUNK, (c + 1) * CHUNK)
            gt = _colsum(la[sl])
            kdec = k_r[sl, :] * jnp.exp(gt - gc[sl])
            d_t = _dot(v_r[sl, :], kdec, "tn") * bd[...]
            s_new = st[...] * jnp.exp(gt) + d_t
            st[...] = s_new
            st_o[c] = s_new
            oraw_o[sl, :] = _dot(q_r[sl, :] * GLA_SCALE, s_new, "nt")
        for h in range(GLA_HEADS):
            hs = slice(h * GLA_DV, (h + 1) * GLA_DV)
            oh = oraw_o[:, hs]
            on = oh * lax.rsqrt(jnp.mean(oh * oh, axis=-1, keepdims=True) + RMS_EPS)
            sil, _ = _silu_and_grad(br_r[:, hs])
            yb_o[:, hs] = (on * ng[:, hs] * sil).astype(yb_o.dtype)

    return pl.pallas_call(
        body,
        name="gla_fwd",
        grid=(nb,),
        in_specs=_gla_specs(TB) + [_full_spec(v) for v in vecs],
        out_specs=[pl.BlockSpec((TB, GLA_V), lambda t: (t, 0)), pl.BlockSpec((TB, GLA_V), lambda t: (t, 0)),
                   pl.BlockSpec((cpb, GLA_V, GLA_QK), lambda t: (t, 0, 0))],
        out_shape=[jax.ShapeDtypeStruct((S, GLA_V), BF16), jax.ShapeDtypeStruct((S, GLA_V), F32),
                   jax.ShapeDtypeStruct((S // CHUNK, GLA_V, GLA_QK), F32)],
        scratch_shapes=[pltpu.VMEM((GLA_V, GLA_QK), F32)],
        compiler_params=pltpu.CompilerParams(dimension_semantics=("arbitrary",), vmem_limit_bytes=VMEM_LIMIT_BYTES),
    )(u, u, u, u, u, *vecs)


def _gla_bwd(dyb, u, oraw, states, gw, S):
    TB = min(512, S)
    nb = S // TB
    cpb = TB // CHUNK
    vecs = [gw["wg2"], gw["bg"], gw["ng"], gw["bd"]]

    def rrow(width):
        return pl.BlockSpec((TB, width), lambda t: (nb - 1 - t, 0))

    def body(dyb_r, oraw_r, q_r, k_r, v_r, br_r, gl_r, st_r, sp_r, wg2, bg, ng, bd,
             dq_o, dk_o, dv_o, dbr_o, dgl_o, dwg2_o, dbg_o, dng_o, dcar, do_buf, dla_buf):
        t = pl.program_id(0)
        blk = nb - 1 - t

        @pl.when(t == 0)
        def _():
            dcar[...] = jnp.zeros_like(dcar)
            dwg2_o[...] = jnp.zeros_like(dwg2_o)
            dbg_o[...] = jnp.zeros_like(dbg_o)
            dng_o[...] = jnp.zeros_like(dng_o)

        pre, la, gc = _gla_gates(gl_r[...], wg2[...], bg[...], TB)
        for h in range(GLA_HEADS):
            hs = slice(h * GLA_DV, (h + 1) * GLA_DV)
            oh = oraw_r[:, hs]
            rs = lax.rsqrt(jnp.mean(oh * oh, axis=-1, keepdims=True) + RMS_EPS)
            on = oh * rs
            sil, dsil = _silu_and_grad(br_r[:, hs])
            dy = dyb_r[:, hs]
            dbr_o[:, hs] = (dy * on * ng[:, hs] * dsil).astype(dbr_o.dtype)
            don = dy * ng[:, hs] * sil
            dng_o[:, hs] += _colsum(dy * on * sil)
            do_buf[:, hs] = rs * (don - on * jnp.mean(don * on, axis=-1, keepdims=True))
        first = jnp.where(blk == 0, 0.0, 1.0)
        for c in reversed(range(cpb)):
            sl = slice(c * CHUNK, (c + 1) * CHUNK)
            s_n = st_r[c]
            s_prev = st_r[c - 1] if c > 0 else sp_r[0] * first
            gt = _colsum(la[sl])
            w = jnp.exp(gt - gc[sl])
            k_c = k_r[sl, :]
            kdec = k_c * w
            qs = q_r[sl, :] * GLA_SCALE
            do_c = do_buf[sl, :]
            dq_o[sl, :] = (_dot(do_c, s_n, "nn") * GLA_SCALE).astype(dq_o.dtype)
            d_n = _dot(do_c, qs, "tn") * bd[...] + dcar[...]
            dv_o[sl, :] = _dot(kdec, d_n, "nt").astype(dv_o.dtype)
            dkdec = _dot(v_r[sl, :], d_n, "nn")
            dk_o[sl, :] = (dkdec * w).astype(dk_o.dtype)
            tt = dkdec * kdec
            e = jnp.exp(gt)
            dgt = _colsum(tt) + _colsum(d_n * s_prev) * e
            _, rc = _scan_rows(None, -tt, CHUNK, reverse=True)
            dla_buf[sl, :] = rc + dgt
            dcar[...] = d_n * e
        dpre = dla_buf[...] * _sigmoid(-pre) * (1.0 / GLA_TAU)
        dbg_o[...] += _colsum(dpre)
        dgl_o[...] = _dot(dpre, wg2[...], "nt").astype(dgl_o.dtype)
        dwg2_o[...] += _dot(gl_r[...], dpre, "tn")

    return pl.pallas_call(
        body,
        name="gla_bwd",
        grid=(nb,),
        in_specs=[rrow(GLA_V), rrow(GLA_V)] + _gla_specs(TB, rev_nb=nb)
        + [pl.BlockSpec((cpb, GLA_V, GLA_QK), lambda t: (nb - 1 - t, 0, 0)),
           pl.BlockSpec((1, GLA_V, GLA_QK), lambda t: (jnp.maximum((nb - 1 - t) * cpb - 1, 0), 0, 0))]
        + [_full_spec(v) for v in vecs],
        out_specs=[rrow(GLA_QK), rrow(GLA_QK), rrow(GLA_V), rrow(GLA_V), rrow(LANES),
                   pl.BlockSpec((LANES, GLA_QK), lambda t: (0, 0)), pl.BlockSpec((1, GLA_QK), lambda t: (0, 0)),
                   pl.BlockSpec((1, GLA_V), lambda t: (0, 0))],
        out_shape=[jax.ShapeDtypeStruct((S, GLA_QK), BF16), jax.ShapeDtypeStruct((S, GLA_QK), BF16),
                   jax.ShapeDtypeStruct((S, GLA_V), BF16), jax.ShapeDtypeStruct((S, GLA_V), BF16),
                   jax.ShapeDtypeStruct((S, LANES), BF16), jax.ShapeDtypeStruct((LANES, GLA_QK), F32),
                   jax.ShapeDtypeStruct((1, GLA_QK), F32), jax.ShapeDtypeStruct((1, GLA_V), F32)],
        scratch_shapes=[pltpu.VMEM((GLA_V, GLA_QK), F32), pltpu.VMEM((TB, GLA_V), F32),
                        pltpu.VMEM((TB, GLA_QK), F32)],
        compiler_params=pltpu.CompilerParams(dimension_semantics=("arbitrary",), vmem_limit_bytes=VMEM_LIMIT_BYTES),
    )(dyb, oraw, u, u, u, u, u, states, states, *vecs)


FOX_SCALE = FOX_DH ** -0.5


def _fox_gate_fwd(u, bfp, S):
    T = min(512, S)

    def body(f_r, b_r, fc_o, car):
        t = pl.program_id(0)

        @pl.when(t == 0)
        def _():
            car[...] = jnp.zeros_like(car)

        _, cs = _scan_rows(None, _log_sigmoid(f_r[...] + b_r[...]), T)
        fc_o[...] = cs + car[...]
        car[...] = fc_o[pl.ds(T - 1, 1), :]

    return pl.pallas_call(
        body,
        name="fox_gate_fwd",
        grid=(S // T,),
        in_specs=[pl.BlockSpec((T, LANES), lambda t: (t, U_CF // LANES)), _full_spec(bfp)],
        out_specs=pl.BlockSpec((T, LANES), lambda t: (t, 0)),
        out_shape=jax.ShapeDtypeStruct((S, LANES), F32),
        scratch_shapes=[pltpu.VMEM((1, LANES), F32)],
        compiler_params=pltpu.CompilerParams(dimension_semantics=("arbitrary",), vmem_limit_bytes=VMEM_LIMIT_BYTES),
    )(u, bfp)


def _fox_gate_bwd(dfc, u, bfp, S):
    T = min(512, S)
    nb = S // T

    def body(d_r, f_r, b_r, df_o, db_o, car, tmp):
        t = pl.program_id(0)

        @pl.when(t == 0)
        def _():
            car[...] = jnp.zeros_like(car)
            db_o[...] = jnp.zeros_like(db_o)

        _, rc = _scan_rows(None, d_r[...], T, reverse=True)
        tmp[...] = rc + car[...]
        car[...] = tmp[pl.ds(0, 1), :]
        df = tmp[...] * _sigmoid(-(f_r[...] + b_r[...]))
        df_o[...] = df.astype(df_o.dtype)
        db_o[...] += _colsum(df)

    return pl.pallas_call(
        body,
        name="fox_gate_bwd",
        grid=(nb,),
        in_specs=[pl.BlockSpec((T, LANES), lambda t: (nb - 1 - t, 0)),
                  pl.BlockSpec((T, LANES), lambda t: (nb - 1 - t, U_CF // LANES)), _full_spec(bfp)],
        out_specs=[pl.BlockSpec((T, LANES), lambda t: (nb - 1 - t, 0)), pl.BlockSpec((1, LANES), lambda t: (0, 0))],
        out_shape=[jax.ShapeDtypeStruct((S, LANES), BF16), jax.ShapeDtypeStruct((1, LANES), F32)],
        scratch_shapes=[pltpu.VMEM((1, LANES), F32), pltpu.VMEM((T, LANES), F32)],
        compiler_params=pltpu.CompilerParams(dimension_semantics=("arbitrary",), vmem_limit_bytes=VMEM_LIMIT_BYTES),
    )(dfc, u, bfp)


def _fox_scores(q, k, fq, fk, qi, ki, tq, tk):
    s = _dot(q, k, "nt") * FOX_SCALE + (fq - fk)
    rows = lax.broadcasted_iota(jnp.int32, (tq, tk), 0) + qi * tq
    cols = lax.broadcasted_iota(jnp.int32, (tq, tk), 1) + ki * tk
    return jnp.where(cols <= rows, s, NEG_BIG)


def _fox_fwd(qh, kh, vh, fq, fk, S):
    tq = tk = min(512, S)
    nq, nk = S // tq, S // tk

    def body(q_r, k_r, v_r, fq_r, fk_r, o_o, lse_o, m_s, l_s, acc):
        qi, ki = pl.program_id(1), pl.program_id(2)

        @pl.when(ki == 0)
        def _():
            m_s[...] = jnp.full_like(m_s, NEG_BIG)
            l_s[...] = jnp.zeros_like(l_s)
            acc[...] = jnp.zeros_like(acc)

        @pl.when(ki <= qi)
        def _():
            s = _fox_scores(q_r[0], k_r[0], fq_r[0], fk_r[0], qi, ki, tq, tk)
            m_new = jnp.maximum(m_s[...], jnp.max(s, axis=-1, keepdims=True))
            p = jnp.exp(s - m_new)
            alpha = jnp.exp(m_s[...] - m_new)
            l_s[...] = alpha * l_s[...] + jnp.sum(p, axis=-1, keepdims=True)
            acc[...] = alpha * acc[...] + _dot(p, v_r[0], "nn")
            m_s[...] = m_new

        @pl.when(ki == nk - 1)
        def _():
            o_o[0] = acc[...] / l_s[...]
            lse_o[0] = m_s[...] + jnp.log(l_s[...])

    kv = pl.BlockSpec((1, tk, FOX_DH), lambda h, i, j: (h, jnp.minimum(j, i), 0))
    return pl.pallas_call(
        body,
        name="fox_fwd",
        grid=(FOX_HEADS, nq, nk),
        in_specs=[pl.BlockSpec((1, tq, FOX_DH), lambda h, i, j: (h, i, 0)), kv, kv,
                  pl.BlockSpec((1, tq, 1), lambda h, i, j: (h, i, 0)),
                  pl.BlockSpec((1, 1, tk), lambda h, i, j: (h, 0, jnp.minimum(j, i)))],
        out_specs=[pl.BlockSpec((1, tq, FOX_DH), lambda h, i, j: (h, i, 0)),
                   pl.BlockSpec((1, tq, 1), lambda h, i, j: (h, i, 0))],
        out_shape=[jax.ShapeDtypeStruct((FOX_HEADS, S, FOX_DH), F32), jax.ShapeDtypeStruct((FOX_HEADS, S, 1), F32)],
        scratch_shapes=[pltpu.VMEM((tq, 1), F32), pltpu.VMEM((tq, 1), F32), pltpu.VMEM((tq, FOX_DH), F32)],
        compiler_params=pltpu.CompilerParams(
            dimension_semantics=("parallel", "parallel", "arbitrary"), vmem_limit_bytes=VMEM_LIMIT_BYTES),
    )(qh, kh, vh, fq, fk)


def _fox_bwd_dq(qh, kh, vh, fq, fk, o, do, lse, S):
    tq = tk = min(512, S)
    nq, nk = S // tq, S // tk

    def body(q_r, k_r, v_r, fq_r, fk_r, o_r, do_r, lse_r, dq_o, dfq_o, dq_acc, df_acc):
        qi, ki = pl.program_id(1), pl.program_id(2)

        @pl.when(ki == 0)
        def _():
            dq_acc[...] = jnp.zeros_like(dq_acc)
            df_acc[...] = jnp.zeros_like(df_acc)

        @pl.when(ki <= qi)
        def _():
            s = _fox_scores(q_r[0], k_r[0], fq_r[0], fk_r[0], qi, ki, tq, tk)
            p = jnp.exp(s - lse_r[0])
            do_t = do_r[0]
            delta = jnp.sum(o_r[0] * do_t, axis=-1, keepdims=True)
            ds = p * (_dot(do_t, v_r[0], "nt") - delta)
            dq_acc[...] += _dot(ds, k_r[0], "nn")
            df_acc[...] += jnp.sum(ds, axis=-1, keepdims=True)

        @pl.when(ki == nk - 1)
        def _():
            dq_o[0] = dq_acc[...] * FOX_SCALE
            dfq_o[0] = df_acc[...]

    qrow = pl.BlockSpec((1, tq, FOX_DH), lambda h, i, j: (h, i, 0))
    qcol = pl.BlockSpec((1, tq, 1), lambda h, i, j: (h, i, 0))
    kv = pl.BlockSpec((1, tk, FOX_DH), lambda h, i, j: (h, jnp.minimum(j, i), 0))
    return pl.pallas_call(
        body,
        name="fox_bwd_dq",
        grid=(FOX_HEADS, nq, nk),
        in_specs=[qrow, kv, kv, qcol, pl.BlockSpec((1, 1, tk), lambda h, i, j: (h, 0, jnp.minimum(j, i))),
                  qrow, qrow, qcol],
        out_specs=[qrow, qcol],
        out_shape=[jax.ShapeDtypeStruct((FOX_HEADS, S, FOX_DH), F32), jax.ShapeDtypeStruct((FOX_HEADS, S, 1), F32)],
        scratch_shapes=[pltpu.VMEM((tq, FOX_DH), F32), pltpu.VMEM((tq, 1), F32)],
        compiler_params=pltpu.CompilerParams(
            dimension_semantics=("parallel", "parallel", "arbitrary"), vmem_limit_bytes=VMEM_LIMIT_BYTES),
    )(qh, kh, vh, fq, fk, o, do, lse)


def _fox_bwd_dkv(qh, kh, vh, fq, fk, o, do, lse, S):
    tq = tk = min(512, S)
    nq, nk = S // tq, S // tk

    def body(q_r, k_r, v_r, fq_r, fk_r, o_r, do_r, lse_r, dk_o, dv_o, dfk_o, dk_acc, dv_acc, df_acc):
        ki, qi = pl.program_id(1), pl.program_id(2)

        @pl.when(qi == 0)
        def _():
            dk_acc[...] = jnp.zeros_like(dk_acc)
            dv_acc[...] = jnp.zeros_like(dv_acc)
            df_acc[...] = jnp.zeros_like(df_acc)

        @pl.when(qi >= ki)
        def _():
            s = _fox_scores(q_r[0], k_r[0], fq_r[0], fk_r[0], qi, ki, tq, tk)
            p = jnp.exp(s - lse_r[0])
            do_t = do_r[0]
            delta = jnp.sum(o_r[0] * do_t, axis=-1, keepdims=True)
            ds = p * (_dot(do_t, v_r[0], "nt") - delta)
            dv_acc[...] += _dot(p, do_t, "tn")
            dk_acc[...] += _dot(ds, q_r[0], "tn")
            df_acc[...] += _colsum(ds)

        @pl.when(qi == nq - 1)
        def _():
            dk_o[0] = dk_acc[...] * FOX_SCALE
            dv_o[0] = dv_acc[...]
            dfk_o[0] = df_acc[...]

    qrow = pl.BlockSpec((1, tq, FOX_DH), lambda h, j, i: (h, jnp.maximum(i, j), 0))
    qcol = pl.BlockSpec((1, tq, 1), lambda h, j, i: (h, jnp.maximum(i, j), 0))
    kv = pl.BlockSpec((1, tk, FOX_DH), lambda h, j, i: (h, j, 0))
    krow = pl.BlockSpec((1, 1, tk), lambda h, j, i: (h, 0, j))
    return pl.pallas_call(
        body,
        name="fox_bwd_dkv",
        grid=(FOX_HEADS, nk, nq),
        in_specs=[qrow, kv, kv, qcol, krow, qrow, qrow, qcol],
        out_specs=[kv, kv, krow],
        out_shape=[jax.ShapeDtypeStruct((FOX_HEADS, S, FOX_DH), F32), jax.ShapeDtypeStruct((FOX_HEADS, S, FOX_DH), F32),
                   jax.ShapeDtypeStruct((FOX_HEADS, 1, S), F32)],
        scratch_shapes=[pltpu.VMEM((tk, FOX_DH), F32), pltpu.VMEM((tk, FOX_DH), F32), pltpu.VMEM((1, tk), F32)],
        compiler_params=pltpu.CompilerParams(
            dimension_semantics=("parallel", "parallel", "arbitrary"), vmem_limit_bytes=VMEM_LIMIT_BYTES),
    )(qh, kh, vh, fq, fk, o, do, lse)


FOX_TILE = 1024
FOX_AUG = 128
FOX_ONES = 3


def _fox_pairs(n, by_key):
    pairs = [(qi, ki) for qi in range(n) for ki in range(qi + 1)]
    if by_key:
        pairs.sort(key=lambda qk: (qk[1], qk[0]))
    qs = jnp.asarray([qk[0] for qk in pairs], jnp.int32)
    ks = jnp.asarray([qk[1] for qk in pairs], jnp.int32)
    return qs, ks


def _fox_augment(q, k, fcum):
    S = q.shape[0]
    def to_bf16_grid(a):
        return lax.reduce_precision(a, exponent_bits=8, mantissa_bits=7)

    hi = to_bf16_grid(fcum)
    mid = to_bf16_grid(fcum - hi)
    lo = to_bf16_grid(fcum - hi - mid)
    f3 = jnp.stack([hi, mid, lo], axis=-1).astype(BF16)
    ones = jnp.ones((S, FOX_HEADS, FOX_ONES), BF16)
    pad = jnp.zeros((S, FOX_HEADS, FOX_AUG - FOX_DH - 2 * FOX_ONES), BF16)
    q_aug = jnp.concatenate([(q * FOX_SCALE).astype(BF16), ones, f3, pad], axis=-1)
    k_aug = jnp.concatenate([k.astype(BF16), -f3, ones, pad], axis=-1)
    return jnp.transpose(q_aug, (1, 0, 2)), jnp.transpose(k_aug, (1, 0, 2))


def _fox_causal(sT):
    keys = lax.broadcasted_iota(jnp.int32, sT.shape, 0)
    queries = lax.broadcasted_iota(jnp.int32, sT.shape, 1)
    return jnp.where(keys <= queries, sT, NEG_BIG)


def _fox_fwd(qT, ka, vT, S):
    t = min(FOX_TILE, S)
    n = S // t
    qi_tab, ki_tab = _fox_pairs(n, by_key=False)

    def body(qi_ref, ki_ref, qT_r, ka_r, vT_r, oT_o, lse_o, m_s, l_s, acc):
        step = pl.program_id(1)
        qi, ki = qi_ref[step], ki_ref[step]

        @pl.when(ki == 0)
        def _():
            m_s[...] = jnp.full_like(m_s, NEG_BIG)
            l_s[...] = jnp.zeros_like(l_s)
            acc[...] = jnp.zeros_like(acc)

        def update(masked):
            sT = _dot(ka_r[0], qT_r[0], "nn")
            if masked:
                sT = _fox_causal(sT)
            m_new = jnp.maximum(m_s[...], jnp.max(sT, axis=0, keepdims=True))
            p = jnp.exp(sT - m_new)
            alpha = jnp.exp(m_s[...] - m_new)
            l_s[...] = alpha * l_s[...] + jnp.sum(p, axis=0, keepdims=True)
            acc[...] = alpha * acc[...] + _dot(vT_r[0], p, "nn")
            m_s[...] = m_new

        @pl.when(ki < qi)
        def _():
            update(False)

        @pl.when(ki == qi)
        def _():
            update(True)
            oT_o[0] = acc[...] / l_s[...]
            lse_o[0] = m_s[...] + jnp.log(l_s[...])

    return pl.pallas_call(
        body,
        name="fox_fwd",
        grid_spec=pltpu.PrefetchScalarGridSpec(
            num_scalar_prefetch=2, grid=(FOX_HEADS, int(qi_tab.shape[0])),
            in_specs=[pl.BlockSpec((1, FOX_AUG, t), lambda h, s, qt, kt: (h, 0, qt[s])),
                      pl.BlockSpec((1, t, FOX_AUG), lambda h, s, qt, kt: (h, kt[s], 0)),
                      pl.BlockSpec((1, FOX_DH, t), lambda h, s, qt, kt: (h, 0, kt[s]))],
            out_specs=[pl.BlockSpec((1, FOX_DH, t), lambda h, s, qt, kt: (h, 0, qt[s])),
                       pl.BlockSpec((1, 1, t), lambda h, s, qt, kt: (h, 0, qt[s]))],
            scratch_shapes=[pltpu.VMEM((1, t), F32), pltpu.VMEM((1, t), F32), pltpu.VMEM((FOX_DH, t), F32)]),
        out_shape=[jax.ShapeDtypeStruct((FOX_HEADS, FOX_DH, S), F32), jax.ShapeDtypeStruct((FOX_HEADS, 1, S), F32)],
        compiler_params=pltpu.CompilerParams(
            dimension_semantics=("parallel", "arbitrary"), vmem_limit_bytes=VMEM_LIMIT_BYTES),
    )(qi_tab, ki_tab, qT, ka, vT)


def _fox_bwd(qT, qa, ka, kT, v, do, doT, oT, lse, S):
    t = min(FOX_TILE, S)
    n = S // t
    qi_tab, ki_tab = _fox_pairs(n, by_key=True)

    def body(qi_ref, ki_ref, qT_r, qa_r, ka_r, kT_r, v_r, do_r, doT_r, oT_r, lse_r, dqT_o, dk_o, dv_o, dk_acc, dv_acc):
        step = pl.program_id(1)
        qi, ki = qi_ref[step], ki_ref[step]

        @pl.when(step == 0)
        def _():
            dqT_o[...] = jnp.zeros_like(dqT_o)

        @pl.when(qi == ki)
        def _():
            dk_acc[...] = jnp.zeros_like(dk_acc)
            dv_acc[...] = jnp.zeros_like(dv_acc)

        def update(masked):
            sT = _dot(ka_r[0], qT_r[0], "nn")
            if masked:
                sT = _fox_causal(sT)
            pT = jnp.exp(sT - lse_r[0])
            delta = jnp.sum(oT_r[0] * doT_r[0], axis=0, keepdims=True)
            dsT = pT * (_dot(v_r[0], doT_r[0], "nn") - delta)
            dv_acc[...] += _dot(pT, do_r[0], "nn")
            dk_acc[...] += _dot(dsT, qa_r[0], "nn")
            dqT_o[0, qi] += _dot(kT_r[0], dsT, "nn")

        @pl.when(qi > ki)
        def _():
            update(False)

        @pl.when(qi == ki)
        def _():
            update(True)

        @pl.when(qi == n - 1)
        def _():
            dk_o[0] = dk_acc[...].T
            dv_o[0] = dv_acc[...]

    def qlane(rows):
        return pl.BlockSpec((1, rows, t), lambda h, s, qt, kt: (h, 0, qt[s]))

    def qrow(cols):
        return pl.BlockSpec((1, t, cols), lambda h, s, qt, kt: (h, qt[s], 0))

    def krow(cols):
        return pl.BlockSpec((1, t, cols), lambda h, s, qt, kt: (h, kt[s], 0))

    return pl.pallas_call(
        body,
        name="fox_bwd",
        grid_spec=pltpu.PrefetchScalarGridSpec(
            num_scalar_prefetch=2, grid=(FOX_HEADS, int(qi_tab.shape[0])),
            in_specs=[qlane(FOX_AUG), qrow(FOX_AUG), krow(FOX_AUG),
                      pl.BlockSpec((1, FOX_AUG, t), lambda h, s, qt, kt: (h, 0, kt[s])), krow(FOX_DH), qrow(FOX_DH),
                      qlane(FOX_DH), qlane(FOX_DH), qlane(1)],
            out_specs=[pl.BlockSpec((1, n, FOX_AUG, t), lambda h, s, qt, kt: (h, 0, 0, 0)),
                       pl.BlockSpec((1, FOX_AUG, t), lambda h, s, qt, kt: (h, 0, kt[s])), krow(FOX_DH)],
            scratch_shapes=[pltpu.VMEM((t, FOX_AUG), F32), pltpu.VMEM((t, FOX_DH), F32)]),
        out_shape=[jax.ShapeDtypeStruct((FOX_HEADS, n, FOX_AUG, t), F32),
                   jax.ShapeDtypeStruct((FOX_HEADS, FOX_AUG, S), F32),
                   jax.ShapeDtypeStruct((FOX_HEADS, S, FOX_DH), F32)],
        compiler_params=pltpu.CompilerParams(
            dimension_semantics=("parallel", "arbitrary"), vmem_limit_bytes=VMEM_LIMIT_BYTES),
    )(qi_tab, ki_tab, qT, qa, ka, kT, v, do, doT, oT, lse)


def _to_heads(x2d, S):
    return jnp.transpose(x2d.reshape(S, FOX_HEADS, FOX_DH), (1, 0, 2))


def _from_heads(xh, S):
    return jnp.transpose(xh, (1, 0, 2)).reshape(S, FOX_HEADS * FOX_DH)


def _ffn_fwd(tag, x, wgT, wuT, wd, g, b, S):
    def up_epi(accs):
        gate, up = accs
        sil, _ = _silu_and_grad(gate)
        return [gate, up, sil * up]

    gate, up, act = _mm(tag + "_up", "nt", [x], [wgT, wuT], [(0, 0, 0), (1, 0, 1)], 2, up_epi, [],
                        [BF16, BF16, BF16], S, D_FF, D_MODEL, tn=1408)

    def down_epi(accs, xr, gg, bb):
        z = ALPHA * xr + 0.5 * accs[0]
        return [z, _ln_fwd(z, gg, bb)]

    z, xn = _mm(tag + "_down", "nn", [act], [wd], [(0, 0, 0)], 1, down_epi, [(x, "mn", 0), (g, "n"), (b, "n")],
                [F32, F32], S, D_MODEL, D_FF, tk=D_FF)
    return xn, dict(x=x, gate=gate, up=up, act=act, z=z)


def _ln_bwd_call(tag, dy, z, g, S):
    def fn(dy_t, z_t, g_t):
        dz, xhat = _ln_bwd(dy_t, z_t, g_t)
        return [dz], [_colsum(dy_t * xhat), _colsum(dy_t)]

    (dz,), (dg, db) = _rowwise(tag + "_ln_bwd", fn, [dy, z], [g], [(D_MODEL, F32)], [D_MODEL, D_MODEL], S)
    return dz, dg, db


def _ffn_bwd(tag, dxn, sv, wgT, wuT, wd, g, S):
    dz, dg, db = _ln_bwd_call(tag, dxn, sv["z"], g, S)

    def act_epi(accs, gate, up):
        da = 0.5 * accs[0]
        sil, dsil = _silu_and_grad(gate.astype(F32))
        return [da * up.astype(F32) * dsil, da * sil]

    dgate, dup = _mm(tag + "_dact", "nt", [dz], [wd], [(0, 0, 0)], 1, act_epi,
                     [(sv["gate"], "mn", 0), (sv["up"], "mn", 0)], [BF16, BF16], S, D_FF, D_MODEL, tn=1408)
    dwd = _mm1(tag + "_dwd", "tn", sv["act"], dz, D_FF, D_MODEL, S, scale=0.5, tm=1408)

    def two(accs):
        return [accs[0], accs[1]]

    dwgT, dwuT = _mm(tag + "_dwup", "tn", [dgate, dup], [sv["x"]], [(0, 0, 0), (1, 1, 0)], 2, two, [], [F32, F32],
                     D_FF, D_MODEL, S, tm=1408, tk=512)

    def dx_epi(accs, dzr):
        return [accs[0] + ALPHA * dzr]

    (dx,) = _mm(tag + "_dx", "nn", [dgate, dup], [wgT, wuT], [(0, 0, 0), (0, 1, 1)], 1, dx_epi, [(dz, "mn", 0)],
                [F32], S, D_MODEL, D_FF, tk=1408)
    return dx, dict(w_upT=jnp.concatenate([dwgT, dwuT], axis=0), w_down=dwd, ln_g=dg, ln_b=db)


def _mixer_fwd(x1, w, S):
    u = _mm1("w_in", "nt", x1, w["w_inT_p"], S, U_WIDTH, D_MODEL, tn=1536)
    ax0 = u[:, U_AX:U_AX + BRANCH]
    ax_shift = [ax0] + [_shift_down(ax0, k) for k in (1, 2, 3)]
    xc, r, gi, a, h, y_a = _lru_fwd(u, ax_shift, w["lru"], S)
    y_b, oraw, states = _gla_fwd(u, w["gla"], S)
    fcum = _fox_gate_fwd(u, w["bfp"], S)
    heads = (S, FOX_HEADS, FOX_DH)
    cq, ck, cv = lax.optimization_barrier(
        (u[:, U_CQ:U_CQ + BRANCH], u[:, U_CK:U_CK + BRANCH], u[:, U_CV:U_CV + BRANCH]))
    qa, ka = _fox_augment(cq.reshape(heads), ck.reshape(heads), fcum[:, :FOX_HEADS])
    vh = _to_heads(cv.astype(BF16), S)
    qT, kT, vT = (jnp.swapaxes(a, 1, 2) for a in (qa, ka, vh))
    oT, lse = _fox_fwd(qT, ka, vT, S)
    y_c = jnp.transpose(oT, (2, 0, 1)).reshape(S, BRANCH).astype(BF16)

    def merge_epi(accs, g0, g1, g2):
        merged = _sigmoid(g0) * accs[0] + _sigmoid(g1) * accs[1] + _sigmoid(g2) * accs[2]
        return [accs[0], accs[1], accs[2], merged]

    wb = w["w_branchT"]
    yp0, yp1, yp2, merged = _mm(
        "merge", "nt", [y_a, y_b, y_c], [wb[0], wb[1], wb[2]], [(0, 0, 0), (1, 1, 1), (2, 2, 2)], 3, merge_epi,
        [(u, "mn", 0), (u, "mn", 1), (u, "mn", 2)], [F32, F32, F32, BF16], S, D_MODEL, BRANCH, tm=256)

    def out_epi(accs, xr, gg, bb):
        z = ALPHA * xr + accs[0]
        return [z, _ln_fwd(z, gg, bb)]

    z2, x2 = _mm("w_out", "nn", [merged], [w["w_out"]], [(0, 0, 0)], 1, out_epi,
                 [(x1, "mn", 0), (w["ln2_g"], "n"), (w["ln2_b"], "n")], [F32, F32], S, D_MODEL, D_MODEL)
    sv = dict(x=x1, u=u, ax_shift=ax_shift, xc=xc, r=r, i=gi, a=a, h=h, y_a=y_a, y_b=y_b, y_c=y_c, oraw=oraw,
              states=states, qT=qT, qa=qa, ka=ka, kT=kT, vh=vh, oT=oT, lse=lse, yp=(yp0, yp1, yp2), merged=merged,
              z=z2)
    return x2, sv


def _mixer_bwd(dx2, sv, w, S):
    u = sv["u"]
    dz, dg2, db2 = _ln_bwd_call("mix", dx2, sv["z"], w["ln2_g"], S)

    def dm_epi(accs, y0, y1, y2, g0, g1, g2):
        dm = accs[0]
        outs_p, outs_g = [], []
        for yp, gl in ((y0, g0), (y1, g1), (y2, g2)):
            sg = _sigmoid(gl)
            outs_p.append(dm * sg)
            outs_g.append(dm * yp * sg * (1.0 - sg))
        return outs_p + outs_g

    yp = sv["yp"]
    dyp0, dyp1, dyp2, dgl0, dgl1, dgl2 = _mm(
        "dmerged", "nt", [dz], [w["w_out"]], [(0, 0, 0)], 1, dm_epi,
        [(yp[0], "mn", 0), (yp[1], "mn", 0), (yp[2], "mn", 0), (u, "mn", 0), (u, "mn", 1), (u, "mn", 2)],
        [BF16] * 6, S, D_MODEL, D_MODEL, tm=256)
    dw_out = _mm1("dw_out", "tn", sv["merged"], dz, D_MODEL, D_MODEL, S)
    wb = w["w_branchT"]
    dys, dwbs = [], []
    for j, (yj, dyp) in enumerate(((sv["y_a"], dyp0), (sv["y_b"], dyp1), (sv["y_c"], dyp2))):
        dys.append(_mm1("dy_branch%d" % j, "nn", dyp, wb[j], S, BRANCH, D_MODEL))
        dwbs.append(_mm1("dw_branch%d" % j, "tn", dyp, yj, D_MODEL, BRANCH, S))
    day, dxc, dwa, dwx, dba, dbx, dlam = _lru_bwd(dys[0], u, sv, w["lru"], S)
    dax, (dcw0, dcw1, dcw2, dcw3, dcb) = _conv_bwd(dxc, sv["ax_shift"], w["lru"], S)
    dbq, dbk, dbv, dbr, dglow, dwg2p, dbg, dng = _gla_bwd(dys[1], u, sv["oraw"], sv["states"], w["gla"], S)
    doh = _to_heads(dys[2], S)
    dqT_aug, dkT_aug, dvh = _fox_bwd(sv["qT"], sv["qa"], sv["ka"], sv["kT"], sv["vh"], doh, jnp.swapaxes(doh, 1, 2),
                                    sv["oT"], sv["lse"], S)
    dqT_aug = jnp.swapaxes(dqT_aug, 1, 2).reshape(FOX_HEADS, FOX_AUG, S)
    dqh = jnp.swapaxes(dqT_aug[:, :FOX_DH, :], 1, 2) * FOX_SCALE
    dkh = jnp.swapaxes(dkT_aug[:, :FOX_DH, :], 1, 2)
    dfc = jnp.transpose(dqT_aug[:, FOX_DH + FOX_ONES, :] - dkT_aug[:, FOX_DH, :])
    dfc = jnp.pad(dfc, ((0, 0), (0, LANES - FOX_HEADS)))
    dcf, dbf = _fox_gate_bwd(dfc, u, w["bfp"], S)
    du = jnp.concatenate(
        [dgl0, dgl1, dgl2, dax, day, dbq, dbk, dbv, dbr, _from_heads(dqh, S).astype(BF16),
         _from_heads(dkh, S).astype(BF16), _from_heads(dvh, S).astype(BF16), dglow, dcf,
         jnp.zeros((S, U_WIDTH - U_CF - LANES), BF16)], axis=1)
    dw_inT_p = _mm1("dw_in", "tn", du, sv["x"], U_WIDTH, D_MODEL, S, tm=1536)

    def dx_epi(accs, dzr):
        return [accs[0] + ALPHA * dzr]

    (dx1,) = _mm("dx_mix", "nn", [du], [w["w_inT_p"]], [(0, 0, 0)], 1, dx_epi, [(dz, "mn", 0)], [F32], S, D_MODEL,
                 U_WIDTH, tk=1536)
    pieces = sorted(W_IN_SEGMENTS)
    dw_inT = jnp.concatenate([dw_inT_p[dst:dst + width] for _, width, dst in pieces], axis=0)
    eye = jnp.eye(LRU_BLOCKS, dtype=F32)
    dwa_b = jnp.einsum("ncmd,nm->ncd", dwa.reshape(LRU_BLOCKS, 64, LRU_BLOCKS, 64), eye)
    dwx_b = jnp.einsum("ncmd,nm->ncd", dwx.reshape(LRU_BLOCKS, 64, LRU_BLOCKS, 64), eye)
    grads = dict(
        w_inT=dw_inT, w_out=dw_out, w_branchT=jnp.stack(dwbs), ln2_g=dg2, ln2_b=db2,
        conv_w=jnp.concatenate([dcw0, dcw1, dcw2, dcw3], axis=0), conv_b=dcb, lru_wa=dwa_b, lru_wx=dwx_b,
        lru_ba=dba, lru_bx=dbx, lru_lambda=dlam, gla_w_g2=dwg2p[:GLA_LOWRANK], gla_b_g=dbg, gla_norm_g=dng,
        fox_b_f=dbf[:, :FOX_HEADS])
    return dx1, grads


def _ple_fwd(x3, p_i, w, S):
    pe = _mm1("ple_proj", "nt", p_i, w["ple_w_projT"], S, D_MODEL, PLE_DIM)

    def epi(accs, xr, per, bg, gg, bb):
        sg = _sigmoid(accs[0] + bg)
        z = ALPHA * xr + sg * per
        return [sg, z, _ln_fwd(z, gg, bb)]

    sg, z4, x4 = _mm("ple_gate", "nn", [x3], [w["ple_w_gate"]], [(0, 0, 0)], 1, epi,
                     [(x3, "mn", 0), (pe, "mn", 0), (w["ple_b_gate"], "n"), (w["ln4_g"], "n"), (w["ln4_b"], "n")],
                     [F32, F32, F32], S, D_MODEL, D_MODEL)
    return x4, dict(x=x3, p=p_i, pe=pe, sg=sg, z=z4)


def _ple_bwd(dx4, sv, w, S):
    def fn(dy_t, z_t, pe_t, sg_t, g_t):
        dz, xhat = _ln_bwd(dy_t, z_t, g_t)
        dgl = dz * pe_t * sg_t * (1.0 - sg_t)
        return [dz, dz * sg_t, dgl], [_colsum(dy_t * xhat), _colsum(dy_t), _colsum(dgl)]

    (dz, dpe, dgl), (dg4, db4, dbg) = _rowwise(
        "ple_bwd", fn, [dx4, sv["z"], sv["pe"], sv["sg"]], [w["ln4_g"]],
        [(D_MODEL, F32), (D_MODEL, BF16), (D_MODEL, BF16)], [D_MODEL] * 3, S)
    dwpT = _mm1("dw_ple_proj", "tn", dpe, sv["p"], D_MODEL, PLE_DIM, S)
    dwg = _mm1("dw_ple_gate", "tn", sv["x"], dgl, D_MODEL, D_MODEL, S)

    def dx_epi(accs, dzr):
        return [accs[0] + ALPHA * dzr]

    (dx3,) = _mm("dx_ple", "nt", [dgl], [w["ple_w_gate"]], [(0, 0, 0)], 1, dx_epi, [(dz, "mn", 0)], [F32], S,
                 D_MODEL, D_MODEL)
    return dx3, dict(ple_w_projT=dwpT, ple_w_gate=dwg, ple_b_gate=dbg, ln4_g=dg4, ln4_b=db4)


def _rows_of_all(g):
    return g.reshape((g.shape[0] * g.shape[1],) + g.shape[2:])


def _layer_weights(gathered, full, i):
    w = {}
    for tag in ("ffn1", "ffn2"):
        upT = _rows_of_all(gathered[tag + "_w_up"])
        w[tag] = (upT[:D_FF], upT[D_FF:], _rows_of_all(gathered[tag + "_w_down"]))
    w_inT = _rows_of_all(gathered["w_in"])
    placed = sorted((dst, src, width) for src, width, dst in W_IN_SEGMENTS)
    parts, pos = [], 0
    for dst, src, width in placed:
        if dst > pos:
            parts.append(jnp.zeros((dst - pos, D_MODEL), w_inT.dtype))
        parts.append(w_inT[src:src + width])
        pos = dst + width
    parts.append(jnp.zeros((U_WIDTH - pos, D_MODEL), w_inT.dtype))
    w["w_inT_p"] = jnp.concatenate(parts, axis=0)
    eye = jnp.eye(LRU_BLOCKS, dtype=F32)

    def dense(blocks):
        return jnp.einsum("ncd,nm->ncmd", blocks, eye).reshape(BRANCH, BRANCH).astype(BF16)

    def vec(name):
        return full[name][i].reshape(1, -1)

    cw = jnp.moveaxis(gathered["conv_w"], 0, 1).reshape(4, BRANCH)
    w_g2 = jnp.moveaxis(gathered["gla_w_g2"], 0, 1).reshape(GLA_LOWRANK, GLA_QK)
    w["lru"] = dict(cw0=cw[0:1], cw1=cw[1:2], cw2=cw[2:3], cw3=cw[3:4], conv_b=vec("conv_b"),
                    wa=dense(full["lru_wa"][i]), wx=dense(full["lru_wx"][i]), ba=vec("lru_ba"), bx=vec("lru_bx"),
                    lam=vec("lru_lambda"))
    hq = jnp.arange(GLA_QK) // GLA_DK
    hv = jnp.arange(GLA_V) // GLA_DV
    w["gla"] = dict(wg2=jnp.pad(w_g2, ((0, LANES - GLA_LOWRANK), (0, 0))).astype(BF16),
                    bg=vec("gla_b_g"), ng=vec("gla_norm_g"), bd=(hv[:, None] == hq[None, :]).astype(F32))
    w["bfp"] = jnp.pad(vec("fox_b_f"), ((0, 0), (0, LANES - FOX_HEADS)))
    w["w_branchT"] = jnp.moveaxis(gathered["w_branch"], 0, 1).reshape(3, D_MODEL, BRANCH)
    w["w_out"] = _rows_of_all(gathered["w_out"])
    w["ple_w_projT"] = _rows_of_all(gathered["ple_w_proj"])
    w["ple_w_gate"] = _rows_of_all(gathered["ple_w_gate"])
    for name in ("ln1_g", "ln1_b", "ln2_g", "ln2_b", "ln3_g", "ln3_b", "ln4_g", "ln4_b", "ple_b_gate"):
        w[name] = vec(name)
    return w


def _layer_fwd(x0, p_i, w, S):
    x1, s1 = _ffn_fwd("ffn1", x0, *w["ffn1"], w["ln1_g"], w["ln1_b"], S)
    x2, s2 = _mixer_fwd(x1, w, S)
    x3, s3 = _ffn_fwd("ffn2", x2, *w["ffn2"], w["ln3_g"], w["ln3_b"], S)
    x4, s4 = _ple_fwd(x3, p_i, w, S)
    return x4, (s1, s2, s3, s4)


def _layer_bwd(dx4, saved, w, S):
    s1, s2, s3, s4 = saved
    dx3, g4 = _ple_bwd(dx4, s4, w, S)
    dx2, g3 = _ffn_bwd("ffn2", dx3, s3, *w["ffn2"], w["ln3_g"], S)
    dx1, g2 = _mixer_bwd(dx2, s2, w, S)
    dx0, g1 = _ffn_bwd("ffn1", dx1, s1, *w["ffn1"], w["ln1_g"], S)
    grads = dict(g2)
    grads.update(g4)
    grads.update(ffn1_w_upT=g1["w_upT"], ffn1_w_down=g1["w_down"], ln1_g=g1["ln_g"], ln1_b=g1["ln_b"],
                 ffn2_w_upT=g3["w_upT"], ffn2_w_down=g3["w_down"], ln3_g=g3["ln_g"], ln3_b=g3["ln_b"])
    return dx0, grads


def _local_step(x, p, target, gathered, full):
    S = x.shape[0]
    ws = [_layer_weights(gathered[i], full, i) for i in range(DEPTH)]
    saved = []
    h = x
    for i in range(DEPTH):
        h, sv = _layer_fwd(h, p[i], ws[i], S)
        saved.append(sv)

    def loss_fn(y, t):
        err = y - t
        return [err * (1.0 / D_MODEL)], [_colsum(err * err) * (0.5 / D_MODEL)]

    (dy,), (lsum,) = _rowwise("loss", loss_fn, [h, target], [], [(D_MODEL, F32)], [D_MODEL], S)
    loss = jnp.sum(lsum)
    layer_grads = [None] * DEPTH
    for i in reversed(range(DEPTH)):
        dy, layer_grads[i] = _layer_bwd(dy, saved[i], ws[i], S)
    return loss, dy, layer_grads


def kernel(x, p, ffn1_w_up, ffn1_w_down, ln1_g, ln1_b, w_in, conv_w, conv_b, lru_wa, lru_ba, lru_wx, lru_bx, lru_lambda, gla_w_g2, gla_b_g, gla_norm_g, fox_b_f, w_branch, w_out, ln2_g, ln2_b, ffn2_w_up, ffn2_w_down, ln3_g, ln3_b, ple_w_proj, ple_w_gate, ple_b_gate, ln4_g, ln4_b, loss_target, m_ffn1_w_up, m_ffn1_w_down, m_ln1_g, m_ln1_b, m_w_in, m_conv_w, m_conv_b, m_lru_wa, m_lru_ba, m_lru_wx, m_lru_bx, m_lru_lambda, m_gla_w_g2, m_gla_b_g, m_gla_norm_g, m_fox_b_f, m_w_branch, m_w_out, m_ln2_g, m_ln2_b, m_ffn2_w_up, m_ffn2_w_down, m_ln3_g, m_ln3_b, m_ple_w_proj, m_ple_w_gate, m_ple_b_gate, m_ln4_g, m_ln4_b, v_ffn1_w_up, v_ffn1_w_down, v_ln1_g, v_ln1_b, v_w_in, v_conv_w, v_conv_b, v_lru_wa, v_lru_ba, v_lru_wx, v_lru_bx, v_lru_lambda, v_gla_w_g2, v_gla_b_g, v_gla_norm_g, v_fox_b_f, v_w_branch, v_w_out, v_ln2_g, v_ln2_b, v_ffn2_w_up, v_ffn2_w_down, v_ln3_g, v_ln3_b, v_ple_w_proj, v_ple_w_gate, v_ple_b_gate, v_ln4_g, v_ln4_b):
    env = dict(locals())
    wts = {n: env[n] for n in WEIGHTS}
    ms = {n: env["m_" + n] for n in WEIGHTS}
    vs = {n: env["v_" + n] for n in WEIGHTS}
    sharded = [n for n, _ in SHARDED]
    keys = [(i, n) for i in range(DEPTH) for n in sharded]

    def travel(n, a):
        return jnp.swapaxes(a, -1, -2) if n in COLUMN_SHARDED else a

    shards = [travel(n, wts[n][i]) for i, n in keys]
    shards = [a if n in SHARDED_F32_GATHER else a.astype(BF16) for (i, n), a in zip(keys, shards)]
    gathered = [{} for _ in range(DEPTH)]
    for (i, n), g in zip(keys, _allgather_multi("gather_weights", shards)):
        gathered[i][n] = g
    full = {n: wts[n] for n in REPLICATED}

    loss_part, grad_x, layer_grads = _local_step(x[0], p[:, 0], loss_target[0], gathered, full)
    loss = lax.psum(loss_part, MESH_AXES)

    dest = [_dest_pieces(n, layer_grads[i][n + "T" if n in COLUMN_SHARDED else n]) for i, n in keys]
    got = _sibling_swap_multi("grad_sibling_swap", dest)
    core = lax.axis_index("c").astype(jnp.int32).reshape(1)
    pairs = [_pair_add("grad_pair_add_%s_%d" % (n, i), core, _as_rows(d, 2), _as_rows(g, 1))
             for (i, n), d, g in zip(keys, dest, got)]
    parts = _chip_all_to_all_multi("grad_chip_all_to_all", pairs)
    rep = list(REPLICATED)
    rep_grads = [jnp.stack([layer_grads[i][n] for i in range(DEPTH)]).reshape(wts[n].shape) for n in rep]
    (gr,) = _allgather_multi("grad_gather_replicated", [_pack(rep_grads, F32)])

    per_layer = {}
    for (i, n), gp, sh in zip(keys, parts, shards):
        gp = travel(n, gp.reshape((4,) + sh.shape))
        res = _adamw("adamw_%s_%d" % (n, i), _as_rows(gp, 1), _as_rows(wts[n][i], 0), _as_rows(ms[n][i], 0),
                     _as_rows(vs[n][i], 0))
        per_layer[i, n] = [arr.reshape(wts[n].shape[1:]) for arr in res]
    out = {}
    for n in sharded:
        for k, kind in enumerate(("grad", "delta", "new_m", "new_v")):
            out[kind + "_" + n] = jnp.stack([per_layer[i, n][k] for i in range(DEPTH)])
    res = _adamw("adamw_replicated", gr, _pack([wts[n] for n in rep], F32), _pack([ms[n] for n in rep], F32),
                 _pack([vs[n] for n in rep], F32))
    shapes = [wts[n].shape for n in rep]
    for kind, buf in zip(("grad", "delta", "new_m", "new_v"), res):
        for n, arr in zip(rep, _unpack(buf, shapes)):
            out[kind + "_" + n] = arr
    return (loss, grad_x[None], *[out["grad_" + n] for n in WEIGHTS], *[out["delta_" + n] for n in WEIGHTS],
            *[out["new_m_" + n] for n in WEIGHTS], *[out["new_v_" + n] for n in WEIGHTS])
```

```python
import functools
import math

import jax
import jax.numpy as jnp
from jax import lax
from jax.experimental import pallas as pl
from jax.experimental.pallas import tpu as pltpu

F32 = jnp.float32
BF16 = jnp.bfloat16

N_DEV = 8
MESH_AXES = ("x", "y", "c")
DEPTH = 2
D_MODEL = 1024
D_FF = 2816
BRANCH = 512
CHUNK = 64
GLA_HEADS = 4
GLA_DK = 64
GLA_DV = 128
GLA_LOWRANK = 16
GLA_TAU = 16.0
FOX_HEADS = 8
FOX_DH = 64
PLE_DIM = 256
LRU_C = 8.0
LRU_BLOCKS = 8
LN_EPS = 1e-5
RMS_EPS = 1e-6
ALPHA = (2 * DEPTH) ** 0.25
LANES = 128
NEG_BIG = -1e30

ADAM_LR = 0.001
ADAM_B1 = 0.9
ADAM_B2 = 0.999
ADAM_EPS = 1e-08
ADAM_WD = 0.01
ADAM_STEP = 10

VMEM_LIMIT_BYTES = 56 * 1024 * 1024

U_GATES = 0
U_AX = 3072
U_AY = 3584
U_BQ = 4096
U_BK = 4352
U_BV = 4608
U_BR = 5120
U_CQ = 5632
U_CK = 6144
U_CV = 6656
U_BLOW = 7168
U_CF = 7296
U_WIDTH = 7680
W_IN_SEGMENTS = (
    (0, 512, U_AX), (512, 512, U_AY), (1024, 256, U_BQ), (1280, 256, U_BK), (1536, 512, U_BV),
    (2048, 16, U_BLOW), (2064, 512, U_BR), (2576, 512, U_CQ), (3088, 512, U_CK), (3600, 512, U_CV),
    (4112, 8, U_CF), (4120, 3072, U_GATES),
)

SHARDED = (
    ("ffn1_w_up", 2), ("ffn1_w_down", 1), ("w_in", 2), ("conv_w", 2), ("gla_w_g2", 2), ("w_branch", 3),
    ("w_out", 1), ("ffn2_w_up", 2), ("ffn2_w_down", 1), ("ple_w_proj", 2), ("ple_w_gate", 1),
)
SHARDED_F32_GATHER = ("conv_w", "gla_w_g2")
COLUMN_SHARDED = ("ffn1_w_up", "ffn2_w_up", "w_in", "w_branch", "ple_w_proj")
REPLICATED = ("ln1_g", "ln1_b", "conv_b", "lru_wa", "lru_ba", "lru_wx", "lru_bx", "lru_lambda", "gla_b_g",
              "gla_norm_g", "fox_b_f", "ln2_g", "ln2_b", "ln3_g", "ln3_b", "ple_b_gate", "ln4_g", "ln4_b")
WEIGHTS = ("ffn1_w_up", "ffn1_w_down", "ln1_g", "ln1_b", "w_in", "conv_w", "conv_b", "lru_wa", "lru_ba", "lru_wx",
           "lru_bx", "lru_lambda", "gla_w_g2", "gla_b_g", "gla_norm_g", "fox_b_f", "w_branch", "w_out", "ln2_g",
           "ln2_b", "ffn2_w_up", "ffn2_w_down", "ln3_g", "ln3_b", "ple_w_proj", "ple_w_gate", "ple_b_gate", "ln4_g",
           "ln4_b")


def _sigmoid(x):
    return 1.0 / (1.0 + jnp.exp(-x))


def _log1p_pos(e):
    return jnp.where(e < 1e-4, e * (1.0 - 0.5 * e), jnp.log(1.0 + e))


def _softplus(x):
    return jnp.maximum(x, 0.0) + _log1p_pos(jnp.exp(-jnp.abs(x)))


def _log_sigmoid(x):
    return -_softplus(-x)


def _neg_expm1(y):
    series = -y * (1.0 + y * (0.5 + y * (1.0 / 6.0 + y * (1.0 / 24.0 + y * (1.0 / 120.0)))))
    return jnp.where(y > -0.1, series, 1.0 - jnp.exp(y))


def _silu_and_grad(x):
    s = _sigmoid(x)
    return x * s, s * (1.0 + x * (1.0 - s))


_GELU_C = math.sqrt(2.0 / math.pi)


def _gelu_and_grad(x):
    inner = _GELU_C * (x + 0.044715 * x * x * x)
    t = jnp.tanh(inner)
    g = 0.5 * x * (1.0 + t)
    dg = 0.5 * (1.0 + t) + 0.5 * x * (1.0 - t * t) * _GELU_C * (1.0 + 3.0 * 0.044715 * x * x)
    return g, dg


def _ln_stats(z):
    mu = jnp.mean(z, axis=-1, keepdims=True)
    zc = z - mu
    var = jnp.mean(zc * zc, axis=-1, keepdims=True)
    rstd = lax.rsqrt(var + LN_EPS)
    return zc * rstd, rstd


def _ln_fwd(z, g, b):
    xhat, _ = _ln_stats(z)
    return xhat * g + b


def _ln_bwd(dy, z, g):
    xhat, rstd = _ln_stats(z)
    dxh = dy * g
    m1 = jnp.mean(dxh, axis=-1, keepdims=True)
    m2 = jnp.mean(dxh * xhat, axis=-1, keepdims=True)
    return rstd * (dxh - m1 - xhat * m2), xhat


def _colsum(x):
    return jnp.sum(x, axis=0, keepdims=True)


def _dot(a, b, dims):
    dn = {"nn": (((1,), (0,)), ((), ())), "nt": (((1,), (1,)), ((), ())), "tn": (((0,), (0,)), ((), ()))}[dims]
    return lax.dot_general(a.astype(BF16), b.astype(BF16), dn, preferred_element_type=F32)


def _scan_rows(a, b, length, reverse=False, seg=None):
    rows = lax.broadcasted_iota(jnp.int32, b.shape, 0)
    span = seg if seg else length
    pos = rows % span if seg else rows
    d = 1
    while d < span:
        shift = (length - d) if reverse else d
        valid = (pos < span - d) if reverse else (pos >= d)
        sb = jnp.where(valid, pltpu.roll(b, shift, 0), 0.0)
        if a is None:
            b = b + sb
        else:
            b = b + a * sb
            a = a * jnp.where(valid, pltpu.roll(a, shift, 0), 1.0)
        d *= 2
    return a, b


def _tile(dim, pref):
    if dim <= pref:
        return dim
    best = None
    t = LANES
    while t <= pref:
        if dim % t == 0:
            best = t
        t += LANES
    assert best is not None, (dim, pref)
    return best


def _full_spec(arr):
    nd = arr.ndim
    return pl.BlockSpec(arr.shape, lambda *_: (0,) * nd)


def _mm(name, dims, a_ops, b_ops, terms, n_acc, epilogue, extras, out_dtypes, M, N, K, tm=512, tn=1024, tk=1024):
    tm, tn, tk = _tile(M, tm), _tile(N, tn), _tile(K, tk)
    gm, gn, gk = M // tm, N // tn, K // tk
    a_bytes = sum(a.size * a.dtype.itemsize for a in a_ops)
    b_bytes = sum(b.size * b.dtype.itemsize for b in b_ops)
    n_outer = gk == 1 and b_bytes + a_bytes * gn < a_bytes + b_bytes * gm

    def spec(shape, fn):
        if n_outer:
            return pl.BlockSpec(shape, lambda j, i, k: fn(i, j, k))
        return pl.BlockSpec(shape, fn)

    if dims == "tn":
        a_spec = spec((tk, tm), lambda i, j, k: (k, i))
    else:
        a_spec = spec((tm, tk), lambda i, j, k: (i, k))
    if dims == "nt":
        b_spec = spec((tn, tk), lambda i, j, k: (j, k))
    else:
        b_spec = spec((tk, tn), lambda i, j, k: (k, j))
    e_specs, e_arrays = [], []
    for ex in extras:
        if ex[1] == "mn":
            e_specs.append(spec((tm, tn), functools.partial(lambda i, j, k, off: (i, j + off), off=ex[2])))
        else:
            e_specs.append(spec((1, tn), lambda i, j, k: (0, j)))
        e_arrays.append(ex[0])
    na, nb, ne, no = len(a_ops), len(b_ops), len(extras), len(out_dtypes)

    def body(*refs):
        a_refs = refs[:na]
        b_refs = refs[na:na + nb]
        e_refs = refs[na + nb:na + nb + ne]
        o_refs = refs[na + nb + ne:na + nb + ne + no]
        acc_refs = refs[na + nb + ne + no:]
        k = pl.program_id(2)

        @pl.when(k == 0)
        def _():
            for acc in acc_refs:
                acc[...] = jnp.zeros_like(acc)

        for r, ai, bi in terms:
            acc_refs[r][...] += _dot(a_refs[ai][...], b_refs[bi][...], dims)

        @pl.when(k == gk - 1)
        def _():
            res = epilogue([acc[...] for acc in acc_refs], *[e[...] for e in e_refs])
            for o, val in zip(o_refs, res):
                o[...] = val.astype(o.dtype)

    outs = pl.pallas_call(
        body,
        name=name,
        grid=(gn, gm, gk) if n_outer else (gm, gn, gk),
        in_specs=[a_spec] * na + [b_spec] * nb + e_specs,
        out_specs=[spec((tm, tn), lambda i, j, k: (i, j))] * no,
        out_shape=[jax.ShapeDtypeStruct((M, N), dt) for dt in out_dtypes],
        scratch_shapes=[pltpu.VMEM((tm, tn), F32)] * n_acc,
        compiler_params=pltpu.CompilerParams(
            dimension_semantics=("parallel", "parallel", "arbitrary"), vmem_limit_bytes=VMEM_LIMIT_BYTES),
    )(*a_ops, *b_ops, *e_arrays)
    return outs


def _mm1(name, dims, a, b, M, N, K, out_dtype=F32, scale=None, **kw):
    def epi(accs):
        return [accs[0] if scale is None else accs[0] * scale]
    return _mm(name, dims, [a], [b], [(0, 0, 0)], 1, epi, [], [out_dtype], M, N, K, **kw)[0]


def _rowwise(name, fn, row_ins, vec_ins, row_outs, sum_outs, S, tr=256, reverse=False):
    tr = min(tr, S)
    g = S // tr
    rmap = (lambda i: (g - 1 - i)) if reverse else (lambda i: i)
    in_specs, arrays = [], []
    for r in row_ins:
        if isinstance(r, tuple):
            arr, width, blk = r
            in_specs.append(pl.BlockSpec((tr, width), functools.partial(lambda i, blk: (rmap(i), blk), blk=blk)))
        else:
            arr = r
            in_specs.append(pl.BlockSpec((tr, arr.shape[1]), lambda i: (rmap(i), 0)))
        arrays.append(arr)
    for v in vec_ins:
        in_specs.append(_full_spec(v))
        arrays.append(v)
    nr, nv, no, ns = len(row_ins), len(vec_ins), len(row_outs), len(sum_outs)

    def body(*refs):
        ins = [r[...] for r in refs[:nr + nv]]
        o_refs = refs[nr + nv:nr + nv + no]
        s_refs = refs[nr + nv + no:]
        outs, sums = fn(*ins)
        for o, val in zip(o_refs, outs):
            o[...] = val.astype(o.dtype)
        if ns:
            i = pl.program_id(0)

            @pl.when(i == 0)
            def _():
                for s, val in zip(s_refs, sums):
                    s[...] = val

            @pl.when(i > 0)
            def _():
                for s, val in zip(s_refs, sums):
                    s[...] += val

    res = pl.pallas_call(
        body,
        name=name,
        grid=(g,),
        in_specs=in_specs,
        out_specs=[pl.BlockSpec((tr, c), lambda i: (rmap(i), 0)) for c, _ in row_outs]
        + [pl.BlockSpec((1, c), lambda i: (0, 0)) for c in sum_outs],
        out_shape=[jax.ShapeDtypeStruct((S, c), dt) for c, dt in row_outs]
        + [jax.ShapeDtypeStruct((1, c), F32) for c in sum_outs],
        compiler_params=pltpu.CompilerParams(
            dimension_semantics=("arbitrary",), vmem_limit_bytes=VMEM_LIMIT_BYTES),
    )(*arrays)
    return res[:no], res[no:]


def _win(arr, offset, width):
    assert offset % width == 0
    return (arr, width, offset // width)


MESH_ID = pl.DeviceIdType.MESH


def _remote(src, dst, send_sem, recv_sem, to):
    return pltpu.make_async_remote_copy(src_ref=src, dst_ref=dst, send_sem=send_sem, recv_sem=recv_sem,
                                        device_id=to, device_id_type=MESH_ID)


def _hbm_call(name, body, arrs, out_shapes, n_send, n_recv, n_local):
    return pl.pallas_call(
        body,
        name=name,
        in_specs=[pl.BlockSpec(memory_space=pltpu.HBM)] * len(arrs),
        out_specs=[pl.BlockSpec(memory_space=pltpu.HBM)] * len(out_shapes),
        out_shape=out_shapes,
        scratch_shapes=[pltpu.SemaphoreType.DMA((n_send,)), pltpu.SemaphoreType.DMA((n_recv,)),
                        pltpu.SemaphoreType.DMA((n_local,))],
        compiler_params=pltpu.CompilerParams(has_side_effects=True),
    )(*arrs)


def _allgather_multi(name, arrs):
    n = len(arrs)

    def body(*refs):
        ins, outs = refs[:n], refs[n:2 * n]
        send_sems, recv_sems, local_sems = refs[2 * n:]
        x, y, c = lax.axis_index("x"), lax.axis_index("y"), lax.axis_index("c")
        me, sibling = (x, y, c), (x, y, 1 - c)
        chips = [(1 - x, y), (x, 1 - y), (1 - x, 1 - y)]

        def slot(i, dev):
            return outs[i].at[4 * dev[0] + 2 * dev[1] + dev[2]]

        def copy(i, k, block, to, src=None):
            dst = slot(i, block)
            return _remote(dst if src is None else src, dst, send_sems.at[7 * i + k], recv_sems.at[7 * i + k], to)

        mine = [pltpu.make_async_copy(ins[i], slot(i, me), local_sems.at[i]) for i in range(n)]
        for cp in mine:
            cp.start()
        first = []
        for i in range(n):
            first.append(copy(i, 0, me, sibling, src=ins[i]))
            first += [copy(i, 1 + j, me, (*chip, c), src=ins[i]) for j, chip in enumerate(chips)]
        for cp in first:
            cp.start()
        passed = []
        for j, chip in enumerate(chips):
            for i in range(n):
                copy(i, 1 + j, (*chip, c), me).wait_recv()
                cp = copy(i, 4 + j, (*chip, c), sibling)
                cp.start()
                passed.append(cp)
        for i in range(n):
            copy(i, 0, sibling, me).wait_recv()
            for j, chip in enumerate(chips):
                copy(i, 4 + j, (*chip, 1 - c), me).wait_recv()
        for cp in first + passed:
            cp.wait_send()
        for cp in mine:
            cp.wait()

    outs = [jax.ShapeDtypeStruct((N_DEV,) + a.shape, a.dtype) for a in arrs]
    return _hbm_call(name, body, arrs, outs, 7 * n, 7 * n, n)


def _sibling_swap_multi(name, arrs):
    n = len(arrs)
    per = 4

    def body(*refs):
        ins, got = refs[:n], refs[n:2 * n]
        send_sems, recv_sems, _ = refs[2 * n:]
        x, y, c = lax.axis_index("x"), lax.axis_index("y"), lax.axis_index("c")
        sibling = (x, y, 1 - c)
        sends = []
        for i in range(n):
            for a in range(4):
                k = per * i + a
                sends.append(_remote(ins[i].at[a, 1 - c], got[i].at[a], send_sems.at[k], recv_sems.at[k], sibling))
        for cp in sends:
            cp.start()
        for cp in sends:
            cp.wait_recv()
        for cp in sends:
            cp.wait_send()

    outs = [jax.ShapeDtypeStruct((4,) + a.shape[2:], a.dtype) for a in arrs]
    return _hbm_call(name, body, arrs, outs, per * n, per * n, 1)


def _chip_all_to_all_multi(name, arrs):
    n = len(arrs)

    def body(*refs):
        ins, outs = refs[:n], refs[n:2 * n]
        send_sems, recv_sems, local_sems = refs[2 * n:]
        x, y, c = lax.axis_index("x"), lax.axis_index("y"), lax.axis_index("c")
        mine = 2 * x + y
        chips = [(1 - x, y), (x, 1 - y), (1 - x, 1 - y)]
        local = [pltpu.make_async_copy(ins[i].at[mine], outs[i].at[mine], local_sems.at[i]) for i in range(n)]
        for cp in local:
            cp.start()
        sends, recvs = [], []
        for i in range(n):
            for j, (px, py) in enumerate(chips):
                peer = 2 * px + py
                sems = (send_sems.at[3 * i + j], recv_sems.at[3 * i + j], (px, py, c))
                sends.append(_remote(ins[i].at[peer], outs[i].at[mine], *sems))
                recvs.append(_remote(ins[i].at[peer], outs[i].at[peer], *sems))
        for cp in sends:
            cp.start()
        for cp in recvs:
            cp.wait_recv()
        for cp in sends:
            cp.wait_send()
        for cp in local:
            cp.wait()

    outs = [jax.ShapeDtypeStruct(a.shape, a.dtype) for a in arrs]
    return _hbm_call(name, body, arrs, outs, 3 * n, 3 * n, n)


def _as_rows(a, lead):
    return a.reshape(a.shape[:lead] + (-1, a.shape[-1]))


def _row_tile(rows, cols, parts):
    budget = 4 * 1024 * 1024 // (4 * max(cols, LANES) * parts)
    return _tile_rows(rows, max(8, min(512, budget // 8 * 8)))


def _pair_add(name, core, both, got):
    _, rows, cols = got.shape
    tr = _row_tile(rows, cols, 2)

    def body(c_ref, a_ref, b_ref, o_ref):
        o_ref[...] = (a_ref[...] + b_ref[...]).astype(o_ref.dtype)

    blk = pl.BlockSpec((1, tr, cols), lambda ch, i, c_ref: (ch, i, 0))
    return pl.pallas_call(
        body, name=name,
        grid_spec=pltpu.PrefetchScalarGridSpec(
            num_scalar_prefetch=1, grid=(4, rows // tr),
            in_specs=[pl.BlockSpec((1, None, tr, cols), lambda ch, i, c_ref: (ch, c_ref[0], i, 0)), blk],
            out_specs=blk),
        out_shape=jax.ShapeDtypeStruct(got.shape, BF16),
        compiler_params=pltpu.CompilerParams(dimension_semantics=("parallel", "parallel"),
                                             vmem_limit_bytes=VMEM_LIMIT_BYTES),
    )(core, both, got)


def _adamw(name, gparts, w, m, v):
    layers = len(gparts)
    parts, rows, cols = gparts[0].shape
    tr = _row_tile(rows, cols, parts * layers)
    c1 = 1.0 / (1.0 - ADAM_B1 ** ADAM_STEP)
    c2 = 1.0 / (1.0 - ADAM_B2 ** ADAM_STEP)

    def body(*refs):
        gp_refs = refs[:layers]
        w_ref, m_ref, v_ref, g_ref, d_ref, nm_ref, nv_ref = refs[layers:]
        layer = pl.program_id(0)
        g = None
        for k, gp_ref in enumerate(gp_refs):
            gk = gp_ref[0].astype(F32)
            for i in range(1, parts):
                gk = gk + gp_ref[i].astype(F32)
            g = gk if g is None else jnp.where(layer == k, gk, g)
        nm = ADAM_B1 * m_ref[...] + (1.0 - ADAM_B1) * g
        nv = ADAM_B2 * v_ref[...] + (1.0 - ADAM_B2) * (g * g)
        m_hat = nm * c1
        v_hat = nv * c2
        g_ref[...] = g
        nm_ref[...] = nm
        nv_ref[...] = nv
        d_ref[...] = -ADAM_LR * (m_hat / (jnp.sqrt(v_hat) + ADAM_EPS) + ADAM_WD * w_ref[...])

    row = pl.BlockSpec((None, tr, cols), lambda l, i: (l, i, 0))
    return pl.pallas_call(
        body,
        name=name,
        grid=(layers, rows // tr),
        in_specs=[pl.BlockSpec((parts, tr, cols), lambda l, i: (0, i, 0))] * layers + [row, row, row],
        out_specs=[row] * 4,
        out_shape=[jax.ShapeDtypeStruct((layers, rows, cols), F32)] * 4,
        compiler_params=pltpu.CompilerParams(dimension_semantics=("parallel", "parallel"),
                                             vmem_limit_bytes=VMEM_LIMIT_BYTES),
    )(*gparts, w, m, v)


def _tile_rows(rows, pref):
    t = min(pref, rows) // 8 * 8
    while t >= 8 and rows % t:
        t -= 8
    return t if t >= 8 else rows


PACK_ROWS = 512


def _pack(arrs, dtype):
    flat = jnp.concatenate([a.astype(dtype).reshape(-1) for a in arrs])
    quantum = PACK_ROWS * LANES
    padded = -(-flat.shape[0] // quantum) * quantum
    return jnp.pad(flat, (0, padded - flat.shape[0])).reshape(-1, LANES)


def _unpack(buf, shapes, lead=()):
    flat = buf.reshape(lead + (-1,))
    out, off = [], 0
    for shp in shapes:
        n = math.prod(shp)
        out.append(flat[..., off:off + n].reshape(lead + tuple(shp)))
        off += n
    return out


def _dest_pieces(name, g):
    if name == "w_branch":
        return jnp.moveaxis(g.reshape(3, 4, 2, D_MODEL // N_DEV, BRANCH), 0, 2)
    if name in SHARDED_F32_GATHER:
        return jnp.moveaxis(g.reshape(g.shape[0], 4, 2, -1), 0, 2)
    return g.reshape((4, 2, g.shape[0] // N_DEV) + g.shape[1:])


def _shift_down(a, k):
    return jnp.pad(a, ((k, 0), (0, 0)))[:a.shape[0]] if k else a


def _shift_up(a, k, fill=0.0):
    return jnp.pad(a, ((0, k), (0, 0)), constant_values=fill)[k:] if k else a


def _lru_fwd(u, ax_shift, lw, S):
    T = min(256, S)
    nb = S // T
    row = pl.BlockSpec((T, BRANCH), lambda t: (t, 0))
    vecs = [lw["cw0"], lw["cw1"], lw["cw2"], lw["cw3"], lw["conv_b"], lw["wa"], lw["wx"], lw["ba"], lw["bx"],
            lw["lam"]]

    def body(ax0, ax1, ax2, ax3, ay, cw0, cw1, cw2, cw3, cb, wa, wx, ba, bx, lam, xc_o, r_o, i_o, a_o, h_o, ya_o, hc):
        t = pl.program_id(0)

        @pl.when(t == 0)
        def _():
            hc[...] = jnp.zeros_like(hc)

        xc = cw3[...] * ax0[...] + cw2[...] * ax1[...] + cw1[...] * ax2[...] + cw0[...] * ax3[...] + cb[...]
        r = _sigmoid(_dot(xc, wa[...], "nn") + ba[...])
        gi = _sigmoid(_dot(xc, wx[...], "nn") + bx[...])
        sp = _softplus(-lam[...])
        la = -LRU_C * r * sp
        a = jnp.exp(la)
        mult = jnp.sqrt(_neg_expm1(2.0 * la))
        A, B = _scan_rows(a, mult * gi * xc, T)
        h = B + A * hc[...]
        h_o[...] = h
        hc[...] = h_o[pl.ds(T - 1, 1), :]
        xc_o[...] = xc
        r_o[...] = r
        i_o[...] = gi
        a_o[...] = a
        gy, _ = _gelu_and_grad(ay[...])
        ya_o[...] = (gy * h).astype(ya_o.dtype)

    outs = pl.pallas_call(
        body,
        name="lru_fwd",
        grid=(nb,),
        in_specs=[pl.BlockSpec((T, BRANCH), lambda t: (t, U_AX // BRANCH))] + [row] * 3
        + [pl.BlockSpec((T, BRANCH), lambda t: (t, U_AY // BRANCH))] + [_full_spec(v) for v in vecs],
        out_specs=[row] * 6,
        out_shape=[jax.ShapeDtypeStruct((S, BRANCH), F32)] * 5 + [jax.ShapeDtypeStruct((S, BRANCH), BF16)],
        scratch_shapes=[pltpu.VMEM((1, BRANCH), F32)],
        compiler_params=pltpu.CompilerParams(dimension_semantics=("arbitrary",), vmem_limit_bytes=VMEM_LIMIT_BYTES),
    )(u, *ax_shift[1:], u, *vecs)
    return outs


def _lru_bwd(dya, u, sv, lw, S):
    T = min(256, S)
    nb = S // T
    rrow = pl.BlockSpec((T, BRANCH), lambda t: (nb - 1 - t, 0))
    sq = pl.BlockSpec((BRANCH, BRANCH), lambda t: (0, 0))
    vrow = pl.BlockSpec((1, BRANCH), lambda t: (0, 0))
    h_prev = _shift_down(sv["h"], 1)
    a_next = _shift_up(sv["a"], 1)

    def body(dya_r, ay, h, hp, xc_r, r_r, i_r, a_r, an, wa, wx, lam,
             day_o, dxc_o, dwa_o, dwx_o, dba_o, dbx_o, dlam_o, lcar, tmp):
        t = pl.program_id(0)

        @pl.when(t == 0)
        def _():
            lcar[...] = jnp.zeros_like(lcar)
            dwa_o[...] = jnp.zeros_like(dwa_o)
            dwx_o[...] = jnp.zeros_like(dwx_o)
            dba_o[...] = jnp.zeros_like(dba_o)
            dbx_o[...] = jnp.zeros_like(dbx_o)
            dlam_o[...] = jnp.zeros_like(dlam_o)

        gy, dgy = _gelu_and_grad(ay[...])
        dy = dya_r[...]
        day_o[...] = (dy * h[...] * dgy).astype(day_o.dtype)
        A, B = _scan_rows(an[...], dy * gy, T, reverse=True)
        lmb = B + A * lcar[...]
        tmp[...] = lmb
        lcar[...] = tmp[pl.ds(0, 1), :]
        xc, r, gi, a = xc_r[...], r_r[...], i_r[...], a_r[...]
        sp = _softplus(-lam[...])
        la = -LRU_C * r * sp
        mult = jnp.sqrt(_neg_expm1(2.0 * la))
        da = lmb * hp[...]
        dmult = lmb * gi * xc
        di = lmb * mult * xc
        dxc = lmb * mult * gi
        dla = da * a - dmult * a * a / mult
        dr = dla * (-LRU_C * sp)
        dlam_o[...] += _colsum(dla * (LRU_C * r)) * _sigmoid(-lam[...])
        dpr = dr * r * (1.0 - r)
        dpi = di * gi * (1.0 - gi)
        dba_o[...] += _colsum(dpr)
        dbx_o[...] += _colsum(dpi)
        dxc_o[...] = dxc + _dot(dpr, wa[...], "nt") + _dot(dpi, wx[...], "nt")
        dwa_o[...] += _dot(xc, dpr, "tn")
        dwx_o[...] += _dot(xc, dpi, "tn")

    outs = pl.pallas_call(
        body,
        name="lru_bwd",
        grid=(nb,),
        in_specs=[rrow, pl.BlockSpec((T, BRANCH), lambda t: (nb - 1 - t, U_AY // BRANCH))] + [rrow] * 7
        + [sq, sq, vrow],
        out_specs=[rrow, rrow, sq, sq, vrow, vrow, vrow],
        out_shape=[jax.ShapeDtypeStruct((S, BRANCH), BF16), jax.ShapeDtypeStruct((S, BRANCH), F32),
                   jax.ShapeDtypeStruct((BRANCH, BRANCH), F32), jax.ShapeDtypeStruct((BRANCH, BRANCH), F32),
                   jax.ShapeDtypeStruct((1, BRANCH), F32), jax.ShapeDtypeStruct((1, BRANCH), F32),
                   jax.ShapeDtypeStruct((1, BRANCH), F32)],
        scratch_shapes=[pltpu.VMEM((1, BRANCH), F32), pltpu.VMEM((T, BRANCH), F32)],
        compiler_params=pltpu.CompilerParams(dimension_semantics=("arbitrary",), vmem_limit_bytes=VMEM_LIMIT_BYTES),
    )(dya, u, sv["h"], h_prev, sv["xc"], sv["r"], sv["i"], sv["a"], a_next, lw["wa"], lw["wx"], lw["lam"])
    return outs


def _conv_bwd(dxc, ax_shift, lw, S):
    dxs = [_shift_up(dxc, k) for k in range(4)]

    def fn(d0, d1, d2, d3, a0, a1, a2, a3, cw0, cw1, cw2, cw3):
        dax = cw3 * d0 + cw2 * d1 + cw1 * d2 + cw0 * d3
        return [dax], [_colsum(d0 * a3), _colsum(d0 * a2), _colsum(d0 * a1), _colsum(d0 * a0), _colsum(d0)]

    (dax,), sums = _rowwise("conv_bwd", fn, dxs + list(ax_shift), [lw["cw0"], lw["cw1"], lw["cw2"], lw["cw3"]],
                            [(BRANCH, BF16)], [BRANCH] * 5, S)
    return dax, sums


GLA_QK = GLA_HEADS * GLA_DK
GLA_V = GLA_HEADS * GLA_DV
GLA_SCALE = GLA_DK ** -0.5


def _gla_specs(TB, rev_nb=None):
    def rmap(t):
        return t if rev_nb is None else rev_nb - 1 - t
    return [
        pl.BlockSpec((TB, GLA_QK), lambda t: (rmap(t), U_BQ // GLA_QK)),
        pl.BlockSpec((TB, GLA_QK), lambda t: (rmap(t), U_BK // GLA_QK)),
        pl.BlockSpec((TB, GLA_V), lambda t: (rmap(t), U_BV // GLA_V)),
        pl.BlockSpec((TB, GLA_V), lambda t: (rmap(t), U_BR // GLA_V)),
        pl.BlockSpec((TB, LANES), lambda t: (rmap(t), U_BLOW // LANES)),
    ]


def _gla_gates(gl, wg2, bg, TB):
    pre = _dot(gl, wg2, "nn") + bg
    la = _log_sigmoid(pre) * (1.0 / GLA_TAU)
    _, gc = _scan_rows(None, la, TB, seg=CHUNK)
    return pre, la, gc


def _gla_fwd(u, gw, S):
    TB = min(512, S)
    nb = S // TB
    cpb = TB // CHUNK
    vecs = [gw["wg2"], gw["bg"], gw["ng"], gw["bd"]]

    def body(q_r, k_r, v_r, br_r, gl_r, wg2, bg, ng, bd, yb_o, oraw_o, st_o, st):
        t = pl.program_id(0)

        @pl.when(t == 0)
        def _():
            st[...] = jnp.zeros_like(st)

        _, la, gc = _gla_gates(gl_r[...], wg2[...], bg[...], TB)
        for c in range(cpb):
            sl = slice(c * CHUNK, (c + 1) * CHUNK)
            gt = _colsum(la[sl])
            kdec = k_r[sl, :] * jnp.exp(gt - gc[sl])
            d_t = _dot(v_r[sl, :], kdec, "tn") * bd[...]
            s_new = st[...] * jnp.exp(gt) + d_t
            st[...] = s_new
            st_o[c] = s_new
            oraw_o[sl, :] = _dot(q_r[sl, :] * GLA_SCALE, s_new, "nt")
        for h in range(GLA_HEADS):
            hs = slice(h * GLA_DV, (h + 1) * GLA_DV)
            oh = oraw_o[:, hs]
            on = oh * lax.rsqrt(jnp.mean(oh * oh, axis=-1, keepdims=True) + RMS_EPS)
            sil, _ = _silu_and_grad(br_r[:, hs])
            yb_o[:, hs] = (on * ng[:, hs] * sil).astype(yb_o.dtype)

    return pl.pallas_call(
        body,
        name="gla_fwd",
        grid=(nb,),
        in_specs=_gla_specs(TB) + [_full_spec(v) for v in vecs],
        out_specs=[pl.BlockSpec((TB, GLA_V), lambda t: (t, 0)), pl.BlockSpec((TB, GLA_V), lambda t: (t, 0)),
                   pl.BlockSpec((cpb, GLA_V, GLA_QK), lambda t: (t, 0, 0))],
        out_shape=[jax.ShapeDtypeStruct((S, GLA_V), BF16), jax.ShapeDtypeStruct((S, GLA_V), F32),
                   jax.ShapeDtypeStruct((S // CHUNK, GLA_V, GLA_QK), F32)],
        scratch_shapes=[pltpu.VMEM((GLA_V, GLA_QK), F32)],
        compiler_params=pltpu.CompilerParams(dimension_semantics=("arbitrary",), vmem_limit_bytes=VMEM_LIMIT_BYTES),
    )(u, u, u, u, u, *vecs)


def _gla_bwd(dyb, u, oraw, states, gw, S):
    TB = min(512, S)
    nb = S // TB
    cpb = TB // CHUNK
    vecs = [gw["wg2"], gw["bg"], gw["ng"], gw["bd"]]

    def rrow(width):
        return pl.BlockSpec((TB, width), lambda t: (nb - 1 - t, 0))

    def body(dyb_r, oraw_r, q_r, k_r, v_r, br_r, gl_r, st_r, sp_r, wg2, bg, ng, bd,
             dq_o, dk_o, dv_o, dbr_o, dgl_o, dwg2_o, dbg_o, dng_o, dcar, do_buf, dla_buf):
        t = pl.program_id(0)
        blk = nb - 1 - t

        @pl.when(t == 0)
        def _():
            dcar[...] = jnp.zeros_like(dcar)
            dwg2_o[...] = jnp.zeros_like(dwg2_o)
            dbg_o[...] = jnp.zeros_like(dbg_o)
            dng_o[...] = jnp.zeros_like(dng_o)

        pre, la, gc = _gla_gates(gl_r[...], wg2[...], bg[...], TB)
        for h in range(GLA_HEADS):
            hs = slice(h * GLA_DV, (h + 1) * GLA_DV)
            oh = oraw_r[:, hs]
            rs = lax.rsqrt(jnp.mean(oh * oh, axis=-1, keepdims=True) + RMS_EPS)
            on = oh * rs
            sil, dsil = _silu_and_grad(br_r[:, hs])
            dy = dyb_r[:, hs]
            dbr_o[:, hs] = (dy * on * ng[:, hs] * dsil).astype(dbr_o.dtype)
            don = dy * ng[:, hs] * sil
            dng_o[:, hs] += _colsum(dy * on * sil)
            do_buf[:, hs] = rs * (don - on * jnp.mean(don * on, axis=-1, keepdims=True))
        first = jnp.where(blk == 0, 0.0, 1.0)
        for c in reversed(range(cpb)):
            sl = slice(c * CHUNK, (c + 1) * CHUNK)
            s_n = st_r[c]
            s_prev = st_r[c - 1] if c > 0 else sp_r[0] * first
            gt = _colsum(la[sl])
            w = jnp.exp(gt - gc[sl])
            k_c = k_r[sl, :]
            kdec = k_c * w
            qs = q_r[sl, :] * GLA_SCALE
            do_c = do_buf[sl, :]
            dq_o[sl, :] = (_dot(do_c, s_n, "nn") * GLA_SCALE).astype(dq_o.dtype)
            d_n = _dot(do_c, qs, "tn") * bd[...] + dcar[...]
            dv_o[sl, :] = _dot(kdec, d_n, "nt").astype(dv_o.dtype)
            dkdec = _dot(v_r[sl, :], d_n, "nn")
            dk_o[sl, :] = (dkdec * w).astype(dk_o.dtype)
            tt = dkdec * kdec
            e = jnp.exp(gt)
            dgt = _colsum(tt) + _colsum(d_n * s_prev) * e
            _, rc = _scan_rows(None, -tt, CHUNK, reverse=True)
            dla_buf[sl, :] = rc + dgt
            dcar[...] = d_n * e
        dpre = dla_buf[...] * _sigmoid(-pre) * (1.0 / GLA_TAU)
        dbg_o[...] += _colsum(dpre)
        dgl_o[...] = _dot(dpre, wg2[...], "nt").astype(dgl_o.dtype)
        dwg2_o[...] += _dot(gl_r[...], dpre, "tn")

    return pl.pallas_call(
        body,
        name="gla_bwd",
        grid=(nb,),
        in_specs=[rrow(GLA_V), rrow(GLA_V)] + _gla_specs(TB, rev_nb=nb)
        + [pl.BlockSpec((cpb, GLA_V, GLA_QK), lambda t: (nb - 1 - t, 0, 0)),
           pl.BlockSpec((1, GLA_V, GLA_QK), lambda t: (jnp.maximum((nb - 1 - t) * cpb - 1, 0), 0, 0))]
        + [_full_spec(v) for v in vecs],
        out_specs=[rrow(GLA_QK), rrow(GLA_QK), rrow(GLA_V), rrow(GLA_V), rrow(LANES),
                   pl.BlockSpec((LANES, GLA_QK), lambda t: (0, 0)), pl.BlockSpec((1, GLA_QK), lambda t: (0, 0)),
                   pl.BlockSpec((1, GLA_V), lambda t: (0, 0))],
        out_shape=[jax.ShapeDtypeStruct((S, GLA_QK), BF16), jax.ShapeDtypeStruct((S, GLA_QK), BF16),
                   jax.ShapeDtypeStruct((S, GLA_V), BF16), jax.ShapeDtypeStruct((S, GLA_V), BF16),
                   jax.ShapeDtypeStruct((S, LANES), BF16), jax.ShapeDtypeStruct((LANES, GLA_QK), F32),
                   jax.ShapeDtypeStruct((1, GLA_QK), F32), jax.ShapeDtypeStruct((1, GLA_V), F32)],
        scratch_shapes=[pltpu.VMEM((GLA_V, GLA_QK), F32), pltpu.VMEM((TB, GLA_V), F32),
                        pltpu.VMEM((TB, GLA_QK), F32)],
        compiler_params=pltpu.CompilerParams(dimension_semantics=("arbitrary",), vmem_limit_bytes=VMEM_LIMIT_BYTES),
    )(dyb, oraw, u, u, u, u, u, states, states, *vecs)


FOX_SCALE = FOX_DH ** -0.5


def _fox_gate_fwd(u, bfp, S):
    T = min(512, S)

    def body(f_r, b_r, fc_o, car):
        t = pl.program_id(0)

        @pl.when(t == 0)
        def _():
            car[...] = jnp.zeros_like(car)

        _, cs = _scan_rows(None, _log_sigmoid(f_r[...] + b_r[...]), T)
        fc_o[...] = cs + car[...]
        car[...] = fc_o[pl.ds(T - 1, 1), :]

    return pl.pallas_call(
        body,
        name="fox_gate_fwd",
        grid=(S // T,),
        in_specs=[pl.BlockSpec((T, LANES), lambda t: (t, U_CF // LANES)), _full_spec(bfp)],
        out_specs=pl.BlockSpec((T, LANES), lambda t: (t, 0)),
        out_shape=jax.ShapeDtypeStruct((S, LANES), F32),
        scratch_shapes=[pltpu.VMEM((1, LANES), F32)],
        compiler_params=pltpu.CompilerParams(dimension_semantics=("arbitrary",), vmem_limit_bytes=VMEM_LIMIT_BYTES),
    )(u, bfp)


def _fox_gate_bwd(dfc, u, bfp, S):
    T = min(512, S)
    nb = S // T

    def body(d_r, f_r, b_r, df_o, db_o, car, tmp):
        t = pl.program_id(0)

        @pl.when(t == 0)
        def _():
            car[...] = jnp.zeros_like(car)
            db_o[...] = jnp.zeros_like(db_o)

        _, rc = _scan_rows(None, d_r[...], T, reverse=True)
        tmp[...] = rc + car[...]
        car[...] = tmp[pl.ds(0, 1), :]
        df = tmp[...] * _sigmoid(-(f_r[...] + b_r[...]))
        df_o[...] = df.astype(df_o.dtype)
        db_o[...] += _colsum(df)

    return pl.pallas_call(
        body,
        name="fox_gate_bwd",
        grid=(nb,),
        in_specs=[pl.BlockSpec((T, LANES), lambda t: (nb - 1 - t, 0)),
                  pl.BlockSpec((T, LANES), lambda t: (nb - 1 - t, U_CF // LANES)), _full_spec(bfp)],
        out_specs=[pl.BlockSpec((T, LANES), lambda t: (nb - 1 - t, 0)), pl.BlockSpec((1, LANES), lambda t: (0, 0))],
        out_shape=[jax.ShapeDtypeStruct((S, LANES), BF16), jax.ShapeDtypeStruct((1, LANES), F32)],
        scratch_shapes=[pltpu.VMEM((1, LANES), F32), pltpu.VMEM((T, LANES), F32)],
        compiler_params=pltpu.CompilerParams(dimension_semantics=("arbitrary",), vmem_limit_bytes=VMEM_LIMIT_BYTES),
    )(dfc, u, bfp)


def _fox_scores(q, k, fq, fk, qi, ki, tq, tk):
    s = _dot(q, k, "nt") * FOX_SCALE + (fq - fk)
    rows = lax.broadcasted_iota(jnp.int32, (tq, tk), 0) + qi * tq
    cols = lax.broadcasted_iota(jnp.int32, (tq, tk), 1) + ki * tk
    return jnp.where(cols <= rows, s, NEG_BIG)


def _fox_fwd(qh, kh, vh, fq, fk, S):
    tq = tk = min(512, S)
    nq, nk = S // tq, S // tk

    def body(q_r, k_r, v_r, fq_r, fk_r, o_o, lse_o, m_s, l_s, acc):
        qi, ki = pl.program_id(1), pl.program_id(2)

        @pl.when(ki == 0)
        def _():
            m_s[...] = jnp.full_like(m_s, NEG_BIG)
            l_s[...] = jnp.zeros_like(l_s)
            acc[...] = jnp.zeros_like(acc)

        @pl.when(ki <= qi)
        def _():
            s = _fox_scores(q_r[0], k_r[0], fq_r[0], fk_r[0], qi, ki, tq, tk)
            m_new = jnp.maximum(m_s[...], jnp.max(s, axis=-1, keepdims=True))
            p = jnp.exp(s - m_new)
            alpha = jnp.exp(m_s[...] - m_new)
            l_s[...] = alpha * l_s[...] + jnp.sum(p, axis=-1, keepdims=True)
            acc[...] = alpha * acc[...] + _dot(p, v_r[0], "nn")
            m_s[...] = m_new

        @pl.when(ki == nk - 1)
        def _():
            o_o[0] = acc[...] / l_s[...]
            lse_o[0] = m_s[...] + jnp.log(l_s[...])

    kv = pl.BlockSpec((1, tk, FOX_DH), lambda h, i, j: (h, jnp.minimum(j, i), 0))
    return pl.pallas_call(
        body,
        name="fox_fwd",
        grid=(FOX_HEADS, nq, nk),
        in_specs=[pl.BlockSpec((1, tq, FOX_DH), lambda h, i, j: (h, i, 0)), kv, kv,
                  pl.BlockSpec((1, tq, 1), lambda h, i, j: (h, i, 0)),
                  pl.BlockSpec((1, 1, tk), lambda h, i, j: (h, 0, jnp.minimum(j, i)))],
        out_specs=[pl.BlockSpec((1, tq, FOX_DH), lambda h, i, j: (h, i, 0)),
                   pl.BlockSpec((1, tq, 1), lambda h, i, j: (h, i, 0))],
        out_shape=[jax.ShapeDtypeStruct((FOX_HEADS, S, FOX_DH), F32), jax.ShapeDtypeStruct((FOX_HEADS, S, 1), F32)],
        scratch_shapes=[pltpu.VMEM((tq, 1), F32), pltpu.VMEM((tq, 1), F32), pltpu.VMEM((tq, FOX_DH), F32)],
        compiler_params=pltpu.CompilerParams(
            dimension_semantics=("parallel", "parallel", "arbitrary"), vmem_limit_bytes=VMEM_LIMIT_BYTES),
    )(qh, kh, vh, fq, fk)


def _fox_bwd_dq(qh, kh, vh, fq, fk, o, do, lse, S):
    tq = tk = min(512, S)
    nq, nk = S // tq, S // tk

    def body(q_r, k_r, v_r, fq_r, fk_r, o_r, do_r, lse_r, dq_o, dfq_o, dq_acc, df_acc):
        qi, ki = pl.program_id(1), pl.program_id(2)

        @pl.when(ki == 0)
        def _():
            dq_acc[...] = jnp.zeros_like(dq_acc)
            df_acc[...] = jnp.zeros_like(df_acc)

        @pl.when(ki <= qi)
        def _():
            s = _fox_scores(q_r[0], k_r[0], fq_r[0], fk_r[0], qi, ki, tq, tk)
            p = jnp.exp(s - lse_r[0])
            do_t = do_r[0]
            delta = jnp.sum(o_r[0] * do_t, axis=-1, keepdims=True)
            ds = p * (_dot(do_t, v_r[0], "nt") - delta)
            dq_acc[...] += _dot(ds, k_r[0], "nn")
            df_acc[...] += jnp.sum(ds, axis=-1, keepdims=True)

        @pl.when(ki == nk - 1)
        def _():
            dq_o[0] = dq_acc[...] * FOX_SCALE
            dfq_o[0] = df_acc[...]

    qrow = pl.BlockSpec((1, tq, FOX_DH), lambda h, i, j: (h, i, 0))
    qcol = pl.BlockSpec((1, tq, 1), lambda h, i, j: (h, i, 0))
    kv = pl.BlockSpec((1, tk, FOX_DH), lambda h, i, j: (h, jnp.minimum(j, i), 0))
    return pl.pallas_call(
        body,
        name="fox_bwd_dq",
        grid=(FOX_HEADS, nq, nk),
        in_specs=[qrow, kv, kv, qcol, pl.BlockSpec((1, 1, tk), lambda h, i, j: (h, 0, jnp.minimum(j, i))),
                  qrow, qrow, qcol],
        out_specs=[qrow, qcol],
        out_shape=[jax.ShapeDtypeStruct((FOX_HEADS, S, FOX_DH), F32), jax.ShapeDtypeStruct((FOX_HEADS, S, 1), F32)],
        scratch_shapes=[pltpu.VMEM((tq, FOX_DH), F32), pltpu.VMEM((tq, 1), F32)],
        compiler_params=pltpu.CompilerParams(
            dimension_semantics=("parallel", "parallel", "arbitrary"), vmem_limit_bytes=VMEM_LIMIT_BYTES),
    )(qh, kh, vh, fq, fk, o, do, lse)


def _fox_bwd_dkv(qh, kh, vh, fq, fk, o, do, lse, S):
    tq = tk = min(512, S)
    nq, nk = S // tq, S // tk

    def body(q_r, k_r, v_r, fq_r, fk_r, o_r, do_r, lse_r, dk_o, dv_o, dfk_o, dk_acc, dv_acc, df_acc):
        ki, qi = pl.program_id(1), pl.program_id(2)

        @pl.when(qi == 0)
        def _():
            dk_acc[...] = jnp.zeros_like(dk_acc)
            dv_acc[...] = jnp.zeros_like(dv_acc)
            df_acc[...] = jnp.zeros_like(df_acc)

        @pl.when(qi >= ki)
        def _():
            s = _fox_scores(q_r[0], k_r[0], fq_r[0], fk_r[0], qi, ki, tq, tk)
            p = jnp.exp(s - lse_r[0])
            do_t = do_r[0]
            delta = jnp.sum(o_r[0] * do_t, axis=-1, keepdims=True)
            ds = p * (_dot(do_t, v_r[0], "nt") - delta)
            dv_acc[...] += _dot(p, do_t, "tn")
            dk_acc[...] += _dot(ds, q_r[0], "tn")
            df_acc[...] += _colsum(ds)

        @pl.when(qi == nq - 1)
        def _():
            dk_o[0] = dk_acc[...] * FOX_SCALE
            dv_o[0] = dv_acc[...]
            dfk_o[0] = df_acc[...]

    qrow = pl.BlockSpec((1, tq, FOX_DH), lambda h, j, i: (h, jnp.maximum(i, j), 0))
    qcol = pl.BlockSpec((1, tq, 1), lambda h, j, i: (h, jnp.maximum(i, j), 0))
    kv = pl.BlockSpec((1, tk, FOX_DH), lambda h, j, i: (h, j, 0))
    krow = pl.BlockSpec((1, 1, tk), lambda h, j, i: (h, 0, j))
    return pl.pallas_call(
        body,
        name="fox_bwd_dkv",
        grid=(FOX_HEADS, nk, nq),
        in_specs=[qrow, kv, kv, qcol, krow, qrow, qrow, qcol],
        out_specs=[kv, kv, krow],
        out_shape=[jax.ShapeDtypeStruct((FOX_HEADS, S, FOX_DH), F32), jax.ShapeDtypeStruct((FOX_HEADS, S, FOX_DH), F32),
                   jax.ShapeDtypeStruct((FOX_HEADS, 1, S), F32)],
        scratch_shapes=[pltpu.VMEM((tk, FOX_DH), F32), pltpu.VMEM((tk, FOX_DH), F32), pltpu.VMEM((1, tk), F32)],
        compiler_params=pltpu.CompilerParams(
            dimension_semantics=("parallel", "parallel", "arbitrary"), vmem_limit_bytes=VMEM_LIMIT_BYTES),
    )(qh, kh, vh, fq, fk, o, do, lse)


FOX_TILE = 1024
FOX_AUG = 128
FOX_ONES = 3


def _fox_pairs(n, by_key):
    pairs = [(qi, ki) for qi in range(n) for ki in range(qi + 1)]
    if by_key:
        pairs.sort(key=lambda qk: (qk[1], qk[0]))
    qs = jnp.asarray([qk[0] for qk in pairs], jnp.int32)
    ks = jnp.asarray([qk[1] for qk in pairs], jnp.int32)
    return qs, ks


def _fox_augment(q, k, fcum):
    S = q.shape[0]
    def to_bf16_grid(a):
        return lax.reduce_precision(a, exponent_bits=8, mantissa_bits=7)

    hi = to_bf16_grid(fcum)
    mid = to_bf16_grid(fcum - hi)
    lo = to_bf16_grid(fcum - hi - mid)
    f3 = jnp.stack([hi, mid, lo], axis=-1).astype(BF16)
    ones = jnp.ones((S, FOX_HEADS, FOX_ONES), BF16)
    pad = jnp.zeros((S, FOX_HEADS, FOX_AUG - FOX_DH - 2 * FOX_ONES), BF16)
    q_aug = jnp.concatenate([(q * FOX_SCALE).astype(BF16), ones, f3, pad], axis=-1)
    k_aug = jnp.concatenate([k.astype(BF16), -f3, ones, pad], axis=-1)
    return jnp.transpose(q_aug, (1, 0, 2)), jnp.transpose(k_aug, (1, 0, 2))


def _fox_causal(sT):
    keys = lax.broadcasted_iota(jnp.int32, sT.shape, 0)
    queries = lax.broadcasted_iota(jnp.int32, sT.shape, 1)
    return jnp.where(keys <= queries, sT, NEG_BIG)


def _fox_fwd(qT, ka, vT, S):
    t = min(FOX_TILE, S)
    n = S // t
    qi_tab, ki_tab = _fox_pairs(n, by_key=False)

    def body(qi_ref, ki_ref, qT_r, ka_r, vT_r, oT_o, lse_o, m_s, l_s, acc):
        step = pl.program_id(1)
        qi, ki = qi_ref[step], ki_ref[step]

        @pl.when(ki == 0)
        def _():
            m_s[...] = jnp.full_like(m_s, NEG_BIG)
            l_s[...] = jnp.zeros_like(l_s)
            acc[...] = jnp.zeros_like(acc)

        def update(masked):
            sT = _dot(ka_r[0], qT_r[0], "nn")
            if masked:
                sT = _fox_causal(sT)
            m_new = jnp.maximum(m_s[...], jnp.max(sT, axis=0, keepdims=True))
            p = jnp.exp(sT - m_new)
            alpha = jnp.exp(m_s[...] - m_new)
            l_s[...] = alpha * l_s[...] + jnp.sum(p, axis=0, keepdims=True)
            acc[...] = alpha * acc[...] + _dot(vT_r[0], p, "nn")
            m_s[...] = m_new

        @pl.when(ki < qi)
        def _():
            update(False)

        @pl.when(ki == qi)
        def _():
            update(True)
            oT_o[0] = acc[...] / l_s[...]
            lse_o[0] = m_s[...] + jnp.log(l_s[...])

    return pl.pallas_call(
        body,
        name="fox_fwd",
        grid_spec=pltpu.PrefetchScalarGridSpec(
            num_scalar_prefetch=2, grid=(FOX_HEADS, int(qi_tab.shape[0])),
            in_specs=[pl.BlockSpec((1, FOX_AUG, t), lambda h, s, qt, kt: (h, 0, qt[s])),
                      pl.BlockSpec((1, t, FOX_AUG), lambda h, s, qt, kt: (h, kt[s], 0)),
                      pl.BlockSpec((1, FOX_DH, t), lambda h, s, qt, kt: (h, 0, kt[s]))],
            out_specs=[pl.BlockSpec((1, FOX_DH, t), lambda h, s, qt, kt: (h, 0, qt[s])),
                       pl.BlockSpec((1, 1, t), lambda h, s, qt, kt: (h, 0, qt[s]))],
            scratch_shapes=[pltpu.VMEM((1, t), F32), pltpu.VMEM((1, t), F32), pltpu.VMEM((FOX_DH, t), F32)]),
        out_shape=[jax.ShapeDtypeStruct((FOX_HEADS, FOX_DH, S), F32), jax.ShapeDtypeStruct((FOX_HEADS, 1, S), F32)],
        compiler_params=pltpu.CompilerParams(
            dimension_semantics=("parallel", "arbitrary"), vmem_limit_bytes=VMEM_LIMIT_BYTES),
    )(qi_tab, ki_tab, qT, ka, vT)


FOX_BIAS_ROWS = 8


def _fox_bwd(qT, qa, ka, kT, v, do, doT, oT, lse, S):
    t = min(FOX_TILE, S)
    n = S // t
    qi_tab, ki_tab = _fox_pairs(n, by_key=True)
    n_steps = int(qi_tab.shape[0])
    slab = slice(FOX_DH, FOX_DH + FOX_BIAS_ROWS)

    def body(qi_ref, ki_ref, qT_r, qa_r, ka_r, kT_r, v_r, do_r, doT_r, oT_r, lse_r,
             dq_o, dfq_o, dk_o, dfk_o, dv_o, dq_acc, dk_acc, dv_acc):
        step = pl.program_id(1)
        qi, ki = qi_ref[step], ki_ref[step]

        @pl.when(step == 0)
        def _():
            dq_acc[...] = jnp.zeros_like(dq_acc)

        @pl.when(qi == ki)
        def _():
            dk_acc[...] = jnp.zeros_like(dk_acc)
            dv_acc[...] = jnp.zeros_like(dv_acc)

        def update(masked):
            sT = _dot(ka_r[0], qT_r[0], "nn")
            if masked:
                sT = _fox_causal(sT)
            pT = jnp.exp(sT - lse_r[0])
            delta = jnp.sum(oT_r[0] * doT_r[0], axis=0, keepdims=True)
            dsT = pT * (_dot(v_r[0], doT_r[0], "nn") - delta)
            dv_acc[...] += _dot(pT, do_r[0], "nn")
            dk_acc[...] += _dot(dsT, qa_r[0], "nn")
            dq_acc[qi] += _dot(kT_r[0], dsT, "nn")

        @pl.when(qi > ki)
        def _():
            update(False)

        @pl.when(qi == ki)
        def _():
            update(True)

        @pl.when(qi == n - 1)
        def _():
            dk = dk_acc[...]
            dk_o[0] = dk[:, :FOX_DH]
            dfk_o[0] = dk.T[slab]
            dv_o[0] = dv_acc[...]

        @pl.when(step == n_steps - 1)
        def _():
            for j in range(n):
                dqT = dq_acc[j]
                dq_o[0, j * t:(j + 1) * t, :] = dqT.T[:, :FOX_DH]
                dfq_o[0, :, j * t:(j + 1) * t] = dqT[slab]

    def qlane(rows):
        return pl.BlockSpec((1, rows, t), lambda h, s, qt, kt: (h, 0, qt[s]))

    def qrow(cols):
        return pl.BlockSpec((1, t, cols), lambda h, s, qt, kt: (h, qt[s], 0))

    def krow(cols):
        return pl.BlockSpec((1, t, cols), lambda h, s, qt, kt: (h, kt[s], 0))

    def klane(rows):
        return pl.BlockSpec((1, rows, t), lambda h, s, qt, kt: (h, 0, kt[s]))

    def head(rows, cols):
        return pl.BlockSpec((1, rows, cols), lambda h, s, qt, kt: (h, 0, 0))

    return pl.pallas_call(
        body,
        name="fox_bwd",
        grid_spec=pltpu.PrefetchScalarGridSpec(
            num_scalar_prefetch=2, grid=(FOX_HEADS, n_steps),
            in_specs=[qlane(FOX_AUG), qrow(FOX_AUG), krow(FOX_AUG), klane(FOX_AUG), krow(FOX_DH), qrow(FOX_DH),
                      qlane(FOX_DH), qlane(FOX_DH), qlane(1)],
            out_specs=[head(S, FOX_DH), head(FOX_BIAS_ROWS, S), krow(FOX_DH), klane(FOX_BIAS_ROWS), krow(FOX_DH)],
            scratch_shapes=[pltpu.VMEM((n, FOX_AUG, t), F32), pltpu.VMEM((t, FOX_AUG), F32),
                            pltpu.VMEM((t, FOX_DH), F32)]),
        out_shape=[jax.ShapeDtypeStruct((FOX_HEADS, S, FOX_DH), F32),
                   jax.ShapeDtypeStruct((FOX_HEADS, FOX_BIAS_ROWS, S), F32),
                   jax.ShapeDtypeStruct((FOX_HEADS, S, FOX_DH), F32),
                   jax.ShapeDtypeStruct((FOX_HEADS, FOX_BIAS_ROWS, S), F32),
                   jax.ShapeDtypeStruct((FOX_HEADS, S, FOX_DH), F32)],
        compiler_params=pltpu.CompilerParams(
            dimension_semantics=("parallel", "arbitrary"), vmem_limit_bytes=VMEM_LIMIT_BYTES),
    )(qi_tab, ki_tab, qT, qa, ka, kT, v, do, doT, oT, lse)


def _fox_prep(u, fcum, S):
    T = min(512, S)
    head_of = jnp.arange(BRANCH) // FOX_DH
    dim_of = jnp.arange(BRANCH) % FOX_DH
    heads = jnp.arange(FOX_HEADS)[:, None, None]
    sel = (head_of[None, :, None] == heads) & (dim_of[None, :, None] == jnp.arange(FOX_AUG)[None, None, :])
    sel_q = (sel * FOX_SCALE).astype(BF16)
    sel_k = sel.astype(BF16)
    sel_vT = jnp.swapaxes(sel[:, :, :FOX_DH], 1, 2).astype(BF16)
    piece = jnp.arange(FOX_ONES * LANES) // LANES
    lane = jnp.arange(FOX_ONES * LANES) % LANES
    col = jnp.arange(FOX_AUG)[None, None, :]
    at_q = (lane[None, :, None] == heads) & (col == FOX_DH + FOX_ONES + piece[None, :, None])
    at_k = (lane[None, :, None] == heads) & (col == FOX_DH + piece[None, :, None])
    bias_q = at_q.astype(BF16)
    bias_k = (-at_k.astype(F32)).astype(BF16)
    cols = jnp.arange(FOX_AUG)[None, :]
    ones_q = ((cols >= FOX_DH) & (cols < FOX_DH + FOX_ONES)).astype(F32)
    ones_k = ((cols >= FOX_DH + FOX_ONES) & (cols < FOX_DH + 2 * FOX_ONES)).astype(F32)
    consts = [sel_q, sel_k, sel_vT, bias_q, bias_k, ones_q, ones_k]

    def body(cq, ck, cv, fc, sq, sk, svT, bq, bk, oq, ok, qa_o, ka_o, qT_o, kT_o, vh_o, vT_o):
        f = fc[...]
        hi = f.astype(BF16).astype(F32)
        mid = (f - hi).astype(BF16).astype(F32)
        lo = (f - hi - mid).astype(BF16).astype(F32)
        pieces = jnp.concatenate([hi, mid, lo], axis=1)
        q, k, v = cq[...], ck[...], cv[...]
        for h in range(FOX_HEADS):
            qa = _dot(q, sq[h], "nn") + _dot(pieces, bq[h], "nn") + oq[...]
            ka = _dot(k, sk[h], "nn") + _dot(pieces, bk[h], "nn") + ok[...]
            qa_o[h] = qa.astype(qa_o.dtype)
            ka_o[h] = ka.astype(ka_o.dtype)
            qT_o[h] = qa.T.astype(qT_o.dtype)
            kT_o[h] = ka.T.astype(kT_o.dtype)
            vT_o[h] = _dot(svT[h], v, "nt").astype(vT_o.dtype)
            vh_o[h] = _dot(v, svT[h], "nt").astype(vh_o.dtype)

    def win(off):
        return pl.BlockSpec((T, BRANCH), functools.partial(lambda i, blk: (i, blk), blk=off // BRANCH))

    def rows(c):
        return pl.BlockSpec((FOX_HEADS, T, c), lambda i: (0, i, 0))

    def lanes(r):
        return pl.BlockSpec((FOX_HEADS, r, T), lambda i: (0, 0, i))

    bf = lambda *shape: jax.ShapeDtypeStruct((FOX_HEADS,) + shape, BF16)
    return pl.pallas_call(
        body,
        name="fox_prep",
        grid=(S // T,),
        in_specs=[win(U_CQ), win(U_CK), win(U_CV), pl.BlockSpec((T, LANES), lambda i: (i, 0))]
        + [_full_spec(c) for c in consts],
        out_specs=[rows(FOX_AUG), rows(FOX_AUG), lanes(FOX_AUG), lanes(FOX_AUG), rows(FOX_DH), lanes(FOX_DH)],
        out_shape=[bf(S, FOX_AUG), bf(S, FOX_AUG), bf(FOX_AUG, S), bf(FOX_AUG, S), bf(S, FOX_DH), bf(FOX_DH, S)],
        compiler_params=pltpu.CompilerParams(dimension_semantics=("parallel",), vmem_limit_bytes=VMEM_LIMIT_BYTES),
    )(u, u, u, fcum, *consts)


def _to_heads(x2d, S):
    return jnp.transpose(x2d.reshape(S, FOX_HEADS, FOX_DH), (1, 0, 2))


def _from_heads(xh, S):
    return jnp.transpose(xh, (1, 0, 2)).reshape(S, FOX_HEADS * FOX_DH)


def _ffn_fwd(tag, x, wgT, wuT, wd, g, b, S):
    def up_epi(accs):
        gate, up = accs
        sil, _ = _silu_and_grad(gate)
        return [gate, up, sil * up]

    gate, up, act = _mm(tag + "_up", "nt", [x], [wgT, wuT], [(0, 0, 0), (1, 0, 1)], 2, up_epi, [],
                        [BF16, BF16, BF16], S, D_FF, D_MODEL, tn=1408)

    def down_epi(accs, xr, gg, bb):
        z = ALPHA * xr + 0.5 * accs[0]
        return [z, _ln_fwd(z, gg, bb)]

    z, xn = _mm(tag + "_down", "nn", [act], [wd], [(0, 0, 0)], 1, down_epi, [(x, "mn", 0), (g, "n"), (b, "n")],
                [F32, F32], S, D_MODEL, D_FF, tk=D_FF)
    return xn, dict(x=x, gate=gate, up=up, act=act, z=z)


def _ln_bwd_call(tag, dy, z, g, S):
    def fn(dy_t, z_t, g_t):
        dz, xhat = _ln_bwd(dy_t, z_t, g_t)
        return [dz], [_colsum(dy_t * xhat), _colsum(dy_t)]

    (dz,), (dg, db) = _rowwise(tag + "_ln_bwd", fn, [dy, z], [g], [(D_MODEL, F32)], [D_MODEL, D_MODEL], S)
    return dz, dg, db


def _ffn_bwd(tag, dxn, sv, wgT, wuT, wd, g, S):
    dz, dg, db = _ln_bwd_call(tag, dxn, sv["z"], g, S)

    def act_epi(accs, gate, up):
        da = 0.5 * accs[0]
        sil, dsil = _silu_and_grad(gate.astype(F32))
        return [da * up.astype(F32) * dsil, da * sil]

    dgate, dup = _mm(tag + "_dact", "nt", [dz], [wd], [(0, 0, 0)], 1, act_epi,
                     [(sv["gate"], "mn", 0), (sv["up"], "mn", 0)], [BF16, BF16], S, D_FF, D_MODEL, tn=1408)
    dwd = _mm1(tag + "_dwd", "tn", sv["act"], dz, D_FF, D_MODEL, S, scale=0.5, tm=1408)

    def two(accs):
        return [accs[0], accs[1]]

    dwgT, dwuT = _mm(tag + "_dwup", "tn", [dgate, dup], [sv["x"]], [(0, 0, 0), (1, 1, 0)], 2, two, [], [F32, F32],
                     D_FF, D_MODEL, S, tm=1408, tk=512)

    def dx_epi(accs, dzr):
        return [accs[0] + ALPHA * dzr]

    (dx,) = _mm(tag + "_dx", "nn", [dgate, dup], [wgT, wuT], [(0, 0, 0), (0, 1, 1)], 1, dx_epi, [(dz, "mn", 0)],
                [F32], S, D_MODEL, D_FF, tk=1408)
    return dx, dict(w_upT=jnp.concatenate([dwgT, dwuT], axis=0), w_down=dwd, ln_g=dg, ln_b=db)


def _mixer_fwd(x1, w, S):
    u = _mm1("w_in", "nt", x1, w["w_inT_p"], S, U_WIDTH, D_MODEL, tn=1536)
    ax0 = u[:, U_AX:U_AX + BRANCH]
    ax_shift = [ax0] + [_shift_down(ax0, k) for k in (1, 2, 3)]
    xc, r, gi, a, h, y_a = _lru_fwd(u, ax_shift, w["lru"], S)
    y_b, oraw, states = _gla_fwd(u, w["gla"], S)
    fcum = _fox_gate_fwd(u, w["bfp"], S)
    qa, ka, qT, kT, vh, vT = _fox_prep(u, fcum, S)
    oT, lse = _fox_fwd(qT, ka, vT, S)
    y_c = jnp.transpose(oT, (2, 0, 1)).reshape(S, BRANCH).astype(BF16)

    def merge_epi(accs, g0, g1, g2):
        merged = _sigmoid(g0) * accs[0] + _sigmoid(g1) * accs[1] + _sigmoid(g2) * accs[2]
        return [accs[0], accs[1], accs[2], merged]

    wb = w["w_branchT"]
    yp0, yp1, yp2, merged = _mm(
        "merge", "nt", [y_a, y_b, y_c], [wb[0], wb[1], wb[2]], [(0, 0, 0), (1, 1, 1), (2, 2, 2)], 3, merge_epi,
        [(u, "mn", 0), (u, "mn", 1), (u, "mn", 2)], [F32, F32, F32, BF16], S, D_MODEL, BRANCH, tm=256)

    def out_epi(accs, xr, gg, bb):
        z = ALPHA * xr + accs[0]
        return [z, _ln_fwd(z, gg, bb)]

    z2, x2 = _mm("w_out", "nn", [merged], [w["w_out"]], [(0, 0, 0)], 1, out_epi,
                 [(x1, "mn", 0), (w["ln2_g"], "n"), (w["ln2_b"], "n")], [F32, F32], S, D_MODEL, D_MODEL)
    sv = dict(x=x1, u=u, ax_shift=ax_shift, xc=xc, r=r, i=gi, a=a, h=h, y_a=y_a, y_b=y_b, y_c=y_c, oraw=oraw,
              states=states, qT=qT, qa=qa, ka=ka, kT=kT, vh=vh, oT=oT, lse=lse, yp=(yp0, yp1, yp2), merged=merged,
              z=z2)
    return x2, sv


def _mixer_bwd(dx2, sv, w, S):
    u = sv["u"]
    dz, dg2, db2 = _ln_bwd_call("mix", dx2, sv["z"], w["ln2_g"], S)

    def dm_epi(accs, y0, y1, y2, g0, g1, g2):
        dm = accs[0]
        outs_p, outs_g = [], []
        for yp, gl in ((y0, g0), (y1, g1), (y2, g2)):
            sg = _sigmoid(gl)
            outs_p.append(dm * sg)
            outs_g.append(dm * yp * sg * (1.0 - sg))
        return outs_p + outs_g

    yp = sv["yp"]
    dyp0, dyp1, dyp2, dgl0, dgl1, dgl2 = _mm(
        "dmerged", "nt", [dz], [w["w_out"]], [(0, 0, 0)], 1, dm_epi,
        [(yp[0], "mn", 0), (yp[1], "mn", 0), (yp[2], "mn", 0), (u, "mn", 0), (u, "mn", 1), (u, "mn", 2)],
        [BF16] * 6, S, D_MODEL, D_MODEL, tm=256)
    dw_out = _mm1("dw_out", "tn", sv["merged"], dz, D_MODEL, D_MODEL, S)
    wb = w["w_branchT"]
    dys, dwbs = [], []
    for j, (yj, dyp) in enumerate(((sv["y_a"], dyp0), (sv["y_b"], dyp1), (sv["y_c"], dyp2))):
        dys.append(_mm1("dy_branch%d" % j, "nn", dyp, wb[j], S, BRANCH, D_MODEL))
        dwbs.append(_mm1("dw_branch%d" % j, "tn", dyp, yj, D_MODEL, BRANCH, S))
    day, dxc, dwa, dwx, dba, dbx, dlam = _lru_bwd(dys[0], u, sv, w["lru"], S)
    dax, (dcw0, dcw1, dcw2, dcw3, dcb) = _conv_bwd(dxc, sv["ax_shift"], w["lru"], S)
    dbq, dbk, dbv, dbr, dglow, dwg2p, dbg, dng = _gla_bwd(dys[1], u, sv["oraw"], sv["states"], w["gla"], S)
    doh = _to_heads(dys[2], S)
    dqh, dfq, dkh, dfk, dvh = _fox_bwd(sv["qT"], sv["qa"], sv["ka"], sv["kT"], sv["vh"], doh,
                                       jnp.swapaxes(doh, 1, 2), sv["oT"], sv["lse"], S)
    dqh = dqh * FOX_SCALE
    dfc = jnp.transpose(dfq[:, FOX_ONES, :] - dfk[:, 0, :])
    dfc = jnp.pad(dfc, ((0, 0), (0, LANES - FOX_HEADS)))
    dcf, dbf = _fox_gate_bwd(dfc, u, w["bfp"], S)
    du = jnp.concatenate(
        [dgl0, dgl1, dgl2, dax, day, dbq, dbk, dbv, dbr, _from_heads(dqh, S).astype(BF16),
         _from_heads(dkh, S).astype(BF16), _from_heads(dvh, S).astype(BF16), dglow, dcf,
         jnp.zeros((S, U_WIDTH - U_CF - LANES), BF16)], axis=1)
    dw_inT_p = _mm1("dw_in", "tn", du, sv["x"], U_WIDTH, D_MODEL, S, tm=1536)

    def dx_epi(accs, dzr):
        return [accs[0] + ALPHA * dzr]

    (dx1,) = _mm("dx_mix", "nn", [du], [w["w_inT_p"]], [(0, 0, 0)], 1, dx_epi, [(dz, "mn", 0)], [F32], S, D_MODEL,
                 U_WIDTH, tk=1536)
    pieces = sorted(W_IN_SEGMENTS)
    dw_inT = jnp.concatenate([dw_inT_p[dst:dst + width] for _, width, dst in pieces], axis=0)
    eye = jnp.eye(LRU_BLOCKS, dtype=F32)
    dwa_b = jnp.einsum("ncmd,nm->ncd", dwa.reshape(LRU_BLOCKS, 64, LRU_BLOCKS, 64), eye)
    dwx_b = jnp.einsum("ncmd,nm->ncd", dwx.reshape(LRU_BLOCKS, 64, LRU_BLOCKS, 64), eye)
    grads = dict(
        w_inT=dw_inT, w_out=dw_out, w_branchT=jnp.stack(dwbs), ln2_g=dg2, ln2_b=db2,
        conv_w=jnp.concatenate([dcw0, dcw1, dcw2, dcw3], axis=0), conv_b=dcb, lru_wa=dwa_b, lru_wx=dwx_b,
        lru_ba=dba, lru_bx=dbx, lru_lambda=dlam, gla_w_g2=dwg2p[:GLA_LOWRANK], gla_b_g=dbg, gla_norm_g=dng,
        fox_b_f=dbf[:, :FOX_HEADS])
    return dx1, grads


def _ple_fwd(x3, p_i, w, S):
    pe = _mm1("ple_proj", "nt", p_i, w["ple_w_projT"], S, D_MODEL, PLE_DIM)

    def epi(accs, xr, per, bg, gg, bb):
        sg = _sigmoid(accs[0] + bg)
        z = ALPHA * xr + sg * per
        return [sg, z, _ln_fwd(z, gg, bb)]

    sg, z4, x4 = _mm("ple_gate", "nn", [x3], [w["ple_w_gate"]], [(0, 0, 0)], 1, epi,
                     [(x3, "mn", 0), (pe, "mn", 0), (w["ple_b_gate"], "n"), (w["ln4_g"], "n"), (w["ln4_b"], "n")],
                     [F32, F32, F32], S, D_MODEL, D_MODEL)
    return x4, dict(x=x3, p=p_i, pe=pe, sg=sg, z=z4)


def _ple_bwd(dx4, sv, w, S):
    def fn(dy_t, z_t, pe_t, sg_t, g_t):
        dz, xhat = _ln_bwd(dy_t, z_t, g_t)
        dgl = dz * pe_t * sg_t * (1.0 - sg_t)
        return [dz, dz * sg_t, dgl], [_colsum(dy_t * xhat), _colsum(dy_t), _colsum(dgl)]

    (dz, dpe, dgl), (dg4, db4, dbg) = _rowwise(
        "ple_bwd", fn, [dx4, sv["z"], sv["pe"], sv["sg"]], [w["ln4_g"]],
        [(D_MODEL, F32), (D_MODEL, BF16), (D_MODEL, BF16)], [D_MODEL] * 3, S)
    dwpT = _mm1("dw_ple_proj", "tn", dpe, sv["p"], D_MODEL, PLE_DIM, S)
    dwg = _mm1("dw_ple_gate", "tn", sv["x"], dgl, D_MODEL, D_MODEL, S)

    def dx_epi(accs, dzr):
        return [accs[0] + ALPHA * dzr]

    (dx3,) = _mm("dx_ple", "nt", [dgl], [w["ple_w_gate"]], [(0, 0, 0)], 1, dx_epi, [(dz, "mn", 0)], [F32], S,
                 D_MODEL, D_MODEL)
    return dx3, dict(ple_w_projT=dwpT, ple_w_gate=dwg, ple_b_gate=dbg, ln4_g=dg4, ln4_b=db4)


def _rows_of_all(g):
    return g.reshape((g.shape[0] * g.shape[1],) + g.shape[2:])


def _layer_weights(gathered, full, i):
    w = {}
    for tag in ("ffn1", "ffn2"):
        upT = _rows_of_all(gathered[tag + "_w_up"])
        w[tag] = (upT[:D_FF], upT[D_FF:], _rows_of_all(gathered[tag + "_w_down"]))
    w_inT = _rows_of_all(gathered["w_in"])
    placed = sorted((dst, src, width) for src, width, dst in W_IN_SEGMENTS)
    parts, pos = [], 0
    for dst, src, width in placed:
        if dst > pos:
            parts.append(jnp.zeros((dst - pos, D_MODEL), w_inT.dtype))
        parts.append(w_inT[src:src + width])
        pos = dst + width
    parts.append(jnp.zeros((U_WIDTH - pos, D_MODEL), w_inT.dtype))
    w["w_inT_p"] = jnp.concatenate(parts, axis=0)
    eye = jnp.eye(LRU_BLOCKS, dtype=F32)

    def dense(blocks):
        return jnp.einsum("ncd,nm->ncmd", blocks, eye).reshape(BRANCH, BRANCH).astype(BF16)

    def vec(name):
        return full[name][i].reshape(1, -1)

    cw = jnp.moveaxis(gathered["conv_w"], 0, 1).reshape(4, BRANCH)
    w_g2 = jnp.moveaxis(gathered["gla_w_g2"], 0, 1).reshape(GLA_LOWRANK, GLA_QK)
    w["lru"] = dict(cw0=cw[0:1], cw1=cw[1:2], cw2=cw[2:3], cw3=cw[3:4], conv_b=vec("conv_b"),
                    wa=dense(full["lru_wa"][i]), wx=dense(full["lru_wx"][i]), ba=vec("lru_ba"), bx=vec("lru_bx"),
                    lam=vec("lru_lambda"))
    hq = jnp.arange(GLA_QK) // GLA_DK
    hv = jnp.arange(GLA_V) // GLA_DV
    w["gla"] = dict(wg2=jnp.pad(w_g2, ((0, LANES - GLA_LOWRANK), (0, 0))).astype(BF16),
                    bg=vec("gla_b_g"), ng=vec("gla_norm_g"), bd=(hv[:, None] == hq[None, :]).astype(F32))
    w["bfp"] = jnp.pad(vec("fox_b_f"), ((0, 0), (0, LANES - FOX_HEADS)))
    w["w_branchT"] = jnp.moveaxis(gathered["w_branch"], 0, 1).reshape(3, D_MODEL, BRANCH)
    w["w_out"] = _rows_of_all(gathered["w_out"])
    w["ple_w_projT"] = _rows_of_all(gathered["ple_w_proj"])
    w["ple_w_gate"] = _rows_of_all(gathered["ple_w_gate"])
    for name in ("ln1_g", "ln1_b", "ln2_g", "ln2_b", "ln3_g", "ln3_b", "ln4_g", "ln4_b", "ple_b_gate"):
        w[name] = vec(name)
    return w


def _layer_fwd(x0, p_i, w, S):
    x1, s1 = _ffn_fwd("ffn1", x0, *w["ffn1"], w["ln1_g"], w["ln1_b"], S)
    x2, s2 = _mixer_fwd(x1, w, S)
    x3, s3 = _ffn_fwd("ffn2", x2, *w["ffn2"], w["ln3_g"], w["ln3_b"], S)
    x4, s4 = _ple_fwd(x3, p_i, w, S)
    return x4, (s1, s2, s3, s4)


def _layer_bwd(dx4, saved, w, S):
    s1, s2, s3, s4 = saved
    dx3, g4 = _ple_bwd(dx4, s4, w, S)
    dx2, g3 = _ffn_bwd("ffn2", dx3, s3, *w["ffn2"], w["ln3_g"], S)
    dx1, g2 = _mixer_bwd(dx2, s2, w, S)
    dx0, g1 = _ffn_bwd("ffn1", dx1, s1, *w["ffn1"], w["ln1_g"], S)
    grads = dict(g2)
    grads.update(g4)
    grads.update(ffn1_w_upT=g1["w_upT"], ffn1_w_down=g1["w_down"], ln1_g=g1["ln_g"], ln1_b=g1["ln_b"],
                 ffn2_w_upT=g3["w_upT"], ffn2_w_down=g3["w_down"], ln3_g=g3["ln_g"], ln3_b=g3["ln_b"])
    return dx0, grads


def _local_step(x, p, target, gathered, full):
    S = x.shape[0]
    ws = [_layer_weights(gathered[i], full, i) for i in range(DEPTH)]
    saved = []
    h = x
    for i in range(DEPTH):
        h, sv = _layer_fwd(h, p[i], ws[i], S)
        saved.append(sv)

    def loss_fn(y, t):
        err = y - t
        return [err * (1.0 / D_MODEL)], [_colsum(err * err) * (0.5 / D_MODEL)]

    (dy,), (lsum,) = _rowwise("loss", loss_fn, [h, target], [], [(D_MODEL, F32)], [D_MODEL], S)
    loss = jnp.sum(lsum)
    layer_grads = [None] * DEPTH
    for i in reversed(range(DEPTH)):
        dy, layer_grads[i] = _layer_bwd(dy, saved[i], ws[i], S)
    return loss, dy, layer_grads


def kernel(x, p, ffn1_w_up, ffn1_w_down, ln1_g, ln1_b, w_in, conv_w, conv_b, lru_wa, lru_ba, lru_wx, lru_bx, lru_lambda, gla_w_g2, gla_b_g, gla_norm_g, fox_b_f, w_branch, w_out, ln2_g, ln2_b, ffn2_w_up, ffn2_w_down, ln3_g, ln3_b, ple_w_proj, ple_w_gate, ple_b_gate, ln4_g, ln4_b, loss_target, m_ffn1_w_up, m_ffn1_w_down, m_ln1_g, m_ln1_b, m_w_in, m_conv_w, m_conv_b, m_lru_wa, m_lru_ba, m_lru_wx, m_lru_bx, m_lru_lambda, m_gla_w_g2, m_gla_b_g, m_gla_norm_g, m_fox_b_f, m_w_branch, m_w_out, m_ln2_g, m_ln2_b, m_ffn2_w_up, m_ffn2_w_down, m_ln3_g, m_ln3_b, m_ple_w_proj, m_ple_w_gate, m_ple_b_gate, m_ln4_g, m_ln4_b, v_ffn1_w_up, v_ffn1_w_down, v_ln1_g, v_ln1_b, v_w_in, v_conv_w, v_conv_b, v_lru_wa, v_lru_ba, v_lru_wx, v_lru_bx, v_lru_lambda, v_gla_w_g2, v_gla_b_g, v_gla_norm_g, v_fox_b_f, v_w_branch, v_w_out, v_ln2_g, v_ln2_b, v_ffn2_w_up, v_ffn2_w_down, v_ln3_g, v_ln3_b, v_ple_w_proj, v_ple_w_gate, v_ple_b_gate, v_ln4_g, v_ln4_b):
    env = dict(locals())
    wts = {n: env[n] for n in WEIGHTS}
    ms = {n: env["m_" + n] for n in WEIGHTS}
    vs = {n: env["v_" + n] for n in WEIGHTS}
    sharded = [n for n, _ in SHARDED]
    keys = [(i, n) for i in range(DEPTH) for n in sharded]

    def travel(n, a):
        return jnp.swapaxes(a, -1, -2) if n in COLUMN_SHARDED else a

    shards = [travel(n, wts[n][i]) for i, n in keys]
    shards = [a if n in SHARDED_F32_GATHER else a.astype(BF16) for (i, n), a in zip(keys, shards)]
    gathered = [{} for _ in range(DEPTH)]
    for (i, n), g in zip(keys, _allgather_multi("gather_weights", shards)):
        gathered[i][n] = g
    full = {n: wts[n] for n in REPLICATED}

    loss_part, grad_x, layer_grads = _local_step(x[0], p[:, 0], loss_target[0], gathered, full)
    loss = lax.psum(loss_part, MESH_AXES)

    dest = [_dest_pieces(n, layer_grads[i][n + "T" if n in COLUMN_SHARDED else n]) for i, n in keys]
    got = _sibling_swap_multi("grad_sibling_swap", dest)
    core = lax.axis_index("c").astype(jnp.int32).reshape(1)
    pairs = [_pair_add("grad_pair_add_%s_%d" % (n, i), core, _as_rows(d, 2), _as_rows(g, 1))
             for (i, n), d, g in zip(keys, dest, got)]
    parts = _chip_all_to_all_multi("grad_chip_all_to_all", pairs)
    rep = list(REPLICATED)
    rep_grads = [jnp.stack([layer_grads[i][n] for i in range(DEPTH)]).reshape(wts[n].shape) for n in rep]
    (gr,) = _allgather_multi("grad_gather_replicated", [_pack(rep_grads, F32)])

    kinds = ("grad", "delta", "new_m", "new_v")
    local_parts = {key: _as_rows(travel(key[1], gp.reshape((4,) + sh.shape)), 1)
                   for key, gp, sh in zip(keys, parts, shards)}
    out = {}
    for n in sharded:
        res = _adamw("adamw_" + n, [local_parts[i, n] for i in range(DEPTH)], _as_rows(wts[n], 1),
                     _as_rows(ms[n], 1), _as_rows(vs[n], 1))
        for kind, arr in zip(kinds, res):
            out[kind + "_" + n] = arr.reshape(wts[n].shape)
    res = _adamw("adamw_replicated", [gr], _pack([wts[n] for n in rep], F32)[None],
                 _pack([ms[n] for n in rep], F32)[None], _pack([vs[n] for n in rep], F32)[None])
    shapes = [wts[n].shape for n in rep]
    for kind, buf in zip(kinds, res):
        for n, arr in zip(rep, _unpack(buf[0], shapes)):
            out[kind + "_" + n] = arr
    return (loss, grad_x[None], *[out["grad_" + n] for n in WEIGHTS], *[out["delta_" + n] for n in WEIGHTS],
            *[out["new_m_" + n] for n in WEIGHTS], *[out["new_v_" + n] for n in WEIGHTS])
```

```python
import functools
import math

import jax
import jax.numpy as jnp
from jax import lax
from jax.experimental import pallas as pl
from jax.experimental.pallas import tpu as pltpu

F32 = jnp.float32
BF16 = jnp.bfloat16

N_DEV = 8
MESH_AXES = ("x", "y", "c")
DEPTH = 2
D_MODEL = 1024
D_FF = 2816
BRANCH = 512
CHUNK = 64
GLA_HEADS = 4
GLA_DK = 64
GLA_DV = 128
GLA_LOWRANK = 16
GLA_TAU = 16.0
FOX_HEADS = 8
FOX_DH = 64
PLE_DIM = 256
LRU_C = 8.0
LRU_BLOCKS = 8
LN_EPS = 1e-5
RMS_EPS = 1e-6
ALPHA = (2 * DEPTH) ** 0.25
LANES = 128
NEG_BIG = -1e30

ADAM_LR = 0.001
ADAM_B1 = 0.9
ADAM_B2 = 0.999
ADAM_EPS = 1e-08
ADAM_WD = 0.01
ADAM_STEP = 10

VMEM_LIMIT_BYTES = 56 * 1024 * 1024

U_GATES = 0
U_AX = 3072
U_AY = 3584
U_BQ = 4096
U_BK = 4352
U_BV = 4608
U_BR = 5120
U_CQ = 5632
U_CK = 6144
U_CV = 6656
U_BLOW = 7168
U_CF = 7296
U_WIDTH = 7680
W_IN_SEGMENTS = (
    (0, 512, U_AX), (512, 512, U_AY), (1024, 256, U_BQ), (1280, 256, U_BK), (1536, 512, U_BV),
    (2048, 16, U_BLOW), (2064, 512, U_BR), (2576, 512, U_CQ), (3088, 512, U_CK), (3600, 512, U_CV),
    (4112, 8, U_CF), (4120, 3072, U_GATES),
)

SHARDED = (
    ("ffn1_w_up", 2), ("ffn1_w_down", 1), ("w_in", 2), ("conv_w", 2), ("gla_w_g2", 2), ("w_branch", 3),
    ("w_out", 1), ("ffn2_w_up", 2), ("ffn2_w_down", 1), ("ple_w_proj", 2), ("ple_w_gate", 1),
)
SHARDED_F32_GATHER = ("conv_w", "gla_w_g2")
COLUMN_SHARDED = ("ffn1_w_up", "ffn2_w_up", "w_in", "w_branch", "ple_w_proj")
REPLICATED = ("ln1_g", "ln1_b", "conv_b", "lru_wa", "lru_ba", "lru_wx", "lru_bx", "lru_lambda", "gla_b_g",
              "gla_norm_g", "fox_b_f", "ln2_g", "ln2_b", "ln3_g", "ln3_b", "ple_b_gate", "ln4_g", "ln4_b")
WEIGHTS = ("ffn1_w_up", "ffn1_w_down", "ln1_g", "ln1_b", "w_in", "conv_w", "conv_b", "lru_wa", "lru_ba", "lru_wx",
           "lru_bx", "lru_lambda", "gla_w_g2", "gla_b_g", "gla_norm_g", "fox_b_f", "w_branch", "w_out", "ln2_g",
           "ln2_b", "ffn2_w_up", "ffn2_w_down", "ln3_g", "ln3_b", "ple_w_proj", "ple_w_gate", "ple_b_gate", "ln4_g",
           "ln4_b")


def _sigmoid(x):
    return 1.0 / (1.0 + jnp.exp(-x))


def _log1p_pos(e):
    return jnp.where(e < 1e-4, e * (1.0 - 0.5 * e), jnp.log(1.0 + e))


def _softplus(x):
    return jnp.maximum(x, 0.0) + _log1p_pos(jnp.exp(-jnp.abs(x)))


def _log_sigmoid(x):
    return -_softplus(-x)


def _neg_expm1(y):
    series = -y * (1.0 + y * (0.5 + y * (1.0 / 6.0 + y * (1.0 / 24.0 + y * (1.0 / 120.0)))))
    return jnp.where(y > -0.1, series, 1.0 - jnp.exp(y))


def _silu_and_grad(x):
    s = _sigmoid(x)
    return x * s, s * (1.0 + x * (1.0 - s))


_GELU_C = math.sqrt(2.0 / math.pi)


def _gelu_and_grad(x):
    inner = _GELU_C * (x + 0.044715 * x * x * x)
    t = jnp.tanh(inner)
    g = 0.5 * x * (1.0 + t)
    dg = 0.5 * (1.0 + t) + 0.5 * x * (1.0 - t * t) * _GELU_C * (1.0 + 3.0 * 0.044715 * x * x)
    return g, dg


def _ln_stats(z):
    mu = jnp.mean(z, axis=-1, keepdims=True)
    zc = z - mu
    var = jnp.mean(zc * zc, axis=-1, keepdims=True)
    rstd = lax.rsqrt(var + LN_EPS)
    return zc * rstd, rstd


def _ln_fwd(z, g, b):
    xhat, _ = _ln_stats(z)
    return xhat * g + b


def _ln_bwd(dy, z, g):
    xhat, rstd = _ln_stats(z)
    dxh = dy * g
    m1 = jnp.mean(dxh, axis=-1, keepdims=True)
    m2 = jnp.mean(dxh * xhat, axis=-1, keepdims=True)
    return rstd * (dxh - m1 - xhat * m2), xhat


def _colsum(x):
    return jnp.sum(x, axis=0, keepdims=True)


def _dot(a, b, dims):
    dn = {"nn": (((1,), (0,)), ((), ())), "nt": (((1,), (1,)), ((), ())), "tn": (((0,), (0,)), ((), ()))}[dims]
    return lax.dot_general(a.astype(BF16), b.astype(BF16), dn, preferred_element_type=F32)


def _scan_rows(a, b, length, reverse=False, seg=None):
    rows = lax.broadcasted_iota(jnp.int32, b.shape, 0)
    span = seg if seg else length
    pos = rows % span if seg else rows
    d = 1
    while d < span:
        shift = (length - d) if reverse else d
        valid = (pos < span - d) if reverse else (pos >= d)
        sb = jnp.where(valid, pltpu.roll(b, shift, 0), 0.0)
        if a is None:
            b = b + sb
        else:
            b = b + a * sb
            a = a * jnp.where(valid, pltpu.roll(a, shift, 0), 1.0)
        d *= 2
    return a, b


def _tile(dim, pref):
    if dim <= pref:
        return dim
    best = None
    t = LANES
    while t <= pref:
        if dim % t == 0:
            best = t
        t += LANES
    assert best is not None, (dim, pref)
    return best


def _full_spec(arr):
    nd = arr.ndim
    return pl.BlockSpec(arr.shape, lambda *_: (0,) * nd)


def _mm(name, dims, a_ops, b_ops, terms, n_acc, epilogue, extras, out_dtypes, M, N, K, tm=512, tn=1024, tk=1024):
    tm, tn, tk = _tile(M, tm), _tile(N, tn), _tile(K, tk)
    gm, gn, gk = M // tm, N // tn, K // tk
    a_bytes = sum(a.size * a.dtype.itemsize for a in a_ops)
    b_bytes = sum(b.size * b.dtype.itemsize for b in b_ops)
    n_outer = gk == 1 and b_bytes + a_bytes * gn < a_bytes + b_bytes * gm

    def spec(shape, fn):
        if n_outer:
            return pl.BlockSpec(shape, lambda j, i, k: fn(i, j, k))
        return pl.BlockSpec(shape, fn)

    if dims == "tn":
        a_spec = spec((tk, tm), lambda i, j, k: (k, i))
    else:
        a_spec = spec((tm, tk), lambda i, j, k: (i, k))
    if dims == "nt":
        b_spec = spec((tn, tk), lambda i, j, k: (j, k))
    else:
        b_spec = spec((tk, tn), lambda i, j, k: (k, j))
    e_specs, e_arrays = [], []
    for ex in extras:
        if ex[1] == "mn":
            e_specs.append(spec((tm, tn), functools.partial(lambda i, j, k, off: (i, j + off), off=ex[2])))
        else:
            e_specs.append(spec((1, tn), lambda i, j, k: (0, j)))
        e_arrays.append(ex[0])
    na, nb, ne, no = len(a_ops), len(b_ops), len(extras), len(out_dtypes)

    def body(*refs):
        a_refs = refs[:na]
        b_refs = refs[na:na + nb]
        e_refs = refs[na + nb:na + nb + ne]
        o_refs = refs[na + nb + ne:na + nb + ne + no]
        acc_refs = refs[na + nb + ne + no:]
        k = pl.program_id(2)

        @pl.when(k == 0)
        def _():
            for acc in acc_refs:
                acc[...] = jnp.zeros_like(acc)

        for r, ai, bi in terms:
            acc_refs[r][...] += _dot(a_refs[ai][...], b_refs[bi][...], dims)

        @pl.when(k == gk - 1)
        def _():
            res = epilogue([acc[...] for acc in acc_refs], *[e[...] for e in e_refs])
            for o, val in zip(o_refs, res):
                o[...] = val.astype(o.dtype)

    outs = pl.pallas_call(
        body,
        name=name,
        grid=(gn, gm, gk) if n_outer else (gm, gn, gk),
        in_specs=[a_spec] * na + [b_spec] * nb + e_specs,
        out_specs=[spec((tm, tn), lambda i, j, k: (i, j))] * no,
        out_shape=[jax.ShapeDtypeStruct((M, N), dt) for dt in out_dtypes],
        scratch_shapes=[pltpu.VMEM((tm, tn), F32)] * n_acc,
        compiler_params=pltpu.CompilerParams(
            dimension_semantics=("parallel", "parallel", "arbitrary"), vmem_limit_bytes=VMEM_LIMIT_BYTES),
    )(*a_ops, *b_ops, *e_arrays)
    return outs


def _mm1(name, dims, a, b, M, N, K, out_dtype=F32, scale=None, **kw):
    def epi(accs):
        return [accs[0] if scale is None else accs[0] * scale]
    return _mm(name, dims, [a], [b], [(0, 0, 0)], 1, epi, [], [out_dtype], M, N, K, **kw)[0]


def _rowwise(name, fn, row_ins, vec_ins, row_outs, sum_outs, S, tr=256, reverse=False):
    tr = min(tr, S)
    g = S // tr
    rmap = (lambda i: (g - 1 - i)) if reverse else (lambda i: i)
    in_specs, arrays = [], []
    for r in row_ins:
        if isinstance(r, tuple):
            arr, width, blk = r
            in_specs.append(pl.BlockSpec((tr, width), functools.partial(lambda i, blk: (rmap(i), blk), blk=blk)))
        else:
            arr = r
            in_specs.append(pl.BlockSpec((tr, arr.shape[1]), lambda i: (rmap(i), 0)))
        arrays.append(arr)
    for v in vec_ins:
        in_specs.append(_full_spec(v))
        arrays.append(v)
    nr, nv, no, ns = len(row_ins), len(vec_ins), len(row_outs), len(sum_outs)

    def body(*refs):
        ins = [r[...] for r in refs[:nr + nv]]
        o_refs = refs[nr + nv:nr + nv + no]
        s_refs = refs[nr + nv + no:]
        outs, sums = fn(*ins)
        for o, val in zip(o_refs, outs):
            o[...] = val.astype(o.dtype)
        if ns:
            i = pl.program_id(0)

            @pl.when(i == 0)
            def _():
                for s, val in zip(s_refs, sums):
                    s[...] = val

            @pl.when(i > 0)
            def _():
                for s, val in zip(s_refs, sums):
                    s[...] += val

    res = pl.pallas_call(
        body,
        name=name,
        grid=(g,),
        in_specs=in_specs,
        out_specs=[pl.BlockSpec((tr, c), lambda i: (rmap(i), 0)) for c, _ in row_outs]
        + [pl.BlockSpec((1, c), lambda i: (0, 0)) for c in sum_outs],
        out_shape=[jax.ShapeDtypeStruct((S, c), dt) for c, dt in row_outs]
        + [jax.ShapeDtypeStruct((1, c), F32) for c in sum_outs],
        compiler_params=pltpu.CompilerParams(
            dimension_semantics=("arbitrary",), vmem_limit_bytes=VMEM_LIMIT_BYTES),
    )(*arrays)
    return res[:no], res[no:]


def _win(arr, offset, width):
    assert offset % width == 0
    return (arr, width, offset // width)


MESH_ID = pl.DeviceIdType.MESH


def _remote(src, dst, send_sem, recv_sem, to):
    return pltpu.make_async_remote_copy(src_ref=src, dst_ref=dst, send_sem=send_sem, recv_sem=recv_sem,
                                        device_id=to, device_id_type=MESH_ID)


def _hbm_call(name, body, arrs, out_shapes, n_send, n_recv, n_local):
    return pl.pallas_call(
        body,
        name=name,
        in_specs=[pl.BlockSpec(memory_space=pltpu.HBM)] * len(arrs),
        out_specs=[pl.BlockSpec(memory_space=pltpu.HBM)] * len(out_shapes),
        out_shape=out_shapes,
        scratch_shapes=[pltpu.SemaphoreType.DMA((n_send,)), pltpu.SemaphoreType.DMA((n_recv,)),
                        pltpu.SemaphoreType.DMA((n_local,))],
        compiler_params=pltpu.CompilerParams(has_side_effects=True),
    )(*arrs)


def _side_job(arrs, out_shapes, n_send, n_recv, n_local, phases):
    return dict(arrs=list(arrs), out_shapes=list(out_shapes), sems=(n_send, n_recv, n_local), phases=phases)


def _side_specs(side):
    hbm = pl.BlockSpec(memory_space=pltpu.HBM)
    sems = [pltpu.SemaphoreType.DMA((k,)) for k in side["sems"]]
    return [hbm] * len(side["arrs"]), [hbm] * len(side["out_shapes"]), sems


def _gather_job(arrs):
    n = len(arrs)

    def plan(ins, outs, send_sems, recv_sems, local_sems):
        x, y, c = lax.axis_index("x"), lax.axis_index("y"), lax.axis_index("c")
        me, sibling = (x, y, c), (x, y, 1 - c)
        chips = [(1 - x, y), (x, 1 - y), (1 - x, 1 - y)]

        def slot(i, dev):
            return outs[i].at[4 * dev[0] + 2 * dev[1] + dev[2]]

        def copy(i, k, block, to, src=None):
            dst = slot(i, block)
            return _remote(dst if src is None else src, dst, send_sems.at[7 * i + k], recv_sems.at[7 * i + k], to)

        mine = [pltpu.make_async_copy(ins[i], slot(i, me), local_sems.at[i]) for i in range(n)]
        first = []
        for i in range(n):
            first.append(copy(i, 0, me, sibling, src=ins[i]))
            first += [copy(i, 1 + j, me, (*chip, c), src=ins[i]) for j, chip in enumerate(chips)]
        arrive = [[copy(i, 1 + j, (*chip, c), me) for i in range(n)] for j, chip in enumerate(chips)]
        passed = [[copy(i, 4 + j, (*chip, c), sibling) for i in range(n)] for j, chip in enumerate(chips)]
        last = [copy(i, 0, sibling, me) for i in range(n)]
        last += [copy(i, 4 + j, (*chip, 1 - c), me) for i in range(n) for j, chip in enumerate(chips)]
        return mine, first, arrive, passed, last

    def start(*refs):
        mine, first, _, _, _ = plan(*refs)
        for cp in mine + first:
            cp.start()

    def forward(*refs):
        _, _, arrive, passed, _ = plan(*refs)
        for came, onward in zip(arrive, passed):
            for a, p in zip(came, onward):
                a.wait_recv()
                p.start()

    def finish(*refs):
        mine, first, _, passed, last = plan(*refs)
        for cp in last:
            cp.wait_recv()
        for cp in first + [p for onward in passed for p in onward]:
            cp.wait_send()
        for cp in mine:
            cp.wait()

    outs = [jax.ShapeDtypeStruct((N_DEV,) + a.shape, a.dtype) for a in arrs]
    return _side_job(arrs, outs, 7 * n, 7 * n, n, [start, forward, finish])


def _scatter_job(arrs):
    n = len(arrs)

    def plan(ins, outs, send_sems, recv_sems, local_sems):
        x, y, c = lax.axis_index("x"), lax.axis_index("y"), lax.axis_index("c")
        here = 2 * x + y
        local = [pltpu.make_async_copy(ins[i].at[here, c], outs[i].at[here, c], local_sems.at[i]) for i in range(n)]
        sends, recvs = [], []
        for i in range(n):
            for k in range(1, N_DEV):
                px = 1 - x if k & 4 else x
                py = 1 - y if k & 2 else y
                pc = 1 - c if k & 1 else c
                sems = (send_sems.at[7 * i + k - 1], recv_sems.at[7 * i + k - 1], (px, py, pc))
                sends.append(_remote(ins[i].at[2 * px + py, pc], outs[i].at[here, c], *sems))
                recvs.append(_remote(ins[i].at[2 * px + py, pc], outs[i].at[2 * px + py, pc], *sems))
        return local, sends, recvs

    def start(*refs):
        local, sends, _ = plan(*refs)
        for cp in local + sends:
            cp.start()

    def finish(*refs):
        local, sends, recvs = plan(*refs)
        for cp in recvs:
            cp.wait_recv()
        for cp in sends:
            cp.wait_send()
        for cp in local:
            cp.wait()

    outs = [jax.ShapeDtypeStruct(a.shape, a.dtype) for a in arrs]
    return _side_job(arrs, outs, 7 * n, 7 * n, n, [start, finish])


def _run_job(name, job):
    na, no = len(job["arrs"]), len(job["out_shapes"])

    def body(*refs):
        ins, outs, sems = refs[:na], refs[na:na + no], refs[na + no:]
        for phase in job["phases"]:
            phase(ins, outs, *sems)

    return _hbm_call(name, body, job["arrs"], job["out_shapes"], *job["sems"])


def _allgather_multi(name, arrs):
    return _run_job(name, _gather_job(arrs))


def _sibling_swap_multi(name, arrs):
    n = len(arrs)
    per = 4

    def body(*refs):
        ins, got = refs[:n], refs[n:2 * n]
        send_sems, recv_sems, _ = refs[2 * n:]
        x, y, c = lax.axis_index("x"), lax.axis_index("y"), lax.axis_index("c")
        sibling = (x, y, 1 - c)
        sends = []
        for i in range(n):
            for a in range(4):
                k = per * i + a
                sends.append(_remote(ins[i].at[a, 1 - c], got[i].at[a], send_sems.at[k], recv_sems.at[k], sibling))
        for cp in sends:
            cp.start()
        for cp in sends:
            cp.wait_recv()
        for cp in sends:
            cp.wait_send()

    outs = [jax.ShapeDtypeStruct((4,) + a.shape[2:], a.dtype) for a in arrs]
    return _hbm_call(name, body, arrs, outs, per * n, per * n, 1)


def _chip_all_to_all_multi(name, arrs):
    n = len(arrs)

    def body(*refs):
        ins, outs = refs[:n], refs[n:2 * n]
        send_sems, recv_sems, local_sems = refs[2 * n:]
        x, y, c = lax.axis_index("x"), lax.axis_index("y"), lax.axis_index("c")
        mine = 2 * x + y
        chips = [(1 - x, y), (x, 1 - y), (1 - x, 1 - y)]
        local = [pltpu.make_async_copy(ins[i].at[mine], outs[i].at[mine], local_sems.at[i]) for i in range(n)]
        for cp in local:
            cp.start()
        sends, recvs = [], []
        for i in range(n):
            for j, (px, py) in enumerate(chips):
                peer = 2 * px + py
                sems = (send_sems.at[3 * i + j], recv_sems.at[3 * i + j], (px, py, c))
                sends.append(_remote(ins[i].at[peer], outs[i].at[mine], *sems))
                recvs.append(_remote(ins[i].at[peer], outs[i].at[peer], *sems))
        for cp in sends:
            cp.start()
        for cp in recvs:
            cp.wait_recv()
        for cp in sends:
            cp.wait_send()
        for cp in local:
            cp.wait()

    outs = [jax.ShapeDtypeStruct(a.shape, a.dtype) for a in arrs]
    return _hbm_call(name, body, arrs, outs, 3 * n, 3 * n, n)


def _as_rows(a, lead):
    return a.reshape(a.shape[:lead] + (-1, a.shape[-1]))


def _row_tile(rows, cols, parts):
    budget = 4 * 1024 * 1024 // (4 * max(cols, LANES) * parts)
    return _tile_rows(rows, max(8, min(512, budget // 8 * 8)))


def _pair_add(name, core, both, got):
    _, rows, cols = got.shape
    tr = _row_tile(rows, cols, 2)

    def body(c_ref, a_ref, b_ref, o_ref):
        o_ref[...] = (a_ref[...] + b_ref[...]).astype(o_ref.dtype)

    blk = pl.BlockSpec((1, tr, cols), lambda ch, i, c_ref: (ch, i, 0))
    return pl.pallas_call(
        body, name=name,
        grid_spec=pltpu.PrefetchScalarGridSpec(
            num_scalar_prefetch=1, grid=(4, rows // tr),
            in_specs=[pl.BlockSpec((1, None, tr, cols), lambda ch, i, c_ref: (ch, c_ref[0], i, 0)), blk],
            out_specs=blk),
        out_shape=jax.ShapeDtypeStruct(got.shape, BF16),
        compiler_params=pltpu.CompilerParams(dimension_semantics=("parallel", "parallel"),
                                             vmem_limit_bytes=VMEM_LIMIT_BYTES),
    )(core, both, got)


def _adamw(name, gparts, w, m, v):
    layers = len(gparts)
    _, rows, cols = gparts[0].shape
    tr = _row_tile(rows, cols, sum(gp.shape[0] for gp in gparts))
    c1 = 1.0 / (1.0 - ADAM_B1 ** ADAM_STEP)
    c2 = 1.0 / (1.0 - ADAM_B2 ** ADAM_STEP)

    def body(*refs):
        gp_refs = refs[:layers]
        w_ref, m_ref, v_ref, g_ref, d_ref, nm_ref, nv_ref = refs[layers:]
        layer = pl.program_id(0)
        g = None
        for k, gp_ref in enumerate(gp_refs):
            gk = gp_ref[0].astype(F32)
            for i in range(1, gp_ref.shape[0]):
                gk = gk + gp_ref[i].astype(F32)
            g = gk if g is None else jnp.where(layer == k, gk, g)
        nm = ADAM_B1 * m_ref[...] + (1.0 - ADAM_B1) * g
        nv = ADAM_B2 * v_ref[...] + (1.0 - ADAM_B2) * (g * g)
        m_hat = nm * c1
        v_hat = nv * c2
        g_ref[...] = g
        nm_ref[...] = nm
        nv_ref[...] = nv
        d_ref[...] = -ADAM_LR * (m_hat / (jnp.sqrt(v_hat) + ADAM_EPS) + ADAM_WD * w_ref[...])

    row = pl.BlockSpec((None, tr, cols), lambda l, i: (l, i, 0))
    return pl.pallas_call(
        body,
        name=name,
        grid=(layers, rows // tr),
        in_specs=[pl.BlockSpec((gp.shape[0], tr, cols), lambda l, i: (0, i, 0)) for gp in gparts] + [row, row, row],
        out_specs=[row] * 4,
        out_shape=[jax.ShapeDtypeStruct((layers, rows, cols), F32)] * 4,
        compiler_params=pltpu.CompilerParams(dimension_semantics=("parallel", "parallel"),
                                             vmem_limit_bytes=VMEM_LIMIT_BYTES),
    )(*gparts, w, m, v)


def _tile_rows(rows, pref):
    t = min(pref, rows) // 8 * 8
    while t >= 8 and rows % t:
        t -= 8
    return t if t >= 8 else rows


PACK_ROWS = 512


def _pack(arrs, dtype):
    flat = jnp.concatenate([a.astype(dtype).reshape(-1) for a in arrs])
    quantum = PACK_ROWS * LANES
    padded = -(-flat.shape[0] // quantum) * quantum
    return jnp.pad(flat, (0, padded - flat.shape[0])).reshape(-1, LANES)


def _unpack(buf, shapes, lead=()):
    flat = buf.reshape(lead + (-1,))
    out, off = [], 0
    for shp in shapes:
        n = math.prod(shp)
        out.append(flat[..., off:off + n].reshape(lead + tuple(shp)))
        off += n
    return out


def _dest_pieces(name, g):
    if name == "w_branch":
        return jnp.moveaxis(g.reshape(3, 4, 2, D_MODEL // N_DEV, BRANCH), 0, 2)
    if name in SHARDED_F32_GATHER:
        return jnp.moveaxis(g.reshape(g.shape[0], 4, 2, -1), 0, 2)
    return g.reshape((4, 2, g.shape[0] // N_DEV) + g.shape[1:])


def _shift_down(a, k):
    return jnp.pad(a, ((k, 0), (0, 0)))[:a.shape[0]] if k else a


def _shift_up(a, k, fill=0.0):
    return jnp.pad(a, ((0, k), (0, 0)), constant_values=fill)[k:] if k else a


def _lru_fwd(u, ax_shift, lw, S):
    T = min(256, S)
    nb = S // T
    row = pl.BlockSpec((T, BRANCH), lambda t: (t, 0))
    vecs = [lw["cw0"], lw["cw1"], lw["cw2"], lw["cw3"], lw["conv_b"], lw["wa"], lw["wx"], lw["ba"], lw["bx"],
            lw["lam"]]

    def body(ax0, ax1, ax2, ax3, ay, cw0, cw1, cw2, cw3, cb, wa, wx, ba, bx, lam, xc_o, r_o, i_o, a_o, h_o, ya_o, hc):
        t = pl.program_id(0)

        @pl.when(t == 0)
        def _():
            hc[...] = jnp.zeros_like(hc)

        xc = cw3[...] * ax0[...] + cw2[...] * ax1[...] + cw1[...] * ax2[...] + cw0[...] * ax3[...] + cb[...]
        r = _sigmoid(_dot(xc, wa[...], "nn") + ba[...])
        gi = _sigmoid(_dot(xc, wx[...], "nn") + bx[...])
        sp = _softplus(-lam[...])
        la = -LRU_C * r * sp
        a = jnp.exp(la)
        mult = jnp.sqrt(_neg_expm1(2.0 * la))
        A, B = _scan_rows(a, mult * gi * xc, T)
        h = B + A * hc[...]
        h_o[...] = h
        hc[...] = h_o[pl.ds(T - 1, 1), :]
        xc_o[...] = xc
        r_o[...] = r
        i_o[...] = gi
        a_o[...] = a
        gy, _ = _gelu_and_grad(ay[...])
        ya_o[...] = (gy * h).astype(ya_o.dtype)

    outs = pl.pallas_call(
        body,
        name="lru_fwd",
        grid=(nb,),
        in_specs=[pl.BlockSpec((T, BRANCH), lambda t: (t, U_AX // BRANCH))] + [row] * 3
        + [pl.BlockSpec((T, BRANCH), lambda t: (t, U_AY // BRANCH))] + [_full_spec(v) for v in vecs],
        out_specs=[row] * 6,
        out_shape=[jax.ShapeDtypeStruct((S, BRANCH), F32)] * 5 + [jax.ShapeDtypeStruct((S, BRANCH), BF16)],
        scratch_shapes=[pltpu.VMEM((1, BRANCH), F32)],
        compiler_params=pltpu.CompilerParams(dimension_semantics=("arbitrary",), vmem_limit_bytes=VMEM_LIMIT_BYTES),
    )(u, *ax_shift[1:], u, *vecs)
    return outs


def _lru_bwd(dya, u, sv, lw, S):
    T = min(256, S)
    nb = S // T
    rrow = pl.BlockSpec((T, BRANCH), lambda t: (nb - 1 - t, 0))
    sq = pl.BlockSpec((BRANCH, BRANCH), lambda t: (0, 0))
    vrow = pl.BlockSpec((1, BRANCH), lambda t: (0, 0))
    h_prev = _shift_down(sv["h"], 1)
    a_next = _shift_up(sv["a"], 1)

    def body(dya_r, ay, h, hp, xc_r, r_r, i_r, a_r, an, wa, wx, lam,
             day_o, dxc_o, dwa_o, dwx_o, dba_o, dbx_o, dlam_o, lcar, tmp):
        t = pl.program_id(0)

        @pl.when(t == 0)
        def _():
            lcar[...] = jnp.zeros_like(lcar)
            dwa_o[...] = jnp.zeros_like(dwa_o)
            dwx_o[...] = jnp.zeros_like(dwx_o)
            dba_o[...] = jnp.zeros_like(dba_o)
            dbx_o[...] = jnp.zeros_like(dbx_o)
            dlam_o[...] = jnp.zeros_like(dlam_o)

        gy, dgy = _gelu_and_grad(ay[...])
        dy = dya_r[...]
        day_o[...] = (dy * h[...] * dgy).astype(day_o.dtype)
        A, B = _scan_rows(an[...], dy * gy, T, reverse=True)
        lmb = B + A * lcar[...]
        tmp[...] = lmb
        lcar[...] = tmp[pl.ds(0, 1), :]
        xc, r, gi, a = xc_r[...], r_r[...], i_r[...], a_r[...]
        sp = _softplus(-lam[...])
        la = -LRU_C * r * sp
        mult = jnp.sqrt(_neg_expm1(2.0 * la))
        da = lmb * hp[...]
        dmult = lmb * gi * xc
        di = lmb * mult * xc
        dxc = lmb * mult * gi
        dla = da * a - dmult * a * a / mult
        dr = dla * (-LRU_C * sp)
        dlam_o[...] += _colsum(dla * (LRU_C * r)) * _sigmoid(-lam[...])
        dpr = dr * r * (1.0 - r)
        dpi = di * gi * (1.0 - gi)
        dba_o[...] += _colsum(dpr)
        dbx_o[...] += _colsum(dpi)
        dxc_o[...] = dxc + _dot(dpr, wa[...], "nt") + _dot(dpi, wx[...], "nt")
        dwa_o[...] += _dot(xc, dpr, "tn")
        dwx_o[...] += _dot(xc, dpi, "tn")

    outs = pl.pallas_call(
        body,
        name="lru_bwd",
        grid=(nb,),
        in_specs=[rrow, pl.BlockSpec((T, BRANCH), lambda t: (nb - 1 - t, U_AY // BRANCH))] + [rrow] * 7
        + [sq, sq, vrow],
        out_specs=[rrow, rrow, sq, sq, vrow, vrow, vrow],
        out_shape=[jax.ShapeDtypeStruct((S, BRANCH), BF16), jax.ShapeDtypeStruct((S, BRANCH), F32),
                   jax.ShapeDtypeStruct((BRANCH, BRANCH), F32), jax.ShapeDtypeStruct((BRANCH, BRANCH), F32),
                   jax.ShapeDtypeStruct((1, BRANCH), F32), jax.ShapeDtypeStruct((1, BRANCH), F32),
                   jax.ShapeDtypeStruct((1, BRANCH), F32)],
        scratch_shapes=[pltpu.VMEM((1, BRANCH), F32), pltpu.VMEM((T, BRANCH), F32)],
        compiler_params=pltpu.CompilerParams(dimension_semantics=("arbitrary",), vmem_limit_bytes=VMEM_LIMIT_BYTES),
    )(dya, u, sv["h"], h_prev, sv["xc"], sv["r"], sv["i"], sv["a"], a_next, lw["wa"], lw["wx"], lw["lam"])
    return outs


def _conv_bwd(dxc, ax_shift, lw, S):
    dxs = [_shift_up(dxc, k) for k in range(4)]

    def fn(d0, d1, d2, d3, a0, a1, a2, a3, cw0, cw1, cw2, cw3):
        dax = cw3 * d0 + cw2 * d1 + cw1 * d2 + cw0 * d3
        return [dax], [_colsum(d0 * a3), _colsum(d0 * a2), _colsum(d0 * a1), _colsum(d0 * a0), _colsum(d0)]

    (dax,), sums = _rowwise("conv_bwd", fn, dxs + list(ax_shift), [lw["cw0"], lw["cw1"], lw["cw2"], lw["cw3"]],
                            [(BRANCH, BF16)], [BRANCH] * 5, S)
    return dax, sums


GLA_QK = GLA_HEADS * GLA_DK
GLA_V = GLA_HEADS * GLA_DV
GLA_SCALE = GLA_DK ** -0.5


def _gla_specs(TB, rev_nb=None):
    def rmap(t):
        return t if rev_nb is None else rev_nb - 1 - t
    return [
        pl.BlockSpec((TB, GLA_QK), lambda t: (rmap(t), U_BQ // GLA_QK)),
        pl.BlockSpec((TB, GLA_QK), lambda t: (rmap(t), U_BK // GLA_QK)),
        pl.BlockSpec((TB, GLA_V), lambda t: (rmap(t), U_BV // GLA_V)),
        pl.BlockSpec((TB, GLA_V), lambda t: (rmap(t), U_BR // GLA_V)),
        pl.BlockSpec((TB, LANES), lambda t: (rmap(t), U_BLOW // LANES)),
    ]


def _gla_gates(gl, wg2, bg, TB):
    pre = _dot(gl, wg2, "nn") + bg
    la = _log_sigmoid(pre) * (1.0 / GLA_TAU)
    _, gc = _scan_rows(None, la, TB, seg=CHUNK)
    return pre, la, gc


def _gla_fwd(u, gw, S):
    TB = min(512, S)
    nb = S // TB
    cpb = TB // CHUNK
    vecs = [gw["wg2"], gw["bg"], gw["ng"], gw["bd"]]

    def body(q_r, k_r, v_r, br_r, gl_r, wg2, bg, ng, bd, yb_o, oraw_o, st_o, st):
        t = pl.program_id(0)

        @pl.when(t == 0)
        def _():
            st[...] = jnp.zeros_like(st)

        _, la, gc = _gla_gates(gl_r[...], wg2[...], bg[...], TB)
        for c in range(cpb):
            sl = slice(c * CHUNK, (c + 1) * CHUNK)
            gt = _colsum(la[sl])
            kdec = k_r[sl, :] * jnp.exp(gt - gc[sl])
            d_t = _dot(v_r[sl, :], kdec, "tn") * bd[...]
            s_new = st[...] * jnp.exp(gt) + d_t
            st[...] = s_new
            st_o[c] = s_new
            oraw_o[sl, :] = _dot(q_r[sl, :] * GLA_SCALE, s_new, "nt")
        for h in range(GLA_HEADS):
            hs = slice(h * GLA_DV, (h + 1) * GLA_DV)
            oh = oraw_o[:, hs]
            on = oh * lax.rsqrt(jnp.mean(oh * oh, axis=-1, keepdims=True) + RMS_EPS)
            sil, _ = _silu_and_grad(br_r[:, hs])
            yb_o[:, hs] = (on * ng[:, hs] * sil).astype(yb_o.dtype)

    return pl.pallas_call(
        body,
        name="gla_fwd",
        grid=(nb,),
        in_specs=_gla_specs(TB) + [_full_spec(v) for v in vecs],
        out_specs=[pl.BlockSpec((TB, GLA_V), lambda t: (t, 0)), pl.BlockSpec((TB, GLA_V), lambda t: (t, 0)),
                   pl.BlockSpec((cpb, GLA_V, GLA_QK), lambda t: (t, 0, 0))],
        out_shape=[jax.ShapeDtypeStruct((S, GLA_V), BF16), jax.ShapeDtypeStruct((S, GLA_V), F32),
                   jax.ShapeDtypeStruct((S // CHUNK, GLA_V, GLA_QK), F32)],
        scratch_shapes=[pltpu.VMEM((GLA_V, GLA_QK), F32)],
        compiler_params=pltpu.CompilerParams(dimension_semantics=("arbitrary",), vmem_limit_bytes=VMEM_LIMIT_BYTES),
    )(u, u, u, u, u, *vecs)


def _gla_bwd(dyb, u, oraw, states, gw, S):
    TB = min(512, S)
    nb = S // TB
    cpb = TB // CHUNK
    vecs = [gw["wg2"], gw["bg"], gw["ng"], gw["bd"]]

    def rrow(width):
        return pl.BlockSpec((TB, width), lambda t: (nb - 1 - t, 0))

    def body(dyb_r, oraw_r, q_r, k_r, v_r, br_r, gl_r, st_r, sp_r, wg2, bg, ng, bd,
             dq_o, dk_o, dv_o, dbr_o, dgl_o, dwg2_o, dbg_o, dng_o, dcar, do_buf, dla_buf):
        t = pl.program_id(0)
        blk = nb - 1 - t

        @pl.when(t == 0)
        def _():
            dcar[...] = jnp.zeros_like(dcar)
            dwg2_o[...] = jnp.zeros_like(dwg2_o)
            dbg_o[...] = jnp.zeros_like(dbg_o)
            dng_o[...] = jnp.zeros_like(dng_o)

        pre, la, gc = _gla_gates(gl_r[...], wg2[...], bg[...], TB)
        for h in range(GLA_HEADS):
            hs = slice(h * GLA_DV, (h + 1) * GLA_DV)
            oh = oraw_r[:, hs]
            rs = lax.rsqrt(jnp.mean(oh * oh, axis=-1, keepdims=True) + RMS_EPS)
            on = oh * rs
            sil, dsil = _silu_and_grad(br_r[:, hs])
            dy = dyb_r[:, hs]
            dbr_o[:, hs] = (dy * on * ng[:, hs] * dsil).astype(dbr_o.dtype)
            don = dy * ng[:, hs] * sil
            dng_o[:, hs] += _colsum(dy * on * sil)
            do_buf[:, hs] = rs * (don - on * jnp.mean(don * on, axis=-1, keepdims=True))
        first = jnp.where(blk == 0, 0.0, 1.0)
        for c in reversed(range(cpb)):
            sl = slice(c * CHUNK, (c + 1) * CHUNK)
            s_n = st_r[c]
            s_prev = st_r[c - 1] if c > 0 else sp_r[0] * first
            gt = _colsum(la[sl])
            w = jnp.exp(gt - gc[sl])
            k_c = k_r[sl, :]
            kdec = k_c * w
            qs = q_r[sl, :] * GLA_SCALE
            do_c = do_buf[sl, :]
            dq_o[sl, :] = (_dot(do_c, s_n, "nn") * GLA_SCALE).astype(dq_o.dtype)
            d_n = _dot(do_c, qs, "tn") * bd[...] + dcar[...]
            dv_o[sl, :] = _dot(kdec, d_n, "nt").astype(dv_o.dtype)
            dkdec = _dot(v_r[sl, :], d_n, "nn")
            dk_o[sl, :] = (dkdec * w).astype(dk_o.dtype)
            tt = dkdec * kdec
            e = jnp.exp(gt)
            dgt = _colsum(tt) + _colsum(d_n * s_prev) * e
            _, rc = _scan_rows(None, -tt, CHUNK, reverse=True)
            dla_buf[sl, :] = rc + dgt
            dcar[...] = d_n * e
        dpre = dla_buf[...] * _sigmoid(-pre) * (1.0 / GLA_TAU)
        dbg_o[...] += _colsum(dpre)
        dgl_o[...] = _dot(dpre, wg2[...], "nt").astype(dgl_o.dtype)
        dwg2_o[...] += _dot(gl_r[...], dpre, "tn")

    return pl.pallas_call(
        body,
        name="gla_bwd",
        grid=(nb,),
        in_specs=[rrow(GLA_V), rrow(GLA_V)] + _gla_specs(TB, rev_nb=nb)
        + [pl.BlockSpec((cpb, GLA_V, GLA_QK), lambda t: (nb - 1 - t, 0, 0)),
           pl.BlockSpec((1, GLA_V, GLA_QK), lambda t: (jnp.maximum((nb - 1 - t) * cpb - 1, 0), 0, 0))]
        + [_full_spec(v) for v in vecs],
        out_specs=[rrow(GLA_QK), rrow(GLA_QK), rrow(GLA_V), rrow(GLA_V), rrow(LANES),
                   pl.BlockSpec((LANES, GLA_QK), lambda t: (0, 0)), pl.BlockSpec((1, GLA_QK), lambda t: (0, 0)),
                   pl.BlockSpec((1, GLA_V), lambda t: (0, 0))],
        out_shape=[jax.ShapeDtypeStruct((S, GLA_QK), BF16), jax.ShapeDtypeStruct((S, GLA_QK), BF16),
                   jax.ShapeDtypeStruct((S, GLA_V), BF16), jax.ShapeDtypeStruct((S, GLA_V), BF16),
                   jax.ShapeDtypeStruct((S, LANES), BF16), jax.ShapeDtypeStruct((LANES, GLA_QK), F32),
                   jax.ShapeDtypeStruct((1, GLA_QK), F32), jax.ShapeDtypeStruct((1, GLA_V), F32)],
        scratch_shapes=[pltpu.VMEM((GLA_V, GLA_QK), F32), pltpu.VMEM((TB, GLA_V), F32),
                        pltpu.VMEM((TB, GLA_QK), F32)],
        compiler_params=pltpu.CompilerParams(dimension_semantics=("arbitrary",), vmem_limit_bytes=VMEM_LIMIT_BYTES),
    )(dyb, oraw, u, u, u, u, u, states, states, *vecs)


FOX_SCALE = FOX_DH ** -0.5


def _fox_gate_fwd(u, bfp, S):
    T = min(512, S)

    def body(f_r, b_r, fc_o, car):
        t = pl.program_id(0)

        @pl.when(t == 0)
        def _():
            car[...] = jnp.zeros_like(car)

        _, cs = _scan_rows(None, _log_sigmoid(f_r[...] + b_r[...]), T)
        fc_o[...] = cs + car[...]
        car[...] = fc_o[pl.ds(T - 1, 1), :]

    return pl.pallas_call(
        body,
        name="fox_gate_fwd",
        grid=(S // T,),
        in_specs=[pl.BlockSpec((T, LANES), lambda t: (t, U_CF // LANES)), _full_spec(bfp)],
        out_specs=pl.BlockSpec((T, LANES), lambda t: (t, 0)),
        out_shape=jax.ShapeDtypeStruct((S, LANES), F32),
        scratch_shapes=[pltpu.VMEM((1, LANES), F32)],
        compiler_params=pltpu.CompilerParams(dimension_semantics=("arbitrary",), vmem_limit_bytes=VMEM_LIMIT_BYTES),
    )(u, bfp)


def _fox_gate_bwd(dfc, u, bfp, S):
    T = min(512, S)
    nb = S // T

    def body(d_r, f_r, b_r, df_o, db_o, car, tmp):
        t = pl.program_id(0)

        @pl.when(t == 0)
        def _():
            car[...] = jnp.zeros_like(car)
            db_o[...] = jnp.zeros_like(db_o)

        _, rc = _scan_rows(None, d_r[...], T, reverse=True)
        tmp[...] = rc + car[...]
        car[...] = tmp[pl.ds(0, 1), :]
        df = tmp[...] * _sigmoid(-(f_r[...] + b_r[...]))
        df_o[...] = df.astype(df_o.dtype)
        db_o[...] += _colsum(df)

    return pl.pallas_call(
        body,
        name="fox_gate_bwd",
        grid=(nb,),
        in_specs=[pl.BlockSpec((T, LANES), lambda t: (nb - 1 - t, 0)),
                  pl.BlockSpec((T, LANES), lambda t: (nb - 1 - t, U_CF // LANES)), _full_spec(bfp)],
        out_specs=[pl.BlockSpec((T, LANES), lambda t: (nb - 1 - t, 0)), pl.BlockSpec((1, LANES), lambda t: (0, 0))],
        out_shape=[jax.ShapeDtypeStruct((S, LANES), BF16), jax.ShapeDtypeStruct((1, LANES), F32)],
        scratch_shapes=[pltpu.VMEM((1, LANES), F32), pltpu.VMEM((T, LANES), F32)],
        compiler_params=pltpu.CompilerParams(dimension_semantics=("arbitrary",), vmem_limit_bytes=VMEM_LIMIT_BYTES),
    )(dfc, u, bfp)


def _fox_scores(q, k, fq, fk, qi, ki, tq, tk):
    s = _dot(q, k, "nt") * FOX_SCALE + (fq - fk)
    rows = lax.broadcasted_iota(jnp.int32, (tq, tk), 0) + qi * tq
    cols = lax.broadcasted_iota(jnp.int32, (tq, tk), 1) + ki * tk
    return jnp.where(cols <= rows, s, NEG_BIG)


def _fox_fwd(qh, kh, vh, fq, fk, S):
    tq = tk = min(512, S)
    nq, nk = S // tq, S // tk

    def body(q_r, k_r, v_r, fq_r, fk_r, o_o, lse_o, m_s, l_s, acc):
        qi, ki = pl.program_id(1), pl.program_id(2)

        @pl.when(ki == 0)
        def _():
            m_s[...] = jnp.full_like(m_s, NEG_BIG)
            l_s[...] = jnp.zeros_like(l_s)
            acc[...] = jnp.zeros_like(acc)

        @pl.when(ki <= qi)
        def _():
            s = _fox_scores(q_r[0], k_r[0], fq_r[0], fk_r[0], qi, ki, tq, tk)
            m_new = jnp.maximum(m_s[...], jnp.max(s, axis=-1, keepdims=True))
            p = jnp.exp(s - m_new)
            alpha = jnp.exp(m_s[...] - m_new)
            l_s[...] = alpha * l_s[...] + jnp.sum(p, axis=-1, keepdims=True)
            acc[...] = alpha * acc[...] + _dot(p, v_r[0], "nn")
            m_s[...] = m_new

        @pl.when(ki == nk - 1)
        def _():
            o_o[0] = acc[...] / l_s[...]
            lse_o[0] = m_s[...] + jnp.log(l_s[...])

    kv = pl.BlockSpec((1, tk, FOX_DH), lambda h, i, j: (h, jnp.minimum(j, i), 0))
    return pl.pallas_call(
        body,
        name="fox_fwd",
        grid=(FOX_HEADS, nq, nk),
        in_specs=[pl.BlockSpec((1, tq, FOX_DH), lambda h, i, j: (h, i, 0)), kv, kv,
                  pl.BlockSpec((1, tq, 1), lambda h, i, j: (h, i, 0)),
                  pl.BlockSpec((1, 1, tk), lambda h, i, j: (h, 0, jnp.minimum(j, i)))],
        out_specs=[pl.BlockSpec((1, tq, FOX_DH), lambda h, i, j: (h, i, 0)),
                   pl.BlockSpec((1, tq, 1), lambda h, i, j: (h, i, 0))],
        out_shape=[jax.ShapeDtypeStruct((FOX_HEADS, S, FOX_DH), F32), jax.ShapeDtypeStruct((FOX_HEADS, S, 1), F32)],
        scratch_shapes=[pltpu.VMEM((tq, 1), F32), pltpu.VMEM((tq, 1), F32), pltpu.VMEM((tq, FOX_DH), F32)],
        compiler_params=pltpu.CompilerParams(
            dimension_semantics=("parallel", "parallel", "arbitrary"), vmem_limit_bytes=VMEM_LIMIT_BYTES),
    )(qh, kh, vh, fq, fk)


def _fox_bwd_dq(qh, kh, vh, fq, fk, o, do, lse, S):
    tq = tk = min(512, S)
    nq, nk = S // tq, S // tk

    def body(q_r, k_r, v_r, fq_r, fk_r, o_r, do_r, lse_r, dq_o, dfq_o, dq_acc, df_acc):
        qi, ki = pl.program_id(1), pl.program_id(2)

        @pl.when(ki == 0)
        def _():
            dq_acc[...] = jnp.zeros_like(dq_acc)
            df_acc[...] = jnp.zeros_like(df_acc)

        @pl.when(ki <= qi)
        def _():
            s = _fox_scores(q_r[0], k_r[0], fq_r[0], fk_r[0], qi, ki, tq, tk)
            p = jnp.exp(s - lse_r[0])
            do_t = do_r[0]
            delta = jnp.sum(o_r[0] * do_t, axis=-1, keepdims=True)
            ds = p * (_dot(do_t, v_r[0], "nt") - delta)
            dq_acc[...] += _dot(ds, k_r[0], "nn")
            df_acc[...] += jnp.sum(ds, axis=-1, keepdims=True)

        @pl.when(ki == nk - 1)
        def _():
            dq_o[0] = dq_acc[...] * FOX_SCALE
            dfq_o[0] = df_acc[...]

    qrow = pl.BlockSpec((1, tq, FOX_DH), lambda h, i, j: (h, i, 0))
    qcol = pl.BlockSpec((1, tq, 1), lambda h, i, j: (h, i, 0))
    kv = pl.BlockSpec((1, tk, FOX_DH), lambda h, i, j: (h, jnp.minimum(j, i), 0))
    return pl.pallas_call(
        body,
        name="fox_bwd_dq",
        grid=(FOX_HEADS, nq, nk),
        in_specs=[qrow, kv, kv, qcol, pl.BlockSpec((1, 1, tk), lambda h, i, j: (h, 0, jnp.minimum(j, i))),
                  qrow, qrow, qcol],
        out_specs=[qrow, qcol],
        out_shape=[jax.ShapeDtypeStruct((FOX_HEADS, S, FOX_DH), F32), jax.ShapeDtypeStruct((FOX_HEADS, S, 1), F32)],
        scratch_shapes=[pltpu.VMEM((tq, FOX_DH), F32), pltpu.VMEM((tq, 1), F32)],
        compiler_params=pltpu.CompilerParams(
            dimension_semantics=("parallel", "parallel", "arbitrary"), vmem_limit_bytes=VMEM_LIMIT_BYTES),
    )(qh, kh, vh, fq, fk, o, do, lse)


def _fox_bwd_dkv(qh, kh, vh, fq, fk, o, do, lse, S):
    tq = tk = min(512, S)
    nq, nk = S // tq, S // tk

    def body(q_r, k_r, v_r, fq_r, fk_r, o_r, do_r, lse_r, dk_o, dv_o, dfk_o, dk_acc, dv_acc, df_acc):
        ki, qi = pl.program_id(1), pl.program_id(2)

        @pl.when(qi == 0)
        def _():
            dk_acc[...] = jnp.zeros_like(dk_acc)
            dv_acc[...] = jnp.zeros_like(dv_acc)
            df_acc[...] = jnp.zeros_like(df_acc)

        @pl.when(qi >= ki)
        def _():
            s = _fox_scores(q_r[0], k_r[0], fq_r[0], fk_r[0], qi, ki, tq, tk)
            p = jnp.exp(s - lse_r[0])
            do_t = do_r[0]
            delta = jnp.sum(o_r[0] * do_t, axis=-1, keepdims=True)
            ds = p * (_dot(do_t, v_r[0], "nt") - delta)
            dv_acc[...] += _dot(p, do_t, "tn")
            dk_acc[...] += _dot(ds, q_r[0], "tn")
            df_acc[...] += _colsum(ds)

        @pl.when(qi == nq - 1)
        def _():
            dk_o[0] = dk_acc[...] * FOX_SCALE
            dv_o[0] = dv_acc[...]
            dfk_o[0] = df_acc[...]

    qrow = pl.BlockSpec((1, tq, FOX_DH), lambda h, j, i: (h, jnp.maximum(i, j), 0))
    qcol = pl.BlockSpec((1, tq, 1), lambda h, j, i: (h, jnp.maximum(i, j), 0))
    kv = pl.BlockSpec((1, tk, FOX_DH), lambda h, j, i: (h, j, 0))
    krow = pl.BlockSpec((1, 1, tk), lambda h, j, i: (h, 0, j))
    return pl.pallas_call(
        body,
        name="fox_bwd_dkv",
        grid=(FOX_HEADS, nk, nq),
        in_specs=[qrow, kv, kv, qcol, krow, qrow, qrow, qcol],
        out_specs=[kv, kv, krow],
        out_shape=[jax.ShapeDtypeStruct((FOX_HEADS, S, FOX_DH), F32), jax.ShapeDtypeStruct((FOX_HEADS, S, FOX_DH), F32),
                   jax.ShapeDtypeStruct((FOX_HEADS, 1, S), F32)],
        scratch_shapes=[pltpu.VMEM((tk, FOX_DH), F32), pltpu.VMEM((tk, FOX_DH), F32), pltpu.VMEM((1, tk), F32)],
        compiler_params=pltpu.CompilerParams(
            dimension_semantics=("parallel", "parallel", "arbitrary"), vmem_limit_bytes=VMEM_LIMIT_BYTES),
    )(qh, kh, vh, fq, fk, o, do, lse)


def _fox_call(name, body, tables, grid, in_specs, out_specs, out_shape, scratch, args, side):
    n_in, n_out, n_scr = len(in_specs), len(out_specs), len(scratch)
    semantics = ("parallel", "arbitrary")
    if side is not None:
        total = grid[0] * grid[1]
        phases = side["phases"]
        triggers = [0, total - 1] if len(phases) == 2 else [0, total * 7 // 10, total - 1]
        na, no = len(side["arrs"]), len(side["out_shapes"])
        s_in, s_out, s_sems = _side_specs(side)
        kernel_body = body

        def body(*refs):
            tabs, rest = refs[:len(tables)], refs[len(tables):]
            ins, s_ins = rest[:n_in], rest[n_in:n_in + na]
            rest = rest[n_in + na:]
            outs, s_outs = rest[:n_out], rest[n_out:n_out + no]
            rest = rest[n_out + no:]
            scr, sems = rest[:n_scr], rest[n_scr:]
            flat = pl.program_id(0) * grid[1] + pl.program_id(1)
            for trigger, phase in zip(triggers[:-1], phases[:-1]):
                @pl.when(flat == trigger)
                def _(phase=phase):
                    phase(s_ins, s_outs, *sems)
            kernel_body(*tabs, *ins, *outs, *scr)

            @pl.when(flat == triggers[-1])
            def _():
                phases[-1](s_ins, s_outs, *sems)

        in_specs, out_specs = in_specs + s_in, out_specs + s_out
        out_shape, scratch = out_shape + side["out_shapes"], scratch + s_sems
        args = list(args) + side["arrs"]
        semantics = ("arbitrary", "arbitrary")
    res = pl.pallas_call(
        body,
        name=name,
        grid_spec=pltpu.PrefetchScalarGridSpec(num_scalar_prefetch=len(tables), grid=grid, in_specs=in_specs,
                                               out_specs=out_specs, scratch_shapes=scratch),
        out_shape=out_shape,
        compiler_params=pltpu.CompilerParams(dimension_semantics=semantics, vmem_limit_bytes=VMEM_LIMIT_BYTES),
    )(*tables, *args)
    return res[:n_out], res[n_out:]


FOX_TILE = 1024
FOX_AUG = 128
FOX_ONES = 3


def _fox_pairs(n, by_key):
    pairs = [(qi, ki) for qi in range(n) for ki in range(qi + 1)]
    if by_key:
        pairs.sort(key=lambda qk: (qk[1], qk[0]))
    qs = jnp.asarray([qk[0] for qk in pairs], jnp.int32)
    ks = jnp.asarray([qk[1] for qk in pairs], jnp.int32)
    return qs, ks


def _fox_augment(q, k, fcum):
    S = q.shape[0]
    def to_bf16_grid(a):
        return lax.reduce_precision(a, exponent_bits=8, mantissa_bits=7)

    hi = to_bf16_grid(fcum)
    mid = to_bf16_grid(fcum - hi)
    lo = to_bf16_grid(fcum - hi - mid)
    f3 = jnp.stack([hi, mid, lo], axis=-1).astype(BF16)
    ones = jnp.ones((S, FOX_HEADS, FOX_ONES), BF16)
    pad = jnp.zeros((S, FOX_HEADS, FOX_AUG - FOX_DH - 2 * FOX_ONES), BF16)
    q_aug = jnp.concatenate([(q * FOX_SCALE).astype(BF16), ones, f3, pad], axis=-1)
    k_aug = jnp.concatenate([k.astype(BF16), -f3, ones, pad], axis=-1)
    return jnp.transpose(q_aug, (1, 0, 2)), jnp.transpose(k_aug, (1, 0, 2))


def _fox_causal(sT):
    keys = lax.broadcasted_iota(jnp.int32, sT.shape, 0)
    queries = lax.broadcasted_iota(jnp.int32, sT.shape, 1)
    return jnp.where(keys <= queries, sT, NEG_BIG)


def _fox_fwd(qT, ka, vT, S, side=None):
    t = min(FOX_TILE, S)
    n = S // t
    qi_tab, ki_tab = _fox_pairs(n, by_key=False)

    def body(qi_ref, ki_ref, qT_r, ka_r, vT_r, oT_o, lse_o, m_s, l_s, acc):
        step = pl.program_id(1)
        qi, ki = qi_ref[step], ki_ref[step]

        @pl.when(ki == 0)
        def _():
            m_s[...] = jnp.full_like(m_s, NEG_BIG)
            l_s[...] = jnp.zeros_like(l_s)
            acc[...] = jnp.zeros_like(acc)

        def update(masked):
            sT = _dot(ka_r[0], qT_r[0], "nn")
            if masked:
                sT = _fox_causal(sT)
            m_new = jnp.maximum(m_s[...], jnp.max(sT, axis=0, keepdims=True))
            p = jnp.exp(sT - m_new)
            alpha = jnp.exp(m_s[...] - m_new)
            l_s[...] = alpha * l_s[...] + jnp.sum(p, axis=0, keepdims=True)
            acc[...] = alpha * acc[...] + _dot(vT_r[0], p, "nn")
            m_s[...] = m_new

        @pl.when(ki < qi)
        def _():
            update(False)

        @pl.when(ki == qi)
        def _():
            update(True)
            oT_o[0] = acc[...] / l_s[...]
            lse_o[0] = m_s[...] + jnp.log(l_s[...])

    return _fox_call(
        "fox_fwd", body, (qi_tab, ki_tab), (FOX_HEADS, int(qi_tab.shape[0])),
        [pl.BlockSpec((1, FOX_AUG, t), lambda h, s, qt, kt: (h, 0, qt[s])),
         pl.BlockSpec((1, t, FOX_AUG), lambda h, s, qt, kt: (h, kt[s], 0)),
         pl.BlockSpec((1, FOX_DH, t), lambda h, s, qt, kt: (h, 0, kt[s]))],
        [pl.BlockSpec((1, FOX_DH, t), lambda h, s, qt, kt: (h, 0, qt[s])),
         pl.BlockSpec((1, 1, t), lambda h, s, qt, kt: (h, 0, qt[s]))],
        [jax.ShapeDtypeStruct((FOX_HEADS, FOX_DH, S), F32), jax.ShapeDtypeStruct((FOX_HEADS, 1, S), F32)],
        [pltpu.VMEM((1, t), F32), pltpu.VMEM((1, t), F32), pltpu.VMEM((FOX_DH, t), F32)],
        (qT, ka, vT), side)


FOX_BIAS_ROWS = 8


def _fox_bwd(qT, qa, ka, kT, v, do, doT, oT, lse, S, side=None):
    t = min(FOX_TILE, S)
    n = S // t
    qi_tab, ki_tab = _fox_pairs(n, by_key=True)
    n_steps = int(qi_tab.shape[0])
    slab = slice(FOX_DH, FOX_DH + FOX_BIAS_ROWS)

    def body(qi_ref, ki_ref, qT_r, qa_r, ka_r, kT_r, v_r, do_r, doT_r, oT_r, lse_r,
             dq_o, dfq_o, dk_o, dfk_o, dv_o, dq_acc, dk_acc, dv_acc):
        step = pl.program_id(1)
        qi, ki = qi_ref[step], ki_ref[step]

        @pl.when(step == 0)
        def _():
            dq_acc[...] = jnp.zeros_like(dq_acc)

        @pl.when(qi == ki)
        def _():
            dk_acc[...] = jnp.zeros_like(dk_acc)
            dv_acc[...] = jnp.zeros_like(dv_acc)

        def update(masked):
            sT = _dot(ka_r[0], qT_r[0], "nn")
            if masked:
                sT = _fox_causal(sT)
            pT = jnp.exp(sT - lse_r[0])
            delta = jnp.sum(oT_r[0] * doT_r[0], axis=0, keepdims=True)
            dsT = pT * (_dot(v_r[0], doT_r[0], "nn") - delta)
            dv_acc[...] += _dot(pT, do_r[0], "nn")
            dk_acc[...] += _dot(dsT, qa_r[0], "nn")
            dq_acc[qi] += _dot(kT_r[0], dsT, "nn")

        @pl.when(qi > ki)
        def _():
            update(False)

        @pl.when(qi == ki)
        def _():
            update(True)

        @pl.when(qi == n - 1)
        def _():
            dk = dk_acc[...]
            dk_o[0] = dk[:, :FOX_DH]
            dfk_o[0] = dk.T[slab]
            dv_o[0] = dv_acc[...]

        @pl.when(step == n_steps - 1)
        def _():
            for j in range(n):
                dqT = dq_acc[j]
                dq_o[0, j * t:(j + 1) * t, :] = dqT.T[:, :FOX_DH]
                dfq_o[0, :, j * t:(j + 1) * t] = dqT[slab]

    def qlane(rows):
        return pl.BlockSpec((1, rows, t), lambda h, s, qt, kt: (h, 0, qt[s]))

    def qrow(cols):
        return pl.BlockSpec((1, t, cols), lambda h, s, qt, kt: (h, qt[s], 0))

    def krow(cols):
        return pl.BlockSpec((1, t, cols), lambda h, s, qt, kt: (h, kt[s], 0))

    def klane(rows):
        return pl.BlockSpec((1, rows, t), lambda h, s, qt, kt: (h, 0, kt[s]))

    def head(rows, cols):
        return pl.BlockSpec((1, rows, cols), lambda h, s, qt, kt: (h, 0, 0))

    return _fox_call(
        "fox_bwd", body, (qi_tab, ki_tab), (FOX_HEADS, n_steps),
        [qlane(FOX_AUG), qrow(FOX_AUG), krow(FOX_AUG), klane(FOX_AUG), krow(FOX_DH), qrow(FOX_DH), qlane(FOX_DH),
         qlane(FOX_DH), qlane(1)],
        [head(S, FOX_DH), head(FOX_BIAS_ROWS, S), krow(FOX_DH), klane(FOX_BIAS_ROWS), krow(FOX_DH)],
        [jax.ShapeDtypeStruct((FOX_HEADS, S, FOX_DH), F32), jax.ShapeDtypeStruct((FOX_HEADS, FOX_BIAS_ROWS, S), F32),
         jax.ShapeDtypeStruct((FOX_HEADS, S, FOX_DH), F32), jax.ShapeDtypeStruct((FOX_HEADS, FOX_BIAS_ROWS, S), F32),
         jax.ShapeDtypeStruct((FOX_HEADS, S, FOX_DH), F32)],
        [pltpu.VMEM((n, FOX_AUG, t), F32), pltpu.VMEM((t, FOX_AUG), F32), pltpu.VMEM((t, FOX_DH), F32)],
        (qT, qa, ka, kT, v, do, doT, oT, lse), side)


def _fox_prep(u, fcum, S):
    T = min(512, S)
    head_of = jnp.arange(BRANCH) // FOX_DH
    dim_of = jnp.arange(BRANCH) % FOX_DH
    heads = jnp.arange(FOX_HEADS)[:, None, None]
    sel = (head_of[None, :, None] == heads) & (dim_of[None, :, None] == jnp.arange(FOX_AUG)[None, None, :])
    sel_q = (sel * FOX_SCALE).astype(BF16)
    sel_k = sel.astype(BF16)
    sel_vT = jnp.swapaxes(sel[:, :, :FOX_DH], 1, 2).astype(BF16)
    piece = jnp.arange(FOX_ONES * LANES) // LANES
    lane = jnp.arange(FOX_ONES * LANES) % LANES
    col = jnp.arange(FOX_AUG)[None, None, :]
    at_q = (lane[None, :, None] == heads) & (col == FOX_DH + FOX_ONES + piece[None, :, None])
    at_k = (lane[None, :, None] == heads) & (col == FOX_DH + piece[None, :, None])
    bias_q = at_q.astype(BF16)
    bias_k = (-at_k.astype(F32)).astype(BF16)
    cols = jnp.arange(FOX_AUG)[None, :]
    ones_q = ((cols >= FOX_DH) & (cols < FOX_DH + FOX_ONES)).astype(F32)
    ones_k = ((cols >= FOX_DH + FOX_ONES) & (cols < FOX_DH + 2 * FOX_ONES)).astype(F32)
    consts = [sel_q, sel_k, sel_vT, bias_q, bias_k, ones_q, ones_k]

    def body(cq, ck, cv, fc, sq, sk, svT, bq, bk, oq, ok, qa_o, ka_o, qT_o, kT_o, vh_o, vT_o):
        f = fc[...]
        hi = f.astype(BF16).astype(F32)
        mid = (f - hi).astype(BF16).astype(F32)
        lo = (f - hi - mid).astype(BF16).astype(F32)
        pieces = jnp.concatenate([hi, mid, lo], axis=1)
        q, k, v = cq[...], ck[...], cv[...]
        for h in range(FOX_HEADS):
            qa = _dot(q, sq[h], "nn") + _dot(pieces, bq[h], "nn") + oq[...]
            ka = _dot(k, sk[h], "nn") + _dot(pieces, bk[h], "nn") + ok[...]
            qa_o[h] = qa.astype(qa_o.dtype)
            ka_o[h] = ka.astype(ka_o.dtype)
            qT_o[h] = qa.T.astype(qT_o.dtype)
            kT_o[h] = ka.T.astype(kT_o.dtype)
            vT_o[h] = _dot(svT[h], v, "nt").astype(vT_o.dtype)
            vh_o[h] = _dot(v, svT[h], "nt").astype(vh_o.dtype)

    def win(off):
        return pl.BlockSpec((T, BRANCH), functools.partial(lambda i, blk: (i, blk), blk=off // BRANCH))

    def rows(c):
        return pl.BlockSpec((FOX_HEADS, T, c), lambda i: (0, i, 0))

    def lanes(r):
        return pl.BlockSpec((FOX_HEADS, r, T), lambda i: (0, 0, i))

    bf = lambda *shape: jax.ShapeDtypeStruct((FOX_HEADS,) + shape, BF16)
    return pl.pallas_call(
        body,
        name="fox_prep",
        grid=(S // T,),
        in_specs=[win(U_CQ), win(U_CK), win(U_CV), pl.BlockSpec((T, LANES), lambda i: (i, 0))]
        + [_full_spec(c) for c in consts],
        out_specs=[rows(FOX_AUG), rows(FOX_AUG), lanes(FOX_AUG), lanes(FOX_AUG), rows(FOX_DH), lanes(FOX_DH)],
        out_shape=[bf(S, FOX_AUG), bf(S, FOX_AUG), bf(FOX_AUG, S), bf(FOX_AUG, S), bf(S, FOX_DH), bf(FOX_DH, S)],
        compiler_params=pltpu.CompilerParams(dimension_semantics=("parallel",), vmem_limit_bytes=VMEM_LIMIT_BYTES),
    )(u, u, u, fcum, *consts)


def _to_heads(x2d, S):
    return jnp.transpose(x2d.reshape(S, FOX_HEADS, FOX_DH), (1, 0, 2))


def _from_heads(xh, S):
    return jnp.transpose(xh, (1, 0, 2)).reshape(S, FOX_HEADS * FOX_DH)


def _ffn_fwd(tag, x, wgT, wuT, wd, g, b, S):
    def up_epi(accs):
        gate, up = accs
        sil, _ = _silu_and_grad(gate)
        return [gate, up, sil * up]

    gate, up, act = _mm(tag + "_up", "nt", [x], [wgT, wuT], [(0, 0, 0), (1, 0, 1)], 2, up_epi, [],
                        [BF16, BF16, BF16], S, D_FF, D_MODEL, tn=1408)

    def down_epi(accs, xr, gg, bb):
        z = ALPHA * xr + 0.5 * accs[0]
        return [z, _ln_fwd(z, gg, bb)]

    z, xn = _mm(tag + "_down", "nn", [act], [wd], [(0, 0, 0)], 1, down_epi, [(x, "mn", 0), (g, "n"), (b, "n")],
                [F32, F32], S, D_MODEL, D_FF, tk=D_FF)
    return xn, dict(x=x, gate=gate, up=up, act=act, z=z)


def _ln_bwd_call(tag, dy, z, g, S):
    def fn(dy_t, z_t, g_t):
        dz, xhat = _ln_bwd(dy_t, z_t, g_t)
        return [dz], [_colsum(dy_t * xhat), _colsum(dy_t)]

    (dz,), (dg, db) = _rowwise(tag + "_ln_bwd", fn, [dy, z], [g], [(D_MODEL, F32)], [D_MODEL, D_MODEL], S)
    return dz, dg, db


def _ffn_bwd(tag, dxn, sv, wgT, wuT, wd, g, S, gdt=F32):
    dz, dg, db = _ln_bwd_call(tag, dxn, sv["z"], g, S)

    def act_epi(accs, gate, up):
        da = 0.5 * accs[0]
        sil, dsil = _silu_and_grad(gate.astype(F32))
        return [da * up.astype(F32) * dsil, da * sil]

    dgate, dup = _mm(tag + "_dact", "nt", [dz], [wd], [(0, 0, 0)], 1, act_epi,
                     [(sv["gate"], "mn", 0), (sv["up"], "mn", 0)], [BF16, BF16], S, D_FF, D_MODEL, tn=1408)
    dwd = _mm1(tag + "_dwd", "tn", sv["act"], dz, D_FF, D_MODEL, S, scale=0.5, tm=1408, out_dtype=gdt)

    def two(accs):
        return [accs[0], accs[1]]

    dwgT, dwuT = _mm(tag + "_dwup", "tn", [dgate, dup], [sv["x"]], [(0, 0, 0), (1, 1, 0)], 2, two, [], [gdt, gdt],
                     D_FF, D_MODEL, S, tm=1408, tk=512)

    def dx_epi(accs, dzr):
        return [accs[0] + ALPHA * dzr]

    (dx,) = _mm(tag + "_dx", "nn", [dgate, dup], [wgT, wuT], [(0, 0, 0), (0, 1, 1)], 1, dx_epi, [(dz, "mn", 0)],
                [F32], S, D_MODEL, D_FF, tk=1408)
    return dx, dict(w_upT=jnp.concatenate([dwgT, dwuT], axis=0), w_down=dwd, ln_g=dg, ln_b=db)


def _mixer_fwd(x1, w, S, side=None):
    u = _mm1("w_in", "nt", x1, w["w_inT_p"], S, U_WIDTH, D_MODEL, tn=1536)
    ax0 = u[:, U_AX:U_AX + BRANCH]
    ax_shift = [ax0] + [_shift_down(ax0, k) for k in (1, 2, 3)]
    xc, r, gi, a, h, y_a = _lru_fwd(u, ax_shift, w["lru"], S)
    y_b, oraw, states = _gla_fwd(u, w["gla"], S)
    fcum = _fox_gate_fwd(u, w["bfp"], S)
    qa, ka, qT, kT, vh, vT = _fox_prep(u, fcum, S)
    (oT, lse), side_out = _fox_fwd(qT, ka, vT, S, side)
    y_c = jnp.transpose(oT, (2, 0, 1)).reshape(S, BRANCH).astype(BF16)

    def merge_epi(accs, g0, g1, g2):
        merged = _sigmoid(g0) * accs[0] + _sigmoid(g1) * accs[1] + _sigmoid(g2) * accs[2]
        return [accs[0], accs[1], accs[2], merged]

    wb = w["w_branchT"]
    yp0, yp1, yp2, merged = _mm(
        "merge", "nt", [y_a, y_b, y_c], [wb[0], wb[1], wb[2]], [(0, 0, 0), (1, 1, 1), (2, 2, 2)], 3, merge_epi,
        [(u, "mn", 0), (u, "mn", 1), (u, "mn", 2)], [F32, F32, F32, BF16], S, D_MODEL, BRANCH, tm=256)

    def out_epi(accs, xr, gg, bb):
        z = ALPHA * xr + accs[0]
        return [z, _ln_fwd(z, gg, bb)]

    z2, x2 = _mm("w_out", "nn", [merged], [w["w_out"]], [(0, 0, 0)], 1, out_epi,
                 [(x1, "mn", 0), (w["ln2_g"], "n"), (w["ln2_b"], "n")], [F32, F32], S, D_MODEL, D_MODEL)
    sv = dict(x=x1, u=u, ax_shift=ax_shift, xc=xc, r=r, i=gi, a=a, h=h, y_a=y_a, y_b=y_b, y_c=y_c, oraw=oraw,
              states=states, qT=qT, qa=qa, ka=ka, kT=kT, vh=vh, oT=oT, lse=lse, yp=(yp0, yp1, yp2), merged=merged,
              z=z2)
    return x2, sv, side_out


def _mixer_bwd(dx2, sv, w, S, side=None, gdt=F32):
    u = sv["u"]
    dz, dg2, db2 = _ln_bwd_call("mix", dx2, sv["z"], w["ln2_g"], S)

    def dm_epi(accs, y0, y1, y2, g0, g1, g2):
        dm = accs[0]
        outs_p, outs_g = [], []
        for yp, gl in ((y0, g0), (y1, g1), (y2, g2)):
            sg = _sigmoid(gl)
            outs_p.append(dm * sg)
            outs_g.append(dm * yp * sg * (1.0 - sg))
        return outs_p + outs_g

    yp = sv["yp"]
    dyp0, dyp1, dyp2, dgl0, dgl1, dgl2 = _mm(
        "dmerged", "nt", [dz], [w["w_out"]], [(0, 0, 0)], 1, dm_epi,
        [(yp[0], "mn", 0), (yp[1], "mn", 0), (yp[2], "mn", 0), (u, "mn", 0), (u, "mn", 1), (u, "mn", 2)],
        [BF16] * 6, S, D_MODEL, D_MODEL, tm=256)
    dw_out = _mm1("dw_out", "tn", sv["merged"], dz, D_MODEL, D_MODEL, S, out_dtype=gdt)
    wb = w["w_branchT"]
    dys, dwbs = [], []
    for j, (yj, dyp) in enumerate(((sv["y_a"], dyp0), (sv["y_b"], dyp1), (sv["y_c"], dyp2))):
        dys.append(_mm1("dy_branch%d" % j, "nn", dyp, wb[j], S, BRANCH, D_MODEL))
        dwbs.append(_mm1("dw_branch%d" % j, "tn", dyp, yj, D_MODEL, BRANCH, S, out_dtype=gdt))
    day, dxc, dwa, dwx, dba, dbx, dlam = _lru_bwd(dys[0], u, sv, w["lru"], S)
    dax, (dcw0, dcw1, dcw2, dcw3, dcb) = _conv_bwd(dxc, sv["ax_shift"], w["lru"], S)
    dbq, dbk, dbv, dbr, dglow, dwg2p, dbg, dng = _gla_bwd(dys[1], u, sv["oraw"], sv["states"], w["gla"], S)
    doh = _to_heads(dys[2], S)
    (dqh, dfq, dkh, dfk, dvh), side_out = _fox_bwd(sv["qT"], sv["qa"], sv["ka"], sv["kT"], sv["vh"], doh,
                                                   jnp.swapaxes(doh, 1, 2), sv["oT"], sv["lse"], S, side)
    dqh = dqh * FOX_SCALE
    dfc = jnp.transpose(dfq[:, FOX_ONES, :] - dfk[:, 0, :])
    dfc = jnp.pad(dfc, ((0, 0), (0, LANES - FOX_HEADS)))
    dcf, dbf = _fox_gate_bwd(dfc, u, w["bfp"], S)
    du = jnp.concatenate(
        [dgl0, dgl1, dgl2, dax, day, dbq, dbk, dbv, dbr, _from_heads(dqh, S).astype(BF16),
         _from_heads(dkh, S).astype(BF16), _from_heads(dvh, S).astype(BF16), dglow, dcf,
         jnp.zeros((S, U_WIDTH - U_CF - LANES), BF16)], axis=1)
    dw_inT_p = _mm1("dw_in", "tn", du, sv["x"], U_WIDTH, D_MODEL, S, tm=1536, out_dtype=gdt)

    def dx_epi(accs, dzr):
        return [accs[0] + ALPHA * dzr]

    (dx1,) = _mm("dx_mix", "nn", [du], [w["w_inT_p"]], [(0, 0, 0)], 1, dx_epi, [(dz, "mn", 0)], [F32], S, D_MODEL,
                 U_WIDTH, tk=1536)
    pieces = sorted(W_IN_SEGMENTS)
    dw_inT = jnp.concatenate([dw_inT_p[dst:dst + width] for _, width, dst in pieces], axis=0)
    eye = jnp.eye(LRU_BLOCKS, dtype=F32)
    dwa_b = jnp.einsum("ncmd,nm->ncd", dwa.reshape(LRU_BLOCKS, 64, LRU_BLOCKS, 64), eye)
    dwx_b = jnp.einsum("ncmd,nm->ncd", dwx.reshape(LRU_BLOCKS, 64, LRU_BLOCKS, 64), eye)
    grads = dict(
        w_inT=dw_inT, w_out=dw_out, w_branchT=jnp.stack(dwbs), ln2_g=dg2, ln2_b=db2,
        conv_w=jnp.concatenate([dcw0, dcw1, dcw2, dcw3], axis=0).astype(gdt), conv_b=dcb, lru_wa=dwa_b, lru_wx=dwx_b,
        lru_ba=dba, lru_bx=dbx, lru_lambda=dlam, gla_w_g2=dwg2p[:GLA_LOWRANK].astype(gdt), gla_b_g=dbg, gla_norm_g=dng,
        fox_b_f=dbf[:, :FOX_HEADS])
    return dx1, grads, side_out


def _ple_fwd(x3, p_i, w, S):
    pe = _mm1("ple_proj", "nt", p_i, w["ple_w_projT"], S, D_MODEL, PLE_DIM)

    def epi(accs, xr, per, bg, gg, bb):
        sg = _sigmoid(accs[0] + bg)
        z = ALPHA * xr + sg * per
        return [sg, z, _ln_fwd(z, gg, bb)]

    sg, z4, x4 = _mm("ple_gate", "nn", [x3], [w["ple_w_gate"]], [(0, 0, 0)], 1, epi,
                     [(x3, "mn", 0), (pe, "mn", 0), (w["ple_b_gate"], "n"), (w["ln4_g"], "n"), (w["ln4_b"], "n")],
                     [F32, F32, F32], S, D_MODEL, D_MODEL)
    return x4, dict(x=x3, p=p_i, pe=pe, sg=sg, z=z4)


def _ple_bwd(dx4, sv, w, S, gdt=F32):
    def fn(dy_t, z_t, pe_t, sg_t, g_t):
        dz, xhat = _ln_bwd(dy_t, z_t, g_t)
        dgl = dz * pe_t * sg_t * (1.0 - sg_t)
        return [dz, dz * sg_t, dgl], [_colsum(dy_t * xhat), _colsum(dy_t), _colsum(dgl)]

    (dz, dpe, dgl), (dg4, db4, dbg) = _rowwise(
        "ple_bwd", fn, [dx4, sv["z"], sv["pe"], sv["sg"]], [w["ln4_g"]],
        [(D_MODEL, F32), (D_MODEL, BF16), (D_MODEL, BF16)], [D_MODEL] * 3, S)
    dwpT = _mm1("dw_ple_proj", "tn", dpe, sv["p"], D_MODEL, PLE_DIM, S, out_dtype=gdt)
    dwg = _mm1("dw_ple_gate", "tn", sv["x"], dgl, D_MODEL, D_MODEL, S, out_dtype=gdt)

    def dx_epi(accs, dzr):
        return [accs[0] + ALPHA * dzr]

    (dx3,) = _mm("dx_ple", "nt", [dgl], [w["ple_w_gate"]], [(0, 0, 0)], 1, dx_epi, [(dz, "mn", 0)], [F32], S,
                 D_MODEL, D_MODEL)
    return dx3, dict(ple_w_projT=dwpT, ple_w_gate=dwg, ple_b_gate=dbg, ln4_g=dg4, ln4_b=db4)


def _rows_of_all(g):
    return g.reshape((g.shape[0] * g.shape[1],) + g.shape[2:])


def _layer_weights(gathered, full, i):
    w = {}
    for tag in ("ffn1", "ffn2"):
        upT = _rows_of_all(gathered[tag + "_w_up"])
        w[tag] = (upT[:D_FF], upT[D_FF:], _rows_of_all(gathered[tag + "_w_down"]))
    w_inT = _rows_of_all(gathered["w_in"])
    placed = sorted((dst, src, width) for src, width, dst in W_IN_SEGMENTS)
    parts, pos = [], 0
    for dst, src, width in placed:
        if dst > pos:
            parts.append(jnp.zeros((dst - pos, D_MODEL), w_inT.dtype))
        parts.append(w_inT[src:src + width])
        pos = dst + width
    parts.append(jnp.zeros((U_WIDTH - pos, D_MODEL), w_inT.dtype))
    w["w_inT_p"] = jnp.concatenate(parts, axis=0)
    eye = jnp.eye(LRU_BLOCKS, dtype=F32)

    def dense(blocks):
        return jnp.einsum("ncd,nm->ncmd", blocks, eye).reshape(BRANCH, BRANCH).astype(BF16)

    def vec(name):
        return full[name][i].reshape(1, -1)

    cw = jnp.moveaxis(gathered["conv_w"], 0, 1).reshape(4, BRANCH)
    w_g2 = jnp.moveaxis(gathered["gla_w_g2"], 0, 1).reshape(GLA_LOWRANK, GLA_QK)
    w["lru"] = dict(cw0=cw[0:1], cw1=cw[1:2], cw2=cw[2:3], cw3=cw[3:4], conv_b=vec("conv_b"),
                    wa=dense(full["lru_wa"][i]), wx=dense(full["lru_wx"][i]), ba=vec("lru_ba"), bx=vec("lru_bx"),
                    lam=vec("lru_lambda"))
    hq = jnp.arange(GLA_QK) // GLA_DK
    hv = jnp.arange(GLA_V) // GLA_DV
    w["gla"] = dict(wg2=jnp.pad(w_g2, ((0, LANES - GLA_LOWRANK), (0, 0))).astype(BF16),
                    bg=vec("gla_b_g"), ng=vec("gla_norm_g"), bd=(hv[:, None] == hq[None, :]).astype(F32))
    w["bfp"] = jnp.pad(vec("fox_b_f"), ((0, 0), (0, LANES - FOX_HEADS)))
    w["w_branchT"] = jnp.moveaxis(gathered["w_branch"], 0, 1).reshape(3, D_MODEL, BRANCH)
    w["w_out"] = _rows_of_all(gathered["w_out"])
    w["ple_w_projT"] = _rows_of_all(gathered["ple_w_proj"])
    w["ple_w_gate"] = _rows_of_all(gathered["ple_w_gate"])
    for name in ("ln1_g", "ln1_b", "ln2_g", "ln2_b", "ln3_g", "ln3_b", "ln4_g", "ln4_b", "ple_b_gate"):
        w[name] = vec(name)
    return w


def _layer_fwd(x0, p_i, w, S, side=None):
    x1, s1 = _ffn_fwd("ffn1", x0, *w["ffn1"], w["ln1_g"], w["ln1_b"], S)
    x2, s2, side_out = _mixer_fwd(x1, w, S, side)
    x3, s3 = _ffn_fwd("ffn2", x2, *w["ffn2"], w["ln3_g"], w["ln3_b"], S)
    x4, s4 = _ple_fwd(x3, p_i, w, S)
    return x4, (s1, s2, s3, s4), side_out


def _layer_bwd(dx4, saved, w, S, side=None, gdt=F32):
    s1, s2, s3, s4 = saved
    dx3, g4 = _ple_bwd(dx4, s4, w, S, gdt)
    dx2, g3 = _ffn_bwd("ffn2", dx3, s3, *w["ffn2"], w["ln3_g"], S, gdt)
    dx1, g2, side_out = _mixer_bwd(dx2, s2, w, S, side, gdt)
    dx0, g1 = _ffn_bwd("ffn1", dx1, s1, *w["ffn1"], w["ln1_g"], S, gdt)
    grads = dict(g2)
    grads.update(g4)
    grads.update(ffn1_w_upT=g1["w_upT"], ffn1_w_down=g1["w_down"], ln1_g=g1["ln_g"], ln1_b=g1["ln_b"],
                 ffn2_w_upT=g3["w_upT"], ffn2_w_down=g3["w_down"], ln3_g=g3["ln_g"], ln3_b=g3["ln_b"])
    return dx0, grads, side_out


def _travel_grads(grads):
    return [_dest_pieces(n, grads[n + "T" if n in COLUMN_SHARDED else n]) for n, _ in SHARDED]


def _local_step(x, p, target, gathered0, layer1, full, overlap):
    S = x.shape[0]
    names = [n for n, _ in SHARDED]
    w0 = _layer_weights(gathered0, full, 0)
    h, saved0, got = _layer_fwd(x, p[0], w0, S, _gather_job(layer1) if overlap else None)
    w1 = _layer_weights(dict(zip(names, got)) if overlap else layer1, full, 1)
    h, saved1, _ = _layer_fwd(h, p[1], w1, S)

    def loss_fn(y, t):
        err = y - t
        return [err * (1.0 / D_MODEL)], [_colsum(err * err) * (0.5 / D_MODEL)]

    (dy,), (lsum,) = _rowwise("loss", loss_fn, [h, target], [], [(D_MODEL, F32)], [D_MODEL], S)
    loss = jnp.sum(lsum)
    dy, g1, _ = _layer_bwd(dy, saved1, w1, S, gdt=BF16 if overlap else F32)
    dy, g0, pieces1 = _layer_bwd(dy, saved0, w0, S, _scatter_job(_travel_grads(g1)) if overlap else None)
    return loss, dy, [g0, g1], pieces1 if overlap else None


def kernel(x, p, ffn1_w_up, ffn1_w_down, ln1_g, ln1_b, w_in, conv_w, conv_b, lru_wa, lru_ba, lru_wx, lru_bx, lru_lambda, gla_w_g2, gla_b_g, gla_norm_g, fox_b_f, w_branch, w_out, ln2_g, ln2_b, ffn2_w_up, ffn2_w_down, ln3_g, ln3_b, ple_w_proj, ple_w_gate, ple_b_gate, ln4_g, ln4_b, loss_target, m_ffn1_w_up, m_ffn1_w_down, m_ln1_g, m_ln1_b, m_w_in, m_conv_w, m_conv_b, m_lru_wa, m_lru_ba, m_lru_wx, m_lru_bx, m_lru_lambda, m_gla_w_g2, m_gla_b_g, m_gla_norm_g, m_fox_b_f, m_w_branch, m_w_out, m_ln2_g, m_ln2_b, m_ffn2_w_up, m_ffn2_w_down, m_ln3_g, m_ln3_b, m_ple_w_proj, m_ple_w_gate, m_ple_b_gate, m_ln4_g, m_ln4_b, v_ffn1_w_up, v_ffn1_w_down, v_ln1_g, v_ln1_b, v_w_in, v_conv_w, v_conv_b, v_lru_wa, v_lru_ba, v_lru_wx, v_lru_bx, v_lru_lambda, v_gla_w_g2, v_gla_b_g, v_gla_norm_g, v_fox_b_f, v_w_branch, v_w_out, v_ln2_g, v_ln2_b, v_ffn2_w_up, v_ffn2_w_down, v_ln3_g, v_ln3_b, v_ple_w_proj, v_ple_w_gate, v_ple_b_gate, v_ln4_g, v_ln4_b):
    env = dict(locals())
    wts = {n: env[n] for n in WEIGHTS}
    ms = {n: env["m_" + n] for n in WEIGHTS}
    vs = {n: env["v_" + n] for n in WEIGHTS}
    sharded = [n for n, _ in SHARDED]

    def travel(n, a):
        return jnp.swapaxes(a, -1, -2) if n in COLUMN_SHARDED else a

    shards = [[travel(n, wts[n][i]) if n in SHARDED_F32_GATHER else travel(n, wts[n][i]).astype(BF16) for n in sharded]
              for i in range(DEPTH)]
    gathered0 = dict(zip(sharded, _allgather_multi("gather_weights", shards[0])))
    full = {n: wts[n] for n in REPLICATED}

    loss_part, grad_x, layer_grads, pieces1 = _local_step(x[0], p[:, 0], loss_target[0], gathered0, shards[1], full,
                                                           True)
    loss = lax.psum(loss_part, MESH_AXES)

    dest = _travel_grads(layer_grads[0])
    got = _sibling_swap_multi("grad_sibling_swap", dest)
    core = lax.axis_index("c").astype(jnp.int32).reshape(1)
    pairs = [_pair_add("grad_pair_add_" + n, core, _as_rows(d, 2), _as_rows(g, 1))
             for n, d, g in zip(sharded, dest, got)]
    pieces0 = _chip_all_to_all_multi("grad_chip_all_to_all", pairs)
    rep = list(REPLICATED)
    rep_grads = [jnp.stack([layer_grads[i][n] for i in range(DEPTH)]).reshape(wts[n].shape) for n in rep]
    (gr,) = _allgather_multi("grad_gather_replicated", [_pack(rep_grads, F32)])

    kinds = ("grad", "delta", "new_m", "new_v")
    out = {}
    for k, n in enumerate(sharded):
        local = [_as_rows(travel(n, pieces.reshape((-1,) + shards[i][k].shape)), 1)
                 for i, pieces in ((0, pieces0[k]), (1, pieces1[k]))]
        res = _adamw("adamw_" + n, local, _as_rows(wts[n], 1), _as_rows(ms[n], 1), _as_rows(vs[n], 1))
        for kind, arr in zip(kinds, res):
            out[kind + "_" + n] = arr.reshape(wts[n].shape)
    res = _adamw("adamw_replicated", [gr], _pack([wts[n] for n in rep], F32)[None],
                 _pack([ms[n] for n in rep], F32)[None], _pack([vs[n] for n in rep], F32)[None])
    shapes = [wts[n].shape for n in rep]
    for kind, buf in zip(kinds, res):
        for n, arr in zip(rep, _unpack(buf[0], shapes)):
            out[kind + "_" + n] = arr
    return (loss, grad_x[None], *[out["grad_" + n] for n in WEIGHTS], *[out["delta_" + n] for n in WEIGHTS],
            *[out["new_m_" + n] for n in WEIGHTS], *[out["new_v_" + n] for n in WEIGHTS])
```

```python
import functools
import math

import jax
import jax.numpy as jnp
from jax import lax
from jax.experimental import pallas as pl
from jax.experimental.pallas import tpu as pltpu

F32 = jnp.float32
BF16 = jnp.bfloat16

N_DEV = 8
MESH_AXES = ("x", "y", "c")
DEPTH = 2
D_MODEL = 1024
D_FF = 2816
BRANCH = 512
CHUNK = 64
GLA_HEADS = 4
GLA_DK = 64
GLA_DV = 128
GLA_LOWRANK = 16
GLA_TAU = 16.0
FOX_HEADS = 8
FOX_DH = 64
PLE_DIM = 256
LRU_C = 8.0
LRU_BLOCKS = 8
LN_EPS = 1e-5
RMS_EPS = 1e-6
ALPHA = (2 * DEPTH) ** 0.25
LANES = 128
NEG_BIG = -1e30

ADAM_LR = 0.001
ADAM_B1 = 0.9
ADAM_B2 = 0.999
ADAM_EPS = 1e-08
ADAM_WD = 0.01
ADAM_STEP = 10

VMEM_LIMIT_BYTES = 56 * 1024 * 1024

U_GATES = 0
U_AX = 3072
U_AY = 3584
U_BQ = 4096
U_BK = 4352
U_BV = 4608
U_BR = 5120
U_CQ = 5632
U_CK = 6144
U_CV = 6656
U_BLOW = 7168
U_CF = 7296
U_WIDTH = 7680
W_IN_SEGMENTS = (
    (0, 512, U_AX), (512, 512, U_AY), (1024, 256, U_BQ), (1280, 256, U_BK), (1536, 512, U_BV),
    (2048, 16, U_BLOW), (2064, 512, U_BR), (2576, 512, U_CQ), (3088, 512, U_CK), (3600, 512, U_CV),
    (4112, 8, U_CF), (4120, 3072, U_GATES),
)

SHARDED = (
    ("ffn1_w_up", 2), ("ffn1_w_down", 1), ("w_in", 2), ("conv_w", 2), ("gla_w_g2", 2), ("w_branch", 3),
    ("w_out", 1), ("ffn2_w_up", 2), ("ffn2_w_down", 1), ("ple_w_proj", 2), ("ple_w_gate", 1),
)
SHARDED_F32_GATHER = ("conv_w", "gla_w_g2")
COLUMN_SHARDED = ("ffn1_w_up", "ffn2_w_up", "w_in", "w_branch", "ple_w_proj")
REPLICATED = ("ln1_g", "ln1_b", "conv_b", "lru_wa", "lru_ba", "lru_wx", "lru_bx", "lru_lambda", "gla_b_g",
              "gla_norm_g", "fox_b_f", "ln2_g", "ln2_b", "ln3_g", "ln3_b", "ple_b_gate", "ln4_g", "ln4_b")
WEIGHTS = ("ffn1_w_up", "ffn1_w_down", "ln1_g", "ln1_b", "w_in", "conv_w", "conv_b", "lru_wa", "lru_ba", "lru_wx",
           "lru_bx", "lru_lambda", "gla_w_g2", "gla_b_g", "gla_norm_g", "fox_b_f", "w_branch", "w_out", "ln2_g",
           "ln2_b", "ffn2_w_up", "ffn2_w_down", "ln3_g", "ln3_b", "ple_w_proj", "ple_w_gate", "ple_b_gate", "ln4_g",
           "ln4_b")


def _sigmoid(x):
    return 1.0 / (1.0 + jnp.exp(-x))


def _log1p_pos(e):
    return jnp.where(e < 1e-4, e * (1.0 - 0.5 * e), jnp.log(1.0 + e))


def _softplus(x):
    return jnp.maximum(x, 0.0) + _log1p_pos(jnp.exp(-jnp.abs(x)))


def _log_sigmoid(x):
    return -_softplus(-x)


def _neg_expm1(y):
    series = -y * (1.0 + y * (0.5 + y * (1.0 / 6.0 + y * (1.0 / 24.0 + y * (1.0 / 120.0)))))
    return jnp.where(y > -0.1, series, 1.0 - jnp.exp(y))


def _silu_and_grad(x):
    s = _sigmoid(x)
    return x * s, s * (1.0 + x * (1.0 - s))


_GELU_C = math.sqrt(2.0 / math.pi)


def _gelu_and_grad(x):
    inner = _GELU_C * (x + 0.044715 * x * x * x)
    t = jnp.tanh(inner)
    g = 0.5 * x * (1.0 + t)
    dg = 0.5 * (1.0 + t) + 0.5 * x * (1.0 - t * t) * _GELU_C * (1.0 + 3.0 * 0.044715 * x * x)
    return g, dg


def _ln_stats(z):
    mu = jnp.mean(z, axis=-1, keepdims=True)
    zc = z - mu
    var = jnp.mean(zc * zc, axis=-1, keepdims=True)
    rstd = lax.rsqrt(var + LN_EPS)
    return zc * rstd, rstd


def _ln_fwd(z, g, b):
    xhat, _ = _ln_stats(z)
    return xhat * g + b


def _ln_bwd(dy, z, g):
    xhat, rstd = _ln_stats(z)
    dxh = dy * g
    m1 = jnp.mean(dxh, axis=-1, keepdims=True)
    m2 = jnp.mean(dxh * xhat, axis=-1, keepdims=True)
    return rstd * (dxh - m1 - xhat * m2), xhat


def _colsum(x):
    return jnp.sum(x, axis=0, keepdims=True)


def _dot(a, b, dims):
    dn = {"nn": (((1,), (0,)), ((), ())), "nt": (((1,), (1,)), ((), ())), "tn": (((0,), (0,)), ((), ()))}[dims]
    return lax.dot_general(a.astype(BF16), b.astype(BF16), dn, preferred_element_type=F32)


def _scan_rows(a, b, length, reverse=False, seg=None):
    rows = lax.broadcasted_iota(jnp.int32, b.shape, 0)
    span = seg if seg else length
    pos = rows % span if seg else rows
    d = 1
    while d < span:
        shift = (length - d) if reverse else d
        valid = (pos < span - d) if reverse else (pos >= d)
        sb = jnp.where(valid, pltpu.roll(b, shift, 0), 0.0)
        if a is None:
            b = b + sb
        else:
            b = b + a * sb
            a = a * jnp.where(valid, pltpu.roll(a, shift, 0), 1.0)
        d *= 2
    return a, b


def _tile(dim, pref):
    if dim <= pref:
        return dim
    best = None
    t = LANES
    while t <= pref:
        if dim % t == 0:
            best = t
        t += LANES
    assert best is not None, (dim, pref)
    return best


def _full_spec(arr):
    nd = arr.ndim
    return pl.BlockSpec(arr.shape, lambda *_: (0,) * nd)


def _mm(name, dims, a_ops, b_ops, terms, n_acc, epilogue, extras, out_dtypes, M, N, K, tm=512, tn=1024, tk=1024):
    tm, tn, tk = _tile(M, tm), _tile(N, tn), _tile(K, tk)
    gm, gn, gk = M // tm, N // tn, K // tk
    a_bytes = sum(a.size * a.dtype.itemsize for a in a_ops)
    b_bytes = sum(b.size * b.dtype.itemsize for b in b_ops)
    n_outer = gk == 1 and b_bytes + a_bytes * gn < a_bytes + b_bytes * gm

    def spec(shape, fn):
        if n_outer:
            return pl.BlockSpec(shape, lambda j, i, k: fn(i, j, k))
        return pl.BlockSpec(shape, fn)

    if dims == "tn":
        a_spec = spec((tk, tm), lambda i, j, k: (k, i))
    else:
        a_spec = spec((tm, tk), lambda i, j, k: (i, k))
    if dims == "nt":
        b_spec = spec((tn, tk), lambda i, j, k: (j, k))
    else:
        b_spec = spec((tk, tn), lambda i, j, k: (k, j))
    e_specs, e_arrays = [], []
    for ex in extras:
        if ex[1] == "mn":
            e_specs.append(spec((tm, tn), functools.partial(lambda i, j, k, off: (i, j + off), off=ex[2])))
        else:
            e_specs.append(spec((1, tn), lambda i, j, k: (0, j)))
        e_arrays.append(ex[0])
    na, nb, ne, no = len(a_ops), len(b_ops), len(extras), len(out_dtypes)

    def body(*refs):
        a_refs = refs[:na]
        b_refs = refs[na:na + nb]
        e_refs = refs[na + nb:na + nb + ne]
        o_refs = refs[na + nb + ne:na + nb + ne + no]
        acc_refs = refs[na + nb + ne + no:]
        k = pl.program_id(2)

        @pl.when(k == 0)
        def _():
            for acc in acc_refs:
                acc[...] = jnp.zeros_like(acc)

        for r, ai, bi in terms:
            acc_refs[r][...] += _dot(a_refs[ai][...], b_refs[bi][...], dims)

        @pl.when(k == gk - 1)
        def _():
            res = epilogue([acc[...] for acc in acc_refs], *[e[...] for e in e_refs])
            for o, val in zip(o_refs, res):
                o[...] = val.astype(o.dtype)

    outs = pl.pallas_call(
        body,
        name=name,
        grid=(gn, gm, gk) if n_outer else (gm, gn, gk),
        in_specs=[a_spec] * na + [b_spec] * nb + e_specs,
        out_specs=[spec((tm, tn), lambda i, j, k: (i, j))] * no,
        out_shape=[jax.ShapeDtypeStruct((M, N), dt) for dt in out_dtypes],
        scratch_shapes=[pltpu.VMEM((tm, tn), F32)] * n_acc,
        compiler_params=pltpu.CompilerParams(
            dimension_semantics=("parallel", "parallel", "arbitrary"), vmem_limit_bytes=VMEM_LIMIT_BYTES),
    )(*a_ops, *b_ops, *e_arrays)
    return outs


def _mm1(name, dims, a, b, M, N, K, out_dtype=F32, scale=None, **kw):
    def epi(accs):
        return [accs[0] if scale is None else accs[0] * scale]
    return _mm(name, dims, [a], [b], [(0, 0, 0)], 1, epi, [], [out_dtype], M, N, K, **kw)[0]


def _rowwise(name, fn, row_ins, vec_ins, row_outs, sum_outs, S, tr=256, reverse=False):
    tr = min(tr, S)
    g = S // tr
    rmap = (lambda i: (g - 1 - i)) if reverse else (lambda i: i)
    in_specs, arrays = [], []
    for r in row_ins:
        if isinstance(r, tuple):
            arr, width, blk = r
            in_specs.append(pl.BlockSpec((tr, width), functools.partial(lambda i, blk: (rmap(i), blk), blk=blk)))
        else:
            arr = r
            in_specs.append(pl.BlockSpec((tr, arr.shape[1]), lambda i: (rmap(i), 0)))
        arrays.append(arr)
    for v in vec_ins:
        in_specs.append(_full_spec(v))
        arrays.append(v)
    nr, nv, no, ns = len(row_ins), len(vec_ins), len(row_outs), len(sum_outs)

    def body(*refs):
        ins = [r[...] for r in refs[:nr + nv]]
        o_refs = refs[nr + nv:nr + nv + no]
        s_refs = refs[nr + nv + no:]
        outs, sums = fn(*ins)
        for o, val in zip(o_refs, outs):
            o[...] = val.astype(o.dtype)
        if ns:
            i = pl.program_id(0)

            @pl.when(i == 0)
            def _():
                for s, val in zip(s_refs, sums):
                    s[...] = val

            @pl.when(i > 0)
            def _():
                for s, val in zip(s_refs, sums):
                    s[...] += val

    res = pl.pallas_call(
        body,
        name=name,
        grid=(g,),
        in_specs=in_specs,
        out_specs=[pl.BlockSpec((tr, c), lambda i: (rmap(i), 0)) for c, _ in row_outs]
        + [pl.BlockSpec((1, c), lambda i: (0, 0)) for c in sum_outs],
        out_shape=[jax.ShapeDtypeStruct((S, c), dt) for c, dt in row_outs]
        + [jax.ShapeDtypeStruct((1, c), F32) for c in sum_outs],
        compiler_params=pltpu.CompilerParams(
            dimension_semantics=("arbitrary",), vmem_limit_bytes=VMEM_LIMIT_BYTES),
    )(*arrays)
    return res[:no], res[no:]


def _win(arr, offset, width):
    assert offset % width == 0
    return (arr, width, offset // width)


MESH_ID = pl.DeviceIdType.MESH


def _remote(src, dst, send_sem, recv_sem, to):
    return pltpu.make_async_remote_copy(src_ref=src, dst_ref=dst, send_sem=send_sem, recv_sem=recv_sem,
                                        device_id=to, device_id_type=MESH_ID)


def _hbm_call(name, body, arrs, out_shapes, n_send, n_recv, n_local):
    return pl.pallas_call(
        body,
        name=name,
        in_specs=[pl.BlockSpec(memory_space=pltpu.HBM)] * len(arrs),
        out_specs=[pl.BlockSpec(memory_space=pltpu.HBM)] * len(out_shapes),
        out_shape=out_shapes,
        scratch_shapes=[pltpu.SemaphoreType.DMA((n_send,)), pltpu.SemaphoreType.DMA((n_recv,)),
                        pltpu.SemaphoreType.DMA((n_local,))],
        compiler_params=pltpu.CompilerParams(has_side_effects=True),
    )(*arrs)


def _side_job(arrs, out_shapes, n_send, n_recv, n_local, phases):
    return dict(arrs=list(arrs), out_shapes=list(out_shapes), sems=(n_send, n_recv, n_local), phases=phases)


def _side_specs(side):
    hbm = pl.BlockSpec(memory_space=pltpu.HBM)
    sems = [pltpu.SemaphoreType.DMA((k,)) for k in side["sems"]]
    return [hbm] * len(side["arrs"]), [hbm] * len(side["out_shapes"]), sems


def _gather_job(arrs):
    n = len(arrs)

    def plan(ins, outs, send_sems, recv_sems, local_sems):
        x, y, c = lax.axis_index("x"), lax.axis_index("y"), lax.axis_index("c")
        me, sibling = (x, y, c), (x, y, 1 - c)
        chips = [(1 - x, y), (x, 1 - y), (1 - x, 1 - y)]

        def slot(i, dev):
            return outs[i].at[4 * dev[0] + 2 * dev[1] + dev[2]]

        def copy(i, k, block, to, src=None):
            dst = slot(i, block)
            return _remote(dst if src is None else src, dst, send_sems.at[7 * i + k], recv_sems.at[7 * i + k], to)

        mine = [pltpu.make_async_copy(ins[i], slot(i, me), local_sems.at[i]) for i in range(n)]
        first = []
        for i in range(n):
            first.append(copy(i, 0, me, sibling, src=ins[i]))
            first += [copy(i, 1 + j, me, (*chip, c), src=ins[i]) for j, chip in enumerate(chips)]
        arrive = [[copy(i, 1 + j, (*chip, c), me) for i in range(n)] for j, chip in enumerate(chips)]
        passed = [[copy(i, 4 + j, (*chip, c), sibling) for i in range(n)] for j, chip in enumerate(chips)]
        last = [copy(i, 0, sibling, me) for i in range(n)]
        last += [copy(i, 4 + j, (*chip, 1 - c), me) for i in range(n) for j, chip in enumerate(chips)]
        return mine, first, arrive, passed, last

    def start(*refs):
        mine, first, _, _, _ = plan(*refs)
        for cp in mine + first:
            cp.start()

    def forward(*refs):
        _, _, arrive, passed, _ = plan(*refs)
        for came, onward in zip(arrive, passed):
            for a, p in zip(came, onward):
                a.wait_recv()
                p.start()

    def finish(*refs):
        mine, first, _, passed, last = plan(*refs)
        for cp in last:
            cp.wait_recv()
        for cp in first + [p for onward in passed for p in onward]:
            cp.wait_send()
        for cp in mine:
            cp.wait()

    outs = [jax.ShapeDtypeStruct((N_DEV,) + a.shape, a.dtype) for a in arrs]
    return _side_job(arrs, outs, 7 * n, 7 * n, n, [start, forward, finish])


def _scatter_job(arrs):
    n = len(arrs)

    def plan(ins, outs, send_sems, recv_sems, local_sems):
        x, y, c = lax.axis_index("x"), lax.axis_index("y"), lax.axis_index("c")
        here = 2 * x + y
        local = [pltpu.make_async_copy(ins[i].at[here, c], outs[i].at[here, c], local_sems.at[i]) for i in range(n)]
        sends, recvs = [], []
        for i in range(n):
            for k in range(1, N_DEV):
                px = 1 - x if k & 4 else x
                py = 1 - y if k & 2 else y
                pc = 1 - c if k & 1 else c
                sems = (send_sems.at[7 * i + k - 1], recv_sems.at[7 * i + k - 1], (px, py, pc))
                sends.append(_remote(ins[i].at[2 * px + py, pc], outs[i].at[here, c], *sems))
                recvs.append(_remote(ins[i].at[2 * px + py, pc], outs[i].at[2 * px + py, pc], *sems))
        return local, sends, recvs

    def start(*refs):
        local, sends, _ = plan(*refs)
        for cp in local + sends:
            cp.start()

    def finish(*refs):
        local, sends, recvs = plan(*refs)
        for cp in recvs:
            cp.wait_recv()
        for cp in sends:
            cp.wait_send()
        for cp in local:
            cp.wait()

    outs = [jax.ShapeDtypeStruct(a.shape, a.dtype) for a in arrs]
    return _side_job(arrs, outs, 7 * n, 7 * n, n, [start, finish])


def _run_job(name, job):
    na, no = len(job["arrs"]), len(job["out_shapes"])

    def body(*refs):
        ins, outs, sems = refs[:na], refs[na:na + no], refs[na + no:]
        for phase in job["phases"]:
            phase(ins, outs, *sems)

    return _hbm_call(name, body, job["arrs"], job["out_shapes"], *job["sems"])


def _allgather_multi(name, arrs):
    return _run_job(name, _gather_job(arrs))


def _sibling_swap_multi(name, arrs):
    n = len(arrs)
    per = 4

    def body(*refs):
        ins, got = refs[:n], refs[n:2 * n]
        send_sems, recv_sems, _ = refs[2 * n:]
        x, y, c = lax.axis_index("x"), lax.axis_index("y"), lax.axis_index("c")
        sibling = (x, y, 1 - c)
        sends = []
        for i in range(n):
            for a in range(4):
                k = per * i + a
                sends.append(_remote(ins[i].at[a, 1 - c], got[i].at[a], send_sems.at[k], recv_sems.at[k], sibling))
        for cp in sends:
            cp.start()
        for cp in sends:
            cp.wait_recv()
        for cp in sends:
            cp.wait_send()

    outs = [jax.ShapeDtypeStruct((4,) + a.shape[2:], a.dtype) for a in arrs]
    return _hbm_call(name, body, arrs, outs, per * n, per * n, 1)


def _chip_all_to_all_multi(name, arrs):
    n = len(arrs)

    def body(*refs):
        ins, outs = refs[:n], refs[n:2 * n]
        send_sems, recv_sems, local_sems = refs[2 * n:]
        x, y, c = lax.axis_index("x"), lax.axis_index("y"), lax.axis_index("c")
        mine = 2 * x + y
        chips = [(1 - x, y), (x, 1 - y), (1 - x, 1 - y)]
        local = [pltpu.make_async_copy(ins[i].at[mine], outs[i].at[mine], local_sems.at[i]) for i in range(n)]
        for cp in local:
            cp.start()
        sends, recvs = [], []
        for i in range(n):
            for j, (px, py) in enumerate(chips):
                peer = 2 * px + py
                sems = (send_sems.at[3 * i + j], recv_sems.at[3 * i + j], (px, py, c))
                sends.append(_remote(ins[i].at[peer], outs[i].at[mine], *sems))
                recvs.append(_remote(ins[i].at[peer], outs[i].at[peer], *sems))
        for cp in sends:
            cp.start()
        for cp in recvs:
            cp.wait_recv()
        for cp in sends:
            cp.wait_send()
        for cp in local:
            cp.wait()

    outs = [jax.ShapeDtypeStruct(a.shape, a.dtype) for a in arrs]
    return _hbm_call(name, body, arrs, outs, 3 * n, 3 * n, n)


def _as_rows(a, lead):
    return a.reshape(a.shape[:lead] + (-1, a.shape[-1]))


def _row_tile(rows, cols, parts):
    budget = 4 * 1024 * 1024 // (4 * max(cols, LANES) * parts)
    return _tile_rows(rows, max(8, min(512, budget // 8 * 8)))


def _pair_add(name, core, both, got):
    _, rows, cols = got.shape
    tr = _row_tile(rows, cols, 2)

    def body(c_ref, a_ref, b_ref, o_ref):
        o_ref[...] = (a_ref[...] + b_ref[...]).astype(o_ref.dtype)

    blk = pl.BlockSpec((1, tr, cols), lambda ch, i, c_ref: (ch, i, 0))
    return pl.pallas_call(
        body, name=name,
        grid_spec=pltpu.PrefetchScalarGridSpec(
            num_scalar_prefetch=1, grid=(4, rows // tr),
            in_specs=[pl.BlockSpec((1, None, tr, cols), lambda ch, i, c_ref: (ch, c_ref[0], i, 0)), blk],
            out_specs=blk),
        out_shape=jax.ShapeDtypeStruct(got.shape, BF16),
        compiler_params=pltpu.CompilerParams(dimension_semantics=("parallel", "parallel"),
                                             vmem_limit_bytes=VMEM_LIMIT_BYTES),
    )(core, both, got)


def _adamw(name, gparts, w, m, v):
    layers = len(gparts)
    _, rows, cols = gparts[0].shape
    tr = _row_tile(rows, cols, sum(gp.shape[0] for gp in gparts))
    c1 = 1.0 / (1.0 - ADAM_B1 ** ADAM_STEP)
    c2 = 1.0 / (1.0 - ADAM_B2 ** ADAM_STEP)

    def body(*refs):
        gp_refs = refs[:layers]
        w_ref, m_ref, v_ref, g_ref, d_ref, nm_ref, nv_ref = refs[layers:]
        layer = pl.program_id(0)
        g = None
        for k, gp_ref in enumerate(gp_refs):
            gk = gp_ref[0].astype(F32)
            for i in range(1, gp_ref.shape[0]):
                gk = gk + gp_ref[i].astype(F32)
            g = gk if g is None else jnp.where(layer == k, gk, g)
        nm = ADAM_B1 * m_ref[...] + (1.0 - ADAM_B1) * g
        nv = ADAM_B2 * v_ref[...] + (1.0 - ADAM_B2) * (g * g)
        m_hat = nm * c1
        v_hat = nv * c2
        g_ref[...] = g
        nm_ref[...] = nm
        nv_ref[...] = nv
        d_ref[...] = -ADAM_LR * (m_hat / (jnp.sqrt(v_hat) + ADAM_EPS) + ADAM_WD * w_ref[...])

    row = pl.BlockSpec((None, tr, cols), lambda l, i: (l, i, 0))
    return pl.pallas_call(
        body,
        name=name,
        grid=(layers, rows // tr),
        in_specs=[pl.BlockSpec((gp.shape[0], tr, cols), lambda l, i: (0, i, 0)) for gp in gparts] + [row, row, row],
        out_specs=[row] * 4,
        out_shape=[jax.ShapeDtypeStruct((layers, rows, cols), F32)] * 4,
        compiler_params=pltpu.CompilerParams(dimension_semantics=("parallel", "parallel"),
                                             vmem_limit_bytes=VMEM_LIMIT_BYTES),
    )(*gparts, w, m, v)


def _tile_rows(rows, pref):
    t = min(pref, rows) // 8 * 8
    while t >= 8 and rows % t:
        t -= 8
    return t if t >= 8 else rows


PACK_ROWS = 512


def _pack(arrs, dtype):
    flat = jnp.concatenate([a.astype(dtype).reshape(-1) for a in arrs])
    quantum = PACK_ROWS * LANES
    padded = -(-flat.shape[0] // quantum) * quantum
    return jnp.pad(flat, (0, padded - flat.shape[0])).reshape(-1, LANES)


def _unpack(buf, shapes, lead=()):
    flat = buf.reshape(lead + (-1,))
    out, off = [], 0
    for shp in shapes:
        n = math.prod(shp)
        out.append(flat[..., off:off + n].reshape(lead + tuple(shp)))
        off += n
    return out


def _dest_pieces(name, g):
    if name == "w_branch":
        return jnp.moveaxis(g.reshape(3, 4, 2, D_MODEL // N_DEV, BRANCH), 0, 2)
    if name in SHARDED_F32_GATHER:
        return jnp.moveaxis(g.reshape(g.shape[0], 4, 2, -1), 0, 2)
    return g.reshape((4, 2, g.shape[0] // N_DEV) + g.shape[1:])


HALO = 8


def _rows_down(x, prev, k):
    xs = pltpu.roll(x, k, 0)
    row = lax.broadcasted_iota(jnp.int32, prev.shape, 0)
    top = jnp.where(row < k, pltpu.roll(prev, k, 0), xs[:HALO])
    return jnp.concatenate([top, xs[HALO:]], axis=0)


def _rows_up(x, nxt, k):
    rows = x.shape[0]
    xs = pltpu.roll(x, rows - k, 0)
    row = lax.broadcasted_iota(jnp.int32, nxt.shape, 0)
    bottom = jnp.where(row >= HALO - k, pltpu.roll(nxt, HALO - k, 0), xs[rows - HALO:])
    return jnp.concatenate([xs[:rows - HALO], bottom], axis=0)


def _halo_before(T, block_of, col=0):
    per = T // HALO
    return pl.BlockSpec((HALO, BRANCH), lambda t: (jnp.maximum(block_of(t) * per - 1, 0), col))


def _halo_after(T, block_of, S, col=0):
    per = T // HALO
    return pl.BlockSpec((HALO, BRANCH), lambda t: (jnp.minimum((block_of(t) + 1) * per, S // HALO - 1), col))


def _lru_fwd(u, lw, S):
    T = min(256, S)
    nb = S // T
    row = pl.BlockSpec((T, BRANCH), lambda t: (t, 0))
    vecs = [lw["cw0"], lw["cw1"], lw["cw2"], lw["cw3"], lw["conv_b"], lw["wa"], lw["wx"], lw["ba"], lw["bx"],
            lw["lam"]]

    def body(ax, ax_before, ay, cw0, cw1, cw2, cw3, cb, wa, wx, ba, bx, lam, xc_o, r_o, i_o, a_o, h_o, ya_o, hc):
        t = pl.program_id(0)

        @pl.when(t == 0)
        def _():
            hc[...] = jnp.zeros_like(hc)

        x = ax[...]
        before = jnp.where(t == 0, 0.0, ax_before[...])
        xc = (cw3[...] * x + cw2[...] * _rows_down(x, before, 1) + cw1[...] * _rows_down(x, before, 2)
              + cw0[...] * _rows_down(x, before, 3) + cb[...])
        r = _sigmoid(_dot(xc, wa[...], "nn") + ba[...])
        gi = _sigmoid(_dot(xc, wx[...], "nn") + bx[...])
        sp = _softplus(-lam[...])
        la = -LRU_C * r * sp
        a = jnp.exp(la)
        mult = jnp.sqrt(_neg_expm1(2.0 * la))
        A, B = _scan_rows(a, mult * gi * xc, T)
        h = B + A * hc[...]
        h_o[...] = h
        hc[...] = h_o[pl.ds(T - 1, 1), :]
        xc_o[...] = xc
        r_o[...] = r
        i_o[...] = gi
        a_o[...] = a
        gy, _ = _gelu_and_grad(ay[...])
        ya_o[...] = (gy * h).astype(ya_o.dtype)

    outs = pl.pallas_call(
        body,
        name="lru_fwd",
        grid=(nb,),
        in_specs=[pl.BlockSpec((T, BRANCH), lambda t: (t, U_AX // BRANCH)),
                  _halo_before(T, lambda t: t, U_AX // BRANCH),
                  pl.BlockSpec((T, BRANCH), lambda t: (t, U_AY // BRANCH))] + [_full_spec(v) for v in vecs],
        out_specs=[row] * 6,
        out_shape=[jax.ShapeDtypeStruct((S, BRANCH), F32)] * 5 + [jax.ShapeDtypeStruct((S, BRANCH), BF16)],
        scratch_shapes=[pltpu.VMEM((1, BRANCH), F32)],
        compiler_params=pltpu.CompilerParams(dimension_semantics=("arbitrary",), vmem_limit_bytes=VMEM_LIMIT_BYTES),
    )(u, u, u, *vecs)
    return outs


def _lru_bwd(dya, u, sv, lw, S):
    T = min(256, S)
    nb = S // T
    rrow = pl.BlockSpec((T, BRANCH), lambda t: (nb - 1 - t, 0))
    sq = pl.BlockSpec((BRANCH, BRANCH), lambda t: (0, 0))
    vrow = pl.BlockSpec((1, BRANCH), lambda t: (0, 0))

    def block(t):
        return nb - 1 - t

    def body(dya_r, ay, h, h_before, xc_r, r_r, i_r, a_r, a_after, wa, wx, lam,
             day_o, dxc_o, dwa_o, dwx_o, dba_o, dbx_o, dlam_o, lcar, tmp):
        t = pl.program_id(0)
        h_prev = _rows_down(h[...], jnp.where(t == nb - 1, 0.0, h_before[...]), 1)
        a_next = _rows_up(a_r[...], jnp.where(t == 0, 0.0, a_after[...]), 1)

        @pl.when(t == 0)
        def _():
            lcar[...] = jnp.zeros_like(lcar)
            dwa_o[...] = jnp.zeros_like(dwa_o)
            dwx_o[...] = jnp.zeros_like(dwx_o)
            dba_o[...] = jnp.zeros_like(dba_o)
            dbx_o[...] = jnp.zeros_like(dbx_o)
            dlam_o[...] = jnp.zeros_like(dlam_o)

        gy, dgy = _gelu_and_grad(ay[...])
        dy = dya_r[...]
        day_o[...] = (dy * h[...] * dgy).astype(day_o.dtype)
        A, B = _scan_rows(a_next, dy * gy, T, reverse=True)
        lmb = B + A * lcar[...]
        tmp[...] = lmb
        lcar[...] = tmp[pl.ds(0, 1), :]
        xc, r, gi, a = xc_r[...], r_r[...], i_r[...], a_r[...]
        sp = _softplus(-lam[...])
        la = -LRU_C * r * sp
        mult = jnp.sqrt(_neg_expm1(2.0 * la))
        da = lmb * h_prev
        dmult = lmb * gi * xc
        di = lmb * mult * xc
        dxc = lmb * mult * gi
        dla = da * a - dmult * a * a / mult
        dr = dla * (-LRU_C * sp)
        dlam_o[...] += _colsum(dla * (LRU_C * r)) * _sigmoid(-lam[...])
        dpr = dr * r * (1.0 - r)
        dpi = di * gi * (1.0 - gi)
        dba_o[...] += _colsum(dpr)
        dbx_o[...] += _colsum(dpi)
        dxc_o[...] = dxc + _dot(dpr, wa[...], "nt") + _dot(dpi, wx[...], "nt")
        dwa_o[...] += _dot(xc, dpr, "tn")
        dwx_o[...] += _dot(xc, dpi, "tn")

    outs = pl.pallas_call(
        body,
        name="lru_bwd",
        grid=(nb,),
        in_specs=[rrow, pl.BlockSpec((T, BRANCH), lambda t: (nb - 1 - t, U_AY // BRANCH)), rrow,
                  _halo_before(T, block), rrow, rrow, rrow, rrow, _halo_after(T, block, S), sq, sq, vrow],
        out_specs=[rrow, rrow, sq, sq, vrow, vrow, vrow],
        out_shape=[jax.ShapeDtypeStruct((S, BRANCH), BF16), jax.ShapeDtypeStruct((S, BRANCH), F32),
                   jax.ShapeDtypeStruct((BRANCH, BRANCH), F32), jax.ShapeDtypeStruct((BRANCH, BRANCH), F32),
                   jax.ShapeDtypeStruct((1, BRANCH), F32), jax.ShapeDtypeStruct((1, BRANCH), F32),
                   jax.ShapeDtypeStruct((1, BRANCH), F32)],
        scratch_shapes=[pltpu.VMEM((1, BRANCH), F32), pltpu.VMEM((T, BRANCH), F32)],
        compiler_params=pltpu.CompilerParams(dimension_semantics=("arbitrary",), vmem_limit_bytes=VMEM_LIMIT_BYTES),
    )(dya, u, sv["h"], sv["h"], sv["xc"], sv["r"], sv["i"], sv["a"], sv["a"], lw["wa"], lw["wx"], lw["lam"])
    return outs


def _conv_bwd(dxc, u, lw, S):
    T = min(256, S)
    nb = S // T
    vecs = [lw["cw0"], lw["cw1"], lw["cw2"], lw["cw3"]]
    vrow = pl.BlockSpec((1, BRANCH), lambda t: (0, 0))

    def body(d_r, d_after, ax, ax_before, cw0, cw1, cw2, cw3, dax_o, dcw0_o, dcw1_o, dcw2_o, dcw3_o, dcb_o):
        t = pl.program_id(0)
        d = d_r[...]
        after = jnp.where(t == nb - 1, 0.0, d_after[...])
        x = ax[...]
        before = jnp.where(t == 0, 0.0, ax_before[...])
        dax = (cw3[...] * d + cw2[...] * _rows_up(d, after, 1) + cw1[...] * _rows_up(d, after, 2)
               + cw0[...] * _rows_up(d, after, 3))
        dax_o[...] = dax.astype(dax_o.dtype)
        sums = [_colsum(d * _rows_down(x, before, 3)), _colsum(d * _rows_down(x, before, 2)),
                _colsum(d * _rows_down(x, before, 1)), _colsum(d * x), _colsum(d)]
        outs = [dcw0_o, dcw1_o, dcw2_o, dcw3_o, dcb_o]

        @pl.when(t == 0)
        def _():
            for o, val in zip(outs, sums):
                o[...] = val

        @pl.when(t > 0)
        def _():
            for o, val in zip(outs, sums):
                o[...] += val

    res = pl.pallas_call(
        body,
        name="conv_bwd",
        grid=(nb,),
        in_specs=[pl.BlockSpec((T, BRANCH), lambda t: (t, 0)), _halo_after(T, lambda t: t, S),
                  pl.BlockSpec((T, BRANCH), lambda t: (t, U_AX // BRANCH)),
                  _halo_before(T, lambda t: t, U_AX // BRANCH)] + [_full_spec(v) for v in vecs],
        out_specs=[pl.BlockSpec((T, BRANCH), lambda t: (t, 0))] + [vrow] * 5,
        out_shape=[jax.ShapeDtypeStruct((S, BRANCH), BF16)] + [jax.ShapeDtypeStruct((1, BRANCH), F32)] * 5,
        compiler_params=pltpu.CompilerParams(dimension_semantics=("arbitrary",), vmem_limit_bytes=VMEM_LIMIT_BYTES),
    )(dxc, dxc, u, u, *vecs)
    return res[0], res[1:]


GLA_QK = GLA_HEADS * GLA_DK
GLA_V = GLA_HEADS * GLA_DV
GLA_SCALE = GLA_DK ** -0.5


def _gla_specs(TB, rev_nb=None):
    def rmap(t):
        return t if rev_nb is None else rev_nb - 1 - t
    return [
        pl.BlockSpec((TB, GLA_QK), lambda t: (rmap(t), U_BQ // GLA_QK)),
        pl.BlockSpec((TB, GLA_QK), lambda t: (rmap(t), U_BK // GLA_QK)),
        pl.BlockSpec((TB, GLA_V), lambda t: (rmap(t), U_BV // GLA_V)),
        pl.BlockSpec((TB, GLA_V), lambda t: (rmap(t), U_BR // GLA_V)),
        pl.BlockSpec((TB, LANES), lambda t: (rmap(t), U_BLOW // LANES)),
    ]


def _gla_gates(gl, wg2, bg, TB):
    pre = _dot(gl, wg2, "nn") + bg
    la = _log_sigmoid(pre) * (1.0 / GLA_TAU)
    _, gc = _scan_rows(None, la, TB, seg=CHUNK)
    return pre, la, gc


def _gla_fwd(u, gw, S):
    TB = min(512, S)
    nb = S // TB
    cpb = TB // CHUNK
    vecs = [gw["wg2"], gw["bg"], gw["ng"], gw["bd"]]

    def body(q_r, k_r, v_r, br_r, gl_r, wg2, bg, ng, bd, yb_o, oraw_o, st_o, st):
        t = pl.program_id(0)

        @pl.when(t == 0)
        def _():
            st[...] = jnp.zeros_like(st)

        _, la, gc = _gla_gates(gl_r[...], wg2[...], bg[...], TB)
        for c in range(cpb):
            sl = slice(c * CHUNK, (c + 1) * CHUNK)
            gt = _colsum(la[sl])
            kdec = k_r[sl, :] * jnp.exp(gt - gc[sl])
            d_t = _dot(v_r[sl, :], kdec, "tn") * bd[...]
            s_new = st[...] * jnp.exp(gt) + d_t
            st[...] = s_new
            st_o[c] = s_new
            oraw_o[sl, :] = _dot(q_r[sl, :] * GLA_SCALE, s_new, "nt")
        for h in range(GLA_HEADS):
            hs = slice(h * GLA_DV, (h + 1) * GLA_DV)
            oh = oraw_o[:, hs]
            on = oh * lax.rsqrt(jnp.mean(oh * oh, axis=-1, keepdims=True) + RMS_EPS)
            sil, _ = _silu_and_grad(br_r[:, hs])
            yb_o[:, hs] = (on * ng[:, hs] * sil).astype(yb_o.dtype)

    return pl.pallas_call(
        body,
        name="gla_fwd",
        grid=(nb,),
        in_specs=_gla_specs(TB) + [_full_spec(v) for v in vecs],
        out_specs=[pl.BlockSpec((TB, GLA_V), lambda t: (t, 0)), pl.BlockSpec((TB, GLA_V), lambda t: (t, 0)),
                   pl.BlockSpec((cpb, GLA_V, GLA_QK), lambda t: (t, 0, 0))],
        out_shape=[jax.ShapeDtypeStruct((S, GLA_V), BF16), jax.ShapeDtypeStruct((S, GLA_V), F32),
                   jax.ShapeDtypeStruct((S // CHUNK, GLA_V, GLA_QK), F32)],
        scratch_shapes=[pltpu.VMEM((GLA_V, GLA_QK), F32)],
        compiler_params=pltpu.CompilerParams(dimension_semantics=("arbitrary",), vmem_limit_bytes=VMEM_LIMIT_BYTES),
    )(u, u, u, u, u, *vecs)


def _gla_bwd(dyb, u, oraw, states, gw, S):
    TB = min(512, S)
    nb = S // TB
    cpb = TB // CHUNK
    vecs = [gw["wg2"], gw["bg"], gw["ng"], gw["bd"]]

    def rrow(width):
        return pl.BlockSpec((TB, width), lambda t: (nb - 1 - t, 0))

    def body(dyb_r, oraw_r, q_r, k_r, v_r, br_r, gl_r, st_r, sp_r, wg2, bg, ng, bd,
             dq_o, dk_o, dv_o, dbr_o, dgl_o, dwg2_o, dbg_o, dng_o, dcar, do_buf, dla_buf):
        t = pl.program_id(0)
        blk = nb - 1 - t

        @pl.when(t == 0)
        def _():
            dcar[...] = jnp.zeros_like(dcar)
            dwg2_o[...] = jnp.zeros_like(dwg2_o)
            dbg_o[...] = jnp.zeros_like(dbg_o)
            dng_o[...] = jnp.zeros_like(dng_o)

        pre, la, gc = _gla_gates(gl_r[...], wg2[...], bg[...], TB)
        for h in range(GLA_HEADS):
            hs = slice(h * GLA_DV, (h + 1) * GLA_DV)
            oh = oraw_r[:, hs]
            rs = lax.rsqrt(jnp.mean(oh * oh, axis=-1, keepdims=True) + RMS_EPS)
            on = oh * rs
            sil, dsil = _silu_and_grad(br_r[:, hs])
            dy = dyb_r[:, hs]
            dbr_o[:, hs] = (dy * on * ng[:, hs] * dsil).astype(dbr_o.dtype)
            don = dy * ng[:, hs] * sil
            dng_o[:, hs] += _colsum(dy * on * sil)
            do_buf[:, hs] = rs * (don - on * jnp.mean(don * on, axis=-1, keepdims=True))
        first = jnp.where(blk == 0, 0.0, 1.0)
        for c in reversed(range(cpb)):
            sl = slice(c * CHUNK, (c + 1) * CHUNK)
            s_n = st_r[c]
            s_prev = st_r[c - 1] if c > 0 else sp_r[0] * first
            gt = _colsum(la[sl])
            w = jnp.exp(gt - gc[sl])
            k_c = k_r[sl, :]
            kdec = k_c * w
            qs = q_r[sl, :] * GLA_SCALE
            do_c = do_buf[sl, :]
            dq_o[sl, :] = (_dot(do_c, s_n, "nn") * GLA_SCALE).astype(dq_o.dtype)
            d_n = _dot(do_c, qs, "tn") * bd[...] + dcar[...]
            dv_o[sl, :] = _dot(kdec, d_n, "nt").astype(dv_o.dtype)
            dkdec = _dot(v_r[sl, :], d_n, "nn")
            dk_o[sl, :] = (dkdec * w).astype(dk_o.dtype)
            tt = dkdec * kdec
            e = jnp.exp(gt)
            dgt = _colsum(tt) + _colsum(d_n * s_prev) * e
            _, rc = _scan_rows(None, -tt, CHUNK, reverse=True)
            dla_buf[sl, :] = rc + dgt
            dcar[...] = d_n * e
        dpre = dla_buf[...] * _sigmoid(-pre) * (1.0 / GLA_TAU)
        dbg_o[...] += _colsum(dpre)
        dgl_o[...] = _dot(dpre, wg2[...], "nt").astype(dgl_o.dtype)
        dwg2_o[...] += _dot(gl_r[...], dpre, "tn")

    return pl.pallas_call(
        body,
        name="gla_bwd",
        grid=(nb,),
        in_specs=[rrow(GLA_V), rrow(GLA_V)] + _gla_specs(TB, rev_nb=nb)
        + [pl.BlockSpec((cpb, GLA_V, GLA_QK), lambda t: (nb - 1 - t, 0, 0)),
           pl.BlockSpec((1, GLA_V, GLA_QK), lambda t: (jnp.maximum((nb - 1 - t) * cpb - 1, 0), 0, 0))]
        + [_full_spec(v) for v in vecs],
        out_specs=[rrow(GLA_QK), rrow(GLA_QK), rrow(GLA_V), rrow(GLA_V), rrow(LANES),
                   pl.BlockSpec((LANES, GLA_QK), lambda t: (0, 0)), pl.BlockSpec((1, GLA_QK), lambda t: (0, 0)),
                   pl.BlockSpec((1, GLA_V), lambda t: (0, 0))],
        out_shape=[jax.ShapeDtypeStruct((S, GLA_QK), BF16), jax.ShapeDtypeStruct((S, GLA_QK), BF16),
                   jax.ShapeDtypeStruct((S, GLA_V), BF16), jax.ShapeDtypeStruct((S, GLA_V), BF16),
                   jax.ShapeDtypeStruct((S, LANES), BF16), jax.ShapeDtypeStruct((LANES, GLA_QK), F32),
                   jax.ShapeDtypeStruct((1, GLA_QK), F32), jax.ShapeDtypeStruct((1, GLA_V), F32)],
        scratch_shapes=[pltpu.VMEM((GLA_V, GLA_QK), F32), pltpu.VMEM((TB, GLA_V), F32),
                        pltpu.VMEM((TB, GLA_QK), F32)],
        compiler_params=pltpu.CompilerParams(dimension_semantics=("arbitrary",), vmem_limit_bytes=VMEM_LIMIT_BYTES),
    )(dyb, oraw, u, u, u, u, u, states, states, *vecs)


FOX_SCALE = FOX_DH ** -0.5


def _fox_gate_fwd(u, bfp, S):
    T = min(512, S)

    def body(f_r, b_r, fc_o, car):
        t = pl.program_id(0)

        @pl.when(t == 0)
        def _():
            car[...] = jnp.zeros_like(car)

        _, cs = _scan_rows(None, _log_sigmoid(f_r[...] + b_r[...]), T)
        fc_o[...] = cs + car[...]
        car[...] = fc_o[pl.ds(T - 1, 1), :]

    return pl.pallas_call(
        body,
        name="fox_gate_fwd",
        grid=(S // T,),
        in_specs=[pl.BlockSpec((T, LANES), lambda t: (t, U_CF // LANES)), _full_spec(bfp)],
        out_specs=pl.BlockSpec((T, LANES), lambda t: (t, 0)),
        out_shape=jax.ShapeDtypeStruct((S, LANES), F32),
        scratch_shapes=[pltpu.VMEM((1, LANES), F32)],
        compiler_params=pltpu.CompilerParams(dimension_semantics=("arbitrary",), vmem_limit_bytes=VMEM_LIMIT_BYTES),
    )(u, bfp)


def _fox_gate_bwd(dfc, u, bfp, S):
    T = min(512, S)
    nb = S // T

    def body(d_r, f_r, b_r, df_o, db_o, car, tmp):
        t = pl.program_id(0)

        @pl.when(t == 0)
        def _():
            car[...] = jnp.zeros_like(car)
            db_o[...] = jnp.zeros_like(db_o)

        _, rc = _scan_rows(None, d_r[...], T, reverse=True)
        tmp[...] = rc + car[...]
        car[...] = tmp[pl.ds(0, 1), :]
        df = tmp[...] * _sigmoid(-(f_r[...] + b_r[...]))
        df_o[...] = df.astype(df_o.dtype)
        db_o[...] += _colsum(df)

    return pl.pallas_call(
        body,
        name="fox_gate_bwd",
        grid=(nb,),
        in_specs=[pl.BlockSpec((T, LANES), lambda t: (nb - 1 - t, 0)),
                  pl.BlockSpec((T, LANES), lambda t: (nb - 1 - t, U_CF // LANES)), _full_spec(bfp)],
        out_specs=[pl.BlockSpec((T, LANES), lambda t: (nb - 1 - t, 0)), pl.BlockSpec((1, LANES), lambda t: (0, 0))],
        out_shape=[jax.ShapeDtypeStruct((S, LANES), BF16), jax.ShapeDtypeStruct((1, LANES), F32)],
        scratch_shapes=[pltpu.VMEM((1, LANES), F32), pltpu.VMEM((T, LANES), F32)],
        compiler_params=pltpu.CompilerParams(dimension_semantics=("arbitrary",), vmem_limit_bytes=VMEM_LIMIT_BYTES),
    )(dfc, u, bfp)


def _fox_scores(q, k, fq, fk, qi, ki, tq, tk):
    s = _dot(q, k, "nt") * FOX_SCALE + (fq - fk)
    rows = lax.broadcasted_iota(jnp.int32, (tq, tk), 0) + qi * tq
    cols = lax.broadcasted_iota(jnp.int32, (tq, tk), 1) + ki * tk
    return jnp.where(cols <= rows, s, NEG_BIG)


def _fox_fwd(qh, kh, vh, fq, fk, S):
    tq = tk = min(512, S)
    nq, nk = S // tq, S // tk

    def body(q_r, k_r, v_r, fq_r, fk_r, o_o, lse_o, m_s, l_s, acc):
        qi, ki = pl.program_id(1), pl.program_id(2)

        @pl.when(ki == 0)
        def _():
            m_s[...] = jnp.full_like(m_s, NEG_BIG)
            l_s[...] = jnp.zeros_like(l_s)
            acc[...] = jnp.zeros_like(acc)

        @pl.when(ki <= qi)
        def _():
            s = _fox_scores(q_r[0], k_r[0], fq_r[0], fk_r[0], qi, ki, tq, tk)
            m_new = jnp.maximum(m_s[...], jnp.max(s, axis=-1, keepdims=True))
            p = jnp.exp(s - m_new)
            alpha = jnp.exp(m_s[...] - m_new)
            l_s[...] = alpha * l_s[...] + jnp.sum(p, axis=-1, keepdims=True)
            acc[...] = alpha * acc[...] + _dot(p, v_r[0], "nn")
            m_s[...] = m_new

        @pl.when(ki == nk - 1)
        def _():
            o_o[0] = acc[...] / l_s[...]
            lse_o[0] = m_s[...] + jnp.log(l_s[...])

    kv = pl.BlockSpec((1, tk, FOX_DH), lambda h, i, j: (h, jnp.minimum(j, i), 0))
    return pl.pallas_call(
        body,
        name="fox_fwd",
        grid=(FOX_HEADS, nq, nk),
        in_specs=[pl.BlockSpec((1, tq, FOX_DH), lambda h, i, j: (h, i, 0)), kv, kv,
                  pl.BlockSpec((1, tq, 1), lambda h, i, j: (h, i, 0)),
                  pl.BlockSpec((1, 1, tk), lambda h, i, j: (h, 0, jnp.minimum(j, i)))],
        out_specs=[pl.BlockSpec((1, tq, FOX_DH), lambda h, i, j: (h, i, 0)),
                   pl.BlockSpec((1, tq, 1), lambda h, i, j: (h, i, 0))],
        out_shape=[jax.ShapeDtypeStruct((FOX_HEADS, S, FOX_DH), F32), jax.ShapeDtypeStruct((FOX_HEADS, S, 1), F32)],
        scratch_shapes=[pltpu.VMEM((tq, 1), F32), pltpu.VMEM((tq, 1), F32), pltpu.VMEM((tq, FOX_DH), F32)],
        compiler_params=pltpu.CompilerParams(
            dimension_semantics=("parallel", "parallel", "arbitrary"), vmem_limit_bytes=VMEM_LIMIT_BYTES),
    )(qh, kh, vh, fq, fk)


def _fox_bwd_dq(qh, kh, vh, fq, fk, o, do, lse, S):
    tq = tk = min(512, S)
    nq, nk = S // tq, S // tk

    def body(q_r, k_r, v_r, fq_r, fk_r, o_r, do_r, lse_r, dq_o, dfq_o, dq_acc, df_acc):
        qi, ki = pl.program_id(1), pl.program_id(2)

        @pl.when(ki == 0)
        def _():
            dq_acc[...] = jnp.zeros_like(dq_acc)
            df_acc[...] = jnp.zeros_like(df_acc)

        @pl.when(ki <= qi)
        def _():
            s = _fox_scores(q_r[0], k_r[0], fq_r[0], fk_r[0], qi, ki, tq, tk)
            p = jnp.exp(s - lse_r[0])
            do_t = do_r[0]
            delta = jnp.sum(o_r[0] * do_t, axis=-1, keepdims=True)
            ds = p * (_dot(do_t, v_r[0], "nt") - delta)
            dq_acc[...] += _dot(ds, k_r[0], "nn")
            df_acc[...] += jnp.sum(ds, axis=-1, keepdims=True)

        @pl.when(ki == nk - 1)
        def _():
            dq_o[0] = dq_acc[...] * FOX_SCALE
            dfq_o[0] = df_acc[...]

    qrow = pl.BlockSpec((1, tq, FOX_DH), lambda h, i, j: (h, i, 0))
    qcol = pl.BlockSpec((1, tq, 1), lambda h, i, j: (h, i, 0))
    kv = pl.BlockSpec((1, tk, FOX_DH), lambda h, i, j: (h, jnp.minimum(j, i), 0))
    return pl.pallas_call(
        body,
        name="fox_bwd_dq",
        grid=(FOX_HEADS, nq, nk),
        in_specs=[qrow, kv, kv, qcol, pl.BlockSpec((1, 1, tk), lambda h, i, j: (h, 0, jnp.minimum(j, i))),
                  qrow, qrow, qcol],
        out_specs=[qrow, qcol],
        out_shape=[jax.ShapeDtypeStruct((FOX_HEADS, S, FOX_DH), F32), jax.ShapeDtypeStruct((FOX_HEADS, S, 1), F32)],
        scratch_shapes=[pltpu.VMEM((tq, FOX_DH), F32), pltpu.VMEM((tq, 1), F32)],
        compiler_params=pltpu.CompilerParams(
            dimension_semantics=("parallel", "parallel", "arbitrary"), vmem_limit_bytes=VMEM_LIMIT_BYTES),
    )(qh, kh, vh, fq, fk, o, do, lse)


def _fox_bwd_dkv(qh, kh, vh, fq, fk, o, do, lse, S):
    tq = tk = min(512, S)
    nq, nk = S // tq, S // tk

    def body(q_r, k_r, v_r, fq_r, fk_r, o_r, do_r, lse_r, dk_o, dv_o, dfk_o, dk_acc, dv_acc, df_acc):
        ki, qi = pl.program_id(1), pl.program_id(2)

        @pl.when(qi == 0)
        def _():
            dk_acc[...] = jnp.zeros_like(dk_acc)
            dv_acc[...] = jnp.zeros_like(dv_acc)
            df_acc[...] = jnp.zeros_like(df_acc)

        @pl.when(qi >= ki)
        def _():
            s = _fox_scores(q_r[0], k_r[0], fq_r[0], fk_r[0], qi, ki, tq, tk)
            p = jnp.exp(s - lse_r[0])
            do_t = do_r[0]
            delta = jnp.sum(o_r[0] * do_t, axis=-1, keepdims=True)
            ds = p * (_dot(do_t, v_r[0], "nt") - delta)
            dv_acc[...] += _dot(p, do_t, "tn")
            dk_acc[...] += _dot(ds, q_r[0], "tn")
            df_acc[...] += _colsum(ds)

        @pl.when(qi == nq - 1)
        def _():
            dk_o[0] = dk_acc[...] * FOX_SCALE
            dv_o[0] = dv_acc[...]
            dfk_o[0] = df_acc[...]

    qrow = pl.BlockSpec((1, tq, FOX_DH), lambda h, j, i: (h, jnp.maximum(i, j), 0))
    qcol = pl.BlockSpec((1, tq, 1), lambda h, j, i: (h, jnp.maximum(i, j), 0))
    kv = pl.BlockSpec((1, tk, FOX_DH), lambda h, j, i: (h, j, 0))
    krow = pl.BlockSpec((1, 1, tk), lambda h, j, i: (h, 0, j))
    return pl.pallas_call(
        body,
        name="fox_bwd_dkv",
        grid=(FOX_HEADS, nk, nq),
        in_specs=[qrow, kv, kv, qcol, krow, qrow, qrow, qcol],
        out_specs=[kv, kv, krow],
        out_shape=[jax.ShapeDtypeStruct((FOX_HEADS, S, FOX_DH), F32), jax.ShapeDtypeStruct((FOX_HEADS, S, FOX_DH), F32),
                   jax.ShapeDtypeStruct((FOX_HEADS, 1, S), F32)],
        scratch_shapes=[pltpu.VMEM((tk, FOX_DH), F32), pltpu.VMEM((tk, FOX_DH), F32), pltpu.VMEM((1, tk), F32)],
        compiler_params=pltpu.CompilerParams(
            dimension_semantics=("parallel", "parallel", "arbitrary"), vmem_limit_bytes=VMEM_LIMIT_BYTES),
    )(qh, kh, vh, fq, fk, o, do, lse)


def _fox_call(name, body, tables, grid, in_specs, out_specs, out_shape, scratch, args, side):
    n_in, n_out, n_scr = len(in_specs), len(out_specs), len(scratch)
    semantics = ("parallel", "arbitrary")
    if side is not None:
        total = grid[0] * grid[1]
        phases = side["phases"]
        triggers = [0, total - 1] if len(phases) == 2 else [0, total * 7 // 10, total - 1]
        na, no = len(side["arrs"]), len(side["out_shapes"])
        s_in, s_out, s_sems = _side_specs(side)
        kernel_body = body

        def body(*refs):
            tabs, rest = refs[:len(tables)], refs[len(tables):]
            ins, s_ins = rest[:n_in], rest[n_in:n_in + na]
            rest = rest[n_in + na:]
            outs, s_outs = rest[:n_out], rest[n_out:n_out + no]
            rest = rest[n_out + no:]
            scr, sems = rest[:n_scr], rest[n_scr:]
            flat = pl.program_id(0) * grid[1] + pl.program_id(1)
            for trigger, phase in zip(triggers[:-1], phases[:-1]):
                @pl.when(flat == trigger)
                def _(phase=phase):
                    phase(s_ins, s_outs, *sems)
            kernel_body(*tabs, *ins, *outs, *scr)

            @pl.when(flat == triggers[-1])
            def _():
                phases[-1](s_ins, s_outs, *sems)

        in_specs, out_specs = in_specs + s_in, out_specs + s_out
        out_shape, scratch = out_shape + side["out_shapes"], scratch + s_sems
        args = list(args) + side["arrs"]
        semantics = ("arbitrary", "arbitrary")
    res = pl.pallas_call(
        body,
        name=name,
        grid_spec=pltpu.PrefetchScalarGridSpec(num_scalar_prefetch=len(tables), grid=grid, in_specs=in_specs,
                                               out_specs=out_specs, scratch_shapes=scratch),
        out_shape=out_shape,
        compiler_params=pltpu.CompilerParams(dimension_semantics=semantics, vmem_limit_bytes=VMEM_LIMIT_BYTES),
    )(*tables, *args)
    return res[:n_out], res[n_out:]


FOX_TILE = 1024
FOX_AUG = 128
FOX_ONES = 3


def _fox_pairs(n, by_key):
    pairs = [(qi, ki) for qi in range(n) for ki in range(qi + 1)]
    if by_key:
        pairs.sort(key=lambda qk: (qk[1], qk[0]))
    qs = jnp.asarray([qk[0] for qk in pairs], jnp.int32)
    ks = jnp.asarray([qk[1] for qk in pairs], jnp.int32)
    return qs, ks


def _fox_augment(q, k, fcum):
    S = q.shape[0]
    def to_bf16_grid(a):
        return lax.reduce_precision(a, exponent_bits=8, mantissa_bits=7)

    hi = to_bf16_grid(fcum)
    mid = to_bf16_grid(fcum - hi)
    lo = to_bf16_grid(fcum - hi - mid)
    f3 = jnp.stack([hi, mid, lo], axis=-1).astype(BF16)
    ones = jnp.ones((S, FOX_HEADS, FOX_ONES), BF16)
    pad = jnp.zeros((S, FOX_HEADS, FOX_AUG - FOX_DH - 2 * FOX_ONES), BF16)
    q_aug = jnp.concatenate([(q * FOX_SCALE).astype(BF16), ones, f3, pad], axis=-1)
    k_aug = jnp.concatenate([k.astype(BF16), -f3, ones, pad], axis=-1)
    return jnp.transpose(q_aug, (1, 0, 2)), jnp.transpose(k_aug, (1, 0, 2))


def _fox_causal(sT):
    keys = lax.broadcasted_iota(jnp.int32, sT.shape, 0)
    queries = lax.broadcasted_iota(jnp.int32, sT.shape, 1)
    return jnp.where(keys <= queries, sT, NEG_BIG)


def _fox_fwd(qT, ka, vT, S, side=None):
    t = min(FOX_TILE, S)
    n = S // t
    qi_tab, ki_tab = _fox_pairs(n, by_key=False)

    def body(qi_ref, ki_ref, qT_r, ka_r, vT_r, oT_o, lse_o, m_s, l_s, acc):
        step = pl.program_id(1)
        qi, ki = qi_ref[step], ki_ref[step]

        @pl.when(ki == 0)
        def _():
            m_s[...] = jnp.full_like(m_s, NEG_BIG)
            l_s[...] = jnp.zeros_like(l_s)
            acc[...] = jnp.zeros_like(acc)

        def update(masked):
            sT = _dot(ka_r[0], qT_r[0], "nn")
            if masked:
                sT = _fox_causal(sT)
            m_new = jnp.maximum(m_s[...], jnp.max(sT, axis=0, keepdims=True))
            p = jnp.exp(sT - m_new)
            alpha = jnp.exp(m_s[...] - m_new)
            l_s[...] = alpha * l_s[...] + jnp.sum(p, axis=0, keepdims=True)
            acc[...] = alpha * acc[...] + _dot(vT_r[0], p, "nn")
            m_s[...] = m_new

        @pl.when(ki < qi)
        def _():
            update(False)

        @pl.when(ki == qi)
        def _():
            update(True)
            oT_o[0] = acc[...] / l_s[...]
            lse_o[0] = m_s[...] + jnp.log(l_s[...])

    return _fox_call(
        "fox_fwd", body, (qi_tab, ki_tab), (FOX_HEADS, int(qi_tab.shape[0])),
        [pl.BlockSpec((1, FOX_AUG, t), lambda h, s, qt, kt: (h, 0, qt[s])),
         pl.BlockSpec((1, t, FOX_AUG), lambda h, s, qt, kt: (h, kt[s], 0)),
         pl.BlockSpec((1, FOX_DH, t), lambda h, s, qt, kt: (h, 0, kt[s]))],
        [pl.BlockSpec((1, FOX_DH, t), lambda h, s, qt, kt: (h, 0, qt[s])),
         pl.BlockSpec((1, 1, t), lambda h, s, qt, kt: (h, 0, qt[s]))],
        [jax.ShapeDtypeStruct((FOX_HEADS, FOX_DH, S), F32), jax.ShapeDtypeStruct((FOX_HEADS, 1, S), F32)],
        [pltpu.VMEM((1, t), F32), pltpu.VMEM((1, t), F32), pltpu.VMEM((FOX_DH, t), F32)],
        (qT, ka, vT), side)


FOX_BIAS_ROWS = 8


def _fox_bwd(qT, qa, ka, kT, v, do, doT, oT, lse, S, side=None):
    t = min(FOX_TILE, S)
    n = S // t
    qi_tab, ki_tab = _fox_pairs(n, by_key=True)
    n_steps = int(qi_tab.shape[0])
    slab = slice(FOX_DH, FOX_DH + FOX_BIAS_ROWS)

    def body(qi_ref, ki_ref, qT_r, qa_r, ka_r, kT_r, v_r, do_r, doT_r, oT_r, lse_r,
             dq_o, dfq_o, dk_o, dfk_o, dv_o, dq_acc, dk_acc, dv_acc):
        step = pl.program_id(1)
        qi, ki = qi_ref[step], ki_ref[step]

        @pl.when(step == 0)
        def _():
            dq_acc[...] = jnp.zeros_like(dq_acc)

        @pl.when(qi == ki)
        def _():
            dk_acc[...] = jnp.zeros_like(dk_acc)
            dv_acc[...] = jnp.zeros_like(dv_acc)

        def update(masked):
            sT = _dot(ka_r[0], qT_r[0], "nn")
            if masked:
                sT = _fox_causal(sT)
            pT = jnp.exp(sT - lse_r[0])
            delta = jnp.sum(oT_r[0] * doT_r[0], axis=0, keepdims=True)
            dsT = pT * (_dot(v_r[0], doT_r[0], "nn") - delta)
            dv_acc[...] += _dot(pT, do_r[0], "nn")
            dk_acc[...] += _dot(dsT, qa_r[0], "nn")
            dq_acc[qi] += _dot(kT_r[0], dsT, "nn")

        @pl.when(qi > ki)
        def _():
            update(False)

        @pl.when(qi == ki)
        def _():
            update(True)

        @pl.when(qi == n - 1)
        def _():
            dk = dk_acc[...]
            dk_o[0] = dk[:, :FOX_DH]
            dfk_o[0] = dk.T[slab]
            dv_o[0] = dv_acc[...]

        @pl.when(step == n_steps - 1)
        def _():
            for j in range(n):
                dqT = dq_acc[j]
                dq_o[0, j * t:(j + 1) * t, :] = dqT.T[:, :FOX_DH]
                dfq_o[0, :, j * t:(j + 1) * t] = dqT[slab]

    def qlane(rows):
        return pl.BlockSpec((1, rows, t), lambda h, s, qt, kt: (h, 0, qt[s]))

    def qrow(cols):
        return pl.BlockSpec((1, t, cols), lambda h, s, qt, kt: (h, qt[s], 0))

    def krow(cols):
        return pl.BlockSpec((1, t, cols), lambda h, s, qt, kt: (h, kt[s], 0))

    def klane(rows):
        return pl.BlockSpec((1, rows, t), lambda h, s, qt, kt: (h, 0, kt[s]))

    def head(rows, cols):
        return pl.BlockSpec((1, rows, cols), lambda h, s, qt, kt: (h, 0, 0))

    return _fox_call(
        "fox_bwd", body, (qi_tab, ki_tab), (FOX_HEADS, n_steps),
        [qlane(FOX_AUG), qrow(FOX_AUG), krow(FOX_AUG), klane(FOX_AUG), krow(FOX_DH), qrow(FOX_DH), qlane(FOX_DH),
         qlane(FOX_DH), qlane(1)],
        [head(S, FOX_DH), head(FOX_BIAS_ROWS, S), krow(FOX_DH), klane(FOX_BIAS_ROWS), krow(FOX_DH)],
        [jax.ShapeDtypeStruct((FOX_HEADS, S, FOX_DH), F32), jax.ShapeDtypeStruct((FOX_HEADS, FOX_BIAS_ROWS, S), F32),
         jax.ShapeDtypeStruct((FOX_HEADS, S, FOX_DH), F32), jax.ShapeDtypeStruct((FOX_HEADS, FOX_BIAS_ROWS, S), F32),
         jax.ShapeDtypeStruct((FOX_HEADS, S, FOX_DH), F32)],
        [pltpu.VMEM((n, FOX_AUG, t), F32), pltpu.VMEM((t, FOX_AUG), F32), pltpu.VMEM((t, FOX_DH), F32)],
        (qT, qa, ka, kT, v, do, doT, oT, lse), side)


def _fox_prep(u, fcum, S):
    T = min(512, S)
    head_of = jnp.arange(BRANCH) // FOX_DH
    dim_of = jnp.arange(BRANCH) % FOX_DH
    heads = jnp.arange(FOX_HEADS)[:, None, None]
    sel = (head_of[None, :, None] == heads) & (dim_of[None, :, None] == jnp.arange(FOX_AUG)[None, None, :])
    sel_q = (sel * FOX_SCALE).astype(BF16)
    sel_k = sel.astype(BF16)
    sel_vT = jnp.swapaxes(sel[:, :, :FOX_DH], 1, 2).astype(BF16)
    piece = jnp.arange(FOX_ONES * LANES) // LANES
    lane = jnp.arange(FOX_ONES * LANES) % LANES
    col = jnp.arange(FOX_AUG)[None, None, :]
    at_q = (lane[None, :, None] == heads) & (col == FOX_DH + FOX_ONES + piece[None, :, None])
    at_k = (lane[None, :, None] == heads) & (col == FOX_DH + piece[None, :, None])
    bias_q = at_q.astype(BF16)
    bias_k = (-at_k.astype(F32)).astype(BF16)
    cols = jnp.arange(FOX_AUG)[None, :]
    ones_q = ((cols >= FOX_DH) & (cols < FOX_DH + FOX_ONES)).astype(F32)
    ones_k = ((cols >= FOX_DH + FOX_ONES) & (cols < FOX_DH + 2 * FOX_ONES)).astype(F32)
    consts = [sel_q, sel_k, sel_vT, bias_q, bias_k, ones_q, ones_k]

    def body(cq, ck, cv, fc, sq, sk, svT, bq, bk, oq, ok, qa_o, ka_o, qT_o, kT_o, vh_o, vT_o):
        f = fc[...]
        hi = f.astype(BF16).astype(F32)
        mid = (f - hi).astype(BF16).astype(F32)
        lo = (f - hi - mid).astype(BF16).astype(F32)
        pieces = jnp.concatenate([hi, mid, lo], axis=1)
        q, k, v = cq[...], ck[...], cv[...]
        for h in range(FOX_HEADS):
            qa = _dot(q, sq[h], "nn") + _dot(pieces, bq[h], "nn") + oq[...]
            ka = _dot(k, sk[h], "nn") + _dot(pieces, bk[h], "nn") + ok[...]
            qa_o[h] = qa.astype(qa_o.dtype)
            ka_o[h] = ka.astype(ka_o.dtype)
            qT_o[h] = qa.T.astype(qT_o.dtype)
            kT_o[h] = ka.T.astype(kT_o.dtype)
            vT_o[h] = _dot(svT[h], v, "nt").astype(vT_o.dtype)
            vh_o[h] = _dot(v, svT[h], "nt").astype(vh_o.dtype)

    def win(off):
        return pl.BlockSpec((T, BRANCH), functools.partial(lambda i, blk: (i, blk), blk=off // BRANCH))

    def rows(c):
        return pl.BlockSpec((FOX_HEADS, T, c), lambda i: (0, i, 0))

    def lanes(r):
        return pl.BlockSpec((FOX_HEADS, r, T), lambda i: (0, 0, i))

    bf = lambda *shape: jax.ShapeDtypeStruct((FOX_HEADS,) + shape, BF16)
    return pl.pallas_call(
        body,
        name="fox_prep",
        grid=(S // T,),
        in_specs=[win(U_CQ), win(U_CK), win(U_CV), pl.BlockSpec((T, LANES), lambda i: (i, 0))]
        + [_full_spec(c) for c in consts],
        out_specs=[rows(FOX_AUG), rows(FOX_AUG), lanes(FOX_AUG), lanes(FOX_AUG), rows(FOX_DH), lanes(FOX_DH)],
        out_shape=[bf(S, FOX_AUG), bf(S, FOX_AUG), bf(FOX_AUG, S), bf(FOX_AUG, S), bf(S, FOX_DH), bf(FOX_DH, S)],
        compiler_params=pltpu.CompilerParams(dimension_semantics=("parallel",), vmem_limit_bytes=VMEM_LIMIT_BYTES),
    )(u, u, u, fcum, *consts)


def _to_heads(x2d, S):
    return jnp.transpose(x2d.reshape(S, FOX_HEADS, FOX_DH), (1, 0, 2))


def _from_heads(xh, S):
    return jnp.transpose(xh, (1, 0, 2)).reshape(S, FOX_HEADS * FOX_DH)


def _ffn_fwd(tag, x, wgT, wuT, wd, g, b, S):
    def up_epi(accs):
        gate, up = accs
        sil, _ = _silu_and_grad(gate)
        return [gate, up, sil * up]

    gate, up, act = _mm(tag + "_up", "nt", [x], [wgT, wuT], [(0, 0, 0), (1, 0, 1)], 2, up_epi, [],
                        [BF16, BF16, BF16], S, D_FF, D_MODEL, tn=1408)

    def down_epi(accs, xr, gg, bb):
        z = ALPHA * xr + 0.5 * accs[0]
        return [z, _ln_fwd(z, gg, bb)]

    z, xn = _mm(tag + "_down", "nn", [act], [wd], [(0, 0, 0)], 1, down_epi, [(x, "mn", 0), (g, "n"), (b, "n")],
                [F32, F32], S, D_MODEL, D_FF, tk=D_FF)
    return xn, dict(x=x, gate=gate, up=up, act=act, z=z)


def _ln_bwd_call(tag, dy, z, g, S):
    def fn(dy_t, z_t, g_t):
        dz, xhat = _ln_bwd(dy_t, z_t, g_t)
        return [dz], [_colsum(dy_t * xhat), _colsum(dy_t)]

    (dz,), (dg, db) = _rowwise(tag + "_ln_bwd", fn, [dy, z], [g], [(D_MODEL, F32)], [D_MODEL, D_MODEL], S)
    return dz, dg, db


def _ffn_bwd(tag, dxn, sv, wgT, wuT, wd, g, S, gdt=F32):
    dz, dg, db = _ln_bwd_call(tag, dxn, sv["z"], g, S)

    def act_epi(accs, gate, up):
        da = 0.5 * accs[0]
        sil, dsil = _silu_and_grad(gate.astype(F32))
        return [da * up.astype(F32) * dsil, da * sil]

    dgate, dup = _mm(tag + "_dact", "nt", [dz], [wd], [(0, 0, 0)], 1, act_epi,
                     [(sv["gate"], "mn", 0), (sv["up"], "mn", 0)], [BF16, BF16], S, D_FF, D_MODEL, tn=1408)
    dwd = _mm1(tag + "_dwd", "tn", sv["act"], dz, D_FF, D_MODEL, S, scale=0.5, tm=1408, out_dtype=gdt)

    def two(accs):
        return [accs[0], accs[1]]

    dwgT, dwuT = _mm(tag + "_dwup", "tn", [dgate, dup], [sv["x"]], [(0, 0, 0), (1, 1, 0)], 2, two, [], [gdt, gdt],
                     D_FF, D_MODEL, S, tm=1408, tk=512)

    def dx_epi(accs, dzr):
        return [accs[0] + ALPHA * dzr]

    (dx,) = _mm(tag + "_dx", "nn", [dgate, dup], [wgT, wuT], [(0, 0, 0), (0, 1, 1)], 1, dx_epi, [(dz, "mn", 0)],
                [F32], S, D_MODEL, D_FF, tm=1024, tk=1408)
    return dx, dict(w_upT=jnp.concatenate([dwgT, dwuT], axis=0), w_down=dwd, ln_g=dg, ln_b=db)


def _mixer_fwd(x1, w, S, side=None):
    u = _mm1("w_in", "nt", x1, w["w_inT_p"], S, U_WIDTH, D_MODEL, tm=1024, tn=1536)
    xc, r, gi, a, h, y_a = _lru_fwd(u, w["lru"], S)
    y_b, oraw, states = _gla_fwd(u, w["gla"], S)
    fcum = _fox_gate_fwd(u, w["bfp"], S)
    qa, ka, qT, kT, vh, vT = _fox_prep(u, fcum, S)
    (oT, lse), side_out = _fox_fwd(qT, ka, vT, S, side)
    y_c = jnp.transpose(oT, (2, 0, 1)).reshape(S, BRANCH).astype(BF16)

    def merge_epi(accs, g0, g1, g2):
        merged = _sigmoid(g0) * accs[0] + _sigmoid(g1) * accs[1] + _sigmoid(g2) * accs[2]
        return [accs[0], accs[1], accs[2], merged]

    wb = w["w_branchT"]
    yp0, yp1, yp2, merged = _mm(
        "merge", "nt", [y_a, y_b, y_c], [wb[0], wb[1], wb[2]], [(0, 0, 0), (1, 1, 1), (2, 2, 2)], 3, merge_epi,
        [(u, "mn", 0), (u, "mn", 1), (u, "mn", 2)], [F32, F32, F32, BF16], S, D_MODEL, BRANCH, tm=256)

    def out_epi(accs, xr, gg, bb):
        z = ALPHA * xr + accs[0]
        return [z, _ln_fwd(z, gg, bb)]

    z2, x2 = _mm("w_out", "nn", [merged], [w["w_out"]], [(0, 0, 0)], 1, out_epi,
                 [(x1, "mn", 0), (w["ln2_g"], "n"), (w["ln2_b"], "n")], [F32, F32], S, D_MODEL, D_MODEL)
    sv = dict(x=x1, u=u, xc=xc, r=r, i=gi, a=a, h=h, y_a=y_a, y_b=y_b, y_c=y_c, oraw=oraw,
              states=states, qT=qT, qa=qa, ka=ka, kT=kT, vh=vh, oT=oT, lse=lse, yp=(yp0, yp1, yp2), merged=merged,
              z=z2)
    return x2, sv, side_out


def _mixer_bwd(dx2, sv, w, S, side=None, gdt=F32):
    u = sv["u"]
    dz, dg2, db2 = _ln_bwd_call("mix", dx2, sv["z"], w["ln2_g"], S)

    def dm_epi(accs, y0, y1, y2, g0, g1, g2):
        dm = accs[0]
        outs_p, outs_g = [], []
        for yp, gl in ((y0, g0), (y1, g1), (y2, g2)):
            sg = _sigmoid(gl)
            outs_p.append(dm * sg)
            outs_g.append(dm * yp * sg * (1.0 - sg))
        return outs_p + outs_g

    yp = sv["yp"]
    dyp0, dyp1, dyp2, dgl0, dgl1, dgl2 = _mm(
        "dmerged", "nt", [dz], [w["w_out"]], [(0, 0, 0)], 1, dm_epi,
        [(yp[0], "mn", 0), (yp[1], "mn", 0), (yp[2], "mn", 0), (u, "mn", 0), (u, "mn", 1), (u, "mn", 2)],
        [BF16] * 6, S, D_MODEL, D_MODEL, tm=256)
    dw_out = _mm1("dw_out", "tn", sv["merged"], dz, D_MODEL, D_MODEL, S, out_dtype=gdt)
    wb = w["w_branchT"]
    dys, dwbs = [], []
    for j, (yj, dyp) in enumerate(((sv["y_a"], dyp0), (sv["y_b"], dyp1), (sv["y_c"], dyp2))):
        dys.append(_mm1("dy_branch%d" % j, "nn", dyp, wb[j], S, BRANCH, D_MODEL))
        dwbs.append(_mm1("dw_branch%d" % j, "tn", dyp, yj, D_MODEL, BRANCH, S, out_dtype=gdt))
    day, dxc, dwa, dwx, dba, dbx, dlam = _lru_bwd(dys[0], u, sv, w["lru"], S)
    dax, (dcw0, dcw1, dcw2, dcw3, dcb) = _conv_bwd(dxc, u, w["lru"], S)
    dbq, dbk, dbv, dbr, dglow, dwg2p, dbg, dng = _gla_bwd(dys[1], u, sv["oraw"], sv["states"], w["gla"], S)
    doh = _to_heads(dys[2], S)
    (dqh, dfq, dkh, dfk, dvh), side_out = _fox_bwd(sv["qT"], sv["qa"], sv["ka"], sv["kT"], sv["vh"], doh,
                                                   jnp.swapaxes(doh, 1, 2), sv["oT"], sv["lse"], S, side)
    dqh = dqh * FOX_SCALE
    dfc = jnp.transpose(dfq[:, FOX_ONES, :] - dfk[:, 0, :])
    dfc = jnp.pad(dfc, ((0, 0), (0, LANES - FOX_HEADS)))
    dcf, dbf = _fox_gate_bwd(dfc, u, w["bfp"], S)
    du = jnp.concatenate(
        [dgl0, dgl1, dgl2, dax, day, dbq, dbk, dbv, dbr, _from_heads(dqh, S).astype(BF16),
         _from_heads(dkh, S).astype(BF16), _from_heads(dvh, S).astype(BF16), dglow, dcf,
         jnp.zeros((S, U_WIDTH - U_CF - LANES), BF16)], axis=1)
    dw_inT_p = _mm1("dw_in", "tn", du, sv["x"], U_WIDTH, D_MODEL, S, tm=1536, out_dtype=gdt)

    def dx_epi(accs, dzr):
        return [accs[0] + ALPHA * dzr]

    (dx1,) = _mm("dx_mix", "nn", [du], [w["w_inT_p"]], [(0, 0, 0)], 1, dx_epi, [(dz, "mn", 0)], [F32], S, D_MODEL,
                 U_WIDTH, tm=1024, tk=1536)
    pieces = sorted(W_IN_SEGMENTS)
    dw_inT = jnp.concatenate([dw_inT_p[dst:dst + width] for _, width, dst in pieces], axis=0)
    eye = jnp.eye(LRU_BLOCKS, dtype=F32)
    dwa_b = jnp.einsum("ncmd,nm->ncd", dwa.reshape(LRU_BLOCKS, 64, LRU_BLOCKS, 64), eye)
    dwx_b = jnp.einsum("ncmd,nm->ncd", dwx.reshape(LRU_BLOCKS, 64, LRU_BLOCKS, 64), eye)
    grads = dict(
        w_inT=dw_inT, w_out=dw_out, w_branchT=jnp.stack(dwbs), ln2_g=dg2, ln2_b=db2,
        conv_w=jnp.concatenate([dcw0, dcw1, dcw2, dcw3], axis=0).astype(gdt), conv_b=dcb, lru_wa=dwa_b, lru_wx=dwx_b,
        lru_ba=dba, lru_bx=dbx, lru_lambda=dlam, gla_w_g2=dwg2p[:GLA_LOWRANK].astype(gdt), gla_b_g=dbg, gla_norm_g=dng,
        fox_b_f=dbf[:, :FOX_HEADS])
    return dx1, grads, side_out


def _ple_fwd(x3, p_i, w, S):
    pe = _mm1("ple_proj", "nt", p_i, w["ple_w_projT"], S, D_MODEL, PLE_DIM)

    def epi(accs, xr, per, bg, gg, bb):
        sg = _sigmoid(accs[0] + bg)
        z = ALPHA * xr + sg * per
        return [sg, z, _ln_fwd(z, gg, bb)]

    sg, z4, x4 = _mm("ple_gate", "nn", [x3], [w["ple_w_gate"]], [(0, 0, 0)], 1, epi,
                     [(x3, "mn", 0), (pe, "mn", 0), (w["ple_b_gate"], "n"), (w["ln4_g"], "n"), (w["ln4_b"], "n")],
                     [F32, F32, F32], S, D_MODEL, D_MODEL)
    return x4, dict(x=x3, p=p_i, pe=pe, sg=sg, z=z4)


def _ple_bwd(dx4, sv, w, S, gdt=F32):
    def fn(dy_t, z_t, pe_t, sg_t, g_t):
        dz, xhat = _ln_bwd(dy_t, z_t, g_t)
        dgl = dz * pe_t * sg_t * (1.0 - sg_t)
        return [dz, dz * sg_t, dgl], [_colsum(dy_t * xhat), _colsum(dy_t), _colsum(dgl)]

    (dz, dpe, dgl), (dg4, db4, dbg) = _rowwise(
        "ple_bwd", fn, [dx4, sv["z"], sv["pe"], sv["sg"]], [w["ln4_g"]],
        [(D_MODEL, F32), (D_MODEL, BF16), (D_MODEL, BF16)], [D_MODEL] * 3, S)
    dwpT = _mm1("dw_ple_proj", "tn", dpe, sv["p"], D_MODEL, PLE_DIM, S, out_dtype=gdt)
    dwg = _mm1("dw_ple_gate", "tn", sv["x"], dgl, D_MODEL, D_MODEL, S, out_dtype=gdt)

    def dx_epi(accs, dzr):
        return [accs[0] + ALPHA * dzr]

    (dx3,) = _mm("dx_ple", "nt", [dgl], [w["ple_w_gate"]], [(0, 0, 0)], 1, dx_epi, [(dz, "mn", 0)], [F32], S,
                 D_MODEL, D_MODEL)
    return dx3, dict(ple_w_projT=dwpT, ple_w_gate=dwg, ple_b_gate=dbg, ln4_g=dg4, ln4_b=db4)


def _rows_of_all(g):
    return g.reshape((g.shape[0] * g.shape[1],) + g.shape[2:])


def _layer_weights(gathered, full, i):
    w = {}
    for tag in ("ffn1", "ffn2"):
        upT = _rows_of_all(gathered[tag + "_w_up"])
        w[tag] = (upT[:D_FF], upT[D_FF:], _rows_of_all(gathered[tag + "_w_down"]))
    w_inT = _rows_of_all(gathered["w_in"])
    placed = sorted((dst, src, width) for src, width, dst in W_IN_SEGMENTS)
    parts, pos = [], 0
    for dst, src, width in placed:
        if dst > pos:
            parts.append(jnp.zeros((dst - pos, D_MODEL), w_inT.dtype))
        parts.append(w_inT[src:src + width])
        pos = dst + width
    parts.append(jnp.zeros((U_WIDTH - pos, D_MODEL), w_inT.dtype))
    w["w_inT_p"] = jnp.concatenate(parts, axis=0)
    eye = jnp.eye(LRU_BLOCKS, dtype=F32)

    def dense(blocks):
        return jnp.einsum("ncd,nm->ncmd", blocks, eye).reshape(BRANCH, BRANCH).astype(BF16)

    def vec(name):
        return full[name][i].reshape(1, -1)

    cw = jnp.moveaxis(gathered["conv_w"], 0, 1).reshape(4, BRANCH)
    w_g2 = jnp.moveaxis(gathered["gla_w_g2"], 0, 1).reshape(GLA_LOWRANK, GLA_QK)
    w["lru"] = dict(cw0=cw[0:1], cw1=cw[1:2], cw2=cw[2:3], cw3=cw[3:4], conv_b=vec("conv_b"),
                    wa=dense(full["lru_wa"][i]), wx=dense(full["lru_wx"][i]), ba=vec("lru_ba"), bx=vec("lru_bx"),
                    lam=vec("lru_lambda"))
    hq = jnp.arange(GLA_QK) // GLA_DK
    hv = jnp.arange(GLA_V) // GLA_DV
    w["gla"] = dict(wg2=jnp.pad(w_g2, ((0, LANES - GLA_LOWRANK), (0, 0))).astype(BF16),
                    bg=vec("gla_b_g"), ng=vec("gla_norm_g"), bd=(hv[:, None] == hq[None, :]).astype(F32))
    w["bfp"] = jnp.pad(vec("fox_b_f"), ((0, 0), (0, LANES - FOX_HEADS)))
    w["w_branchT"] = jnp.moveaxis(gathered["w_branch"], 0, 1).reshape(3, D_MODEL, BRANCH)
    w["w_out"] = _rows_of_all(gathered["w_out"])
    w["ple_w_projT"] = _rows_of_all(gathered["ple_w_proj"])
    w["ple_w_gate"] = _rows_of_all(gathered["ple_w_gate"])
    for name in ("ln1_g", "ln1_b", "ln2_g", "ln2_b", "ln3_g", "ln3_b", "ln4_g", "ln4_b", "ple_b_gate"):
        w[name] = vec(name)
    return w


def _layer_fwd(x0, p_i, w, S, side=None):
    x1, s1 = _ffn_fwd("ffn1", x0, *w["ffn1"], w["ln1_g"], w["ln1_b"], S)
    x2, s2, side_out = _mixer_fwd(x1, w, S, side)
    x3, s3 = _ffn_fwd("ffn2", x2, *w["ffn2"], w["ln3_g"], w["ln3_b"], S)
    x4, s4 = _ple_fwd(x3, p_i, w, S)
    return x4, (s1, s2, s3, s4), side_out


def _layer_bwd(dx4, saved, w, S, side=None, gdt=F32):
    s1, s2, s3, s4 = saved
    dx3, g4 = _ple_bwd(dx4, s4, w, S, gdt)
    dx2, g3 = _ffn_bwd("ffn2", dx3, s3, *w["ffn2"], w["ln3_g"], S, gdt)
    dx1, g2, side_out = _mixer_bwd(dx2, s2, w, S, side, gdt)
    dx0, g1 = _ffn_bwd("ffn1", dx1, s1, *w["ffn1"], w["ln1_g"], S, gdt)
    grads = dict(g2)
    grads.update(g4)
    grads.update(ffn1_w_upT=g1["w_upT"], ffn1_w_down=g1["w_down"], ln1_g=g1["ln_g"], ln1_b=g1["ln_b"],
                 ffn2_w_upT=g3["w_upT"], ffn2_w_down=g3["w_down"], ln3_g=g3["ln_g"], ln3_b=g3["ln_b"])
    return dx0, grads, side_out


def _travel_grads(grads):
    return [_dest_pieces(n, grads[n + "T" if n in COLUMN_SHARDED else n]) for n, _ in SHARDED]


def _local_step(x, p, target, gathered0, layer1, full, overlap):
    S = x.shape[0]
    names = [n for n, _ in SHARDED]
    w0 = _layer_weights(gathered0, full, 0)
    h, saved0, got = _layer_fwd(x, p[0], w0, S, _gather_job(layer1) if overlap else None)
    w1 = _layer_weights(dict(zip(names, got)) if overlap else layer1, full, 1)
    h, saved1, _ = _layer_fwd(h, p[1], w1, S)

    def loss_fn(y, t):
        err = y - t
        return [err * (1.0 / D_MODEL)], [_colsum(err * err) * (0.5 / D_MODEL)]

    (dy,), (lsum,) = _rowwise("loss", loss_fn, [h, target], [], [(D_MODEL, F32)], [D_MODEL], S)
    loss = jnp.sum(lsum)
    dy, g1, _ = _layer_bwd(dy, saved1, w1, S, gdt=BF16 if overlap else F32)
    dy, g0, pieces1 = _layer_bwd(dy, saved0, w0, S, _scatter_job(_travel_grads(g1)) if overlap else None)
    return loss, dy, [g0, g1], pieces1 if overlap else None


def kernel(x, p, ffn1_w_up, ffn1_w_down, ln1_g, ln1_b, w_in, conv_w, conv_b, lru_wa, lru_ba, lru_wx, lru_bx, lru_lambda, gla_w_g2, gla_b_g, gla_norm_g, fox_b_f, w_branch, w_out, ln2_g, ln2_b, ffn2_w_up, ffn2_w_down, ln3_g, ln3_b, ple_w_proj, ple_w_gate, ple_b_gate, ln4_g, ln4_b, loss_target, m_ffn1_w_up, m_ffn1_w_down, m_ln1_g, m_ln1_b, m_w_in, m_conv_w, m_conv_b, m_lru_wa, m_lru_ba, m_lru_wx, m_lru_bx, m_lru_lambda, m_gla_w_g2, m_gla_b_g, m_gla_norm_g, m_fox_b_f, m_w_branch, m_w_out, m_ln2_g, m_ln2_b, m_ffn2_w_up, m_ffn2_w_down, m_ln3_g, m_ln3_b, m_ple_w_proj, m_ple_w_gate, m_ple_b_gate, m_ln4_g, m_ln4_b, v_ffn1_w_up, v_ffn1_w_down, v_ln1_g, v_ln1_b, v_w_in, v_conv_w, v_conv_b, v_lru_wa, v_lru_ba, v_lru_wx, v_lru_bx, v_lru_lambda, v_gla_w_g2, v_gla_b_g, v_gla_norm_g, v_fox_b_f, v_w_branch, v_w_out, v_ln2_g, v_ln2_b, v_ffn2_w_up, v_ffn2_w_down, v_ln3_g, v_ln3_b, v_ple_w_proj, v_ple_w_gate, v_ple_b_gate, v_ln4_g, v_ln4_b):
    env = dict(locals())
    wts = {n: env[n] for n in WEIGHTS}
    ms = {n: env["m_" + n] for n in WEIGHTS}
    vs = {n: env["v_" + n] for n in WEIGHTS}
    sharded = [n for n, _ in SHARDED]

    def travel(n, a):
        return jnp.swapaxes(a, -1, -2) if n in COLUMN_SHARDED else a

    shards = [[travel(n, wts[n][i]) if n in SHARDED_F32_GATHER else travel(n, wts[n][i]).astype(BF16) for n in sharded]
              for i in range(DEPTH)]
    gathered0 = dict(zip(sharded, _allgather_multi("gather_weights", shards[0])))
    full = {n: wts[n] for n in REPLICATED}

    loss_part, grad_x, layer_grads, pieces1 = _local_step(x[0], p[:, 0], loss_target[0], gathered0, shards[1], full,
                                                           True)
    loss = lax.psum(loss_part, MESH_AXES)

    dest = _travel_grads(layer_grads[0])
    got = _sibling_swap_multi("grad_sibling_swap", dest)
    core = lax.axis_index("c").astype(jnp.int32).reshape(1)
    pairs = [_pair_add("grad_pair_add_" + n, core, _as_rows(d, 2), _as_rows(g, 1))
             for n, d, g in zip(sharded, dest, got)]
    pieces0 = _chip_all_to_all_multi("grad_chip_all_to_all", pairs)
    rep = list(REPLICATED)
    rep_grads = [jnp.stack([layer_grads[i][n] for i in range(DEPTH)]).reshape(wts[n].shape) for n in rep]
    (gr,) = _allgather_multi("grad_gather_replicated", [_pack(rep_grads, F32)])

    kinds = ("grad", "delta", "new_m", "new_v")
    out = {}
    for k, n in enumerate(sharded):
        local = [_as_rows(travel(n, pieces.reshape((-1,) + shards[i][k].shape)), 1)
                 for i, pieces in ((0, pieces0[k]), (1, pieces1[k]))]
        res = _adamw("adamw_" + n, local, _as_rows(wts[n], 1), _as_rows(ms[n], 1), _as_rows(vs[n], 1))
        for kind, arr in zip(kinds, res):
            out[kind + "_" + n] = arr.reshape(wts[n].shape)
    res = _adamw("adamw_replicated", [gr], _pack([wts[n] for n in rep], F32)[None],
                 _pack([ms[n] for n in rep], F32)[None], _pack([vs[n] for n in rep], F32)[None])
    shapes = [wts[n].shape for n in rep]
    for kind, buf in zip(kinds, res):
        for n, arr in zip(rep, _unpack(buf[0], shapes)):
            out[kind + "_" + n] = arr
    return (loss, grad_x[None], *[out["grad_" + n] for n in WEIGHTS], *[out["delta_" + n] for n in WEIGHTS],
            *[out["new_m_" + n] for n in WEIGHTS], *[out["new_v_" + n] for n in WEIGHTS])
```

```python
import functools
import math

import jax
import jax.numpy as jnp
from jax import lax
from jax.experimental import pallas as pl
from jax.experimental.pallas import tpu as pltpu

F32 = jnp.float32
BF16 = jnp.bfloat16

N_DEV = 8
MESH_AXES = ("x", "y", "c")
DEPTH = 2
D_MODEL = 1024
D_FF = 2816
BRANCH = 512
CHUNK = 64
GLA_HEADS = 4
GLA_DK = 64
GLA_DV = 128
GLA_LOWRANK = 16
GLA_TAU = 16.0
FOX_HEADS = 8
FOX_DH = 64
PLE_DIM = 256
LRU_C = 8.0
LRU_BLOCKS = 8
LN_EPS = 1e-5
RMS_EPS = 1e-6
ALPHA = (2 * DEPTH) ** 0.25
LANES = 128
NEG_BIG = -1e30

ADAM_LR = 0.001
ADAM_B1 = 0.9
ADAM_B2 = 0.999
ADAM_EPS = 1e-08
ADAM_WD = 0.01
ADAM_STEP = 10

VMEM_LIMIT_BYTES = 56 * 1024 * 1024

U_GATES = 0
U_AX = 3072
U_AY = 3584
U_BQ = 4096
U_BK = 4352
U_BV = 4608
U_BR = 5120
U_CQ = 5632
U_CK = 6144
U_CV = 6656
U_BLOW = 7168
U_CF = 7296
U_WIDTH = 7680
W_IN_SEGMENTS = (
    (0, 512, U_AX), (512, 512, U_AY), (1024, 256, U_BQ), (1280, 256, U_BK), (1536, 512, U_BV),
    (2048, 16, U_BLOW), (2064, 512, U_BR), (2576, 512, U_CQ), (3088, 512, U_CK), (3600, 512, U_CV),
    (4112, 8, U_CF), (4120, 3072, U_GATES),
)

SHARDED = (
    ("ffn1_w_up", 2), ("ffn1_w_down", 1), ("w_in", 2), ("conv_w", 2), ("gla_w_g2", 2), ("w_branch", 3),
    ("w_out", 1), ("ffn2_w_up", 2), ("ffn2_w_down", 1), ("ple_w_proj", 2), ("ple_w_gate", 1),
)
SHARDED_F32_GATHER = ("conv_w", "gla_w_g2")
COLUMN_SHARDED = ("ffn1_w_up", "ffn2_w_up", "w_in", "w_branch", "ple_w_proj")
REPLICATED = ("ln1_g", "ln1_b", "conv_b", "lru_wa", "lru_ba", "lru_wx", "lru_bx", "lru_lambda", "gla_b_g",
              "gla_norm_g", "fox_b_f", "ln2_g", "ln2_b", "ln3_g", "ln3_b", "ple_b_gate", "ln4_g", "ln4_b")
WEIGHTS = ("ffn1_w_up", "ffn1_w_down", "ln1_g", "ln1_b", "w_in", "conv_w", "conv_b", "lru_wa", "lru_ba", "lru_wx",
           "lru_bx", "lru_lambda", "gla_w_g2", "gla_b_g", "gla_norm_g", "fox_b_f", "w_branch", "w_out", "ln2_g",
           "ln2_b", "ffn2_w_up", "ffn2_w_down", "ln3_g", "ln3_b", "ple_w_proj", "ple_w_gate", "ple_b_gate", "ln4_g",
           "ln4_b")


def _sigmoid(x):
    return 1.0 / (1.0 + jnp.exp(-x))


def _log1p_pos(e):
    return jnp.where(e < 1e-4, e * (1.0 - 0.5 * e), jnp.log(1.0 + e))


def _softplus(x):
    return jnp.maximum(x, 0.0) + _log1p_pos(jnp.exp(-jnp.abs(x)))


def _log_sigmoid(x):
    return -_softplus(-x)


def _neg_expm1(y):
    series = -y * (1.0 + y * (0.5 + y * (1.0 / 6.0 + y * (1.0 / 24.0 + y * (1.0 / 120.0)))))
    return jnp.where(y > -0.1, series, 1.0 - jnp.exp(y))


def _silu_and_grad(x):
    s = _sigmoid(x)
    return x * s, s * (1.0 + x * (1.0 - s))


_GELU_C = math.sqrt(2.0 / math.pi)


def _gelu_and_grad(x):
    inner = _GELU_C * (x + 0.044715 * x * x * x)
    t = jnp.tanh(inner)
    g = 0.5 * x * (1.0 + t)
    dg = 0.5 * (1.0 + t) + 0.5 * x * (1.0 - t * t) * _GELU_C * (1.0 + 3.0 * 0.044715 * x * x)
    return g, dg


def _ln_stats(z):
    mu = jnp.mean(z, axis=-1, keepdims=True)
    zc = z - mu
    var = jnp.mean(zc * zc, axis=-1, keepdims=True)
    rstd = lax.rsqrt(var + LN_EPS)
    return zc * rstd, rstd


def _ln_fwd(z, g, b):
    xhat, _ = _ln_stats(z)
    return xhat * g + b


def _ln_bwd(dy, z, g):
    xhat, rstd = _ln_stats(z)
    dxh = dy * g
    m1 = jnp.mean(dxh, axis=-1, keepdims=True)
    m2 = jnp.mean(dxh * xhat, axis=-1, keepdims=True)
    return rstd * (dxh - m1 - xhat * m2), xhat


def _colsum(x):
    return jnp.sum(x, axis=0, keepdims=True)


def _dot(a, b, dims):
    dn = {"nn": (((1,), (0,)), ((), ())), "nt": (((1,), (1,)), ((), ())), "tn": (((0,), (0,)), ((), ()))}[dims]
    return lax.dot_general(a.astype(BF16), b.astype(BF16), dn, preferred_element_type=F32)


def _scan_rows(a, b, length, reverse=False, seg=None):
    rows = lax.broadcasted_iota(jnp.int32, b.shape, 0)
    span = seg if seg else length
    pos = rows % span if seg else rows
    d = 1
    while d < span:
        shift = (length - d) if reverse else d
        valid = (pos < span - d) if reverse else (pos >= d)
        sb = jnp.where(valid, pltpu.roll(b, shift, 0), 0.0)
        if a is None:
            b = b + sb
        else:
            b = b + a * sb
            a = a * jnp.where(valid, pltpu.roll(a, shift, 0), 1.0)
        d *= 2
    return a, b


def _tile(dim, pref):
    if dim <= pref:
        return dim
    best = None
    t = LANES
    while t <= pref:
        if dim % t == 0:
            best = t
        t += LANES
    assert best is not None, (dim, pref)
    return best


def _full_spec(arr):
    nd = arr.ndim
    return pl.BlockSpec(arr.shape, lambda *_: (0,) * nd)


def _mm(name, dims, a_ops, b_ops, terms, n_acc, epilogue, extras, out_dtypes, M, N, K, tm=512, tn=1024, tk=1024):
    tm, tn, tk = _tile(M, tm), _tile(N, tn), _tile(K, tk)
    gm, gn, gk = M // tm, N // tn, K // tk
    a_bytes = sum(a.size * a.dtype.itemsize for a in a_ops)
    b_bytes = sum(b.size * b.dtype.itemsize for b in b_ops)
    n_outer = gk == 1 and b_bytes + a_bytes * gn < a_bytes + b_bytes * gm

    def spec(shape, fn):
        if n_outer:
            return pl.BlockSpec(shape, lambda j, i, k: fn(i, j, k))
        return pl.BlockSpec(shape, fn)

    if dims == "tn":
        a_spec = spec((tk, tm), lambda i, j, k: (k, i))
    else:
        a_spec = spec((tm, tk), lambda i, j, k: (i, k))
    if dims == "nt":
        b_spec = spec((tn, tk), lambda i, j, k: (j, k))
    else:
        b_spec = spec((tk, tn), lambda i, j, k: (k, j))
    e_specs, e_arrays = [], []
    for ex in extras:
        if ex[1] == "mn":
            e_specs.append(spec((tm, tn), functools.partial(lambda i, j, k, off: (i, j + off), off=ex[2])))
        else:
            e_specs.append(spec((1, tn), lambda i, j, k: (0, j)))
        e_arrays.append(ex[0])
    na, nb, ne, no = len(a_ops), len(b_ops), len(extras), len(out_dtypes)

    def body(*refs):
        a_refs = refs[:na]
        b_refs = refs[na:na + nb]
        e_refs = refs[na + nb:na + nb + ne]
        o_refs = refs[na + nb + ne:na + nb + ne + no]
        acc_refs = refs[na + nb + ne + no:]
        k = pl.program_id(2)

        @pl.when(k == 0)
        def _():
            for acc in acc_refs:
                acc[...] = jnp.zeros_like(acc)

        for r, ai, bi in terms:
            acc_refs[r][...] += _dot(a_refs[ai][...], b_refs[bi][...], dims)

        @pl.when(k == gk - 1)
        def _():
            res = epilogue([acc[...] for acc in acc_refs], *[e[...] for e in e_refs])
            for o, val in zip(o_refs, res):
                o[...] = val.astype(o.dtype)

    outs = pl.pallas_call(
        body,
        name=name,
        grid=(gn, gm, gk) if n_outer else (gm, gn, gk),
        in_specs=[a_spec] * na + [b_spec] * nb + e_specs,
        out_specs=[spec((tm, tn), lambda i, j, k: (i, j))] * no,
        out_shape=[jax.ShapeDtypeStruct((M, N), dt) for dt in out_dtypes],
        scratch_shapes=[pltpu.VMEM((tm, tn), F32)] * n_acc,
        compiler_params=pltpu.CompilerParams(
            dimension_semantics=("parallel", "parallel", "arbitrary"), vmem_limit_bytes=VMEM_LIMIT_BYTES),
    )(*a_ops, *b_ops, *e_arrays)
    return outs


def _mm1(name, dims, a, b, M, N, K, out_dtype=F32, scale=None, **kw):
    def epi(accs):
        return [accs[0] if scale is None else accs[0] * scale]
    return _mm(name, dims, [a], [b], [(0, 0, 0)], 1, epi, [], [out_dtype], M, N, K, **kw)[0]


def _rowwise(name, fn, row_ins, vec_ins, row_outs, sum_outs, S, tr=256, reverse=False):
    tr = min(tr, S)
    g = S // tr
    rmap = (lambda i: (g - 1 - i)) if reverse else (lambda i: i)
    in_specs, arrays = [], []
    for r in row_ins:
        if isinstance(r, tuple):
            arr, width, blk = r
            in_specs.append(pl.BlockSpec((tr, width), functools.partial(lambda i, blk: (rmap(i), blk), blk=blk)))
        else:
            arr = r
            in_specs.append(pl.BlockSpec((tr, arr.shape[1]), lambda i: (rmap(i), 0)))
        arrays.append(arr)
    for v in vec_ins:
        in_specs.append(_full_spec(v))
        arrays.append(v)
    nr, nv, no, ns = len(row_ins), len(vec_ins), len(row_outs), len(sum_outs)

    def body(*refs):
        ins = [r[...] for r in refs[:nr + nv]]
        o_refs = refs[nr + nv:nr + nv + no]
        s_refs = refs[nr + nv + no:]
        outs, sums = fn(*ins)
        for o, val in zip(o_refs, outs):
            o[...] = val.astype(o.dtype)
        if ns:
            i = pl.program_id(0)

            @pl.when(i == 0)
            def _():
                for s, val in zip(s_refs, sums):
                    s[...] = val

            @pl.when(i > 0)
            def _():
                for s, val in zip(s_refs, sums):
                    s[...] += val

    res = pl.pallas_call(
        body,
        name=name,
        grid=(g,),
        in_specs=in_specs,
        out_specs=[pl.BlockSpec((tr, c), lambda i: (rmap(i), 0)) for c, _ in row_outs]
        + [pl.BlockSpec((1, c), lambda i: (0, 0)) for c in sum_outs],
        out_shape=[jax.ShapeDtypeStruct((S, c), dt) for c, dt in row_outs]
        + [jax.ShapeDtypeStruct((1, c), F32) for c in sum_outs],
        compiler_params=pltpu.CompilerParams(
            dimension_semantics=("arbitrary",), vmem_limit_bytes=VMEM_LIMIT_BYTES),
    )(*arrays)
    return res[:no], res[no:]


def _win(arr, offset, width):
    assert offset % width == 0
    return (arr, width, offset // width)


MESH_ID = pl.DeviceIdType.MESH


def _remote(src, dst, send_sem, recv_sem, to):
    return pltpu.make_async_remote_copy(src_ref=src, dst_ref=dst, send_sem=send_sem, recv_sem=recv_sem,
                                        device_id=to, device_id_type=MESH_ID)


def _hbm_call(name, body, arrs, out_shapes, n_send, n_recv, n_local):
    return pl.pallas_call(
        body,
        name=name,
        in_specs=[pl.BlockSpec(memory_space=pltpu.HBM)] * len(arrs),
        out_specs=[pl.BlockSpec(memory_space=pltpu.HBM)] * len(out_shapes),
        out_shape=out_shapes,
        scratch_shapes=[pltpu.SemaphoreType.DMA((n_send,)), pltpu.SemaphoreType.DMA((n_recv,)),
                        pltpu.SemaphoreType.DMA((n_local,))],
        compiler_params=pltpu.CompilerParams(has_side_effects=True),
    )(*arrs)


def _side_job(arrs, out_shapes, n_send, n_recv, n_local, phases):
    return dict(arrs=list(arrs), out_shapes=list(out_shapes), sems=(n_send, n_recv, n_local), phases=phases)


def _side_specs(side):
    hbm = pl.BlockSpec(memory_space=pltpu.HBM)
    sems = [pltpu.SemaphoreType.DMA((k,)) for k in side["sems"]]
    return [hbm] * len(side["arrs"]), [hbm] * len(side["out_shapes"]), sems


def _gather_job(arrs):
    n = len(arrs)

    def plan(ins, outs, send_sems, recv_sems, local_sems):
        x, y, c = lax.axis_index("x"), lax.axis_index("y"), lax.axis_index("c")
        me, sibling = (x, y, c), (x, y, 1 - c)
        chips = [(1 - x, y), (x, 1 - y), (1 - x, 1 - y)]

        def slot(i, dev):
            return outs[i].at[4 * dev[0] + 2 * dev[1] + dev[2]]

        def copy(i, k, block, to, src=None):
            dst = slot(i, block)
            return _remote(dst if src is None else src, dst, send_sems.at[7 * i + k], recv_sems.at[7 * i + k], to)

        mine = [pltpu.make_async_copy(ins[i], slot(i, me), local_sems.at[i]) for i in range(n)]
        first = []
        for i in range(n):
            first.append(copy(i, 0, me, sibling, src=ins[i]))
            first += [copy(i, 1 + j, me, (*chip, c), src=ins[i]) for j, chip in enumerate(chips)]
        arrive = [[copy(i, 1 + j, (*chip, c), me) for i in range(n)] for j, chip in enumerate(chips)]
        passed = [[copy(i, 4 + j, (*chip, c), sibling) for i in range(n)] for j, chip in enumerate(chips)]
        last = [copy(i, 0, sibling, me) for i in range(n)]
        last += [copy(i, 4 + j, (*chip, 1 - c), me) for i in range(n) for j, chip in enumerate(chips)]
        return mine, first, arrive, passed, last

    def start(*refs):
        mine, first, _, _, _ = plan(*refs)
        for cp in mine + first:
            cp.start()

    def forward(*refs):
        _, _, arrive, passed, _ = plan(*refs)
        for came, onward in zip(arrive, passed):
            for a, p in zip(came, onward):
                a.wait_recv()
                p.start()

    def finish(*refs):
        mine, first, _, passed, last = plan(*refs)
        for cp in last:
            cp.wait_recv()
        for cp in first + [p for onward in passed for p in onward]:
            cp.wait_send()
        for cp in mine:
            cp.wait()

    outs = [jax.ShapeDtypeStruct((N_DEV,) + a.shape, a.dtype) for a in arrs]
    return _side_job(arrs, outs, 7 * n, 7 * n, n, [start, forward, finish])


def _scatter_job(arrs):
    n = len(arrs)

    def plan(ins, outs, send_sems, recv_sems, local_sems):
        x, y, c = lax.axis_index("x"), lax.axis_index("y"), lax.axis_index("c")
        here = 2 * x + y
        local = [pltpu.make_async_copy(ins[i].at[here, c], outs[i].at[here, c], local_sems.at[i]) for i in range(n)]
        sends, recvs = [], []
        for i in range(n):
            for k in range(1, N_DEV):
                px = 1 - x if k & 4 else x
                py = 1 - y if k & 2 else y
                pc = 1 - c if k & 1 else c
                sems = (send_sems.at[7 * i + k - 1], recv_sems.at[7 * i + k - 1], (px, py, pc))
                sends.append(_remote(ins[i].at[2 * px + py, pc], outs[i].at[here, c], *sems))
                recvs.append(_remote(ins[i].at[2 * px + py, pc], outs[i].at[2 * px + py, pc], *sems))
        return local, sends, recvs

    def start(*refs):
        local, sends, _ = plan(*refs)
        for cp in local + sends:
            cp.start()

    def finish(*refs):
        local, sends, recvs = plan(*refs)
        for cp in recvs:
            cp.wait_recv()
        for cp in sends:
            cp.wait_send()
        for cp in local:
            cp.wait()

    outs = [jax.ShapeDtypeStruct(a.shape, a.dtype) for a in arrs]
    return _side_job(arrs, outs, 7 * n, 7 * n, n, [start, finish])


def _run_job(name, job):
    na, no = len(job["arrs"]), len(job["out_shapes"])

    def body(*refs):
        ins, outs, sems = refs[:na], refs[na:na + no], refs[na + no:]
        for phase in job["phases"]:
            phase(ins, outs, *sems)

    return _hbm_call(name, body, job["arrs"], job["out_shapes"], *job["sems"])


def _allgather_multi(name, arrs):
    return _run_job(name, _gather_job(arrs))


def _sibling_swap_multi(name, arrs):
    n = len(arrs)
    per = 4

    def body(*refs):
        ins, got = refs[:n], refs[n:2 * n]
        send_sems, recv_sems, _ = refs[2 * n:]
        x, y, c = lax.axis_index("x"), lax.axis_index("y"), lax.axis_index("c")
        sibling = (x, y, 1 - c)
        sends = []
        for i in range(n):
            for a in range(4):
                k = per * i + a
                sends.append(_remote(ins[i].at[a, 1 - c], got[i].at[a], send_sems.at[k], recv_sems.at[k], sibling))
        for cp in sends:
            cp.start()
        for cp in sends:
            cp.wait_recv()
        for cp in sends:
            cp.wait_send()

    outs = [jax.ShapeDtypeStruct((4,) + a.shape[2:], a.dtype) for a in arrs]
    return _hbm_call(name, body, arrs, outs, per * n, per * n, 1)


def _chip_all_to_all_multi(name, arrs):
    n = len(arrs)

    def body(*refs):
        ins, outs = refs[:n], refs[n:2 * n]
        send_sems, recv_sems, local_sems = refs[2 * n:]
        x, y, c = lax.axis_index("x"), lax.axis_index("y"), lax.axis_index("c")
        mine = 2 * x + y
        chips = [(1 - x, y), (x, 1 - y), (1 - x, 1 - y)]
        local = [pltpu.make_async_copy(ins[i].at[mine], outs[i].at[mine], local_sems.at[i]) for i in range(n)]
        for cp in local:
            cp.start()
        sends, recvs = [], []
        for i in range(n):
            for j, (px, py) in enumerate(chips):
                peer = 2 * px + py
                sems = (send_sems.at[3 * i + j], recv_sems.at[3 * i + j], (px, py, c))
                sends.append(_remote(ins[i].at[peer], outs[i].at[mine], *sems))
                recvs.append(_remote(ins[i].at[peer], outs[i].at[peer], *sems))
        for cp in sends:
            cp.start()
        for cp in recvs:
            cp.wait_recv()
        for cp in sends:
            cp.wait_send()
        for cp in local:
            cp.wait()

    outs = [jax.ShapeDtypeStruct(a.shape, a.dtype) for a in arrs]
    return _hbm_call(name, body, arrs, outs, 3 * n, 3 * n, n)


def _as_rows(a, lead):
    return a.reshape(a.shape[:lead] + (-1, a.shape[-1]))


def _row_tile(rows, cols, parts):
    budget = 4 * 1024 * 1024 // (4 * max(cols, LANES) * parts)
    return _tile_rows(rows, max(8, min(512, budget // 8 * 8)))


def _pair_add(name, core, both, got):
    _, rows, cols = got.shape
    tr = _row_tile(rows, cols, 2)

    def body(c_ref, a_ref, b_ref, o_ref):
        o_ref[...] = (a_ref[...] + b_ref[...]).astype(o_ref.dtype)

    blk = pl.BlockSpec((1, tr, cols), lambda ch, i, c_ref: (ch, i, 0))
    return pl.pallas_call(
        body, name=name,
        grid_spec=pltpu.PrefetchScalarGridSpec(
            num_scalar_prefetch=1, grid=(4, rows // tr),
            in_specs=[pl.BlockSpec((1, None, tr, cols), lambda ch, i, c_ref: (ch, c_ref[0], i, 0)), blk],
            out_specs=blk),
        out_shape=jax.ShapeDtypeStruct(got.shape, BF16),
        compiler_params=pltpu.CompilerParams(dimension_semantics=("parallel", "parallel"),
                                             vmem_limit_bytes=VMEM_LIMIT_BYTES),
    )(core, both, got)


def _adamw(name, gparts, w, m, v):
    layers = len(gparts)
    _, rows, cols = gparts[0].shape
    tr = _row_tile(rows, cols, sum(gp.shape[0] for gp in gparts))
    c1 = 1.0 / (1.0 - ADAM_B1 ** ADAM_STEP)
    c2 = 1.0 / (1.0 - ADAM_B2 ** ADAM_STEP)

    def body(*refs):
        gp_refs = refs[:layers]
        w_ref, m_ref, v_ref, g_ref, d_ref, nm_ref, nv_ref = refs[layers:]
        layer = pl.program_id(0)
        g = None
        for k, gp_ref in enumerate(gp_refs):
            gk = gp_ref[0].astype(F32)
            for i in range(1, gp_ref.shape[0]):
                gk = gk + gp_ref[i].astype(F32)
            g = gk if g is None else jnp.where(layer == k, gk, g)
        nm = ADAM_B1 * m_ref[...] + (1.0 - ADAM_B1) * g
        nv = ADAM_B2 * v_ref[...] + (1.0 - ADAM_B2) * (g * g)
        m_hat = nm * c1
        v_hat = nv * c2
        g_ref[...] = g
        nm_ref[...] = nm
        nv_ref[...] = nv
        d_ref[...] = -ADAM_LR * (m_hat / (jnp.sqrt(v_hat) + ADAM_EPS) + ADAM_WD * w_ref[...])

    row = pl.BlockSpec((None, tr, cols), lambda l, i: (l, i, 0))
    return pl.pallas_call(
        body,
        name=name,
        grid=(layers, rows // tr),
        in_specs=[pl.BlockSpec((gp.shape[0], tr, cols), lambda l, i: (0, i, 0)) for gp in gparts] + [row, row, row],
        out_specs=[row] * 4,
        out_shape=[jax.ShapeDtypeStruct((layers, rows, cols), F32)] * 4,
        compiler_params=pltpu.CompilerParams(dimension_semantics=("parallel", "parallel"),
                                             vmem_limit_bytes=VMEM_LIMIT_BYTES),
    )(*gparts, w, m, v)


def _tile_rows(rows, pref):
    t = min(pref, rows) // 8 * 8
    while t >= 8 and rows % t:
        t -= 8
    return t if t >= 8 else rows


PACK_ROWS = 512


def _pack(arrs, dtype):
    flat = jnp.concatenate([a.astype(dtype).reshape(-1) for a in arrs])
    quantum = PACK_ROWS * LANES
    padded = -(-flat.shape[0] // quantum) * quantum
    return jnp.pad(flat, (0, padded - flat.shape[0])).reshape(-1, LANES)


def _unpack(buf, shapes, lead=()):
    flat = buf.reshape(lead + (-1,))
    out, off = [], 0
    for shp in shapes:
        n = math.prod(shp)
        out.append(flat[..., off:off + n].reshape(lead + tuple(shp)))
        off += n
    return out


def _dest_pieces(name, g):
    if name == "w_branch":
        return jnp.moveaxis(g.reshape(3, 4, 2, D_MODEL // N_DEV, BRANCH), 0, 2)
    if name in SHARDED_F32_GATHER:
        return jnp.moveaxis(g.reshape(g.shape[0], 4, 2, -1), 0, 2)
    return g.reshape((4, 2, g.shape[0] // N_DEV) + g.shape[1:])


HALO = 8


def _rows_down(x, prev, k):
    xs = pltpu.roll(x, k, 0)
    row = lax.broadcasted_iota(jnp.int32, prev.shape, 0)
    top = jnp.where(row < k, pltpu.roll(prev, k, 0), xs[:HALO])
    return jnp.concatenate([top, xs[HALO:]], axis=0)


def _rows_up(x, nxt, k):
    rows = x.shape[0]
    xs = pltpu.roll(x, rows - k, 0)
    row = lax.broadcasted_iota(jnp.int32, nxt.shape, 0)
    bottom = jnp.where(row >= HALO - k, pltpu.roll(nxt, HALO - k, 0), xs[rows - HALO:])
    return jnp.concatenate([xs[:rows - HALO], bottom], axis=0)


def _halo_before(T, block_of, col=0):
    per = T // HALO
    return pl.BlockSpec((HALO, BRANCH), lambda t: (jnp.maximum(block_of(t) * per - 1, 0), col))


def _halo_after(T, block_of, S, col=0):
    per = T // HALO
    return pl.BlockSpec((HALO, BRANCH), lambda t: (jnp.minimum((block_of(t) + 1) * per, S // HALO - 1), col))


def _lru_fwd(u, lw, S):
    T = min(256, S)
    nb = S // T
    row = pl.BlockSpec((T, BRANCH), lambda t: (t, 0))
    vecs = [lw["cw0"], lw["cw1"], lw["cw2"], lw["cw3"], lw["conv_b"], lw["wa"], lw["wx"], lw["ba"], lw["bx"],
            lw["lam"]]

    def body(ax, ax_before, ay, cw0, cw1, cw2, cw3, cb, wa, wx, ba, bx, lam, xc_o, r_o, i_o, a_o, h_o, ya_o, hc):
        t = pl.program_id(0)

        @pl.when(t == 0)
        def _():
            hc[...] = jnp.zeros_like(hc)

        x = ax[...]
        before = jnp.where(t == 0, 0.0, ax_before[...])
        xc = (cw3[...] * x + cw2[...] * _rows_down(x, before, 1) + cw1[...] * _rows_down(x, before, 2)
              + cw0[...] * _rows_down(x, before, 3) + cb[...])
        r = _sigmoid(_dot(xc, wa[...], "nn") + ba[...])
        gi = _sigmoid(_dot(xc, wx[...], "nn") + bx[...])
        sp = _softplus(-lam[...])
        la = -LRU_C * r * sp
        a = jnp.exp(la)
        mult = jnp.sqrt(_neg_expm1(2.0 * la))
        A, B = _scan_rows(a, mult * gi * xc, T)
        h = B + A * hc[...]
        h_o[...] = h
        hc[...] = h_o[pl.ds(T - 1, 1), :]
        xc_o[...] = xc
        r_o[...] = r
        i_o[...] = gi
        a_o[...] = a
        gy, _ = _gelu_and_grad(ay[...])
        ya_o[...] = (gy * h).astype(ya_o.dtype)

    outs = pl.pallas_call(
        body,
        name="lru_fwd",
        grid=(nb,),
        in_specs=[pl.BlockSpec((T, BRANCH), lambda t: (t, U_AX // BRANCH)),
                  _halo_before(T, lambda t: t, U_AX // BRANCH),
                  pl.BlockSpec((T, BRANCH), lambda t: (t, U_AY // BRANCH))] + [_full_spec(v) for v in vecs],
        out_specs=[row] * 6,
        out_shape=[jax.ShapeDtypeStruct((S, BRANCH), F32)] * 5 + [jax.ShapeDtypeStruct((S, BRANCH), BF16)],
        scratch_shapes=[pltpu.VMEM((1, BRANCH), F32)],
        compiler_params=pltpu.CompilerParams(dimension_semantics=("arbitrary",), vmem_limit_bytes=VMEM_LIMIT_BYTES),
    )(u, u, u, *vecs)
    return outs


def _lru_bwd(dya, u, sv, lw, S):
    T = min(256, S)
    nb = S // T
    rrow = pl.BlockSpec((T, BRANCH), lambda t: (nb - 1 - t, 0))
    sq = pl.BlockSpec((BRANCH, BRANCH), lambda t: (0, 0))
    vrow = pl.BlockSpec((1, BRANCH), lambda t: (0, 0))

    def block(t):
        return nb - 1 - t

    def body(dya_r, ay, h, h_before, xc_r, r_r, i_r, a_r, a_after, wa, wx, lam,
             day_o, dxc_o, dwa_o, dwx_o, dba_o, dbx_o, dlam_o, lcar, tmp):
        t = pl.program_id(0)
        h_prev = _rows_down(h[...], jnp.where(t == nb - 1, 0.0, h_before[...]), 1)
        a_next = _rows_up(a_r[...], jnp.where(t == 0, 0.0, a_after[...]), 1)

        @pl.when(t == 0)
        def _():
            lcar[...] = jnp.zeros_like(lcar)
            dwa_o[...] = jnp.zeros_like(dwa_o)
            dwx_o[...] = jnp.zeros_like(dwx_o)
            dba_o[...] = jnp.zeros_like(dba_o)
            dbx_o[...] = jnp.zeros_like(dbx_o)
            dlam_o[...] = jnp.zeros_like(dlam_o)

        gy, dgy = _gelu_and_grad(ay[...])
        dy = dya_r[...]
        day_o[...] = (dy * h[...] * dgy).astype(day_o.dtype)
        A, B = _scan_rows(a_next, dy * gy, T, reverse=True)
        lmb = B + A * lcar[...]
        tmp[...] = lmb
        lcar[...] = tmp[pl.ds(0, 1), :]
        xc, r, gi, a = xc_r[...], r_r[...], i_r[...], a_r[...]
        sp = _softplus(-lam[...])
        la = -LRU_C * r * sp
        mult = jnp.sqrt(_neg_expm1(2.0 * la))
        da = lmb * h_prev
        dmult = lmb * gi * xc
        di = lmb * mult * xc
        dxc = lmb * mult * gi
        dla = da * a - dmult * a * a / mult
        dr = dla * (-LRU_C * sp)
        dlam_o[...] += _colsum(dla * (LRU_C * r)) * _sigmoid(-lam[...])
        dpr = dr * r * (1.0 - r)
        dpi = di * gi * (1.0 - gi)
        dba_o[...] += _colsum(dpr)
        dbx_o[...] += _colsum(dpi)
        dxc_o[...] = dxc + _dot(dpr, wa[...], "nt") + _dot(dpi, wx[...], "nt")
        dwa_o[...] += _dot(xc, dpr, "tn")
        dwx_o[...] += _dot(xc, dpi, "tn")

    outs = pl.pallas_call(
        body,
        name="lru_bwd",
        grid=(nb,),
        in_specs=[rrow, pl.BlockSpec((T, BRANCH), lambda t: (nb - 1 - t, U_AY // BRANCH)), rrow,
                  _halo_before(T, block), rrow, rrow, rrow, rrow, _halo_after(T, block, S), sq, sq, vrow],
        out_specs=[rrow, rrow, sq, sq, vrow, vrow, vrow],
        out_shape=[jax.ShapeDtypeStruct((S, BRANCH), BF16), jax.ShapeDtypeStruct((S, BRANCH), F32),
                   jax.ShapeDtypeStruct((BRANCH, BRANCH), F32), jax.ShapeDtypeStruct((BRANCH, BRANCH), F32),
                   jax.ShapeDtypeStruct((1, BRANCH), F32), jax.ShapeDtypeStruct((1, BRANCH), F32),
                   jax.ShapeDtypeStruct((1, BRANCH), F32)],
        scratch_shapes=[pltpu.VMEM((1, BRANCH), F32), pltpu.VMEM((T, BRANCH), F32)],
        compiler_params=pltpu.CompilerParams(dimension_semantics=("arbitrary",), vmem_limit_bytes=VMEM_LIMIT_BYTES),
    )(dya, u, sv["h"], sv["h"], sv["xc"], sv["r"], sv["i"], sv["a"], sv["a"], lw["wa"], lw["wx"], lw["lam"])
    return outs


def _conv_bwd(dxc, u, lw, S):
    T = min(256, S)
    nb = S // T
    vecs = [lw["cw0"], lw["cw1"], lw["cw2"], lw["cw3"]]
    vrow = pl.BlockSpec((1, BRANCH), lambda t: (0, 0))

    def body(d_r, d_after, ax, ax_before, cw0, cw1, cw2, cw3, dax_o, dcw0_o, dcw1_o, dcw2_o, dcw3_o, dcb_o):
        t = pl.program_id(0)
        d = d_r[...]
        after = jnp.where(t == nb - 1, 0.0, d_after[...])
        x = ax[...]
        before = jnp.where(t == 0, 0.0, ax_before[...])
        dax = (cw3[...] * d + cw2[...] * _rows_up(d, after, 1) + cw1[...] * _rows_up(d, after, 2)
               + cw0[...] * _rows_up(d, after, 3))
        dax_o[...] = dax.astype(dax_o.dtype)
        sums = [_colsum(d * _rows_down(x, before, 3)), _colsum(d * _rows_down(x, before, 2)),
                _colsum(d * _rows_down(x, before, 1)), _colsum(d * x), _colsum(d)]
        outs = [dcw0_o, dcw1_o, dcw2_o, dcw3_o, dcb_o]

        @pl.when(t == 0)
        def _():
            for o, val in zip(outs, sums):
                o[...] = val

        @pl.when(t > 0)
        def _():
            for o, val in zip(outs, sums):
                o[...] += val

    res = pl.pallas_call(
        body,
        name="conv_bwd",
        grid=(nb,),
        in_specs=[pl.BlockSpec((T, BRANCH), lambda t: (t, 0)), _halo_after(T, lambda t: t, S),
                  pl.BlockSpec((T, BRANCH), lambda t: (t, U_AX // BRANCH)),
                  _halo_before(T, lambda t: t, U_AX // BRANCH)] + [_full_spec(v) for v in vecs],
        out_specs=[pl.BlockSpec((T, BRANCH), lambda t: (t, 0))] + [vrow] * 5,
        out_shape=[jax.ShapeDtypeStruct((S, BRANCH), BF16)] + [jax.ShapeDtypeStruct((1, BRANCH), F32)] * 5,
        compiler_params=pltpu.CompilerParams(dimension_semantics=("arbitrary",), vmem_limit_bytes=VMEM_LIMIT_BYTES),
    )(dxc, dxc, u, u, *vecs)
    return res[0], res[1:]


GLA_QK = GLA_HEADS * GLA_DK
GLA_V = GLA_HEADS * GLA_DV
GLA_SCALE = GLA_DK ** -0.5


def _gla_specs(TB, rev_nb=None):
    def rmap(t):
        return t if rev_nb is None else rev_nb - 1 - t
    return [
        pl.BlockSpec((TB, GLA_QK), lambda t: (rmap(t), U_BQ // GLA_QK)),
        pl.BlockSpec((TB, GLA_QK), lambda t: (rmap(t), U_BK // GLA_QK)),
        pl.BlockSpec((TB, GLA_V), lambda t: (rmap(t), U_BV // GLA_V)),
        pl.BlockSpec((TB, GLA_V), lambda t: (rmap(t), U_BR // GLA_V)),
        pl.BlockSpec((TB, LANES), lambda t: (rmap(t), U_BLOW // LANES)),
    ]


def _gla_gates(gl, wg2, bg, TB):
    pre = _dot(gl, wg2, "nn") + bg
    la = _log_sigmoid(pre) * (1.0 / GLA_TAU)
    _, gc = _scan_rows(None, la, TB, seg=CHUNK)
    return pre, la, gc


def _gla_fwd(u, gw, S):
    TB = min(512, S)
    nb = S // TB
    cpb = TB // CHUNK
    vecs = [gw["wg2"], gw["bg"], gw["ng"], gw["bd"]]

    def body(q_r, k_r, v_r, br_r, gl_r, wg2, bg, ng, bd, yb_o, oraw_o, st_o, st):
        t = pl.program_id(0)

        @pl.when(t == 0)
        def _():
            st[...] = jnp.zeros_like(st)

        _, la, gc = _gla_gates(gl_r[...], wg2[...], bg[...], TB)
        for c in range(cpb):
            sl = slice(c * CHUNK, (c + 1) * CHUNK)
            gt = _colsum(la[sl])
            kdec = k_r[sl, :] * jnp.exp(gt - gc[sl])
            d_t = _dot(v_r[sl, :], kdec, "tn") * bd[...]
            s_new = st[...] * jnp.exp(gt) + d_t
            st[...] = s_new
            st_o[c] = s_new
            oraw_o[sl, :] = _dot(q_r[sl, :] * GLA_SCALE, s_new, "nt")
        for h in range(GLA_HEADS):
            hs = slice(h * GLA_DV, (h + 1) * GLA_DV)
            oh = oraw_o[:, hs]
            on = oh * lax.rsqrt(jnp.mean(oh * oh, axis=-1, keepdims=True) + RMS_EPS)
            sil, _ = _silu_and_grad(br_r[:, hs])
            yb_o[:, hs] = (on * ng[:, hs] * sil).astype(yb_o.dtype)

    return pl.pallas_call(
        body,
        name="gla_fwd",
        grid=(nb,),
        in_specs=_gla_specs(TB) + [_full_spec(v) for v in vecs],
        out_specs=[pl.BlockSpec((TB, GLA_V), lambda t: (t, 0)), pl.BlockSpec((TB, GLA_V), lambda t: (t, 0)),
                   pl.BlockSpec((cpb, GLA_V, GLA_QK), lambda t: (t, 0, 0))],
        out_shape=[jax.ShapeDtypeStruct((S, GLA_V), BF16), jax.ShapeDtypeStruct((S, GLA_V), F32),
                   jax.ShapeDtypeStruct((S // CHUNK, GLA_V, GLA_QK), F32)],
        scratch_shapes=[pltpu.VMEM((GLA_V, GLA_QK), F32)],
        compiler_params=pltpu.CompilerParams(dimension_semantics=("arbitrary",), vmem_limit_bytes=VMEM_LIMIT_BYTES),
    )(u, u, u, u, u, *vecs)


def _gla_bwd(dyb, u, oraw, states, gw, S):
    TB = min(512, S)
    nb = S // TB
    cpb = TB // CHUNK
    vecs = [gw["wg2"], gw["bg"], gw["ng"], gw["bd"]]

    def rrow(width):
        return pl.BlockSpec((TB, width), lambda t: (nb - 1 - t, 0))

    def body(dyb_r, oraw_r, q_r, k_r, v_r, br_r, gl_r, st_r, sp_r, wg2, bg, ng, bd,
             dq_o, dk_o, dv_o, dbr_o, dgl_o, dwg2_o, dbg_o, dng_o, dcar, do_buf, dla_buf):
        t = pl.program_id(0)
        blk = nb - 1 - t

        @pl.when(t == 0)
        def _():
            dcar[...] = jnp.zeros_like(dcar)
            dwg2_o[...] = jnp.zeros_like(dwg2_o)
            dbg_o[...] = jnp.zeros_like(dbg_o)
            dng_o[...] = jnp.zeros_like(dng_o)

        pre, la, gc = _gla_gates(gl_r[...], wg2[...], bg[...], TB)
        for h in range(GLA_HEADS):
            hs = slice(h * GLA_DV, (h + 1) * GLA_DV)
            oh = oraw_r[:, hs]
            rs = lax.rsqrt(jnp.mean(oh * oh, axis=-1, keepdims=True) + RMS_EPS)
            on = oh * rs
            sil, dsil = _silu_and_grad(br_r[:, hs])
            dy = dyb_r[:, hs]
            dbr_o[:, hs] = (dy * on * ng[:, hs] * dsil).astype(dbr_o.dtype)
            don = dy * ng[:, hs] * sil
            dng_o[:, hs] += _colsum(dy * on * sil)
            do_buf[:, hs] = rs * (don - on * jnp.mean(don * on, axis=-1, keepdims=True))
        first = jnp.where(blk == 0, 0.0, 1.0)
        for c in reversed(range(cpb)):
            sl = slice(c * CHUNK, (c + 1) * CHUNK)
            s_n = st_r[c]
            s_prev = st_r[c - 1] if c > 0 else sp_r[0] * first
            gt = _colsum(la[sl])
            w = jnp.exp(gt - gc[sl])
            k_c = k_r[sl, :]
            kdec = k_c * w
            qs = q_r[sl, :] * GLA_SCALE
            do_c = do_buf[sl, :]
            dq_o[sl, :] = (_dot(do_c, s_n, "nn") * GLA_SCALE).astype(dq_o.dtype)
            d_n = _dot(do_c, qs, "tn") * bd[...] + dcar[...]
            dv_o[sl, :] = _dot(kdec, d_n, "nt").astype(dv_o.dtype)
            dkdec = _dot(v_r[sl, :], d_n, "nn")
            dk_o[sl, :] = (dkdec * w).astype(dk_o.dtype)
            tt = dkdec * kdec
            e = jnp.exp(gt)
            dgt = _colsum(tt) + _colsum(d_n * s_prev) * e
            _, rc = _scan_rows(None, -tt, CHUNK, reverse=True)
            dla_buf[sl, :] = rc + dgt
            dcar[...] = d_n * e
        dpre = dla_buf[...] * _sigmoid(-pre) * (1.0 / GLA_TAU)
        dbg_o[...] += _colsum(dpre)
        dgl_o[...] = _dot(dpre, wg2[...], "nt").astype(dgl_o.dtype)
        dwg2_o[...] += _dot(gl_r[...], dpre, "tn")

    return pl.pallas_call(
        body,
        name="gla_bwd",
        grid=(nb,),
        in_specs=[rrow(GLA_V), rrow(GLA_V)] + _gla_specs(TB, rev_nb=nb)
        + [pl.BlockSpec((cpb, GLA_V, GLA_QK), lambda t: (nb - 1 - t, 0, 0)),
           pl.BlockSpec((1, GLA_V, GLA_QK), lambda t: (jnp.maximum((nb - 1 - t) * cpb - 1, 0), 0, 0))]
        + [_full_spec(v) for v in vecs],
        out_specs=[rrow(GLA_QK), rrow(GLA_QK), rrow(GLA_V), rrow(GLA_V), rrow(LANES),
                   pl.BlockSpec((LANES, GLA_QK), lambda t: (0, 0)), pl.BlockSpec((1, GLA_QK), lambda t: (0, 0)),
                   pl.BlockSpec((1, GLA_V), lambda t: (0, 0))],
        out_shape=[jax.ShapeDtypeStruct((S, GLA_QK), BF16), jax.ShapeDtypeStruct((S, GLA_QK), BF16),
                   jax.ShapeDtypeStruct((S, GLA_V), BF16), jax.ShapeDtypeStruct((S, GLA_V), BF16),
                   jax.ShapeDtypeStruct((S, LANES), BF16), jax.ShapeDtypeStruct((LANES, GLA_QK), F32),
                   jax.ShapeDtypeStruct((1, GLA_QK), F32), jax.ShapeDtypeStruct((1, GLA_V), F32)],
        scratch_shapes=[pltpu.VMEM((GLA_V, GLA_QK), F32), pltpu.VMEM((TB, GLA_V), F32),
                        pltpu.VMEM((TB, GLA_QK), F32)],
        compiler_params=pltpu.CompilerParams(dimension_semantics=("arbitrary",), vmem_limit_bytes=VMEM_LIMIT_BYTES),
    )(dyb, oraw, u, u, u, u, u, states, states, *vecs)


FOX_SCALE = FOX_DH ** -0.5


def _fox_gate_fwd(u, bfp, S):
    T = min(512, S)

    def body(f_r, b_r, fc_o, car):
        t = pl.program_id(0)

        @pl.when(t == 0)
        def _():
            car[...] = jnp.zeros_like(car)

        _, cs = _scan_rows(None, _log_sigmoid(f_r[...] + b_r[...]), T)
        fc_o[...] = cs + car[...]
        car[...] = fc_o[pl.ds(T - 1, 1), :]

    return pl.pallas_call(
        body,
        name="fox_gate_fwd",
        grid=(S // T,),
        in_specs=[pl.BlockSpec((T, LANES), lambda t: (t, U_CF // LANES)), _full_spec(bfp)],
        out_specs=pl.BlockSpec((T, LANES), lambda t: (t, 0)),
        out_shape=jax.ShapeDtypeStruct((S, LANES), F32),
        scratch_shapes=[pltpu.VMEM((1, LANES), F32)],
        compiler_params=pltpu.CompilerParams(dimension_semantics=("arbitrary",), vmem_limit_bytes=VMEM_LIMIT_BYTES),
    )(u, bfp)


def _fox_gate_bwd(dfc, u, bfp, S):
    T = min(512, S)
    nb = S // T

    def body(d_r, f_r, b_r, df_o, db_o, car, tmp):
        t = pl.program_id(0)

        @pl.when(t == 0)
        def _():
            car[...] = jnp.zeros_like(car)
            db_o[...] = jnp.zeros_like(db_o)

        _, rc = _scan_rows(None, d_r[...], T, reverse=True)
        tmp[...] = rc + car[...]
        car[...] = tmp[pl.ds(0, 1), :]
        df = tmp[...] * _sigmoid(-(f_r[...] + b_r[...]))
        df_o[...] = df.astype(df_o.dtype)
        db_o[...] += _colsum(df)

    return pl.pallas_call(
        body,
        name="fox_gate_bwd",
        grid=(nb,),
        in_specs=[pl.BlockSpec((T, LANES), lambda t: (nb - 1 - t, 0)),
                  pl.BlockSpec((T, LANES), lambda t: (nb - 1 - t, U_CF // LANES)), _full_spec(bfp)],
        out_specs=[pl.BlockSpec((T, LANES), lambda t: (nb - 1 - t, 0)), pl.BlockSpec((1, LANES), lambda t: (0, 0))],
        out_shape=[jax.ShapeDtypeStruct((S, LANES), BF16), jax.ShapeDtypeStruct((1, LANES), F32)],
        scratch_shapes=[pltpu.VMEM((1, LANES), F32), pltpu.VMEM((T, LANES), F32)],
        compiler_params=pltpu.CompilerParams(dimension_semantics=("arbitrary",), vmem_limit_bytes=VMEM_LIMIT_BYTES),
    )(dfc, u, bfp)


def _fox_scores(q, k, fq, fk, qi, ki, tq, tk):
    s = _dot(q, k, "nt") * FOX_SCALE + (fq - fk)
    rows = lax.broadcasted_iota(jnp.int32, (tq, tk), 0) + qi * tq
    cols = lax.broadcasted_iota(jnp.int32, (tq, tk), 1) + ki * tk
    return jnp.where(cols <= rows, s, NEG_BIG)


def _fox_fwd(qh, kh, vh, fq, fk, S):
    tq = tk = min(512, S)
    nq, nk = S // tq, S // tk

    def body(q_r, k_r, v_r, fq_r, fk_r, o_o, lse_o, m_s, l_s, acc):
        qi, ki = pl.program_id(1), pl.program_id(2)

        @pl.when(ki == 0)
        def _():
            m_s[...] = jnp.full_like(m_s, NEG_BIG)
            l_s[...] = jnp.zeros_like(l_s)
            acc[...] = jnp.zeros_like(acc)

        @pl.when(ki <= qi)
        def _():
            s = _fox_scores(q_r[0], k_r[0], fq_r[0], fk_r[0], qi, ki, tq, tk)
            m_new = jnp.maximum(m_s[...], jnp.max(s, axis=-1, keepdims=True))
            p = jnp.exp(s - m_new)
            alpha = jnp.exp(m_s[...] - m_new)
            l_s[...] = alpha * l_s[...] + jnp.sum(p, axis=-1, keepdims=True)
            acc[...] = alpha * acc[...] + _dot(p, v_r[0], "nn")
            m_s[...] = m_new

        @pl.when(ki == nk - 1)
        def _():
            o_o[0] = acc[...] / l_s[...]
            lse_o[0] = m_s[...] + jnp.log(l_s[...])

    kv = pl.BlockSpec((1, tk, FOX_DH), lambda h, i, j: (h, jnp.minimum(j, i), 0))
    return pl.pallas_call(
        body,
        name="fox_fwd",
        grid=(FOX_HEADS, nq, nk),
        in_specs=[pl.BlockSpec((1, tq, FOX_DH), lambda h, i, j: (h, i, 0)), kv, kv,
                  pl.BlockSpec((1, tq, 1), lambda h, i, j: (h, i, 0)),
                  pl.BlockSpec((1, 1, tk), lambda h, i, j: (h, 0, jnp.minimum(j, i)))],
        out_specs=[pl.BlockSpec((1, tq, FOX_DH), lambda h, i, j: (h, i, 0)),
                   pl.BlockSpec((1, tq, 1), lambda h, i, j: (h, i, 0))],
        out_shape=[jax.ShapeDtypeStruct((FOX_HEADS, S, FOX_DH), F32), jax.ShapeDtypeStruct((FOX_HEADS, S, 1), F32)],
        scratch_shapes=[pltpu.VMEM((tq, 1), F32), pltpu.VMEM((tq, 1), F32), pltpu.VMEM((tq, FOX_DH), F32)],
        compiler_params=pltpu.CompilerParams(
            dimension_semantics=("parallel", "parallel", "arbitrary"), vmem_limit_bytes=VMEM_LIMIT_BYTES),
    )(qh, kh, vh, fq, fk)


def _fox_bwd_dq(qh, kh, vh, fq, fk, o, do, lse, S):
    tq = tk = min(512, S)
    nq, nk = S // tq, S // tk

    def body(q_r, k_r, v_r, fq_r, fk_r, o_r, do_r, lse_r, dq_o, dfq_o, dq_acc, df_acc):
        qi, ki = pl.program_id(1), pl.program_id(2)

        @pl.when(ki == 0)
        def _():
            dq_acc[...] = jnp.zeros_like(dq_acc)
            df_acc[...] = jnp.zeros_like(df_acc)

        @pl.when(ki <= qi)
        def _():
            s = _fox_scores(q_r[0], k_r[0], fq_r[0], fk_r[0], qi, ki, tq, tk)
            p = jnp.exp(s - lse_r[0])
            do_t = do_r[0]
            delta = jnp.sum(o_r[0] * do_t, axis=-1, keepdims=True)
            ds = p * (_dot(do_t, v_r[0], "nt") - delta)
            dq_acc[...] += _dot(ds, k_r[0], "nn")
            df_acc[...] += jnp.sum(ds, axis=-1, keepdims=True)

        @pl.when(ki == nk - 1)
        def _():
            dq_o[0] = dq_acc[...] * FOX_SCALE
            dfq_o[0] = df_acc[...]

    qrow = pl.BlockSpec((1, tq, FOX_DH), lambda h, i, j: (h, i, 0))
    qcol = pl.BlockSpec((1, tq, 1), lambda h, i, j: (h, i, 0))
    kv = pl.BlockSpec((1, tk, FOX_DH), lambda h, i, j: (h, jnp.minimum(j, i), 0))
    return pl.pallas_call(
        body,
        name="fox_bwd_dq",
        grid=(FOX_HEADS, nq, nk),
        in_specs=[qrow, kv, kv, qcol, pl.BlockSpec((1, 1, tk), lambda h, i, j: (h, 0, jnp.minimum(j, i))),
                  qrow, qrow, qcol],
        out_specs=[qrow, qcol],
        out_shape=[jax.ShapeDtypeStruct((FOX_HEADS, S, FOX_DH), F32), jax.ShapeDtypeStruct((FOX_HEADS, S, 1), F32)],
        scratch_shapes=[pltpu.VMEM((tq, FOX_DH), F32), pltpu.VMEM((tq, 1), F32)],
        compiler_params=pltpu.CompilerParams(
            dimension_semantics=("parallel", "parallel", "arbitrary"), vmem_limit_bytes=VMEM_LIMIT_BYTES),
    )(qh, kh, vh, fq, fk, o, do, lse)


def _fox_bwd_dkv(qh, kh, vh, fq, fk, o, do, lse, S):
    tq = tk = min(512, S)
    nq, nk = S // tq, S // tk

    def body(q_r, k_r, v_r, fq_r, fk_r, o_r, do_r, lse_r, dk_o, dv_o, dfk_o, dk_acc, dv_acc, df_acc):
        ki, qi = pl.program_id(1), pl.program_id(2)

        @pl.when(qi == 0)
        def _():
            dk_acc[...] = jnp.zeros_like(dk_acc)
            dv_acc[...] = jnp.zeros_like(dv_acc)
            df_acc[...] = jnp.zeros_like(df_acc)

        @pl.when(qi >= ki)
        def _():
            s = _fox_scores(q_r[0], k_r[0], fq_r[0], fk_r[0], qi, ki, tq, tk)
            p = jnp.exp(s - lse_r[0])
            do_t = do_r[0]
            delta = jnp.sum(o_r[0] * do_t, axis=-1, keepdims=True)
            ds = p * (_dot(do_t, v_r[0], "nt") - delta)
            dv_acc[...] += _dot(p, do_t, "tn")
            dk_acc[...] += _dot(ds, q_r[0], "tn")
            df_acc[...] += _colsum(ds)

        @pl.when(qi == nq - 1)
        def _():
            dk_o[0] = dk_acc[...] * FOX_SCALE
            dv_o[0] = dv_acc[...]
            dfk_o[0] = df_acc[...]

    qrow = pl.BlockSpec((1, tq, FOX_DH), lambda h, j, i: (h, jnp.maximum(i, j), 0))
    qcol = pl.BlockSpec((1, tq, 1), lambda h, j, i: (h, jnp.maximum(i, j), 0))
    kv = pl.BlockSpec((1, tk, FOX_DH), lambda h, j, i: (h, j, 0))
    krow = pl.BlockSpec((1, 1, tk), lambda h, j, i: (h, 0, j))
    return pl.pallas_call(
        body,
        name="fox_bwd_dkv",
        grid=(FOX_HEADS, nk, nq),
        in_specs=[qrow, kv, kv, qcol, krow, qrow, qrow, qcol],
        out_specs=[kv, kv, krow],
        out_shape=[jax.ShapeDtypeStruct((FOX_HEADS, S, FOX_DH), F32), jax.ShapeDtypeStruct((FOX_HEADS, S, FOX_DH), F32),
                   jax.ShapeDtypeStruct((FOX_HEADS, 1, S), F32)],
        scratch_shapes=[pltpu.VMEM((tk, FOX_DH), F32), pltpu.VMEM((tk, FOX_DH), F32), pltpu.VMEM((1, tk), F32)],
        compiler_params=pltpu.CompilerParams(
            dimension_semantics=("parallel", "parallel", "arbitrary"), vmem_limit_bytes=VMEM_LIMIT_BYTES),
    )(qh, kh, vh, fq, fk, o, do, lse)


def _fox_call(name, body, tables, grid, in_specs, out_specs, out_shape, scratch, args, side):
    n_in, n_out, n_scr = len(in_specs), len(out_specs), len(scratch)
    semantics = ("parallel", "arbitrary")
    if side is not None:
        total = grid[0] * grid[1]
        phases = side["phases"]
        triggers = [0, total - 1] if len(phases) == 2 else [0, total * 7 // 10, total - 1]
        na, no = len(side["arrs"]), len(side["out_shapes"])
        s_in, s_out, s_sems = _side_specs(side)
        kernel_body = body

        def body(*refs):
            tabs, rest = refs[:len(tables)], refs[len(tables):]
            ins, s_ins = rest[:n_in], rest[n_in:n_in + na]
            rest = rest[n_in + na:]
            outs, s_outs = rest[:n_out], rest[n_out:n_out + no]
            rest = rest[n_out + no:]
            scr, sems = rest[:n_scr], rest[n_scr:]
            flat = pl.program_id(0) * grid[1] + pl.program_id(1)
            for trigger, phase in zip(triggers[:-1], phases[:-1]):
                @pl.when(flat == trigger)
                def _(phase=phase):
                    phase(s_ins, s_outs, *sems)
            kernel_body(*tabs, *ins, *outs, *scr)

            @pl.when(flat == triggers[-1])
            def _():
                phases[-1](s_ins, s_outs, *sems)

        in_specs, out_specs = in_specs + s_in, out_specs + s_out
        out_shape, scratch = out_shape + side["out_shapes"], scratch + s_sems
        args = list(args) + side["arrs"]
        semantics = ("arbitrary", "arbitrary")
    res = pl.pallas_call(
        body,
        name=name,
        grid_spec=pltpu.PrefetchScalarGridSpec(num_scalar_prefetch=len(tables), grid=grid, in_specs=in_specs,
                                               out_specs=out_specs, scratch_shapes=scratch),
        out_shape=out_shape,
        compiler_params=pltpu.CompilerParams(dimension_semantics=semantics, vmem_limit_bytes=VMEM_LIMIT_BYTES),
    )(*tables, *args)
    return res[:n_out], res[n_out:]


FOX_TILE = 1024
FOX_GROUP = 2
FOX_AUG = 128
FOX_ONES = 3


def _fox_pairs(n, by_key):
    pairs = [(qi, ki) for qi in range(n) for ki in range(qi + 1)]
    if by_key:
        pairs.sort(key=lambda qk: (qk[1], qk[0]))
    qs = jnp.asarray([qk[0] for qk in pairs], jnp.int32)
    ks = jnp.asarray([qk[1] for qk in pairs], jnp.int32)
    return qs, ks


def _fox_augment(q, k, fcum):
    S = q.shape[0]
    def to_bf16_grid(a):
        return lax.reduce_precision(a, exponent_bits=8, mantissa_bits=7)

    hi = to_bf16_grid(fcum)
    mid = to_bf16_grid(fcum - hi)
    lo = to_bf16_grid(fcum - hi - mid)
    f3 = jnp.stack([hi, mid, lo], axis=-1).astype(BF16)
    ones = jnp.ones((S, FOX_HEADS, FOX_ONES), BF16)
    pad = jnp.zeros((S, FOX_HEADS, FOX_AUG - FOX_DH - 2 * FOX_ONES), BF16)
    q_aug = jnp.concatenate([(q * FOX_SCALE).astype(BF16), ones, f3, pad], axis=-1)
    k_aug = jnp.concatenate([k.astype(BF16), -f3, ones, pad], axis=-1)
    return jnp.transpose(q_aug, (1, 0, 2)), jnp.transpose(k_aug, (1, 0, 2))


def _fox_causal(sT):
    keys = lax.broadcasted_iota(jnp.int32, sT.shape, 0)
    queries = lax.broadcasted_iota(jnp.int32, sT.shape, 1)
    return jnp.where(keys <= queries, sT, NEG_BIG)


def _fox_fwd(qT, ka, vT, S, side=None):
    t = min(FOX_TILE, S)
    n = S // t
    qi_tab, ki_tab = _fox_pairs(n, by_key=False)

    G = FOX_GROUP

    def body(qi_ref, ki_ref, qT_r, ka_r, vT_r, oT_o, lse_o, m_s, l_s, acc):
        step = pl.program_id(1)
        qi, ki = qi_ref[step], ki_ref[step]

        @pl.when(ki == 0)
        def _():
            m_s[...] = jnp.full_like(m_s, NEG_BIG)
            l_s[...] = jnp.zeros_like(l_s)
            acc[...] = jnp.zeros_like(acc)

        def update(g, masked):
            sT = _dot(ka_r[g], qT_r[g], "nn")
            if masked:
                sT = _fox_causal(sT)
            m_new = jnp.maximum(m_s[g], jnp.max(sT, axis=0, keepdims=True))
            p = jnp.exp(sT - m_new)
            alpha = jnp.exp(m_s[g] - m_new)
            l_s[g] = alpha * l_s[g] + jnp.sum(p, axis=0, keepdims=True)
            acc[g] = alpha * acc[g] + _dot(vT_r[g], p, "nn")
            m_s[g] = m_new

        @pl.when(ki < qi)
        def _():
            for g in range(G):
                update(g, False)

        @pl.when(ki == qi)
        def _():
            for g in range(G):
                update(g, True)
                oT_o[g] = acc[g] / l_s[g]
                lse_o[g] = m_s[g] + jnp.log(l_s[g])

    return _fox_call(
        "fox_fwd", body, (qi_tab, ki_tab), (FOX_HEADS // G, int(qi_tab.shape[0])),
        [pl.BlockSpec((G, FOX_AUG, t), lambda h, s, qt, kt: (h, 0, qt[s])),
         pl.BlockSpec((G, t, FOX_AUG), lambda h, s, qt, kt: (h, kt[s], 0)),
         pl.BlockSpec((G, FOX_DH, t), lambda h, s, qt, kt: (h, 0, kt[s]))],
        [pl.BlockSpec((G, FOX_DH, t), lambda h, s, qt, kt: (h, 0, qt[s])),
         pl.BlockSpec((G, 1, t), lambda h, s, qt, kt: (h, 0, qt[s]))],
        [jax.ShapeDtypeStruct((FOX_HEADS, FOX_DH, S), F32), jax.ShapeDtypeStruct((FOX_HEADS, 1, S), F32)],
        [pltpu.VMEM((G, 1, t), F32), pltpu.VMEM((G, 1, t), F32), pltpu.VMEM((G, FOX_DH, t), F32)],
        (qT, ka, vT), side)


FOX_BIAS_ROWS = 8


def _fox_bwd(qT, qa, ka, kT, v, do, doT, oT, lse, S, side=None):
    t = min(FOX_TILE, S)
    n = S // t
    qi_tab, ki_tab = _fox_pairs(n, by_key=True)
    n_steps = int(qi_tab.shape[0])
    slab = slice(FOX_DH, FOX_DH + FOX_BIAS_ROWS)

    G = FOX_GROUP

    def body(qi_ref, ki_ref, qT_r, qa_r, ka_r, kT_r, v_r, do_r, doT_r, oT_r, lse_r,
             dq_o, dfq_o, dk_o, dfk_o, dv_o, dq_acc, dk_acc, dv_acc):
        step = pl.program_id(1)
        qi, ki = qi_ref[step], ki_ref[step]

        @pl.when(step == 0)
        def _():
            dq_acc[...] = jnp.zeros_like(dq_acc)

        @pl.when(qi == ki)
        def _():
            dk_acc[...] = jnp.zeros_like(dk_acc)
            dv_acc[...] = jnp.zeros_like(dv_acc)

        def update(g, masked):
            sT = _dot(ka_r[g], qT_r[g], "nn")
            if masked:
                sT = _fox_causal(sT)
            pT = jnp.exp(sT - lse_r[g])
            delta = jnp.sum(oT_r[g] * doT_r[g], axis=0, keepdims=True)
            dsT = pT * (_dot(v_r[g], doT_r[g], "nn") - delta)
            dv_acc[g] += _dot(pT, do_r[g], "nn")
            dk_acc[g] += _dot(dsT, qa_r[g], "nn")
            dq_acc[g, qi] += _dot(kT_r[g], dsT, "nn")

        @pl.when(qi > ki)
        def _():
            for g in range(G):
                update(g, False)

        @pl.when(qi == ki)
        def _():
            for g in range(G):
                update(g, True)

        @pl.when(qi == n - 1)
        def _():
            for g in range(G):
                dk = dk_acc[g]
                dk_o[g] = dk[:, :FOX_DH].astype(dk_o.dtype)
                dfk_o[g] = dk.T[slab]
                dv_o[g] = dv_acc[g].astype(dv_o.dtype)

        @pl.when(step == n_steps - 1)
        def _():
            for g in range(G):
                for j in range(n):
                    dqT = dq_acc[g, j]
                    dq_o[g, j * t:(j + 1) * t, :] = (dqT.T[:, :FOX_DH] * FOX_SCALE).astype(dq_o.dtype)
                    dfq_o[g, :, j * t:(j + 1) * t] = dqT[slab]

    def qlane(rows):
        return pl.BlockSpec((G, rows, t), lambda h, s, qt, kt: (h, 0, qt[s]))

    def qrow(cols):
        return pl.BlockSpec((G, t, cols), lambda h, s, qt, kt: (h, qt[s], 0))

    def krow(cols):
        return pl.BlockSpec((G, t, cols), lambda h, s, qt, kt: (h, kt[s], 0))

    def klane(rows):
        return pl.BlockSpec((G, rows, t), lambda h, s, qt, kt: (h, 0, kt[s]))

    def head(rows, cols):
        return pl.BlockSpec((G, rows, cols), lambda h, s, qt, kt: (h, 0, 0))

    return _fox_call(
        "fox_bwd", body, (qi_tab, ki_tab), (FOX_HEADS // G, n_steps),
        [qlane(FOX_AUG), qrow(FOX_AUG), krow(FOX_AUG), klane(FOX_AUG), krow(FOX_DH), qrow(FOX_DH), qlane(FOX_DH),
         qlane(FOX_DH), qlane(1)],
        [head(S, FOX_DH), head(FOX_BIAS_ROWS, S), krow(FOX_DH), klane(FOX_BIAS_ROWS), krow(FOX_DH)],
        [jax.ShapeDtypeStruct((FOX_HEADS, S, FOX_DH), BF16), jax.ShapeDtypeStruct((FOX_HEADS, FOX_BIAS_ROWS, S), F32),
         jax.ShapeDtypeStruct((FOX_HEADS, S, FOX_DH), BF16), jax.ShapeDtypeStruct((FOX_HEADS, FOX_BIAS_ROWS, S), F32),
         jax.ShapeDtypeStruct((FOX_HEADS, S, FOX_DH), BF16)],
        [pltpu.VMEM((G, n, FOX_AUG, t), F32), pltpu.VMEM((G, t, FOX_AUG), F32), pltpu.VMEM((G, t, FOX_DH), F32)],
        (qT, qa, ka, kT, v, do, doT, oT, lse), side)


def _fox_prep(u, fcum, S):
    T = min(512, S)
    head_of = jnp.arange(BRANCH) // FOX_DH
    dim_of = jnp.arange(BRANCH) % FOX_DH
    heads = jnp.arange(FOX_HEADS)[:, None, None]
    sel = (head_of[None, :, None] == heads) & (dim_of[None, :, None] == jnp.arange(FOX_AUG)[None, None, :])
    sel_q = (sel * FOX_SCALE).astype(BF16)
    sel_k = sel.astype(BF16)
    sel_vT = jnp.swapaxes(sel[:, :, :FOX_DH], 1, 2).astype(BF16)
    piece = jnp.arange(FOX_ONES * LANES) // LANES
    lane = jnp.arange(FOX_ONES * LANES) % LANES
    col = jnp.arange(FOX_AUG)[None, None, :]
    at_q = (lane[None, :, None] == heads) & (col == FOX_DH + FOX_ONES + piece[None, :, None])
    at_k = (lane[None, :, None] == heads) & (col == FOX_DH + piece[None, :, None])
    bias_q = at_q.astype(BF16)
    bias_k = (-at_k.astype(F32)).astype(BF16)
    cols = jnp.arange(FOX_AUG)[None, :]
    ones_q = ((cols >= FOX_DH) & (cols < FOX_DH + FOX_ONES)).astype(F32)
    ones_k = ((cols >= FOX_DH + FOX_ONES) & (cols < FOX_DH + 2 * FOX_ONES)).astype(F32)
    consts = [sel_q, sel_k, sel_vT, bias_q, bias_k, ones_q, ones_k]

    def body(cq, ck, cv, fc, sq, sk, svT, bq, bk, oq, ok, qa_o, ka_o, qT_o, kT_o, vh_o, vT_o):
        f = fc[...]
        hi = f.astype(BF16).astype(F32)
        mid = (f - hi).astype(BF16).astype(F32)
        lo = (f - hi - mid).astype(BF16).astype(F32)
        pieces = jnp.concatenate([hi, mid, lo], axis=1)
        q, k, v = cq[...], ck[...], cv[...]
        for h in range(FOX_HEADS):
            qa = _dot(q, sq[h], "nn") + _dot(pieces, bq[h], "nn") + oq[...]
            ka = _dot(k, sk[h], "nn") + _dot(pieces, bk[h], "nn") + ok[...]
            qa_o[h] = qa.astype(qa_o.dtype)
            ka_o[h] = ka.astype(ka_o.dtype)
            qT_o[h] = qa.T.astype(qT_o.dtype)
            kT_o[h] = ka.T.astype(kT_o.dtype)
            vT_o[h] = _dot(svT[h], v, "nt").astype(vT_o.dtype)
            vh_o[h] = _dot(v, svT[h], "nt").astype(vh_o.dtype)

    def win(off):
        return pl.BlockSpec((T, BRANCH), functools.partial(lambda i, blk: (i, blk), blk=off // BRANCH))

    def rows(c):
        return pl.BlockSpec((FOX_HEADS, T, c), lambda i: (0, i, 0))

    def lanes(r):
        return pl.BlockSpec((FOX_HEADS, r, T), lambda i: (0, 0, i))

    bf = lambda *shape: jax.ShapeDtypeStruct((FOX_HEADS,) + shape, BF16)
    return pl.pallas_call(
        body,
        name="fox_prep",
        grid=(S // T,),
        in_specs=[win(U_CQ), win(U_CK), win(U_CV), pl.BlockSpec((T, LANES), lambda i: (i, 0))]
        + [_full_spec(c) for c in consts],
        out_specs=[rows(FOX_AUG), rows(FOX_AUG), lanes(FOX_AUG), lanes(FOX_AUG), rows(FOX_DH), lanes(FOX_DH)],
        out_shape=[bf(S, FOX_AUG), bf(S, FOX_AUG), bf(FOX_AUG, S), bf(FOX_AUG, S), bf(S, FOX_DH), bf(FOX_DH, S)],
        compiler_params=pltpu.CompilerParams(dimension_semantics=("parallel",), vmem_limit_bytes=VMEM_LIMIT_BYTES),
    )(u, u, u, fcum, *consts)


def _to_heads(x2d, S):
    return jnp.transpose(x2d.reshape(S, FOX_HEADS, FOX_DH), (1, 0, 2))


def _from_heads(xh, S):
    return jnp.transpose(xh, (1, 0, 2)).reshape(S, FOX_HEADS * FOX_DH)


def _ffn_fwd(tag, x, wgT, wuT, wd, g, b, S):
    def up_epi(accs):
        gate, up = accs
        sil, _ = _silu_and_grad(gate)
        return [gate, up, sil * up]

    gate, up, act = _mm(tag + "_up", "nt", [x], [wgT, wuT], [(0, 0, 0), (1, 0, 1)], 2, up_epi, [],
                        [BF16, BF16, BF16], S, D_FF, D_MODEL, tn=1408)

    def down_epi(accs, xr, gg, bb):
        z = ALPHA * xr + 0.5 * accs[0]
        return [z, _ln_fwd(z, gg, bb)]

    z, xn = _mm(tag + "_down", "nn", [act], [wd], [(0, 0, 0)], 1, down_epi, [(x, "mn", 0), (g, "n"), (b, "n")],
                [F32, F32], S, D_MODEL, D_FF, tk=D_FF)
    return xn, dict(x=x, gate=gate, up=up, act=act, z=z)


def _ln_bwd_call(tag, dy, z, g, S):
    def fn(dy_t, z_t, g_t):
        dz, xhat = _ln_bwd(dy_t, z_t, g_t)
        return [dz], [_colsum(dy_t * xhat), _colsum(dy_t)]

    (dz,), (dg, db) = _rowwise(tag + "_ln_bwd", fn, [dy, z], [g], [(D_MODEL, F32)], [D_MODEL, D_MODEL], S)
    return dz, dg, db


def _ffn_bwd(tag, dxn, sv, wgT, wuT, wd, g, S, gdt=F32):
    dz, dg, db = _ln_bwd_call(tag, dxn, sv["z"], g, S)

    def act_epi(accs, gate, up):
        da = 0.5 * accs[0]
        sil, dsil = _silu_and_grad(gate.astype(F32))
        return [da * up.astype(F32) * dsil, da * sil]

    dgate, dup = _mm(tag + "_dact", "nt", [dz], [wd], [(0, 0, 0)], 1, act_epi,
                     [(sv["gate"], "mn", 0), (sv["up"], "mn", 0)], [BF16, BF16], S, D_FF, D_MODEL, tn=1408)
    dwd = _mm1(tag + "_dwd", "tn", sv["act"], dz, D_FF, D_MODEL, S, scale=0.5, tm=1408, out_dtype=gdt)

    def two(accs):
        return [accs[0], accs[1]]

    dwgT, dwuT = _mm(tag + "_dwup", "tn", [dgate, dup], [sv["x"]], [(0, 0, 0), (1, 1, 0)], 2, two, [], [gdt, gdt],
                     D_FF, D_MODEL, S, tm=1408, tk=512)

    def dx_epi(accs, dzr):
        return [accs[0] + ALPHA * dzr]

    (dx,) = _mm(tag + "_dx", "nn", [dgate, dup], [wgT, wuT], [(0, 0, 0), (0, 1, 1)], 1, dx_epi, [(dz, "mn", 0)],
                [F32], S, D_MODEL, D_FF, tm=1024, tk=1408)
    return dx, dict(w_upT=jnp.concatenate([dwgT, dwuT], axis=0), w_down=dwd, ln_g=dg, ln_b=db)


def _mixer_fwd(x1, w, S, side=None, on_side=None):
    u = _mm1("w_in", "nt", x1, w["w_inT_p"], S, U_WIDTH, D_MODEL, tm=1024, tn=1536)
    xc, r, gi, a, h, y_a = _lru_fwd(u, w["lru"], S)
    y_b, oraw, states = _gla_fwd(u, w["gla"], S)
    fcum = _fox_gate_fwd(u, w["bfp"], S)
    qa, ka, qT, kT, vh, vT = _fox_prep(u, fcum, S)
    (oT, lse), side_out = _fox_fwd(qT, ka, vT, S, side)
    if on_side is not None:
        on_side(side_out)
    y_c = jnp.transpose(oT, (2, 0, 1)).reshape(S, BRANCH).astype(BF16)

    def merge_epi(accs, g0, g1, g2):
        merged = _sigmoid(g0) * accs[0] + _sigmoid(g1) * accs[1] + _sigmoid(g2) * accs[2]
        return [accs[0], accs[1], accs[2], merged]

    wb = w["w_branchT"]
    yp0, yp1, yp2, merged = _mm(
        "merge", "nt", [y_a, y_b, y_c], [wb[0], wb[1], wb[2]], [(0, 0, 0), (1, 1, 1), (2, 2, 2)], 3, merge_epi,
        [(u, "mn", 0), (u, "mn", 1), (u, "mn", 2)], [F32, F32, F32, BF16], S, D_MODEL, BRANCH, tm=256)

    def out_epi(accs, xr, gg, bb):
        z = ALPHA * xr + accs[0]
        return [z, _ln_fwd(z, gg, bb)]

    z2, x2 = _mm("w_out", "nn", [merged], [w["w_out"]], [(0, 0, 0)], 1, out_epi,
                 [(x1, "mn", 0), (w["ln2_g"], "n"), (w["ln2_b"], "n")], [F32, F32], S, D_MODEL, D_MODEL)
    sv = dict(x=x1, u=u, xc=xc, r=r, i=gi, a=a, h=h, y_a=y_a, y_b=y_b, y_c=y_c, oraw=oraw,
              states=states, qT=qT, qa=qa, ka=ka, kT=kT, vh=vh, oT=oT, lse=lse, yp=(yp0, yp1, yp2), merged=merged,
              z=z2)
    return x2, sv, side_out


def _mixer_bwd(dx2, sv, w, S, side=None, gdt=F32):
    u = sv["u"]
    dz, dg2, db2 = _ln_bwd_call("mix", dx2, sv["z"], w["ln2_g"], S)

    def dm_epi(accs, y0, y1, y2, g0, g1, g2):
        dm = accs[0]
        outs_p, outs_g = [], []
        for yp, gl in ((y0, g0), (y1, g1), (y2, g2)):
            sg = _sigmoid(gl)
            outs_p.append(dm * sg)
            outs_g.append(dm * yp * sg * (1.0 - sg))
        return outs_p + outs_g

    yp = sv["yp"]
    dyp0, dyp1, dyp2, dgl0, dgl1, dgl2 = _mm(
        "dmerged", "nt", [dz], [w["w_out"]], [(0, 0, 0)], 1, dm_epi,
        [(yp[0], "mn", 0), (yp[1], "mn", 0), (yp[2], "mn", 0), (u, "mn", 0), (u, "mn", 1), (u, "mn", 2)],
        [BF16] * 6, S, D_MODEL, D_MODEL, tm=256)
    dw_out = _mm1("dw_out", "tn", sv["merged"], dz, D_MODEL, D_MODEL, S, out_dtype=gdt)
    wb = w["w_branchT"]
    dys, dwbs = [], []
    for j, (yj, dyp) in enumerate(((sv["y_a"], dyp0), (sv["y_b"], dyp1), (sv["y_c"], dyp2))):
        dys.append(_mm1("dy_branch%d" % j, "nn", dyp, wb[j], S, BRANCH, D_MODEL))
        dwbs.append(_mm1("dw_branch%d" % j, "tn", dyp, yj, D_MODEL, BRANCH, S, out_dtype=gdt))
    day, dxc, dwa, dwx, dba, dbx, dlam = _lru_bwd(dys[0], u, sv, w["lru"], S)
    dax, (dcw0, dcw1, dcw2, dcw3, dcb) = _conv_bwd(dxc, u, w["lru"], S)
    dbq, dbk, dbv, dbr, dglow, dwg2p, dbg, dng = _gla_bwd(dys[1], u, sv["oraw"], sv["states"], w["gla"], S)
    doh = _to_heads(dys[2], S)
    (dqh, dfq, dkh, dfk, dvh), side_out = _fox_bwd(sv["qT"], sv["qa"], sv["ka"], sv["kT"], sv["vh"], doh,
                                                   jnp.swapaxes(doh, 1, 2), sv["oT"], sv["lse"], S, side)
    dfc = jnp.transpose(dfq[:, FOX_ONES, :] - dfk[:, 0, :])
    dfc = jnp.pad(dfc, ((0, 0), (0, LANES - FOX_HEADS)))
    dcf, dbf = _fox_gate_bwd(dfc, u, w["bfp"], S)
    du = jnp.concatenate(
        [dgl0, dgl1, dgl2, dax, day, dbq, dbk, dbv, dbr, _from_heads(dqh, S).astype(BF16),
         _from_heads(dkh, S).astype(BF16), _from_heads(dvh, S).astype(BF16), dglow, dcf,
         jnp.zeros((S, U_WIDTH - U_CF - LANES), BF16)], axis=1)
    dw_inT_p = _mm1("dw_in", "tn", du, sv["x"], U_WIDTH, D_MODEL, S, tm=1536, out_dtype=gdt)

    def dx_epi(accs, dzr):
        return [accs[0] + ALPHA * dzr]

    (dx1,) = _mm("dx_mix", "nn", [du], [w["w_inT_p"]], [(0, 0, 0)], 1, dx_epi, [(dz, "mn", 0)], [F32], S, D_MODEL,
                 U_WIDTH, tm=1024, tk=1536)
    pieces = sorted(W_IN_SEGMENTS)
    dw_inT = jnp.concatenate([dw_inT_p[dst:dst + width] for _, width, dst in pieces], axis=0)
    eye = jnp.eye(LRU_BLOCKS, dtype=F32)
    dwa_b = jnp.einsum("ncmd,nm->ncd", dwa.reshape(LRU_BLOCKS, 64, LRU_BLOCKS, 64), eye)
    dwx_b = jnp.einsum("ncmd,nm->ncd", dwx.reshape(LRU_BLOCKS, 64, LRU_BLOCKS, 64), eye)
    grads = dict(
        w_inT=dw_inT, w_out=dw_out, w_branchT=jnp.stack(dwbs), ln2_g=dg2, ln2_b=db2,
        conv_w=jnp.concatenate([dcw0, dcw1, dcw2, dcw3], axis=0).astype(gdt), conv_b=dcb, lru_wa=dwa_b, lru_wx=dwx_b,
        lru_ba=dba, lru_bx=dbx, lru_lambda=dlam, gla_w_g2=dwg2p[:GLA_LOWRANK].astype(gdt), gla_b_g=dbg, gla_norm_g=dng,
        fox_b_f=dbf[:, :FOX_HEADS])
    return dx1, grads, side_out


def _ple_fwd(x3, p_i, w, S):
    pe = _mm1("ple_proj", "nt", p_i, w["ple_w_projT"], S, D_MODEL, PLE_DIM)

    def epi(accs, xr, per, bg, gg, bb):
        sg = _sigmoid(accs[0] + bg)
        z = ALPHA * xr + sg * per
        return [sg, z, _ln_fwd(z, gg, bb)]

    sg, z4, x4 = _mm("ple_gate", "nn", [x3], [w["ple_w_gate"]], [(0, 0, 0)], 1, epi,
                     [(x3, "mn", 0), (pe, "mn", 0), (w["ple_b_gate"], "n"), (w["ln4_g"], "n"), (w["ln4_b"], "n")],
                     [F32, F32, F32], S, D_MODEL, D_MODEL)
    return x4, dict(x=x3, p=p_i, pe=pe, sg=sg, z=z4)


def _ple_bwd(dx4, sv, w, S, gdt=F32):
    def fn(dy_t, z_t, pe_t, sg_t, g_t):
        dz, xhat = _ln_bwd(dy_t, z_t, g_t)
        dgl = dz * pe_t * sg_t * (1.0 - sg_t)
        return [dz, dz * sg_t, dgl], [_colsum(dy_t * xhat), _colsum(dy_t), _colsum(dgl)]

    (dz, dpe, dgl), (dg4, db4, dbg) = _rowwise(
        "ple_bwd", fn, [dx4, sv["z"], sv["pe"], sv["sg"]], [w["ln4_g"]],
        [(D_MODEL, F32), (D_MODEL, BF16), (D_MODEL, BF16)], [D_MODEL] * 3, S)
    dwpT = _mm1("dw_ple_proj", "tn", dpe, sv["p"], D_MODEL, PLE_DIM, S, out_dtype=gdt)
    dwg = _mm1("dw_ple_gate", "tn", sv["x"], dgl, D_MODEL, D_MODEL, S, out_dtype=gdt)

    def dx_epi(accs, dzr):
        return [accs[0] + ALPHA * dzr]

    (dx3,) = _mm("dx_ple", "nt", [dgl], [w["ple_w_gate"]], [(0, 0, 0)], 1, dx_epi, [(dz, "mn", 0)], [F32], S,
                 D_MODEL, D_MODEL)
    return dx3, dict(ple_w_projT=dwpT, ple_w_gate=dwg, ple_b_gate=dbg, ln4_g=dg4, ln4_b=db4)


def _rows_of_all(g):
    return g.reshape((g.shape[0] * g.shape[1],) + g.shape[2:])


EARLY_WEIGHTS = ("ffn1_w_up", "ffn1_w_down", "w_in", "conv_w", "gla_w_g2")


def _ffn_weights(gathered, tag):
    upT = _rows_of_all(gathered[tag + "_w_up"])
    return upT[:D_FF], upT[D_FF:], _rows_of_all(gathered[tag + "_w_down"])


def _late_weights(gathered):
    return dict(ffn2=_ffn_weights(gathered, "ffn2"),
                w_branchT=jnp.moveaxis(gathered["w_branch"], 0, 1).reshape(3, D_MODEL, BRANCH),
                w_out=_rows_of_all(gathered["w_out"]),
                ple_w_projT=_rows_of_all(gathered["ple_w_proj"]),
                ple_w_gate=_rows_of_all(gathered["ple_w_gate"]))


def _layer_weights(gathered, full, i):
    w = _late_weights(gathered) if "w_out" in gathered else {}
    w["ffn1"] = _ffn_weights(gathered, "ffn1")
    w_inT = _rows_of_all(gathered["w_in"])
    placed = sorted((dst, src, width) for src, width, dst in W_IN_SEGMENTS)
    parts, pos = [], 0
    for dst, src, width in placed:
        if dst > pos:
            parts.append(jnp.zeros((dst - pos, D_MODEL), w_inT.dtype))
        parts.append(w_inT[src:src + width])
        pos = dst + width
    parts.append(jnp.zeros((U_WIDTH - pos, D_MODEL), w_inT.dtype))
    w["w_inT_p"] = jnp.concatenate(parts, axis=0)
    eye = jnp.eye(LRU_BLOCKS, dtype=F32)

    def dense(blocks):
        return jnp.einsum("ncd,nm->ncmd", blocks, eye).reshape(BRANCH, BRANCH).astype(BF16)

    def vec(name):
        return full[name][i].reshape(1, -1)

    cw = jnp.moveaxis(gathered["conv_w"], 0, 1).reshape(4, BRANCH)
    w_g2 = jnp.moveaxis(gathered["gla_w_g2"], 0, 1).reshape(GLA_LOWRANK, GLA_QK)
    w["lru"] = dict(cw0=cw[0:1], cw1=cw[1:2], cw2=cw[2:3], cw3=cw[3:4], conv_b=vec("conv_b"),
                    wa=dense(full["lru_wa"][i]), wx=dense(full["lru_wx"][i]), ba=vec("lru_ba"), bx=vec("lru_bx"),
                    lam=vec("lru_lambda"))
    hq = jnp.arange(GLA_QK) // GLA_DK
    hv = jnp.arange(GLA_V) // GLA_DV
    w["gla"] = dict(wg2=jnp.pad(w_g2, ((0, LANES - GLA_LOWRANK), (0, 0))).astype(BF16),
                    bg=vec("gla_b_g"), ng=vec("gla_norm_g"), bd=(hv[:, None] == hq[None, :]).astype(F32))
    w["bfp"] = jnp.pad(vec("fox_b_f"), ((0, 0), (0, LANES - FOX_HEADS)))
    for name in ("ln1_g", "ln1_b", "ln2_g", "ln2_b", "ln3_g", "ln3_b", "ln4_g", "ln4_b", "ple_b_gate"):
        w[name] = vec(name)
    return w


def _layer_fwd(x0, p_i, w, S, side=None, on_side=None):
    x1, s1 = _ffn_fwd("ffn1", x0, *w["ffn1"], w["ln1_g"], w["ln1_b"], S)
    x2, s2, side_out = _mixer_fwd(x1, w, S, side, on_side)
    x3, s3 = _ffn_fwd("ffn2", x2, *w["ffn2"], w["ln3_g"], w["ln3_b"], S)
    x4, s4 = _ple_fwd(x3, p_i, w, S)
    return x4, (s1, s2, s3, s4), side_out


def _layer_bwd(dx4, saved, w, S, side=None, gdt=F32):
    s1, s2, s3, s4 = saved
    dx3, g4 = _ple_bwd(dx4, s4, w, S, gdt)
    dx2, g3 = _ffn_bwd("ffn2", dx3, s3, *w["ffn2"], w["ln3_g"], S, gdt)
    dx1, g2, side_out = _mixer_bwd(dx2, s2, w, S, side, gdt)
    dx0, g1 = _ffn_bwd("ffn1", dx1, s1, *w["ffn1"], w["ln1_g"], S, gdt)
    grads = dict(g2)
    grads.update(g4)
    grads.update(ffn1_w_upT=g1["w_upT"], ffn1_w_down=g1["w_down"], ln1_g=g1["ln_g"], ln1_b=g1["ln_b"],
                 ffn2_w_upT=g3["w_upT"], ffn2_w_down=g3["w_down"], ln3_g=g3["ln_g"], ln3_b=g3["ln_b"])
    return dx0, grads, side_out


def _travel_grads(grads):
    return [_dest_pieces(n, grads[n + "T" if n in COLUMN_SHARDED else n]) for n, _ in SHARDED]


def _local_step(x, p, target, gathered0, layer1, full, overlap):
    S = x.shape[0]
    names = [n for n, _ in SHARDED]
    late = [n for n in names if n not in EARLY_WEIGHTS]
    w0 = _layer_weights(gathered0, full, 0)
    if overlap:
        late0, shards1 = layer1

        def on_gathered(got):
            w0.update(_late_weights(dict(zip(late, got[:len(late)]))))

        h, saved0, got = _layer_fwd(x, p[0], w0, S, _gather_job(list(late0) + list(shards1)), on_gathered)
        w1 = _layer_weights(dict(zip(names, got[len(late):])), full, 1)
    else:
        h, saved0, _ = _layer_fwd(x, p[0], w0, S)
        w1 = _layer_weights(layer1, full, 1)
    h, saved1, _ = _layer_fwd(h, p[1], w1, S)

    def loss_fn(y, t):
        err = y - t
        return [err * (1.0 / D_MODEL)], [_colsum(err * err) * (0.5 / D_MODEL)]

    (dy,), (lsum,) = _rowwise("loss", loss_fn, [h, target], [], [(D_MODEL, F32)], [D_MODEL], S)
    loss = jnp.sum(lsum)
    dy, g1, _ = _layer_bwd(dy, saved1, w1, S, gdt=BF16 if overlap else F32)
    dy, g0, pieces1 = _layer_bwd(dy, saved0, w0, S, _scatter_job(_travel_grads(g1)) if overlap else None)
    return loss, dy, [g0, g1], pieces1 if overlap else None


def kernel(x, p, ffn1_w_up, ffn1_w_down, ln1_g, ln1_b, w_in, conv_w, conv_b, lru_wa, lru_ba, lru_wx, lru_bx, lru_lambda, gla_w_g2, gla_b_g, gla_norm_g, fox_b_f, w_branch, w_out, ln2_g, ln2_b, ffn2_w_up, ffn2_w_down, ln3_g, ln3_b, ple_w_proj, ple_w_gate, ple_b_gate, ln4_g, ln4_b, loss_target, m_ffn1_w_up, m_ffn1_w_down, m_ln1_g, m_ln1_b, m_w_in, m_conv_w, m_conv_b, m_lru_wa, m_lru_ba, m_lru_wx, m_lru_bx, m_lru_lambda, m_gla_w_g2, m_gla_b_g, m_gla_norm_g, m_fox_b_f, m_w_branch, m_w_out, m_ln2_g, m_ln2_b, m_ffn2_w_up, m_ffn2_w_down, m_ln3_g, m_ln3_b, m_ple_w_proj, m_ple_w_gate, m_ple_b_gate, m_ln4_g, m_ln4_b, v_ffn1_w_up, v_ffn1_w_down, v_ln1_g, v_ln1_b, v_w_in, v_conv_w, v_conv_b, v_lru_wa, v_lru_ba, v_lru_wx, v_lru_bx, v_lru_lambda, v_gla_w_g2, v_gla_b_g, v_gla_norm_g, v_fox_b_f, v_w_branch, v_w_out, v_ln2_g, v_ln2_b, v_ffn2_w_up, v_ffn2_w_down, v_ln3_g, v_ln3_b, v_ple_w_proj, v_ple_w_gate, v_ple_b_gate, v_ln4_g, v_ln4_b):
    env = dict(locals())
    wts = {n: env[n] for n in WEIGHTS}
    ms = {n: env["m_" + n] for n in WEIGHTS}
    vs = {n: env["v_" + n] for n in WEIGHTS}
    sharded = [n for n, _ in SHARDED]

    def travel(n, a):
        return jnp.swapaxes(a, -1, -2) if n in COLUMN_SHARDED else a

    shards = [[travel(n, wts[n][i]) if n in SHARDED_F32_GATHER else travel(n, wts[n][i]).astype(BF16) for n in sharded]
              for i in range(DEPTH)]
    early = [k for k, n in enumerate(sharded) if n in EARLY_WEIGHTS]
    late0 = [shards[0][k] for k, n in enumerate(sharded) if n not in EARLY_WEIGHTS]
    gathered0 = dict(zip([sharded[k] for k in early],
                         _allgather_multi("gather_weights", [shards[0][k] for k in early])))
    full = {n: wts[n] for n in REPLICATED}

    loss_part, grad_x, layer_grads, pieces1 = _local_step(x[0], p[:, 0], loss_target[0], gathered0,
                                                           (late0, shards[1]), full, True)
    loss = lax.psum(loss_part, MESH_AXES)

    dest = _travel_grads(layer_grads[0])
    got = _sibling_swap_multi("grad_sibling_swap", dest)
    core = lax.axis_index("c").astype(jnp.int32).reshape(1)
    pairs = [_pair_add("grad_pair_add_" + n, core, _as_rows(d, 2), _as_rows(g, 1))
             for n, d, g in zip(sharded, dest, got)]
    pieces0 = _chip_all_to_all_multi("grad_chip_all_to_all", pairs)
    rep = list(REPLICATED)
    rep_grads = [jnp.stack([layer_grads[i][n] for i in range(DEPTH)]).reshape(wts[n].shape) for n in rep]
    (gr,) = _allgather_multi("grad_gather_replicated", [_pack(rep_grads, F32)])

    kinds = ("grad", "delta", "new_m", "new_v")
    out = {}
    for k, n in enumerate(sharded):
        local = [_as_rows(travel(n, pieces.reshape((-1,) + shards[i][k].shape)), 1)
                 for i, pieces in ((0, pieces0[k]), (1, pieces1[k]))]
        res = _adamw("adamw_" + n, local, _as_rows(wts[n], 1), _as_rows(ms[n], 1), _as_rows(vs[n], 1))
        for kind, arr in zip(kinds, res):
            out[kind + "_" + n] = arr.reshape(wts[n].shape)
    res = _adamw("adamw_replicated", [gr], _pack([wts[n] for n in rep], F32)[None],
                 _pack([ms[n] for n in rep], F32)[None], _pack([vs[n] for n in rep], F32)[None])
    shapes = [wts[n].shape for n in rep]
    for kind, buf in zip(kinds, res):
        for n, arr in zip(rep, _unpack(buf[0], shapes)):
            out[kind + "_" + n] = arr
    return (loss, grad_x[None], *[out["grad_" + n] for n in WEIGHTS], *[out["delta_" + n] for n in WEIGHTS],
            *[out["new_m_" + n] for n in WEIGHTS], *[out["new_v_" + n] for n in WEIGHTS])
```

```python
import functools
import math

import jax
import jax.numpy as jnp
from jax import lax
from jax.experimental import pallas as pl
from jax.experimental.pallas import tpu as pltpu

F32 = jnp.float32
BF16 = jnp.bfloat16

N_DEV = 8
MESH_AXES = ("x", "y", "c")
DEPTH = 2
D_MODEL = 1024
D_FF = 2816
BRANCH = 512
CHUNK = 64
GLA_HEADS = 4
GLA_DK = 64
GLA_DV = 128
GLA_LOWRANK = 16
GLA_TAU = 16.0
FOX_HEADS = 8
FOX_DH = 64
PLE_DIM = 256
LRU_C = 8.0
LRU_BLOCKS = 8
LN_EPS = 1e-5
RMS_EPS = 1e-6
ALPHA = (2 * DEPTH) ** 0.25
LANES = 128
NEG_BIG = -1e30

ADAM_LR = 0.001
ADAM_B1 = 0.9
ADAM_B2 = 0.999
ADAM_EPS = 1e-08
ADAM_WD = 0.01
ADAM_STEP = 10

VMEM_LIMIT_BYTES = 56 * 1024 * 1024

U_GATES = 0
U_AX = 3072
U_AY = 3584
U_BQ = 4096
U_BK = 4352
U_BV = 4608
U_BR = 5120
U_CQ = 5632
U_CK = 6144
U_CV = 6656
U_BLOW = 7168
U_CF = 7296
U_WIDTH = 7680
W_IN_SEGMENTS = (
    (0, 512, U_AX), (512, 512, U_AY), (1024, 256, U_BQ), (1280, 256, U_BK), (1536, 512, U_BV),
    (2048, 16, U_BLOW), (2064, 512, U_BR), (2576, 512, U_CQ), (3088, 512, U_CK), (3600, 512, U_CV),
    (4112, 8, U_CF), (4120, 3072, U_GATES),
)

SHARDED = (
    ("ffn1_w_up", 2), ("ffn1_w_down", 1), ("w_in", 2), ("conv_w", 2), ("gla_w_g2", 2), ("w_branch", 3),
    ("w_out", 1), ("ffn2_w_up", 2), ("ffn2_w_down", 1), ("ple_w_proj", 2), ("ple_w_gate", 1),
)
SHARDED_F32_GATHER = ("conv_w", "gla_w_g2")
COLUMN_SHARDED = ("ffn1_w_up", "ffn2_w_up", "w_in", "w_branch", "ple_w_proj")
REPLICATED = ("ln1_g", "ln1_b", "conv_b", "lru_wa", "lru_ba", "lru_wx", "lru_bx", "lru_lambda", "gla_b_g",
              "gla_norm_g", "fox_b_f", "ln2_g", "ln2_b", "ln3_g", "ln3_b", "ple_b_gate", "ln4_g", "ln4_b")
WEIGHTS = ("ffn1_w_up", "ffn1_w_down", "ln1_g", "ln1_b", "w_in", "conv_w", "conv_b", "lru_wa", "lru_ba", "lru_wx",
           "lru_bx", "lru_lambda", "gla_w_g2", "gla_b_g", "gla_norm_g", "fox_b_f", "w_branch", "w_out", "ln2_g",
           "ln2_b", "ffn2_w_up", "ffn2_w_down", "ln3_g", "ln3_b", "ple_w_proj", "ple_w_gate", "ple_b_gate", "ln4_g",
           "ln4_b")


def _sigmoid(x):
    return 1.0 / (1.0 + jnp.exp(-x))


def _log1p_pos(e):
    return jnp.where(e < 1e-4, e * (1.0 - 0.5 * e), jnp.log(1.0 + e))


def _softplus(x):
    return jnp.maximum(x, 0.0) + _log1p_pos(jnp.exp(-jnp.abs(x)))


def _log_sigmoid(x):
    return -_softplus(-x)


def _neg_expm1(y):
    series = -y * (1.0 + y * (0.5 + y * (1.0 / 6.0 + y * (1.0 / 24.0 + y * (1.0 / 120.0)))))
    return jnp.where(y > -0.1, series, 1.0 - jnp.exp(y))


def _silu_and_grad(x):
    s = _sigmoid(x)
    return x * s, s * (1.0 + x * (1.0 - s))


_GELU_C = math.sqrt(2.0 / math.pi)


def _gelu_and_grad(x):
    inner = _GELU_C * (x + 0.044715 * x * x * x)
    t = jnp.tanh(inner)
    g = 0.5 * x * (1.0 + t)
    dg = 0.5 * (1.0 + t) + 0.5 * x * (1.0 - t * t) * _GELU_C * (1.0 + 3.0 * 0.044715 * x * x)
    return g, dg


def _ln_stats(z):
    mu = jnp.mean(z, axis=-1, keepdims=True)
    zc = z - mu
    var = jnp.mean(zc * zc, axis=-1, keepdims=True)
    rstd = lax.rsqrt(var + LN_EPS)
    return zc * rstd, rstd


def _ln_fwd(z, g, b):
    xhat, _ = _ln_stats(z)
    return xhat * g + b


def _ln_bwd(dy, z, g):
    xhat, rstd = _ln_stats(z)
    dxh = dy * g
    m1 = jnp.mean(dxh, axis=-1, keepdims=True)
    m2 = jnp.mean(dxh * xhat, axis=-1, keepdims=True)
    return rstd * (dxh - m1 - xhat * m2), xhat


def _colsum(x):
    return jnp.sum(x, axis=0, keepdims=True)


def _dot(a, b, dims):
    dn = {"nn": (((1,), (0,)), ((), ())), "nt": (((1,), (1,)), ((), ())), "tn": (((0,), (0,)), ((), ()))}[dims]
    return lax.dot_general(a.astype(BF16), b.astype(BF16), dn, preferred_element_type=F32)


def _scan_rows(a, b, length, reverse=False, seg=None):
    rows = lax.broadcasted_iota(jnp.int32, b.shape, 0)
    span = seg if seg else length
    pos = rows % span if seg else rows
    d = 1
    while d < span:
        shift = (length - d) if reverse else d
        valid = (pos < span - d) if reverse else (pos >= d)
        sb = jnp.where(valid, pltpu.roll(b, shift, 0), 0.0)
        if a is None:
            b = b + sb
        else:
            b = b + a * sb
            a = a * jnp.where(valid, pltpu.roll(a, shift, 0), 1.0)
        d *= 2
    return a, b


def _tile(dim, pref):
    if dim <= pref:
        return dim
    best = None
    t = LANES
    while t <= pref:
        if dim % t == 0:
            best = t
        t += LANES
    assert best is not None, (dim, pref)
    return best


def _full_spec(arr):
    nd = arr.ndim
    return pl.BlockSpec(arr.shape, lambda *_: (0,) * nd)


def _mm(name, dims, a_ops, b_ops, terms, n_acc, epilogue, extras, out_dtypes, M, N, K, tm=512, tn=1024, tk=1024):
    tm, tn, tk = _tile(M, tm), _tile(N, tn), _tile(K, tk)
    gm, gn, gk = M // tm, N // tn, K // tk
    a_bytes = sum(a.size * a.dtype.itemsize for a in a_ops)
    b_bytes = sum(b.size * b.dtype.itemsize for b in b_ops)
    n_outer = gk == 1 and b_bytes + a_bytes * gn < a_bytes + b_bytes * gm

    def spec(shape, fn):
        if n_outer:
            return pl.BlockSpec(shape, lambda j, i, k: fn(i, j, k))
        return pl.BlockSpec(shape, fn)

    if dims == "tn":
        a_spec = spec((tk, tm), lambda i, j, k: (k, i))
    else:
        a_spec = spec((tm, tk), lambda i, j, k: (i, k))
    if dims == "nt":
        b_spec = spec((tn, tk), lambda i, j, k: (j, k))
    else:
        b_spec = spec((tk, tn), lambda i, j, k: (k, j))
    e_specs, e_arrays = [], []
    for ex in extras:
        if ex[1] == "mn":
            e_specs.append(spec((tm, tn), functools.partial(lambda i, j, k, off: (i, j + off), off=ex[2])))
        else:
            e_specs.append(spec((1, tn), lambda i, j, k: (0, j)))
        e_arrays.append(ex[0])
    na, nb, ne, no = len(a_ops), len(b_ops), len(extras), len(out_dtypes)

    def body(*refs):
        a_refs = refs[:na]
        b_refs = refs[na:na + nb]
        e_refs = refs[na + nb:na + nb + ne]
        o_refs = refs[na + nb + ne:na + nb + ne + no]
        acc_refs = refs[na + nb + ne + no:]
        k = pl.program_id(2)

        @pl.when(k == 0)
        def _():
            for acc in acc_refs:
                acc[...] = jnp.zeros_like(acc)

        for r, ai, bi in terms:
            acc_refs[r][...] += _dot(a_refs[ai][...], b_refs[bi][...], dims)

        @pl.when(k == gk - 1)
        def _():
            res = epilogue([acc[...] for acc in acc_refs], *[e[...] for e in e_refs])
            for o, val in zip(o_refs, res):
                o[...] = val.astype(o.dtype)

    outs = pl.pallas_call(
        body,
        name=name,
        grid=(gn, gm, gk) if n_outer else (gm, gn, gk),
        in_specs=[a_spec] * na + [b_spec] * nb + e_specs,
        out_specs=[spec((tm, tn), lambda i, j, k: (i, j))] * no,
        out_shape=[jax.ShapeDtypeStruct((M, N), dt) for dt in out_dtypes],
        scratch_shapes=[pltpu.VMEM((tm, tn), F32)] * n_acc,
        compiler_params=pltpu.CompilerParams(
            dimension_semantics=("parallel", "parallel", "arbitrary"), vmem_limit_bytes=VMEM_LIMIT_BYTES),
    )(*a_ops, *b_ops, *e_arrays)
    return outs


def _mm1(name, dims, a, b, M, N, K, out_dtype=F32, scale=None, **kw):
    def epi(accs):
        return [accs[0] if scale is None else accs[0] * scale]
    return _mm(name, dims, [a], [b], [(0, 0, 0)], 1, epi, [], [out_dtype], M, N, K, **kw)[0]


def _rowwise(name, fn, row_ins, vec_ins, row_outs, sum_outs, S, tr=256, reverse=False):
    tr = min(tr, S)
    g = S // tr
    rmap = (lambda i: (g - 1 - i)) if reverse else (lambda i: i)
    in_specs, arrays = [], []
    for r in row_ins:
        if isinstance(r, tuple):
            arr, width, blk = r
            in_specs.append(pl.BlockSpec((tr, width), functools.partial(lambda i, blk: (rmap(i), blk), blk=blk)))
        else:
            arr = r
            in_specs.append(pl.BlockSpec((tr, arr.shape[1]), lambda i: (rmap(i), 0)))
        arrays.append(arr)
    for v in vec_ins:
        in_specs.append(_full_spec(v))
        arrays.append(v)
    nr, nv, no, ns = len(row_ins), len(vec_ins), len(row_outs), len(sum_outs)

    def body(*refs):
        ins = [r[...] for r in refs[:nr + nv]]
        o_refs = refs[nr + nv:nr + nv + no]
        s_refs = refs[nr + nv + no:]
        outs, sums = fn(*ins)
        for o, val in zip(o_refs, outs):
            o[...] = val.astype(o.dtype)
        if ns:
            i = pl.program_id(0)

            @pl.when(i == 0)
            def _():
                for s, val in zip(s_refs, sums):
                    s[...] = val

            @pl.when(i > 0)
            def _():
                for s, val in zip(s_refs, sums):
                    s[...] += val

    res = pl.pallas_call(
        body,
        name=name,
        grid=(g,),
        in_specs=in_specs,
        out_specs=[pl.BlockSpec((tr, c), lambda i: (rmap(i), 0)) for c, _ in row_outs]
        + [pl.BlockSpec((1, c), lambda i: (0, 0)) for c in sum_outs],
        out_shape=[jax.ShapeDtypeStruct((S, c), dt) for c, dt in row_outs]
        + [jax.ShapeDtypeStruct((1, c), F32) for c in sum_outs],
        compiler_params=pltpu.CompilerParams(
            dimension_semantics=("arbitrary",), vmem_limit_bytes=VMEM_LIMIT_BYTES),
    )(*arrays)
    return res[:no], res[no:]


def _win(arr, offset, width):
    assert offset % width == 0
    return (arr, width, offset // width)


MESH_ID = pl.DeviceIdType.MESH


def _remote(src, dst, send_sem, recv_sem, to):
    return pltpu.make_async_remote_copy(src_ref=src, dst_ref=dst, send_sem=send_sem, recv_sem=recv_sem,
                                        device_id=to, device_id_type=MESH_ID)


def _hbm_call(name, body, arrs, out_shapes, n_send, n_recv, n_local):
    return pl.pallas_call(
        body,
        name=name,
        in_specs=[pl.BlockSpec(memory_space=pltpu.HBM)] * len(arrs),
        out_specs=[pl.BlockSpec(memory_space=pltpu.HBM)] * len(out_shapes),
        out_shape=out_shapes,
        scratch_shapes=[pltpu.SemaphoreType.DMA((n_send,)), pltpu.SemaphoreType.DMA((n_recv,)),
                        pltpu.SemaphoreType.DMA((n_local,))],
        compiler_params=pltpu.CompilerParams(has_side_effects=True),
    )(*arrs)


def _side_job(arrs, out_shapes, n_send, n_recv, n_local, phases):
    return dict(arrs=list(arrs), out_shapes=list(out_shapes), sems=(n_send, n_recv, n_local), phases=phases)


def _side_specs(side):
    hbm = pl.BlockSpec(memory_space=pltpu.HBM)
    sems = [pltpu.SemaphoreType.DMA((k,)) for k in side["sems"]]
    return [hbm] * len(side["arrs"]), [hbm] * len(side["out_shapes"]), sems


def _gather_job(arrs):
    n = len(arrs)

    def plan(ins, outs, send_sems, recv_sems, local_sems):
        x, y, c = lax.axis_index("x"), lax.axis_index("y"), lax.axis_index("c")
        me, sibling = (x, y, c), (x, y, 1 - c)
        chips = [(1 - x, y), (x, 1 - y), (1 - x, 1 - y)]

        def slot(i, dev):
            return outs[i].at[4 * dev[0] + 2 * dev[1] + dev[2]]

        def copy(i, k, block, to, src=None):
            dst = slot(i, block)
            return _remote(dst if src is None else src, dst, send_sems.at[7 * i + k], recv_sems.at[7 * i + k], to)

        mine = [pltpu.make_async_copy(ins[i], slot(i, me), local_sems.at[i]) for i in range(n)]
        first = []
        for i in range(n):
            first.append(copy(i, 0, me, sibling, src=ins[i]))
            first += [copy(i, 1 + j, me, (*chip, c), src=ins[i]) for j, chip in enumerate(chips)]
        arrive = [[copy(i, 1 + j, (*chip, c), me) for i in range(n)] for j, chip in enumerate(chips)]
        passed = [[copy(i, 4 + j, (*chip, c), sibling) for i in range(n)] for j, chip in enumerate(chips)]
        last = [copy(i, 0, sibling, me) for i in range(n)]
        last += [copy(i, 4 + j, (*chip, 1 - c), me) for i in range(n) for j, chip in enumerate(chips)]
        return mine, first, arrive, passed, last

    def start(*refs):
        mine, first, _, _, _ = plan(*refs)
        for cp in mine + first:
            cp.start()

    def forward(*refs):
        _, _, arrive, passed, _ = plan(*refs)
        for came, onward in zip(arrive, passed):
            for a, p in zip(came, onward):
                a.wait_recv()
                p.start()

    def finish(*refs):
        mine, first, _, passed, last = plan(*refs)
        for cp in last:
            cp.wait_recv()
        for cp in first + [p for onward in passed for p in onward]:
            cp.wait_send()
        for cp in mine:
            cp.wait()

    outs = [jax.ShapeDtypeStruct((N_DEV,) + a.shape, a.dtype) for a in arrs]
    return _side_job(arrs, outs, 7 * n, 7 * n, n, [start, forward, finish])


def _scatter_job(arrs):
    n = len(arrs)

    def plan(ins, outs, send_sems, recv_sems, local_sems):
        x, y, c = lax.axis_index("x"), lax.axis_index("y"), lax.axis_index("c")
        here = 2 * x + y
        local = [pltpu.make_async_copy(ins[i].at[here, c], outs[i].at[here, c], local_sems.at[i]) for i in range(n)]
        sends, recvs = [], []
        for i in range(n):
            for k in range(1, N_DEV):
                px = 1 - x if k & 4 else x
                py = 1 - y if k & 2 else y
                pc = 1 - c if k & 1 else c
                sems = (send_sems.at[7 * i + k - 1], recv_sems.at[7 * i + k - 1], (px, py, pc))
                sends.append(_remote(ins[i].at[2 * px + py, pc], outs[i].at[here, c], *sems))
                recvs.append(_remote(ins[i].at[2 * px + py, pc], outs[i].at[2 * px + py, pc], *sems))
        return local, sends, recvs

    def start(*refs):
        local, sends, _ = plan(*refs)
        for cp in local + sends:
            cp.start()

    def finish(*refs):
        local, sends, recvs = plan(*refs)
        for cp in recvs:
            cp.wait_recv()
        for cp in sends:
            cp.wait_send()
        for cp in local:
            cp.wait()

    outs = [jax.ShapeDtypeStruct(a.shape, a.dtype) for a in arrs]
    return _side_job(arrs, outs, 7 * n, 7 * n, n, [start, finish])


def _run_job(name, job):
    na, no = len(job["arrs"]), len(job["out_shapes"])

    def body(*refs):
        ins, outs, sems = refs[:na], refs[na:na + no], refs[na + no:]
        for phase in job["phases"]:
            phase(ins, outs, *sems)

    return _hbm_call(name, body, job["arrs"], job["out_shapes"], *job["sems"])


def _allgather_multi(name, arrs):
    return _run_job(name, _gather_job(arrs))


def _sibling_swap_multi(name, arrs):
    n = len(arrs)
    per = 4

    def body(*refs):
        ins, got = refs[:n], refs[n:2 * n]
        send_sems, recv_sems, _ = refs[2 * n:]
        x, y, c = lax.axis_index("x"), lax.axis_index("y"), lax.axis_index("c")
        sibling = (x, y, 1 - c)
        sends = []
        for i in range(n):
            for a in range(4):
                k = per * i + a
                sends.append(_remote(ins[i].at[a, 1 - c], got[i].at[a], send_sems.at[k], recv_sems.at[k], sibling))
        for cp in sends:
            cp.start()
        for cp in sends:
            cp.wait_recv()
        for cp in sends:
            cp.wait_send()

    outs = [jax.ShapeDtypeStruct((4,) + a.shape[2:], a.dtype) for a in arrs]
    return _hbm_call(name, body, arrs, outs, per * n, per * n, 1)


def _chip_all_to_all_multi(name, arrs):
    n = len(arrs)

    def body(*refs):
        ins, outs = refs[:n], refs[n:2 * n]
        send_sems, recv_sems, local_sems = refs[2 * n:]
        x, y, c = lax.axis_index("x"), lax.axis_index("y"), lax.axis_index("c")
        mine = 2 * x + y
        chips = [(1 - x, y), (x, 1 - y), (1 - x, 1 - y)]
        local = [pltpu.make_async_copy(ins[i].at[mine], outs[i].at[mine], local_sems.at[i]) for i in range(n)]
        for cp in local:
            cp.start()
        sends, recvs = [], []
        for i in range(n):
            for j, (px, py) in enumerate(chips):
                peer = 2 * px + py
                sems = (send_sems.at[3 * i + j], recv_sems.at[3 * i + j], (px, py, c))
                sends.append(_remote(ins[i].at[peer], outs[i].at[mine], *sems))
                recvs.append(_remote(ins[i].at[peer], outs[i].at[peer], *sems))
        for cp in sends:
            cp.start()
        for cp in recvs:
            cp.wait_recv()
        for cp in sends:
            cp.wait_send()
        for cp in local:
            cp.wait()

    outs = [jax.ShapeDtypeStruct(a.shape, a.dtype) for a in arrs]
    return _hbm_call(name, body, arrs, outs, 3 * n, 3 * n, n)


def _as_rows(a, lead):
    return a.reshape(a.shape[:lead] + (-1, a.shape[-1]))


def _row_tile(rows, cols, parts):
    budget = 4 * 1024 * 1024 // (4 * max(cols, LANES) * parts)
    return _tile_rows(rows, max(8, min(512, budget // 8 * 8)))


def _pair_add(name, core, both, got):
    _, rows, cols = got.shape
    tr = _row_tile(rows, cols, 2)

    def body(c_ref, a_ref, b_ref, o_ref):
        o_ref[...] = (a_ref[...] + b_ref[...]).astype(o_ref.dtype)

    blk = pl.BlockSpec((1, tr, cols), lambda ch, i, c_ref: (ch, i, 0))
    return pl.pallas_call(
        body, name=name,
        grid_spec=pltpu.PrefetchScalarGridSpec(
            num_scalar_prefetch=1, grid=(4, rows // tr),
            in_specs=[pl.BlockSpec((1, None, tr, cols), lambda ch, i, c_ref: (ch, c_ref[0], i, 0)), blk],
            out_specs=blk),
        out_shape=jax.ShapeDtypeStruct(got.shape, BF16),
        compiler_params=pltpu.CompilerParams(dimension_semantics=("parallel", "parallel"),
                                             vmem_limit_bytes=VMEM_LIMIT_BYTES),
    )(core, both, got)


def _adamw(name, gparts, w, m, v):
    layers = len(gparts)
    _, rows, cols = gparts[0].shape
    tr = _row_tile(rows, cols, sum(gp.shape[0] for gp in gparts))
    c1 = 1.0 / (1.0 - ADAM_B1 ** ADAM_STEP)
    c2 = 1.0 / (1.0 - ADAM_B2 ** ADAM_STEP)

    def body(*refs):
        gp_refs = refs[:layers]
        w_ref, m_ref, v_ref, g_ref, d_ref, nm_ref, nv_ref = refs[layers:]
        layer = pl.program_id(0)
        g = None
        for k, gp_ref in enumerate(gp_refs):
            gk = gp_ref[0].astype(F32)
            for i in range(1, gp_ref.shape[0]):
                gk = gk + gp_ref[i].astype(F32)
            g = gk if g is None else jnp.where(layer == k, gk, g)
        nm = ADAM_B1 * m_ref[...] + (1.0 - ADAM_B1) * g
        nv = ADAM_B2 * v_ref[...] + (1.0 - ADAM_B2) * (g * g)
        m_hat = nm * c1
        v_hat = nv * c2
        g_ref[...] = g
        nm_ref[...] = nm
        nv_ref[...] = nv
        d_ref[...] = -ADAM_LR * (m_hat / (jnp.sqrt(v_hat) + ADAM_EPS) + ADAM_WD * w_ref[...])

    row = pl.BlockSpec((None, tr, cols), lambda l, i: (l, i, 0))
    return pl.pallas_call(
        body,
        name=name,
        grid=(layers, rows // tr),
        in_specs=[pl.BlockSpec((gp.shape[0], tr, cols), lambda l, i: (0, i, 0)) for gp in gparts] + [row, row, row],
        out_specs=[row] * 4,
        out_shape=[jax.ShapeDtypeStruct((layers, rows, cols), F32)] * 4,
        compiler_params=pltpu.CompilerParams(dimension_semantics=("parallel", "parallel"),
                                             vmem_limit_bytes=VMEM_LIMIT_BYTES),
    )(*gparts, w, m, v)


def _tile_rows(rows, pref):
    t = min(pref, rows) // 8 * 8
    while t >= 8 and rows % t:
        t -= 8
    return t if t >= 8 else rows


PACK_ROWS = 512


def _pack(arrs, dtype):
    flat = jnp.concatenate([a.astype(dtype).reshape(-1) for a in arrs])
    quantum = PACK_ROWS * LANES
    padded = -(-flat.shape[0] // quantum) * quantum
    return jnp.pad(flat, (0, padded - flat.shape[0])).reshape(-1, LANES)


def _unpack(buf, shapes, lead=()):
    flat = buf.reshape(lead + (-1,))
    out, off = [], 0
    for shp in shapes:
        n = math.prod(shp)
        out.append(flat[..., off:off + n].reshape(lead + tuple(shp)))
        off += n
    return out


def _dest_pieces(name, g):
    if name == "w_branch":
        return jnp.moveaxis(g.reshape(3, 4, 2, D_MODEL // N_DEV, BRANCH), 0, 2)
    if name in SHARDED_F32_GATHER:
        return jnp.moveaxis(g.reshape(g.shape[0], 4, 2, -1), 0, 2)
    return g.reshape((4, 2, g.shape[0] // N_DEV) + g.shape[1:])


HALO = 8


def _rows_down(x, prev, k):
    xs = pltpu.roll(x, k, 0)
    row = lax.broadcasted_iota(jnp.int32, prev.shape, 0)
    top = jnp.where(row < k, pltpu.roll(prev, k, 0), xs[:HALO])
    return jnp.concatenate([top, xs[HALO:]], axis=0)


def _rows_up(x, nxt, k):
    rows = x.shape[0]
    xs = pltpu.roll(x, rows - k, 0)
    row = lax.broadcasted_iota(jnp.int32, nxt.shape, 0)
    bottom = jnp.where(row >= HALO - k, pltpu.roll(nxt, HALO - k, 0), xs[rows - HALO:])
    return jnp.concatenate([xs[:rows - HALO], bottom], axis=0)


def _halo_before(T, block_of, col=0):
    per = T // HALO
    return pl.BlockSpec((HALO, BRANCH), lambda t: (jnp.maximum(block_of(t) * per - 1, 0), col))


def _halo_after(T, block_of, S, col=0):
    per = T // HALO
    return pl.BlockSpec((HALO, BRANCH), lambda t: (jnp.minimum((block_of(t) + 1) * per, S // HALO - 1), col))


def _lru_fwd(u, lw, S):
    T = min(256, S)
    nb = S // T
    row = pl.BlockSpec((T, BRANCH), lambda t: (t, 0))
    vecs = [lw["cw0"], lw["cw1"], lw["cw2"], lw["cw3"], lw["conv_b"], lw["wa"], lw["wx"], lw["ba"], lw["bx"],
            lw["lam"]]

    def body(ax, ax_before, ay, cw0, cw1, cw2, cw3, cb, wa, wx, ba, bx, lam, xc_o, r_o, i_o, a_o, h_o, ya_o, hc):
        t = pl.program_id(0)

        @pl.when(t == 0)
        def _():
            hc[...] = jnp.zeros_like(hc)

        x = ax[...]
        before = jnp.where(t == 0, 0.0, ax_before[...])
        xc = (cw3[...] * x + cw2[...] * _rows_down(x, before, 1) + cw1[...] * _rows_down(x, before, 2)
              + cw0[...] * _rows_down(x, before, 3) + cb[...])
        r = _sigmoid(_dot(xc, wa[...], "nn") + ba[...])
        gi = _sigmoid(_dot(xc, wx[...], "nn") + bx[...])
        sp = _softplus(-lam[...])
        la = -LRU_C * r * sp
        a = jnp.exp(la)
        mult = jnp.sqrt(_neg_expm1(2.0 * la))
        A, B = _scan_rows(a, mult * gi * xc, T)
        h = B + A * hc[...]
        h_o[...] = h
        hc[...] = h_o[pl.ds(T - 1, 1), :]
        xc_o[...] = xc
        r_o[...] = r
        i_o[...] = gi
        a_o[...] = a
        gy, _ = _gelu_and_grad(ay[...])
        ya_o[...] = (gy * h).astype(ya_o.dtype)

    outs = pl.pallas_call(
        body,
        name="lru_fwd",
        grid=(nb,),
        in_specs=[pl.BlockSpec((T, BRANCH), lambda t: (t, U_AX // BRANCH)),
                  _halo_before(T, lambda t: t, U_AX // BRANCH),
                  pl.BlockSpec((T, BRANCH), lambda t: (t, U_AY // BRANCH))] + [_full_spec(v) for v in vecs],
        out_specs=[row] * 6,
        out_shape=[jax.ShapeDtypeStruct((S, BRANCH), F32)] * 5 + [jax.ShapeDtypeStruct((S, BRANCH), BF16)],
        scratch_shapes=[pltpu.VMEM((1, BRANCH), F32)],
        compiler_params=pltpu.CompilerParams(dimension_semantics=("arbitrary",), vmem_limit_bytes=VMEM_LIMIT_BYTES),
    )(u, u, u, *vecs)
    return outs


def _lru_bwd(dya, u, sv, lw, S):
    T = min(256, S)
    nb = S // T
    rrow = pl.BlockSpec((T, BRANCH), lambda t: (nb - 1 - t, 0))
    sq = pl.BlockSpec((BRANCH, BRANCH), lambda t: (0, 0))
    vrow = pl.BlockSpec((1, BRANCH), lambda t: (0, 0))

    def block(t):
        return nb - 1 - t

    def body(dya_r, ay, h, h_before, xc_r, r_r, i_r, a_r, a_after, wa, wx, lam,
             day_o, dxc_o, dwa_o, dwx_o, dba_o, dbx_o, dlam_o, lcar, tmp):
        t = pl.program_id(0)
        h_prev = _rows_down(h[...], jnp.where(t == nb - 1, 0.0, h_before[...]), 1)
        a_next = _rows_up(a_r[...], jnp.where(t == 0, 0.0, a_after[...]), 1)

        @pl.when(t == 0)
        def _():
            lcar[...] = jnp.zeros_like(lcar)
            dwa_o[...] = jnp.zeros_like(dwa_o)
            dwx_o[...] = jnp.zeros_like(dwx_o)
            dba_o[...] = jnp.zeros_like(dba_o)
            dbx_o[...] = jnp.zeros_like(dbx_o)
            dlam_o[...] = jnp.zeros_like(dlam_o)

        gy, dgy = _gelu_and_grad(ay[...])
        dy = dya_r[...]
        day_o[...] = (dy * h[...] * dgy).astype(day_o.dtype)
        A, B = _scan_rows(a_next, dy * gy, T, reverse=True)
        lmb = B + A * lcar[...]
        tmp[...] = lmb
        lcar[...] = tmp[pl.ds(0, 1), :]
        xc, r, gi, a = xc_r[...], r_r[...], i_r[...], a_r[...]
        sp = _softplus(-lam[...])
        la = -LRU_C * r * sp
        mult = jnp.sqrt(_neg_expm1(2.0 * la))
        da = lmb * h_prev
        dmult = lmb * gi * xc
        di = lmb * mult * xc
        dxc = lmb * mult * gi
        dla = da * a - dmult * a * a / mult
        dr = dla * (-LRU_C * sp)
        dlam_o[...] += _colsum(dla * (LRU_C * r)) * _sigmoid(-lam[...])
        dpr = dr * r * (1.0 - r)
        dpi = di * gi * (1.0 - gi)
        dba_o[...] += _colsum(dpr)
        dbx_o[...] += _colsum(dpi)
        dxc_o[...] = dxc + _dot(dpr, wa[...], "nt") + _dot(dpi, wx[...], "nt")
        dwa_o[...] += _dot(xc, dpr, "tn")
        dwx_o[...] += _dot(xc, dpi, "tn")

    outs = pl.pallas_call(
        body,
        name="lru_bwd",
        grid=(nb,),
        in_specs=[rrow, pl.BlockSpec((T, BRANCH), lambda t: (nb - 1 - t, U_AY // BRANCH)), rrow,
                  _halo_before(T, block), rrow, rrow, rrow, rrow, _halo_after(T, block, S), sq, sq, vrow],
        out_specs=[rrow, rrow, sq, sq, vrow, vrow, vrow],
        out_shape=[jax.ShapeDtypeStruct((S, BRANCH), BF16), jax.ShapeDtypeStruct((S, BRANCH), F32),
                   jax.ShapeDtypeStruct((BRANCH, BRANCH), F32), jax.ShapeDtypeStruct((BRANCH, BRANCH), F32),
                   jax.ShapeDtypeStruct((1, BRANCH), F32), jax.ShapeDtypeStruct((1, BRANCH), F32),
                   jax.ShapeDtypeStruct((1, BRANCH), F32)],
        scratch_shapes=[pltpu.VMEM((1, BRANCH), F32), pltpu.VMEM((T, BRANCH), F32)],
        compiler_params=pltpu.CompilerParams(dimension_semantics=("arbitrary",), vmem_limit_bytes=VMEM_LIMIT_BYTES),
    )(dya, u, sv["h"], sv["h"], sv["xc"], sv["r"], sv["i"], sv["a"], sv["a"], lw["wa"], lw["wx"], lw["lam"])
    return outs


def _conv_bwd(dxc, u, lw, S):
    T = min(256, S)
    nb = S // T
    vecs = [lw["cw0"], lw["cw1"], lw["cw2"], lw["cw3"]]
    vrow = pl.BlockSpec((1, BRANCH), lambda t: (0, 0))

    def body(d_r, d_after, ax, ax_before, cw0, cw1, cw2, cw3, dax_o, dcw0_o, dcw1_o, dcw2_o, dcw3_o, dcb_o):
        t = pl.program_id(0)
        d = d_r[...]
        after = jnp.where(t == nb - 1, 0.0, d_after[...])
        x = ax[...]
        before = jnp.where(t == 0, 0.0, ax_before[...])
        dax = (cw3[...] * d + cw2[...] * _rows_up(d, after, 1) + cw1[...] * _rows_up(d, after, 2)
               + cw0[...] * _rows_up(d, after, 3))
        dax_o[...] = dax.astype(dax_o.dtype)
        sums = [_colsum(d * _rows_down(x, before, 3)), _colsum(d * _rows_down(x, before, 2)),
                _colsum(d * _rows_down(x, before, 1)), _colsum(d * x), _colsum(d)]
        outs = [dcw0_o, dcw1_o, dcw2_o, dcw3_o, dcb_o]

        @pl.when(t == 0)
        def _():
            for o, val in zip(outs, sums):
                o[...] = val

        @pl.when(t > 0)
        def _():
            for o, val in zip(outs, sums):
                o[...] += val

    res = pl.pallas_call(
        body,
        name="conv_bwd",
        grid=(nb,),
        in_specs=[pl.BlockSpec((T, BRANCH), lambda t: (t, 0)), _halo_after(T, lambda t: t, S),
                  pl.BlockSpec((T, BRANCH), lambda t: (t, U_AX // BRANCH)),
                  _halo_before(T, lambda t: t, U_AX // BRANCH)] + [_full_spec(v) for v in vecs],
        out_specs=[pl.BlockSpec((T, BRANCH), lambda t: (t, 0))] + [vrow] * 5,
        out_shape=[jax.ShapeDtypeStruct((S, BRANCH), BF16)] + [jax.ShapeDtypeStruct((1, BRANCH), F32)] * 5,
        compiler_params=pltpu.CompilerParams(dimension_semantics=("arbitrary",), vmem_limit_bytes=VMEM_LIMIT_BYTES),
    )(dxc, dxc, u, u, *vecs)
    return res[0], res[1:]


GLA_QK = GLA_HEADS * GLA_DK
GLA_V = GLA_HEADS * GLA_DV
GLA_SCALE = GLA_DK ** -0.5


def _gla_specs(TB, rev_nb=None):
    def rmap(t):
        return t if rev_nb is None else rev_nb - 1 - t
    return [
        pl.BlockSpec((TB, GLA_QK), lambda t: (rmap(t), U_BQ // GLA_QK)),
        pl.BlockSpec((TB, GLA_QK), lambda t: (rmap(t), U_BK // GLA_QK)),
        pl.BlockSpec((TB, GLA_V), lambda t: (rmap(t), U_BV // GLA_V)),
        pl.BlockSpec((TB, GLA_V), lambda t: (rmap(t), U_BR // GLA_V)),
        pl.BlockSpec((TB, LANES), lambda t: (rmap(t), U_BLOW // LANES)),
    ]


def _gla_gates(gl, wg2, bg, TB):
    pre = _dot(gl, wg2, "nn") + bg
    la = _log_sigmoid(pre) * (1.0 / GLA_TAU)
    _, gc = _scan_rows(None, la, TB, seg=CHUNK)
    return pre, la, gc


def _gla_fwd(u, gw, S):
    TB = min(512, S)
    nb = S // TB
    cpb = TB // CHUNK
    vecs = [gw["wg2"], gw["bg"], gw["ng"], gw["bd"]]

    def body(q_r, k_r, v_r, br_r, gl_r, wg2, bg, ng, bd, yb_o, oraw_o, st_o, st):
        t = pl.program_id(0)

        @pl.when(t == 0)
        def _():
            st[...] = jnp.zeros_like(st)

        _, la, gc = _gla_gates(gl_r[...], wg2[...], bg[...], TB)
        for c in range(cpb):
            sl = slice(c * CHUNK, (c + 1) * CHUNK)
            gt = _colsum(la[sl])
            kdec = k_r[sl, :] * jnp.exp(gt - gc[sl])
            d_t = _dot(v_r[sl, :], kdec, "tn") * bd[...]
            s_new = st[...] * jnp.exp(gt) + d_t
            st[...] = s_new
            st_o[c] = s_new
            oraw_o[sl, :] = _dot(q_r[sl, :] * GLA_SCALE, s_new, "nt")
        for h in range(GLA_HEADS):
            hs = slice(h * GLA_DV, (h + 1) * GLA_DV)
            oh = oraw_o[:, hs]
            on = oh * lax.rsqrt(jnp.mean(oh * oh, axis=-1, keepdims=True) + RMS_EPS)
            sil, _ = _silu_and_grad(br_r[:, hs])
            yb_o[:, hs] = (on * ng[:, hs] * sil).astype(yb_o.dtype)

    return pl.pallas_call(
        body,
        name="gla_fwd",
        grid=(nb,),
        in_specs=_gla_specs(TB) + [_full_spec(v) for v in vecs],
        out_specs=[pl.BlockSpec((TB, GLA_V), lambda t: (t, 0)), pl.BlockSpec((TB, GLA_V), lambda t: (t, 0)),
                   pl.BlockSpec((cpb, GLA_V, GLA_QK), lambda t: (t, 0, 0))],
        out_shape=[jax.ShapeDtypeStruct((S, GLA_V), BF16), jax.ShapeDtypeStruct((S, GLA_V), F32),
                   jax.ShapeDtypeStruct((S // CHUNK, GLA_V, GLA_QK), F32)],
        scratch_shapes=[pltpu.VMEM((GLA_V, GLA_QK), F32)],
        compiler_params=pltpu.CompilerParams(dimension_semantics=("arbitrary",), vmem_limit_bytes=VMEM_LIMIT_BYTES),
    )(u, u, u, u, u, *vecs)


def _gla_bwd(dyb, u, oraw, states, gw, S):
    TB = min(512, S)
    nb = S // TB
    cpb = TB // CHUNK
    vecs = [gw["wg2"], gw["bg"], gw["ng"], gw["bd"]]

    def rrow(width):
        return pl.BlockSpec((TB, width), lambda t: (nb - 1 - t, 0))

    def body(dyb_r, oraw_r, q_r, k_r, v_r, br_r, gl_r, st_r, sp_r, wg2, bg, ng, bd,
             dq_o, dk_o, dv_o, dbr_o, dgl_o, dwg2_o, dbg_o, dng_o, dcar, do_buf, dla_buf):
        t = pl.program_id(0)
        blk = nb - 1 - t

        @pl.when(t == 0)
        def _():
            dcar[...] = jnp.zeros_like(dcar)
            dwg2_o[...] = jnp.zeros_like(dwg2_o)
            dbg_o[...] = jnp.zeros_like(dbg_o)
            dng_o[...] = jnp.zeros_like(dng_o)

        pre, la, gc = _gla_gates(gl_r[...], wg2[...], bg[...], TB)
        for h in range(GLA_HEADS):
            hs = slice(h * GLA_DV, (h + 1) * GLA_DV)
            oh = oraw_r[:, hs]
            rs = lax.rsqrt(jnp.mean(oh * oh, axis=-1, keepdims=True) + RMS_EPS)
            on = oh * rs
            sil, dsil = _silu_and_grad(br_r[:, hs])
            dy = dyb_r[:, hs]
            dbr_o[:, hs] = (dy * on * ng[:, hs] * dsil).astype(dbr_o.dtype)
            don = dy * ng[:, hs] * sil
            dng_o[:, hs] += _colsum(dy * on * sil)
            do_buf[:, hs] = rs * (don - on * jnp.mean(don * on, axis=-1, keepdims=True))
        first = jnp.where(blk == 0, 0.0, 1.0)
        for c in reversed(range(cpb)):
            sl = slice(c * CHUNK, (c + 1) * CHUNK)
            s_n = st_r[c]
            s_prev = st_r[c - 1] if c > 0 else sp_r[0] * first
            gt = _colsum(la[sl])
            w = jnp.exp(gt - gc[sl])
            k_c = k_r[sl, :]
            kdec = k_c * w
            qs = q_r[sl, :] * GLA_SCALE
            do_c = do_buf[sl, :]
            dq_o[sl, :] = (_dot(do_c, s_n, "nn") * GLA_SCALE).astype(dq_o.dtype)
            d_n = _dot(do_c, qs, "tn") * bd[...] + dcar[...]
            dv_o[sl, :] = _dot(kdec, d_n, "nt").astype(dv_o.dtype)
            dkdec = _dot(v_r[sl, :], d_n, "nn")
            dk_o[sl, :] = (dkdec * w).astype(dk_o.dtype)
            tt = dkdec * kdec
            e = jnp.exp(gt)
            dgt = _colsum(tt) + _colsum(d_n * s_prev) * e
            _, rc = _scan_rows(None, -tt, CHUNK, reverse=True)
            dla_buf[sl, :] = rc + dgt
            dcar[...] = d_n * e
        dpre = dla_buf[...] * _sigmoid(-pre) * (1.0 / GLA_TAU)
        dbg_o[...] += _colsum(dpre)
        dgl_o[...] = _dot(dpre, wg2[...], "nt").astype(dgl_o.dtype)
        dwg2_o[...] += _dot(gl_r[...], dpre, "tn")

    return pl.pallas_call(
        body,
        name="gla_bwd",
        grid=(nb,),
        in_specs=[rrow(GLA_V), rrow(GLA_V)] + _gla_specs(TB, rev_nb=nb)
        + [pl.BlockSpec((cpb, GLA_V, GLA_QK), lambda t: (nb - 1 - t, 0, 0)),
           pl.BlockSpec((1, GLA_V, GLA_QK), lambda t: (jnp.maximum((nb - 1 - t) * cpb - 1, 0), 0, 0))]
        + [_full_spec(v) for v in vecs],
        out_specs=[rrow(GLA_QK), rrow(GLA_QK), rrow(GLA_V), rrow(GLA_V), rrow(LANES),
                   pl.BlockSpec((LANES, GLA_QK), lambda t: (0, 0)), pl.BlockSpec((1, GLA_QK), lambda t: (0, 0)),
                   pl.BlockSpec((1, GLA_V), lambda t: (0, 0))],
        out_shape=[jax.ShapeDtypeStruct((S, GLA_QK), BF16), jax.ShapeDtypeStruct((S, GLA_QK), BF16),
                   jax.ShapeDtypeStruct((S, GLA_V), BF16), jax.ShapeDtypeStruct((S, GLA_V), BF16),
                   jax.ShapeDtypeStruct((S, LANES), BF16), jax.ShapeDtypeStruct((LANES, GLA_QK), F32),
                   jax.ShapeDtypeStruct((1, GLA_QK), F32), jax.ShapeDtypeStruct((1, GLA_V), F32)],
        scratch_shapes=[pltpu.VMEM((GLA_V, GLA_QK), F32), pltpu.VMEM((TB, GLA_V), F32),
                        pltpu.VMEM((TB, GLA_QK), F32)],
        compiler_params=pltpu.CompilerParams(dimension_semantics=("arbitrary",), vmem_limit_bytes=VMEM_LIMIT_BYTES),
    )(dyb, oraw, u, u, u, u, u, states, states, *vecs)


FOX_SCALE = FOX_DH ** -0.5


def _fox_gate_fwd(u, bfp, S):
    T = min(512, S)

    def body(f_r, b_r, fc_o, car):
        t = pl.program_id(0)

        @pl.when(t == 0)
        def _():
            car[...] = jnp.zeros_like(car)

        _, cs = _scan_rows(None, _log_sigmoid(f_r[...] + b_r[...]), T)
        fc_o[...] = cs + car[...]
        car[...] = fc_o[pl.ds(T - 1, 1), :]

    return pl.pallas_call(
        body,
        name="fox_gate_fwd",
        grid=(S // T,),
        in_specs=[pl.BlockSpec((T, LANES), lambda t: (t, U_CF // LANES)), _full_spec(bfp)],
        out_specs=pl.BlockSpec((T, LANES), lambda t: (t, 0)),
        out_shape=jax.ShapeDtypeStruct((S, LANES), F32),
        scratch_shapes=[pltpu.VMEM((1, LANES), F32)],
        compiler_params=pltpu.CompilerParams(dimension_semantics=("arbitrary",), vmem_limit_bytes=VMEM_LIMIT_BYTES),
    )(u, bfp)


def _fox_gate_bwd(dfc, u, bfp, S):
    T = min(512, S)
    nb = S // T

    def body(d_r, f_r, b_r, df_o, db_o, car, tmp):
        t = pl.program_id(0)

        @pl.when(t == 0)
        def _():
            car[...] = jnp.zeros_like(car)
            db_o[...] = jnp.zeros_like(db_o)

        _, rc = _scan_rows(None, d_r[...], T, reverse=True)
        tmp[...] = rc + car[...]
        car[...] = tmp[pl.ds(0, 1), :]
        df = tmp[...] * _sigmoid(-(f_r[...] + b_r[...]))
        df_o[...] = df.astype(df_o.dtype)
        db_o[...] += _colsum(df)

    return pl.pallas_call(
        body,
        name="fox_gate_bwd",
        grid=(nb,),
        in_specs=[pl.BlockSpec((T, LANES), lambda t: (nb - 1 - t, 0)),
                  pl.BlockSpec((T, LANES), lambda t: (nb - 1 - t, U_CF // LANES)), _full_spec(bfp)],
        out_specs=[pl.BlockSpec((T, LANES), lambda t: (nb - 1 - t, 0)), pl.BlockSpec((1, LANES), lambda t: (0, 0))],
        out_shape=[jax.ShapeDtypeStruct((S, LANES), BF16), jax.ShapeDtypeStruct((1, LANES), F32)],
        scratch_shapes=[pltpu.VMEM((1, LANES), F32), pltpu.VMEM((T, LANES), F32)],
        compiler_params=pltpu.CompilerParams(dimension_semantics=("arbitrary",), vmem_limit_bytes=VMEM_LIMIT_BYTES),
    )(dfc, u, bfp)


def _fox_scores(q, k, fq, fk, qi, ki, tq, tk):
    s = _dot(q, k, "nt") * FOX_SCALE + (fq - fk)
    rows = lax.broadcasted_iota(jnp.int32, (tq, tk), 0) + qi * tq
    cols = lax.broadcasted_iota(jnp.int32, (tq, tk), 1) + ki * tk
    return jnp.where(cols <= rows, s, NEG_BIG)


def _fox_fwd(qh, kh, vh, fq, fk, S):
    tq = tk = min(512, S)
    nq, nk = S // tq, S // tk

    def body(q_r, k_r, v_r, fq_r, fk_r, o_o, lse_o, m_s, l_s, acc):
        qi, ki = pl.program_id(1), pl.program_id(2)

        @pl.when(ki == 0)
        def _():
            m_s[...] = jnp.full_like(m_s, NEG_BIG)
            l_s[...] = jnp.zeros_like(l_s)
            acc[...] = jnp.zeros_like(acc)

        @pl.when(ki <= qi)
        def _():
            s = _fox_scores(q_r[0], k_r[0], fq_r[0], fk_r[0], qi, ki, tq, tk)
            m_new = jnp.maximum(m_s[...], jnp.max(s, axis=-1, keepdims=True))
            p = jnp.exp(s - m_new)
            alpha = jnp.exp(m_s[...] - m_new)
            l_s[...] = alpha * l_s[...] + jnp.sum(p, axis=-1, keepdims=True)
            acc[...] = alpha * acc[...] + _dot(p, v_r[0], "nn")
            m_s[...] = m_new

        @pl.when(ki == nk - 1)
        def _():
            o_o[0] = acc[...] / l_s[...]
            lse_o[0] = m_s[...] + jnp.log(l_s[...])

    kv = pl.BlockSpec((1, tk, FOX_DH), lambda h, i, j: (h, jnp.minimum(j, i), 0))
    return pl.pallas_call(
        body,
        name="fox_fwd",
        grid=(FOX_HEADS, nq, nk),
        in_specs=[pl.BlockSpec((1, tq, FOX_DH), lambda h, i, j: (h, i, 0)), kv, kv,
                  pl.BlockSpec((1, tq, 1), lambda h, i, j: (h, i, 0)),
                  pl.BlockSpec((1, 1, tk), lambda h, i, j: (h, 0, jnp.minimum(j, i)))],
        out_specs=[pl.BlockSpec((1, tq, FOX_DH), lambda h, i, j: (h, i, 0)),
                   pl.BlockSpec((1, tq, 1), lambda h, i, j: (h, i, 0))],
        out_shape=[jax.ShapeDtypeStruct((FOX_HEADS, S, FOX_DH), F32), jax.ShapeDtypeStruct((FOX_HEADS, S, 1), F32)],
        scratch_shapes=[pltpu.VMEM((tq, 1), F32), pltpu.VMEM((tq, 1), F32), pltpu.VMEM((tq, FOX_DH), F32)],
        compiler_params=pltpu.CompilerParams(
            dimension_semantics=("parallel", "parallel", "arbitrary"), vmem_limit_bytes=VMEM_LIMIT_BYTES),
    )(qh, kh, vh, fq, fk)


def _fox_bwd_dq(qh, kh, vh, fq, fk, o, do, lse, S):
    tq = tk = min(512, S)
    nq, nk = S // tq, S // tk

    def body(q_r, k_r, v_r, fq_r, fk_r, o_r, do_r, lse_r, dq_o, dfq_o, dq_acc, df_acc):
        qi, ki = pl.program_id(1), pl.program_id(2)

        @pl.when(ki == 0)
        def _():
            dq_acc[...] = jnp.zeros_like(dq_acc)
            df_acc[...] = jnp.zeros_like(df_acc)

        @pl.when(ki <= qi)
        def _():
            s = _fox_scores(q_r[0], k_r[0], fq_r[0], fk_r[0], qi, ki, tq, tk)
            p = jnp.exp(s - lse_r[0])
            do_t = do_r[0]
            delta = jnp.sum(o_r[0] * do_t, axis=-1, keepdims=True)
            ds = p * (_dot(do_t, v_r[0], "nt") - delta)
            dq_acc[...] += _dot(ds, k_r[0], "nn")
            df_acc[...] += jnp.sum(ds, axis=-1, keepdims=True)

        @pl.when(ki == nk - 1)
        def _():
            dq_o[0] = dq_acc[...] * FOX_SCALE
            dfq_o[0] = df_acc[...]

    qrow = pl.BlockSpec((1, tq, FOX_DH), lambda h, i, j: (h, i, 0))
    qcol = pl.BlockSpec((1, tq, 1), lambda h, i, j: (h, i, 0))
    kv = pl.BlockSpec((1, tk, FOX_DH), lambda h, i, j: (h, jnp.minimum(j, i), 0))
    return pl.pallas_call(
        body,
        name="fox_bwd_dq",
        grid=(FOX_HEADS, nq, nk),
        in_specs=[qrow, kv, kv, qcol, pl.BlockSpec((1, 1, tk), lambda h, i, j: (h, 0, jnp.minimum(j, i))),
                  qrow, qrow, qcol],
        out_specs=[qrow, qcol],
        out_shape=[jax.ShapeDtypeStruct((FOX_HEADS, S, FOX_DH), F32), jax.ShapeDtypeStruct((FOX_HEADS, S, 1), F32)],
        scratch_shapes=[pltpu.VMEM((tq, FOX_DH), F32), pltpu.VMEM((tq, 1), F32)],
        compiler_params=pltpu.CompilerParams(
            dimension_semantics=("parallel", "parallel", "arbitrary"), vmem_limit_bytes=VMEM_LIMIT_BYTES),
    )(qh, kh, vh, fq, fk, o, do, lse)


def _fox_bwd_dkv(qh, kh, vh, fq, fk, o, do, lse, S):
    tq = tk = min(512, S)
    nq, nk = S // tq, S // tk

    def body(q_r, k_r, v_r, fq_r, fk_r, o_r, do_r, lse_r, dk_o, dv_o, dfk_o, dk_acc, dv_acc, df_acc):
        ki, qi = pl.program_id(1), pl.program_id(2)

        @pl.when(qi == 0)
        def _():
            dk_acc[...] = jnp.zeros_like(dk_acc)
            dv_acc[...] = jnp.zeros_like(dv_acc)
            df_acc[...] = jnp.zeros_like(df_acc)

        @pl.when(qi >= ki)
        def _():
            s = _fox_scores(q_r[0], k_r[0], fq_r[0], fk_r[0], qi, ki, tq, tk)
            p = jnp.exp(s - lse_r[0])
            do_t = do_r[0]
            delta = jnp.sum(o_r[0] * do_t, axis=-1, keepdims=True)
            ds = p * (_dot(do_t, v_r[0], "nt") - delta)
            dv_acc[...] += _dot(p, do_t, "tn")
            dk_acc[...] += _dot(ds, q_r[0], "tn")
            df_acc[...] += _colsum(ds)

        @pl.when(qi == nq - 1)
        def _():
            dk_o[0] = dk_acc[...] * FOX_SCALE
            dv_o[0] = dv_acc[...]
            dfk_o[0] = df_acc[...]

    qrow = pl.BlockSpec((1, tq, FOX_DH), lambda h, j, i: (h, jnp.maximum(i, j), 0))
    qcol = pl.BlockSpec((1, tq, 1), lambda h, j, i: (h, jnp.maximum(i, j), 0))
    kv = pl.BlockSpec((1, tk, FOX_DH), lambda h, j, i: (h, j, 0))
    krow = pl.BlockSpec((1, 1, tk), lambda h, j, i: (h, 0, j))
    return pl.pallas_call(
        body,
        name="fox_bwd_dkv",
        grid=(FOX_HEADS, nk, nq),
        in_specs=[qrow, kv, kv, qcol, krow, qrow, qrow, qcol],
        out_specs=[kv, kv, krow],
        out_shape=[jax.ShapeDtypeStruct((FOX_HEADS, S, FOX_DH), F32), jax.ShapeDtypeStruct((FOX_HEADS, S, FOX_DH), F32),
                   jax.ShapeDtypeStruct((FOX_HEADS, 1, S), F32)],
        scratch_shapes=[pltpu.VMEM((tk, FOX_DH), F32), pltpu.VMEM((tk, FOX_DH), F32), pltpu.VMEM((1, tk), F32)],
        compiler_params=pltpu.CompilerParams(
            dimension_semantics=("parallel", "parallel", "arbitrary"), vmem_limit_bytes=VMEM_LIMIT_BYTES),
    )(qh, kh, vh, fq, fk, o, do, lse)


def _fox_call(name, body, tables, grid, in_specs, out_specs, out_shape, scratch, args, side):
    n_in, n_out, n_scr = len(in_specs), len(out_specs), len(scratch)
    semantics = ("parallel", "arbitrary")
    if side is not None:
        total = grid[0] * grid[1]
        phases = side["phases"]
        triggers = [0, total - 1] if len(phases) == 2 else [0, total * 7 // 10, total - 1]
        na, no = len(side["arrs"]), len(side["out_shapes"])
        s_in, s_out, s_sems = _side_specs(side)
        kernel_body = body

        def body(*refs):
            tabs, rest = refs[:len(tables)], refs[len(tables):]
            ins, s_ins = rest[:n_in], rest[n_in:n_in + na]
            rest = rest[n_in + na:]
            outs, s_outs = rest[:n_out], rest[n_out:n_out + no]
            rest = rest[n_out + no:]
            scr, sems = rest[:n_scr], rest[n_scr:]
            flat = pl.program_id(0) * grid[1] + pl.program_id(1)
            for trigger, phase in zip(triggers[:-1], phases[:-1]):
                @pl.when(flat == trigger)
                def _(phase=phase):
                    phase(s_ins, s_outs, *sems)
            kernel_body(*tabs, *ins, *outs, *scr)

            @pl.when(flat == triggers[-1])
            def _():
                phases[-1](s_ins, s_outs, *sems)

        in_specs, out_specs = in_specs + s_in, out_specs + s_out
        out_shape, scratch = out_shape + side["out_shapes"], scratch + s_sems
        args = list(args) + side["arrs"]
        semantics = ("arbitrary", "arbitrary")
    res = pl.pallas_call(
        body,
        name=name,
        grid_spec=pltpu.PrefetchScalarGridSpec(num_scalar_prefetch=len(tables), grid=grid, in_specs=in_specs,
                                               out_specs=out_specs, scratch_shapes=scratch),
        out_shape=out_shape,
        compiler_params=pltpu.CompilerParams(dimension_semantics=semantics, vmem_limit_bytes=VMEM_LIMIT_BYTES),
    )(*tables, *args)
    return res[:n_out], res[n_out:]


FOX_TILE = 1024
FOX_GROUP = 2
FOX_AUG = 128
FOX_ONES = 3


def _fox_pairs(n, by_key):
    pairs = [(qi, ki) for qi in range(n) for ki in range(qi + 1)]
    if by_key:
        pairs.sort(key=lambda qk: (qk[1], qk[0]))
    qs = jnp.asarray([qk[0] for qk in pairs], jnp.int32)
    ks = jnp.asarray([qk[1] for qk in pairs], jnp.int32)
    return qs, ks


def _fox_augment(q, k, fcum):
    S = q.shape[0]
    def to_bf16_grid(a):
        return lax.reduce_precision(a, exponent_bits=8, mantissa_bits=7)

    hi = to_bf16_grid(fcum)
    mid = to_bf16_grid(fcum - hi)
    lo = to_bf16_grid(fcum - hi - mid)
    f3 = jnp.stack([hi, mid, lo], axis=-1).astype(BF16)
    ones = jnp.ones((S, FOX_HEADS, FOX_ONES), BF16)
    pad = jnp.zeros((S, FOX_HEADS, FOX_AUG - FOX_DH - 2 * FOX_ONES), BF16)
    q_aug = jnp.concatenate([(q * FOX_SCALE).astype(BF16), ones, f3, pad], axis=-1)
    k_aug = jnp.concatenate([k.astype(BF16), -f3, ones, pad], axis=-1)
    return jnp.transpose(q_aug, (1, 0, 2)), jnp.transpose(k_aug, (1, 0, 2))


def _fox_causal(sT):
    keys = lax.broadcasted_iota(jnp.int32, sT.shape, 0)
    queries = lax.broadcasted_iota(jnp.int32, sT.shape, 1)
    return jnp.where(keys <= queries, sT, NEG_BIG)


def _fox_fwd(qT, ka, vT, S, side=None):
    t = min(FOX_TILE, S)
    n = S // t
    qi_tab, ki_tab = _fox_pairs(n, by_key=False)

    G = FOX_GROUP

    def body(qi_ref, ki_ref, qT_r, ka_r, vT_r, oT_o, lse_o, m_s, l_s, acc):
        step = pl.program_id(1)
        qi, ki = qi_ref[step], ki_ref[step]

        @pl.when(ki == 0)
        def _():
            m_s[...] = jnp.full_like(m_s, NEG_BIG)
            l_s[...] = jnp.zeros_like(l_s)
            acc[...] = jnp.zeros_like(acc)

        def update(g, masked):
            sT = _dot(ka_r[g], qT_r[g], "nn")
            if masked:
                sT = _fox_causal(sT)
            m_new = jnp.maximum(m_s[g], jnp.max(sT, axis=0, keepdims=True))
            p = jnp.exp(sT - m_new)
            alpha = jnp.exp(m_s[g] - m_new)
            l_s[g] = alpha * l_s[g] + jnp.sum(p, axis=0, keepdims=True)
            acc[g] = alpha * acc[g] + _dot(vT_r[g], p, "nn")
            m_s[g] = m_new

        @pl.when(ki < qi)
        def _():
            for g in range(G):
                update(g, False)

        @pl.when(ki == qi)
        def _():
            for g in range(G):
                update(g, True)
                oT_o[g] = acc[g] / l_s[g]
                lse_o[g] = m_s[g] + jnp.log(l_s[g])

    return _fox_call(
        "fox_fwd", body, (qi_tab, ki_tab), (FOX_HEADS // G, int(qi_tab.shape[0])),
        [pl.BlockSpec((G, FOX_AUG, t), lambda h, s, qt, kt: (h, 0, qt[s])),
         pl.BlockSpec((G, t, FOX_AUG), lambda h, s, qt, kt: (h, kt[s], 0)),
         pl.BlockSpec((G, FOX_DH, t), lambda h, s, qt, kt: (h, 0, kt[s]))],
        [pl.BlockSpec((G, FOX_DH, t), lambda h, s, qt, kt: (h, 0, qt[s])),
         pl.BlockSpec((G, 1, t), lambda h, s, qt, kt: (h, 0, qt[s]))],
        [jax.ShapeDtypeStruct((FOX_HEADS, FOX_DH, S), F32), jax.ShapeDtypeStruct((FOX_HEADS, 1, S), F32)],
        [pltpu.VMEM((G, 1, t), F32), pltpu.VMEM((G, 1, t), F32), pltpu.VMEM((G, FOX_DH, t), F32)],
        (qT, ka, vT), side)


FOX_BIAS_ROWS = 8


def _fox_bwd(qT, qa, ka, kT, v, do, doT, oT, lse, S, side=None):
    t = min(FOX_TILE, S)
    n = S // t
    qi_tab, ki_tab = _fox_pairs(n, by_key=True)
    n_steps = int(qi_tab.shape[0])
    slab = slice(FOX_DH, FOX_DH + FOX_BIAS_ROWS)

    G = FOX_GROUP

    def body(qi_ref, ki_ref, qT_r, qa_r, ka_r, kT_r, v_r, do_r, doT_r, oT_r, lse_r,
             dq_o, dfq_o, dk_o, dfk_o, dv_o, dq_acc, dk_acc, dv_acc):
        step = pl.program_id(1)
        qi, ki = qi_ref[step], ki_ref[step]

        @pl.when(step == 0)
        def _():
            dq_acc[...] = jnp.zeros_like(dq_acc)

        @pl.when(qi == ki)
        def _():
            dk_acc[...] = jnp.zeros_like(dk_acc)
            dv_acc[...] = jnp.zeros_like(dv_acc)

        def update(g, masked):
            sT = _dot(ka_r[g], qT_r[g], "nn")
            if masked:
                sT = _fox_causal(sT)
            pT = jnp.exp(sT - lse_r[g])
            delta = jnp.sum(oT_r[g] * doT_r[g], axis=0, keepdims=True)
            dsT = pT * (_dot(v_r[g], doT_r[g], "nn") - delta)
            dv_acc[g] += _dot(pT, do_r[g], "nn")
            dk_acc[g] += _dot(dsT, qa_r[g], "nn")
            dq_acc[g, qi] += _dot(kT_r[g], dsT, "nn")

        @pl.when(qi > ki)
        def _():
            for g in range(G):
                update(g, False)

        @pl.when(qi == ki)
        def _():
            for g in range(G):
                update(g, True)

        @pl.when(qi == n - 1)
        def _():
            for g in range(G):
                dk = dk_acc[g]
                dk_o[g] = dk[:, :FOX_DH].astype(dk_o.dtype)
                dfk_o[g] = dk.T[slab]
                dv_o[g] = dv_acc[g].astype(dv_o.dtype)

        @pl.when(step == n_steps - 1)
        def _():
            for g in range(G):
                for j in range(n):
                    dqT = dq_acc[g, j]
                    dq_o[g, j * t:(j + 1) * t, :] = (dqT.T[:, :FOX_DH] * FOX_SCALE).astype(dq_o.dtype)
                    dfq_o[g, :, j * t:(j + 1) * t] = dqT[slab]

    def qlane(rows):
        return pl.BlockSpec((G, rows, t), lambda h, s, qt, kt: (h, 0, qt[s]))

    def qrow(cols):
        return pl.BlockSpec((G, t, cols), lambda h, s, qt, kt: (h, qt[s], 0))

    def krow(cols):
        return pl.BlockSpec((G, t, cols), lambda h, s, qt, kt: (h, kt[s], 0))

    def klane(rows):
        return pl.BlockSpec((G, rows, t), lambda h, s, qt, kt: (h, 0, kt[s]))

    def head(rows, cols):
        return pl.BlockSpec((G, rows, cols), lambda h, s, qt, kt: (h, 0, 0))

    return _fox_call(
        "fox_bwd", body, (qi_tab, ki_tab), (FOX_HEADS // G, n_steps),
        [qlane(FOX_AUG), qrow(FOX_AUG), krow(FOX_AUG), klane(FOX_AUG), krow(FOX_DH), qrow(FOX_DH), qlane(FOX_DH),
         qlane(FOX_DH), qlane(1)],
        [head(S, FOX_DH), head(FOX_BIAS_ROWS, S), krow(FOX_DH), klane(FOX_BIAS_ROWS), krow(FOX_DH)],
        [jax.ShapeDtypeStruct((FOX_HEADS, S, FOX_DH), BF16), jax.ShapeDtypeStruct((FOX_HEADS, FOX_BIAS_ROWS, S), F32),
         jax.ShapeDtypeStruct((FOX_HEADS, S, FOX_DH), BF16), jax.ShapeDtypeStruct((FOX_HEADS, FOX_BIAS_ROWS, S), F32),
         jax.ShapeDtypeStruct((FOX_HEADS, S, FOX_DH), BF16)],
        [pltpu.VMEM((G, n, FOX_AUG, t), F32), pltpu.VMEM((G, t, FOX_AUG), F32), pltpu.VMEM((G, t, FOX_DH), F32)],
        (qT, qa, ka, kT, v, do, doT, oT, lse), side)


def _fox_prep(u, fcum, S):
    T = min(512, S)
    head_of = jnp.arange(BRANCH) // FOX_DH
    dim_of = jnp.arange(BRANCH) % FOX_DH
    heads = jnp.arange(FOX_HEADS)[:, None, None]
    sel = (head_of[None, :, None] == heads) & (dim_of[None, :, None] == jnp.arange(FOX_AUG)[None, None, :])
    sel_q = (sel * FOX_SCALE).astype(BF16)
    sel_k = sel.astype(BF16)
    sel_vT = jnp.swapaxes(sel[:, :, :FOX_DH], 1, 2).astype(BF16)
    piece = jnp.arange(FOX_ONES * LANES) // LANES
    lane = jnp.arange(FOX_ONES * LANES) % LANES
    col = jnp.arange(FOX_AUG)[None, None, :]
    at_q = (lane[None, :, None] == heads) & (col == FOX_DH + FOX_ONES + piece[None, :, None])
    at_k = (lane[None, :, None] == heads) & (col == FOX_DH + piece[None, :, None])
    bias_q = at_q.astype(BF16)
    bias_k = (-at_k.astype(F32)).astype(BF16)
    cols = jnp.arange(FOX_AUG)[None, :]
    ones_q = ((cols >= FOX_DH) & (cols < FOX_DH + FOX_ONES)).astype(F32)
    ones_k = ((cols >= FOX_DH + FOX_ONES) & (cols < FOX_DH + 2 * FOX_ONES)).astype(F32)
    consts = [sel_q, sel_k, sel_vT, bias_q, bias_k, ones_q, ones_k]

    def body(cq, ck, cv, fc, sq, sk, svT, bq, bk, oq, ok, qa_o, ka_o, qT_o, kT_o, vh_o, vT_o):
        f = fc[...]
        hi = f.astype(BF16).astype(F32)
        mid = (f - hi).astype(BF16).astype(F32)
        lo = (f - hi - mid).astype(BF16).astype(F32)
        pieces = jnp.concatenate([hi, mid, lo], axis=1)
        q, k, v = cq[...], ck[...], cv[...]
        for h in range(FOX_HEADS):
            qa = _dot(q, sq[h], "nn") + _dot(pieces, bq[h], "nn") + oq[...]
            ka = _dot(k, sk[h], "nn") + _dot(pieces, bk[h], "nn") + ok[...]
            qa_o[h] = qa.astype(qa_o.dtype)
            ka_o[h] = ka.astype(ka_o.dtype)
            qT_o[h] = qa.T.astype(qT_o.dtype)
            kT_o[h] = ka.T.astype(kT_o.dtype)
            vT_o[h] = _dot(svT[h], v, "nt").astype(vT_o.dtype)
            vh_o[h] = _dot(v, svT[h], "nt").astype(vh_o.dtype)

    def win(off):
        return pl.BlockSpec((T, BRANCH), functools.partial(lambda i, blk: (i, blk), blk=off // BRANCH))

    def rows(c):
        return pl.BlockSpec((FOX_HEADS, T, c), lambda i: (0, i, 0))

    def lanes(r):
        return pl.BlockSpec((FOX_HEADS, r, T), lambda i: (0, 0, i))

    bf = lambda *shape: jax.ShapeDtypeStruct((FOX_HEADS,) + shape, BF16)
    return pl.pallas_call(
        body,
        name="fox_prep",
        grid=(S // T,),
        in_specs=[win(U_CQ), win(U_CK), win(U_CV), pl.BlockSpec((T, LANES), lambda i: (i, 0))]
        + [_full_spec(c) for c in consts],
        out_specs=[rows(FOX_AUG), rows(FOX_AUG), lanes(FOX_AUG), lanes(FOX_AUG), rows(FOX_DH), lanes(FOX_DH)],
        out_shape=[bf(S, FOX_AUG), bf(S, FOX_AUG), bf(FOX_AUG, S), bf(FOX_AUG, S), bf(S, FOX_DH), bf(FOX_DH, S)],
        compiler_params=pltpu.CompilerParams(dimension_semantics=("parallel",), vmem_limit_bytes=VMEM_LIMIT_BYTES),
    )(u, u, u, fcum, *consts)


def _to_heads(x2d, S):
    return jnp.transpose(x2d.reshape(S, FOX_HEADS, FOX_DH), (1, 0, 2))


def _from_heads(xh, S):
    return jnp.transpose(xh, (1, 0, 2)).reshape(S, FOX_HEADS * FOX_DH)


def _ffn_fwd(tag, x, wgT, wuT, wd, g, b, S):
    def up_epi(accs):
        gate, up = accs
        sil, _ = _silu_and_grad(gate)
        return [gate, up, sil * up]

    gate, up, act = _mm(tag + "_up", "nt", [x], [wgT, wuT], [(0, 0, 0), (1, 0, 1)], 2, up_epi, [],
                        [BF16, BF16, BF16], S, D_FF, D_MODEL, tn=1408)

    def down_epi(accs, xr, gg, bb):
        z = ALPHA * xr + 0.5 * accs[0]
        return [z, _ln_fwd(z, gg, bb)]

    z, xn = _mm(tag + "_down", "nn", [act], [wd], [(0, 0, 0)], 1, down_epi, [(x, "mn", 0), (g, "n"), (b, "n")],
                [F32, F32], S, D_MODEL, D_FF, tk=D_FF)
    return xn, dict(x=x, gate=gate, up=up, act=act, z=z)


def _ln_bwd_call(tag, dy, z, g, S):
    def fn(dy_t, z_t, g_t):
        dz, xhat = _ln_bwd(dy_t, z_t, g_t)
        return [dz], [_colsum(dy_t * xhat), _colsum(dy_t)]

    (dz,), (dg, db) = _rowwise(tag + "_ln_bwd", fn, [dy, z], [g], [(D_MODEL, F32)], [D_MODEL, D_MODEL], S)
    return dz, dg, db


def _ffn_bwd(tag, dxn, sv, wgT, wuT, wd, g, S, gdt=F32):
    dz, dg, db = _ln_bwd_call(tag, dxn, sv["z"], g, S)

    def act_epi(accs, gate, up):
        da = 0.5 * accs[0]
        sil, dsil = _silu_and_grad(gate.astype(F32))
        return [da * up.astype(F32) * dsil, da * sil]

    dgate, dup = _mm(tag + "_dact", "nt", [dz], [wd], [(0, 0, 0)], 1, act_epi,
                     [(sv["gate"], "mn", 0), (sv["up"], "mn", 0)], [BF16, BF16], S, D_FF, D_MODEL, tn=1408)
    dwd = _mm1(tag + "_dwd", "tn", sv["act"], dz, D_FF, D_MODEL, S, scale=0.5, tm=1408, out_dtype=gdt)

    def two(accs):
        return [accs[0], accs[1]]

    dwgT, dwuT = _mm(tag + "_dwup", "tn", [dgate, dup], [sv["x"]], [(0, 0, 0), (1, 1, 0)], 2, two, [], [gdt, gdt],
                     D_FF, D_MODEL, S, tm=1408, tk=512)

    def dx_epi(accs, dzr):
        return [accs[0] + ALPHA * dzr]

    (dx,) = _mm(tag + "_dx", "nn", [dgate, dup], [wgT, wuT], [(0, 0, 0), (0, 1, 1)], 1, dx_epi, [(dz, "mn", 0)],
                [F32], S, D_MODEL, D_FF, tm=1024, tk=1408)
    return dx, dict(w_upT=jnp.concatenate([dwgT, dwuT], axis=0), w_down=dwd, ln_g=dg, ln_b=db)


def _mixer_fwd(x1, w, S, side=None, on_side=None):
    u = _mm1("w_in", "nt", x1, w["w_inT_p"], S, U_WIDTH, D_MODEL, tm=1024, tn=1536)
    xc, r, gi, a, h, y_a = _lru_fwd(u, w["lru"], S)
    y_b, oraw, states = _gla_fwd(u, w["gla"], S)
    fcum = _fox_gate_fwd(u, w["bfp"], S)
    qa, ka, qT, kT, vh, vT = _fox_prep(u, fcum, S)
    (oT, lse), side_out = _fox_fwd(qT, ka, vT, S, side)
    if on_side is not None:
        on_side(side_out)
    y_c = jnp.transpose(oT, (2, 0, 1)).reshape(S, BRANCH).astype(BF16)

    def merge_epi(accs, g0, g1, g2):
        merged = _sigmoid(g0) * accs[0] + _sigmoid(g1) * accs[1] + _sigmoid(g2) * accs[2]
        return [accs[0], accs[1], accs[2], merged]

    wb = w["w_branchT"]
    yp0, yp1, yp2, merged = _mm(
        "merge", "nt", [y_a, y_b, y_c], [wb[0], wb[1], wb[2]], [(0, 0, 0), (1, 1, 1), (2, 2, 2)], 3, merge_epi,
        [(u, "mn", 0), (u, "mn", 1), (u, "mn", 2)], [F32, F32, F32, BF16], S, D_MODEL, BRANCH, tm=256)

    def out_epi(accs, xr, gg, bb):
        z = ALPHA * xr + accs[0]
        return [z, _ln_fwd(z, gg, bb)]

    z2, x2 = _mm("w_out", "nn", [merged], [w["w_out"]], [(0, 0, 0)], 1, out_epi,
                 [(x1, "mn", 0), (w["ln2_g"], "n"), (w["ln2_b"], "n")], [F32, F32], S, D_MODEL, D_MODEL)
    sv = dict(x=x1, u=u, xc=xc, r=r, i=gi, a=a, h=h, y_a=y_a, y_b=y_b, y_c=y_c, oraw=oraw,
              states=states, qT=qT, qa=qa, ka=ka, kT=kT, vh=vh, oT=oT, lse=lse, yp=(yp0, yp1, yp2), merged=merged,
              z=z2)
    return x2, sv, side_out


def _mixer_bwd(dx2, sv, w, S, make_side=None, gdt_a=F32, gdt_b=F32):
    u = sv["u"]
    dz, dg2, db2 = _ln_bwd_call("mix", dx2, sv["z"], w["ln2_g"], S)

    def dm_epi(accs, y0, y1, y2, g0, g1, g2):
        dm = accs[0]
        outs_p, outs_g = [], []
        for yp, gl in ((y0, g0), (y1, g1), (y2, g2)):
            sg = _sigmoid(gl)
            outs_p.append(dm * sg)
            outs_g.append(dm * yp * sg * (1.0 - sg))
        return outs_p + outs_g

    yp = sv["yp"]
    dyp0, dyp1, dyp2, dgl0, dgl1, dgl2 = _mm(
        "dmerged", "nt", [dz], [w["w_out"]], [(0, 0, 0)], 1, dm_epi,
        [(yp[0], "mn", 0), (yp[1], "mn", 0), (yp[2], "mn", 0), (u, "mn", 0), (u, "mn", 1), (u, "mn", 2)],
        [BF16] * 6, S, D_MODEL, D_MODEL, tm=256)
    dw_out = _mm1("dw_out", "tn", sv["merged"], dz, D_MODEL, D_MODEL, S, out_dtype=gdt_b)
    wb = w["w_branchT"]
    dys, dwbs = [], []
    for j, (yj, dyp) in enumerate(((sv["y_a"], dyp0), (sv["y_b"], dyp1), (sv["y_c"], dyp2))):
        dys.append(_mm1("dy_branch%d" % j, "nn", dyp, wb[j], S, BRANCH, D_MODEL))
        dwbs.append(_mm1("dw_branch%d" % j, "tn", dyp, yj, D_MODEL, BRANCH, S, out_dtype=gdt_b))
    day, dxc, dwa, dwx, dba, dbx, dlam = _lru_bwd(dys[0], u, sv, w["lru"], S)
    dax, (dcw0, dcw1, dcw2, dcw3, dcb) = _conv_bwd(dxc, u, w["lru"], S)
    dbq, dbk, dbv, dbr, dglow, dwg2p, dbg, dng = _gla_bwd(dys[1], u, sv["oraw"], sv["states"], w["gla"], S)
    doh = _to_heads(dys[2], S)
    dw_branchT = jnp.stack(dwbs)
    side = make_side(dict(w_out=dw_out, w_branchT=dw_branchT)) if make_side is not None else None
    (dqh, dfq, dkh, dfk, dvh), side_out = _fox_bwd(sv["qT"], sv["qa"], sv["ka"], sv["kT"], sv["vh"], doh,
                                                   jnp.swapaxes(doh, 1, 2), sv["oT"], sv["lse"], S, side)
    dfc = jnp.transpose(dfq[:, FOX_ONES, :] - dfk[:, 0, :])
    dfc = jnp.pad(dfc, ((0, 0), (0, LANES - FOX_HEADS)))
    dcf, dbf = _fox_gate_bwd(dfc, u, w["bfp"], S)
    du = jnp.concatenate(
        [dgl0, dgl1, dgl2, dax, day, dbq, dbk, dbv, dbr, _from_heads(dqh, S).astype(BF16),
         _from_heads(dkh, S).astype(BF16), _from_heads(dvh, S).astype(BF16), dglow, dcf,
         jnp.zeros((S, U_WIDTH - U_CF - LANES), BF16)], axis=1)
    dw_inT_p = _mm1("dw_in", "tn", du, sv["x"], U_WIDTH, D_MODEL, S, tm=1536, out_dtype=gdt_a)

    def dx_epi(accs, dzr):
        return [accs[0] + ALPHA * dzr]

    (dx1,) = _mm("dx_mix", "nn", [du], [w["w_inT_p"]], [(0, 0, 0)], 1, dx_epi, [(dz, "mn", 0)], [F32], S, D_MODEL,
                 U_WIDTH, tm=1024, tk=1536)
    pieces = sorted(W_IN_SEGMENTS)
    dw_inT = jnp.concatenate([dw_inT_p[dst:dst + width] for _, width, dst in pieces], axis=0)
    eye = jnp.eye(LRU_BLOCKS, dtype=F32)
    dwa_b = jnp.einsum("ncmd,nm->ncd", dwa.reshape(LRU_BLOCKS, 64, LRU_BLOCKS, 64), eye)
    dwx_b = jnp.einsum("ncmd,nm->ncd", dwx.reshape(LRU_BLOCKS, 64, LRU_BLOCKS, 64), eye)
    grads = dict(
        w_inT=dw_inT, w_out=dw_out, w_branchT=dw_branchT, ln2_g=dg2, ln2_b=db2,
        conv_w=jnp.concatenate([dcw0, dcw1, dcw2, dcw3], axis=0).astype(gdt_a), conv_b=dcb, lru_wa=dwa_b, lru_wx=dwx_b,
        lru_ba=dba, lru_bx=dbx, lru_lambda=dlam, gla_w_g2=dwg2p[:GLA_LOWRANK].astype(gdt_a), gla_b_g=dbg, gla_norm_g=dng,
        fox_b_f=dbf[:, :FOX_HEADS])
    return dx1, grads, side_out


def _ple_fwd(x3, p_i, w, S):
    pe = _mm1("ple_proj", "nt", p_i, w["ple_w_projT"], S, D_MODEL, PLE_DIM)

    def epi(accs, xr, per, bg, gg, bb):
        sg = _sigmoid(accs[0] + bg)
        z = ALPHA * xr + sg * per
        return [sg, z, _ln_fwd(z, gg, bb)]

    sg, z4, x4 = _mm("ple_gate", "nn", [x3], [w["ple_w_gate"]], [(0, 0, 0)], 1, epi,
                     [(x3, "mn", 0), (pe, "mn", 0), (w["ple_b_gate"], "n"), (w["ln4_g"], "n"), (w["ln4_b"], "n")],
                     [F32, F32, F32], S, D_MODEL, D_MODEL)
    return x4, dict(x=x3, p=p_i, pe=pe, sg=sg, z=z4)


def _ple_bwd(dx4, sv, w, S, gdt=F32):
    def fn(dy_t, z_t, pe_t, sg_t, g_t):
        dz, xhat = _ln_bwd(dy_t, z_t, g_t)
        dgl = dz * pe_t * sg_t * (1.0 - sg_t)
        return [dz, dz * sg_t, dgl], [_colsum(dy_t * xhat), _colsum(dy_t), _colsum(dgl)]

    (dz, dpe, dgl), (dg4, db4, dbg) = _rowwise(
        "ple_bwd", fn, [dx4, sv["z"], sv["pe"], sv["sg"]], [w["ln4_g"]],
        [(D_MODEL, F32), (D_MODEL, BF16), (D_MODEL, BF16)], [D_MODEL] * 3, S)
    dwpT = _mm1("dw_ple_proj", "tn", dpe, sv["p"], D_MODEL, PLE_DIM, S, out_dtype=gdt)
    dwg = _mm1("dw_ple_gate", "tn", sv["x"], dgl, D_MODEL, D_MODEL, S, out_dtype=gdt)

    def dx_epi(accs, dzr):
        return [accs[0] + ALPHA * dzr]

    (dx3,) = _mm("dx_ple", "nt", [dgl], [w["ple_w_gate"]], [(0, 0, 0)], 1, dx_epi, [(dz, "mn", 0)], [F32], S,
                 D_MODEL, D_MODEL)
    return dx3, dict(ple_w_projT=dwpT, ple_w_gate=dwg, ple_b_gate=dbg, ln4_g=dg4, ln4_b=db4)


def _rows_of_all(g):
    return g.reshape((g.shape[0] * g.shape[1],) + g.shape[2:])


EARLY_WEIGHTS = ("ffn1_w_up", "ffn1_w_down", "w_in", "conv_w", "gla_w_g2")


def _ffn_weights(gathered, tag):
    upT = _rows_of_all(gathered[tag + "_w_up"])
    return upT[:D_FF], upT[D_FF:], _rows_of_all(gathered[tag + "_w_down"])


def _late_weights(gathered):
    return dict(ffn2=_ffn_weights(gathered, "ffn2"),
                w_branchT=jnp.moveaxis(gathered["w_branch"], 0, 1).reshape(3, D_MODEL, BRANCH),
                w_out=_rows_of_all(gathered["w_out"]),
                ple_w_projT=_rows_of_all(gathered["ple_w_proj"]),
                ple_w_gate=_rows_of_all(gathered["ple_w_gate"]))


def _layer_weights(gathered, full, i):
    w = _late_weights(gathered) if "w_out" in gathered else {}
    w["ffn1"] = _ffn_weights(gathered, "ffn1")
    w_inT = _rows_of_all(gathered["w_in"])
    placed = sorted((dst, src, width) for src, width, dst in W_IN_SEGMENTS)
    parts, pos = [], 0
    for dst, src, width in placed:
        if dst > pos:
            parts.append(jnp.zeros((dst - pos, D_MODEL), w_inT.dtype))
        parts.append(w_inT[src:src + width])
        pos = dst + width
    parts.append(jnp.zeros((U_WIDTH - pos, D_MODEL), w_inT.dtype))
    w["w_inT_p"] = jnp.concatenate(parts, axis=0)
    eye = jnp.eye(LRU_BLOCKS, dtype=F32)

    def dense(blocks):
        return jnp.einsum("ncd,nm->ncmd", blocks, eye).reshape(BRANCH, BRANCH).astype(BF16)

    def vec(name):
        return full[name][i].reshape(1, -1)

    cw = jnp.moveaxis(gathered["conv_w"], 0, 1).reshape(4, BRANCH)
    w_g2 = jnp.moveaxis(gathered["gla_w_g2"], 0, 1).reshape(GLA_LOWRANK, GLA_QK)
    w["lru"] = dict(cw0=cw[0:1], cw1=cw[1:2], cw2=cw[2:3], cw3=cw[3:4], conv_b=vec("conv_b"),
                    wa=dense(full["lru_wa"][i]), wx=dense(full["lru_wx"][i]), ba=vec("lru_ba"), bx=vec("lru_bx"),
                    lam=vec("lru_lambda"))
    hq = jnp.arange(GLA_QK) // GLA_DK
    hv = jnp.arange(GLA_V) // GLA_DV
    w["gla"] = dict(wg2=jnp.pad(w_g2, ((0, LANES - GLA_LOWRANK), (0, 0))).astype(BF16),
                    bg=vec("gla_b_g"), ng=vec("gla_norm_g"), bd=(hv[:, None] == hq[None, :]).astype(F32))
    w["bfp"] = jnp.pad(vec("fox_b_f"), ((0, 0), (0, LANES - FOX_HEADS)))
    for name in ("ln1_g", "ln1_b", "ln2_g", "ln2_b", "ln3_g", "ln3_b", "ln4_g", "ln4_b", "ple_b_gate"):
        w[name] = vec(name)
    return w


def _layer_fwd(x0, p_i, w, S, side=None, on_side=None):
    x1, s1 = _ffn_fwd("ffn1", x0, *w["ffn1"], w["ln1_g"], w["ln1_b"], S)
    x2, s2, side_out = _mixer_fwd(x1, w, S, side, on_side)
    x3, s3 = _ffn_fwd("ffn2", x2, *w["ffn2"], w["ln3_g"], w["ln3_b"], S)
    x4, s4 = _ple_fwd(x3, p_i, w, S)
    return x4, (s1, s2, s3, s4), side_out


def _layer_bwd(dx4, saved, w, S, make_side=None, gdt_a=F32, gdt_b=F32):
    s1, s2, s3, s4 = saved
    dx3, g4 = _ple_bwd(dx4, s4, w, S, gdt_b)
    dx2, g3 = _ffn_bwd("ffn2", dx3, s3, *w["ffn2"], w["ln3_g"], S, gdt_b)
    late = dict(ffn2_w_upT=g3["w_upT"], ffn2_w_down=g3["w_down"], ple_w_projT=g4["ple_w_projT"],
                ple_w_gate=g4["ple_w_gate"])
    mixer_side = None if make_side is None else (lambda mix: make_side({**late, **mix}))
    dx1, g2, side_out = _mixer_bwd(dx2, s2, w, S, mixer_side, gdt_a, gdt_b)
    dx0, g1 = _ffn_bwd("ffn1", dx1, s1, *w["ffn1"], w["ln1_g"], S, gdt_a)
    grads = dict(g2)
    grads.update(g4)
    grads.update(late)
    grads.update(ffn1_w_upT=g1["w_upT"], ffn1_w_down=g1["w_down"], ln1_g=g1["ln_g"], ln1_b=g1["ln_b"],
                 ln3_g=g3["ln_g"], ln3_b=g3["ln_b"])
    return dx0, grads, side_out


def _travel_grads(grads, names):
    return [_dest_pieces(n, grads[n + "T" if n in COLUMN_SHARDED else n]) for n in names]


def _local_step(x, p, target, gathered0, rest, full, overlap):
    S = x.shape[0]
    names = [n for n, _ in SHARDED]
    early = [n for n in names if n in EARLY_WEIGHTS]
    late = [n for n in names if n not in EARLY_WEIGHTS]
    w0 = _layer_weights(gathered0, full, 0)
    if not overlap:
        h, saved0, _ = _layer_fwd(x, p[0], w0, S)
        w1 = _layer_weights(rest, full, 1)
        h, saved1, _ = _layer_fwd(h, p[1], w1, S)
    else:
        late0, early1, late1 = rest
        h, saved0, got = _layer_fwd(x, p[0], w0, S, _gather_job(list(late0) + list(early1)),
                                    lambda got: w0.update(_late_weights(dict(zip(late, got[:len(late)])))))
        w1 = _layer_weights(dict(zip(early, got[len(late):])), full, 1)
        h, saved1, _ = _layer_fwd(h, p[1], w1, S, _gather_job(list(late1)),
                                  lambda got: w1.update(_late_weights(dict(zip(late, got)))))

    def loss_fn(y, t):
        err = y - t
        return [err * (1.0 / D_MODEL)], [_colsum(err * err) * (0.5 / D_MODEL)]

    (dy,), (lsum,) = _rowwise("loss", loss_fn, [h, target], [], [(D_MODEL, F32)], [D_MODEL], S)
    loss = jnp.sum(lsum)
    if not overlap:
        dy, g1, _ = _layer_bwd(dy, saved1, w1, S)
        dy, g0, _ = _layer_bwd(dy, saved0, w0, S)
        return loss, dy, [g0, g1], {}
    dy, g1, late1_pieces = _layer_bwd(dy, saved1, w1, S, lambda g: _scatter_job(_travel_grads(g, late)), BF16, BF16)
    dy, g0, pieces = _layer_bwd(
        dy, saved0, w0, S, lambda g: _scatter_job(_travel_grads(g1, early) + _travel_grads(g, late)), F32, BF16)
    exchanged = {(1, n): a for n, a in zip(late, late1_pieces)}
    exchanged.update({(1, n): a for n, a in zip(early, pieces[:len(early)])})
    exchanged.update({(0, n): a for n, a in zip(late, pieces[len(early):])})
    return loss, dy, [g0, g1], exchanged


def kernel(x, p, ffn1_w_up, ffn1_w_down, ln1_g, ln1_b, w_in, conv_w, conv_b, lru_wa, lru_ba, lru_wx, lru_bx, lru_lambda, gla_w_g2, gla_b_g, gla_norm_g, fox_b_f, w_branch, w_out, ln2_g, ln2_b, ffn2_w_up, ffn2_w_down, ln3_g, ln3_b, ple_w_proj, ple_w_gate, ple_b_gate, ln4_g, ln4_b, loss_target, m_ffn1_w_up, m_ffn1_w_down, m_ln1_g, m_ln1_b, m_w_in, m_conv_w, m_conv_b, m_lru_wa, m_lru_ba, m_lru_wx, m_lru_bx, m_lru_lambda, m_gla_w_g2, m_gla_b_g, m_gla_norm_g, m_fox_b_f, m_w_branch, m_w_out, m_ln2_g, m_ln2_b, m_ffn2_w_up, m_ffn2_w_down, m_ln3_g, m_ln3_b, m_ple_w_proj, m_ple_w_gate, m_ple_b_gate, m_ln4_g, m_ln4_b, v_ffn1_w_up, v_ffn1_w_down, v_ln1_g, v_ln1_b, v_w_in, v_conv_w, v_conv_b, v_lru_wa, v_lru_ba, v_lru_wx, v_lru_bx, v_lru_lambda, v_gla_w_g2, v_gla_b_g, v_gla_norm_g, v_fox_b_f, v_w_branch, v_w_out, v_ln2_g, v_ln2_b, v_ffn2_w_up, v_ffn2_w_down, v_ln3_g, v_ln3_b, v_ple_w_proj, v_ple_w_gate, v_ple_b_gate, v_ln4_g, v_ln4_b):
    env = dict(locals())
    wts = {n: env[n] for n in WEIGHTS}
    ms = {n: env["m_" + n] for n in WEIGHTS}
    vs = {n: env["v_" + n] for n in WEIGHTS}
    sharded = [n for n, _ in SHARDED]

    def travel(n, a):
        return jnp.swapaxes(a, -1, -2) if n in COLUMN_SHARDED else a

    shards = {(i, n): travel(n, wts[n][i]) if n in SHARDED_F32_GATHER else travel(n, wts[n][i]).astype(BF16)
              for i in range(DEPTH) for n in sharded}
    early = [n for n in sharded if n in EARLY_WEIGHTS]
    late = [n for n in sharded if n not in EARLY_WEIGHTS]
    gathered0 = dict(zip(early, _allgather_multi("gather_weights", [shards[0, n] for n in early])))
    rest = ([shards[0, n] for n in late], [shards[1, n] for n in early], [shards[1, n] for n in late])
    full = {n: wts[n] for n in REPLICATED}

    loss_part, grad_x, layer_grads, pieces = _local_step(x[0], p[:, 0], loss_target[0], gathered0, rest, full, True)
    loss = lax.psum(loss_part, MESH_AXES)

    dest = _travel_grads(layer_grads[0], early)
    got = _sibling_swap_multi("grad_sibling_swap", dest)
    core = lax.axis_index("c").astype(jnp.int32).reshape(1)
    pairs = [_pair_add("grad_pair_add_" + n, core, _as_rows(d, 2), _as_rows(g, 1))
             for n, d, g in zip(early, dest, got)]
    pieces.update({(0, n): a for n, a in zip(early, _chip_all_to_all_multi("grad_chip_all_to_all", pairs))})
    rep = list(REPLICATED)
    rep_grads = [jnp.stack([layer_grads[i][n] for i in range(DEPTH)]).reshape(wts[n].shape) for n in rep]
    (gr,) = _allgather_multi("grad_gather_replicated", [_pack(rep_grads, F32)])

    kinds = ("grad", "delta", "new_m", "new_v")
    out = {}
    for n in sharded:
        local = [_as_rows(travel(n, pieces[i, n].reshape((-1,) + shards[i, n].shape)), 1) for i in range(DEPTH)]
        res = _adamw("adamw_" + n, local, _as_rows(wts[n], 1), _as_rows(ms[n], 1), _as_rows(vs[n], 1))
        for kind, arr in zip(kinds, res):
            out[kind + "_" + n] = arr.reshape(wts[n].shape)
    res = _adamw("adamw_replicated", [gr], _pack([wts[n] for n in rep], F32)[None],
                 _pack([ms[n] for n in rep], F32)[None], _pack([vs[n] for n in rep], F32)[None])
    shapes = [wts[n].shape for n in rep]
    for kind, buf in zip(kinds, res):
        for n, arr in zip(rep, _unpack(buf[0], shapes)):
            out[kind + "_" + n] = arr
    return (loss, grad_x[None], *[out["grad_" + n] for n in WEIGHTS], *[out["delta_" + n] for n in WEIGHTS],
            *[out["new_m_" + n] for n in WEIGHTS], *[out["new_v_" + n] for n in WEIGHTS])
```

```python
import functools
import math

import jax
import jax.numpy as jnp
from jax import lax
from jax.experimental import pallas as pl
from jax.experimental.pallas import tpu as pltpu

F32 = jnp.float32
BF16 = jnp.bfloat16

N_DEV = 8
MESH_AXES = ("x", "y", "c")
DEPTH = 2
D_MODEL = 1024
D_FF = 2816
BRANCH = 512
CHUNK = 64
GLA_HEADS = 4
GLA_DK = 64
GLA_DV = 128
GLA_LOWRANK = 16
GLA_TAU = 16.0
FOX_HEADS = 8
FOX_DH = 64
PLE_DIM = 256
LRU_C = 8.0
LRU_BLOCKS = 8
LN_EPS = 1e-5
RMS_EPS = 1e-6
ALPHA = (2 * DEPTH) ** 0.25
LANES = 128
NEG_BIG = -1e30

ADAM_LR = 0.001
ADAM_B1 = 0.9
ADAM_B2 = 0.999
ADAM_EPS = 1e-08
ADAM_WD = 0.01
ADAM_STEP = 10

VMEM_LIMIT_BYTES = 56 * 1024 * 1024

U_GATES = 0
U_AX = 3072
U_AY = 3584
U_BQ = 4096
U_BK = 4352
U_BV = 4608
U_BR = 5120
U_CQ = 5632
U_CK = 6144
U_CV = 6656
U_BLOW = 7168
U_CF = 7296
U_WIDTH = 7680
W_IN_SEGMENTS = (
    (0, 512, U_AX), (512, 512, U_AY), (1024, 256, U_BQ), (1280, 256, U_BK), (1536, 512, U_BV),
    (2048, 16, U_BLOW), (2064, 512, U_BR), (2576, 512, U_CQ), (3088, 512, U_CK), (3600, 512, U_CV),
    (4112, 8, U_CF), (4120, 3072, U_GATES),
)

SHARDED = (
    ("ffn1_w_up", 2), ("ffn1_w_down", 1), ("w_in", 2), ("conv_w", 2), ("gla_w_g2", 2), ("w_branch", 3),
    ("w_out", 1), ("ffn2_w_up", 2), ("ffn2_w_down", 1), ("ple_w_proj", 2), ("ple_w_gate", 1),
)
SHARDED_F32_GATHER = ("conv_w", "gla_w_g2")
COLUMN_SHARDED = ("ffn1_w_up", "ffn2_w_up", "w_in", "w_branch", "ple_w_proj")
REPLICATED = ("ln1_g", "ln1_b", "conv_b", "lru_wa", "lru_ba", "lru_wx", "lru_bx", "lru_lambda", "gla_b_g",
              "gla_norm_g", "fox_b_f", "ln2_g", "ln2_b", "ln3_g", "ln3_b", "ple_b_gate", "ln4_g", "ln4_b")
WEIGHTS = ("ffn1_w_up", "ffn1_w_down", "ln1_g", "ln1_b", "w_in", "conv_w", "conv_b", "lru_wa", "lru_ba", "lru_wx",
           "lru_bx", "lru_lambda", "gla_w_g2", "gla_b_g", "gla_norm_g", "fox_b_f", "w_branch", "w_out", "ln2_g",
           "ln2_b", "ffn2_w_up", "ffn2_w_down", "ln3_g", "ln3_b", "ple_w_proj", "ple_w_gate", "ple_b_gate", "ln4_g",
           "ln4_b")


def _sigmoid(x):
    return 1.0 / (1.0 + jnp.exp(-x))


def _log1p_pos(e):
    return jnp.where(e < 1e-4, e * (1.0 - 0.5 * e), jnp.log(1.0 + e))


def _softplus(x):
    return jnp.maximum(x, 0.0) + _log1p_pos(jnp.exp(-jnp.abs(x)))


def _log_sigmoid(x):
    return -_softplus(-x)


def _neg_expm1(y):
    series = -y * (1.0 + y * (0.5 + y * (1.0 / 6.0 + y * (1.0 / 24.0 + y * (1.0 / 120.0)))))
    return jnp.where(y > -0.1, series, 1.0 - jnp.exp(y))


def _silu_and_grad(x):
    s = _sigmoid(x)
    return x * s, s * (1.0 + x * (1.0 - s))


_GELU_C = math.sqrt(2.0 / math.pi)


def _gelu_and_grad(x):
    inner = _GELU_C * (x + 0.044715 * x * x * x)
    t = jnp.tanh(inner)
    g = 0.5 * x * (1.0 + t)
    dg = 0.5 * (1.0 + t) + 0.5 * x * (1.0 - t * t) * _GELU_C * (1.0 + 3.0 * 0.044715 * x * x)
    return g, dg


def _ln_stats(z):
    mu = jnp.mean(z, axis=-1, keepdims=True)
    zc = z - mu
    var = jnp.mean(zc * zc, axis=-1, keepdims=True)
    rstd = lax.rsqrt(var + LN_EPS)
    return zc * rstd, rstd


def _ln_fwd(z, g, b):
    xhat, _ = _ln_stats(z)
    return xhat * g + b


def _ln_bwd(dy, z, g):
    xhat, rstd = _ln_stats(z)
    dxh = dy * g
    m1 = jnp.mean(dxh, axis=-1, keepdims=True)
    m2 = jnp.mean(dxh * xhat, axis=-1, keepdims=True)
    return rstd * (dxh - m1 - xhat * m2), xhat


def _colsum(x):
    return jnp.sum(x, axis=0, keepdims=True)


def _dot(a, b, dims):
    dn = {"nn": (((1,), (0,)), ((), ())), "nt": (((1,), (1,)), ((), ())), "tn": (((0,), (0,)), ((), ()))}[dims]
    return lax.dot_general(a.astype(BF16), b.astype(BF16), dn, preferred_element_type=F32)


def _scan_rows(a, b, length, reverse=False, seg=None):
    rows = lax.broadcasted_iota(jnp.int32, b.shape, 0)
    span = seg if seg else length
    pos = rows % span if seg else rows
    d = 1
    while d < span:
        shift = (length - d) if reverse else d
        valid = (pos < span - d) if reverse else (pos >= d)
        sb = jnp.where(valid, pltpu.roll(b, shift, 0), 0.0)
        if a is None:
            b = b + sb
        else:
            b = b + a * sb
            a = a * jnp.where(valid, pltpu.roll(a, shift, 0), 1.0)
        d *= 2
    return a, b


def _tile(dim, pref):
    if dim <= pref:
        return dim
    best = None
    t = LANES
    while t <= pref:
        if dim % t == 0:
            best = t
        t += LANES
    assert best is not None, (dim, pref)
    return best


def _full_spec(arr):
    nd = arr.ndim
    return pl.BlockSpec(arr.shape, lambda *_: (0,) * nd)


def _mm(name, dims, a_ops, b_ops, terms, n_acc, epilogue, extras, out_dtypes, M, N, K, tm=512, tn=1024, tk=1024):
    tm, tn, tk = _tile(M, tm), _tile(N, tn), _tile(K, tk)
    gm, gn, gk = M // tm, N // tn, K // tk
    a_bytes = sum(a.size * a.dtype.itemsize for a in a_ops)
    b_bytes = sum(b.size * b.dtype.itemsize for b in b_ops)
    n_outer = gk == 1 and b_bytes + a_bytes * gn < a_bytes + b_bytes * gm

    def spec(shape, fn):
        if n_outer:
            return pl.BlockSpec(shape, lambda j, i, k: fn(i, j, k))
        return pl.BlockSpec(shape, fn)

    if dims == "tn":
        a_spec = spec((tk, tm), lambda i, j, k: (k, i))
    else:
        a_spec = spec((tm, tk), lambda i, j, k: (i, k))
    if dims == "nt":
        b_spec = spec((tn, tk), lambda i, j, k: (j, k))
    else:
        b_spec = spec((tk, tn), lambda i, j, k: (k, j))
    e_specs, e_arrays = [], []
    for ex in extras:
        if ex[1] == "mn":
            e_specs.append(spec((tm, tn), functools.partial(lambda i, j, k, off: (i, j + off), off=ex[2])))
        else:
            e_specs.append(spec((1, tn), lambda i, j, k: (0, j)))
        e_arrays.append(ex[0])
    na, nb, ne, no = len(a_ops), len(b_ops), len(extras), len(out_dtypes)

    def body(*refs):
        a_refs = refs[:na]
        b_refs = refs[na:na + nb]
        e_refs = refs[na + nb:na + nb + ne]
        o_refs = refs[na + nb + ne:na + nb + ne + no]
        acc_refs = refs[na + nb + ne + no:]
        k = pl.program_id(2)

        @pl.when(k == 0)
        def _():
            for acc in acc_refs:
                acc[...] = jnp.zeros_like(acc)

        for r, ai, bi in terms:
            acc_refs[r][...] += _dot(a_refs[ai][...], b_refs[bi][...], dims)

        @pl.when(k == gk - 1)
        def _():
            res = epilogue([acc[...] for acc in acc_refs], *[e[...] for e in e_refs])
            for o, val in zip(o_refs, res):
                o[...] = val.astype(o.dtype)

    outs = pl.pallas_call(
        body,
        name=name,
        grid=(gn, gm, gk) if n_outer else (gm, gn, gk),
        in_specs=[a_spec] * na + [b_spec] * nb + e_specs,
        out_specs=[spec((tm, tn), lambda i, j, k: (i, j))] * no,
        out_shape=[jax.ShapeDtypeStruct((M, N), dt) for dt in out_dtypes],
        scratch_shapes=[pltpu.VMEM((tm, tn), F32)] * n_acc,
        compiler_params=pltpu.CompilerParams(
            dimension_semantics=("parallel", "parallel", "arbitrary"), vmem_limit_bytes=VMEM_LIMIT_BYTES),
    )(*a_ops, *b_ops, *e_arrays)
    return outs


def _mm1(name, dims, a, b, M, N, K, out_dtype=F32, scale=None, **kw):
    def epi(accs):
        return [accs[0] if scale is None else accs[0] * scale]
    return _mm(name, dims, [a], [b], [(0, 0, 0)], 1, epi, [], [out_dtype], M, N, K, **kw)[0]


def _rowwise(name, fn, row_ins, vec_ins, row_outs, sum_outs, S, tr=512, reverse=False):
    tr = min(tr, S)
    g = S // tr
    rmap = (lambda i: (g - 1 - i)) if reverse else (lambda i: i)
    in_specs, arrays = [], []
    for r in row_ins:
        if isinstance(r, tuple):
            arr, width, blk = r
            in_specs.append(pl.BlockSpec((tr, width), functools.partial(lambda i, blk: (rmap(i), blk), blk=blk)))
        else:
            arr = r
            in_specs.append(pl.BlockSpec((tr, arr.shape[1]), lambda i: (rmap(i), 0)))
        arrays.append(arr)
    for v in vec_ins:
        in_specs.append(_full_spec(v))
        arrays.append(v)
    nr, nv, no, ns = len(row_ins), len(vec_ins), len(row_outs), len(sum_outs)

    def body(*refs):
        ins = [r[...] for r in refs[:nr + nv]]
        o_refs = refs[nr + nv:nr + nv + no]
        s_refs = refs[nr + nv + no:]
        outs, sums = fn(*ins)
        for o, val in zip(o_refs, outs):
            o[...] = val.astype(o.dtype)
        if ns:
            i = pl.program_id(0)

            @pl.when(i == 0)
            def _():
                for s, val in zip(s_refs, sums):
                    s[...] = val

            @pl.when(i > 0)
            def _():
                for s, val in zip(s_refs, sums):
                    s[...] += val

    res = pl.pallas_call(
        body,
        name=name,
        grid=(g,),
        in_specs=in_specs,
        out_specs=[pl.BlockSpec((tr, c), lambda i: (rmap(i), 0)) for c, _ in row_outs]
        + [pl.BlockSpec((1, c), lambda i: (0, 0)) for c in sum_outs],
        out_shape=[jax.ShapeDtypeStruct((S, c), dt) for c, dt in row_outs]
        + [jax.ShapeDtypeStruct((1, c), F32) for c in sum_outs],
        compiler_params=pltpu.CompilerParams(
            dimension_semantics=("arbitrary",), vmem_limit_bytes=VMEM_LIMIT_BYTES),
    )(*arrays)
    return res[:no], res[no:]


MESH_ID = pl.DeviceIdType.MESH


def _remote(src, dst, send_sem, recv_sem, to):
    return pltpu.make_async_remote_copy(src_ref=src, dst_ref=dst, send_sem=send_sem, recv_sem=recv_sem,
                                        device_id=to, device_id_type=MESH_ID)


def _hbm_call(name, body, arrs, out_shapes, n_send, n_recv, n_local):
    return pl.pallas_call(
        body,
        name=name,
        in_specs=[pl.BlockSpec(memory_space=pltpu.HBM)] * len(arrs),
        out_specs=[pl.BlockSpec(memory_space=pltpu.HBM)] * len(out_shapes),
        out_shape=out_shapes,
        scratch_shapes=[pltpu.SemaphoreType.DMA((n_send,)), pltpu.SemaphoreType.DMA((n_recv,)),
                        pltpu.SemaphoreType.DMA((n_local,))],
        compiler_params=pltpu.CompilerParams(has_side_effects=True),
    )(*arrs)


def _side_job(arrs, out_shapes, n_send, n_recv, n_local, phases):
    return dict(arrs=list(arrs), out_shapes=list(out_shapes), sems=(n_send, n_recv, n_local), phases=phases)


def _side_specs(side):
    hbm = pl.BlockSpec(memory_space=pltpu.HBM)
    sems = [pltpu.SemaphoreType.DMA((k,)) for k in side["sems"]]
    return [hbm] * len(side["arrs"]), [hbm] * len(side["out_shapes"]), sems


def _gather_job(arrs):
    n = len(arrs)

    def plan(ins, outs, send_sems, recv_sems, local_sems):
        x, y, c = lax.axis_index("x"), lax.axis_index("y"), lax.axis_index("c")
        me, sibling = (x, y, c), (x, y, 1 - c)
        chips = [(1 - x, y), (x, 1 - y), (1 - x, 1 - y)]

        def slot(i, dev):
            return outs[i].at[4 * dev[0] + 2 * dev[1] + dev[2]]

        def copy(i, k, block, to, src=None):
            dst = slot(i, block)
            return _remote(dst if src is None else src, dst, send_sems.at[7 * i + k], recv_sems.at[7 * i + k], to)

        mine = [pltpu.make_async_copy(ins[i], slot(i, me), local_sems.at[i]) for i in range(n)]
        first = []
        for i in range(n):
            first.append(copy(i, 0, me, sibling, src=ins[i]))
            first += [copy(i, 1 + j, me, (*chip, c), src=ins[i]) for j, chip in enumerate(chips)]
        arrive = [[copy(i, 1 + j, (*chip, c), me) for i in range(n)] for j, chip in enumerate(chips)]
        passed = [[copy(i, 4 + j, (*chip, c), sibling) for i in range(n)] for j, chip in enumerate(chips)]
        last = [copy(i, 0, sibling, me) for i in range(n)]
        last += [copy(i, 4 + j, (*chip, 1 - c), me) for i in range(n) for j, chip in enumerate(chips)]
        return mine, first, arrive, passed, last

    def start(*refs):
        mine, first, _, _, _ = plan(*refs)
        for cp in mine + first:
            cp.start()

    def forward(*refs):
        _, _, arrive, passed, _ = plan(*refs)
        for came, onward in zip(arrive, passed):
            for a, p in zip(came, onward):
                a.wait_recv()
                p.start()

    def finish(*refs):
        mine, first, _, passed, last = plan(*refs)
        for cp in last:
            cp.wait_recv()
        for cp in first + [p for onward in passed for p in onward]:
            cp.wait_send()
        for cp in mine:
            cp.wait()

    outs = [jax.ShapeDtypeStruct((N_DEV,) + a.shape, a.dtype) for a in arrs]
    return _side_job(arrs, outs, 7 * n, 7 * n, n, [start, forward, finish])


def _scatter_job(arrs):
    n = len(arrs)

    def plan(ins, outs, send_sems, recv_sems, local_sems):
        x, y, c = lax.axis_index("x"), lax.axis_index("y"), lax.axis_index("c")
        here = 2 * x + y
        local = [pltpu.make_async_copy(ins[i].at[here, c], outs[i].at[here, c], local_sems.at[i]) for i in range(n)]
        sends, recvs = [], []
        for i in range(n):
            for k in range(1, N_DEV):
                px = 1 - x if k & 4 else x
                py = 1 - y if k & 2 else y
                pc = 1 - c if k & 1 else c
                sems = (send_sems.at[7 * i + k - 1], recv_sems.at[7 * i + k - 1], (px, py, pc))
                sends.append(_remote(ins[i].at[2 * px + py, pc], outs[i].at[here, c], *sems))
                recvs.append(_remote(ins[i].at[2 * px + py, pc], outs[i].at[2 * px + py, pc], *sems))
        return local, sends, recvs

    def start(*refs):
        local, sends, _ = plan(*refs)
        for cp in local + sends:
            cp.start()

    def finish(*refs):
        local, sends, recvs = plan(*refs)
        for cp in recvs:
            cp.wait_recv()
        for cp in sends:
            cp.wait_send()
        for cp in local:
            cp.wait()

    outs = [jax.ShapeDtypeStruct(a.shape, a.dtype) for a in arrs]
    return _side_job(arrs, outs, 7 * n, 7 * n, n, [start, finish])


def _run_job(name, job):
    na, no = len(job["arrs"]), len(job["out_shapes"])

    def body(*refs):
        ins, outs, sems = refs[:na], refs[na:na + no], refs[na + no:]
        for phase in job["phases"]:
            phase(ins, outs, *sems)

    return _hbm_call(name, body, job["arrs"], job["out_shapes"], *job["sems"])


def _allgather_multi(name, arrs):
    return _run_job(name, _gather_job(arrs))


def _sibling_swap_multi(name, arrs):
    n = len(arrs)
    per = 4

    def body(*refs):
        ins, got = refs[:n], refs[n:2 * n]
        send_sems, recv_sems, _ = refs[2 * n:]
        x, y, c = lax.axis_index("x"), lax.axis_index("y"), lax.axis_index("c")
        sibling = (x, y, 1 - c)
        sends = []
        for i in range(n):
            for a in range(4):
                k = per * i + a
                sends.append(_remote(ins[i].at[a, 1 - c], got[i].at[a], send_sems.at[k], recv_sems.at[k], sibling))
        for cp in sends:
            cp.start()
        for cp in sends:
            cp.wait_recv()
        for cp in sends:
            cp.wait_send()

    outs = [jax.ShapeDtypeStruct((4,) + a.shape[2:], a.dtype) for a in arrs]
    return _hbm_call(name, body, arrs, outs, per * n, per * n, 1)


def _chip_all_to_all_multi(name, arrs):
    n = len(arrs)

    def body(*refs):
        ins, outs = refs[:n], refs[n:2 * n]
        send_sems, recv_sems, local_sems = refs[2 * n:]
        x, y, c = lax.axis_index("x"), lax.axis_index("y"), lax.axis_index("c")
        mine = 2 * x + y
        chips = [(1 - x, y), (x, 1 - y), (1 - x, 1 - y)]
        local = [pltpu.make_async_copy(ins[i].at[mine], outs[i].at[mine], local_sems.at[i]) for i in range(n)]
        for cp in local:
            cp.start()
        sends, recvs = [], []
        for i in range(n):
            for j, (px, py) in enumerate(chips):
                peer = 2 * px + py
                sems = (send_sems.at[3 * i + j], recv_sems.at[3 * i + j], (px, py, c))
                sends.append(_remote(ins[i].at[peer], outs[i].at[mine], *sems))
                recvs.append(_remote(ins[i].at[peer], outs[i].at[peer], *sems))
        for cp in sends:
            cp.start()
        for cp in recvs:
            cp.wait_recv()
        for cp in sends:
            cp.wait_send()
        for cp in local:
            cp.wait()

    outs = [jax.ShapeDtypeStruct(a.shape, a.dtype) for a in arrs]
    return _hbm_call(name, body, arrs, outs, 3 * n, 3 * n, n)


def _as_rows(a, lead):
    return a.reshape(a.shape[:lead] + (-1, a.shape[-1]))


def _row_tile(rows, cols, parts):
    budget = 4 * 1024 * 1024 // (4 * max(cols, LANES) * parts)
    return _tile_rows(rows, max(8, min(512, budget // 8 * 8)))


def _pair_add(name, core, both, got):
    _, rows, cols = got.shape
    tr = _row_tile(rows, cols, 2)

    def body(c_ref, a_ref, b_ref, o_ref):
        o_ref[...] = (a_ref[...] + b_ref[...]).astype(o_ref.dtype)

    blk = pl.BlockSpec((1, tr, cols), lambda ch, i, c_ref: (ch, i, 0))
    return pl.pallas_call(
        body, name=name,
        grid_spec=pltpu.PrefetchScalarGridSpec(
            num_scalar_prefetch=1, grid=(4, rows // tr),
            in_specs=[pl.BlockSpec((1, None, tr, cols), lambda ch, i, c_ref: (ch, c_ref[0], i, 0)), blk],
            out_specs=blk),
        out_shape=jax.ShapeDtypeStruct(got.shape, BF16),
        compiler_params=pltpu.CompilerParams(dimension_semantics=("parallel", "parallel"),
                                             vmem_limit_bytes=VMEM_LIMIT_BYTES),
    )(core, both, got)


def _adamw(name, gparts, w, m, v):
    layers = len(gparts)
    _, rows, cols = gparts[0].shape
    tr = _row_tile(rows, cols, sum(gp.shape[0] for gp in gparts))
    c1 = 1.0 / (1.0 - ADAM_B1 ** ADAM_STEP)
    c2 = 1.0 / (1.0 - ADAM_B2 ** ADAM_STEP)

    def body(*refs):
        gp_refs = refs[:layers]
        w_ref, m_ref, v_ref, g_ref, d_ref, nm_ref, nv_ref = refs[layers:]
        layer = pl.program_id(0)
        g = None
        for k, gp_ref in enumerate(gp_refs):
            gk = gp_ref[0].astype(F32)
            for i in range(1, gp_ref.shape[0]):
                gk = gk + gp_ref[i].astype(F32)
            g = gk if g is None else jnp.where(layer == k, gk, g)
        nm = ADAM_B1 * m_ref[...] + (1.0 - ADAM_B1) * g
        nv = ADAM_B2 * v_ref[...] + (1.0 - ADAM_B2) * (g * g)
        m_hat = nm * c1
        v_hat = nv * c2
        g_ref[...] = g
        nm_ref[...] = nm
        nv_ref[...] = nv
        d_ref[...] = -ADAM_LR * (m_hat / (jnp.sqrt(v_hat) + ADAM_EPS) + ADAM_WD * w_ref[...])

    row = pl.BlockSpec((None, tr, cols), lambda l, i: (l, i, 0))
    return pl.pallas_call(
        body,
        name=name,
        grid=(layers, rows // tr),
        in_specs=[pl.BlockSpec((gp.shape[0], tr, cols), lambda l, i: (0, i, 0)) for gp in gparts] + [row, row, row],
        out_specs=[row] * 4,
        out_shape=[jax.ShapeDtypeStruct((layers, rows, cols), F32)] * 4,
        compiler_params=pltpu.CompilerParams(dimension_semantics=("parallel", "parallel"),
                                             vmem_limit_bytes=VMEM_LIMIT_BYTES),
    )(*gparts, w, m, v)


def _tile_rows(rows, pref):
    t = min(pref, rows) // 8 * 8
    while t >= 8 and rows % t:
        t -= 8
    return t if t >= 8 else rows


PACK_ROWS = 512


def _pack(arrs, dtype):
    flat = jnp.concatenate([a.astype(dtype).reshape(-1) for a in arrs])
    quantum = PACK_ROWS * LANES
    padded = -(-flat.shape[0] // quantum) * quantum
    return jnp.pad(flat, (0, padded - flat.shape[0])).reshape(-1, LANES)


def _unpack(buf, shapes, lead=()):
    flat = buf.reshape(lead + (-1,))
    out, off = [], 0
    for shp in shapes:
        n = math.prod(shp)
        out.append(flat[..., off:off + n].reshape(lead + tuple(shp)))
        off += n
    return out


def _dest_pieces(name, g):
    if name == "w_branch":
        return jnp.moveaxis(g.reshape(3, 4, 2, D_MODEL // N_DEV, BRANCH), 0, 2)
    if name in SHARDED_F32_GATHER:
        return jnp.moveaxis(g.reshape(g.shape[0], 4, 2, -1), 0, 2)
    return g.reshape((4, 2, g.shape[0] // N_DEV) + g.shape[1:])


HALO = 8
LRU_TILE = 512


def _rows_down(x, prev, k):
    xs = pltpu.roll(x, k, 0)
    row = lax.broadcasted_iota(jnp.int32, prev.shape, 0)
    top = jnp.where(row < k, pltpu.roll(prev, k, 0), xs[:HALO])
    return jnp.concatenate([top, xs[HALO:]], axis=0)


def _rows_up(x, nxt, k):
    rows = x.shape[0]
    xs = pltpu.roll(x, rows - k, 0)
    row = lax.broadcasted_iota(jnp.int32, nxt.shape, 0)
    bottom = jnp.where(row >= HALO - k, pltpu.roll(nxt, HALO - k, 0), xs[rows - HALO:])
    return jnp.concatenate([xs[:rows - HALO], bottom], axis=0)


def _halo_before(T, block_of, col=0):
    per = T // HALO
    return pl.BlockSpec((HALO, BRANCH), lambda t: (jnp.maximum(block_of(t) * per - 1, 0), col))


def _halo_after(T, block_of, S, col=0):
    per = T // HALO
    return pl.BlockSpec((HALO, BRANCH), lambda t: (jnp.minimum((block_of(t) + 1) * per, S // HALO - 1), col))


def _lru_fwd(u, lw, S):
    T = min(LRU_TILE, S)
    nb = S // T
    row = pl.BlockSpec((T, BRANCH), lambda t: (t, 0))
    vecs = [lw["cw0"], lw["cw1"], lw["cw2"], lw["cw3"], lw["conv_b"], lw["wa"], lw["wx"], lw["ba"], lw["bx"],
            lw["lam"]]

    def body(ax, ax_before, ay, cw0, cw1, cw2, cw3, cb, wa, wx, ba, bx, lam, xc_o, r_o, i_o, a_o, h_o, ya_o, hc):
        t = pl.program_id(0)

        @pl.when(t == 0)
        def _():
            hc[...] = jnp.zeros_like(hc)

        x = ax[...]
        before = jnp.where(t == 0, 0.0, ax_before[...])
        xc = (cw3[...] * x + cw2[...] * _rows_down(x, before, 1) + cw1[...] * _rows_down(x, before, 2)
              + cw0[...] * _rows_down(x, before, 3) + cb[...])
        r = _sigmoid(_dot(xc, wa[...], "nn") + ba[...])
        gi = _sigmoid(_dot(xc, wx[...], "nn") + bx[...])
        sp = _softplus(-lam[...])
        la = -LRU_C * r * sp
        a = jnp.exp(la)
        mult = jnp.sqrt(_neg_expm1(2.0 * la))
        A, B = _scan_rows(a, mult * gi * xc, T)
        h = B + A * hc[...]
        h_o[...] = h
        hc[...] = h_o[pl.ds(T - 1, 1), :]
        xc_o[...] = xc
        r_o[...] = r
        i_o[...] = gi
        a_o[...] = a
        gy, _ = _gelu_and_grad(ay[...])
        ya_o[...] = (gy * h).astype(ya_o.dtype)

    outs = pl.pallas_call(
        body,
        name="lru_fwd",
        grid=(nb,),
        in_specs=[pl.BlockSpec((T, BRANCH), lambda t: (t, U_AX // BRANCH)),
                  _halo_before(T, lambda t: t, U_AX // BRANCH),
                  pl.BlockSpec((T, BRANCH), lambda t: (t, U_AY // BRANCH))] + [_full_spec(v) for v in vecs],
        out_specs=[row] * 6,
        out_shape=[jax.ShapeDtypeStruct((S, BRANCH), F32)] * 5 + [jax.ShapeDtypeStruct((S, BRANCH), BF16)],
        scratch_shapes=[pltpu.VMEM((1, BRANCH), F32)],
        compiler_params=pltpu.CompilerParams(dimension_semantics=("arbitrary",), vmem_limit_bytes=VMEM_LIMIT_BYTES),
    )(u, u, u, *vecs)
    return outs


def _lru_bwd(dya, u, sv, lw, S):
    T = min(LRU_TILE, S)
    nb = S // T
    rrow = pl.BlockSpec((T, BRANCH), lambda t: (nb - 1 - t, 0))
    sq = pl.BlockSpec((BRANCH, BRANCH), lambda t: (0, 0))
    vrow = pl.BlockSpec((1, BRANCH), lambda t: (0, 0))

    def block(t):
        return nb - 1 - t

    def body(dya_r, ay, h, h_before, xc_r, r_r, i_r, a_r, a_after, wa, wx, lam,
             day_o, dxc_o, dwa_o, dwx_o, dba_o, dbx_o, dlam_o, lcar, tmp):
        t = pl.program_id(0)
        h_prev = _rows_down(h[...], jnp.where(t == nb - 1, 0.0, h_before[...]), 1)
        a_next = _rows_up(a_r[...], jnp.where(t == 0, 0.0, a_after[...]), 1)

        @pl.when(t == 0)
        def _():
            lcar[...] = jnp.zeros_like(lcar)
            dwa_o[...] = jnp.zeros_like(dwa_o)
            dwx_o[...] = jnp.zeros_like(dwx_o)
            dba_o[...] = jnp.zeros_like(dba_o)
            dbx_o[...] = jnp.zeros_like(dbx_o)
            dlam_o[...] = jnp.zeros_like(dlam_o)

        gy, dgy = _gelu_and_grad(ay[...])
        dy = dya_r[...]
        day_o[...] = (dy * h[...] * dgy).astype(day_o.dtype)
        A, B = _scan_rows(a_next, dy * gy, T, reverse=True)
        lmb = B + A * lcar[...]
        tmp[...] = lmb
        lcar[...] = tmp[pl.ds(0, 1), :]
        xc, r, gi, a = xc_r[...], r_r[...], i_r[...], a_r[...]
        sp = _softplus(-lam[...])
        la = -LRU_C * r * sp
        mult = jnp.sqrt(_neg_expm1(2.0 * la))
        da = lmb * h_prev
        dmult = lmb * gi * xc
        di = lmb * mult * xc
        dxc = lmb * mult * gi
        dla = da * a - dmult * a * a / mult
        dr = dla * (-LRU_C * sp)
        dlam_o[...] += _colsum(dla * (LRU_C * r)) * _sigmoid(-lam[...])
        dpr = dr * r * (1.0 - r)
        dpi = di * gi * (1.0 - gi)
        dba_o[...] += _colsum(dpr)
        dbx_o[...] += _colsum(dpi)
        dxc_o[...] = dxc + _dot(dpr, wa[...], "nt") + _dot(dpi, wx[...], "nt")
        dwa_o[...] += _dot(xc, dpr, "tn")
        dwx_o[...] += _dot(xc, dpi, "tn")

    outs = pl.pallas_call(
        body,
        name="lru_bwd",
        grid=(nb,),
        in_specs=[rrow, pl.BlockSpec((T, BRANCH), lambda t: (nb - 1 - t, U_AY // BRANCH)), rrow,
                  _halo_before(T, block), rrow, rrow, rrow, rrow, _halo_after(T, block, S), sq, sq, vrow],
        out_specs=[rrow, rrow, sq, sq, vrow, vrow, vrow],
        out_shape=[jax.ShapeDtypeStruct((S, BRANCH), BF16), jax.ShapeDtypeStruct((S, BRANCH), F32),
                   jax.ShapeDtypeStruct((BRANCH, BRANCH), F32), jax.ShapeDtypeStruct((BRANCH, BRANCH), F32),
                   jax.ShapeDtypeStruct((1, BRANCH), F32), jax.ShapeDtypeStruct((1, BRANCH), F32),
                   jax.ShapeDtypeStruct((1, BRANCH), F32)],
        scratch_shapes=[pltpu.VMEM((1, BRANCH), F32), pltpu.VMEM((T, BRANCH), F32)],
        compiler_params=pltpu.CompilerParams(dimension_semantics=("arbitrary",), vmem_limit_bytes=VMEM_LIMIT_BYTES),
    )(dya, u, sv["h"], sv["h"], sv["xc"], sv["r"], sv["i"], sv["a"], sv["a"], lw["wa"], lw["wx"], lw["lam"])
    return outs


def _conv_bwd(dxc, u, lw, S):
    T = min(LRU_TILE, S)
    nb = S // T
    vecs = [lw["cw0"], lw["cw1"], lw["cw2"], lw["cw3"]]
    vrow = pl.BlockSpec((1, BRANCH), lambda t: (0, 0))

    def body(d_r, d_after, ax, ax_before, cw0, cw1, cw2, cw3, dax_o, dcw0_o, dcw1_o, dcw2_o, dcw3_o, dcb_o):
        t = pl.program_id(0)
        d = d_r[...]
        after = jnp.where(t == nb - 1, 0.0, d_after[...])
        x = ax[...]
        before = jnp.where(t == 0, 0.0, ax_before[...])
        dax = (cw3[...] * d + cw2[...] * _rows_up(d, after, 1) + cw1[...] * _rows_up(d, after, 2)
               + cw0[...] * _rows_up(d, after, 3))
        dax_o[...] = dax.astype(dax_o.dtype)
        sums = [_colsum(d * _rows_down(x, before, 3)), _colsum(d * _rows_down(x, before, 2)),
                _colsum(d * _rows_down(x, before, 1)), _colsum(d * x), _colsum(d)]
        outs = [dcw0_o, dcw1_o, dcw2_o, dcw3_o, dcb_o]

        @pl.when(t == 0)
        def _():
            for o, val in zip(outs, sums):
                o[...] = val

        @pl.when(t > 0)
        def _():
            for o, val in zip(outs, sums):
                o[...] += val

    res = pl.pallas_call(
        body,
        name="conv_bwd",
        grid=(nb,),
        in_specs=[pl.BlockSpec((T, BRANCH), lambda t: (t, 0)), _halo_after(T, lambda t: t, S),
                  pl.BlockSpec((T, BRANCH), lambda t: (t, U_AX // BRANCH)),
                  _halo_before(T, lambda t: t, U_AX // BRANCH)] + [_full_spec(v) for v in vecs],
        out_specs=[pl.BlockSpec((T, BRANCH), lambda t: (t, 0))] + [vrow] * 5,
        out_shape=[jax.ShapeDtypeStruct((S, BRANCH), BF16)] + [jax.ShapeDtypeStruct((1, BRANCH), F32)] * 5,
        compiler_params=pltpu.CompilerParams(dimension_semantics=("arbitrary",), vmem_limit_bytes=VMEM_LIMIT_BYTES),
    )(dxc, dxc, u, u, *vecs)
    return res[0], res[1:]


GLA_QK = GLA_HEADS * GLA_DK
GLA_V = GLA_HEADS * GLA_DV
GLA_SCALE = GLA_DK ** -0.5


def _gla_specs(TB, rev_nb=None):
    def rmap(t):
        return t if rev_nb is None else rev_nb - 1 - t
    return [
        pl.BlockSpec((TB, GLA_QK), lambda t: (rmap(t), U_BQ // GLA_QK)),
        pl.BlockSpec((TB, GLA_QK), lambda t: (rmap(t), U_BK // GLA_QK)),
        pl.BlockSpec((TB, GLA_V), lambda t: (rmap(t), U_BV // GLA_V)),
        pl.BlockSpec((TB, GLA_V), lambda t: (rmap(t), U_BR // GLA_V)),
        pl.BlockSpec((TB, LANES), lambda t: (rmap(t), U_BLOW // LANES)),
    ]


def _gla_gates(gl, wg2, bg, TB):
    pre = _dot(gl, wg2, "nn") + bg
    la = _log_sigmoid(pre) * (1.0 / GLA_TAU)
    _, gc = _scan_rows(None, la, TB, seg=CHUNK)
    return pre, la, gc


def _gla_fwd(u, gw, S):
    TB = min(512, S)
    nb = S // TB
    cpb = TB // CHUNK
    vecs = [gw["wg2"], gw["bg"], gw["ng"], gw["bd"]]

    def body(q_r, k_r, v_r, br_r, gl_r, wg2, bg, ng, bd, yb_o, oraw_o, st_o, st):
        t = pl.program_id(0)

        @pl.when(t == 0)
        def _():
            st[...] = jnp.zeros_like(st)

        _, la, gc = _gla_gates(gl_r[...], wg2[...], bg[...], TB)
        for c in range(cpb):
            sl = slice(c * CHUNK, (c + 1) * CHUNK)
            gt = _colsum(la[sl])
            kdec = k_r[sl, :] * jnp.exp(gt - gc[sl])
            d_t = _dot(v_r[sl, :], kdec, "tn") * bd[...]
            s_new = st[...] * jnp.exp(gt) + d_t
            st[...] = s_new
            st_o[c] = s_new
            oraw_o[sl, :] = _dot(q_r[sl, :] * GLA_SCALE, s_new, "nt")
        for h in range(GLA_HEADS):
            hs = slice(h * GLA_DV, (h + 1) * GLA_DV)
            oh = oraw_o[:, hs]
            on = oh * lax.rsqrt(jnp.mean(oh * oh, axis=-1, keepdims=True) + RMS_EPS)
            sil, _ = _silu_and_grad(br_r[:, hs])
            yb_o[:, hs] = (on * ng[:, hs] * sil).astype(yb_o.dtype)

    return pl.pallas_call(
        body,
        name="gla_fwd",
        grid=(nb,),
        in_specs=_gla_specs(TB) + [_full_spec(v) for v in vecs],
        out_specs=[pl.BlockSpec((TB, GLA_V), lambda t: (t, 0)), pl.BlockSpec((TB, GLA_V), lambda t: (t, 0)),
                   pl.BlockSpec((cpb, GLA_V, GLA_QK), lambda t: (t, 0, 0))],
        out_shape=[jax.ShapeDtypeStruct((S, GLA_V), BF16), jax.ShapeDtypeStruct((S, GLA_V), F32),
                   jax.ShapeDtypeStruct((S // CHUNK, GLA_V, GLA_QK), F32)],
        scratch_shapes=[pltpu.VMEM((GLA_V, GLA_QK), F32)],
        compiler_params=pltpu.CompilerParams(dimension_semantics=("arbitrary",), vmem_limit_bytes=VMEM_LIMIT_BYTES),
    )(u, u, u, u, u, *vecs)


def _gla_bwd(dyb, u, oraw, states, gw, S):
    TB = min(512, S)
    nb = S // TB
    cpb = TB // CHUNK
    vecs = [gw["wg2"], gw["bg"], gw["ng"], gw["bd"]]

    def rrow(width):
        return pl.BlockSpec((TB, width), lambda t: (nb - 1 - t, 0))

    def body(dyb_r, oraw_r, q_r, k_r, v_r, br_r, gl_r, st_r, sp_r, wg2, bg, ng, bd,
             dq_o, dk_o, dv_o, dbr_o, dgl_o, dwg2_o, dbg_o, dng_o, dcar, do_buf, dla_buf):
        t = pl.program_id(0)
        blk = nb - 1 - t

        @pl.when(t == 0)
        def _():
            dcar[...] = jnp.zeros_like(dcar)
            dwg2_o[...] = jnp.zeros_like(dwg2_o)
            dbg_o[...] = jnp.zeros_like(dbg_o)
            dng_o[...] = jnp.zeros_like(dng_o)

        pre, la, gc = _gla_gates(gl_r[...], wg2[...], bg[...], TB)
        for h in range(GLA_HEADS):
            hs = slice(h * GLA_DV, (h + 1) * GLA_DV)
            oh = oraw_r[:, hs]
            rs = lax.rsqrt(jnp.mean(oh * oh, axis=-1, keepdims=True) + RMS_EPS)
            on = oh * rs
            sil, dsil = _silu_and_grad(br_r[:, hs])
            dy = dyb_r[:, hs]
            dbr_o[:, hs] = (dy * on * ng[:, hs] * dsil).astype(dbr_o.dtype)
            don = dy * ng[:, hs] * sil
            dng_o[:, hs] += _colsum(dy * on * sil)
            do_buf[:, hs] = rs * (don - on * jnp.mean(don * on, axis=-1, keepdims=True))
        first = jnp.where(blk == 0, 0.0, 1.0)
        for c in reversed(range(cpb)):
            sl = slice(c * CHUNK, (c + 1) * CHUNK)
            s_n = st_r[c]
            s_prev = st_r[c - 1] if c > 0 else sp_r[0] * first
            gt = _colsum(la[sl])
            w = jnp.exp(gt - gc[sl])
            k_c = k_r[sl, :]
            kdec = k_c * w
            qs = q_r[sl, :] * GLA_SCALE
            do_c = do_buf[sl, :]
            dq_o[sl, :] = (_dot(do_c, s_n, "nn") * GLA_SCALE).astype(dq_o.dtype)
            d_n = _dot(do_c, qs, "tn") * bd[...] + dcar[...]
            dv_o[sl, :] = _dot(kdec, d_n, "nt").astype(dv_o.dtype)
            dkdec = _dot(v_r[sl, :], d_n, "nn")
            dk_o[sl, :] = (dkdec * w).astype(dk_o.dtype)
            tt = dkdec * kdec
            e = jnp.exp(gt)
            dgt = _colsum(tt) + _colsum(d_n * s_prev) * e
            _, rc = _scan_rows(None, -tt, CHUNK, reverse=True)
            dla_buf[sl, :] = rc + dgt
            dcar[...] = d_n * e
        dpre = dla_buf[...] * _sigmoid(-pre) * (1.0 / GLA_TAU)
        dbg_o[...] += _colsum(dpre)
        dgl_o[...] = _dot(dpre, wg2[...], "nt").astype(dgl_o.dtype)
        dwg2_o[...] += _dot(gl_r[...], dpre, "tn")

    return pl.pallas_call(
        body,
        name="gla_bwd",
        grid=(nb,),
        in_specs=[rrow(GLA_V), rrow(GLA_V)] + _gla_specs(TB, rev_nb=nb)
        + [pl.BlockSpec((cpb, GLA_V, GLA_QK), lambda t: (nb - 1 - t, 0, 0)),
           pl.BlockSpec((1, GLA_V, GLA_QK), lambda t: (jnp.maximum((nb - 1 - t) * cpb - 1, 0), 0, 0))]
        + [_full_spec(v) for v in vecs],
        out_specs=[rrow(GLA_QK), rrow(GLA_QK), rrow(GLA_V), rrow(GLA_V), rrow(LANES),
                   pl.BlockSpec((LANES, GLA_QK), lambda t: (0, 0)), pl.BlockSpec((1, GLA_QK), lambda t: (0, 0)),
                   pl.BlockSpec((1, GLA_V), lambda t: (0, 0))],
        out_shape=[jax.ShapeDtypeStruct((S, GLA_QK), BF16), jax.ShapeDtypeStruct((S, GLA_QK), BF16),
                   jax.ShapeDtypeStruct((S, GLA_V), BF16), jax.ShapeDtypeStruct((S, GLA_V), BF16),
                   jax.ShapeDtypeStruct((S, LANES), BF16), jax.ShapeDtypeStruct((LANES, GLA_QK), F32),
                   jax.ShapeDtypeStruct((1, GLA_QK), F32), jax.ShapeDtypeStruct((1, GLA_V), F32)],
        scratch_shapes=[pltpu.VMEM((GLA_V, GLA_QK), F32), pltpu.VMEM((TB, GLA_V), F32),
                        pltpu.VMEM((TB, GLA_QK), F32)],
        compiler_params=pltpu.CompilerParams(dimension_semantics=("arbitrary",), vmem_limit_bytes=VMEM_LIMIT_BYTES),
    )(dyb, oraw, u, u, u, u, u, states, states, *vecs)


FOX_SCALE = FOX_DH ** -0.5


def _fox_gate_fwd(u, bfp, S):
    T = min(512, S)

    def body(f_r, b_r, fc_o, car):
        t = pl.program_id(0)

        @pl.when(t == 0)
        def _():
            car[...] = jnp.zeros_like(car)

        _, cs = _scan_rows(None, _log_sigmoid(f_r[...] + b_r[...]), T)
        fc_o[...] = cs + car[...]
        car[...] = fc_o[pl.ds(T - 1, 1), :]

    return pl.pallas_call(
        body,
        name="fox_gate_fwd",
        grid=(S // T,),
        in_specs=[pl.BlockSpec((T, LANES), lambda t: (t, U_CF // LANES)), _full_spec(bfp)],
        out_specs=pl.BlockSpec((T, LANES), lambda t: (t, 0)),
        out_shape=jax.ShapeDtypeStruct((S, LANES), F32),
        scratch_shapes=[pltpu.VMEM((1, LANES), F32)],
        compiler_params=pltpu.CompilerParams(dimension_semantics=("arbitrary",), vmem_limit_bytes=VMEM_LIMIT_BYTES),
    )(u, bfp)


def _fox_gate_bwd(dfc, u, bfp, S):
    T = min(512, S)
    nb = S // T

    def body(d_r, f_r, b_r, df_o, db_o, car, tmp):
        t = pl.program_id(0)

        @pl.when(t == 0)
        def _():
            car[...] = jnp.zeros_like(car)
            db_o[...] = jnp.zeros_like(db_o)

        _, rc = _scan_rows(None, d_r[...], T, reverse=True)
        tmp[...] = rc + car[...]
        car[...] = tmp[pl.ds(0, 1), :]
        df = tmp[...] * _sigmoid(-(f_r[...] + b_r[...]))
        df_o[...] = df.astype(df_o.dtype)
        db_o[...] += _colsum(df)

    return pl.pallas_call(
        body,
        name="fox_gate_bwd",
        grid=(nb,),
        in_specs=[pl.BlockSpec((T, LANES), lambda t: (nb - 1 - t, 0)),
                  pl.BlockSpec((T, LANES), lambda t: (nb - 1 - t, U_CF // LANES)), _full_spec(bfp)],
        out_specs=[pl.BlockSpec((T, LANES), lambda t: (nb - 1 - t, 0)), pl.BlockSpec((1, LANES), lambda t: (0, 0))],
        out_shape=[jax.ShapeDtypeStruct((S, LANES), BF16), jax.ShapeDtypeStruct((1, LANES), F32)],
        scratch_shapes=[pltpu.VMEM((1, LANES), F32), pltpu.VMEM((T, LANES), F32)],
        compiler_params=pltpu.CompilerParams(dimension_semantics=("arbitrary",), vmem_limit_bytes=VMEM_LIMIT_BYTES),
    )(dfc, u, bfp)


def _fox_call(name, body, tables, grid, in_specs, out_specs, out_shape, scratch, args, side):
    n_in, n_out, n_scr = len(in_specs), len(out_specs), len(scratch)
    semantics = ("parallel", "arbitrary")
    if side is not None:
        total = grid[0] * grid[1]
        phases = side["phases"]
        triggers = [0, total - 1] if len(phases) == 2 else [0, total * 7 // 10, total - 1]
        na, no = len(side["arrs"]), len(side["out_shapes"])
        s_in, s_out, s_sems = _side_specs(side)
        kernel_body = body

        def body(*refs):
            tabs, rest = refs[:len(tables)], refs[len(tables):]
            ins, s_ins = rest[:n_in], rest[n_in:n_in + na]
            rest = rest[n_in + na:]
            outs, s_outs = rest[:n_out], rest[n_out:n_out + no]
            rest = rest[n_out + no:]
            scr, sems = rest[:n_scr], rest[n_scr:]
            flat = pl.program_id(0) * grid[1] + pl.program_id(1)
            for trigger, phase in zip(triggers[:-1], phases[:-1]):
                @pl.when(flat == trigger)
                def _(phase=phase):
                    phase(s_ins, s_outs, *sems)
            kernel_body(*tabs, *ins, *outs, *scr)

            @pl.when(flat == triggers[-1])
            def _():
                phases[-1](s_ins, s_outs, *sems)

        in_specs, out_specs = in_specs + s_in, out_specs + s_out
        out_shape, scratch = out_shape + side["out_shapes"], scratch + s_sems
        args = list(args) + side["arrs"]
        semantics = ("arbitrary", "arbitrary")
    res = pl.pallas_call(
        body,
        name=name,
        grid_spec=pltpu.PrefetchScalarGridSpec(num_scalar_prefetch=len(tables), grid=grid, in_specs=in_specs,
                                               out_specs=out_specs, scratch_shapes=scratch),
        out_shape=out_shape,
        compiler_params=pltpu.CompilerParams(dimension_semantics=semantics, vmem_limit_bytes=VMEM_LIMIT_BYTES),
    )(*tables, *args)
    return res[:n_out], res[n_out:]


FOX_TILE = 1024
FOX_GROUP = 2
FOX_AUG = 128
FOX_ONES = 3


def _fox_pairs(n, by_key):
    pairs = [(qi, ki) for qi in range(n) for ki in range(qi + 1)]
    if by_key:
        pairs.sort(key=lambda qk: (qk[1], qk[0]))
    qs = jnp.asarray([qk[0] for qk in pairs], jnp.int32)
    ks = jnp.asarray([qk[1] for qk in pairs], jnp.int32)
    return qs, ks


def _fox_causal(sT):
    keys = lax.broadcasted_iota(jnp.int32, sT.shape, 0)
    queries = lax.broadcasted_iota(jnp.int32, sT.shape, 1)
    return jnp.where(keys <= queries, sT, NEG_BIG)


def _fox_fwd(qT, ka, vT, S, side=None):
    t = min(FOX_TILE, S)
    n = S // t
    qi_tab, ki_tab = _fox_pairs(n, by_key=False)

    G = FOX_GROUP

    def body(qi_ref, ki_ref, qT_r, ka_r, vT_r, oT_o, lse_o, m_s, l_s, acc):
        step = pl.program_id(1)
        qi, ki = qi_ref[step], ki_ref[step]

        @pl.when(ki == 0)
        def _():
            m_s[...] = jnp.full_like(m_s, NEG_BIG)
            l_s[...] = jnp.zeros_like(l_s)
            acc[...] = jnp.zeros_like(acc)

        def update(g, masked):
            sT = _dot(ka_r[g], qT_r[g], "nn")
            if masked:
                sT = _fox_causal(sT)
            m_new = jnp.maximum(m_s[g], jnp.max(sT, axis=0, keepdims=True))
            p = jnp.exp(sT - m_new)
            alpha = jnp.exp(m_s[g] - m_new)
            l_s[g] = alpha * l_s[g] + jnp.sum(p, axis=0, keepdims=True)
            acc[g] = alpha * acc[g] + _dot(vT_r[g], p, "nn")
            m_s[g] = m_new

        @pl.when(ki < qi)
        def _():
            for g in range(G):
                update(g, False)

        @pl.when(ki == qi)
        def _():
            for g in range(G):
                update(g, True)
                oT_o[g] = acc[g] / l_s[g]
                lse_o[g] = m_s[g] + jnp.log(l_s[g])

    return _fox_call(
        "fox_fwd", body, (qi_tab, ki_tab), (FOX_HEADS // G, int(qi_tab.shape[0])),
        [pl.BlockSpec((G, FOX_AUG, t), lambda h, s, qt, kt: (h, 0, qt[s])),
         pl.BlockSpec((G, t, FOX_AUG), lambda h, s, qt, kt: (h, kt[s], 0)),
         pl.BlockSpec((G, FOX_DH, t), lambda h, s, qt, kt: (h, 0, kt[s]))],
        [pl.BlockSpec((G, FOX_DH, t), lambda h, s, qt, kt: (h, 0, qt[s])),
         pl.BlockSpec((G, 1, t), lambda h, s, qt, kt: (h, 0, qt[s]))],
        [jax.ShapeDtypeStruct((FOX_HEADS, FOX_DH, S), F32), jax.ShapeDtypeStruct((FOX_HEADS, 1, S), F32)],
        [pltpu.VMEM((G, 1, t), F32), pltpu.VMEM((G, 1, t), F32), pltpu.VMEM((G, FOX_DH, t), F32)],
        (qT, ka, vT), side)


FOX_BIAS_ROWS = 8


def _fox_bwd(qT, qa, ka, kT, v, do, doT, oT, lse, S, side=None):
    t = min(FOX_TILE, S)
    n = S // t
    qi_tab, ki_tab = _fox_pairs(n, by_key=True)
    n_steps = int(qi_tab.shape[0])
    slab = slice(FOX_DH, FOX_DH + FOX_BIAS_ROWS)

    G = FOX_GROUP

    def body(qi_ref, ki_ref, qT_r, qa_r, ka_r, kT_r, v_r, do_r, doT_r, oT_r, lse_r,
             dq_o, dfq_o, dk_o, dfk_o, dv_o, dq_acc, dk_acc, dv_acc):
        step = pl.program_id(1)
        qi, ki = qi_ref[step], ki_ref[step]

        @pl.when(step == 0)
        def _():
            dq_acc[...] = jnp.zeros_like(dq_acc)

        @pl.when(qi == ki)
        def _():
            dk_acc[...] = jnp.zeros_like(dk_acc)
            dv_acc[...] = jnp.zeros_like(dv_acc)

        def update(g, masked):
            sT = _dot(ka_r[g], qT_r[g], "nn")
            if masked:
                sT = _fox_causal(sT)
            pT = jnp.exp(sT - lse_r[g])
            delta = jnp.sum(oT_r[g] * doT_r[g], axis=0, keepdims=True)
            dsT = pT * (_dot(v_r[g], doT_r[g], "nn") - delta)
            dv_acc[g] += _dot(pT, do_r[g], "nn")
            dk_acc[g] += _dot(dsT, qa_r[g], "nn")
            dq_acc[g, qi] += _dot(kT_r[g], dsT, "nn")

        @pl.when(qi > ki)
        def _():
            for g in range(G):
                update(g, False)

        @pl.when(qi == ki)
        def _():
            for g in range(G):
                update(g, True)

        @pl.when(qi == n - 1)
        def _():
            for g in range(G):
                dk = dk_acc[g]
                dk_o[g] = dk[:, :FOX_DH].astype(dk_o.dtype)
                dfk_o[g] = dk.T[slab]
                dv_o[g] = dv_acc[g].astype(dv_o.dtype)

        @pl.when(step == n_steps - 1)
        def _():
            for g in range(G):
                for j in range(n):
                    dqT = dq_acc[g, j]
                    dq_o[g, j * t:(j + 1) * t, :] = (dqT.T[:, :FOX_DH] * FOX_SCALE).astype(dq_o.dtype)
                    dfq_o[g, :, j * t:(j + 1) * t] = dqT[slab]

    def qlane(rows):
        return pl.BlockSpec((G, rows, t), lambda h, s, qt, kt: (h, 0, qt[s]))

    def qrow(cols):
        return pl.BlockSpec((G, t, cols), lambda h, s, qt, kt: (h, qt[s], 0))

    def krow(cols):
        return pl.BlockSpec((G, t, cols), lambda h, s, qt, kt: (h, kt[s], 0))

    def klane(rows):
        return pl.BlockSpec((G, rows, t), lambda h, s, qt, kt: (h, 0, kt[s]))

    def head(rows, cols):
        return pl.BlockSpec((G, rows, cols), lambda h, s, qt, kt: (h, 0, 0))

    return _fox_call(
        "fox_bwd", body, (qi_tab, ki_tab), (FOX_HEADS // G, n_steps),
        [qlane(FOX_AUG), qrow(FOX_AUG), krow(FOX_AUG), klane(FOX_AUG), krow(FOX_DH), qrow(FOX_DH), qlane(FOX_DH),
         qlane(FOX_DH), qlane(1)],
        [head(S, FOX_DH), head(FOX_BIAS_ROWS, S), krow(FOX_DH), klane(FOX_BIAS_ROWS), krow(FOX_DH)],
        [jax.ShapeDtypeStruct((FOX_HEADS, S, FOX_DH), BF16), jax.ShapeDtypeStruct((FOX_HEADS, FOX_BIAS_ROWS, S), F32),
         jax.ShapeDtypeStruct((FOX_HEADS, S, FOX_DH), BF16), jax.ShapeDtypeStruct((FOX_HEADS, FOX_BIAS_ROWS, S), F32),
         jax.ShapeDtypeStruct((FOX_HEADS, S, FOX_DH), BF16)],
        [pltpu.VMEM((G, n, FOX_AUG, t), F32), pltpu.VMEM((G, t, FOX_AUG), F32), pltpu.VMEM((G, t, FOX_DH), F32)],
        (qT, qa, ka, kT, v, do, doT, oT, lse), side)


def _fox_prep(u, fcum, S):
    T = min(512, S)
    head_of = jnp.arange(BRANCH) // FOX_DH
    dim_of = jnp.arange(BRANCH) % FOX_DH
    heads = jnp.arange(FOX_HEADS)[:, None, None]
    sel = (head_of[None, :, None] == heads) & (dim_of[None, :, None] == jnp.arange(FOX_AUG)[None, None, :])
    sel_q = (sel * FOX_SCALE).astype(BF16)
    sel_k = sel.astype(BF16)
    sel_vT = jnp.swapaxes(sel[:, :, :FOX_DH], 1, 2).astype(BF16)
    piece = jnp.arange(FOX_ONES * LANES) // LANES
    lane = jnp.arange(FOX_ONES * LANES) % LANES
    col = jnp.arange(FOX_AUG)[None, None, :]
    at_q = (lane[None, :, None] == heads) & (col == FOX_DH + FOX_ONES + piece[None, :, None])
    at_k = (lane[None, :, None] == heads) & (col == FOX_DH + piece[None, :, None])
    bias_q = at_q.astype(BF16)
    bias_k = (-at_k.astype(F32)).astype(BF16)
    cols = jnp.arange(FOX_AUG)[None, :]
    ones_q = ((cols >= FOX_DH) & (cols < FOX_DH + FOX_ONES)).astype(F32)
    ones_k = ((cols >= FOX_DH + FOX_ONES) & (cols < FOX_DH + 2 * FOX_ONES)).astype(F32)
    consts = [sel_q, sel_k, sel_vT, bias_q, bias_k, ones_q, ones_k]

    def body(cq, ck, cv, fc, sq, sk, svT, bq, bk, oq, ok, qa_o, ka_o, qT_o, kT_o, vh_o, vT_o):
        f = fc[...]
        hi = f.astype(BF16).astype(F32)
        mid = (f - hi).astype(BF16).astype(F32)
        lo = (f - hi - mid).astype(BF16).astype(F32)
        pieces = jnp.concatenate([hi, mid, lo], axis=1)
        q, k, v = cq[...], ck[...], cv[...]
        for h in range(FOX_HEADS):
            qa = _dot(q, sq[h], "nn") + _dot(pieces, bq[h], "nn") + oq[...]
            ka = _dot(k, sk[h], "nn") + _dot(pieces, bk[h], "nn") + ok[...]
            qa_o[h] = qa.astype(qa_o.dtype)
            ka_o[h] = ka.astype(ka_o.dtype)
            qT_o[h] = qa.T.astype(qT_o.dtype)
            kT_o[h] = ka.T.astype(kT_o.dtype)
            vT_o[h] = _dot(svT[h], v, "nt").astype(vT_o.dtype)
            vh_o[h] = _dot(v, svT[h], "nt").astype(vh_o.dtype)

    def win(off):
        return pl.BlockSpec((T, BRANCH), functools.partial(lambda i, blk: (i, blk), blk=off // BRANCH))

    def rows(c):
        return pl.BlockSpec((FOX_HEADS, T, c), lambda i: (0, i, 0))

    def lanes(r):
        return pl.BlockSpec((FOX_HEADS, r, T), lambda i: (0, 0, i))

    bf = lambda *shape: jax.ShapeDtypeStruct((FOX_HEADS,) + shape, BF16)
    return pl.pallas_call(
        body,
        name="fox_prep",
        grid=(S // T,),
        in_specs=[win(U_CQ), win(U_CK), win(U_CV), pl.BlockSpec((T, LANES), lambda i: (i, 0))]
        + [_full_spec(c) for c in consts],
        out_specs=[rows(FOX_AUG), rows(FOX_AUG), lanes(FOX_AUG), lanes(FOX_AUG), rows(FOX_DH), lanes(FOX_DH)],
        out_shape=[bf(S, FOX_AUG), bf(S, FOX_AUG), bf(FOX_AUG, S), bf(FOX_AUG, S), bf(S, FOX_DH), bf(FOX_DH, S)],
        compiler_params=pltpu.CompilerParams(dimension_semantics=("parallel",), vmem_limit_bytes=VMEM_LIMIT_BYTES),
    )(u, u, u, fcum, *consts)


def _to_heads(x2d, S):
    return jnp.transpose(x2d.reshape(S, FOX_HEADS, FOX_DH), (1, 0, 2))


def _from_heads(xh, S):
    return jnp.transpose(xh, (1, 0, 2)).reshape(S, FOX_HEADS * FOX_DH)


def _ffn_fwd(tag, x, wgT, wuT, wd, g, b, S):
    def up_epi(accs):
        gate, up = accs
        sil, _ = _silu_and_grad(gate)
        return [gate, up, sil * up]

    gate, up, act = _mm(tag + "_up", "nt", [x], [wgT, wuT], [(0, 0, 0), (1, 0, 1)], 2, up_epi, [],
                        [BF16, BF16, BF16], S, D_FF, D_MODEL, tn=1408)

    def down_epi(accs, xr, gg, bb):
        z = ALPHA * xr + 0.5 * accs[0]
        return [z, _ln_fwd(z, gg, bb)]

    z, xn = _mm(tag + "_down", "nn", [act], [wd], [(0, 0, 0)], 1, down_epi, [(x, "mn", 0), (g, "n"), (b, "n")],
                [F32, F32], S, D_MODEL, D_FF, tk=D_FF)
    return xn, dict(x=x, gate=gate, up=up, act=act, z=z)


def _ln_bwd_call(tag, dy, z, g, S):
    def fn(dy_t, z_t, g_t):
        dz, xhat = _ln_bwd(dy_t, z_t, g_t)
        return [dz], [_colsum(dy_t * xhat), _colsum(dy_t)]

    (dz,), (dg, db) = _rowwise(tag + "_ln_bwd", fn, [dy, z], [g], [(D_MODEL, F32)], [D_MODEL, D_MODEL], S)
    return dz, dg, db


def _ffn_bwd(tag, dxn, sv, wgT, wuT, wd, g, S, gdt=F32):
    dz, dg, db = _ln_bwd_call(tag, dxn, sv["z"], g, S)

    def act_epi(accs, gate, up):
        da = 0.5 * accs[0]
        sil, dsil = _silu_and_grad(gate.astype(F32))
        return [da * up.astype(F32) * dsil, da * sil]

    dgate, dup = _mm(tag + "_dact", "nt", [dz], [wd], [(0, 0, 0)], 1, act_epi,
                     [(sv["gate"], "mn", 0), (sv["up"], "mn", 0)], [BF16, BF16], S, D_FF, D_MODEL, tn=1408)
    dwd = _mm1(tag + "_dwd", "tn", sv["act"], dz, D_FF, D_MODEL, S, scale=0.5, tm=1408, out_dtype=gdt)

    def two(accs):
        return [accs[0], accs[1]]

    dwgT, dwuT = _mm(tag + "_dwup", "tn", [dgate, dup], [sv["x"]], [(0, 0, 0), (1, 1, 0)], 2, two, [], [gdt, gdt],
                     D_FF, D_MODEL, S, tm=1408, tk=512)

    def dx_epi(accs, dzr):
        return [accs[0] + ALPHA * dzr]

    (dx,) = _mm(tag + "_dx", "nn", [dgate, dup], [wgT, wuT], [(0, 0, 0), (0, 1, 1)], 1, dx_epi, [(dz, "mn", 0)],
                [F32], S, D_MODEL, D_FF, tm=1024, tk=1408)
    return dx, dict(w_upT=jnp.concatenate([dwgT, dwuT], axis=0), w_down=dwd, ln_g=dg, ln_b=db)


def _mixer_fwd(x1, w, S, side=None, on_side=None):
    u = _mm1("w_in", "nt", x1, w["w_inT_p"], S, U_WIDTH, D_MODEL, tm=1024, tn=1536)
    xc, r, gi, a, h, y_a = _lru_fwd(u, w["lru"], S)
    y_b, oraw, states = _gla_fwd(u, w["gla"], S)
    fcum = _fox_gate_fwd(u, w["bfp"], S)
    qa, ka, qT, kT, vh, vT = _fox_prep(u, fcum, S)
    (oT, lse), side_out = _fox_fwd(qT, ka, vT, S, side)
    if on_side is not None:
        on_side(side_out)
    y_c = jnp.transpose(oT, (2, 0, 1)).reshape(S, BRANCH).astype(BF16)

    def merge_epi(accs, g0, g1, g2):
        merged = _sigmoid(g0) * accs[0] + _sigmoid(g1) * accs[1] + _sigmoid(g2) * accs[2]
        return [accs[0], accs[1], accs[2], merged]

    wb = w["w_branchT"]
    yp0, yp1, yp2, merged = _mm(
        "merge", "nt", [y_a, y_b, y_c], [wb[0], wb[1], wb[2]], [(0, 0, 0), (1, 1, 1), (2, 2, 2)], 3, merge_epi,
        [(u, "mn", 0), (u, "mn", 1), (u, "mn", 2)], [BF16, BF16, BF16, BF16], S, D_MODEL, BRANCH, tm=256)

    def out_epi(accs, xr, gg, bb):
        z = ALPHA * xr + accs[0]
        return [z, _ln_fwd(z, gg, bb)]

    z2, x2 = _mm("w_out", "nn", [merged], [w["w_out"]], [(0, 0, 0)], 1, out_epi,
                 [(x1, "mn", 0), (w["ln2_g"], "n"), (w["ln2_b"], "n")], [F32, F32], S, D_MODEL, D_MODEL)
    sv = dict(x=x1, u=u, xc=xc, r=r, i=gi, a=a, h=h, y_a=y_a, y_b=y_b, y_c=y_c, oraw=oraw,
              states=states, qT=qT, qa=qa, ka=ka, kT=kT, vh=vh, oT=oT, lse=lse, yp=(yp0, yp1, yp2), merged=merged,
              z=z2)
    return x2, sv, side_out


def _mixer_bwd(dx2, sv, w, S, make_side=None, gdt_a=F32, gdt_b=F32):
    u = sv["u"]
    dz, dg2, db2 = _ln_bwd_call("mix", dx2, sv["z"], w["ln2_g"], S)

    def dm_epi(accs, y0, y1, y2, g0, g1, g2):
        dm = accs[0]
        outs_p, outs_g = [], []
        for yp, gl in ((y0, g0), (y1, g1), (y2, g2)):
            sg = _sigmoid(gl)
            outs_p.append(dm * sg)
            outs_g.append(dm * yp.astype(F32) * sg * (1.0 - sg))
        return outs_p + outs_g

    yp = sv["yp"]
    dyp0, dyp1, dyp2, dgl0, dgl1, dgl2 = _mm(
        "dmerged", "nt", [dz], [w["w_out"]], [(0, 0, 0)], 1, dm_epi,
        [(yp[0], "mn", 0), (yp[1], "mn", 0), (yp[2], "mn", 0), (u, "mn", 0), (u, "mn", 1), (u, "mn", 2)],
        [BF16] * 6, S, D_MODEL, D_MODEL, tm=256)
    dw_out = _mm1("dw_out", "tn", sv["merged"], dz, D_MODEL, D_MODEL, S, out_dtype=gdt_b)
    wb = w["w_branchT"]
    dys, dwbs = [], []
    for j, (yj, dyp) in enumerate(((sv["y_a"], dyp0), (sv["y_b"], dyp1), (sv["y_c"], dyp2))):
        dys.append(_mm1("dy_branch%d" % j, "nn", dyp, wb[j], S, BRANCH, D_MODEL))
        dwbs.append(_mm1("dw_branch%d" % j, "tn", dyp, yj, D_MODEL, BRANCH, S, out_dtype=gdt_b))
    day, dxc, dwa, dwx, dba, dbx, dlam = _lru_bwd(dys[0], u, sv, w["lru"], S)
    dax, (dcw0, dcw1, dcw2, dcw3, dcb) = _conv_bwd(dxc, u, w["lru"], S)
    dbq, dbk, dbv, dbr, dglow, dwg2p, dbg, dng = _gla_bwd(dys[1], u, sv["oraw"], sv["states"], w["gla"], S)
    doh = _to_heads(dys[2], S)
    dw_branchT = jnp.stack(dwbs)
    side = make_side(dict(w_out=dw_out, w_branchT=dw_branchT)) if make_side is not None else None
    (dqh, dfq, dkh, dfk, dvh), side_out = _fox_bwd(sv["qT"], sv["qa"], sv["ka"], sv["kT"], sv["vh"], doh,
                                                   jnp.swapaxes(doh, 1, 2), sv["oT"], sv["lse"], S, side)
    dfc = jnp.transpose(dfq[:, FOX_ONES, :] - dfk[:, 0, :])
    dfc = jnp.pad(dfc, ((0, 0), (0, LANES - FOX_HEADS)))
    dcf, dbf = _fox_gate_bwd(dfc, u, w["bfp"], S)
    du = jnp.concatenate(
        [dgl0, dgl1, dgl2, dax, day, dbq, dbk, dbv, dbr, _from_heads(dqh, S).astype(BF16),
         _from_heads(dkh, S).astype(BF16), _from_heads(dvh, S).astype(BF16), dglow, dcf,
         jnp.zeros((S, U_WIDTH - U_CF - LANES), BF16)], axis=1)
    dw_inT_p = _mm1("dw_in", "tn", du, sv["x"], U_WIDTH, D_MODEL, S, tm=1536, out_dtype=gdt_a)

    def dx_epi(accs, dzr):
        return [accs[0] + ALPHA * dzr]

    (dx1,) = _mm("dx_mix", "nn", [du], [w["w_inT_p"]], [(0, 0, 0)], 1, dx_epi, [(dz, "mn", 0)], [F32], S, D_MODEL,
                 U_WIDTH, tm=1024, tk=1536)
    pieces = sorted(W_IN_SEGMENTS)
    dw_inT = jnp.concatenate([dw_inT_p[dst:dst + width] for _, width, dst in pieces], axis=0)
    eye = jnp.eye(LRU_BLOCKS, dtype=F32)
    dwa_b = jnp.einsum("ncmd,nm->ncd", dwa.reshape(LRU_BLOCKS, 64, LRU_BLOCKS, 64), eye)
    dwx_b = jnp.einsum("ncmd,nm->ncd", dwx.reshape(LRU_BLOCKS, 64, LRU_BLOCKS, 64), eye)
    grads = dict(
        w_inT=dw_inT, w_out=dw_out, w_branchT=dw_branchT, ln2_g=dg2, ln2_b=db2,
        conv_w=jnp.concatenate([dcw0, dcw1, dcw2, dcw3], axis=0).astype(gdt_a), conv_b=dcb, lru_wa=dwa_b, lru_wx=dwx_b,
        lru_ba=dba, lru_bx=dbx, lru_lambda=dlam, gla_w_g2=dwg2p[:GLA_LOWRANK].astype(gdt_a), gla_b_g=dbg, gla_norm_g=dng,
        fox_b_f=dbf[:, :FOX_HEADS])
    return dx1, grads, side_out


def _ple_fwd(x3, p_i, w, S):
    pe = _mm1("ple_proj", "nt", p_i, w["ple_w_projT"], S, D_MODEL, PLE_DIM)

    def epi(accs, xr, per, bg, gg, bb):
        sg = _sigmoid(accs[0] + bg)
        z = ALPHA * xr + sg * per
        return [sg, z, _ln_fwd(z, gg, bb)]

    sg, z4, x4 = _mm("ple_gate", "nn", [x3], [w["ple_w_gate"]], [(0, 0, 0)], 1, epi,
                     [(x3, "mn", 0), (pe, "mn", 0), (w["ple_b_gate"], "n"), (w["ln4_g"], "n"), (w["ln4_b"], "n")],
                     [F32, F32, F32], S, D_MODEL, D_MODEL)
    return x4, dict(x=x3, p=p_i, pe=pe, sg=sg, z=z4)


def _ple_bwd(dx4, sv, w, S, gdt=F32):
    def fn(dy_t, z_t, pe_t, sg_t, g_t):
        dz, xhat = _ln_bwd(dy_t, z_t, g_t)
        dgl = dz * pe_t * sg_t * (1.0 - sg_t)
        return [dz, dz * sg_t, dgl], [_colsum(dy_t * xhat), _colsum(dy_t), _colsum(dgl)]

    (dz, dpe, dgl), (dg4, db4, dbg) = _rowwise(
        "ple_bwd", fn, [dx4, sv["z"], sv["pe"], sv["sg"]], [w["ln4_g"]],
        [(D_MODEL, F32), (D_MODEL, BF16), (D_MODEL, BF16)], [D_MODEL] * 3, S)
    dwpT = _mm1("dw_ple_proj", "tn", dpe, sv["p"], D_MODEL, PLE_DIM, S, out_dtype=gdt)
    dwg = _mm1("dw_ple_gate", "tn", sv["x"], dgl, D_MODEL, D_MODEL, S, out_dtype=gdt)

    def dx_epi(accs, dzr):
        return [accs[0] + ALPHA * dzr]

    (dx3,) = _mm("dx_ple", "nt", [dgl], [w["ple_w_gate"]], [(0, 0, 0)], 1, dx_epi, [(dz, "mn", 0)], [F32], S,
                 D_MODEL, D_MODEL)
    return dx3, dict(ple_w_projT=dwpT, ple_w_gate=dwg, ple_b_gate=dbg, ln4_g=dg4, ln4_b=db4)


def _rows_of_all(g):
    return g.reshape((g.shape[0] * g.shape[1],) + g.shape[2:])


EARLY_WEIGHTS = ("ffn1_w_up", "ffn1_w_down", "w_in", "conv_w", "gla_w_g2")


def _ffn_weights(gathered, tag):
    upT = _rows_of_all(gathered[tag + "_w_up"])
    return upT[:D_FF], upT[D_FF:], _rows_of_all(gathered[tag + "_w_down"])


def _late_weights(gathered):
    return dict(ffn2=_ffn_weights(gathered, "ffn2"),
                w_branchT=jnp.moveaxis(gathered["w_branch"], 0, 1).reshape(3, D_MODEL, BRANCH),
                w_out=_rows_of_all(gathered["w_out"]),
                ple_w_projT=_rows_of_all(gathered["ple_w_proj"]),
                ple_w_gate=_rows_of_all(gathered["ple_w_gate"]))


def _layer_weights(gathered, full, i):
    w = _late_weights(gathered) if "w_out" in gathered else {}
    w["ffn1"] = _ffn_weights(gathered, "ffn1")
    w_inT = _rows_of_all(gathered["w_in"])
    placed = sorted((dst, src, width) for src, width, dst in W_IN_SEGMENTS)
    parts, pos = [], 0
    for dst, src, width in placed:
        if dst > pos:
            parts.append(jnp.zeros((dst - pos, D_MODEL), w_inT.dtype))
        parts.append(w_inT[src:src + width])
        pos = dst + width
    parts.append(jnp.zeros((U_WIDTH - pos, D_MODEL), w_inT.dtype))
    w["w_inT_p"] = jnp.concatenate(parts, axis=0)
    eye = jnp.eye(LRU_BLOCKS, dtype=F32)

    def dense(blocks):
        return jnp.einsum("ncd,nm->ncmd", blocks, eye).reshape(BRANCH, BRANCH).astype(BF16)

    def vec(name):
        return full[name][i].reshape(1, -1)

    cw = jnp.moveaxis(gathered["conv_w"], 0, 1).reshape(4, BRANCH)
    w_g2 = jnp.moveaxis(gathered["gla_w_g2"], 0, 1).reshape(GLA_LOWRANK, GLA_QK)
    w["lru"] = dict(cw0=cw[0:1], cw1=cw[1:2], cw2=cw[2:3], cw3=cw[3:4], conv_b=vec("conv_b"),
                    wa=dense(full["lru_wa"][i]), wx=dense(full["lru_wx"][i]), ba=vec("lru_ba"), bx=vec("lru_bx"),
                    lam=vec("lru_lambda"))
    hq = jnp.arange(GLA_QK) // GLA_DK
    hv = jnp.arange(GLA_V) // GLA_DV
    w["gla"] = dict(wg2=jnp.pad(w_g2, ((0, LANES - GLA_LOWRANK), (0, 0))).astype(BF16),
                    bg=vec("gla_b_g"), ng=vec("gla_norm_g"), bd=(hv[:, None] == hq[None, :]).astype(F32))
    w["bfp"] = jnp.pad(vec("fox_b_f"), ((0, 0), (0, LANES - FOX_HEADS)))
    for name in ("ln1_g", "ln1_b", "ln2_g", "ln2_b", "ln3_g", "ln3_b", "ln4_g", "ln4_b", "ple_b_gate"):
        w[name] = vec(name)
    return w


def _layer_fwd(x0, p_i, w, S, side=None, on_side=None):
    x1, s1 = _ffn_fwd("ffn1", x0, *w["ffn1"], w["ln1_g"], w["ln1_b"], S)
    x2, s2, side_out = _mixer_fwd(x1, w, S, side, on_side)
    x3, s3 = _ffn_fwd("ffn2", x2, *w["ffn2"], w["ln3_g"], w["ln3_b"], S)
    x4, s4 = _ple_fwd(x3, p_i, w, S)
    return x4, (s1, s2, s3, s4), side_out


def _layer_bwd(dx4, saved, w, S, make_side=None, gdt_a=F32, gdt_b=F32):
    s1, s2, s3, s4 = saved
    dx3, g4 = _ple_bwd(dx4, s4, w, S, gdt_b)
    dx2, g3 = _ffn_bwd("ffn2", dx3, s3, *w["ffn2"], w["ln3_g"], S, gdt_b)
    late = dict(ffn2_w_upT=g3["w_upT"], ffn2_w_down=g3["w_down"], ple_w_projT=g4["ple_w_projT"],
                ple_w_gate=g4["ple_w_gate"])
    mixer_side = None if make_side is None else (lambda mix: make_side({**late, **mix}))
    dx1, g2, side_out = _mixer_bwd(dx2, s2, w, S, mixer_side, gdt_a, gdt_b)
    dx0, g1 = _ffn_bwd("ffn1", dx1, s1, *w["ffn1"], w["ln1_g"], S, gdt_a)
    grads = dict(g2)
    grads.update(g4)
    grads.update(late)
    grads.update(ffn1_w_upT=g1["w_upT"], ffn1_w_down=g1["w_down"], ln1_g=g1["ln_g"], ln1_b=g1["ln_b"],
                 ln3_g=g3["ln_g"], ln3_b=g3["ln_b"])
    return dx0, grads, side_out


def _travel_grads(grads, names):
    return [_dest_pieces(n, grads[n + "T" if n in COLUMN_SHARDED else n]) for n in names]


def _local_step(x, p, target, gathered0, rest, full, overlap):
    S = x.shape[0]
    names = [n for n, _ in SHARDED]
    early = [n for n in names if n in EARLY_WEIGHTS]
    late = [n for n in names if n not in EARLY_WEIGHTS]
    w0 = _layer_weights(gathered0, full, 0)
    if not overlap:
        h, saved0, _ = _layer_fwd(x, p[0], w0, S)
        w1 = _layer_weights(rest, full, 1)
        h, saved1, _ = _layer_fwd(h, p[1], w1, S)
    else:
        late0, early1, late1 = rest
        h, saved0, got = _layer_fwd(x, p[0], w0, S, _gather_job(list(late0) + list(early1)),
                                    lambda got: w0.update(_late_weights(dict(zip(late, got[:len(late)])))))
        w1 = _layer_weights(dict(zip(early, got[len(late):])), full, 1)
        h, saved1, _ = _layer_fwd(h, p[1], w1, S, _gather_job(list(late1)),
                                  lambda got: w1.update(_late_weights(dict(zip(late, got)))))

    def loss_fn(y, t):
        err = y - t
        return [err * (1.0 / D_MODEL)], [_colsum(err * err) * (0.5 / D_MODEL)]

    (dy,), (lsum,) = _rowwise("loss", loss_fn, [h, target], [], [(D_MODEL, F32)], [D_MODEL], S)
    loss = jnp.sum(lsum)
    if not overlap:
        dy, g1, _ = _layer_bwd(dy, saved1, w1, S)
        dy, g0, _ = _layer_bwd(dy, saved0, w0, S)
        return loss, dy, [g0, g1], {}
    dy, g1, late1_pieces = _layer_bwd(dy, saved1, w1, S, lambda g: _scatter_job(_travel_grads(g, late)), BF16, BF16)
    dy, g0, pieces = _layer_bwd(
        dy, saved0, w0, S, lambda g: _scatter_job(_travel_grads(g1, early) + _travel_grads(g, late)), F32, BF16)
    exchanged = {(1, n): a for n, a in zip(late, late1_pieces)}
    exchanged.update({(1, n): a for n, a in zip(early, pieces[:len(early)])})
    exchanged.update({(0, n): a for n, a in zip(late, pieces[len(early):])})
    return loss, dy, [g0, g1], exchanged


def kernel(x, p, ffn1_w_up, ffn1_w_down, ln1_g, ln1_b, w_in, conv_w, conv_b, lru_wa, lru_ba, lru_wx, lru_bx, lru_lambda, gla_w_g2, gla_b_g, gla_norm_g, fox_b_f, w_branch, w_out, ln2_g, ln2_b, ffn2_w_up, ffn2_w_down, ln3_g, ln3_b, ple_w_proj, ple_w_gate, ple_b_gate, ln4_g, ln4_b, loss_target, m_ffn1_w_up, m_ffn1_w_down, m_ln1_g, m_ln1_b, m_w_in, m_conv_w, m_conv_b, m_lru_wa, m_lru_ba, m_lru_wx, m_lru_bx, m_lru_lambda, m_gla_w_g2, m_gla_b_g, m_gla_norm_g, m_fox_b_f, m_w_branch, m_w_out, m_ln2_g, m_ln2_b, m_ffn2_w_up, m_ffn2_w_down, m_ln3_g, m_ln3_b, m_ple_w_proj, m_ple_w_gate, m_ple_b_gate, m_ln4_g, m_ln4_b, v_ffn1_w_up, v_ffn1_w_down, v_ln1_g, v_ln1_b, v_w_in, v_conv_w, v_conv_b, v_lru_wa, v_lru_ba, v_lru_wx, v_lru_bx, v_lru_lambda, v_gla_w_g2, v_gla_b_g, v_gla_norm_g, v_fox_b_f, v_w_branch, v_w_out, v_ln2_g, v_ln2_b, v_ffn2_w_up, v_ffn2_w_down, v_ln3_g, v_ln3_b, v_ple_w_proj, v_ple_w_gate, v_ple_b_gate, v_ln4_g, v_ln4_b):
    env = dict(locals())
    wts = {n: env[n] for n in WEIGHTS}
    ms = {n: env["m_" + n] for n in WEIGHTS}
    vs = {n: env["v_" + n] for n in WEIGHTS}
    sharded = [n for n, _ in SHARDED]

    def travel(n, a):
        return jnp.swapaxes(a, -1, -2) if n in COLUMN_SHARDED else a

    shards = {(i, n): travel(n, wts[n][i]) if n in SHARDED_F32_GATHER else travel(n, wts[n][i]).astype(BF16)
              for i in range(DEPTH) for n in sharded}
    early = [n for n in sharded if n in EARLY_WEIGHTS]
    late = [n for n in sharded if n not in EARLY_WEIGHTS]
    gathered0 = dict(zip(early, _allgather_multi("gather_weights", [shards[0, n] for n in early])))
    rest = ([shards[0, n] for n in late], [shards[1, n] for n in early], [shards[1, n] for n in late])
    full = {n: wts[n] for n in REPLICATED}

    loss_part, grad_x, layer_grads, pieces = _local_step(x[0], p[:, 0], loss_target[0], gathered0, rest, full, True)
    loss = lax.psum(loss_part, MESH_AXES)

    dest = _travel_grads(layer_grads[0], early)
    got = _sibling_swap_multi("grad_sibling_swap", dest)
    core = lax.axis_index("c").astype(jnp.int32).reshape(1)
    pairs = [_pair_add("grad_pair_add_" + n, core, _as_rows(d, 2), _as_rows(g, 1))
             for n, d, g in zip(early, dest, got)]
    pieces.update({(0, n): a for n, a in zip(early, _chip_all_to_all_multi("grad_chip_all_to_all", pairs))})
    rep = list(REPLICATED)
    rep_grads = [jnp.stack([layer_grads[i][n] for i in range(DEPTH)]).reshape(wts[n].shape) for n in rep]
    (gr,) = _allgather_multi("grad_gather_replicated", [_pack(rep_grads, F32)])

    kinds = ("grad", "delta", "new_m", "new_v")
    out = {}
    for n in sharded:
        local = [_as_rows(travel(n, pieces[i, n].reshape((-1,) + shards[i, n].shape)), 1) for i in range(DEPTH)]
        res = _adamw("adamw_" + n, local, _as_rows(wts[n], 1), _as_rows(ms[n], 1), _as_rows(vs[n], 1))
        for kind, arr in zip(kinds, res):
            out[kind + "_" + n] = arr.reshape(wts[n].shape)
    res = _adamw("adamw_replicated", [gr], _pack([wts[n] for n in rep], F32)[None],
                 _pack([ms[n] for n in rep], F32)[None], _pack([vs[n] for n in rep], F32)[None])
    shapes = [wts[n].shape for n in rep]
    for kind, buf in zip(kinds, res):
        for n, arr in zip(rep, _unpack(buf[0], shapes)):
            out[kind + "_" + n] = arr
    return (loss, grad_x[None], *[out["grad_" + n] for n in WEIGHTS], *[out["delta_" + n] for n in WEIGHTS],
            *[out["new_m_" + n] for n in WEIGHTS], *[out["new_v_" + n] for n in WEIGHTS])
```

```python
import functools
import math

import jax
import jax.numpy as jnp
from jax import lax
from jax.experimental import pallas as pl
from jax.experimental.pallas import tpu as pltpu

F32 = jnp.float32
BF16 = jnp.bfloat16

N_DEV = 8
MESH_AXES = ("x", "y", "c")
DEPTH = 2
D_MODEL = 1024
D_FF = 2816
BRANCH = 512
CHUNK = 64
GLA_HEADS = 4
GLA_DK = 64
GLA_DV = 128
GLA_LOWRANK = 16
GLA_TAU = 16.0
FOX_HEADS = 8
FOX_DH = 64
PLE_DIM = 256
LRU_C = 8.0
LRU_BLOCKS = 8
LN_EPS = 1e-5
RMS_EPS = 1e-6
ALPHA = (2 * DEPTH) ** 0.25
LANES = 128
NEG_BIG = -1e30

ADAM_LR = 0.001
ADAM_B1 = 0.9
ADAM_B2 = 0.999
ADAM_EPS = 1e-08
ADAM_WD = 0.01
ADAM_STEP = 10

VMEM_LIMIT_BYTES = 56 * 1024 * 1024

U_GATES = 0
U_AX = 3072
U_AY = 3584
U_BQ = 4096
U_BK = 4352
U_BV = 4608
U_BR = 5120
U_CQ = 5632
U_CK = 6144
U_CV = 6656
U_BLOW = 7168
U_CF = 7296
U_WIDTH = 7680
W_IN_SEGMENTS = (
    (0, 512, U_AX), (512, 512, U_AY), (1024, 256, U_BQ), (1280, 256, U_BK), (1536, 512, U_BV),
    (2048, 16, U_BLOW), (2064, 512, U_BR), (2576, 512, U_CQ), (3088, 512, U_CK), (3600, 512, U_CV),
    (4112, 8, U_CF), (4120, 3072, U_GATES),
)

SHARDED = (
    ("ffn1_w_up", 2), ("ffn1_w_down", 1), ("w_in", 2), ("conv_w", 2), ("gla_w_g2", 2), ("w_branch", 3),
    ("w_out", 1), ("ffn2_w_up", 2), ("ffn2_w_down", 1), ("ple_w_proj", 2), ("ple_w_gate", 1),
)
SHARDED_F32_GATHER = ("conv_w", "gla_w_g2")
COLUMN_SHARDED = ("ffn1_w_up", "ffn2_w_up", "w_in", "w_branch", "ple_w_proj")
REPLICATED = ("ln1_g", "ln1_b", "conv_b", "lru_wa", "lru_ba", "lru_wx", "lru_bx", "lru_lambda", "gla_b_g",
              "gla_norm_g", "fox_b_f", "ln2_g", "ln2_b", "ln3_g", "ln3_b", "ple_b_gate", "ln4_g", "ln4_b")
WEIGHTS = ("ffn1_w_up", "ffn1_w_down", "ln1_g", "ln1_b", "w_in", "conv_w", "conv_b", "lru_wa", "lru_ba", "lru_wx",
           "lru_bx", "lru_lambda", "gla_w_g2", "gla_b_g", "gla_norm_g", "fox_b_f", "w_branch", "w_out", "ln2_g",
           "ln2_b", "ffn2_w_up", "ffn2_w_down", "ln3_g", "ln3_b", "ple_w_proj", "ple_w_gate", "ple_b_gate", "ln4_g",
           "ln4_b")


def _sigmoid(x):
    return 1.0 / (1.0 + jnp.exp(-x))


def _log1p_pos(e):
    return jnp.where(e < 1e-4, e * (1.0 - 0.5 * e), jnp.log(1.0 + e))


def _softplus(x):
    return jnp.maximum(x, 0.0) + _log1p_pos(jnp.exp(-jnp.abs(x)))


def _log_sigmoid(x):
    return -_softplus(-x)


def _neg_expm1(y):
    series = -y * (1.0 + y * (0.5 + y * (1.0 / 6.0 + y * (1.0 / 24.0 + y * (1.0 / 120.0)))))
    return jnp.where(y > -0.1, series, 1.0 - jnp.exp(y))


def _silu_and_grad(x):
    s = _sigmoid(x)
    return x * s, s * (1.0 + x * (1.0 - s))


_GELU_C = math.sqrt(2.0 / math.pi)


def _gelu_and_grad(x):
    inner = _GELU_C * (x + 0.044715 * x * x * x)
    t = jnp.tanh(inner)
    g = 0.5 * x * (1.0 + t)
    dg = 0.5 * (1.0 + t) + 0.5 * x * (1.0 - t * t) * _GELU_C * (1.0 + 3.0 * 0.044715 * x * x)
    return g, dg


def _ln_stats(z):
    mu = jnp.mean(z, axis=-1, keepdims=True)
    zc = z - mu
    var = jnp.mean(zc * zc, axis=-1, keepdims=True)
    rstd = lax.rsqrt(var + LN_EPS)
    return zc * rstd, rstd


def _ln_fwd(z, g, b):
    xhat, _ = _ln_stats(z)
    return xhat * g + b


def _ln_bwd(dy, z, g):
    xhat, rstd = _ln_stats(z)
    dxh = dy * g
    m1 = jnp.mean(dxh, axis=-1, keepdims=True)
    m2 = jnp.mean(dxh * xhat, axis=-1, keepdims=True)
    return rstd * (dxh - m1 - xhat * m2), xhat


def _colsum(x):
    return jnp.sum(x, axis=0, keepdims=True)


def _dot(a, b, dims):
    dn = {"nn": (((1,), (0,)), ((), ())), "nt": (((1,), (1,)), ((), ())), "tn": (((0,), (0,)), ((), ()))}[dims]
    return lax.dot_general(a.astype(BF16), b.astype(BF16), dn, preferred_element_type=F32)


def _scan_rows(a, b, length, reverse=False, seg=None):
    rows = lax.broadcasted_iota(jnp.int32, b.shape, 0)
    span = seg if seg else length
    pos = rows % span if seg else rows
    d = 1
    while d < span:
        shift = (length - d) if reverse else d
        valid = (pos < span - d) if reverse else (pos >= d)
        sb = jnp.where(valid, pltpu.roll(b, shift, 0), 0.0)
        if a is None:
            b = b + sb
        else:
            b = b + a * sb
            a = a * jnp.where(valid, pltpu.roll(a, shift, 0), 1.0)
        d *= 2
    return a, b


def _tile(dim, pref):
    if dim <= pref:
        return dim
    best = None
    t = LANES
    while t <= pref:
        if dim % t == 0:
            best = t
        t += LANES
    assert best is not None, (dim, pref)
    return best


def _full_spec(arr):
    nd = arr.ndim
    return pl.BlockSpec(arr.shape, lambda *_: (0,) * nd)


def _mm(name, dims, a_ops, b_ops, terms, n_acc, epilogue, extras, out_dtypes, M, N, K, tm=512, tn=1024, tk=1024,
        side=None):
    tm, tn, tk = _tile(M, tm), _tile(N, tn), _tile(K, tk)
    gm, gn, gk = M // tm, N // tn, K // tk
    a_bytes = sum(a.size * a.dtype.itemsize for a in a_ops)
    b_bytes = sum(b.size * b.dtype.itemsize for b in b_ops)
    n_outer = gk == 1 and b_bytes + a_bytes * gn < a_bytes + b_bytes * gm

    def spec(shape, fn):
        if n_outer:
            return pl.BlockSpec(shape, lambda j, i, k: fn(i, j, k))
        return pl.BlockSpec(shape, fn)

    if dims == "tn":
        a_spec = spec((tk, tm), lambda i, j, k: (k, i))
    else:
        a_spec = spec((tm, tk), lambda i, j, k: (i, k))
    if dims == "nt":
        b_spec = spec((tn, tk), lambda i, j, k: (j, k))
    else:
        b_spec = spec((tk, tn), lambda i, j, k: (k, j))
    e_specs, e_arrays = [], []
    for ex in extras:
        if ex[1] == "mn":
            e_specs.append(spec((tm, tn), functools.partial(lambda i, j, k, off: (i, j + off), off=ex[2])))
        else:
            e_specs.append(spec((1, tn), lambda i, j, k: (0, j)))
        e_arrays.append(ex[0])
    na, nb, ne, no = len(a_ops), len(b_ops), len(extras), len(out_dtypes)
    grid = (gn, gm, gk) if n_outer else (gm, gn, gk)
    s_in, s_out, s_sems = _side_specs(side) if side is not None else ([], [], [])
    nsi, nso = len(s_in), len(s_out)
    total = grid[0] * grid[1] * grid[2]

    def body(*refs):
        a_refs = refs[:na]
        b_refs = refs[na:na + nb]
        e_refs = refs[na + nb:na + nb + ne]
        pos = na + nb + ne
        s_ins, o_refs = refs[pos:pos + nsi], refs[pos + nsi:pos + nsi + no]
        pos += nsi + no
        s_outs, acc_refs, sems = refs[pos:pos + nso], refs[pos + nso:pos + nso + n_acc], refs[pos + nso + n_acc:]
        k = pl.program_id(2)
        if side is not None:
            phases = side["phases"]
            triggers = [0, total - 1] if len(phases) == 2 else [0, total * 7 // 10, total - 1]
            flat = (pl.program_id(0) * grid[1] + pl.program_id(1)) * grid[2] + k
            for trigger, phase in zip(triggers[:-1], phases[:-1]):
                @pl.when(flat == trigger)
                def _(phase=phase):
                    phase(s_ins, s_outs, *sems)

        @pl.when(k == 0)
        def _():
            for acc in acc_refs:
                acc[...] = jnp.zeros_like(acc)

        for r, ai, bi in terms:
            acc_refs[r][...] += _dot(a_refs[ai][...], b_refs[bi][...], dims)

        @pl.when(k == gk - 1)
        def _():
            res = epilogue([acc[...] for acc in acc_refs], *[e[...] for e in e_refs])
            for o, val in zip(o_refs, res):
                o[...] = val.astype(o.dtype)

        if side is not None:
            @pl.when(flat == triggers[-1])
            def _():
                phases[-1](s_ins, s_outs, *sems)

    outs = pl.pallas_call(
        body,
        name=name,
        grid=grid,
        in_specs=[a_spec] * na + [b_spec] * nb + e_specs + s_in,
        out_specs=[spec((tm, tn), lambda i, j, k: (i, j))] * no + s_out,
        out_shape=[jax.ShapeDtypeStruct((M, N), dt) for dt in out_dtypes] + (side["out_shapes"] if side else []),
        scratch_shapes=[pltpu.VMEM((tm, tn), F32)] * n_acc + s_sems,
        compiler_params=pltpu.CompilerParams(
            dimension_semantics=("arbitrary",) * 3 if side is not None else ("parallel", "parallel", "arbitrary"),
            vmem_limit_bytes=VMEM_LIMIT_BYTES),
    )(*a_ops, *b_ops, *e_arrays, *(side["arrs"] if side else []))
    return outs if side is None else (outs[:no], outs[no:])


def _mm1(name, dims, a, b, M, N, K, out_dtype=F32, scale=None, **kw):
    def epi(accs):
        return [accs[0] if scale is None else accs[0] * scale]
    return _mm(name, dims, [a], [b], [(0, 0, 0)], 1, epi, [], [out_dtype], M, N, K, **kw)[0]


def _rowwise(name, fn, row_ins, vec_ins, row_outs, sum_outs, S, tr=512, reverse=False):
    tr = min(tr, S)
    g = S // tr
    rmap = (lambda i: (g - 1 - i)) if reverse else (lambda i: i)
    in_specs, arrays = [], []
    for r in row_ins:
        if isinstance(r, tuple):
            arr, width, blk = r
            in_specs.append(pl.BlockSpec((tr, width), functools.partial(lambda i, blk: (rmap(i), blk), blk=blk)))
        else:
            arr = r
            in_specs.append(pl.BlockSpec((tr, arr.shape[1]), lambda i: (rmap(i), 0)))
        arrays.append(arr)
    for v in vec_ins:
        in_specs.append(_full_spec(v))
        arrays.append(v)
    nr, nv, no, ns = len(row_ins), len(vec_ins), len(row_outs), len(sum_outs)

    def body(*refs):
        ins = [r[...] for r in refs[:nr + nv]]
        o_refs = refs[nr + nv:nr + nv + no]
        s_refs = refs[nr + nv + no:]
        outs, sums = fn(*ins)
        for o, val in zip(o_refs, outs):
            o[...] = val.astype(o.dtype)
        if ns:
            i = pl.program_id(0)

            @pl.when(i == 0)
            def _():
                for s, val in zip(s_refs, sums):
                    s[...] = val

            @pl.when(i > 0)
            def _():
                for s, val in zip(s_refs, sums):
                    s[...] += val

    res = pl.pallas_call(
        body,
        name=name,
        grid=(g,),
        in_specs=in_specs,
        out_specs=[pl.BlockSpec((tr, c), lambda i: (rmap(i), 0)) for c, _ in row_outs]
        + [pl.BlockSpec((1, c), lambda i: (0, 0)) for c in sum_outs],
        out_shape=[jax.ShapeDtypeStruct((S, c), dt) for c, dt in row_outs]
        + [jax.ShapeDtypeStruct((1, c), F32) for c in sum_outs],
        compiler_params=pltpu.CompilerParams(
            dimension_semantics=("arbitrary",), vmem_limit_bytes=VMEM_LIMIT_BYTES),
    )(*arrays)
    return res[:no], res[no:]


MESH_ID = pl.DeviceIdType.MESH


def _remote(src, dst, send_sem, recv_sem, to):
    return pltpu.make_async_remote_copy(src_ref=src, dst_ref=dst, send_sem=send_sem, recv_sem=recv_sem,
                                        device_id=to, device_id_type=MESH_ID)


def _hbm_call(name, body, arrs, out_shapes, n_send, n_recv, n_local):
    return pl.pallas_call(
        body,
        name=name,
        in_specs=[pl.BlockSpec(memory_space=pltpu.HBM)] * len(arrs),
        out_specs=[pl.BlockSpec(memory_space=pltpu.HBM)] * len(out_shapes),
        out_shape=out_shapes,
        scratch_shapes=[pltpu.SemaphoreType.DMA((n_send,)), pltpu.SemaphoreType.DMA((n_recv,)),
                        pltpu.SemaphoreType.DMA((n_local,))],
        compiler_params=pltpu.CompilerParams(has_side_effects=True),
    )(*arrs)


def _side_job(arrs, out_shapes, n_send, n_recv, n_local, phases):
    return dict(arrs=list(arrs), out_shapes=list(out_shapes), sems=(n_send, n_recv, n_local), phases=phases)


def _side_specs(side):
    hbm = pl.BlockSpec(memory_space=pltpu.HBM)
    sems = [pltpu.SemaphoreType.DMA((k,)) for k in side["sems"]]
    return [hbm] * len(side["arrs"]), [hbm] * len(side["out_shapes"]), sems


def _gather_job(arrs):
    n = len(arrs)

    def plan(ins, outs, send_sems, recv_sems, local_sems):
        x, y, c = lax.axis_index("x"), lax.axis_index("y"), lax.axis_index("c")
        me, sibling = (x, y, c), (x, y, 1 - c)
        chips = [(1 - x, y), (x, 1 - y), (1 - x, 1 - y)]

        def slot(i, dev):
            return outs[i].at[4 * dev[0] + 2 * dev[1] + dev[2]]

        def copy(i, k, block, to, src=None):
            dst = slot(i, block)
            return _remote(dst if src is None else src, dst, send_sems.at[7 * i + k], recv_sems.at[7 * i + k], to)

        mine = [pltpu.make_async_copy(ins[i], slot(i, me), local_sems.at[i]) for i in range(n)]
        first = []
        for i in range(n):
            first.append(copy(i, 0, me, sibling, src=ins[i]))
            first += [copy(i, 1 + j, me, (*chip, c), src=ins[i]) for j, chip in enumerate(chips)]
        arrive = [[copy(i, 1 + j, (*chip, c), me) for i in range(n)] for j, chip in enumerate(chips)]
        passed = [[copy(i, 4 + j, (*chip, c), sibling) for i in range(n)] for j, chip in enumerate(chips)]
        last = [copy(i, 0, sibling, me) for i in range(n)]
        last += [copy(i, 4 + j, (*chip, 1 - c), me) for i in range(n) for j, chip in enumerate(chips)]
        return mine, first, arrive, passed, last

    def start(*refs):
        mine, first, _, _, _ = plan(*refs)
        for cp in mine + first:
            cp.start()

    def forward(*refs):
        _, _, arrive, passed, _ = plan(*refs)
        for came, onward in zip(arrive, passed):
            for a, p in zip(came, onward):
                a.wait_recv()
                p.start()

    def finish(*refs):
        mine, first, _, passed, last = plan(*refs)
        for cp in last:
            cp.wait_recv()
        for cp in first + [p for onward in passed for p in onward]:
            cp.wait_send()
        for cp in mine:
            cp.wait()

    outs = [jax.ShapeDtypeStruct((N_DEV,) + a.shape, a.dtype) for a in arrs]
    return _side_job(arrs, outs, 7 * n, 7 * n, n, [start, forward, finish])


def _scatter_job(arrs):
    n = len(arrs)

    def plan(ins, outs, send_sems, recv_sems, local_sems):
        x, y, c = lax.axis_index("x"), lax.axis_index("y"), lax.axis_index("c")
        here = 2 * x + y
        local = [pltpu.make_async_copy(ins[i].at[here, c], outs[i].at[here, c], local_sems.at[i]) for i in range(n)]
        sends, recvs = [], []
        for i in range(n):
            for k in range(1, N_DEV):
                px = 1 - x if k & 4 else x
                py = 1 - y if k & 2 else y
                pc = 1 - c if k & 1 else c
                sems = (send_sems.at[7 * i + k - 1], recv_sems.at[7 * i + k - 1], (px, py, pc))
                sends.append(_remote(ins[i].at[2 * px + py, pc], outs[i].at[here, c], *sems))
                recvs.append(_remote(ins[i].at[2 * px + py, pc], outs[i].at[2 * px + py, pc], *sems))
        return local, sends, recvs

    def start(*refs):
        local, sends, _ = plan(*refs)
        for cp in local + sends:
            cp.start()

    def finish(*refs):
        local, sends, recvs = plan(*refs)
        for cp in recvs:
            cp.wait_recv()
        for cp in sends:
            cp.wait_send()
        for cp in local:
            cp.wait()

    outs = [jax.ShapeDtypeStruct(a.shape, a.dtype) for a in arrs]
    return _side_job(arrs, outs, 7 * n, 7 * n, n, [start, finish])


def _run_job(name, job):
    na, no = len(job["arrs"]), len(job["out_shapes"])

    def body(*refs):
        ins, outs, sems = refs[:na], refs[na:na + no], refs[na + no:]
        for phase in job["phases"]:
            phase(ins, outs, *sems)

    return _hbm_call(name, body, job["arrs"], job["out_shapes"], *job["sems"])


def _allgather_multi(name, arrs):
    return _run_job(name, _gather_job(arrs))


def _sibling_swap_multi(name, arrs):
    n = len(arrs)
    per = 4

    def body(*refs):
        ins, got = refs[:n], refs[n:2 * n]
        send_sems, recv_sems, _ = refs[2 * n:]
        x, y, c = lax.axis_index("x"), lax.axis_index("y"), lax.axis_index("c")
        sibling = (x, y, 1 - c)
        sends = []
        for i in range(n):
            for a in range(4):
                k = per * i + a
                sends.append(_remote(ins[i].at[a, 1 - c], got[i].at[a], send_sems.at[k], recv_sems.at[k], sibling))
        for cp in sends:
            cp.start()
        for cp in sends:
            cp.wait_recv()
        for cp in sends:
            cp.wait_send()

    outs = [jax.ShapeDtypeStruct((4,) + a.shape[2:], a.dtype) for a in arrs]
    return _hbm_call(name, body, arrs, outs, per * n, per * n, 1)


def _chip_job(arrs):
    n = len(arrs)

    def plan(ins, outs, send_sems, recv_sems, local_sems):
        x, y, c = lax.axis_index("x"), lax.axis_index("y"), lax.axis_index("c")
        mine = 2 * x + y
        chips = [(1 - x, y), (x, 1 - y), (1 - x, 1 - y)]
        local = [pltpu.make_async_copy(ins[i].at[mine], outs[i].at[mine], local_sems.at[i]) for i in range(n)]
        sends, recvs = [], []
        for i in range(n):
            for j, (px, py) in enumerate(chips):
                peer = 2 * px + py
                sems = (send_sems.at[3 * i + j], recv_sems.at[3 * i + j], (px, py, c))
                sends.append(_remote(ins[i].at[peer], outs[i].at[mine], *sems))
                recvs.append(_remote(ins[i].at[peer], outs[i].at[peer], *sems))
        return local, sends, recvs

    def start(*refs):
        local, sends, _ = plan(*refs)
        for cp in local + sends:
            cp.start()

    def finish(*refs):
        local, sends, recvs = plan(*refs)
        for cp in recvs:
            cp.wait_recv()
        for cp in sends:
            cp.wait_send()
        for cp in local:
            cp.wait()

    outs = [jax.ShapeDtypeStruct(a.shape, a.dtype) for a in arrs]
    return _side_job(arrs, outs, 3 * n, 3 * n, n, [start, finish])


def _as_rows(a, lead):
    return a.reshape(a.shape[:lead] + (-1, a.shape[-1]))


def _row_tile(rows, cols, parts):
    budget = 4 * 1024 * 1024 // (4 * max(cols, LANES) * parts)
    return _tile_rows(rows, max(8, min(512, budget // 8 * 8)))


def _pair_add(name, core, both, got):
    _, rows, cols = got.shape
    tr = _row_tile(rows, cols, 2)

    def body(c_ref, a_ref, b_ref, o_ref):
        o_ref[...] = (a_ref[...] + b_ref[...]).astype(o_ref.dtype)

    blk = pl.BlockSpec((1, tr, cols), lambda ch, i, c_ref: (ch, i, 0))
    return pl.pallas_call(
        body, name=name,
        grid_spec=pltpu.PrefetchScalarGridSpec(
            num_scalar_prefetch=1, grid=(4, rows // tr),
            in_specs=[pl.BlockSpec((1, None, tr, cols), lambda ch, i, c_ref: (ch, c_ref[0], i, 0)), blk],
            out_specs=blk),
        out_shape=jax.ShapeDtypeStruct(got.shape, BF16),
        compiler_params=pltpu.CompilerParams(dimension_semantics=("parallel", "parallel"),
                                             vmem_limit_bytes=VMEM_LIMIT_BYTES),
    )(core, both, got)


def _adamw(name, gparts, w, m, v):
    layers = len(gparts)
    _, rows, cols = gparts[0].shape
    tr = _row_tile(rows, cols, sum(gp.shape[0] for gp in gparts))
    c1 = 1.0 / (1.0 - ADAM_B1 ** ADAM_STEP)
    c2 = 1.0 / (1.0 - ADAM_B2 ** ADAM_STEP)

    def body(*refs):
        gp_refs = refs[:layers]
        w_ref, m_ref, v_ref, g_ref, d_ref, nm_ref, nv_ref = refs[layers:]
        layer = pl.program_id(0)
        g = None
        for k, gp_ref in enumerate(gp_refs):
            gk = gp_ref[0].astype(F32)
            for i in range(1, gp_ref.shape[0]):
                gk = gk + gp_ref[i].astype(F32)
            g = gk if g is None else jnp.where(layer == k, gk, g)
        nm = ADAM_B1 * m_ref[...] + (1.0 - ADAM_B1) * g
        nv = ADAM_B2 * v_ref[...] + (1.0 - ADAM_B2) * (g * g)
        m_hat = nm * c1
        v_hat = nv * c2
        g_ref[...] = g
        nm_ref[...] = nm
        nv_ref[...] = nv
        d_ref[...] = -ADAM_LR * (m_hat / (jnp.sqrt(v_hat) + ADAM_EPS) + ADAM_WD * w_ref[...])

    row = pl.BlockSpec((None, tr, cols), lambda l, i: (l, i, 0))
    return pl.pallas_call(
        body,
        name=name,
        grid=(layers, rows // tr),
        in_specs=[pl.BlockSpec((gp.shape[0], tr, cols), lambda l, i: (0, i, 0)) for gp in gparts] + [row, row, row],
        out_specs=[row] * 4,
        out_shape=[jax.ShapeDtypeStruct((layers, rows, cols), F32)] * 4,
        compiler_params=pltpu.CompilerParams(dimension_semantics=("parallel", "parallel"),
                                             vmem_limit_bytes=VMEM_LIMIT_BYTES),
    )(*gparts, w, m, v)


def _tile_rows(rows, pref):
    t = min(pref, rows) // 8 * 8
    while t >= 8 and rows % t:
        t -= 8
    return t if t >= 8 else rows


PACK_ROWS = 512


def _pack(arrs, dtype):
    flat = jnp.concatenate([a.astype(dtype).reshape(-1) for a in arrs])
    quantum = PACK_ROWS * LANES
    padded = -(-flat.shape[0] // quantum) * quantum
    return jnp.pad(flat, (0, padded - flat.shape[0])).reshape(-1, LANES)


def _unpack(buf, shapes, lead=()):
    flat = buf.reshape(lead + (-1,))
    out, off = [], 0
    for shp in shapes:
        n = math.prod(shp)
        out.append(flat[..., off:off + n].reshape(lead + tuple(shp)))
        off += n
    return out


def _dest_pieces(name, g):
    if name == "w_branch":
        return jnp.moveaxis(g.reshape(3, 4, 2, D_MODEL // N_DEV, BRANCH), 0, 2)
    if name in SHARDED_F32_GATHER:
        return jnp.moveaxis(g.reshape(g.shape[0], 4, 2, -1), 0, 2)
    return g.reshape((4, 2, g.shape[0] // N_DEV) + g.shape[1:])


HALO = 8
LRU_TILE = 256
CONV_TILE = 512


def _rows_down(x, prev, k):
    xs = pltpu.roll(x, k, 0)
    row = lax.broadcasted_iota(jnp.int32, prev.shape, 0)
    top = jnp.where(row < k, pltpu.roll(prev, k, 0), xs[:HALO])
    return jnp.concatenate([top, xs[HALO:]], axis=0)


def _rows_up(x, nxt, k):
    rows = x.shape[0]
    xs = pltpu.roll(x, rows - k, 0)
    row = lax.broadcasted_iota(jnp.int32, nxt.shape, 0)
    bottom = jnp.where(row >= HALO - k, pltpu.roll(nxt, HALO - k, 0), xs[rows - HALO:])
    return jnp.concatenate([xs[:rows - HALO], bottom], axis=0)


def _halo_before(T, block_of, col=0):
    per = T // HALO
    return pl.BlockSpec((HALO, BRANCH), lambda t: (jnp.maximum(block_of(t) * per - 1, 0), col))


def _halo_after(T, block_of, S, col=0):
    per = T // HALO
    return pl.BlockSpec((HALO, BRANCH), lambda t: (jnp.minimum((block_of(t) + 1) * per, S // HALO - 1), col))


def _lru_fwd(u, lw, S):
    T = min(LRU_TILE, S)
    nb = S // T
    row = pl.BlockSpec((T, BRANCH), lambda t: (t, 0))
    vecs = [lw["cw0"], lw["cw1"], lw["cw2"], lw["cw3"], lw["conv_b"], lw["wa"], lw["wx"], lw["ba"], lw["bx"],
            lw["lam"]]

    def body(ax, ax_before, ay, cw0, cw1, cw2, cw3, cb, wa, wx, ba, bx, lam, xc_o, r_o, i_o, a_o, h_o, ya_o, hc):
        t = pl.program_id(0)

        @pl.when(t == 0)
        def _():
            hc[...] = jnp.zeros_like(hc)

        x = ax[...]
        before = jnp.where(t == 0, 0.0, ax_before[...])
        xc = (cw3[...] * x + cw2[...] * _rows_down(x, before, 1) + cw1[...] * _rows_down(x, before, 2)
              + cw0[...] * _rows_down(x, before, 3) + cb[...])
        r = _sigmoid(_dot(xc, wa[...], "nn") + ba[...])
        gi = _sigmoid(_dot(xc, wx[...], "nn") + bx[...])
        sp = _softplus(-lam[...])
        la = -LRU_C * r * sp
        a = jnp.exp(la)
        mult = jnp.sqrt(_neg_expm1(2.0 * la))
        A, B = _scan_rows(a, mult * gi * xc, T)
        h = B + A * hc[...]
        h_o[...] = h
        hc[...] = h_o[pl.ds(T - 1, 1), :]
        xc_o[...] = xc
        r_o[...] = r
        i_o[...] = gi
        a_o[...] = a
        gy, _ = _gelu_and_grad(ay[...])
        ya_o[...] = (gy * h).astype(ya_o.dtype)

    outs = pl.pallas_call(
        body,
        name="lru_fwd",
        grid=(nb,),
        in_specs=[pl.BlockSpec((T, BRANCH), lambda t: (t, U_AX // BRANCH)),
                  _halo_before(T, lambda t: t, U_AX // BRANCH),
                  pl.BlockSpec((T, BRANCH), lambda t: (t, U_AY // BRANCH))] + [_full_spec(v) for v in vecs],
        out_specs=[row] * 6,
        out_shape=[jax.ShapeDtypeStruct((S, BRANCH), F32)] * 5 + [jax.ShapeDtypeStruct((S, BRANCH), BF16)],
        scratch_shapes=[pltpu.VMEM((1, BRANCH), F32)],
        compiler_params=pltpu.CompilerParams(dimension_semantics=("arbitrary",), vmem_limit_bytes=VMEM_LIMIT_BYTES),
    )(u, u, u, *vecs)
    return outs


def _lru_bwd(dya, u, sv, lw, S):
    T = min(LRU_TILE, S)
    nb = S // T
    rrow = pl.BlockSpec((T, BRANCH), lambda t: (nb - 1 - t, 0))
    sq = pl.BlockSpec((BRANCH, BRANCH), lambda t: (0, 0))
    vrow = pl.BlockSpec((1, BRANCH), lambda t: (0, 0))

    def block(t):
        return nb - 1 - t

    def body(dya_r, ay, h, h_before, xc_r, r_r, i_r, a_r, a_after, wa, wx, lam,
             day_o, dxc_o, dwa_o, dwx_o, dba_o, dbx_o, dlam_o, lcar, tmp):
        t = pl.program_id(0)
        h_prev = _rows_down(h[...], jnp.where(t == nb - 1, 0.0, h_before[...]), 1)
        a_next = _rows_up(a_r[...], jnp.where(t == 0, 0.0, a_after[...]), 1)

        @pl.when(t == 0)
        def _():
            lcar[...] = jnp.zeros_like(lcar)
            dwa_o[...] = jnp.zeros_like(dwa_o)
            dwx_o[...] = jnp.zeros_like(dwx_o)
            dba_o[...] = jnp.zeros_like(dba_o)
            dbx_o[...] = jnp.zeros_like(dbx_o)
            dlam_o[...] = jnp.zeros_like(dlam_o)

        gy, dgy = _gelu_and_grad(ay[...])
        dy = dya_r[...]
        day_o[...] = (dy * h[...] * dgy).astype(day_o.dtype)
        A, B = _scan_rows(a_next, dy * gy, T, reverse=True)
        lmb = B + A * lcar[...]
        tmp[...] = lmb
        lcar[...] = tmp[pl.ds(0, 1), :]
        xc, r, gi, a = xc_r[...], r_r[...], i_r[...], a_r[...]
        sp = _softplus(-lam[...])
        la = -LRU_C * r * sp
        mult = jnp.sqrt(_neg_expm1(2.0 * la))
        da = lmb * h_prev
        dmult = lmb * gi * xc
        di = lmb * mult * xc
        dxc = lmb * mult * gi
        dla = da * a - dmult * a * a / mult
        dr = dla * (-LRU_C * sp)
        dlam_o[...] += _colsum(dla * (LRU_C * r)) * _sigmoid(-lam[...])
        dpr = dr * r * (1.0 - r)
        dpi = di * gi * (1.0 - gi)
        dba_o[...] += _colsum(dpr)
        dbx_o[...] += _colsum(dpi)
        dxc_o[...] = dxc + _dot(dpr, wa[...], "nt") + _dot(dpi, wx[...], "nt")
        dwa_o[...] += _dot(xc, dpr, "tn")
        dwx_o[...] += _dot(xc, dpi, "tn")

    outs = pl.pallas_call(
        body,
        name="lru_bwd",
        grid=(nb,),
        in_specs=[rrow, pl.BlockSpec((T, BRANCH), lambda t: (nb - 1 - t, U_AY // BRANCH)), rrow,
                  _halo_before(T, block), rrow, rrow, rrow, rrow, _halo_after(T, block, S), sq, sq, vrow],
        out_specs=[rrow, rrow, sq, sq, vrow, vrow, vrow],
        out_shape=[jax.ShapeDtypeStruct((S, BRANCH), BF16), jax.ShapeDtypeStruct((S, BRANCH), F32),
                   jax.ShapeDtypeStruct((BRANCH, BRANCH), F32), jax.ShapeDtypeStruct((BRANCH, BRANCH), F32),
                   jax.ShapeDtypeStruct((1, BRANCH), F32), jax.ShapeDtypeStruct((1, BRANCH), F32),
                   jax.ShapeDtypeStruct((1, BRANCH), F32)],
        scratch_shapes=[pltpu.VMEM((1, BRANCH), F32), pltpu.VMEM((T, BRANCH), F32)],
        compiler_params=pltpu.CompilerParams(dimension_semantics=("arbitrary",), vmem_limit_bytes=VMEM_LIMIT_BYTES),
    )(dya, u, sv["h"], sv["h"], sv["xc"], sv["r"], sv["i"], sv["a"], sv["a"], lw["wa"], lw["wx"], lw["lam"])
    return outs


def _conv_bwd(dxc, u, lw, S):
    T = min(CONV_TILE, S)
    nb = S // T
    vecs = [lw["cw0"], lw["cw1"], lw["cw2"], lw["cw3"]]
    vrow = pl.BlockSpec((1, BRANCH), lambda t: (0, 0))

    def body(d_r, d_after, ax, ax_before, cw0, cw1, cw2, cw3, dax_o, dcw0_o, dcw1_o, dcw2_o, dcw3_o, dcb_o):
        t = pl.program_id(0)
        d = d_r[...]
        after = jnp.where(t == nb - 1, 0.0, d_after[...])
        x = ax[...]
        before = jnp.where(t == 0, 0.0, ax_before[...])
        dax = (cw3[...] * d + cw2[...] * _rows_up(d, after, 1) + cw1[...] * _rows_up(d, after, 2)
               + cw0[...] * _rows_up(d, after, 3))
        dax_o[...] = dax.astype(dax_o.dtype)
        sums = [_colsum(d * _rows_down(x, before, 3)), _colsum(d * _rows_down(x, before, 2)),
                _colsum(d * _rows_down(x, before, 1)), _colsum(d * x), _colsum(d)]
        outs = [dcw0_o, dcw1_o, dcw2_o, dcw3_o, dcb_o]

        @pl.when(t == 0)
        def _():
            for o, val in zip(outs, sums):
                o[...] = val

        @pl.when(t > 0)
        def _():
            for o, val in zip(outs, sums):
                o[...] += val

    res = pl.pallas_call(
        body,
        name="conv_bwd",
        grid=(nb,),
        in_specs=[pl.BlockSpec((T, BRANCH), lambda t: (t, 0)), _halo_after(T, lambda t: t, S),
                  pl.BlockSpec((T, BRANCH), lambda t: (t, U_AX // BRANCH)),
                  _halo_before(T, lambda t: t, U_AX // BRANCH)] + [_full_spec(v) for v in vecs],
        out_specs=[pl.BlockSpec((T, BRANCH), lambda t: (t, 0))] + [vrow] * 5,
        out_shape=[jax.ShapeDtypeStruct((S, BRANCH), BF16)] + [jax.ShapeDtypeStruct((1, BRANCH), F32)] * 5,
        compiler_params=pltpu.CompilerParams(dimension_semantics=("arbitrary",), vmem_limit_bytes=VMEM_LIMIT_BYTES),
    )(dxc, dxc, u, u, *vecs)
    return res[0], res[1:]


GLA_QK = GLA_HEADS * GLA_DK
GLA_V = GLA_HEADS * GLA_DV
GLA_SCALE = GLA_DK ** -0.5


def _gla_specs(TB, rev_nb=None):
    def rmap(t):
        return t if rev_nb is None else rev_nb - 1 - t
    return [
        pl.BlockSpec((TB, GLA_QK), lambda t: (rmap(t), U_BQ // GLA_QK)),
        pl.BlockSpec((TB, GLA_QK), lambda t: (rmap(t), U_BK // GLA_QK)),
        pl.BlockSpec((TB, GLA_V), lambda t: (rmap(t), U_BV // GLA_V)),
        pl.BlockSpec((TB, GLA_V), lambda t: (rmap(t), U_BR // GLA_V)),
        pl.BlockSpec((TB, LANES), lambda t: (rmap(t), U_BLOW // LANES)),
    ]


def _gla_gates(gl, wg2, bg, TB):
    pre = _dot(gl, wg2, "nn") + bg
    la = _log_sigmoid(pre) * (1.0 / GLA_TAU)
    _, gc = _scan_rows(None, la, TB, seg=CHUNK)
    return pre, la, gc


def _gla_fwd(u, gw, S):
    TB = min(512, S)
    nb = S // TB
    cpb = TB // CHUNK
    vecs = [gw["wg2"], gw["bg"], gw["ng"], gw["bd"]]

    def body(q_r, k_r, v_r, br_r, gl_r, wg2, bg, ng, bd, yb_o, oraw_o, st_o, st):
        t = pl.program_id(0)

        @pl.when(t == 0)
        def _():
            st[...] = jnp.zeros_like(st)

        _, la, gc = _gla_gates(gl_r[...], wg2[...], bg[...], TB)
        for c in range(cpb):
            sl = slice(c * CHUNK, (c + 1) * CHUNK)
            gt = _colsum(la[sl])
            kdec = k_r[sl, :] * jnp.exp(gt - gc[sl])
            d_t = _dot(v_r[sl, :], kdec, "tn") * bd[...]
            s_new = st[...] * jnp.exp(gt) + d_t
            st[...] = s_new
            st_o[c] = s_new
            oraw_o[sl, :] = _dot(q_r[sl, :] * GLA_SCALE, s_new, "nt")
        for h in range(GLA_HEADS):
            hs = slice(h * GLA_DV, (h + 1) * GLA_DV)
            oh = oraw_o[:, hs]
            on = oh * lax.rsqrt(jnp.mean(oh * oh, axis=-1, keepdims=True) + RMS_EPS)
            sil, _ = _silu_and_grad(br_r[:, hs])
            yb_o[:, hs] = (on * ng[:, hs] * sil).astype(yb_o.dtype)

    return pl.pallas_call(
        body,
        name="gla_fwd",
        grid=(nb,),
        in_specs=_gla_specs(TB) + [_full_spec(v) for v in vecs],
        out_specs=[pl.BlockSpec((TB, GLA_V), lambda t: (t, 0)), pl.BlockSpec((TB, GLA_V), lambda t: (t, 0)),
                   pl.BlockSpec((cpb, GLA_V, GLA_QK), lambda t: (t, 0, 0))],
        out_shape=[jax.ShapeDtypeStruct((S, GLA_V), BF16), jax.ShapeDtypeStruct((S, GLA_V), F32),
                   jax.ShapeDtypeStruct((S // CHUNK, GLA_V, GLA_QK), F32)],
        scratch_shapes=[pltpu.VMEM((GLA_V, GLA_QK), F32)],
        compiler_params=pltpu.CompilerParams(dimension_semantics=("arbitrary",), vmem_limit_bytes=VMEM_LIMIT_BYTES),
    )(u, u, u, u, u, *vecs)


def _gla_bwd(dyb, u, oraw, states, gw, S):
    TB = min(512, S)
    nb = S // TB
    cpb = TB // CHUNK
    vecs = [gw["wg2"], gw["bg"], gw["ng"], gw["bd"]]

    def rrow(width):
        return pl.BlockSpec((TB, width), lambda t: (nb - 1 - t, 0))

    def body(dyb_r, oraw_r, q_r, k_r, v_r, br_r, gl_r, st_r, sp_r, wg2, bg, ng, bd,
             dq_o, dk_o, dv_o, dbr_o, dgl_o, dwg2_o, dbg_o, dng_o, dcar, do_buf, dla_buf):
        t = pl.program_id(0)
        blk = nb - 1 - t

        @pl.when(t == 0)
        def _():
            dcar[...] = jnp.zeros_like(dcar)
            dwg2_o[...] = jnp.zeros_like(dwg2_o)
            dbg_o[...] = jnp.zeros_like(dbg_o)
            dng_o[...] = jnp.zeros_like(dng_o)

        pre, la, gc = _gla_gates(gl_r[...], wg2[...], bg[...], TB)
        for h in range(GLA_HEADS):
            hs = slice(h * GLA_DV, (h + 1) * GLA_DV)
            oh = oraw_r[:, hs]
            rs = lax.rsqrt(jnp.mean(oh * oh, axis=-1, keepdims=True) + RMS_EPS)
            on = oh * rs
            sil, dsil = _silu_and_grad(br_r[:, hs])
            dy = dyb_r[:, hs]
            dbr_o[:, hs] = (dy * on * ng[:, hs] * dsil).astype(dbr_o.dtype)
            don = dy * ng[:, hs] * sil
            dng_o[:, hs] += _colsum(dy * on * sil)
            do_buf[:, hs] = rs * (don - on * jnp.mean(don * on, axis=-1, keepdims=True))
        first = jnp.where(blk == 0, 0.0, 1.0)
        for c in reversed(range(cpb)):
            sl = slice(c * CHUNK, (c + 1) * CHUNK)
            s_n = st_r[c]
            s_prev = st_r[c - 1] if c > 0 else sp_r[0] * first
            gt = _colsum(la[sl])
            w = jnp.exp(gt - gc[sl])
            k_c = k_r[sl, :]
            kdec = k_c * w
            qs = q_r[sl, :] * GLA_SCALE
            do_c = do_buf[sl, :]
            dq_o[sl, :] = (_dot(do_c, s_n, "nn") * GLA_SCALE).astype(dq_o.dtype)
            d_n = _dot(do_c, qs, "tn") * bd[...] + dcar[...]
            dv_o[sl, :] = _dot(kdec, d_n, "nt").astype(dv_o.dtype)
            dkdec = _dot(v_r[sl, :], d_n, "nn")
            dk_o[sl, :] = (dkdec * w).astype(dk_o.dtype)
            tt = dkdec * kdec
            e = jnp.exp(gt)
            dgt = _colsum(tt) + _colsum(d_n * s_prev) * e
            _, rc = _scan_rows(None, -tt, CHUNK, reverse=True)
            dla_buf[sl, :] = rc + dgt
            dcar[...] = d_n * e
        dpre = dla_buf[...] * _sigmoid(-pre) * (1.0 / GLA_TAU)
        dbg_o[...] += _colsum(dpre)
        dgl_o[...] = _dot(dpre, wg2[...], "nt").astype(dgl_o.dtype)
        dwg2_o[...] += _dot(gl_r[...], dpre, "tn")

    return pl.pallas_call(
        body,
        name="gla_bwd",
        grid=(nb,),
        in_specs=[rrow(GLA_V), rrow(GLA_V)] + _gla_specs(TB, rev_nb=nb)
        + [pl.BlockSpec((cpb, GLA_V, GLA_QK), lambda t: (nb - 1 - t, 0, 0)),
           pl.BlockSpec((1, GLA_V, GLA_QK), lambda t: (jnp.maximum((nb - 1 - t) * cpb - 1, 0), 0, 0))]
        + [_full_spec(v) for v in vecs],
        out_specs=[rrow(GLA_QK), rrow(GLA_QK), rrow(GLA_V), rrow(GLA_V), rrow(LANES),
                   pl.BlockSpec((LANES, GLA_QK), lambda t: (0, 0)), pl.BlockSpec((1, GLA_QK), lambda t: (0, 0)),
                   pl.BlockSpec((1, GLA_V), lambda t: (0, 0))],
        out_shape=[jax.ShapeDtypeStruct((S, GLA_QK), BF16), jax.ShapeDtypeStruct((S, GLA_QK), BF16),
                   jax.ShapeDtypeStruct((S, GLA_V), BF16), jax.ShapeDtypeStruct((S, GLA_V), BF16),
                   jax.ShapeDtypeStruct((S, LANES), BF16), jax.ShapeDtypeStruct((LANES, GLA_QK), F32),
                   jax.ShapeDtypeStruct((1, GLA_QK), F32), jax.ShapeDtypeStruct((1, GLA_V), F32)],
        scratch_shapes=[pltpu.VMEM((GLA_V, GLA_QK), F32), pltpu.VMEM((TB, GLA_V), F32),
                        pltpu.VMEM((TB, GLA_QK), F32)],
        compiler_params=pltpu.CompilerParams(dimension_semantics=("arbitrary",), vmem_limit_bytes=VMEM_LIMIT_BYTES),
    )(dyb, oraw, u, u, u, u, u, states, states, *vecs)


FOX_SCALE = FOX_DH ** -0.5


def _fox_gate_fwd(u, bfp, S):
    T = min(512, S)

    def body(f_r, b_r, fc_o, car):
        t = pl.program_id(0)

        @pl.when(t == 0)
        def _():
            car[...] = jnp.zeros_like(car)

        _, cs = _scan_rows(None, _log_sigmoid(f_r[...] + b_r[...]), T)
        fc_o[...] = cs + car[...]
        car[...] = fc_o[pl.ds(T - 1, 1), :]

    return pl.pallas_call(
        body,
        name="fox_gate_fwd",
        grid=(S // T,),
        in_specs=[pl.BlockSpec((T, LANES), lambda t: (t, U_CF // LANES)), _full_spec(bfp)],
        out_specs=pl.BlockSpec((T, LANES), lambda t: (t, 0)),
        out_shape=jax.ShapeDtypeStruct((S, LANES), F32),
        scratch_shapes=[pltpu.VMEM((1, LANES), F32)],
        compiler_params=pltpu.CompilerParams(dimension_semantics=("arbitrary",), vmem_limit_bytes=VMEM_LIMIT_BYTES),
    )(u, bfp)


def _fox_gate_bwd(dfc, u, bfp, S):
    T = min(512, S)
    nb = S // T

    def body(d_r, f_r, b_r, df_o, db_o, car, tmp):
        t = pl.program_id(0)

        @pl.when(t == 0)
        def _():
            car[...] = jnp.zeros_like(car)
            db_o[...] = jnp.zeros_like(db_o)

        _, rc = _scan_rows(None, d_r[...], T, reverse=True)
        tmp[...] = rc + car[...]
        car[...] = tmp[pl.ds(0, 1), :]
        df = tmp[...] * _sigmoid(-(f_r[...] + b_r[...]))
        df_o[...] = df.astype(df_o.dtype)
        db_o[...] += _colsum(df)

    return pl.pallas_call(
        body,
        name="fox_gate_bwd",
        grid=(nb,),
        in_specs=[pl.BlockSpec((T, LANES), lambda t: (nb - 1 - t, 0)),
                  pl.BlockSpec((T, LANES), lambda t: (nb - 1 - t, U_CF // LANES)), _full_spec(bfp)],
        out_specs=[pl.BlockSpec((T, LANES), lambda t: (nb - 1 - t, 0)), pl.BlockSpec((1, LANES), lambda t: (0, 0))],
        out_shape=[jax.ShapeDtypeStruct((S, LANES), BF16), jax.ShapeDtypeStruct((1, LANES), F32)],
        scratch_shapes=[pltpu.VMEM((1, LANES), F32), pltpu.VMEM((T, LANES), F32)],
        compiler_params=pltpu.CompilerParams(dimension_semantics=("arbitrary",), vmem_limit_bytes=VMEM_LIMIT_BYTES),
    )(dfc, u, bfp)


def _fox_call(name, body, tables, grid, in_specs, out_specs, out_shape, scratch, args, side):
    n_in, n_out, n_scr = len(in_specs), len(out_specs), len(scratch)
    semantics = ("parallel", "arbitrary")
    if side is not None:
        total = grid[0] * grid[1]
        phases = side["phases"]
        triggers = [0, total - 1] if len(phases) == 2 else [0, total * 7 // 10, total - 1]
        na, no = len(side["arrs"]), len(side["out_shapes"])
        s_in, s_out, s_sems = _side_specs(side)
        kernel_body = body

        def body(*refs):
            tabs, rest = refs[:len(tables)], refs[len(tables):]
            ins, s_ins = rest[:n_in], rest[n_in:n_in + na]
            rest = rest[n_in + na:]
            outs, s_outs = rest[:n_out], rest[n_out:n_out + no]
            rest = rest[n_out + no:]
            scr, sems = rest[:n_scr], rest[n_scr:]
            flat = pl.program_id(0) * grid[1] + pl.program_id(1)
            for trigger, phase in zip(triggers[:-1], phases[:-1]):
                @pl.when(flat == trigger)
                def _(phase=phase):
                    phase(s_ins, s_outs, *sems)
            kernel_body(*tabs, *ins, *outs, *scr)

            @pl.when(flat == triggers[-1])
            def _():
                phases[-1](s_ins, s_outs, *sems)

        in_specs, out_specs = in_specs + s_in, out_specs + s_out
        out_shape, scratch = out_shape + side["out_shapes"], scratch + s_sems
        args = list(args) + side["arrs"]
        semantics = ("arbitrary", "arbitrary")
    res = pl.pallas_call(
        body,
        name=name,
        grid_spec=pltpu.PrefetchScalarGridSpec(num_scalar_prefetch=len(tables), grid=grid, in_specs=in_specs,
                                               out_specs=out_specs, scratch_shapes=scratch),
        out_shape=out_shape,
        compiler_params=pltpu.CompilerParams(dimension_semantics=semantics, vmem_limit_bytes=VMEM_LIMIT_BYTES),
    )(*tables, *args)
    return res[:n_out], res[n_out:]


FOX_TILE = 1024
FOX_GROUP = 2
FOX_GROUP_FWD = 4
FOX_AUG = 128
FOX_ONES = 3


def _fox_pairs(n, by_key):
    pairs = [(qi, ki) for qi in range(n) for ki in range(qi + 1)]
    if by_key:
        pairs.sort(key=lambda qk: (qk[1], qk[0]))
    qs = jnp.asarray([qk[0] for qk in pairs], jnp.int32)
    ks = jnp.asarray([qk[1] for qk in pairs], jnp.int32)
    return qs, ks


def _fox_causal(sT):
    keys = lax.broadcasted_iota(jnp.int32, sT.shape, 0)
    queries = lax.broadcasted_iota(jnp.int32, sT.shape, 1)
    return jnp.where(keys <= queries, sT, NEG_BIG)


def _fox_fwd(qT, ka, vT, S, side=None):
    t = min(FOX_TILE, S)
    n = S // t
    qi_tab, ki_tab = _fox_pairs(n, by_key=False)

    G = FOX_GROUP_FWD

    def body(qi_ref, ki_ref, qT_r, ka_r, vT_r, oT_o, lse_o, m_s, l_s, acc):
        step = pl.program_id(1)
        qi, ki = qi_ref[step], ki_ref[step]

        @pl.when(ki == 0)
        def _():
            m_s[...] = jnp.full_like(m_s, NEG_BIG)
            l_s[...] = jnp.zeros_like(l_s)
            acc[...] = jnp.zeros_like(acc)

        def update(g, masked):
            sT = _dot(ka_r[g], qT_r[g], "nn")
            if masked:
                sT = _fox_causal(sT)
            m_new = jnp.maximum(m_s[g], jnp.max(sT, axis=0, keepdims=True))
            p = jnp.exp(sT - m_new)
            alpha = jnp.exp(m_s[g] - m_new)
            l_s[g] = alpha * l_s[g] + jnp.sum(p, axis=0, keepdims=True)
            acc[g] = alpha * acc[g] + _dot(vT_r[g], p, "nn")
            m_s[g] = m_new

        @pl.when(ki < qi)
        def _():
            for g in range(G):
                update(g, False)

        @pl.when(ki == qi)
        def _():
            for g in range(G):
                update(g, True)
                oT_o[g] = acc[g] / l_s[g]
                lse_o[g] = m_s[g] + jnp.log(l_s[g])

    return _fox_call(
        "fox_fwd", body, (qi_tab, ki_tab), (FOX_HEADS // G, int(qi_tab.shape[0])),
        [pl.BlockSpec((G, FOX_AUG, t), lambda h, s, qt, kt: (h, 0, qt[s])),
         pl.BlockSpec((G, t, FOX_AUG), lambda h, s, qt, kt: (h, kt[s], 0)),
         pl.BlockSpec((G, FOX_DH, t), lambda h, s, qt, kt: (h, 0, kt[s]))],
        [pl.BlockSpec((G, FOX_DH, t), lambda h, s, qt, kt: (h, 0, qt[s])),
         pl.BlockSpec((G, 1, t), lambda h, s, qt, kt: (h, 0, qt[s]))],
        [jax.ShapeDtypeStruct((FOX_HEADS, FOX_DH, S), F32), jax.ShapeDtypeStruct((FOX_HEADS, 1, S), F32)],
        [pltpu.VMEM((G, 1, t), F32), pltpu.VMEM((G, 1, t), F32), pltpu.VMEM((G, FOX_DH, t), F32)],
        (qT, ka, vT), side)


FOX_BIAS_ROWS = 8


def _fox_bwd(qT, qa, ka, kT, v, do, doT, oT, lse, S, side=None):
    t = min(FOX_TILE, S)
    n = S // t
    qi_tab, ki_tab = _fox_pairs(n, by_key=True)
    n_steps = int(qi_tab.shape[0])
    slab = slice(FOX_DH, FOX_DH + FOX_BIAS_ROWS)

    G = FOX_GROUP

    def body(qi_ref, ki_ref, qT_r, qa_r, ka_r, kT_r, v_r, do_r, doT_r, oT_r, lse_r,
             dq_o, dfq_o, dk_o, dfk_o, dv_o, dq_acc, dk_acc, dv_acc):
        step = pl.program_id(1)
        qi, ki = qi_ref[step], ki_ref[step]

        @pl.when(step == 0)
        def _():
            dq_acc[...] = jnp.zeros_like(dq_acc)

        @pl.when(qi == ki)
        def _():
            dk_acc[...] = jnp.zeros_like(dk_acc)
            dv_acc[...] = jnp.zeros_like(dv_acc)

        def update(g, masked):
            sT = _dot(ka_r[g], qT_r[g], "nn")
            if masked:
                sT = _fox_causal(sT)
            pT = jnp.exp(sT - lse_r[g])
            delta = jnp.sum(oT_r[g] * doT_r[g], axis=0, keepdims=True)
            dsT = pT * (_dot(v_r[g], doT_r[g], "nn") - delta)
            dv_acc[g] += _dot(pT, do_r[g], "nn")
            dk_acc[g] += _dot(dsT, qa_r[g], "nn")
            dq_acc[g, qi] += _dot(kT_r[g], dsT, "nn")

        @pl.when(qi > ki)
        def _():
            for g in range(G):
                update(g, False)

        @pl.when(qi == ki)
        def _():
            for g in range(G):
                update(g, True)

        @pl.when(qi == n - 1)
        def _():
            for g in range(G):
                dk = dk_acc[g]
                dk_o[g] = dk[:, :FOX_DH].astype(dk_o.dtype)
                dfk_o[g] = dk.T[slab]
                dv_o[g] = dv_acc[g].astype(dv_o.dtype)

        @pl.when(step == n_steps - 1)
        def _():
            for g in range(G):
                for j in range(n):
                    dqT = dq_acc[g, j]
                    dq_o[g, j * t:(j + 1) * t, :] = (dqT.T[:, :FOX_DH] * FOX_SCALE).astype(dq_o.dtype)
                    dfq_o[g, :, j * t:(j + 1) * t] = dqT[slab]

    def qlane(rows):
        return pl.BlockSpec((G, rows, t), lambda h, s, qt, kt: (h, 0, qt[s]))

    def qrow(cols):
        return pl.BlockSpec((G, t, cols), lambda h, s, qt, kt: (h, qt[s], 0))

    def krow(cols):
        return pl.BlockSpec((G, t, cols), lambda h, s, qt, kt: (h, kt[s], 0))

    def klane(rows):
        return pl.BlockSpec((G, rows, t), lambda h, s, qt, kt: (h, 0, kt[s]))

    def head(rows, cols):
        return pl.BlockSpec((G, rows, cols), lambda h, s, qt, kt: (h, 0, 0))

    return _fox_call(
        "fox_bwd", body, (qi_tab, ki_tab), (FOX_HEADS // G, n_steps),
        [qlane(FOX_AUG), qrow(FOX_AUG), krow(FOX_AUG), klane(FOX_AUG), krow(FOX_DH), qrow(FOX_DH), qlane(FOX_DH),
         qlane(FOX_DH), qlane(1)],
        [head(S, FOX_DH), head(FOX_BIAS_ROWS, S), krow(FOX_DH), klane(FOX_BIAS_ROWS), krow(FOX_DH)],
        [jax.ShapeDtypeStruct((FOX_HEADS, S, FOX_DH), BF16), jax.ShapeDtypeStruct((FOX_HEADS, FOX_BIAS_ROWS, S), F32),
         jax.ShapeDtypeStruct((FOX_HEADS, S, FOX_DH), BF16), jax.ShapeDtypeStruct((FOX_HEADS, FOX_BIAS_ROWS, S), F32),
         jax.ShapeDtypeStruct((FOX_HEADS, S, FOX_DH), BF16)],
        [pltpu.VMEM((G, n, FOX_AUG, t), F32), pltpu.VMEM((G, t, FOX_AUG), F32), pltpu.VMEM((G, t, FOX_DH), F32)],
        (qT, qa, ka, kT, v, do, doT, oT, lse), side)


def _fox_prep(u, fcum, S):
    T = min(512, S)
    head_of = jnp.arange(BRANCH) // FOX_DH
    dim_of = jnp.arange(BRANCH) % FOX_DH
    heads = jnp.arange(FOX_HEADS)[:, None, None]
    sel = (head_of[None, :, None] == heads) & (dim_of[None, :, None] == jnp.arange(FOX_AUG)[None, None, :])
    sel_q = (sel * FOX_SCALE).astype(BF16)
    sel_k = sel.astype(BF16)
    sel_vT = jnp.swapaxes(sel[:, :, :FOX_DH], 1, 2).astype(BF16)
    piece = jnp.arange(FOX_ONES * LANES) // LANES
    lane = jnp.arange(FOX_ONES * LANES) % LANES
    col = jnp.arange(FOX_AUG)[None, None, :]
    at_q = (lane[None, :, None] == heads) & (col == FOX_DH + FOX_ONES + piece[None, :, None])
    at_k = (lane[None, :, None] == heads) & (col == FOX_DH + piece[None, :, None])
    bias_q = at_q.astype(BF16)
    bias_k = (-at_k.astype(F32)).astype(BF16)
    cols = jnp.arange(FOX_AUG)[None, :]
    ones_q = ((cols >= FOX_DH) & (cols < FOX_DH + FOX_ONES)).astype(F32)
    ones_k = ((cols >= FOX_DH + FOX_ONES) & (cols < FOX_DH + 2 * FOX_ONES)).astype(F32)
    consts = [sel_q, sel_k, sel_vT, bias_q, bias_k, ones_q, ones_k]

    def body(cq, ck, cv, fc, sq, sk, svT, bq, bk, oq, ok, qa_o, ka_o, qT_o, kT_o, vh_o, vT_o):
        f = fc[...]
        hi = f.astype(BF16).astype(F32)
        mid = (f - hi).astype(BF16).astype(F32)
        lo = (f - hi - mid).astype(BF16).astype(F32)
        pieces = jnp.concatenate([hi, mid, lo], axis=1)
        q, k, v = cq[...], ck[...], cv[...]
        for h in range(FOX_HEADS):
            qa = _dot(q, sq[h], "nn") + _dot(pieces, bq[h], "nn") + oq[...]
            ka = _dot(k, sk[h], "nn") + _dot(pieces, bk[h], "nn") + ok[...]
            qa_o[h] = qa.astype(qa_o.dtype)
            ka_o[h] = ka.astype(ka_o.dtype)
            qT_o[h] = qa.T.astype(qT_o.dtype)
            kT_o[h] = ka.T.astype(kT_o.dtype)
            vT_o[h] = _dot(svT[h], v, "nt").astype(vT_o.dtype)
            vh_o[h] = _dot(v, svT[h], "nt").astype(vh_o.dtype)

    def win(off):
        return pl.BlockSpec((T, BRANCH), functools.partial(lambda i, blk: (i, blk), blk=off // BRANCH))

    def rows(c):
        return pl.BlockSpec((FOX_HEADS, T, c), lambda i: (0, i, 0))

    def lanes(r):
        return pl.BlockSpec((FOX_HEADS, r, T), lambda i: (0, 0, i))

    bf = lambda *shape: jax.ShapeDtypeStruct((FOX_HEADS,) + shape, BF16)
    return pl.pallas_call(
        body,
        name="fox_prep",
        grid=(S // T,),
        in_specs=[win(U_CQ), win(U_CK), win(U_CV), pl.BlockSpec((T, LANES), lambda i: (i, 0))]
        + [_full_spec(c) for c in consts],
        out_specs=[rows(FOX_AUG), rows(FOX_AUG), lanes(FOX_AUG), lanes(FOX_AUG), rows(FOX_DH), lanes(FOX_DH)],
        out_shape=[bf(S, FOX_AUG), bf(S, FOX_AUG), bf(FOX_AUG, S), bf(FOX_AUG, S), bf(S, FOX_DH), bf(FOX_DH, S)],
        compiler_params=pltpu.CompilerParams(dimension_semantics=("parallel",), vmem_limit_bytes=VMEM_LIMIT_BYTES),
    )(u, u, u, fcum, *consts)


def _to_heads(x2d, S):
    return jnp.transpose(x2d.reshape(S, FOX_HEADS, FOX_DH), (1, 0, 2))


def _from_heads(xh, S):
    return jnp.transpose(xh, (1, 0, 2)).reshape(S, FOX_HEADS * FOX_DH)


def _ffn_fwd(tag, x, wgT, wuT, wd, g, b, S, side=None):
    def up_epi(accs):
        gate, up = accs
        sil, _ = _silu_and_grad(gate)
        return [gate, up, sil * up]

    res = _mm(tag + "_up", "nt", [x], [wgT, wuT], [(0, 0, 0), (1, 0, 1)], 2, up_epi, [],
              [BF16, BF16, BF16], S, D_FF, D_MODEL, tn=1408, side=side)
    (gate, up, act), side_out = res if side is not None else (res, None)

    def down_epi(accs, xr, gg, bb):
        z = ALPHA * xr + 0.5 * accs[0]
        return [z, _ln_fwd(z, gg, bb)]

    z, xn = _mm(tag + "_down", "nn", [act], [wd], [(0, 0, 0)], 1, down_epi, [(x, "mn", 0), (g, "n"), (b, "n")],
                [F32, F32], S, D_MODEL, D_FF, tk=D_FF)
    return xn, dict(x=x, gate=gate, up=up, act=act, z=z), side_out


def _ln_bwd_call(tag, dy, z, g, S):
    def fn(dy_t, z_t, g_t):
        dz, xhat = _ln_bwd(dy_t, z_t, g_t)
        return [dz], [_colsum(dy_t * xhat), _colsum(dy_t)]

    (dz,), (dg, db) = _rowwise(tag + "_ln_bwd", fn, [dy, z], [g], [(D_MODEL, F32)], [D_MODEL, D_MODEL], S)
    return dz, dg, db


def _ffn_bwd(tag, dxn, sv, wgT, wuT, wd, g, S, gdt=F32, make_side=None):
    dz, dg, db = _ln_bwd_call(tag, dxn, sv["z"], g, S)

    def act_epi(accs, gate, up):
        da = 0.5 * accs[0]
        sil, dsil = _silu_and_grad(gate.astype(F32))
        return [da * up.astype(F32) * dsil, da * sil]

    dgate, dup = _mm(tag + "_dact", "nt", [dz], [wd], [(0, 0, 0)], 1, act_epi,
                     [(sv["gate"], "mn", 0), (sv["up"], "mn", 0)], [BF16, BF16], S, D_FF, D_MODEL, tn=1408)
    dwd = _mm1(tag + "_dwd", "tn", sv["act"], dz, D_FF, D_MODEL, S, scale=0.5, tm=1408, out_dtype=gdt)

    def two(accs):
        return [accs[0], accs[1]]

    dwgT, dwuT = _mm(tag + "_dwup", "tn", [dgate, dup], [sv["x"]], [(0, 0, 0), (1, 1, 0)], 2, two, [], [gdt, gdt],
                     D_FF, D_MODEL, S, tm=1408, tk=512)

    def dx_epi(accs, dzr):
        return [accs[0] + ALPHA * dzr]

    grads = dict(w_upT=jnp.concatenate([dwgT, dwuT], axis=0), w_down=dwd, ln_g=dg, ln_b=db)
    side = make_side(grads) if make_side is not None else None
    res = _mm(tag + "_dx", "nn", [dgate, dup], [wgT, wuT], [(0, 0, 0), (0, 1, 1)], 1, dx_epi, [(dz, "mn", 0)],
              [F32], S, D_MODEL, D_FF, tm=1024, tk=1408, side=side)
    (dx,), side_out = res if side is not None else (res, None)
    return dx, grads, side_out


def _mixer_fwd(x1, w, S, side=None, on_side=None):
    u = _mm1("w_in", "nt", x1, w["w_inT_p"], S, U_WIDTH, D_MODEL, tm=1024, tn=1536)
    xc, r, gi, a, h, y_a = _lru_fwd(u, w["lru"], S)
    y_b, oraw, states = _gla_fwd(u, w["gla"], S)
    fcum = _fox_gate_fwd(u, w["bfp"], S)
    qa, ka, qT, kT, vh, vT = _fox_prep(u, fcum, S)
    (oT, lse), side_out = _fox_fwd(qT, ka, vT, S, side)
    if on_side is not None:
        on_side(side_out)
    y_c = jnp.transpose(oT, (2, 0, 1)).reshape(S, BRANCH).astype(BF16)

    def merge_epi(accs, g0, g1, g2):
        merged = _sigmoid(g0) * accs[0] + _sigmoid(g1) * accs[1] + _sigmoid(g2) * accs[2]
        return [accs[0], accs[1], accs[2], merged]

    wb = w["w_branchT"]
    yp0, yp1, yp2, merged = _mm(
        "merge", "nt", [y_a, y_b, y_c], [wb[0], wb[1], wb[2]], [(0, 0, 0), (1, 1, 1), (2, 2, 2)], 3, merge_epi,
        [(u, "mn", 0), (u, "mn", 1), (u, "mn", 2)], [BF16, BF16, BF16, BF16], S, D_MODEL, BRANCH, tm=256)

    def out_epi(accs, xr, gg, bb):
        z = ALPHA * xr + accs[0]
        return [z, _ln_fwd(z, gg, bb)]

    z2, x2 = _mm("w_out", "nn", [merged], [w["w_out"]], [(0, 0, 0)], 1, out_epi,
                 [(x1, "mn", 0), (w["ln2_g"], "n"), (w["ln2_b"], "n")], [F32, F32], S, D_MODEL, D_MODEL)
    sv = dict(x=x1, u=u, xc=xc, r=r, i=gi, a=a, h=h, y_a=y_a, y_b=y_b, y_c=y_c, oraw=oraw,
              states=states, qT=qT, qa=qa, ka=ka, kT=kT, vh=vh, oT=oT, lse=lse, yp=(yp0, yp1, yp2), merged=merged,
              z=z2)
    return x2, sv, side_out


def _mixer_bwd(dx2, sv, w, S, make_side=None, gdt_a=F32, gdt_b=F32):
    u = sv["u"]
    dz, dg2, db2 = _ln_bwd_call("mix", dx2, sv["z"], w["ln2_g"], S)

    def dm_epi(accs, y0, y1, y2, g0, g1, g2):
        dm = accs[0]
        outs_p, outs_g = [], []
        for yp, gl in ((y0, g0), (y1, g1), (y2, g2)):
            sg = _sigmoid(gl)
            outs_p.append(dm * sg)
            outs_g.append(dm * yp.astype(F32) * sg * (1.0 - sg))
        return outs_p + outs_g

    yp = sv["yp"]
    dyp0, dyp1, dyp2, dgl0, dgl1, dgl2 = _mm(
        "dmerged", "nt", [dz], [w["w_out"]], [(0, 0, 0)], 1, dm_epi,
        [(yp[0], "mn", 0), (yp[1], "mn", 0), (yp[2], "mn", 0), (u, "mn", 0), (u, "mn", 1), (u, "mn", 2)],
        [BF16] * 6, S, D_MODEL, D_MODEL, tm=256)
    dw_out = _mm1("dw_out", "tn", sv["merged"], dz, D_MODEL, D_MODEL, S, out_dtype=gdt_b)
    wb = w["w_branchT"]
    dys, dwbs = [], []
    for j, (yj, dyp) in enumerate(((sv["y_a"], dyp0), (sv["y_b"], dyp1), (sv["y_c"], dyp2))):
        dys.append(_mm1("dy_branch%d" % j, "nn", dyp, wb[j], S, BRANCH, D_MODEL))
        dwbs.append(_mm1("dw_branch%d" % j, "tn", dyp, yj, D_MODEL, BRANCH, S, out_dtype=gdt_b))
    day, dxc, dwa, dwx, dba, dbx, dlam = _lru_bwd(dys[0], u, sv, w["lru"], S)
    dax, (dcw0, dcw1, dcw2, dcw3, dcb) = _conv_bwd(dxc, u, w["lru"], S)
    dbq, dbk, dbv, dbr, dglow, dwg2p, dbg, dng = _gla_bwd(dys[1], u, sv["oraw"], sv["states"], w["gla"], S)
    doh = _to_heads(dys[2], S)
    dw_branchT = jnp.stack(dwbs)
    side = make_side(dict(w_out=dw_out, w_branchT=dw_branchT)) if make_side is not None else None
    (dqh, dfq, dkh, dfk, dvh), side_out = _fox_bwd(sv["qT"], sv["qa"], sv["ka"], sv["kT"], sv["vh"], doh,
                                                   jnp.swapaxes(doh, 1, 2), sv["oT"], sv["lse"], S, side)
    dfc = jnp.transpose(dfq[:, FOX_ONES, :] - dfk[:, 0, :])
    dfc = jnp.pad(dfc, ((0, 0), (0, LANES - FOX_HEADS)))
    dcf, dbf = _fox_gate_bwd(dfc, u, w["bfp"], S)
    du = jnp.concatenate(
        [dgl0, dgl1, dgl2, dax, day, dbq, dbk, dbv, dbr, _from_heads(dqh, S).astype(BF16),
         _from_heads(dkh, S).astype(BF16), _from_heads(dvh, S).astype(BF16), dglow, dcf,
         jnp.zeros((S, U_WIDTH - U_CF - LANES), BF16)], axis=1)
    dw_inT_p = _mm1("dw_in", "tn", du, sv["x"], U_WIDTH, D_MODEL, S, tm=1536, out_dtype=gdt_a)

    def dx_epi(accs, dzr):
        return [accs[0] + ALPHA * dzr]

    (dx1,) = _mm("dx_mix", "nn", [du], [w["w_inT_p"]], [(0, 0, 0)], 1, dx_epi, [(dz, "mn", 0)], [F32], S, D_MODEL,
                 U_WIDTH, tm=1024, tk=1536)
    pieces = sorted(W_IN_SEGMENTS)
    dw_inT = jnp.concatenate([dw_inT_p[dst:dst + width] for _, width, dst in pieces], axis=0)
    eye = jnp.eye(LRU_BLOCKS, dtype=F32)
    dwa_b = jnp.einsum("ncmd,nm->ncd", dwa.reshape(LRU_BLOCKS, 64, LRU_BLOCKS, 64), eye)
    dwx_b = jnp.einsum("ncmd,nm->ncd", dwx.reshape(LRU_BLOCKS, 64, LRU_BLOCKS, 64), eye)
    grads = dict(
        w_inT=dw_inT, w_out=dw_out, w_branchT=dw_branchT, ln2_g=dg2, ln2_b=db2,
        conv_w=jnp.concatenate([dcw0, dcw1, dcw2, dcw3], axis=0).astype(gdt_a), conv_b=dcb, lru_wa=dwa_b, lru_wx=dwx_b,
        lru_ba=dba, lru_bx=dbx, lru_lambda=dlam, gla_w_g2=dwg2p[:GLA_LOWRANK].astype(gdt_a), gla_b_g=dbg, gla_norm_g=dng,
        fox_b_f=dbf[:, :FOX_HEADS])
    return dx1, grads, side_out


def _ple_fwd(x3, p_i, w, S):
    pe = _mm1("ple_proj", "nt", p_i, w["ple_w_projT"], S, D_MODEL, PLE_DIM)

    def epi(accs, xr, per, bg, gg, bb):
        sg = _sigmoid(accs[0] + bg)
        z = ALPHA * xr + sg * per
        return [sg, z, _ln_fwd(z, gg, bb)]

    sg, z4, x4 = _mm("ple_gate", "nn", [x3], [w["ple_w_gate"]], [(0, 0, 0)], 1, epi,
                     [(x3, "mn", 0), (pe, "mn", 0), (w["ple_b_gate"], "n"), (w["ln4_g"], "n"), (w["ln4_b"], "n")],
                     [F32, F32, F32], S, D_MODEL, D_MODEL)
    return x4, dict(x=x3, p=p_i, pe=pe, sg=sg, z=z4)


def _ple_bwd(dx4, sv, w, S, gdt=F32):
    def fn(dy_t, z_t, pe_t, sg_t, g_t):
        dz, xhat = _ln_bwd(dy_t, z_t, g_t)
        dgl = dz * pe_t * sg_t * (1.0 - sg_t)
        return [dz, dz * sg_t, dgl], [_colsum(dy_t * xhat), _colsum(dy_t), _colsum(dgl)]

    (dz, dpe, dgl), (dg4, db4, dbg) = _rowwise(
        "ple_bwd", fn, [dx4, sv["z"], sv["pe"], sv["sg"]], [w["ln4_g"]],
        [(D_MODEL, F32), (D_MODEL, BF16), (D_MODEL, BF16)], [D_MODEL] * 3, S)
    dwpT = _mm1("dw_ple_proj", "tn", dpe, sv["p"], D_MODEL, PLE_DIM, S, out_dtype=gdt)
    dwg = _mm1("dw_ple_gate", "tn", sv["x"], dgl, D_MODEL, D_MODEL, S, out_dtype=gdt)

    def dx_epi(accs, dzr):
        return [accs[0] + ALPHA * dzr]

    (dx3,) = _mm("dx_ple", "nt", [dgl], [w["ple_w_gate"]], [(0, 0, 0)], 1, dx_epi, [(dz, "mn", 0)], [F32], S,
                 D_MODEL, D_MODEL)
    return dx3, dict(ple_w_projT=dwpT, ple_w_gate=dwg, ple_b_gate=dbg, ln4_g=dg4, ln4_b=db4)


def _rows_of_all(g):
    return g.reshape((g.shape[0] * g.shape[1],) + g.shape[2:])


EARLY_WEIGHTS = ("ffn1_w_up", "ffn1_w_down", "w_in", "conv_w", "gla_w_g2")


def _ffn_weights(gathered, tag):
    upT = _rows_of_all(gathered[tag + "_w_up"])
    return upT[:D_FF], upT[D_FF:], _rows_of_all(gathered[tag + "_w_down"])


def _late_weights(gathered):
    return dict(ffn2=_ffn_weights(gathered, "ffn2"),
                w_branchT=jnp.moveaxis(gathered["w_branch"], 0, 1).reshape(3, D_MODEL, BRANCH),
                w_out=_rows_of_all(gathered["w_out"]),
                ple_w_projT=_rows_of_all(gathered["ple_w_proj"]),
                ple_w_gate=_rows_of_all(gathered["ple_w_gate"]))


def _w_in_operand(gathered_w_in):
    w_inT = _rows_of_all(gathered_w_in)
    placed = sorted((dst, src, width) for src, width, dst in W_IN_SEGMENTS)
    parts, pos = [], 0
    for dst, src, width in placed:
        if dst > pos:
            parts.append(jnp.zeros((dst - pos, D_MODEL), w_inT.dtype))
        parts.append(w_inT[src:src + width])
        pos = dst + width
    parts.append(jnp.zeros((U_WIDTH - pos, D_MODEL), w_inT.dtype))
    return jnp.concatenate(parts, axis=0)


def _layer_weights(gathered, full, i):
    w = _late_weights(gathered) if "w_out" in gathered else {}
    w["ffn1"] = _ffn_weights(gathered, "ffn1")
    if "w_in" in gathered:
        w["w_inT_p"] = _w_in_operand(gathered["w_in"])
    eye = jnp.eye(LRU_BLOCKS, dtype=F32)

    def dense(blocks):
        return jnp.einsum("ncd,nm->ncmd", blocks, eye).reshape(BRANCH, BRANCH).astype(BF16)

    def vec(name):
        return full[name][i].reshape(1, -1)

    cw = jnp.moveaxis(gathered["conv_w"], 0, 1).reshape(4, BRANCH)
    w_g2 = jnp.moveaxis(gathered["gla_w_g2"], 0, 1).reshape(GLA_LOWRANK, GLA_QK)
    w["lru"] = dict(cw0=cw[0:1], cw1=cw[1:2], cw2=cw[2:3], cw3=cw[3:4], conv_b=vec("conv_b"),
                    wa=dense(full["lru_wa"][i]), wx=dense(full["lru_wx"][i]), ba=vec("lru_ba"), bx=vec("lru_bx"),
                    lam=vec("lru_lambda"))
    hq = jnp.arange(GLA_QK) // GLA_DK
    hv = jnp.arange(GLA_V) // GLA_DV
    w["gla"] = dict(wg2=jnp.pad(w_g2, ((0, LANES - GLA_LOWRANK), (0, 0))).astype(BF16),
                    bg=vec("gla_b_g"), ng=vec("gla_norm_g"), bd=(hv[:, None] == hq[None, :]).astype(F32))
    w["bfp"] = jnp.pad(vec("fox_b_f"), ((0, 0), (0, LANES - FOX_HEADS)))
    for name in ("ln1_g", "ln1_b", "ln2_g", "ln2_b", "ln3_g", "ln3_b", "ln4_g", "ln4_b", "ple_b_gate"):
        w[name] = vec(name)
    return w


def _layer_fwd(x0, p_i, w, S, side=None, on_side=None, first_side=None, on_first=None):
    x1, s1, first_out = _ffn_fwd("ffn1", x0, *w["ffn1"], w["ln1_g"], w["ln1_b"], S, first_side)
    if on_first is not None:
        on_first(first_out)
    x2, s2, side_out = _mixer_fwd(x1, w, S, side, on_side)
    x3, s3, _ = _ffn_fwd("ffn2", x2, *w["ffn2"], w["ln3_g"], w["ln3_b"], S)
    x4, s4 = _ple_fwd(x3, p_i, w, S)
    return x4, (s1, s2, s3, s4), side_out


def _layer_bwd(dx4, saved, w, S, make_side=None, gdt_a=F32, gdt_b=F32, make_last=None):
    s1, s2, s3, s4 = saved
    dx3, g4 = _ple_bwd(dx4, s4, w, S, gdt_b)
    dx2, g3, _ = _ffn_bwd("ffn2", dx3, s3, *w["ffn2"], w["ln3_g"], S, gdt_b)
    late = dict(ffn2_w_upT=g3["w_upT"], ffn2_w_down=g3["w_down"], ple_w_projT=g4["ple_w_projT"],
                ple_w_gate=g4["ple_w_gate"])
    mixer_side = None if make_side is None else (lambda mix: make_side({**late, **mix}))
    dx1, g2, side_out = _mixer_bwd(dx2, s2, w, S, mixer_side, gdt_a, gdt_b)
    last = None if make_last is None else (
        lambda g: make_last({**g2, "ffn1_w_upT": g["w_upT"], "ffn1_w_down": g["w_down"]}))
    dx0, g1, last_out = _ffn_bwd("ffn1", dx1, s1, *w["ffn1"], w["ln1_g"], S, gdt_a, last)
    grads = dict(g2)
    grads.update(g4)
    grads.update(late)
    grads.update(ffn1_w_upT=g1["w_upT"], ffn1_w_down=g1["w_down"], ln1_g=g1["ln_g"], ln1_b=g1["ln_b"],
                 ln3_g=g3["ln_g"], ln3_b=g3["ln_b"])
    return dx0, grads, side_out, last_out


def _travel_grads(grads, names):
    return [_dest_pieces(n, grads[n + "T" if n in COLUMN_SHARDED else n]) for n in names]


def _local_step(x, p, target, gathered0, rest, full, overlap):
    S = x.shape[0]
    names = [n for n, _ in SHARDED]
    early = [n for n in names if n in EARLY_WEIGHTS]
    late = [n for n in names if n not in EARLY_WEIGHTS]
    w0 = _layer_weights(gathered0, full, 0)
    if not overlap:
        h, saved0, _ = _layer_fwd(x, p[0], w0, S)
        w1 = _layer_weights(rest, full, 1)
        h, saved1, _ = _layer_fwd(h, p[1], w1, S)
    else:
        w_in0, late0, early1, late1 = rest
        h, saved0, got = _layer_fwd(x, p[0], w0, S, _gather_job(list(late0) + list(early1)),
                                    lambda got: w0.update(_late_weights(dict(zip(late, got[:len(late)])))),
                                    _gather_job([w_in0]), lambda got: w0.update(w_inT_p=_w_in_operand(got[0])))
        w1 = _layer_weights(dict(zip(early, got[len(late):])), full, 1)
        h, saved1, _ = _layer_fwd(h, p[1], w1, S, _gather_job(list(late1)),
                                  lambda got: w1.update(_late_weights(dict(zip(late, got)))))

    def loss_fn(y, t):
        err = y - t
        return [err * (1.0 / D_MODEL)], [_colsum(err * err) * (0.5 / D_MODEL)]

    (dy,), (lsum,) = _rowwise("loss", loss_fn, [h, target], [], [(D_MODEL, F32)], [D_MODEL], S)
    loss = jnp.sum(lsum)
    if not overlap:
        dy, g1, _, _ = _layer_bwd(dy, saved1, w1, S)
        dy, g0, _, _ = _layer_bwd(dy, saved0, w0, S)
        return loss, dy, [g0, g1], {}

    def last_exchange(g):
        dest = _travel_grads(g, early)
        got = _sibling_swap_multi("grad_sibling_swap", dest)
        core = lax.axis_index("c").astype(jnp.int32).reshape(1)
        return _chip_job([_pair_add("grad_pair_add_" + n, core, _as_rows(d, 2), _as_rows(a, 1))
                          for n, d, a in zip(early, dest, got)])

    dy, g1, late1_pieces, _ = _layer_bwd(dy, saved1, w1, S, lambda g: _scatter_job(_travel_grads(g, late)),
                                         BF16, BF16)
    dy, g0, pieces, early0_pieces = _layer_bwd(
        dy, saved0, w0, S, lambda g: _scatter_job(_travel_grads(g1, early) + _travel_grads(g, late)), F32, BF16,
        last_exchange)
    exchanged = {(1, n): a for n, a in zip(late, late1_pieces)}
    exchanged.update({(1, n): a for n, a in zip(early, pieces[:len(early)])})
    exchanged.update({(0, n): a for n, a in zip(late, pieces[len(early):])})
    exchanged.update({(0, n): a for n, a in zip(early, early0_pieces)})
    return loss, dy, [g0, g1], exchanged


def kernel(x, p, ffn1_w_up, ffn1_w_down, ln1_g, ln1_b, w_in, conv_w, conv_b, lru_wa, lru_ba, lru_wx, lru_bx, lru_lambda, gla_w_g2, gla_b_g, gla_norm_g, fox_b_f, w_branch, w_out, ln2_g, ln2_b, ffn2_w_up, ffn2_w_down, ln3_g, ln3_b, ple_w_proj, ple_w_gate, ple_b_gate, ln4_g, ln4_b, loss_target, m_ffn1_w_up, m_ffn1_w_down, m_ln1_g, m_ln1_b, m_w_in, m_conv_w, m_conv_b, m_lru_wa, m_lru_ba, m_lru_wx, m_lru_bx, m_lru_lambda, m_gla_w_g2, m_gla_b_g, m_gla_norm_g, m_fox_b_f, m_w_branch, m_w_out, m_ln2_g, m_ln2_b, m_ffn2_w_up, m_ffn2_w_down, m_ln3_g, m_ln3_b, m_ple_w_proj, m_ple_w_gate, m_ple_b_gate, m_ln4_g, m_ln4_b, v_ffn1_w_up, v_ffn1_w_down, v_ln1_g, v_ln1_b, v_w_in, v_conv_w, v_conv_b, v_lru_wa, v_lru_ba, v_lru_wx, v_lru_bx, v_lru_lambda, v_gla_w_g2, v_gla_b_g, v_gla_norm_g, v_fox_b_f, v_w_branch, v_w_out, v_ln2_g, v_ln2_b, v_ffn2_w_up, v_ffn2_w_down, v_ln3_g, v_ln3_b, v_ple_w_proj, v_ple_w_gate, v_ple_b_gate, v_ln4_g, v_ln4_b):
    env = dict(locals())
    wts = {n: env[n] for n in WEIGHTS}
    ms = {n: env["m_" + n] for n in WEIGHTS}
    vs = {n: env["v_" + n] for n in WEIGHTS}
    sharded = [n for n, _ in SHARDED]

    def travel(n, a):
        return jnp.swapaxes(a, -1, -2) if n in COLUMN_SHARDED else a

    shards = {(i, n): travel(n, wts[n][i]) if n in SHARDED_F32_GATHER else travel(n, wts[n][i]).astype(BF16)
              for i in range(DEPTH) for n in sharded}
    early = [n for n in sharded if n in EARLY_WEIGHTS]
    late = [n for n in sharded if n not in EARLY_WEIGHTS]
    first = [n for n in early if n != "w_in"]
    gathered0 = dict(zip(first, _allgather_multi("gather_weights", [shards[0, n] for n in first])))
    rest = (shards[0, "w_in"], [shards[0, n] for n in late], [shards[1, n] for n in early],
            [shards[1, n] for n in late])
    full = {n: wts[n] for n in REPLICATED}

    loss_part, grad_x, layer_grads, pieces = _local_step(x[0], p[:, 0], loss_target[0], gathered0, rest, full, True)
    loss = lax.psum(loss_part, MESH_AXES)

    rep = list(REPLICATED)
    rep_grads = [jnp.stack([layer_grads[i][n] for i in range(DEPTH)]).reshape(wts[n].shape) for n in rep]
    (gr,) = _allgather_multi("grad_gather_replicated", [_pack(rep_grads, F32)])

    kinds = ("grad", "delta", "new_m", "new_v")
    out = {}
    for n in sharded:
        local = [_as_rows(travel(n, pieces[i, n].reshape((-1,) + shards[i, n].shape)), 1) for i in range(DEPTH)]
        res = _adamw("adamw_" + n, local, _as_rows(wts[n], 1), _as_rows(ms[n], 1), _as_rows(vs[n], 1))
        for kind, arr in zip(kinds, res):
            out[kind + "_" + n] = arr.reshape(wts[n].shape)
    res = _adamw("adamw_replicated", [gr], _pack([wts[n] for n in rep], F32)[None],
                 _pack([ms[n] for n in rep], F32)[None], _pack([vs[n] for n in rep], F32)[None])
    shapes = [wts[n].shape for n in rep]
    for kind, buf in zip(kinds, res):
        for n, arr in zip(rep, _unpack(buf[0], shapes)):
            out[kind + "_" + n] = arr
    return (loss, grad_x[None], *[out["grad_" + n] for n in WEIGHTS], *[out["delta_" + n] for n in WEIGHTS],
            *[out["new_m_" + n] for n in WEIGHTS], *[out["new_v_" + n] for n in WEIGHTS])
```

```python
import functools
import math

import jax
import jax.numpy as jnp
from jax import lax
from jax.experimental import pallas as pl
from jax.experimental.pallas import tpu as pltpu

F32 = jnp.float32
BF16 = jnp.bfloat16

N_DEV = 8
MESH_AXES = ("x", "y", "c")
DEPTH = 2
D_MODEL = 1024
D_FF = 2816
BRANCH = 512
CHUNK = 64
GLA_HEADS = 4
GLA_DK = 64
GLA_DV = 128
GLA_LOWRANK = 16
GLA_TAU = 16.0
FOX_HEADS = 8
FOX_DH = 64
PLE_DIM = 256
LRU_C = 8.0
LRU_BLOCKS = 8
LN_EPS = 1e-5
RMS_EPS = 1e-6
ALPHA = (2 * DEPTH) ** 0.25
LANES = 128
NEG_BIG = -1e30

ADAM_LR = 0.001
ADAM_B1 = 0.9
ADAM_B2 = 0.999
ADAM_EPS = 1e-08
ADAM_WD = 0.01
ADAM_STEP = 10

VMEM_LIMIT_BYTES = 56 * 1024 * 1024

U_GATES = 0
U_AX = 3072
U_AY = 3584
U_BQ = 4096
U_BK = 4352
U_BV = 4608
U_BR = 5120
U_CQ = 5632
U_CK = 6144
U_CV = 6656
U_BLOW = 7168
U_CF = 7296
U_WIDTH = 7680
W_IN_SEGMENTS = (
    (0, 512, U_AX), (512, 512, U_AY), (1024, 256, U_BQ), (1280, 256, U_BK), (1536, 512, U_BV),
    (2048, 16, U_BLOW), (2064, 512, U_BR), (2576, 512, U_CQ), (3088, 512, U_CK), (3600, 512, U_CV),
    (4112, 8, U_CF), (4120, 3072, U_GATES),
)

SHARDED = (
    ("ffn1_w_up", 2), ("ffn1_w_down", 1), ("w_in", 2), ("conv_w", 2), ("gla_w_g2", 2), ("w_branch", 3),
    ("w_out", 1), ("ffn2_w_up", 2), ("ffn2_w_down", 1), ("ple_w_proj", 2), ("ple_w_gate", 1),
)
SHARDED_F32_GATHER = ("conv_w", "gla_w_g2")
COLUMN_SHARDED = ("ffn1_w_up", "ffn2_w_up", "w_in", "w_branch", "ple_w_proj")
REPLICATED = ("ln1_g", "ln1_b", "conv_b", "lru_wa", "lru_ba", "lru_wx", "lru_bx", "lru_lambda", "gla_b_g",
              "gla_norm_g", "fox_b_f", "ln2_g", "ln2_b", "ln3_g", "ln3_b", "ple_b_gate", "ln4_g", "ln4_b")
WEIGHTS = ("ffn1_w_up", "ffn1_w_down", "ln1_g", "ln1_b", "w_in", "conv_w", "conv_b", "lru_wa", "lru_ba", "lru_wx",
           "lru_bx", "lru_lambda", "gla_w_g2", "gla_b_g", "gla_norm_g", "fox_b_f", "w_branch", "w_out", "ln2_g",
           "ln2_b", "ffn2_w_up", "ffn2_w_down", "ln3_g", "ln3_b", "ple_w_proj", "ple_w_gate", "ple_b_gate", "ln4_g",
           "ln4_b")


def _sigmoid(x):
    return 1.0 / (1.0 + jnp.exp(-x))


def _log1p_pos(e):
    return jnp.where(e < 1e-4, e * (1.0 - 0.5 * e), jnp.log(1.0 + e))


def _softplus(x):
    return jnp.maximum(x, 0.0) + _log1p_pos(jnp.exp(-jnp.abs(x)))


def _log_sigmoid(x):
    return -_softplus(-x)


def _neg_expm1(y):
    series = -y * (1.0 + y * (0.5 + y * (1.0 / 6.0 + y * (1.0 / 24.0 + y * (1.0 / 120.0)))))
    return jnp.where(y > -0.1, series, 1.0 - jnp.exp(y))


def _silu_and_grad(x):
    s = _sigmoid(x)
    return x * s, s * (1.0 + x * (1.0 - s))


_GELU_C = math.sqrt(2.0 / math.pi)


def _gelu_and_grad(x):
    inner = _GELU_C * (x + 0.044715 * x * x * x)
    t = jnp.tanh(inner)
    g = 0.5 * x * (1.0 + t)
    dg = 0.5 * (1.0 + t) + 0.5 * x * (1.0 - t * t) * _GELU_C * (1.0 + 3.0 * 0.044715 * x * x)
    return g, dg


def _ln_stats(z):
    mu = jnp.mean(z, axis=-1, keepdims=True)
    zc = z - mu
    var = jnp.mean(zc * zc, axis=-1, keepdims=True)
    rstd = lax.rsqrt(var + LN_EPS)
    return zc * rstd, rstd


def _ln_fwd(z, g, b):
    xhat, _ = _ln_stats(z)
    return xhat * g + b


def _ln_bwd(dy, z, g):
    xhat, rstd = _ln_stats(z)
    dxh = dy * g
    m1 = jnp.mean(dxh, axis=-1, keepdims=True)
    m2 = jnp.mean(dxh * xhat, axis=-1, keepdims=True)
    return rstd * (dxh - m1 - xhat * m2), xhat


def _colsum(x):
    return jnp.sum(x, axis=0, keepdims=True)


def _dot(a, b, dims):
    dn = {"nn": (((1,), (0,)), ((), ())), "nt": (((1,), (1,)), ((), ())), "tn": (((0,), (0,)), ((), ()))}[dims]
    return lax.dot_general(a.astype(BF16), b.astype(BF16), dn, preferred_element_type=F32)


def _scan_rows(a, b, length, reverse=False, seg=None):
    rows = lax.broadcasted_iota(jnp.int32, b.shape, 0)
    span = seg if seg else length
    pos = rows % span if seg else rows
    d = 1
    while d < span:
        shift = (length - d) if reverse else d
        valid = (pos < span - d) if reverse else (pos >= d)
        sb = jnp.where(valid, pltpu.roll(b, shift, 0), 0.0)
        if a is None:
            b = b + sb
        else:
            b = b + a * sb
            a = a * jnp.where(valid, pltpu.roll(a, shift, 0), 1.0)
        d *= 2
    return a, b


def _tile(dim, pref):
    if dim <= pref:
        return dim
    best = None
    t = LANES
    while t <= pref:
        if dim % t == 0:
            best = t
        t += LANES
    assert best is not None, (dim, pref)
    return best


def _full_spec(arr):
    nd = arr.ndim
    return pl.BlockSpec(arr.shape, lambda *_: (0,) * nd)


def _mm(name, dims, a_ops, b_ops, terms, n_acc, epilogue, extras, out_dtypes, M, N, K, tm=512, tn=1024, tk=1024,
        side=None):
    tm, tn, tk = _tile(M, tm), _tile(N, tn), _tile(K, tk)
    gm, gn, gk = M // tm, N // tn, K // tk
    a_bytes = sum(a.size * a.dtype.itemsize for a in a_ops)
    b_bytes = sum(b.size * b.dtype.itemsize for b in b_ops)
    n_outer = gk == 1 and b_bytes + a_bytes * gn < a_bytes + b_bytes * gm

    def spec(shape, fn):
        if n_outer:
            return pl.BlockSpec(shape, lambda j, i, k: fn(i, j, k))
        return pl.BlockSpec(shape, fn)

    if dims == "tn":
        a_spec = spec((tk, tm), lambda i, j, k: (k, i))
    else:
        a_spec = spec((tm, tk), lambda i, j, k: (i, k))
    if dims == "nt":
        b_spec = spec((tn, tk), lambda i, j, k: (j, k))
    else:
        b_spec = spec((tk, tn), lambda i, j, k: (k, j))
    e_specs, e_arrays = [], []
    for ex in extras:
        if ex[1] == "mn":
            e_specs.append(spec((tm, tn), functools.partial(lambda i, j, k, off: (i, j + off), off=ex[2])))
        else:
            e_specs.append(spec((1, tn), lambda i, j, k: (0, j)))
        e_arrays.append(ex[0])
    na, nb, ne, no = len(a_ops), len(b_ops), len(extras), len(out_dtypes)
    grid = (gn, gm, gk) if n_outer else (gm, gn, gk)
    s_in, s_out, s_sems = _side_specs(side) if side is not None else ([], [], [])
    nsi, nso = len(s_in), len(s_out)
    total = grid[0] * grid[1] * grid[2]

    def body(*refs):
        a_refs = refs[:na]
        b_refs = refs[na:na + nb]
        e_refs = refs[na + nb:na + nb + ne]
        pos = na + nb + ne
        s_ins, o_refs = refs[pos:pos + nsi], refs[pos + nsi:pos + nsi + no]
        pos += nsi + no
        s_outs, acc_refs, sems = refs[pos:pos + nso], refs[pos + nso:pos + nso + n_acc], refs[pos + nso + n_acc:]
        k = pl.program_id(2)
        if side is not None:
            phases = side["phases"]
            triggers = [0, total - 1] if len(phases) == 2 else [0, total * 7 // 10, total - 1]
            flat = (pl.program_id(0) * grid[1] + pl.program_id(1)) * grid[2] + k
            for trigger, phase in zip(triggers[:-1], phases[:-1]):
                @pl.when(flat == trigger)
                def _(phase=phase):
                    phase(s_ins, s_outs, *sems)

        @pl.when(k == 0)
        def _():
            for acc in acc_refs:
                acc[...] = jnp.zeros_like(acc)

        for r, ai, bi in terms:
            acc_refs[r][...] += _dot(a_refs[ai][...], b_refs[bi][...], dims)

        @pl.when(k == gk - 1)
        def _():
            res = epilogue([acc[...] for acc in acc_refs], *[e[...] for e in e_refs])
            for o, val in zip(o_refs, res):
                o[...] = val.astype(o.dtype)

        if side is not None:
            @pl.when(flat == triggers[-1])
            def _():
                phases[-1](s_ins, s_outs, *sems)

    outs = pl.pallas_call(
        body,
        name=name,
        grid=grid,
        in_specs=[a_spec] * na + [b_spec] * nb + e_specs + s_in,
        out_specs=[spec((tm, tn), lambda i, j, k: (i, j))] * no + s_out,
        out_shape=[jax.ShapeDtypeStruct((M, N), dt) for dt in out_dtypes] + (side["out_shapes"] if side else []),
        scratch_shapes=[pltpu.VMEM((tm, tn), F32)] * n_acc + s_sems,
        compiler_params=pltpu.CompilerParams(
            dimension_semantics=("arbitrary",) * 3 if side is not None else ("parallel", "parallel", "arbitrary"),
            vmem_limit_bytes=VMEM_LIMIT_BYTES),
    )(*a_ops, *b_ops, *e_arrays, *(side["arrs"] if side else []))
    return outs if side is None else (outs[:no], outs[no:])


def _mm1(name, dims, a, b, M, N, K, out_dtype=F32, scale=None, **kw):
    def epi(accs):
        return [accs[0] if scale is None else accs[0] * scale]
    return _mm(name, dims, [a], [b], [(0, 0, 0)], 1, epi, [], [out_dtype], M, N, K, **kw)[0]


def _rowwise(name, fn, row_ins, vec_ins, row_outs, sum_outs, S, tr=512, reverse=False):
    tr = min(tr, S)
    g = S // tr
    rmap = (lambda i: (g - 1 - i)) if reverse else (lambda i: i)
    in_specs, arrays = [], []
    for r in row_ins:
        if isinstance(r, tuple):
            arr, width, blk = r
            in_specs.append(pl.BlockSpec((tr, width), functools.partial(lambda i, blk: (rmap(i), blk), blk=blk)))
        else:
            arr = r
            in_specs.append(pl.BlockSpec((tr, arr.shape[1]), lambda i: (rmap(i), 0)))
        arrays.append(arr)
    for v in vec_ins:
        in_specs.append(_full_spec(v))
        arrays.append(v)
    nr, nv, no, ns = len(row_ins), len(vec_ins), len(row_outs), len(sum_outs)

    def body(*refs):
        ins = [r[...] for r in refs[:nr + nv]]
        o_refs = refs[nr + nv:nr + nv + no]
        s_refs = refs[nr + nv + no:]
        outs, sums = fn(*ins)
        for o, val in zip(o_refs, outs):
            o[...] = val.astype(o.dtype)
        if ns:
            i = pl.program_id(0)

            @pl.when(i == 0)
            def _():
                for s, val in zip(s_refs, sums):
                    s[...] = val

            @pl.when(i > 0)
            def _():
                for s, val in zip(s_refs, sums):
                    s[...] += val

    res = pl.pallas_call(
        body,
        name=name,
        grid=(g,),
        in_specs=in_specs,
        out_specs=[pl.BlockSpec((tr, c), lambda i: (rmap(i), 0)) for c, _ in row_outs]
        + [pl.BlockSpec((1, c), lambda i: (0, 0)) for c in sum_outs],
        out_shape=[jax.ShapeDtypeStruct((S, c), dt) for c, dt in row_outs]
        + [jax.ShapeDtypeStruct((1, c), F32) for c in sum_outs],
        compiler_params=pltpu.CompilerParams(
            dimension_semantics=("arbitrary",), vmem_limit_bytes=VMEM_LIMIT_BYTES),
    )(*arrays)
    return res[:no], res[no:]


MESH_ID = pl.DeviceIdType.MESH


def _remote(src, dst, send_sem, recv_sem, to):
    return pltpu.make_async_remote_copy(src_ref=src, dst_ref=dst, send_sem=send_sem, recv_sem=recv_sem,
                                        device_id=to, device_id_type=MESH_ID)


def _hbm_call(name, body, arrs, out_shapes, n_send, n_recv, n_local):
    return pl.pallas_call(
        body,
        name=name,
        in_specs=[pl.BlockSpec(memory_space=pltpu.HBM)] * len(arrs),
        out_specs=[pl.BlockSpec(memory_space=pltpu.HBM)] * len(out_shapes),
        out_shape=out_shapes,
        scratch_shapes=[pltpu.SemaphoreType.DMA((n_send,)), pltpu.SemaphoreType.DMA((n_recv,)),
                        pltpu.SemaphoreType.DMA((n_local,))],
        compiler_params=pltpu.CompilerParams(has_side_effects=True),
    )(*arrs)


def _side_job(arrs, out_shapes, n_send, n_recv, n_local, phases):
    return dict(arrs=list(arrs), out_shapes=list(out_shapes), sems=(n_send, n_recv, n_local), phases=phases)


def _side_specs(side):
    hbm = pl.BlockSpec(memory_space=pltpu.HBM)
    sems = [pltpu.SemaphoreType.DMA((k,)) for k in side["sems"]]
    return [hbm] * len(side["arrs"]), [hbm] * len(side["out_shapes"]), sems


def _gather_job(arrs):
    n = len(arrs)

    def plan(ins, outs, send_sems, recv_sems, local_sems):
        x, y, c = lax.axis_index("x"), lax.axis_index("y"), lax.axis_index("c")
        me, sibling = (x, y, c), (x, y, 1 - c)
        chips = [(1 - x, y), (x, 1 - y), (1 - x, 1 - y)]

        def slot(i, dev):
            return outs[i].at[4 * dev[0] + 2 * dev[1] + dev[2]]

        def copy(i, k, block, to, src=None):
            dst = slot(i, block)
            return _remote(dst if src is None else src, dst, send_sems.at[7 * i + k], recv_sems.at[7 * i + k], to)

        mine = [pltpu.make_async_copy(ins[i], slot(i, me), local_sems.at[i]) for i in range(n)]
        first = []
        for i in range(n):
            first.append(copy(i, 0, me, sibling, src=ins[i]))
            first += [copy(i, 1 + j, me, (*chip, c), src=ins[i]) for j, chip in enumerate(chips)]
        arrive = [[copy(i, 1 + j, (*chip, c), me) for i in range(n)] for j, chip in enumerate(chips)]
        passed = [[copy(i, 4 + j, (*chip, c), sibling) for i in range(n)] for j, chip in enumerate(chips)]
        last = [copy(i, 0, sibling, me) for i in range(n)]
        last += [copy(i, 4 + j, (*chip, 1 - c), me) for i in range(n) for j, chip in enumerate(chips)]
        return mine, first, arrive, passed, last

    def start(*refs):
        mine, first, _, _, _ = plan(*refs)
        for cp in mine + first:
            cp.start()

    def forward(*refs):
        _, _, arrive, passed, _ = plan(*refs)
        for came, onward in zip(arrive, passed):
            for a, p in zip(came, onward):
                a.wait_recv()
                p.start()

    def finish(*refs):
        mine, first, _, passed, last = plan(*refs)
        for cp in last:
            cp.wait_recv()
        for cp in first + [p for onward in passed for p in onward]:
            cp.wait_send()
        for cp in mine:
            cp.wait()

    outs = [jax.ShapeDtypeStruct((N_DEV,) + a.shape, a.dtype) for a in arrs]
    return _side_job(arrs, outs, 7 * n, 7 * n, n, [start, forward, finish])


def _scatter_job(arrs):
    n = len(arrs)

    def plan(ins, outs, send_sems, recv_sems, local_sems):
        x, y, c = lax.axis_index("x"), lax.axis_index("y"), lax.axis_index("c")
        here = 2 * x + y
        local = [pltpu.make_async_copy(ins[i].at[here, c], outs[i].at[here, c], local_sems.at[i]) for i in range(n)]
        sends, recvs = [], []
        for i in range(n):
            for k in range(1, N_DEV):
                px = 1 - x if k & 4 else x
                py = 1 - y if k & 2 else y
                pc = 1 - c if k & 1 else c
                sems = (send_sems.at[7 * i + k - 1], recv_sems.at[7 * i + k - 1], (px, py, pc))
                sends.append(_remote(ins[i].at[2 * px + py, pc], outs[i].at[here, c], *sems))
                recvs.append(_remote(ins[i].at[2 * px + py, pc], outs[i].at[2 * px + py, pc], *sems))
        return local, sends, recvs

    def start(*refs):
        local, sends, _ = plan(*refs)
        for cp in local + sends:
            cp.start()

    def finish(*refs):
        local, sends, recvs = plan(*refs)
        for cp in recvs:
            cp.wait_recv()
        for cp in sends:
            cp.wait_send()
        for cp in local:
            cp.wait()

    outs = [jax.ShapeDtypeStruct(a.shape, a.dtype) for a in arrs]
    return _side_job(arrs, outs, 7 * n, 7 * n, n, [start, finish])


def _direct_gather_job(arrs):
    n = len(arrs)

    def plan(ins, outs, send_sems, recv_sems, local_sems):
        x, y, c = lax.axis_index("x"), lax.axis_index("y"), lax.axis_index("c")
        me = 4 * x + 2 * y + c
        local = [pltpu.make_async_copy(ins[i], outs[i].at[me], local_sems.at[i]) for i in range(n)]
        sends, recvs = [], []
        for i in range(n):
            for k in range(1, N_DEV):
                px = 1 - x if k & 4 else x
                py = 1 - y if k & 2 else y
                pc = 1 - c if k & 1 else c
                sems = (send_sems.at[7 * i + k - 1], recv_sems.at[7 * i + k - 1], (px, py, pc))
                sends.append(_remote(ins[i], outs[i].at[me], *sems))
                recvs.append(_remote(ins[i], outs[i].at[4 * px + 2 * py + pc], *sems))
        return local, sends, recvs

    def start(*refs):
        local, sends, _ = plan(*refs)
        for cp in local + sends:
            cp.start()

    def finish(*refs):
        local, sends, recvs = plan(*refs)
        for cp in recvs:
            cp.wait_recv()
        for cp in sends:
            cp.wait_send()
        for cp in local:
            cp.wait()

    outs = [jax.ShapeDtypeStruct((N_DEV,) + a.shape, a.dtype) for a in arrs]
    return _side_job(arrs, outs, 7 * n, 7 * n, n, [start, finish])


def _run_job(name, job):
    na, no = len(job["arrs"]), len(job["out_shapes"])

    def body(*refs):
        ins, outs, sems = refs[:na], refs[na:na + no], refs[na + no:]
        for phase in job["phases"]:
            phase(ins, outs, *sems)

    return _hbm_call(name, body, job["arrs"], job["out_shapes"], *job["sems"])


def _allgather_multi(name, arrs):
    return _run_job(name, _gather_job(arrs))


def _sibling_swap_multi(name, arrs):
    n = len(arrs)
    per = 4

    def body(*refs):
        ins, got = refs[:n], refs[n:2 * n]
        send_sems, recv_sems, _ = refs[2 * n:]
        x, y, c = lax.axis_index("x"), lax.axis_index("y"), lax.axis_index("c")
        sibling = (x, y, 1 - c)
        sends = []
        for i in range(n):
            for a in range(4):
                k = per * i + a
                sends.append(_remote(ins[i].at[a, 1 - c], got[i].at[a], send_sems.at[k], recv_sems.at[k], sibling))
        for cp in sends:
            cp.start()
        for cp in sends:
            cp.wait_recv()
        for cp in sends:
            cp.wait_send()

    outs = [jax.ShapeDtypeStruct((4,) + a.shape[2:], a.dtype) for a in arrs]
    return _hbm_call(name, body, arrs, outs, per * n, per * n, 1)


def _chip_job(arrs):
    n = len(arrs)

    def plan(ins, outs, send_sems, recv_sems, local_sems):
        x, y, c = lax.axis_index("x"), lax.axis_index("y"), lax.axis_index("c")
        mine = 2 * x + y
        chips = [(1 - x, y), (x, 1 - y), (1 - x, 1 - y)]
        local = [pltpu.make_async_copy(ins[i].at[mine], outs[i].at[mine], local_sems.at[i]) for i in range(n)]
        sends, recvs = [], []
        for i in range(n):
            for j, (px, py) in enumerate(chips):
                peer = 2 * px + py
                sems = (send_sems.at[3 * i + j], recv_sems.at[3 * i + j], (px, py, c))
                sends.append(_remote(ins[i].at[peer], outs[i].at[mine], *sems))
                recvs.append(_remote(ins[i].at[peer], outs[i].at[peer], *sems))
        return local, sends, recvs

    def start(*refs):
        local, sends, _ = plan(*refs)
        for cp in local + sends:
            cp.start()

    def finish(*refs):
        local, sends, recvs = plan(*refs)
        for cp in recvs:
            cp.wait_recv()
        for cp in sends:
            cp.wait_send()
        for cp in local:
            cp.wait()

    outs = [jax.ShapeDtypeStruct(a.shape, a.dtype) for a in arrs]
    return _side_job(arrs, outs, 3 * n, 3 * n, n, [start, finish])


def _as_rows(a, lead):
    return a.reshape(a.shape[:lead] + (-1, a.shape[-1]))


def _row_tile(rows, cols, parts):
    budget = 4 * 1024 * 1024 // (4 * max(cols, LANES) * parts)
    return _tile_rows(rows, max(8, min(512, budget // 8 * 8)))


def _pair_add(name, core, both, got):
    _, rows, cols = got.shape
    tr = _row_tile(rows, cols, 2)

    def body(c_ref, a_ref, b_ref, o_ref):
        o_ref[...] = (a_ref[...] + b_ref[...]).astype(o_ref.dtype)

    blk = pl.BlockSpec((1, tr, cols), lambda ch, i, c_ref: (ch, i, 0))
    return pl.pallas_call(
        body, name=name,
        grid_spec=pltpu.PrefetchScalarGridSpec(
            num_scalar_prefetch=1, grid=(4, rows // tr),
            in_specs=[pl.BlockSpec((1, None, tr, cols), lambda ch, i, c_ref: (ch, c_ref[0], i, 0)), blk],
            out_specs=blk),
        out_shape=jax.ShapeDtypeStruct(got.shape, BF16),
        compiler_params=pltpu.CompilerParams(dimension_semantics=("parallel", "parallel"),
                                             vmem_limit_bytes=VMEM_LIMIT_BYTES),
    )(core, both, got)


def _adamw(name, gparts, w, m, v):
    layers = len(gparts)
    _, rows, cols = gparts[0].shape
    tr = _row_tile(rows, cols, sum(gp.shape[0] for gp in gparts))
    c1 = 1.0 / (1.0 - ADAM_B1 ** ADAM_STEP)
    c2 = 1.0 / (1.0 - ADAM_B2 ** ADAM_STEP)

    def body(*refs):
        gp_refs = refs[:layers]
        w_ref, m_ref, v_ref, g_ref, d_ref, nm_ref, nv_ref = refs[layers:]
        layer = pl.program_id(0)
        g = None
        for k, gp_ref in enumerate(gp_refs):
            gk = gp_ref[0].astype(F32)
            for i in range(1, gp_ref.shape[0]):
                gk = gk + gp_ref[i].astype(F32)
            g = gk if g is None else jnp.where(layer == k, gk, g)
        nm = ADAM_B1 * m_ref[...] + (1.0 - ADAM_B1) * g
        nv = ADAM_B2 * v_ref[...] + (1.0 - ADAM_B2) * (g * g)
        m_hat = nm * c1
        v_hat = nv * c2
        g_ref[...] = g
        nm_ref[...] = nm
        nv_ref[...] = nv
        d_ref[...] = -ADAM_LR * (m_hat / (jnp.sqrt(v_hat) + ADAM_EPS) + ADAM_WD * w_ref[...])

    row = pl.BlockSpec((None, tr, cols), lambda l, i: (l, i, 0))
    return pl.pallas_call(
        body,
        name=name,
        grid=(layers, rows // tr),
        in_specs=[pl.BlockSpec((gp.shape[0], tr, cols), lambda l, i: (0, i, 0)) for gp in gparts] + [row, row, row],
        out_specs=[row] * 4,
        out_shape=[jax.ShapeDtypeStruct((layers, rows, cols), F32)] * 4,
        compiler_params=pltpu.CompilerParams(dimension_semantics=("parallel", "parallel"),
                                             vmem_limit_bytes=VMEM_LIMIT_BYTES),
    )(*gparts, w, m, v)


def _tile_rows(rows, pref):
    t = min(pref, rows) // 8 * 8
    while t >= 8 and rows % t:
        t -= 8
    return t if t >= 8 else rows


PACK_ROWS = 512


def _pack(arrs, dtype):
    flat = jnp.concatenate([a.astype(dtype).reshape(-1) for a in arrs])
    quantum = PACK_ROWS * LANES
    padded = -(-flat.shape[0] // quantum) * quantum
    return jnp.pad(flat, (0, padded - flat.shape[0])).reshape(-1, LANES)


def _unpack(buf, shapes, lead=()):
    flat = buf.reshape(lead + (-1,))
    out, off = [], 0
    for shp in shapes:
        n = math.prod(shp)
        out.append(flat[..., off:off + n].reshape(lead + tuple(shp)))
        off += n
    return out


def _dest_pieces(name, g):
    if name == "w_branch":
        return jnp.moveaxis(g.reshape(3, 4, 2, D_MODEL // N_DEV, BRANCH), 0, 2)
    if name in SHARDED_F32_GATHER:
        return jnp.moveaxis(g.reshape(g.shape[0], 4, 2, -1), 0, 2)
    return g.reshape((4, 2, g.shape[0] // N_DEV) + g.shape[1:])


HALO = 8
LRU_TILE = 256
CONV_TILE = 512


def _rows_down(x, prev, k):
    xs = pltpu.roll(x, k, 0)
    row = lax.broadcasted_iota(jnp.int32, prev.shape, 0)
    top = jnp.where(row < k, pltpu.roll(prev, k, 0), xs[:HALO])
    return jnp.concatenate([top, xs[HALO:]], axis=0)


def _rows_up(x, nxt, k):
    rows = x.shape[0]
    xs = pltpu.roll(x, rows - k, 0)
    row = lax.broadcasted_iota(jnp.int32, nxt.shape, 0)
    bottom = jnp.where(row >= HALO - k, pltpu.roll(nxt, HALO - k, 0), xs[rows - HALO:])
    return jnp.concatenate([xs[:rows - HALO], bottom], axis=0)


def _halo_before(T, block_of, col=0):
    per = T // HALO
    return pl.BlockSpec((HALO, BRANCH), lambda t: (jnp.maximum(block_of(t) * per - 1, 0), col))


def _halo_after(T, block_of, S, col=0):
    per = T // HALO
    return pl.BlockSpec((HALO, BRANCH), lambda t: (jnp.minimum((block_of(t) + 1) * per, S // HALO - 1), col))


def _lru_fwd(u, lw, S):
    T = min(LRU_TILE, S)
    nb = S // T
    row = pl.BlockSpec((T, BRANCH), lambda t: (t, 0))
    vecs = [lw["cw0"], lw["cw1"], lw["cw2"], lw["cw3"], lw["conv_b"], lw["wa"], lw["wx"], lw["ba"], lw["bx"],
            lw["lam"]]

    def body(ax, ax_before, ay, cw0, cw1, cw2, cw3, cb, wa, wx, ba, bx, lam, xc_o, r_o, i_o, a_o, h_o, ya_o, hc):
        t = pl.program_id(0)

        @pl.when(t == 0)
        def _():
            hc[...] = jnp.zeros_like(hc)

        x = ax[...]
        before = jnp.where(t == 0, 0.0, ax_before[...])
        xc = (cw3[...] * x + cw2[...] * _rows_down(x, before, 1) + cw1[...] * _rows_down(x, before, 2)
              + cw0[...] * _rows_down(x, before, 3) + cb[...])
        r = _sigmoid(_dot(xc, wa[...], "nn") + ba[...])
        gi = _sigmoid(_dot(xc, wx[...], "nn") + bx[...])
        sp = _softplus(-lam[...])
        la = -LRU_C * r * sp
        a = jnp.exp(la)
        mult = jnp.sqrt(_neg_expm1(2.0 * la))
        A, B = _scan_rows(a, mult * gi * xc, T)
        h = B + A * hc[...]
        h_o[...] = h
        hc[...] = h_o[pl.ds(T - 1, 1), :]
        xc_o[...] = xc
        r_o[...] = r
        i_o[...] = gi
        a_o[...] = a
        gy, _ = _gelu_and_grad(ay[...])
        ya_o[...] = (gy * h).astype(ya_o.dtype)

    outs = pl.pallas_call(
        body,
        name="lru_fwd",
        grid=(nb,),
        in_specs=[pl.BlockSpec((T, BRANCH), lambda t: (t, U_AX // BRANCH)),
                  _halo_before(T, lambda t: t, U_AX // BRANCH),
                  pl.BlockSpec((T, BRANCH), lambda t: (t, U_AY // BRANCH))] + [_full_spec(v) for v in vecs],
        out_specs=[row] * 6,
        out_shape=[jax.ShapeDtypeStruct((S, BRANCH), F32)] * 5 + [jax.ShapeDtypeStruct((S, BRANCH), BF16)],
        scratch_shapes=[pltpu.VMEM((1, BRANCH), F32)],
        compiler_params=pltpu.CompilerParams(dimension_semantics=("arbitrary",), vmem_limit_bytes=VMEM_LIMIT_BYTES),
    )(u, u, u, *vecs)
    return outs


def _lru_bwd(dya, u, sv, lw, S):
    T = min(LRU_TILE, S)
    nb = S // T
    rrow = pl.BlockSpec((T, BRANCH), lambda t: (nb - 1 - t, 0))
    sq = pl.BlockSpec((BRANCH, BRANCH), lambda t: (0, 0))
    vrow = pl.BlockSpec((1, BRANCH), lambda t: (0, 0))

    def block(t):
        return nb - 1 - t

    def body(dya_r, ay, h, h_before, xc_r, r_r, i_r, a_r, a_after, wa, wx, lam,
             day_o, dxc_o, dwa_o, dwx_o, dba_o, dbx_o, dlam_o, lcar, tmp):
        t = pl.program_id(0)
        h_prev = _rows_down(h[...], jnp.where(t == nb - 1, 0.0, h_before[...]), 1)
        a_next = _rows_up(a_r[...], jnp.where(t == 0, 0.0, a_after[...]), 1)

        @pl.when(t == 0)
        def _():
            lcar[...] = jnp.zeros_like(lcar)
            dwa_o[...] = jnp.zeros_like(dwa_o)
            dwx_o[...] = jnp.zeros_like(dwx_o)
            dba_o[...] = jnp.zeros_like(dba_o)
            dbx_o[...] = jnp.zeros_like(dbx_o)
            dlam_o[...] = jnp.zeros_like(dlam_o)

        gy, dgy = _gelu_and_grad(ay[...])
        dy = dya_r[...]
        day_o[...] = (dy * h[...] * dgy).astype(day_o.dtype)
        A, B = _scan_rows(a_next, dy * gy, T, reverse=True)
        lmb = B + A * lcar[...]
        tmp[...] = lmb
        lcar[...] = tmp[pl.ds(0, 1), :]
        xc, r, gi, a = xc_r[...], r_r[...], i_r[...], a_r[...]
        sp = _softplus(-lam[...])
        la = -LRU_C * r * sp
        mult = jnp.sqrt(_neg_expm1(2.0 * la))
        da = lmb * h_prev
        dmult = lmb * gi * xc
        di = lmb * mult * xc
        dxc = lmb * mult * gi
        dla = da * a - dmult * a * a / mult
        dr = dla * (-LRU_C * sp)
        dlam_o[...] += _colsum(dla * (LRU_C * r)) * _sigmoid(-lam[...])
        dpr = dr * r * (1.0 - r)
        dpi = di * gi * (1.0 - gi)
        dba_o[...] += _colsum(dpr)
        dbx_o[...] += _colsum(dpi)
        dxc_o[...] = dxc + _dot(dpr, wa[...], "nt") + _dot(dpi, wx[...], "nt")
        dwa_o[...] += _dot(xc, dpr, "tn")
        dwx_o[...] += _dot(xc, dpi, "tn")

    outs = pl.pallas_call(
        body,
        name="lru_bwd",
        grid=(nb,),
        in_specs=[rrow, pl.BlockSpec((T, BRANCH), lambda t: (nb - 1 - t, U_AY // BRANCH)), rrow,
                  _halo_before(T, block), rrow, rrow, rrow, rrow, _halo_after(T, block, S), sq, sq, vrow],
        out_specs=[rrow, rrow, sq, sq, vrow, vrow, vrow],
        out_shape=[jax.ShapeDtypeStruct((S, BRANCH), BF16), jax.ShapeDtypeStruct((S, BRANCH), F32),
                   jax.ShapeDtypeStruct((BRANCH, BRANCH), F32), jax.ShapeDtypeStruct((BRANCH, BRANCH), F32),
                   jax.ShapeDtypeStruct((1, BRANCH), F32), jax.ShapeDtypeStruct((1, BRANCH), F32),
                   jax.ShapeDtypeStruct((1, BRANCH), F32)],
        scratch_shapes=[pltpu.VMEM((1, BRANCH), F32), pltpu.VMEM((T, BRANCH), F32)],
        compiler_params=pltpu.CompilerParams(dimension_semantics=("arbitrary",), vmem_limit_bytes=VMEM_LIMIT_BYTES),
    )(dya, u, sv["h"], sv["h"], sv["xc"], sv["r"], sv["i"], sv["a"], sv["a"], lw["wa"], lw["wx"], lw["lam"])
    return outs


def _conv_bwd(dxc, u, lw, S):
    T = min(CONV_TILE, S)
    nb = S // T
    vecs = [lw["cw0"], lw["cw1"], lw["cw2"], lw["cw3"]]
    vrow = pl.BlockSpec((1, BRANCH), lambda t: (0, 0))

    def body(d_r, d_after, ax, ax_before, cw0, cw1, cw2, cw3, dax_o, dcw0_o, dcw1_o, dcw2_o, dcw3_o, dcb_o):
        t = pl.program_id(0)
        d = d_r[...]
        after = jnp.where(t == nb - 1, 0.0, d_after[...])
        x = ax[...]
        before = jnp.where(t == 0, 0.0, ax_before[...])
        dax = (cw3[...] * d + cw2[...] * _rows_up(d, after, 1) + cw1[...] * _rows_up(d, after, 2)
               + cw0[...] * _rows_up(d, after, 3))
        dax_o[...] = dax.astype(dax_o.dtype)
        sums = [_colsum(d * _rows_down(x, before, 3)), _colsum(d * _rows_down(x, before, 2)),
                _colsum(d * _rows_down(x, before, 1)), _colsum(d * x), _colsum(d)]
        outs = [dcw0_o, dcw1_o, dcw2_o, dcw3_o, dcb_o]

        @pl.when(t == 0)
        def _():
            for o, val in zip(outs, sums):
                o[...] = val

        @pl.when(t > 0)
        def _():
            for o, val in zip(outs, sums):
                o[...] += val

    res = pl.pallas_call(
        body,
        name="conv_bwd",
        grid=(nb,),
        in_specs=[pl.BlockSpec((T, BRANCH), lambda t: (t, 0)), _halo_after(T, lambda t: t, S),
                  pl.BlockSpec((T, BRANCH), lambda t: (t, U_AX // BRANCH)),
                  _halo_before(T, lambda t: t, U_AX // BRANCH)] + [_full_spec(v) for v in vecs],
        out_specs=[pl.BlockSpec((T, BRANCH), lambda t: (t, 0))] + [vrow] * 5,
        out_shape=[jax.ShapeDtypeStruct((S, BRANCH), BF16)] + [jax.ShapeDtypeStruct((1, BRANCH), F32)] * 5,
        compiler_params=pltpu.CompilerParams(dimension_semantics=("arbitrary",), vmem_limit_bytes=VMEM_LIMIT_BYTES),
    )(dxc, dxc, u, u, *vecs)
    return res[0], res[1:]


GLA_QK = GLA_HEADS * GLA_DK
GLA_V = GLA_HEADS * GLA_DV
GLA_SCALE = GLA_DK ** -0.5


def _gla_specs(TB, rev_nb=None):
    def rmap(t):
        return t if rev_nb is None else rev_nb - 1 - t
    return [
        pl.BlockSpec((TB, GLA_QK), lambda t: (rmap(t), U_BQ // GLA_QK)),
        pl.BlockSpec((TB, GLA_QK), lambda t: (rmap(t), U_BK // GLA_QK)),
        pl.BlockSpec((TB, GLA_V), lambda t: (rmap(t), U_BV // GLA_V)),
        pl.BlockSpec((TB, GLA_V), lambda t: (rmap(t), U_BR // GLA_V)),
        pl.BlockSpec((TB, LANES), lambda t: (rmap(t), U_BLOW // LANES)),
    ]


def _gla_gates(gl, wg2, bg, TB):
    pre = _dot(gl, wg2, "nn") + bg
    la = _log_sigmoid(pre) * (1.0 / GLA_TAU)
    _, gc = _scan_rows(None, la, TB, seg=CHUNK)
    return pre, la, gc


def _gla_fwd(u, gw, S):
    TB = min(512, S)
    nb = S // TB
    cpb = TB // CHUNK
    vecs = [gw["wg2"], gw["bg"], gw["ng"], gw["bd"]]

    def body(q_r, k_r, v_r, br_r, gl_r, wg2, bg, ng, bd, yb_o, oraw_o, st_o, st):
        t = pl.program_id(0)

        @pl.when(t == 0)
        def _():
            st[...] = jnp.zeros_like(st)

        _, la, gc = _gla_gates(gl_r[...], wg2[...], bg[...], TB)
        for c in range(cpb):
            sl = slice(c * CHUNK, (c + 1) * CHUNK)
            gt = _colsum(la[sl])
            kdec = k_r[sl, :] * jnp.exp(gt - gc[sl])
            d_t = _dot(v_r[sl, :], kdec, "tn") * bd[...]
            s_new = st[...] * jnp.exp(gt) + d_t
            st[...] = s_new
            st_o[c] = s_new
            oraw_o[sl, :] = _dot(q_r[sl, :] * GLA_SCALE, s_new, "nt")
        for h in range(GLA_HEADS):
            hs = slice(h * GLA_DV, (h + 1) * GLA_DV)
            oh = oraw_o[:, hs]
            on = oh * lax.rsqrt(jnp.mean(oh * oh, axis=-1, keepdims=True) + RMS_EPS)
            sil, _ = _silu_and_grad(br_r[:, hs])
            yb_o[:, hs] = (on * ng[:, hs] * sil).astype(yb_o.dtype)

    return pl.pallas_call(
        body,
        name="gla_fwd",
        grid=(nb,),
        in_specs=_gla_specs(TB) + [_full_spec(v) for v in vecs],
        out_specs=[pl.BlockSpec((TB, GLA_V), lambda t: (t, 0)), pl.BlockSpec((TB, GLA_V), lambda t: (t, 0)),
                   pl.BlockSpec((cpb, GLA_V, GLA_QK), lambda t: (t, 0, 0))],
        out_shape=[jax.ShapeDtypeStruct((S, GLA_V), BF16), jax.ShapeDtypeStruct((S, GLA_V), F32),
                   jax.ShapeDtypeStruct((S // CHUNK, GLA_V, GLA_QK), F32)],
        scratch_shapes=[pltpu.VMEM((GLA_V, GLA_QK), F32)],
        compiler_params=pltpu.CompilerParams(dimension_semantics=("arbitrary",), vmem_limit_bytes=VMEM_LIMIT_BYTES),
    )(u, u, u, u, u, *vecs)


def _gla_bwd(dyb, u, oraw, states, gw, S):
    TB = min(512, S)
    nb = S // TB
    cpb = TB // CHUNK
    vecs = [gw["wg2"], gw["bg"], gw["ng"], gw["bd"]]

    def rrow(width):
        return pl.BlockSpec((TB, width), lambda t: (nb - 1 - t, 0))

    def body(dyb_r, oraw_r, q_r, k_r, v_r, br_r, gl_r, st_r, sp_r, wg2, bg, ng, bd,
             dq_o, dk_o, dv_o, dbr_o, dgl_o, dwg2_o, dbg_o, dng_o, dcar, do_buf, dla_buf):
        t = pl.program_id(0)
        blk = nb - 1 - t

        @pl.when(t == 0)
        def _():
            dcar[...] = jnp.zeros_like(dcar)
            dwg2_o[...] = jnp.zeros_like(dwg2_o)
            dbg_o[...] = jnp.zeros_like(dbg_o)
            dng_o[...] = jnp.zeros_like(dng_o)

        pre, la, gc = _gla_gates(gl_r[...], wg2[...], bg[...], TB)
        for h in range(GLA_HEADS):
            hs = slice(h * GLA_DV, (h + 1) * GLA_DV)
            oh = oraw_r[:, hs]
            rs = lax.rsqrt(jnp.mean(oh * oh, axis=-1, keepdims=True) + RMS_EPS)
            on = oh * rs
            sil, dsil = _silu_and_grad(br_r[:, hs])
            dy = dyb_r[:, hs]
            dbr_o[:, hs] = (dy * on * ng[:, hs] * dsil).astype(dbr_o.dtype)
            don = dy * ng[:, hs] * sil
            dng_o[:, hs] += _colsum(dy * on * sil)
            do_buf[:, hs] = rs * (don - on * jnp.mean(don * on, axis=-1, keepdims=True))
        first = jnp.where(blk == 0, 0.0, 1.0)
        for c in reversed(range(cpb)):
            sl = slice(c * CHUNK, (c + 1) * CHUNK)
            s_n = st_r[c]
            s_prev = st_r[c - 1] if c > 0 else sp_r[0] * first
            gt = _colsum(la[sl])
            w = jnp.exp(gt - gc[sl])
            k_c = k_r[sl, :]
            kdec = k_c * w
            qs = q_r[sl, :] * GLA_SCALE
            do_c = do_buf[sl, :]
            dq_o[sl, :] = (_dot(do_c, s_n, "nn") * GLA_SCALE).astype(dq_o.dtype)
            d_n = _dot(do_c, qs, "tn") * bd[...] + dcar[...]
            dv_o[sl, :] = _dot(kdec, d_n, "nt").astype(dv_o.dtype)
            dkdec = _dot(v_r[sl, :], d_n, "nn")
            dk_o[sl, :] = (dkdec * w).astype(dk_o.dtype)
            tt = dkdec * kdec
            e = jnp.exp(gt)
            dgt = _colsum(tt) + _colsum(d_n * s_prev) * e
            _, rc = _scan_rows(None, -tt, CHUNK, reverse=True)
            dla_buf[sl, :] = rc + dgt
            dcar[...] = d_n * e
        dpre = dla_buf[...] * _sigmoid(-pre) * (1.0 / GLA_TAU)
        dbg_o[...] += _colsum(dpre)
        dgl_o[...] = _dot(dpre, wg2[...], "nt").astype(dgl_o.dtype)
        dwg2_o[...] += _dot(gl_r[...], dpre, "tn")

    return pl.pallas_call(
        body,
        name="gla_bwd",
        grid=(nb,),
        in_specs=[rrow(GLA_V), rrow(GLA_V)] + _gla_specs(TB, rev_nb=nb)
        + [pl.BlockSpec((cpb, GLA_V, GLA_QK), lambda t: (nb - 1 - t, 0, 0)),
           pl.BlockSpec((1, GLA_V, GLA_QK), lambda t: (jnp.maximum((nb - 1 - t) * cpb - 1, 0), 0, 0))]
        + [_full_spec(v) for v in vecs],
        out_specs=[rrow(GLA_QK), rrow(GLA_QK), rrow(GLA_V), rrow(GLA_V), rrow(LANES),
                   pl.BlockSpec((LANES, GLA_QK), lambda t: (0, 0)), pl.BlockSpec((1, GLA_QK), lambda t: (0, 0)),
                   pl.BlockSpec((1, GLA_V), lambda t: (0, 0))],
        out_shape=[jax.ShapeDtypeStruct((S, GLA_QK), BF16), jax.ShapeDtypeStruct((S, GLA_QK), BF16),
                   jax.ShapeDtypeStruct((S, GLA_V), BF16), jax.ShapeDtypeStruct((S, GLA_V), BF16),
                   jax.ShapeDtypeStruct((S, LANES), BF16), jax.ShapeDtypeStruct((LANES, GLA_QK), F32),
                   jax.ShapeDtypeStruct((1, GLA_QK), F32), jax.ShapeDtypeStruct((1, GLA_V), F32)],
        scratch_shapes=[pltpu.VMEM((GLA_V, GLA_QK), F32), pltpu.VMEM((TB, GLA_V), F32),
                        pltpu.VMEM((TB, GLA_QK), F32)],
        compiler_params=pltpu.CompilerParams(dimension_semantics=("arbitrary",), vmem_limit_bytes=VMEM_LIMIT_BYTES),
    )(dyb, oraw, u, u, u, u, u, states, states, *vecs)


FOX_SCALE = FOX_DH ** -0.5


def _fox_gate_fwd(u, bfp, S):
    T = min(512, S)

    def body(f_r, b_r, fc_o, car):
        t = pl.program_id(0)

        @pl.when(t == 0)
        def _():
            car[...] = jnp.zeros_like(car)

        _, cs = _scan_rows(None, _log_sigmoid(f_r[...] + b_r[...]), T)
        fc_o[...] = cs + car[...]
        car[...] = fc_o[pl.ds(T - 1, 1), :]

    return pl.pallas_call(
        body,
        name="fox_gate_fwd",
        grid=(S // T,),
        in_specs=[pl.BlockSpec((T, LANES), lambda t: (t, U_CF // LANES)), _full_spec(bfp)],
        out_specs=pl.BlockSpec((T, LANES), lambda t: (t, 0)),
        out_shape=jax.ShapeDtypeStruct((S, LANES), F32),
        scratch_shapes=[pltpu.VMEM((1, LANES), F32)],
        compiler_params=pltpu.CompilerParams(dimension_semantics=("arbitrary",), vmem_limit_bytes=VMEM_LIMIT_BYTES),
    )(u, bfp)


def _fox_gate_bwd(dfc, u, bfp, S):
    T = min(512, S)
    nb = S // T

    def body(d_r, f_r, b_r, df_o, db_o, car, tmp):
        t = pl.program_id(0)

        @pl.when(t == 0)
        def _():
            car[...] = jnp.zeros_like(car)
            db_o[...] = jnp.zeros_like(db_o)

        _, rc = _scan_rows(None, d_r[...], T, reverse=True)
        tmp[...] = rc + car[...]
        car[...] = tmp[pl.ds(0, 1), :]
        df = tmp[...] * _sigmoid(-(f_r[...] + b_r[...]))
        df_o[...] = df.astype(df_o.dtype)
        db_o[...] += _colsum(df)

    return pl.pallas_call(
        body,
        name="fox_gate_bwd",
        grid=(nb,),
        in_specs=[pl.BlockSpec((T, LANES), lambda t: (nb - 1 - t, 0)),
                  pl.BlockSpec((T, LANES), lambda t: (nb - 1 - t, U_CF // LANES)), _full_spec(bfp)],
        out_specs=[pl.BlockSpec((T, LANES), lambda t: (nb - 1 - t, 0)), pl.BlockSpec((1, LANES), lambda t: (0, 0))],
        out_shape=[jax.ShapeDtypeStruct((S, LANES), BF16), jax.ShapeDtypeStruct((1, LANES), F32)],
        scratch_shapes=[pltpu.VMEM((1, LANES), F32), pltpu.VMEM((T, LANES), F32)],
        compiler_params=pltpu.CompilerParams(dimension_semantics=("arbitrary",), vmem_limit_bytes=VMEM_LIMIT_BYTES),
    )(dfc, u, bfp)


def _fox_call(name, body, tables, grid, in_specs, out_specs, out_shape, scratch, args, side):
    n_in, n_out, n_scr = len(in_specs), len(out_specs), len(scratch)
    semantics = ("parallel", "arbitrary")
    if side is not None:
        total = grid[0] * grid[1]
        phases = side["phases"]
        triggers = [0, total - 1] if len(phases) == 2 else [0, total * 7 // 10, total - 1]
        na, no = len(side["arrs"]), len(side["out_shapes"])
        s_in, s_out, s_sems = _side_specs(side)
        kernel_body = body

        def body(*refs):
            tabs, rest = refs[:len(tables)], refs[len(tables):]
            ins, s_ins = rest[:n_in], rest[n_in:n_in + na]
            rest = rest[n_in + na:]
            outs, s_outs = rest[:n_out], rest[n_out:n_out + no]
            rest = rest[n_out + no:]
            scr, sems = rest[:n_scr], rest[n_scr:]
            flat = pl.program_id(0) * grid[1] + pl.program_id(1)
            for trigger, phase in zip(triggers[:-1], phases[:-1]):
                @pl.when(flat == trigger)
                def _(phase=phase):
                    phase(s_ins, s_outs, *sems)
            kernel_body(*tabs, *ins, *outs, *scr)

            @pl.when(flat == triggers[-1])
            def _():
                phases[-1](s_ins, s_outs, *sems)

        in_specs, out_specs = in_specs + s_in, out_specs + s_out
        out_shape, scratch = out_shape + side["out_shapes"], scratch + s_sems
        args = list(args) + side["arrs"]
        semantics = ("arbitrary", "arbitrary")
    res = pl.pallas_call(
        body,
        name=name,
        grid_spec=pltpu.PrefetchScalarGridSpec(num_scalar_prefetch=len(tables), grid=grid, in_specs=in_specs,
                                               out_specs=out_specs, scratch_shapes=scratch),
        out_shape=out_shape,
        compiler_params=pltpu.CompilerParams(dimension_semantics=semantics, vmem_limit_bytes=VMEM_LIMIT_BYTES),
    )(*tables, *args)
    return res[:n_out], res[n_out:]


FOX_TILE = 1024
FOX_GROUP = 2
FOX_GROUP_FWD = 4
FOX_AUG = 128
FOX_ONES = 3


def _fox_pairs(n, by_key):
    pairs = [(qi, ki) for qi in range(n) for ki in range(qi + 1)]
    if by_key:
        pairs.sort(key=lambda qk: (qk[1], qk[0]))
    qs = jnp.asarray([qk[0] for qk in pairs], jnp.int32)
    ks = jnp.asarray([qk[1] for qk in pairs], jnp.int32)
    return qs, ks


def _fox_causal(sT):
    keys = lax.broadcasted_iota(jnp.int32, sT.shape, 0)
    queries = lax.broadcasted_iota(jnp.int32, sT.shape, 1)
    return jnp.where(keys <= queries, sT, NEG_BIG)


def _fox_fwd(qT, ka, vT, S, side=None):
    t = min(FOX_TILE, S)
    n = S // t
    qi_tab, ki_tab = _fox_pairs(n, by_key=False)

    G = FOX_GROUP_FWD

    def body(qi_ref, ki_ref, qT_r, ka_r, vT_r, oT_o, lse_o, m_s, l_s, acc):
        step = pl.program_id(1)
        qi, ki = qi_ref[step], ki_ref[step]

        @pl.when(ki == 0)
        def _():
            m_s[...] = jnp.full_like(m_s, NEG_BIG)
            l_s[...] = jnp.zeros_like(l_s)
            acc[...] = jnp.zeros_like(acc)

        def update(g, masked):
            sT = _dot(ka_r[g], qT_r[g], "nn")
            if masked:
                sT = _fox_causal(sT)
            m_new = jnp.maximum(m_s[g], jnp.max(sT, axis=0, keepdims=True))
            p = jnp.exp(sT - m_new)
            alpha = jnp.exp(m_s[g] - m_new)
            l_s[g] = alpha * l_s[g] + jnp.sum(p, axis=0, keepdims=True)
            acc[g] = alpha * acc[g] + _dot(vT_r[g], p, "nn")
            m_s[g] = m_new

        @pl.when(ki < qi)
        def _():
            for g in range(G):
                update(g, False)

        @pl.when(ki == qi)
        def _():
            for g in range(G):
                update(g, True)
                oT_o[g] = acc[g] / l_s[g]
                lse_o[g] = m_s[g] + jnp.log(l_s[g])

    return _fox_call(
        "fox_fwd", body, (qi_tab, ki_tab), (FOX_HEADS // G, int(qi_tab.shape[0])),
        [pl.BlockSpec((G, FOX_AUG, t), lambda h, s, qt, kt: (h, 0, qt[s])),
         pl.BlockSpec((G, t, FOX_AUG), lambda h, s, qt, kt: (h, kt[s], 0)),
         pl.BlockSpec((G, FOX_DH, t), lambda h, s, qt, kt: (h, 0, kt[s]))],
        [pl.BlockSpec((G, FOX_DH, t), lambda h, s, qt, kt: (h, 0, qt[s])),
         pl.BlockSpec((G, 1, t), lambda h, s, qt, kt: (h, 0, qt[s]))],
        [jax.ShapeDtypeStruct((FOX_HEADS, FOX_DH, S), F32), jax.ShapeDtypeStruct((FOX_HEADS, 1, S), F32)],
        [pltpu.VMEM((G, 1, t), F32), pltpu.VMEM((G, 1, t), F32), pltpu.VMEM((G, FOX_DH, t), F32)],
        (qT, ka, vT), side)


FOX_BIAS_ROWS = 8


def _fox_bwd(qT, qa, ka, kT, v, do, doT, oT, lse, S, side=None):
    t = min(FOX_TILE, S)
    n = S // t
    qi_tab, ki_tab = _fox_pairs(n, by_key=True)
    n_steps = int(qi_tab.shape[0])
    slab = slice(FOX_DH, FOX_DH + FOX_BIAS_ROWS)

    G = FOX_GROUP

    def body(qi_ref, ki_ref, qT_r, qa_r, ka_r, kT_r, v_r, do_r, doT_r, oT_r, lse_r,
             dq_o, dfq_o, dk_o, dfk_o, dv_o, dq_acc, dk_acc, dv_acc):
        step = pl.program_id(1)
        qi, ki = qi_ref[step], ki_ref[step]

        @pl.when(step == 0)
        def _():
            dq_acc[...] = jnp.zeros_like(dq_acc)

        @pl.when(qi == ki)
        def _():
            dk_acc[...] = jnp.zeros_like(dk_acc)
            dv_acc[...] = jnp.zeros_like(dv_acc)

        def update(g, masked):
            sT = _dot(ka_r[g], qT_r[g], "nn")
            if masked:
                sT = _fox_causal(sT)
            pT = jnp.exp(sT - lse_r[g])
            delta = jnp.sum(oT_r[g] * doT_r[g], axis=0, keepdims=True)
            dsT = pT * (_dot(v_r[g], doT_r[g], "nn") - delta)
            dv_acc[g] += _dot(pT, do_r[g], "nn")
            dk_acc[g] += _dot(dsT, qa_r[g], "nn")
            dq_acc[g, qi] += _dot(kT_r[g], dsT, "nn")

        @pl.when(qi > ki)
        def _():
            for g in range(G):
                update(g, False)

        @pl.when(qi == ki)
        def _():
            for g in range(G):
                update(g, True)

        @pl.when(qi == n - 1)
        def _():
            for g in range(G):
                dk = dk_acc[g]
                dk_o[g] = dk[:, :FOX_DH].astype(dk_o.dtype)
                dfk_o[g] = dk.T[slab]
                dv_o[g] = dv_acc[g].astype(dv_o.dtype)

        @pl.when(step == n_steps - 1)
        def _():
            for g in range(G):
                for j in range(n):
                    dqT = dq_acc[g, j]
                    dq_o[g, j * t:(j + 1) * t, :] = (dqT.T[:, :FOX_DH] * FOX_SCALE).astype(dq_o.dtype)
                    dfq_o[g, :, j * t:(j + 1) * t] = dqT[slab]

    def qlane(rows):
        return pl.BlockSpec((G, rows, t), lambda h, s, qt, kt: (h, 0, qt[s]))

    def qrow(cols):
        return pl.BlockSpec((G, t, cols), lambda h, s, qt, kt: (h, qt[s], 0))

    def krow(cols):
        return pl.BlockSpec((G, t, cols), lambda h, s, qt, kt: (h, kt[s], 0))

    def klane(rows):
        return pl.BlockSpec((G, rows, t), lambda h, s, qt, kt: (h, 0, kt[s]))

    def head(rows, cols):
        return pl.BlockSpec((G, rows, cols), lambda h, s, qt, kt: (h, 0, 0))

    return _fox_call(
        "fox_bwd", body, (qi_tab, ki_tab), (FOX_HEADS // G, n_steps),
        [qlane(FOX_AUG), qrow(FOX_AUG), krow(FOX_AUG), klane(FOX_AUG), krow(FOX_DH), qrow(FOX_DH), qlane(FOX_DH),
         qlane(FOX_DH), qlane(1)],
        [head(S, FOX_DH), head(FOX_BIAS_ROWS, S), krow(FOX_DH), klane(FOX_BIAS_ROWS), krow(FOX_DH)],
        [jax.ShapeDtypeStruct((FOX_HEADS, S, FOX_DH), BF16), jax.ShapeDtypeStruct((FOX_HEADS, FOX_BIAS_ROWS, S), F32),
         jax.ShapeDtypeStruct((FOX_HEADS, S, FOX_DH), BF16), jax.ShapeDtypeStruct((FOX_HEADS, FOX_BIAS_ROWS, S), F32),
         jax.ShapeDtypeStruct((FOX_HEADS, S, FOX_DH), BF16)],
        [pltpu.VMEM((G, n, FOX_AUG, t), F32), pltpu.VMEM((G, t, FOX_AUG), F32), pltpu.VMEM((G, t, FOX_DH), F32)],
        (qT, qa, ka, kT, v, do, doT, oT, lse), side)


def _fox_prep(u, fcum, S):
    T = min(512, S)
    head_of = jnp.arange(BRANCH) // FOX_DH
    dim_of = jnp.arange(BRANCH) % FOX_DH
    heads = jnp.arange(FOX_HEADS)[:, None, None]
    sel = (head_of[None, :, None] == heads) & (dim_of[None, :, None] == jnp.arange(FOX_AUG)[None, None, :])
    sel_q = (sel * FOX_SCALE).astype(BF16)
    sel_k = sel.astype(BF16)
    sel_vT = jnp.swapaxes(sel[:, :, :FOX_DH], 1, 2).astype(BF16)
    piece = jnp.arange(FOX_ONES * LANES) // LANES
    lane = jnp.arange(FOX_ONES * LANES) % LANES
    col = jnp.arange(FOX_AUG)[None, None, :]
    at_q = (lane[None, :, None] == heads) & (col == FOX_DH + FOX_ONES + piece[None, :, None])
    at_k = (lane[None, :, None] == heads) & (col == FOX_DH + piece[None, :, None])
    bias_q = at_q.astype(BF16)
    bias_k = (-at_k.astype(F32)).astype(BF16)
    cols = jnp.arange(FOX_AUG)[None, :]
    ones_q = ((cols >= FOX_DH) & (cols < FOX_DH + FOX_ONES)).astype(F32)
    ones_k = ((cols >= FOX_DH + FOX_ONES) & (cols < FOX_DH + 2 * FOX_ONES)).astype(F32)
    consts = [sel_q, sel_k, sel_vT, bias_q, bias_k, ones_q, ones_k]

    def body(cq, ck, cv, fc, sq, sk, svT, bq, bk, oq, ok, qa_o, ka_o, qT_o, kT_o, vh_o, vT_o):
        f = fc[...]
        hi = f.astype(BF16).astype(F32)
        mid = (f - hi).astype(BF16).astype(F32)
        lo = (f - hi - mid).astype(BF16).astype(F32)
        pieces = jnp.concatenate([hi, mid, lo], axis=1)
        q, k, v = cq[...], ck[...], cv[...]
        for h in range(FOX_HEADS):
            qa = _dot(q, sq[h], "nn") + _dot(pieces, bq[h], "nn") + oq[...]
            ka = _dot(k, sk[h], "nn") + _dot(pieces, bk[h], "nn") + ok[...]
            qa_o[h] = qa.astype(qa_o.dtype)
            ka_o[h] = ka.astype(ka_o.dtype)
            qT_o[h] = qa.T.astype(qT_o.dtype)
            kT_o[h] = ka.T.astype(kT_o.dtype)
            vT_o[h] = _dot(svT[h], v, "nt").astype(vT_o.dtype)
            vh_o[h] = _dot(v, svT[h], "nt").astype(vh_o.dtype)

    def win(off):
        return pl.BlockSpec((T, BRANCH), functools.partial(lambda i, blk: (i, blk), blk=off // BRANCH))

    def rows(c):
        return pl.BlockSpec((FOX_HEADS, T, c), lambda i: (0, i, 0))

    def lanes(r):
        return pl.BlockSpec((FOX_HEADS, r, T), lambda i: (0, 0, i))

    bf = lambda *shape: jax.ShapeDtypeStruct((FOX_HEADS,) + shape, BF16)
    return pl.pallas_call(
        body,
        name="fox_prep",
        grid=(S // T,),
        in_specs=[win(U_CQ), win(U_CK), win(U_CV), pl.BlockSpec((T, LANES), lambda i: (i, 0))]
        + [_full_spec(c) for c in consts],
        out_specs=[rows(FOX_AUG), rows(FOX_AUG), lanes(FOX_AUG), lanes(FOX_AUG), rows(FOX_DH), lanes(FOX_DH)],
        out_shape=[bf(S, FOX_AUG), bf(S, FOX_AUG), bf(FOX_AUG, S), bf(FOX_AUG, S), bf(S, FOX_DH), bf(FOX_DH, S)],
        compiler_params=pltpu.CompilerParams(dimension_semantics=("parallel",), vmem_limit_bytes=VMEM_LIMIT_BYTES),
    )(u, u, u, fcum, *consts)


def _to_heads(x2d, S):
    return jnp.transpose(x2d.reshape(S, FOX_HEADS, FOX_DH), (1, 0, 2))


def _from_heads(xh, S):
    return jnp.transpose(xh, (1, 0, 2)).reshape(S, FOX_HEADS * FOX_DH)


def _ffn_fwd(tag, x, wgT, wuT, wd, g, b, S, side=None):
    def up_epi(accs):
        gate, up = accs
        sil, _ = _silu_and_grad(gate)
        return [gate, up, sil * up]

    res = _mm(tag + "_up", "nt", [x], [wgT, wuT], [(0, 0, 0), (1, 0, 1)], 2, up_epi, [],
              [BF16, BF16, BF16], S, D_FF, D_MODEL, tn=1408, side=side)
    (gate, up, act), side_out = res if side is not None else (res, None)

    def down_epi(accs, xr, gg, bb):
        z = ALPHA * xr + 0.5 * accs[0]
        return [z, _ln_fwd(z, gg, bb)]

    z, xn = _mm(tag + "_down", "nn", [act], [wd], [(0, 0, 0)], 1, down_epi, [(x, "mn", 0), (g, "n"), (b, "n")],
                [F32, F32], S, D_MODEL, D_FF, tk=D_FF)
    return xn, dict(x=x, gate=gate, up=up, act=act, z=z), side_out


def _ln_bwd_call(tag, dy, z, g, S):
    def fn(dy_t, z_t, g_t):
        dz, xhat = _ln_bwd(dy_t, z_t, g_t)
        return [dz], [_colsum(dy_t * xhat), _colsum(dy_t)]

    (dz,), (dg, db) = _rowwise(tag + "_ln_bwd", fn, [dy, z], [g], [(D_MODEL, F32)], [D_MODEL, D_MODEL], S)
    return dz, dg, db


def _ffn_bwd(tag, dxn, sv, wgT, wuT, wd, g, S, gdt=F32, make_side=None, first_side=None):
    dz, dg, db = _ln_bwd_call(tag, dxn, sv["z"], g, S)

    def act_epi(accs, gate, up):
        da = 0.5 * accs[0]
        sil, dsil = _silu_and_grad(gate.astype(F32))
        return [da * up.astype(F32) * dsil, da * sil]

    res = _mm(tag + "_dact", "nt", [dz], [wd], [(0, 0, 0)], 1, act_epi,
              [(sv["gate"], "mn", 0), (sv["up"], "mn", 0)], [BF16, BF16], S, D_FF, D_MODEL, tn=1408, side=first_side)
    (dgate, dup), first_out = res if first_side is not None else (res, None)
    dwd = _mm1(tag + "_dwd", "tn", sv["act"], dz, D_FF, D_MODEL, S, scale=0.5, tm=1408, out_dtype=gdt)

    def two(accs):
        return [accs[0], accs[1]]

    dwgT, dwuT = _mm(tag + "_dwup", "tn", [dgate, dup], [sv["x"]], [(0, 0, 0), (1, 1, 0)], 2, two, [], [gdt, gdt],
                     D_FF, D_MODEL, S, tm=1408, tk=512)

    def dx_epi(accs, dzr):
        return [accs[0] + ALPHA * dzr]

    grads = dict(w_upT=jnp.concatenate([dwgT, dwuT], axis=0), w_down=dwd, ln_g=dg, ln_b=db)
    side = make_side(grads) if make_side is not None else None
    res = _mm(tag + "_dx", "nn", [dgate, dup], [wgT, wuT], [(0, 0, 0), (0, 1, 1)], 1, dx_epi, [(dz, "mn", 0)],
              [F32], S, D_MODEL, D_FF, tm=1024, tk=1408, side=side)
    (dx,), side_out = res if side is not None else (res, None)
    return dx, grads, (first_out, side_out)


def _mixer_fwd(x1, w, S, side=None, on_side=None):
    u = _mm1("w_in", "nt", x1, w["w_inT_p"], S, U_WIDTH, D_MODEL, tm=1024, tn=1536)
    xc, r, gi, a, h, y_a = _lru_fwd(u, w["lru"], S)
    y_b, oraw, states = _gla_fwd(u, w["gla"], S)
    fcum = _fox_gate_fwd(u, w["bfp"], S)
    qa, ka, qT, kT, vh, vT = _fox_prep(u, fcum, S)
    (oT, lse), side_out = _fox_fwd(qT, ka, vT, S, side)
    if on_side is not None:
        on_side(side_out)
    y_c = jnp.transpose(oT, (2, 0, 1)).reshape(S, BRANCH).astype(BF16)

    def merge_epi(accs, g0, g1, g2):
        merged = _sigmoid(g0) * accs[0] + _sigmoid(g1) * accs[1] + _sigmoid(g2) * accs[2]
        return [accs[0], accs[1], accs[2], merged]

    wb = w["w_branchT"]
    yp0, yp1, yp2, merged = _mm(
        "merge", "nt", [y_a, y_b, y_c], [wb[0], wb[1], wb[2]], [(0, 0, 0), (1, 1, 1), (2, 2, 2)], 3, merge_epi,
        [(u, "mn", 0), (u, "mn", 1), (u, "mn", 2)], [BF16, BF16, BF16, BF16], S, D_MODEL, BRANCH, tm=256)

    def out_epi(accs, xr, gg, bb):
        z = ALPHA * xr + accs[0]
        return [z, _ln_fwd(z, gg, bb)]

    z2, x2 = _mm("w_out", "nn", [merged], [w["w_out"]], [(0, 0, 0)], 1, out_epi,
                 [(x1, "mn", 0), (w["ln2_g"], "n"), (w["ln2_b"], "n")], [F32, F32], S, D_MODEL, D_MODEL)
    sv = dict(x=x1, u=u, xc=xc, r=r, i=gi, a=a, h=h, y_a=y_a, y_b=y_b, y_c=y_c, oraw=oraw,
              states=states, qT=qT, qa=qa, ka=ka, kT=kT, vh=vh, oT=oT, lse=lse, yp=(yp0, yp1, yp2), merged=merged,
              z=z2)
    return x2, sv, side_out


def _mixer_bwd(dx2, sv, w, S, make_side=None, gdt_a=F32, gdt_b=F32):
    u = sv["u"]
    dz, dg2, db2 = _ln_bwd_call("mix", dx2, sv["z"], w["ln2_g"], S)

    def dm_epi(accs, y0, y1, y2, g0, g1, g2):
        dm = accs[0]
        outs_p, outs_g = [], []
        for yp, gl in ((y0, g0), (y1, g1), (y2, g2)):
            sg = _sigmoid(gl)
            outs_p.append(dm * sg)
            outs_g.append(dm * yp.astype(F32) * sg * (1.0 - sg))
        return outs_p + outs_g

    yp = sv["yp"]
    dyp0, dyp1, dyp2, dgl0, dgl1, dgl2 = _mm(
        "dmerged", "nt", [dz], [w["w_out"]], [(0, 0, 0)], 1, dm_epi,
        [(yp[0], "mn", 0), (yp[1], "mn", 0), (yp[2], "mn", 0), (u, "mn", 0), (u, "mn", 1), (u, "mn", 2)],
        [BF16] * 6, S, D_MODEL, D_MODEL, tm=256)
    dw_out = _mm1("dw_out", "tn", sv["merged"], dz, D_MODEL, D_MODEL, S, out_dtype=gdt_b)
    wb = w["w_branchT"]
    dys, dwbs = [], []
    for j, (yj, dyp) in enumerate(((sv["y_a"], dyp0), (sv["y_b"], dyp1), (sv["y_c"], dyp2))):
        dys.append(_mm1("dy_branch%d" % j, "nn", dyp, wb[j], S, BRANCH, D_MODEL))
        dwbs.append(_mm1("dw_branch%d" % j, "tn", dyp, yj, D_MODEL, BRANCH, S, out_dtype=gdt_b))
    day, dxc, dwa, dwx, dba, dbx, dlam = _lru_bwd(dys[0], u, sv, w["lru"], S)
    dax, (dcw0, dcw1, dcw2, dcw3, dcb) = _conv_bwd(dxc, u, w["lru"], S)
    dbq, dbk, dbv, dbr, dglow, dwg2p, dbg, dng = _gla_bwd(dys[1], u, sv["oraw"], sv["states"], w["gla"], S)
    doh = _to_heads(dys[2], S)
    dw_branchT = jnp.stack(dwbs)
    side = make_side(dict(w_out=dw_out, w_branchT=dw_branchT)) if make_side is not None else None
    (dqh, dfq, dkh, dfk, dvh), side_out = _fox_bwd(sv["qT"], sv["qa"], sv["ka"], sv["kT"], sv["vh"], doh,
                                                   jnp.swapaxes(doh, 1, 2), sv["oT"], sv["lse"], S, side)
    dfc = jnp.transpose(dfq[:, FOX_ONES, :] - dfk[:, 0, :])
    dfc = jnp.pad(dfc, ((0, 0), (0, LANES - FOX_HEADS)))
    dcf, dbf = _fox_gate_bwd(dfc, u, w["bfp"], S)
    du = jnp.concatenate(
        [dgl0, dgl1, dgl2, dax, day, dbq, dbk, dbv, dbr, _from_heads(dqh, S).astype(BF16),
         _from_heads(dkh, S).astype(BF16), _from_heads(dvh, S).astype(BF16), dglow, dcf,
         jnp.zeros((S, U_WIDTH - U_CF - LANES), BF16)], axis=1)
    dw_inT_p = _mm1("dw_in", "tn", du, sv["x"], U_WIDTH, D_MODEL, S, tm=1536, out_dtype=gdt_a)

    def dx_epi(accs, dzr):
        return [accs[0] + ALPHA * dzr]

    (dx1,) = _mm("dx_mix", "nn", [du], [w["w_inT_p"]], [(0, 0, 0)], 1, dx_epi, [(dz, "mn", 0)], [F32], S, D_MODEL,
                 U_WIDTH, tm=1024, tk=1536)
    pieces = sorted(W_IN_SEGMENTS)
    dw_inT = jnp.concatenate([dw_inT_p[dst:dst + width] for _, width, dst in pieces], axis=0)
    eye = jnp.eye(LRU_BLOCKS, dtype=F32)
    dwa_b = jnp.einsum("ncmd,nm->ncd", dwa.reshape(LRU_BLOCKS, 64, LRU_BLOCKS, 64), eye)
    dwx_b = jnp.einsum("ncmd,nm->ncd", dwx.reshape(LRU_BLOCKS, 64, LRU_BLOCKS, 64), eye)
    grads = dict(
        w_inT=dw_inT, w_out=dw_out, w_branchT=dw_branchT, ln2_g=dg2, ln2_b=db2,
        conv_w=jnp.concatenate([dcw0, dcw1, dcw2, dcw3], axis=0).astype(gdt_a), conv_b=dcb, lru_wa=dwa_b, lru_wx=dwx_b,
        lru_ba=dba, lru_bx=dbx, lru_lambda=dlam, gla_w_g2=dwg2p[:GLA_LOWRANK].astype(gdt_a), gla_b_g=dbg, gla_norm_g=dng,
        fox_b_f=dbf[:, :FOX_HEADS])
    return dx1, grads, side_out


def _ple_fwd(x3, p_i, w, S):
    pe = _mm1("ple_proj", "nt", p_i, w["ple_w_projT"], S, D_MODEL, PLE_DIM)

    def epi(accs, xr, per, bg, gg, bb):
        sg = _sigmoid(accs[0] + bg)
        z = ALPHA * xr + sg * per
        return [sg, z, _ln_fwd(z, gg, bb)]

    sg, z4, x4 = _mm("ple_gate", "nn", [x3], [w["ple_w_gate"]], [(0, 0, 0)], 1, epi,
                     [(x3, "mn", 0), (pe, "mn", 0), (w["ple_b_gate"], "n"), (w["ln4_g"], "n"), (w["ln4_b"], "n")],
                     [F32, F32, F32], S, D_MODEL, D_MODEL)
    return x4, dict(x=x3, p=p_i, pe=pe, sg=sg, z=z4)


def _ple_bwd(dx4, sv, w, S, gdt=F32):
    def fn(dy_t, z_t, pe_t, sg_t, g_t):
        dz, xhat = _ln_bwd(dy_t, z_t, g_t)
        dgl = dz * pe_t * sg_t * (1.0 - sg_t)
        return [dz, dz * sg_t, dgl], [_colsum(dy_t * xhat), _colsum(dy_t), _colsum(dgl)]

    (dz, dpe, dgl), (dg4, db4, dbg) = _rowwise(
        "ple_bwd", fn, [dx4, sv["z"], sv["pe"], sv["sg"]], [w["ln4_g"]],
        [(D_MODEL, F32), (D_MODEL, BF16), (D_MODEL, BF16)], [D_MODEL] * 3, S)
    dwpT = _mm1("dw_ple_proj", "tn", dpe, sv["p"], D_MODEL, PLE_DIM, S, out_dtype=gdt)
    dwg = _mm1("dw_ple_gate", "tn", sv["x"], dgl, D_MODEL, D_MODEL, S, out_dtype=gdt)

    def dx_epi(accs, dzr):
        return [accs[0] + ALPHA * dzr]

    (dx3,) = _mm("dx_ple", "nt", [dgl], [w["ple_w_gate"]], [(0, 0, 0)], 1, dx_epi, [(dz, "mn", 0)], [F32], S,
                 D_MODEL, D_MODEL)
    return dx3, dict(ple_w_projT=dwpT, ple_w_gate=dwg, ple_b_gate=dbg, ln4_g=dg4, ln4_b=db4)


def _rows_of_all(g):
    return g.reshape((g.shape[0] * g.shape[1],) + g.shape[2:])


EARLY_WEIGHTS = ("ffn1_w_up", "ffn1_w_down", "w_in", "conv_w", "gla_w_g2")


def _ffn_weights(gathered, tag):
    upT = _rows_of_all(gathered[tag + "_w_up"])
    return upT[:D_FF], upT[D_FF:], _rows_of_all(gathered[tag + "_w_down"])


def _late_weights(gathered):
    return dict(ffn2=_ffn_weights(gathered, "ffn2"),
                w_branchT=jnp.moveaxis(gathered["w_branch"], 0, 1).reshape(3, D_MODEL, BRANCH),
                w_out=_rows_of_all(gathered["w_out"]),
                ple_w_projT=_rows_of_all(gathered["ple_w_proj"]),
                ple_w_gate=_rows_of_all(gathered["ple_w_gate"]))


def _w_in_operand(gathered_w_in):
    w_inT = _rows_of_all(gathered_w_in)
    placed = sorted((dst, src, width) for src, width, dst in W_IN_SEGMENTS)
    parts, pos = [], 0
    for dst, src, width in placed:
        if dst > pos:
            parts.append(jnp.zeros((dst - pos, D_MODEL), w_inT.dtype))
        parts.append(w_inT[src:src + width])
        pos = dst + width
    parts.append(jnp.zeros((U_WIDTH - pos, D_MODEL), w_inT.dtype))
    return jnp.concatenate(parts, axis=0)


def _layer_weights(gathered, full, i):
    w = _late_weights(gathered) if "w_out" in gathered else {}
    w["ffn1"] = _ffn_weights(gathered, "ffn1")
    if "w_in" in gathered:
        w["w_inT_p"] = _w_in_operand(gathered["w_in"])
    eye = jnp.eye(LRU_BLOCKS, dtype=F32)

    def dense(blocks):
        return jnp.einsum("ncd,nm->ncmd", blocks, eye).reshape(BRANCH, BRANCH).astype(BF16)

    def vec(name):
        return full[name][i].reshape(1, -1)

    cw = jnp.moveaxis(gathered["conv_w"], 0, 1).reshape(4, BRANCH)
    w_g2 = jnp.moveaxis(gathered["gla_w_g2"], 0, 1).reshape(GLA_LOWRANK, GLA_QK)
    w["lru"] = dict(cw0=cw[0:1], cw1=cw[1:2], cw2=cw[2:3], cw3=cw[3:4], conv_b=vec("conv_b"),
                    wa=dense(full["lru_wa"][i]), wx=dense(full["lru_wx"][i]), ba=vec("lru_ba"), bx=vec("lru_bx"),
                    lam=vec("lru_lambda"))
    hq = jnp.arange(GLA_QK) // GLA_DK
    hv = jnp.arange(GLA_V) // GLA_DV
    w["gla"] = dict(wg2=jnp.pad(w_g2, ((0, LANES - GLA_LOWRANK), (0, 0))).astype(BF16),
                    bg=vec("gla_b_g"), ng=vec("gla_norm_g"), bd=(hv[:, None] == hq[None, :]).astype(F32))
    w["bfp"] = jnp.pad(vec("fox_b_f"), ((0, 0), (0, LANES - FOX_HEADS)))
    for name in ("ln1_g", "ln1_b", "ln2_g", "ln2_b", "ln3_g", "ln3_b", "ln4_g", "ln4_b", "ple_b_gate"):
        w[name] = vec(name)
    return w


def _layer_fwd(x0, p_i, w, S, side=None, on_side=None, first_side=None, on_first=None):
    x1, s1, first_out = _ffn_fwd("ffn1", x0, *w["ffn1"], w["ln1_g"], w["ln1_b"], S, first_side)
    if on_first is not None:
        on_first(first_out)
    x2, s2, side_out = _mixer_fwd(x1, w, S, side, on_side)
    x3, s3, _ = _ffn_fwd("ffn2", x2, *w["ffn2"], w["ln3_g"], w["ln3_b"], S)
    x4, s4 = _ple_fwd(x3, p_i, w, S)
    return x4, (s1, s2, s3, s4), side_out


def _layer_bwd(dx4, saved, w, S, make_side=None, gdt_a=F32, gdt_b=F32, make_first=None, make_last=None):
    s1, s2, s3, s4 = saved
    dx3, g4 = _ple_bwd(dx4, s4, w, S, gdt_b)
    dx2, g3, _ = _ffn_bwd("ffn2", dx3, s3, *w["ffn2"], w["ln3_g"], S, gdt_b)
    late = dict(ffn2_w_upT=g3["w_upT"], ffn2_w_down=g3["w_down"], ple_w_projT=g4["ple_w_projT"],
                ple_w_gate=g4["ple_w_gate"])
    mixer_side = None if make_side is None else (lambda mix: make_side({**late, **mix}))
    dx1, g2, side_out = _mixer_bwd(dx2, s2, w, S, mixer_side, gdt_a, gdt_b)
    last = None if make_last is None else (
        lambda g: make_last({"ffn1_w_upT": g["w_upT"], "ffn1_w_down": g["w_down"]}))
    first = None if make_first is None else make_first(g2)
    dx0, g1, last_out = _ffn_bwd("ffn1", dx1, s1, *w["ffn1"], w["ln1_g"], S, gdt_a, last, first)
    grads = dict(g2)
    grads.update(g4)
    grads.update(late)
    grads.update(ffn1_w_upT=g1["w_upT"], ffn1_w_down=g1["w_down"], ln1_g=g1["ln_g"], ln1_b=g1["ln_b"],
                 ln3_g=g3["ln_g"], ln3_b=g3["ln_b"])
    return dx0, grads, side_out, last_out


def _travel_grads(grads, names):
    return [_dest_pieces(n, grads[n + "T" if n in COLUMN_SHARDED else n]) for n in names]


def _local_step(x, p, target, gathered0, rest, full, overlap):
    S = x.shape[0]
    names = [n for n, _ in SHARDED]
    early = [n for n in names if n in EARLY_WEIGHTS]
    late = [n for n in names if n not in EARLY_WEIGHTS]
    w0 = _layer_weights(gathered0, full, 0)
    if not overlap:
        h, saved0, _ = _layer_fwd(x, p[0], w0, S)
        w1 = _layer_weights(rest, full, 1)
        h, saved1, _ = _layer_fwd(h, p[1], w1, S)
    else:
        w_in0, late0, early1, late1 = rest
        h, saved0, got = _layer_fwd(x, p[0], w0, S, _gather_job(list(late0) + list(early1)),
                                    lambda got: w0.update(_late_weights(dict(zip(late, got[:len(late)])))),
                                    _gather_job([w_in0]), lambda got: w0.update(w_inT_p=_w_in_operand(got[0])))
        w1 = _layer_weights(dict(zip(early, got[len(late):])), full, 1)
        h, saved1, _ = _layer_fwd(h, p[1], w1, S, _gather_job(list(late1)),
                                  lambda got: w1.update(_late_weights(dict(zip(late, got)))))

    def loss_fn(y, t):
        err = y - t
        return [err * (1.0 / D_MODEL)], [_colsum(err * err) * (0.5 / D_MODEL)]

    (dy,), (lsum,) = _rowwise("loss", loss_fn, [h, target], [], [(D_MODEL, F32)], [D_MODEL], S)
    loss = jnp.sum(lsum)
    if not overlap:
        dy, g1, _, _ = _layer_bwd(dy, saved1, w1, S)
        dy, g0, _, _ = _layer_bwd(dy, saved0, w0, S)
        return loss, dy, [g0, g1], {}

    ffn_early = [n for n in early if n.startswith("ffn1")]
    mix_early = [n for n in early if not n.startswith("ffn1")]

    def two_stage(tag, which):
        def make(g):
            dest = _travel_grads(g, which)
            got = _sibling_swap_multi("grad_sibling_swap_" + tag, dest)
            core = lax.axis_index("c").astype(jnp.int32).reshape(1)
            return _chip_job([_pair_add("grad_pair_add_" + n, core, _as_rows(d, 2), _as_rows(a, 1))
                              for n, d, a in zip(which, dest, got)])
        return make

    dy, g1, late1_pieces, _ = _layer_bwd(dy, saved1, w1, S, lambda g: _scatter_job(_travel_grads(g, late)),
                                         BF16, BF16)
    dy, g0, pieces, (mix0_pieces, ffn0_pieces) = _layer_bwd(
        dy, saved0, w0, S, lambda g: _scatter_job(_travel_grads(g1, early) + _travel_grads(g, late)), F32, BF16,
        two_stage("mixer", mix_early), two_stage("ffn", ffn_early))
    exchanged = {(1, n): a for n, a in zip(late, late1_pieces)}
    exchanged.update({(1, n): a for n, a in zip(early, pieces[:len(early)])})
    exchanged.update({(0, n): a for n, a in zip(late, pieces[len(early):])})
    exchanged.update({(0, n): a for n, a in zip(mix_early, mix0_pieces)})
    exchanged.update({(0, n): a for n, a in zip(ffn_early, ffn0_pieces)})
    return loss, dy, [g0, g1], exchanged


def kernel(x, p, ffn1_w_up, ffn1_w_down, ln1_g, ln1_b, w_in, conv_w, conv_b, lru_wa, lru_ba, lru_wx, lru_bx, lru_lambda, gla_w_g2, gla_b_g, gla_norm_g, fox_b_f, w_branch, w_out, ln2_g, ln2_b, ffn2_w_up, ffn2_w_down, ln3_g, ln3_b, ple_w_proj, ple_w_gate, ple_b_gate, ln4_g, ln4_b, loss_target, m_ffn1_w_up, m_ffn1_w_down, m_ln1_g, m_ln1_b, m_w_in, m_conv_w, m_conv_b, m_lru_wa, m_lru_ba, m_lru_wx, m_lru_bx, m_lru_lambda, m_gla_w_g2, m_gla_b_g, m_gla_norm_g, m_fox_b_f, m_w_branch, m_w_out, m_ln2_g, m_ln2_b, m_ffn2_w_up, m_ffn2_w_down, m_ln3_g, m_ln3_b, m_ple_w_proj, m_ple_w_gate, m_ple_b_gate, m_ln4_g, m_ln4_b, v_ffn1_w_up, v_ffn1_w_down, v_ln1_g, v_ln1_b, v_w_in, v_conv_w, v_conv_b, v_lru_wa, v_lru_ba, v_lru_wx, v_lru_bx, v_lru_lambda, v_gla_w_g2, v_gla_b_g, v_gla_norm_g, v_fox_b_f, v_w_branch, v_w_out, v_ln2_g, v_ln2_b, v_ffn2_w_up, v_ffn2_w_down, v_ln3_g, v_ln3_b, v_ple_w_proj, v_ple_w_gate, v_ple_b_gate, v_ln4_g, v_ln4_b):
    env = dict(locals())
    wts = {n: env[n] for n in WEIGHTS}
    ms = {n: env["m_" + n] for n in WEIGHTS}
    vs = {n: env["v_" + n] for n in WEIGHTS}
    sharded = [n for n, _ in SHARDED]

    def travel(n, a):
        return jnp.swapaxes(a, -1, -2) if n in COLUMN_SHARDED else a

    shards = {(i, n): travel(n, wts[n][i]) if n in SHARDED_F32_GATHER else travel(n, wts[n][i]).astype(BF16)
              for i in range(DEPTH) for n in sharded}
    early = [n for n in sharded if n in EARLY_WEIGHTS]
    late = [n for n in sharded if n not in EARLY_WEIGHTS]
    first = [n for n in early if n != "w_in"]
    gathered0 = dict(zip(first, _allgather_multi("gather_weights", [shards[0, n] for n in first])))
    rest = (shards[0, "w_in"], [shards[0, n] for n in late], [shards[1, n] for n in early],
            [shards[1, n] for n in late])
    full = {n: wts[n] for n in REPLICATED}

    loss_part, grad_x, layer_grads, pieces = _local_step(x[0], p[:, 0], loss_target[0], gathered0, rest, full, True)
    loss = lax.psum(loss_part, MESH_AXES)

    rep = list(REPLICATED)
    rep_grads = [jnp.stack([layer_grads[i][n] for i in range(DEPTH)]).reshape(wts[n].shape) for n in rep]
    (gr,) = _run_job("grad_gather_replicated", _direct_gather_job([_pack(rep_grads, F32)]))

    kinds = ("grad", "delta", "new_m", "new_v")
    out = {}
    for n in sharded:
        local = [_as_rows(travel(n, pieces[i, n].reshape((-1,) + shards[i, n].shape)), 1) for i in range(DEPTH)]
        res = _adamw("adamw_" + n, local, _as_rows(wts[n], 1), _as_rows(ms[n], 1), _as_rows(vs[n], 1))
        for kind, arr in zip(kinds, res):
            out[kind + "_" + n] = arr.reshape(wts[n].shape)
    res = _adamw("adamw_replicated", [gr], _pack([wts[n] for n in rep], F32)[None],
                 _pack([ms[n] for n in rep], F32)[None], _pack([vs[n] for n in rep], F32)[None])
    shapes = [wts[n].shape for n in rep]
    for kind, buf in zip(kinds, res):
        for n, arr in zip(rep, _unpack(buf[0], shapes)):
            out[kind + "_" + n] = arr
    return (loss, grad_x[None], *[out["grad_" + n] for n in WEIGHTS], *[out["delta_" + n] for n in WEIGHTS],
            *[out["new_m_" + n] for n in WEIGHTS], *[out["new_v_" + n] for n in WEIGHTS])
```

```python
import functools
import math

import jax
import jax.numpy as jnp
from jax import lax
from jax.experimental import pallas as pl
from jax.experimental.pallas import tpu as pltpu

F32 = jnp.float32
BF16 = jnp.bfloat16

N_DEV = 8
MESH_AXES = ("x", "y", "c")
DEPTH = 2
D_MODEL = 1024
D_FF = 2816
BRANCH = 512
CHUNK = 64
GLA_HEADS = 4
GLA_DK = 64
GLA_DV = 128
GLA_LOWRANK = 16
GLA_TAU = 16.0
FOX_HEADS = 8
FOX_DH = 64
PLE_DIM = 256
LRU_C = 8.0
LRU_BLOCKS = 8
LN_EPS = 1e-5
RMS_EPS = 1e-6
ALPHA = (2 * DEPTH) ** 0.25
LANES = 128
NEG_BIG = -1e30

ADAM_LR = 0.001
ADAM_B1 = 0.9
ADAM_B2 = 0.999
ADAM_EPS = 1e-08
ADAM_WD = 0.01
ADAM_STEP = 10

VMEM_LIMIT_BYTES = 56 * 1024 * 1024

U_GATES = 0
U_AX = 3072
U_AY = 3584
U_BQ = 4096
U_BK = 4352
U_BV = 4608
U_BR = 5120
U_CQ = 5632
U_CK = 6144
U_CV = 6656
U_BLOW = 7168
U_CF = 7296
U_WIDTH = 7680
W_IN_SEGMENTS = (
    (0, 512, U_AX), (512, 512, U_AY), (1024, 256, U_BQ), (1280, 256, U_BK), (1536, 512, U_BV),
    (2048, 16, U_BLOW), (2064, 512, U_BR), (2576, 512, U_CQ), (3088, 512, U_CK), (3600, 512, U_CV),
    (4112, 8, U_CF), (4120, 3072, U_GATES),
)

SHARDED = (
    ("ffn1_w_up", 2), ("ffn1_w_down", 1), ("w_in", 2), ("conv_w", 2), ("gla_w_g2", 2), ("w_branch", 3),
    ("w_out", 1), ("ffn2_w_up", 2), ("ffn2_w_down", 1), ("ple_w_proj", 2), ("ple_w_gate", 1),
)
SHARDED_F32_GATHER = ("conv_w", "gla_w_g2")
COLUMN_SHARDED = ("ffn1_w_up", "ffn2_w_up", "w_in", "w_branch", "ple_w_proj")
REPLICATED = ("ln1_g", "ln1_b", "conv_b", "lru_wa", "lru_ba", "lru_wx", "lru_bx", "lru_lambda", "gla_b_g",
              "gla_norm_g", "fox_b_f", "ln2_g", "ln2_b", "ln3_g", "ln3_b", "ple_b_gate", "ln4_g", "ln4_b")
WEIGHTS = ("ffn1_w_up", "ffn1_w_down", "ln1_g", "ln1_b", "w_in", "conv_w", "conv_b", "lru_wa", "lru_ba", "lru_wx",
           "lru_bx", "lru_lambda", "gla_w_g2", "gla_b_g", "gla_norm_g", "fox_b_f", "w_branch", "w_out", "ln2_g",
           "ln2_b", "ffn2_w_up", "ffn2_w_down", "ln3_g", "ln3_b", "ple_w_proj", "ple_w_gate", "ple_b_gate", "ln4_g",
           "ln4_b")


def _sigmoid(x):
    return 1.0 / (1.0 + jnp.exp(-x))


def _log1p_pos(e):
    return jnp.where(e < 1e-4, e * (1.0 - 0.5 * e), jnp.log(1.0 + e))


def _softplus(x):
    return jnp.maximum(x, 0.0) + _log1p_pos(jnp.exp(-jnp.abs(x)))


def _log_sigmoid(x):
    return -_softplus(-x)


def _neg_expm1(y):
    series = -y * (1.0 + y * (0.5 + y * (1.0 / 6.0 + y * (1.0 / 24.0 + y * (1.0 / 120.0)))))
    return jnp.where(y > -0.1, series, 1.0 - jnp.exp(y))


def _silu_and_grad(x):
    s = _sigmoid(x)
    return x * s, s * (1.0 + x * (1.0 - s))


_GELU_C = math.sqrt(2.0 / math.pi)


def _gelu_and_grad(x):
    inner = _GELU_C * (x + 0.044715 * x * x * x)
    t = jnp.tanh(inner)
    g = 0.5 * x * (1.0 + t)
    dg = 0.5 * (1.0 + t) + 0.5 * x * (1.0 - t * t) * _GELU_C * (1.0 + 3.0 * 0.044715 * x * x)
    return g, dg


def _ln_stats(z):
    mu = jnp.mean(z, axis=-1, keepdims=True)
    zc = z - mu
    var = jnp.mean(zc * zc, axis=-1, keepdims=True)
    rstd = lax.rsqrt(var + LN_EPS)
    return zc * rstd, rstd


def _ln_fwd(z, g, b):
    xhat, _ = _ln_stats(z)
    return xhat * g + b


def _ln_bwd(dy, z, g):
    xhat, rstd = _ln_stats(z)
    dxh = dy * g
    m1 = jnp.mean(dxh, axis=-1, keepdims=True)
    m2 = jnp.mean(dxh * xhat, axis=-1, keepdims=True)
    return rstd * (dxh - m1 - xhat * m2), xhat


def _colsum(x):
    return jnp.sum(x, axis=0, keepdims=True)


def _dot(a, b, dims):
    dn = {"nn": (((1,), (0,)), ((), ())), "nt": (((1,), (1,)), ((), ())), "tn": (((0,), (0,)), ((), ()))}[dims]
    return lax.dot_general(a.astype(BF16), b.astype(BF16), dn, preferred_element_type=F32)


def _scan_rows(a, b, length, reverse=False, seg=None):
    rows = lax.broadcasted_iota(jnp.int32, b.shape, 0)
    span = seg if seg else length
    pos = rows % span if seg else rows
    d = 1
    while d < span:
        shift = (length - d) if reverse else d
        valid = (pos < span - d) if reverse else (pos >= d)
        sb = jnp.where(valid, pltpu.roll(b, shift, 0), 0.0)
        if a is None:
            b = b + sb
        else:
            b = b + a * sb
            a = a * jnp.where(valid, pltpu.roll(a, shift, 0), 1.0)
        d *= 2
    return a, b


def _tile(dim, pref):
    if dim <= pref:
        return dim
    best = None
    t = LANES
    while t <= pref:
        if dim % t == 0:
            best = t
        t += LANES
    assert best is not None, (dim, pref)
    return best


def _full_spec(arr):
    nd = arr.ndim
    return pl.BlockSpec(arr.shape, lambda *_: (0,) * nd)


def _mm(name, dims, a_ops, b_ops, terms, n_acc, epilogue, extras, out_dtypes, M, N, K, tm=512, tn=1024, tk=1024,
        side=None):
    tm, tn, tk = _tile(M, tm), _tile(N, tn), _tile(K, tk)
    gm, gn, gk = M // tm, N // tn, K // tk
    a_bytes = sum(a.size * a.dtype.itemsize for a in a_ops)
    b_bytes = sum(b.size * b.dtype.itemsize for b in b_ops)
    n_outer = gk == 1 and b_bytes + a_bytes * gn < a_bytes + b_bytes * gm

    def spec(shape, fn):
        if n_outer:
            return pl.BlockSpec(shape, lambda j, i, k: fn(i, j, k))
        return pl.BlockSpec(shape, fn)

    if dims == "tn":
        a_spec = spec((tk, tm), lambda i, j, k: (k, i))
    else:
        a_spec = spec((tm, tk), lambda i, j, k: (i, k))
    if dims == "nt":
        b_spec = spec((tn, tk), lambda i, j, k: (j, k))
    else:
        b_spec = spec((tk, tn), lambda i, j, k: (k, j))
    e_specs, e_arrays = [], []
    for ex in extras:
        if ex[1] == "mn":
            e_specs.append(spec((tm, tn), functools.partial(lambda i, j, k, off: (i, j + off), off=ex[2])))
        else:
            e_specs.append(spec((1, tn), lambda i, j, k: (0, j)))
        e_arrays.append(ex[0])
    na, nb, ne, no = len(a_ops), len(b_ops), len(extras), len(out_dtypes)
    grid = (gn, gm, gk) if n_outer else (gm, gn, gk)
    s_in, s_out, s_sems = _side_specs(side) if side is not None else ([], [], [])
    nsi, nso = len(s_in), len(s_out)
    total = grid[0] * grid[1] * grid[2]

    def body(*refs):
        a_refs = refs[:na]
        b_refs = refs[na:na + nb]
        e_refs = refs[na + nb:na + nb + ne]
        pos = na + nb + ne
        s_ins, o_refs = refs[pos:pos + nsi], refs[pos + nsi:pos + nsi + no]
        pos += nsi + no
        s_outs, acc_refs, sems = refs[pos:pos + nso], refs[pos + nso:pos + nso + n_acc], refs[pos + nso + n_acc:]
        k = pl.program_id(2)
        if side is not None:
            phases = side["phases"]
            triggers = [0, total - 1] if len(phases) == 2 else [0, total * 7 // 10, total - 1]
            flat = (pl.program_id(0) * grid[1] + pl.program_id(1)) * grid[2] + k
            for trigger, phase in zip(triggers[:-1], phases[:-1]):
                @pl.when(flat == trigger)
                def _(phase=phase):
                    phase(s_ins, s_outs, *sems)

        @pl.when(k == 0)
        def _():
            for acc in acc_refs:
                acc[...] = jnp.zeros_like(acc)

        for r, ai, bi in terms:
            acc_refs[r][...] += _dot(a_refs[ai][...], b_refs[bi][...], dims)

        @pl.when(k == gk - 1)
        def _():
            res = epilogue([acc[...] for acc in acc_refs], *[e[...] for e in e_refs])
            for o, val in zip(o_refs, res):
                o[...] = val.astype(o.dtype)

        if side is not None:
            @pl.when(flat == triggers[-1])
            def _():
                phases[-1](s_ins, s_outs, *sems)

    outs = pl.pallas_call(
        body,
        name=name,
        grid=grid,
        in_specs=[a_spec] * na + [b_spec] * nb + e_specs + s_in,
        out_specs=[spec((tm, tn), lambda i, j, k: (i, j))] * no + s_out,
        out_shape=[jax.ShapeDtypeStruct((M, N), dt) for dt in out_dtypes] + (side["out_shapes"] if side else []),
        scratch_shapes=[pltpu.VMEM((tm, tn), F32)] * n_acc + s_sems,
        compiler_params=pltpu.CompilerParams(
            dimension_semantics=("arbitrary",) * 3 if side is not None else ("parallel", "parallel", "arbitrary"),
            vmem_limit_bytes=VMEM_LIMIT_BYTES),
    )(*a_ops, *b_ops, *e_arrays, *(side["arrs"] if side else []))
    return outs if side is None else (outs[:no], outs[no:])


def _mm1(name, dims, a, b, M, N, K, out_dtype=F32, scale=None, **kw):
    def epi(accs):
        return [accs[0] if scale is None else accs[0] * scale]
    return _mm(name, dims, [a], [b], [(0, 0, 0)], 1, epi, [], [out_dtype], M, N, K, **kw)[0]


def _rowwise(name, fn, row_ins, vec_ins, row_outs, sum_outs, S, tr=512, reverse=False):
    tr = min(tr, S)
    g = S // tr
    rmap = (lambda i: (g - 1 - i)) if reverse else (lambda i: i)
    in_specs, arrays = [], []
    for r in row_ins:
        if isinstance(r, tuple):
            arr, width, blk = r
            in_specs.append(pl.BlockSpec((tr, width), functools.partial(lambda i, blk: (rmap(i), blk), blk=blk)))
        else:
            arr = r
            in_specs.append(pl.BlockSpec((tr, arr.shape[1]), lambda i: (rmap(i), 0)))
        arrays.append(arr)
    for v in vec_ins:
        in_specs.append(_full_spec(v))
        arrays.append(v)
    nr, nv, no, ns = len(row_ins), len(vec_ins), len(row_outs), len(sum_outs)

    def body(*refs):
        ins = [r[...] for r in refs[:nr + nv]]
        o_refs = refs[nr + nv:nr + nv + no]
        s_refs = refs[nr + nv + no:]
        outs, sums = fn(*ins)
        for o, val in zip(o_refs, outs):
            o[...] = val.astype(o.dtype)
        if ns:
            i = pl.program_id(0)

            @pl.when(i == 0)
            def _():
                for s, val in zip(s_refs, sums):
                    s[...] = val

            @pl.when(i > 0)
            def _():
                for s, val in zip(s_refs, sums):
                    s[...] += val

    res = pl.pallas_call(
        body,
        name=name,
        grid=(g,),
        in_specs=in_specs,
        out_specs=[pl.BlockSpec((tr, c), lambda i: (rmap(i), 0)) for c, _ in row_outs]
        + [pl.BlockSpec((1, c), lambda i: (0, 0)) for c in sum_outs],
        out_shape=[jax.ShapeDtypeStruct((S, c), dt) for c, dt in row_outs]
        + [jax.ShapeDtypeStruct((1, c), F32) for c in sum_outs],
        compiler_params=pltpu.CompilerParams(
            dimension_semantics=("arbitrary",), vmem_limit_bytes=VMEM_LIMIT_BYTES),
    )(*arrays)
    return res[:no], res[no:]


MESH_ID = pl.DeviceIdType.MESH


def _remote(src, dst, send_sem, recv_sem, to):
    return pltpu.make_async_remote_copy(src_ref=src, dst_ref=dst, send_sem=send_sem, recv_sem=recv_sem,
                                        device_id=to, device_id_type=MESH_ID)


def _hbm_call(name, body, arrs, out_shapes, n_send, n_recv, n_local):
    return pl.pallas_call(
        body,
        name=name,
        in_specs=[pl.BlockSpec(memory_space=pltpu.HBM)] * len(arrs),
        out_specs=[pl.BlockSpec(memory_space=pltpu.HBM)] * len(out_shapes),
        out_shape=out_shapes,
        scratch_shapes=[pltpu.SemaphoreType.DMA((n_send,)), pltpu.SemaphoreType.DMA((n_recv,)),
                        pltpu.SemaphoreType.DMA((n_local,))],
        compiler_params=pltpu.CompilerParams(has_side_effects=True),
    )(*arrs)


def _side_job(arrs, out_shapes, n_send, n_recv, n_local, phases):
    return dict(arrs=list(arrs), out_shapes=list(out_shapes), sems=(n_send, n_recv, n_local), phases=phases)


def _side_specs(side):
    hbm = pl.BlockSpec(memory_space=pltpu.HBM)
    sems = [pltpu.SemaphoreType.DMA((k,)) for k in side["sems"]]
    return [hbm] * len(side["arrs"]), [hbm] * len(side["out_shapes"]), sems


def _gather_job(arrs):
    n = len(arrs)

    def plan(ins, outs, send_sems, recv_sems, local_sems):
        x, y, c = lax.axis_index("x"), lax.axis_index("y"), lax.axis_index("c")
        me, sibling = (x, y, c), (x, y, 1 - c)
        chips = [(1 - x, y), (x, 1 - y), (1 - x, 1 - y)]

        def slot(i, dev):
            return outs[i].at[4 * dev[0] + 2 * dev[1] + dev[2]]

        def copy(i, k, block, to, src=None):
            dst = slot(i, block)
            return _remote(dst if src is None else src, dst, send_sems.at[7 * i + k], recv_sems.at[7 * i + k], to)

        mine = [pltpu.make_async_copy(ins[i], slot(i, me), local_sems.at[i]) for i in range(n)]
        first = []
        for i in range(n):
            first.append(copy(i, 0, me, sibling, src=ins[i]))
            first += [copy(i, 1 + j, me, (*chip, c), src=ins[i]) for j, chip in enumerate(chips)]
        arrive = [[copy(i, 1 + j, (*chip, c), me) for i in range(n)] for j, chip in enumerate(chips)]
        passed = [[copy(i, 4 + j, (*chip, c), sibling) for i in range(n)] for j, chip in enumerate(chips)]
        last = [copy(i, 0, sibling, me) for i in range(n)]
        last += [copy(i, 4 + j, (*chip, 1 - c), me) for i in range(n) for j, chip in enumerate(chips)]
        return mine, first, arrive, passed, last

    def start(*refs):
        mine, first, _, _, _ = plan(*refs)
        for cp in mine + first:
            cp.start()

    def forward(*refs):
        _, _, arrive, passed, _ = plan(*refs)
        for came, onward in zip(arrive, passed):
            for a, p in zip(came, onward):
                a.wait_recv()
                p.start()

    def finish(*refs):
        mine, first, _, passed, last = plan(*refs)
        for cp in last:
            cp.wait_recv()
        for cp in first + [p for onward in passed for p in onward]:
            cp.wait_send()
        for cp in mine:
            cp.wait()

    outs = [jax.ShapeDtypeStruct((N_DEV,) + a.shape, a.dtype) for a in arrs]
    return _side_job(arrs, outs, 7 * n, 7 * n, n, [start, forward, finish])


def _scatter_job(arrs):
    n = len(arrs)

    def plan(ins, outs, send_sems, recv_sems, local_sems):
        x, y, c = lax.axis_index("x"), lax.axis_index("y"), lax.axis_index("c")
        here = 2 * x + y
        local = [pltpu.make_async_copy(ins[i].at[here, c], outs[i].at[here, c], local_sems.at[i]) for i in range(n)]
        sends, recvs = [], []
        for i in range(n):
            for k in range(1, N_DEV):
                px = 1 - x if k & 4 else x
                py = 1 - y if k & 2 else y
                pc = 1 - c if k & 1 else c
                sems = (send_sems.at[7 * i + k - 1], recv_sems.at[7 * i + k - 1], (px, py, pc))
                sends.append(_remote(ins[i].at[2 * px + py, pc], outs[i].at[here, c], *sems))
                recvs.append(_remote(ins[i].at[2 * px + py, pc], outs[i].at[2 * px + py, pc], *sems))
        return local, sends, recvs

    def start(*refs):
        local, sends, _ = plan(*refs)
        for cp in local + sends:
            cp.start()

    def finish(*refs):
        local, sends, recvs = plan(*refs)
        for cp in recvs:
            cp.wait_recv()
        for cp in sends:
            cp.wait_send()
        for cp in local:
            cp.wait()

    outs = [jax.ShapeDtypeStruct(a.shape, a.dtype) for a in arrs]
    return _side_job(arrs, outs, 7 * n, 7 * n, n, [start, finish])


def _run_job(name, job):
    na, no = len(job["arrs"]), len(job["out_shapes"])

    def body(*refs):
        ins, outs, sems = refs[:na], refs[na:na + no], refs[na + no:]
        for phase in job["phases"]:
            phase(ins, outs, *sems)

    return _hbm_call(name, body, job["arrs"], job["out_shapes"], *job["sems"])


def _allgather_multi(name, arrs):
    return _run_job(name, _gather_job(arrs))


def _sibling_swap_multi(name, arrs):
    n = len(arrs)
    per = 4

    def body(*refs):
        ins, got = refs[:n], refs[n:2 * n]
        send_sems, recv_sems, _ = refs[2 * n:]
        x, y, c = lax.axis_index("x"), lax.axis_index("y"), lax.axis_index("c")
        sibling = (x, y, 1 - c)
        sends = []
        for i in range(n):
            for a in range(4):
                k = per * i + a
                sends.append(_remote(ins[i].at[a, 1 - c], got[i].at[a], send_sems.at[k], recv_sems.at[k], sibling))
        for cp in sends:
            cp.start()
        for cp in sends:
            cp.wait_recv()
        for cp in sends:
            cp.wait_send()

    outs = [jax.ShapeDtypeStruct((4,) + a.shape[2:], a.dtype) for a in arrs]
    return _hbm_call(name, body, arrs, outs, per * n, per * n, 1)


def _chip_job(arrs):
    n = len(arrs)

    def plan(ins, outs, send_sems, recv_sems, local_sems):
        x, y, c = lax.axis_index("x"), lax.axis_index("y"), lax.axis_index("c")
        mine = 2 * x + y
        chips = [(1 - x, y), (x, 1 - y), (1 - x, 1 - y)]
        local = [pltpu.make_async_copy(ins[i].at[mine], outs[i].at[mine], local_sems.at[i]) for i in range(n)]
        sends, recvs = [], []
        for i in range(n):
            for j, (px, py) in enumerate(chips):
                peer = 2 * px + py
                sems = (send_sems.at[3 * i + j], recv_sems.at[3 * i + j], (px, py, c))
                sends.append(_remote(ins[i].at[peer], outs[i].at[mine], *sems))
                recvs.append(_remote(ins[i].at[peer], outs[i].at[peer], *sems))
        return local, sends, recvs

    def start(*refs):
        local, sends, _ = plan(*refs)
        for cp in local + sends:
            cp.start()

    def finish(*refs):
        local, sends, recvs = plan(*refs)
        for cp in recvs:
            cp.wait_recv()
        for cp in sends:
            cp.wait_send()
        for cp in local:
            cp.wait()

    outs = [jax.ShapeDtypeStruct(a.shape, a.dtype) for a in arrs]
    return _side_job(arrs, outs, 3 * n, 3 * n, n, [start, finish])


def _as_rows(a, lead):
    return a.reshape(a.shape[:lead] + (-1, a.shape[-1]))


def _row_tile(rows, cols, parts):
    budget = 4 * 1024 * 1024 // (4 * max(cols, LANES) * parts)
    return _tile_rows(rows, max(8, min(512, budget // 8 * 8)))


def _pair_add(name, core, both, got):
    _, rows, cols = got.shape
    tr = _row_tile(rows, cols, 2)

    def body(c_ref, a_ref, b_ref, o_ref):
        o_ref[...] = (a_ref[...] + b_ref[...]).astype(o_ref.dtype)

    blk = pl.BlockSpec((1, tr, cols), lambda ch, i, c_ref: (ch, i, 0))
    return pl.pallas_call(
        body, name=name,
        grid_spec=pltpu.PrefetchScalarGridSpec(
            num_scalar_prefetch=1, grid=(4, rows // tr),
            in_specs=[pl.BlockSpec((1, None, tr, cols), lambda ch, i, c_ref: (ch, c_ref[0], i, 0)), blk],
            out_specs=blk),
        out_shape=jax.ShapeDtypeStruct(got.shape, BF16),
        compiler_params=pltpu.CompilerParams(dimension_semantics=("parallel", "parallel"),
                                             vmem_limit_bytes=VMEM_LIMIT_BYTES),
    )(core, both, got)


def _adamw(name, gparts, w, m, v):
    layers = len(gparts)
    _, rows, cols = gparts[0].shape
    tr = _row_tile(rows, cols, sum(gp.shape[0] for gp in gparts))
    c1 = 1.0 / (1.0 - ADAM_B1 ** ADAM_STEP)
    c2 = 1.0 / (1.0 - ADAM_B2 ** ADAM_STEP)

    def body(*refs):
        gp_refs = refs[:layers]
        w_ref, m_ref, v_ref, g_ref, d_ref, nm_ref, nv_ref = refs[layers:]
        layer = pl.program_id(0)
        g = None
        for k, gp_ref in enumerate(gp_refs):
            gk = gp_ref[0].astype(F32)
            for i in range(1, gp_ref.shape[0]):
                gk = gk + gp_ref[i].astype(F32)
            g = gk if g is None else jnp.where(layer == k, gk, g)
        nm = ADAM_B1 * m_ref[...] + (1.0 - ADAM_B1) * g
        nv = ADAM_B2 * v_ref[...] + (1.0 - ADAM_B2) * (g * g)
        m_hat = nm * c1
        v_hat = nv * c2
        g_ref[...] = g
        nm_ref[...] = nm
        nv_ref[...] = nv
        d_ref[...] = -ADAM_LR * (m_hat / (jnp.sqrt(v_hat) + ADAM_EPS) + ADAM_WD * w_ref[...])

    row = pl.BlockSpec((None, tr, cols), lambda l, i: (l, i, 0))
    return pl.pallas_call(
        body,
        name=name,
        grid=(layers, rows // tr),
        in_specs=[pl.BlockSpec((gp.shape[0], tr, cols), lambda l, i: (0, i, 0)) for gp in gparts] + [row, row, row],
        out_specs=[row] * 4,
        out_shape=[jax.ShapeDtypeStruct((layers, rows, cols), F32)] * 4,
        compiler_params=pltpu.CompilerParams(dimension_semantics=("parallel", "parallel"),
                                             vmem_limit_bytes=VMEM_LIMIT_BYTES),
    )(*gparts, w, m, v)


def _tile_rows(rows, pref):
    t = min(pref, rows) // 8 * 8
    while t >= 8 and rows % t:
        t -= 8
    return t if t >= 8 else rows


PACK_ROWS = 512


def _pack(arrs, dtype):
    flat = jnp.concatenate([a.astype(dtype).reshape(-1) for a in arrs])
    quantum = PACK_ROWS * LANES
    padded = -(-flat.shape[0] // quantum) * quantum
    return jnp.pad(flat, (0, padded - flat.shape[0])).reshape(-1, LANES)


def _unpack(buf, shapes, lead=()):
    flat = buf.reshape(lead + (-1,))
    out, off = [], 0
    for shp in shapes:
        n = math.prod(shp)
        out.append(flat[..., off:off + n].reshape(lead + tuple(shp)))
        off += n
    return out


def _dest_pieces(name, g):
    if name == "w_branch":
        return jnp.moveaxis(g.reshape(3, 4, 2, D_MODEL // N_DEV, BRANCH), 0, 2)
    if name in SHARDED_F32_GATHER:
        return jnp.moveaxis(g.reshape(g.shape[0], 4, 2, -1), 0, 2)
    return g.reshape((4, 2, g.shape[0] // N_DEV) + g.shape[1:])


HALO = 8
LRU_TILE = 256
CONV_TILE = 512


def _rows_down(x, prev, k):
    xs = pltpu.roll(x, k, 0)
    row = lax.broadcasted_iota(jnp.int32, prev.shape, 0)
    top = jnp.where(row < k, pltpu.roll(prev, k, 0), xs[:HALO])
    return jnp.concatenate([top, xs[HALO:]], axis=0)


def _rows_up(x, nxt, k):
    rows = x.shape[0]
    xs = pltpu.roll(x, rows - k, 0)
    row = lax.broadcasted_iota(jnp.int32, nxt.shape, 0)
    bottom = jnp.where(row >= HALO - k, pltpu.roll(nxt, HALO - k, 0), xs[rows - HALO:])
    return jnp.concatenate([xs[:rows - HALO], bottom], axis=0)


def _halo_before(T, block_of, col=0):
    per = T // HALO
    return pl.BlockSpec((HALO, BRANCH), lambda t: (jnp.maximum(block_of(t) * per - 1, 0), col))


def _halo_after(T, block_of, S, col=0):
    per = T // HALO
    return pl.BlockSpec((HALO, BRANCH), lambda t: (jnp.minimum((block_of(t) + 1) * per, S // HALO - 1), col))


def _lru_fwd(u, lw, S):
    T = min(LRU_TILE, S)
    nb = S // T
    row = pl.BlockSpec((T, BRANCH), lambda t: (t, 0))
    vecs = [lw["cw0"], lw["cw1"], lw["cw2"], lw["cw3"], lw["conv_b"], lw["wa"], lw["wx"], lw["ba"], lw["bx"],
            lw["lam"]]

    def body(ax, ax_before, ay, cw0, cw1, cw2, cw3, cb, wa, wx, ba, bx, lam, xc_o, r_o, i_o, a_o, h_o, ya_o, hc):
        t = pl.program_id(0)

        @pl.when(t == 0)
        def _():
            hc[...] = jnp.zeros_like(hc)

        x = ax[...]
        before = jnp.where(t == 0, 0.0, ax_before[...])
        xc = (cw3[...] * x + cw2[...] * _rows_down(x, before, 1) + cw1[...] * _rows_down(x, before, 2)
              + cw0[...] * _rows_down(x, before, 3) + cb[...])
        r = _sigmoid(_dot(xc, wa[...], "nn") + ba[...])
        gi = _sigmoid(_dot(xc, wx[...], "nn") + bx[...])
        sp = _softplus(-lam[...])
        la = -LRU_C * r * sp
        a = jnp.exp(la)
        mult = jnp.sqrt(_neg_expm1(2.0 * la))
        A, B = _scan_rows(a, mult * gi * xc, T)
        h = B + A * hc[...]
        h_o[...] = h
        hc[...] = h_o[pl.ds(T - 1, 1), :]
        xc_o[...] = xc
        r_o[...] = r
        i_o[...] = gi
        a_o[...] = a
        gy, _ = _gelu_and_grad(ay[...])
        ya_o[...] = (gy * h).astype(ya_o.dtype)

    outs = pl.pallas_call(
        body,
        name="lru_fwd",
        grid=(nb,),
        in_specs=[pl.BlockSpec((T, BRANCH), lambda t: (t, U_AX // BRANCH)),
                  _halo_before(T, lambda t: t, U_AX // BRANCH),
                  pl.BlockSpec((T, BRANCH), lambda t: (t, U_AY // BRANCH))] + [_full_spec(v) for v in vecs],
        out_specs=[row] * 6,
        out_shape=[jax.ShapeDtypeStruct((S, BRANCH), F32)] * 5 + [jax.ShapeDtypeStruct((S, BRANCH), BF16)],
        scratch_shapes=[pltpu.VMEM((1, BRANCH), F32)],
        compiler_params=pltpu.CompilerParams(dimension_semantics=("arbitrary",), vmem_limit_bytes=VMEM_LIMIT_BYTES),
    )(u, u, u, *vecs)
    return outs


def _lru_bwd(dya, u, sv, lw, S):
    T = min(LRU_TILE, S)
    nb = S // T
    rrow = pl.BlockSpec((T, BRANCH), lambda t: (nb - 1 - t, 0))
    sq = pl.BlockSpec((BRANCH, BRANCH), lambda t: (0, 0))
    vrow = pl.BlockSpec((1, BRANCH), lambda t: (0, 0))

    def block(t):
        return nb - 1 - t

    def body(dya_r, ay, h, h_before, xc_r, r_r, i_r, a_r, a_after, wa, wx, lam,
             day_o, dxc_o, dwa_o, dwx_o, dba_o, dbx_o, dlam_o, lcar, tmp):
        t = pl.program_id(0)
        h_prev = _rows_down(h[...], jnp.where(t == nb - 1, 0.0, h_before[...]), 1)
        a_next = _rows_up(a_r[...], jnp.where(t == 0, 0.0, a_after[...]), 1)

        @pl.when(t == 0)
        def _():
            lcar[...] = jnp.zeros_like(lcar)
            dwa_o[...] = jnp.zeros_like(dwa_o)
            dwx_o[...] = jnp.zeros_like(dwx_o)
            dba_o[...] = jnp.zeros_like(dba_o)
            dbx_o[...] = jnp.zeros_like(dbx_o)
            dlam_o[...] = jnp.zeros_like(dlam_o)

        gy, dgy = _gelu_and_grad(ay[...])
        dy = dya_r[...]
        day_o[...] = (dy * h[...] * dgy).astype(day_o.dtype)
        A, B = _scan_rows(a_next, dy * gy, T, reverse=True)
        lmb = B + A * lcar[...]
        tmp[...] = lmb
        lcar[...] = tmp[pl.ds(0, 1), :]
        xc, r, gi, a = xc_r[...], r_r[...], i_r[...], a_r[...]
        sp = _softplus(-lam[...])
        la = -LRU_C * r * sp
        mult = jnp.sqrt(_neg_expm1(2.0 * la))
        da = lmb * h_prev
        dmult = lmb * gi * xc
        di = lmb * mult * xc
        dxc = lmb * mult * gi
        dla = da * a - dmult * a * a / mult
        dr = dla * (-LRU_C * sp)
        dlam_o[...] += _colsum(dla * (LRU_C * r)) * _sigmoid(-lam[...])
        dpr = dr * r * (1.0 - r)
        dpi = di * gi * (1.0 - gi)
        dba_o[...] += _colsum(dpr)
        dbx_o[...] += _colsum(dpi)
        dxc_o[...] = dxc + _dot(dpr, wa[...], "nt") + _dot(dpi, wx[...], "nt")
        dwa_o[...] += _dot(xc, dpr, "tn")
        dwx_o[...] += _dot(xc, dpi, "tn")

    outs = pl.pallas_call(
        body,
        name="lru_bwd",
        grid=(nb,),
        in_specs=[rrow, pl.BlockSpec((T, BRANCH), lambda t: (nb - 1 - t, U_AY // BRANCH)), rrow,
                  _halo_before(T, block), rrow, rrow, rrow, rrow, _halo_after(T, block, S), sq, sq, vrow],
        out_specs=[rrow, rrow, sq, sq, vrow, vrow, vrow],
        out_shape=[jax.ShapeDtypeStruct((S, BRANCH), BF16), jax.ShapeDtypeStruct((S, BRANCH), F32),
                   jax.ShapeDtypeStruct((BRANCH, BRANCH), F32), jax.ShapeDtypeStruct((BRANCH, BRANCH), F32),
                   jax.ShapeDtypeStruct((1, BRANCH), F32), jax.ShapeDtypeStruct((1, BRANCH), F32),
                   jax.ShapeDtypeStruct((1, BRANCH), F32)],
        scratch_shapes=[pltpu.VMEM((1, BRANCH), F32), pltpu.VMEM((T, BRANCH), F32)],
        compiler_params=pltpu.CompilerParams(dimension_semantics=("arbitrary",), vmem_limit_bytes=VMEM_LIMIT_BYTES),
    )(dya, u, sv["h"], sv["h"], sv["xc"], sv["r"], sv["i"], sv["a"], sv["a"], lw["wa"], lw["wx"], lw["lam"])
    return outs


def _conv_bwd(dxc, u, lw, S):
    T = min(CONV_TILE, S)
    nb = S // T
    vecs = [lw["cw0"], lw["cw1"], lw["cw2"], lw["cw3"]]
    vrow = pl.BlockSpec((1, BRANCH), lambda t: (0, 0))

    def body(d_r, d_after, ax, ax_before, cw0, cw1, cw2, cw3, dax_o, dcw0_o, dcw1_o, dcw2_o, dcw3_o, dcb_o):
        t = pl.program_id(0)
        d = d_r[...]
        after = jnp.where(t == nb - 1, 0.0, d_after[...])
        x = ax[...]
        before = jnp.where(t == 0, 0.0, ax_before[...])
        dax = (cw3[...] * d + cw2[...] * _rows_up(d, after, 1) + cw1[...] * _rows_up(d, after, 2)
               + cw0[...] * _rows_up(d, after, 3))
        dax_o[...] = dax.astype(dax_o.dtype)
        sums = [_colsum(d * _rows_down(x, before, 3)), _colsum(d * _rows_down(x, before, 2)),
                _colsum(d * _rows_down(x, before, 1)), _colsum(d * x), _colsum(d)]
        outs = [dcw0_o, dcw1_o, dcw2_o, dcw3_o, dcb_o]

        @pl.when(t == 0)
        def _():
            for o, val in zip(outs, sums):
                o[...] = val

        @pl.when(t > 0)
        def _():
            for o, val in zip(outs, sums):
                o[...] += val

    res = pl.pallas_call(
        body,
        name="conv_bwd",
        grid=(nb,),
        in_specs=[pl.BlockSpec((T, BRANCH), lambda t: (t, 0)), _halo_after(T, lambda t: t, S),
                  pl.BlockSpec((T, BRANCH), lambda t: (t, U_AX // BRANCH)),
                  _halo_before(T, lambda t: t, U_AX // BRANCH)] + [_full_spec(v) for v in vecs],
        out_specs=[pl.BlockSpec((T, BRANCH), lambda t: (t, 0))] + [vrow] * 5,
        out_shape=[jax.ShapeDtypeStruct((S, BRANCH), BF16)] + [jax.ShapeDtypeStruct((1, BRANCH), F32)] * 5,
        compiler_params=pltpu.CompilerParams(dimension_semantics=("arbitrary",), vmem_limit_bytes=VMEM_LIMIT_BYTES),
    )(dxc, dxc, u, u, *vecs)
    return res[0], res[1:]


GLA_QK = GLA_HEADS * GLA_DK
GLA_V = GLA_HEADS * GLA_DV
GLA_SCALE = GLA_DK ** -0.5


def _gla_specs(TB, rev_nb=None):
    def rmap(t):
        return t if rev_nb is None else rev_nb - 1 - t
    return [
        pl.BlockSpec((TB, GLA_QK), lambda t: (rmap(t), U_BQ // GLA_QK)),
        pl.BlockSpec((TB, GLA_QK), lambda t: (rmap(t), U_BK // GLA_QK)),
        pl.BlockSpec((TB, GLA_V), lambda t: (rmap(t), U_BV // GLA_V)),
        pl.BlockSpec((TB, GLA_V), lambda t: (rmap(t), U_BR // GLA_V)),
        pl.BlockSpec((TB, LANES), lambda t: (rmap(t), U_BLOW // LANES)),
    ]


def _gla_gates(gl, wg2, bg, TB):
    pre = _dot(gl, wg2, "nn") + bg
    la = _log_sigmoid(pre) * (1.0 / GLA_TAU)
    _, gc = _scan_rows(None, la, TB, seg=CHUNK)
    return pre, la, gc


def _gla_fwd(u, gw, S):
    TB = min(512, S)
    nb = S // TB
    cpb = TB // CHUNK
    vecs = [gw["wg2"], gw["bg"], gw["ng"], gw["bd"]]

    def body(q_r, k_r, v_r, br_r, gl_r, wg2, bg, ng, bd, yb_o, oraw_o, st_o, st):
        t = pl.program_id(0)

        @pl.when(t == 0)
        def _():
            st[...] = jnp.zeros_like(st)

        _, la, gc = _gla_gates(gl_r[...], wg2[...], bg[...], TB)
        for c in range(cpb):
            sl = slice(c * CHUNK, (c + 1) * CHUNK)
            gt = _colsum(la[sl])
            kdec = k_r[sl, :] * jnp.exp(gt - gc[sl])
            d_t = _dot(v_r[sl, :], kdec, "tn") * bd[...]
            s_new = st[...] * jnp.exp(gt) + d_t
            st[...] = s_new
            st_o[c] = s_new
            oraw_o[sl, :] = _dot(q_r[sl, :] * GLA_SCALE, s_new, "nt")
        for h in range(GLA_HEADS):
            hs = slice(h * GLA_DV, (h + 1) * GLA_DV)
            oh = oraw_o[:, hs]
            on = oh * lax.rsqrt(jnp.mean(oh * oh, axis=-1, keepdims=True) + RMS_EPS)
            sil, _ = _silu_and_grad(br_r[:, hs])
            yb_o[:, hs] = (on * ng[:, hs] * sil).astype(yb_o.dtype)

    return pl.pallas_call(
        body,
        name="gla_fwd",
        grid=(nb,),
        in_specs=_gla_specs(TB) + [_full_spec(v) for v in vecs],
        out_specs=[pl.BlockSpec((TB, GLA_V), lambda t: (t, 0)), pl.BlockSpec((TB, GLA_V), lambda t: (t, 0)),
                   pl.BlockSpec((cpb, GLA_V, GLA_QK), lambda t: (t, 0, 0))],
        out_shape=[jax.ShapeDtypeStruct((S, GLA_V), BF16), jax.ShapeDtypeStruct((S, GLA_V), F32),
                   jax.ShapeDtypeStruct((S // CHUNK, GLA_V, GLA_QK), F32)],
        scratch_shapes=[pltpu.VMEM((GLA_V, GLA_QK), F32)],
        compiler_params=pltpu.CompilerParams(dimension_semantics=("arbitrary",), vmem_limit_bytes=VMEM_LIMIT_BYTES),
    )(u, u, u, u, u, *vecs)


def _gla_bwd(dyb, u, oraw, states, gw, S):
    TB = min(512, S)
    nb = S // TB
    cpb = TB // CHUNK
    vecs = [gw["wg2"], gw["bg"], gw["ng"], gw["bd"]]

    def rrow(width):
        return pl.BlockSpec((TB, width), lambda t: (nb - 1 - t, 0))

    def body(dyb_r, oraw_r, q_r, k_r, v_r, br_r, gl_r, st_r, sp_r, wg2, bg, ng, bd,
             dq_o, dk_o, dv_o, dbr_o, dgl_o, dwg2_o, dbg_o, dng_o, dcar, do_buf, dla_buf):
        t = pl.program_id(0)
        blk = nb - 1 - t

        @pl.when(t == 0)
        def _():
            dcar[...] = jnp.zeros_like(dcar)
            dwg2_o[...] = jnp.zeros_like(dwg2_o)
            dbg_o[...] = jnp.zeros_like(dbg_o)
            dng_o[...] = jnp.zeros_like(dng_o)

        pre, la, gc = _gla_gates(gl_r[...], wg2[...], bg[...], TB)
        for h in range(GLA_HEADS):
            hs = slice(h * GLA_DV, (h + 1) * GLA_DV)
            oh = oraw_r[:, hs]
            rs = lax.rsqrt(jnp.mean(oh * oh, axis=-1, keepdims=True) + RMS_EPS)
            on = oh * rs
            sil, dsil = _silu_and_grad(br_r[:, hs])
            dy = dyb_r[:, hs]
            dbr_o[:, hs] = (dy * on * ng[:, hs] * dsil).astype(dbr_o.dtype)
            don = dy * ng[:, hs] * sil
            dng_o[:, hs] += _colsum(dy * on * sil)
            do_buf[:, hs] = rs * (don - on * jnp.mean(don * on, axis=-1, keepdims=True))
        first = jnp.where(blk == 0, 0.0, 1.0)
        for c in reversed(range(cpb)):
            sl = slice(c * CHUNK, (c + 1) * CHUNK)
            s_n = st_r[c]
            s_prev = st_r[c - 1] if c > 0 else sp_r[0] * first
            gt = _colsum(la[sl])
            w = jnp.exp(gt - gc[sl])
            k_c = k_r[sl, :]
            kdec = k_c * w
            qs = q_r[sl, :] * GLA_SCALE
            do_c = do_buf[sl, :]
            dq_o[sl, :] = (_dot(do_c, s_n, "nn") * GLA_SCALE).astype(dq_o.dtype)
            d_n = _dot(do_c, qs, "tn") * bd[...] + dcar[...]
            dv_o[sl, :] = _dot(kdec, d_n, "nt").astype(dv_o.dtype)
            dkdec = _dot(v_r[sl, :], d_n, "nn")
            dk_o[sl, :] = (dkdec * w).astype(dk_o.dtype)
            tt = dkdec * kdec
            e = jnp.exp(gt)
            dgt = _colsum(tt) + _colsum(d_n * s_prev) * e
            _, rc = _scan_rows(None, -tt, CHUNK, reverse=True)
            dla_buf[sl, :] = rc + dgt
            dcar[...] = d_n * e
        dpre = dla_buf[...] * _sigmoid(-pre) * (1.0 / GLA_TAU)
        dbg_o[...] += _colsum(dpre)
        dgl_o[...] = _dot(dpre, wg2[...], "nt").astype(dgl_o.dtype)
        dwg2_o[...] += _dot(gl_r[...], dpre, "tn")

    return pl.pallas_call(
        body,
        name="gla_bwd",
        grid=(nb,),
        in_specs=[rrow(GLA_V), rrow(GLA_V)] + _gla_specs(TB, rev_nb=nb)
        + [pl.BlockSpec((cpb, GLA_V, GLA_QK), lambda t: (nb - 1 - t, 0, 0)),
           pl.BlockSpec((1, GLA_V, GLA_QK), lambda t: (jnp.maximum((nb - 1 - t) * cpb - 1, 0), 0, 0))]
        + [_full_spec(v) for v in vecs],
        out_specs=[rrow(GLA_QK), rrow(GLA_QK), rrow(GLA_V), rrow(GLA_V), rrow(LANES),
                   pl.BlockSpec((LANES, GLA_QK), lambda t: (0, 0)), pl.BlockSpec((1, GLA_QK), lambda t: (0, 0)),
                   pl.BlockSpec((1, GLA_V), lambda t: (0, 0))],
        out_shape=[jax.ShapeDtypeStruct((S, GLA_QK), BF16), jax.ShapeDtypeStruct((S, GLA_QK), BF16),
                   jax.ShapeDtypeStruct((S, GLA_V), BF16), jax.ShapeDtypeStruct((S, GLA_V), BF16),
                   jax.ShapeDtypeStruct((S, LANES), BF16), jax.ShapeDtypeStruct((LANES, GLA_QK), F32),
                   jax.ShapeDtypeStruct((1, GLA_QK), F32), jax.ShapeDtypeStruct((1, GLA_V), F32)],
        scratch_shapes=[pltpu.VMEM((GLA_V, GLA_QK), F32), pltpu.VMEM((TB, GLA_V), F32),
                        pltpu.VMEM((TB, GLA_QK), F32)],
        compiler_params=pltpu.CompilerParams(dimension_semantics=("arbitrary",), vmem_limit_bytes=VMEM_LIMIT_BYTES),
    )(dyb, oraw, u, u, u, u, u, states, states, *vecs)


FOX_SCALE = FOX_DH ** -0.5


def _fox_gate_fwd(u, bfp, S):
    T = min(512, S)

    def body(f_r, b_r, fc_o, car):
        t = pl.program_id(0)

        @pl.when(t == 0)
        def _():
            car[...] = jnp.zeros_like(car)

        _, cs = _scan_rows(None, _log_sigmoid(f_r[...] + b_r[...]), T)
        fc_o[...] = cs + car[...]
        car[...] = fc_o[pl.ds(T - 1, 1), :]

    return pl.pallas_call(
        body,
        name="fox_gate_fwd",
        grid=(S // T,),
        in_specs=[pl.BlockSpec((T, LANES), lambda t: (t, U_CF // LANES)), _full_spec(bfp)],
        out_specs=pl.BlockSpec((T, LANES), lambda t: (t, 0)),
        out_shape=jax.ShapeDtypeStruct((S, LANES), F32),
        scratch_shapes=[pltpu.VMEM((1, LANES), F32)],
        compiler_params=pltpu.CompilerParams(dimension_semantics=("arbitrary",), vmem_limit_bytes=VMEM_LIMIT_BYTES),
    )(u, bfp)


def _fox_gate_bwd(dfc, u, bfp, S):
    T = min(512, S)
    nb = S // T

    def body(d_r, f_r, b_r, df_o, db_o, car, tmp):
        t = pl.program_id(0)

        @pl.when(t == 0)
        def _():
            car[...] = jnp.zeros_like(car)
            db_o[...] = jnp.zeros_like(db_o)

        _, rc = _scan_rows(None, d_r[...], T, reverse=True)
        tmp[...] = rc + car[...]
        car[...] = tmp[pl.ds(0, 1), :]
        df = tmp[...] * _sigmoid(-(f_r[...] + b_r[...]))
        df_o[...] = df.astype(df_o.dtype)
        db_o[...] += _colsum(df)

    return pl.pallas_call(
        body,
        name="fox_gate_bwd",
        grid=(nb,),
        in_specs=[pl.BlockSpec((T, LANES), lambda t: (nb - 1 - t, 0)),
                  pl.BlockSpec((T, LANES), lambda t: (nb - 1 - t, U_CF // LANES)), _full_spec(bfp)],
        out_specs=[pl.BlockSpec((T, LANES), lambda t: (nb - 1 - t, 0)), pl.BlockSpec((1, LANES), lambda t: (0, 0))],
        out_shape=[jax.ShapeDtypeStruct((S, LANES), BF16), jax.ShapeDtypeStruct((1, LANES), F32)],
        scratch_shapes=[pltpu.VMEM((1, LANES), F32), pltpu.VMEM((T, LANES), F32)],
        compiler_params=pltpu.CompilerParams(dimension_semantics=("arbitrary",), vmem_limit_bytes=VMEM_LIMIT_BYTES),
    )(dfc, u, bfp)


def _fox_call(name, body, tables, grid, in_specs, out_specs, out_shape, scratch, args, side):
    n_in, n_out, n_scr = len(in_specs), len(out_specs), len(scratch)
    semantics = ("parallel", "arbitrary")
    if side is not None:
        total = grid[0] * grid[1]
        phases = side["phases"]
        triggers = [0, total - 1] if len(phases) == 2 else [0, total * 7 // 10, total - 1]
        na, no = len(side["arrs"]), len(side["out_shapes"])
        s_in, s_out, s_sems = _side_specs(side)
        kernel_body = body

        def body(*refs):
            tabs, rest = refs[:len(tables)], refs[len(tables):]
            ins, s_ins = rest[:n_in], rest[n_in:n_in + na]
            rest = rest[n_in + na:]
            outs, s_outs = rest[:n_out], rest[n_out:n_out + no]
            rest = rest[n_out + no:]
            scr, sems = rest[:n_scr], rest[n_scr:]
            flat = pl.program_id(0) * grid[1] + pl.program_id(1)
            for trigger, phase in zip(triggers[:-1], phases[:-1]):
                @pl.when(flat == trigger)
                def _(phase=phase):
                    phase(s_ins, s_outs, *sems)
            kernel_body(*tabs, *ins, *outs, *scr)

            @pl.when(flat == triggers[-1])
            def _():
                phases[-1](s_ins, s_outs, *sems)

        in_specs, out_specs = in_specs + s_in, out_specs + s_out
        out_shape, scratch = out_shape + side["out_shapes"], scratch + s_sems
        args = list(args) + side["arrs"]
        semantics = ("arbitrary", "arbitrary")
    res = pl.pallas_call(
        body,
        name=name,
        grid_spec=pltpu.PrefetchScalarGridSpec(num_scalar_prefetch=len(tables), grid=grid, in_specs=in_specs,
                                               out_specs=out_specs, scratch_shapes=scratch),
        out_shape=out_shape,
        compiler_params=pltpu.CompilerParams(dimension_semantics=semantics, vmem_limit_bytes=VMEM_LIMIT_BYTES),
    )(*tables, *args)
    return res[:n_out], res[n_out:]


FOX_TILE = 1024
FOX_GROUP = 2
FOX_GROUP_FWD = 4
FOX_AUG = 128
FOX_ONES = 3


def _fox_pairs(n, by_key):
    pairs = [(qi, ki) for qi in range(n) for ki in range(qi + 1)]
    if by_key:
        pairs.sort(key=lambda qk: (qk[1], qk[0]))
    qs = jnp.asarray([qk[0] for qk in pairs], jnp.int32)
    ks = jnp.asarray([qk[1] for qk in pairs], jnp.int32)
    return qs, ks


def _fox_causal(sT):
    keys = lax.broadcasted_iota(jnp.int32, sT.shape, 0)
    queries = lax.broadcasted_iota(jnp.int32, sT.shape, 1)
    return jnp.where(keys <= queries, sT, NEG_BIG)


def _fox_fwd(qT, ka, vT, S, side=None):
    t = min(FOX_TILE, S)
    n = S // t
    qi_tab, ki_tab = _fox_pairs(n, by_key=False)

    G = FOX_GROUP_FWD

    def body(qi_ref, ki_ref, qT_r, ka_r, vT_r, oT_o, lse_o, m_s, l_s, acc):
        step = pl.program_id(1)
        qi, ki = qi_ref[step], ki_ref[step]

        @pl.when(ki == 0)
        def _():
            m_s[...] = jnp.full_like(m_s, NEG_BIG)
            l_s[...] = jnp.zeros_like(l_s)
            acc[...] = jnp.zeros_like(acc)

        def update(g, masked):
            sT = _dot(ka_r[g], qT_r[g], "nn")
            if masked:
                sT = _fox_causal(sT)
            m_new = jnp.maximum(m_s[g], jnp.max(sT, axis=0, keepdims=True))
            p = jnp.exp(sT - m_new)
            alpha = jnp.exp(m_s[g] - m_new)
            l_s[g] = alpha * l_s[g] + jnp.sum(p, axis=0, keepdims=True)
            acc[g] = alpha * acc[g] + _dot(vT_r[g], p, "nn")
            m_s[g] = m_new

        @pl.when(ki < qi)
        def _():
            for g in range(G):
                update(g, False)

        @pl.when(ki == qi)
        def _():
            for g in range(G):
                update(g, True)
                oT_o[g] = acc[g] / l_s[g]
                lse_o[g] = m_s[g] + jnp.log(l_s[g])

    return _fox_call(
        "fox_fwd", body, (qi_tab, ki_tab), (FOX_HEADS // G, int(qi_tab.shape[0])),
        [pl.BlockSpec((G, FOX_AUG, t), lambda h, s, qt, kt: (h, 0, qt[s])),
         pl.BlockSpec((G, t, FOX_AUG), lambda h, s, qt, kt: (h, kt[s], 0)),
         pl.BlockSpec((G, FOX_DH, t), lambda h, s, qt, kt: (h, 0, kt[s]))],
        [pl.BlockSpec((G, FOX_DH, t), lambda h, s, qt, kt: (h, 0, qt[s])),
         pl.BlockSpec((G, 1, t), lambda h, s, qt, kt: (h, 0, qt[s]))],
        [jax.ShapeDtypeStruct((FOX_HEADS, FOX_DH, S), F32), jax.ShapeDtypeStruct((FOX_HEADS, 1, S), F32)],
        [pltpu.VMEM((G, 1, t), F32), pltpu.VMEM((G, 1, t), F32), pltpu.VMEM((G, FOX_DH, t), F32)],
        (qT, ka, vT), side)


FOX_BIAS_ROWS = 8


def _fox_bwd(qT, qa, ka, kT, v, do, doT, oT, lse, S, side=None):
    t = min(FOX_TILE, S)
    n = S // t
    qi_tab, ki_tab = _fox_pairs(n, by_key=True)
    n_steps = int(qi_tab.shape[0])
    slab = slice(FOX_DH, FOX_DH + FOX_BIAS_ROWS)

    G = FOX_GROUP

    def body(qi_ref, ki_ref, qT_r, qa_r, ka_r, kT_r, v_r, do_r, doT_r, oT_r, lse_r,
             dq_o, dfq_o, dk_o, dfk_o, dv_o, dq_acc, dk_acc, dv_acc):
        step = pl.program_id(1)
        qi, ki = qi_ref[step], ki_ref[step]

        @pl.when(step == 0)
        def _():
            dq_acc[...] = jnp.zeros_like(dq_acc)

        @pl.when(qi == ki)
        def _():
            dk_acc[...] = jnp.zeros_like(dk_acc)
            dv_acc[...] = jnp.zeros_like(dv_acc)

        def update(g, masked):
            sT = _dot(ka_r[g], qT_r[g], "nn")
            if masked:
                sT = _fox_causal(sT)
            pT = jnp.exp(sT - lse_r[g])
            delta = jnp.sum(oT_r[g] * doT_r[g], axis=0, keepdims=True)
            dsT = pT * (_dot(v_r[g], doT_r[g], "nn") - delta)
            dv_acc[g] += _dot(pT, do_r[g], "nn")
            dk_acc[g] += _dot(dsT, qa_r[g], "nn")
            dq_acc[g, qi] += _dot(kT_r[g], dsT, "nn")

        @pl.when(qi > ki)
        def _():
            for g in range(G):
                update(g, False)

        @pl.when(qi == ki)
        def _():
            for g in range(G):
                update(g, True)

        @pl.when(qi == n - 1)
        def _():
            for g in range(G):
                dk = dk_acc[g]
                dk_o[g] = dk[:, :FOX_DH].astype(dk_o.dtype)
                dfk_o[g] = dk.T[slab]
                dv_o[g] = dv_acc[g].astype(dv_o.dtype)

        @pl.when(step == n_steps - 1)
        def _():
            for g in range(G):
                for j in range(n):
                    dqT = dq_acc[g, j]
                    dq_o[g, j * t:(j + 1) * t, :] = (dqT.T[:, :FOX_DH] * FOX_SCALE).astype(dq_o.dtype)
                    dfq_o[g, :, j * t:(j + 1) * t] = dqT[slab]

    def qlane(rows):
        return pl.BlockSpec((G, rows, t), lambda h, s, qt, kt: (h, 0, qt[s]))

    def qrow(cols):
        return pl.BlockSpec((G, t, cols), lambda h, s, qt, kt: (h, qt[s], 0))

    def krow(cols):
        return pl.BlockSpec((G, t, cols), lambda h, s, qt, kt: (h, kt[s], 0))

    def klane(rows):
        return pl.BlockSpec((G, rows, t), lambda h, s, qt, kt: (h, 0, kt[s]))

    def head(rows, cols):
        return pl.BlockSpec((G, rows, cols), lambda h, s, qt, kt: (h, 0, 0))

    return _fox_call(
        "fox_bwd", body, (qi_tab, ki_tab), (FOX_HEADS // G, n_steps),
        [qlane(FOX_AUG), qrow(FOX_AUG), krow(FOX_AUG), klane(FOX_AUG), krow(FOX_DH), qrow(FOX_DH), qlane(FOX_DH),
         qlane(FOX_DH), qlane(1)],
        [head(S, FOX_DH), head(FOX_BIAS_ROWS, S), krow(FOX_DH), klane(FOX_BIAS_ROWS), krow(FOX_DH)],
        [jax.ShapeDtypeStruct((FOX_HEADS, S, FOX_DH), BF16), jax.ShapeDtypeStruct((FOX_HEADS, FOX_BIAS_ROWS, S), F32),
         jax.ShapeDtypeStruct((FOX_HEADS, S, FOX_DH), BF16), jax.ShapeDtypeStruct((FOX_HEADS, FOX_BIAS_ROWS, S), F32),
         jax.ShapeDtypeStruct((FOX_HEADS, S, FOX_DH), BF16)],
        [pltpu.VMEM((G, n, FOX_AUG, t), F32), pltpu.VMEM((G, t, FOX_AUG), F32), pltpu.VMEM((G, t, FOX_DH), F32)],
        (qT, qa, ka, kT, v, do, doT, oT, lse), side)


def _fox_prep(u, fcum, S):
    T = min(512, S)
    head_of = jnp.arange(BRANCH) // FOX_DH
    dim_of = jnp.arange(BRANCH) % FOX_DH
    heads = jnp.arange(FOX_HEADS)[:, None, None]
    sel = (head_of[None, :, None] == heads) & (dim_of[None, :, None] == jnp.arange(FOX_AUG)[None, None, :])
    sel_q = (sel * FOX_SCALE).astype(BF16)
    sel_k = sel.astype(BF16)
    sel_vT = jnp.swapaxes(sel[:, :, :FOX_DH], 1, 2).astype(BF16)
    piece = jnp.arange(FOX_ONES * LANES) // LANES
    lane = jnp.arange(FOX_ONES * LANES) % LANES
    col = jnp.arange(FOX_AUG)[None, None, :]
    at_q = (lane[None, :, None] == heads) & (col == FOX_DH + FOX_ONES + piece[None, :, None])
    at_k = (lane[None, :, None] == heads) & (col == FOX_DH + piece[None, :, None])
    bias_q = at_q.astype(BF16)
    bias_k = (-at_k.astype(F32)).astype(BF16)
    cols = jnp.arange(FOX_AUG)[None, :]
    ones_q = ((cols >= FOX_DH) & (cols < FOX_DH + FOX_ONES)).astype(F32)
    ones_k = ((cols >= FOX_DH + FOX_ONES) & (cols < FOX_DH + 2 * FOX_ONES)).astype(F32)
    consts = [sel_q, sel_k, sel_vT, bias_q, bias_k, ones_q, ones_k]

    def body(cq, ck, cv, fc, sq, sk, svT, bq, bk, oq, ok, qa_o, ka_o, qT_o, kT_o, vh_o, vT_o):
        f = fc[...]
        hi = f.astype(BF16).astype(F32)
        mid = (f - hi).astype(BF16).astype(F32)
        lo = (f - hi - mid).astype(BF16).astype(F32)
        pieces = jnp.concatenate([hi, mid, lo], axis=1)
        q, k, v = cq[...], ck[...], cv[...]
        for h in range(FOX_HEADS):
            qa = _dot(q, sq[h], "nn") + _dot(pieces, bq[h], "nn") + oq[...]
            ka = _dot(k, sk[h], "nn") + _dot(pieces, bk[h], "nn") + ok[...]
            qa_o[h] = qa.astype(qa_o.dtype)
            ka_o[h] = ka.astype(ka_o.dtype)
            qT_o[h] = qa.T.astype(qT_o.dtype)
            kT_o[h] = ka.T.astype(kT_o.dtype)
            vT_o[h] = _dot(svT[h], v, "nt").astype(vT_o.dtype)
            vh_o[h] = _dot(v, svT[h], "nt").astype(vh_o.dtype)

    def win(off):
        return pl.BlockSpec((T, BRANCH), functools.partial(lambda i, blk: (i, blk), blk=off // BRANCH))

    def rows(c):
        return pl.BlockSpec((FOX_HEADS, T, c), lambda i: (0, i, 0))

    def lanes(r):
        return pl.BlockSpec((FOX_HEADS, r, T), lambda i: (0, 0, i))

    bf = lambda *shape: jax.ShapeDtypeStruct((FOX_HEADS,) + shape, BF16)
    return pl.pallas_call(
        body,
        name="fox_prep",
        grid=(S // T,),
        in_specs=[win(U_CQ), win(U_CK), win(U_CV), pl.BlockSpec((T, LANES), lambda i: (i, 0))]
        + [_full_spec(c) for c in consts],
        out_specs=[rows(FOX_AUG), rows(FOX_AUG), lanes(FOX_AUG), lanes(FOX_AUG), rows(FOX_DH), lanes(FOX_DH)],
        out_shape=[bf(S, FOX_AUG), bf(S, FOX_AUG), bf(FOX_AUG, S), bf(FOX_AUG, S), bf(S, FOX_DH), bf(FOX_DH, S)],
        compiler_params=pltpu.CompilerParams(dimension_semantics=("parallel",), vmem_limit_bytes=VMEM_LIMIT_BYTES),
    )(u, u, u, fcum, *consts)


def _to_heads(x2d, S):
    return jnp.transpose(x2d.reshape(S, FOX_HEADS, FOX_DH), (1, 0, 2))


def _from_heads(xh, S):
    return jnp.transpose(xh, (1, 0, 2)).reshape(S, FOX_HEADS * FOX_DH)


def _ffn_fwd(tag, x, wgT, wuT, wd, g, b, S, side=None):
    def up_epi(accs):
        gate, up = accs
        sil, _ = _silu_and_grad(gate)
        return [gate, up, sil * up]

    res = _mm(tag + "_up", "nt", [x], [wgT, wuT], [(0, 0, 0), (1, 0, 1)], 2, up_epi, [],
              [BF16, BF16, BF16], S, D_FF, D_MODEL, tn=1408, side=side)
    (gate, up, act), side_out = res if side is not None else (res, None)

    def down_epi(accs, xr, gg, bb):
        z = ALPHA * xr + 0.5 * accs[0]
        return [z, _ln_fwd(z, gg, bb)]

    z, xn = _mm(tag + "_down", "nn", [act], [wd], [(0, 0, 0)], 1, down_epi, [(x, "mn", 0), (g, "n"), (b, "n")],
                [F32, F32], S, D_MODEL, D_FF, tk=D_FF)
    return xn, dict(x=x, gate=gate, up=up, act=act, z=z), side_out


def _ln_bwd_call(tag, dy, z, g, S):
    def fn(dy_t, z_t, g_t):
        dz, xhat = _ln_bwd(dy_t, z_t, g_t)
        return [dz], [_colsum(dy_t * xhat), _colsum(dy_t)]

    (dz,), (dg, db) = _rowwise(tag + "_ln_bwd", fn, [dy, z], [g], [(D_MODEL, F32)], [D_MODEL, D_MODEL], S)
    return dz, dg, db


def _ffn_bwd(tag, dxn, sv, wgT, wuT, wd, g, S, gdt=F32, make_side=None, first_side=None):
    dz, dg, db = _ln_bwd_call(tag, dxn, sv["z"], g, S)

    def act_epi(accs, gate, up):
        da = 0.5 * accs[0]
        sil, dsil = _silu_and_grad(gate.astype(F32))
        return [da * up.astype(F32) * dsil, da * sil]

    res = _mm(tag + "_dact", "nt", [dz], [wd], [(0, 0, 0)], 1, act_epi,
              [(sv["gate"], "mn", 0), (sv["up"], "mn", 0)], [BF16, BF16], S, D_FF, D_MODEL, tn=1408, side=first_side)
    (dgate, dup), first_out = res if first_side is not None else (res, None)
    dwd = _mm1(tag + "_dwd", "tn", sv["act"], dz, D_FF, D_MODEL, S, scale=0.5, tm=1408, out_dtype=gdt)

    def two(accs):
        return [accs[0], accs[1]]

    dwgT, dwuT = _mm(tag + "_dwup", "tn", [dgate, dup], [sv["x"]], [(0, 0, 0), (1, 1, 0)], 2, two, [], [gdt, gdt],
                     D_FF, D_MODEL, S, tm=1408, tk=512)

    def dx_epi(accs, dzr):
        return [accs[0] + ALPHA * dzr]

    grads = dict(w_upT=jnp.concatenate([dwgT, dwuT], axis=0), w_down=dwd, ln_g=dg, ln_b=db)
    side = make_side(grads) if make_side is not None else None
    res = _mm(tag + "_dx", "nn", [dgate, dup], [wgT, wuT], [(0, 0, 0), (0, 1, 1)], 1, dx_epi, [(dz, "mn", 0)],
              [F32], S, D_MODEL, D_FF, tm=1024, tk=1408, side=side)
    (dx,), side_out = res if side is not None else (res, None)
    return dx, grads, (first_out, side_out)


def _mixer_fwd(x1, w, S, side=None, on_side=None):
    u = _mm1("w_in", "nt", x1, w["w_inT_p"], S, U_WIDTH, D_MODEL, tm=1024, tn=1536)
    xc, r, gi, a, h, y_a = _lru_fwd(u, w["lru"], S)
    y_b, oraw, states = _gla_fwd(u, w["gla"], S)
    fcum = _fox_gate_fwd(u, w["bfp"], S)
    qa, ka, qT, kT, vh, vT = _fox_prep(u, fcum, S)
    (oT, lse), side_out = _fox_fwd(qT, ka, vT, S, side)
    if on_side is not None:
        on_side(side_out)
    y_c = jnp.transpose(oT, (2, 0, 1)).reshape(S, BRANCH).astype(BF16)

    def merge_epi(accs, g0, g1, g2):
        merged = _sigmoid(g0) * accs[0] + _sigmoid(g1) * accs[1] + _sigmoid(g2) * accs[2]
        return [accs[0], accs[1], accs[2], merged]

    wb = w["w_branchT"]
    yp0, yp1, yp2, merged = _mm(
        "merge", "nt", [y_a, y_b, y_c], [wb[0], wb[1], wb[2]], [(0, 0, 0), (1, 1, 1), (2, 2, 2)], 3, merge_epi,
        [(u, "mn", 0), (u, "mn", 1), (u, "mn", 2)], [BF16, BF16, BF16, BF16], S, D_MODEL, BRANCH, tm=256)

    def out_epi(accs, xr, gg, bb):
        z = ALPHA * xr + accs[0]
        return [z, _ln_fwd(z, gg, bb)]

    z2, x2 = _mm("w_out", "nn", [merged], [w["w_out"]], [(0, 0, 0)], 1, out_epi,
                 [(x1, "mn", 0), (w["ln2_g"], "n"), (w["ln2_b"], "n")], [F32, F32], S, D_MODEL, D_MODEL)
    sv = dict(x=x1, u=u, xc=xc, r=r, i=gi, a=a, h=h, y_a=y_a, y_b=y_b, y_c=y_c, oraw=oraw,
              states=states, qT=qT, qa=qa, ka=ka, kT=kT, vh=vh, oT=oT, lse=lse, yp=(yp0, yp1, yp2), merged=merged,
              z=z2)
    return x2, sv, side_out


def _mixer_bwd(dx2, sv, w, S, make_side=None, gdt_a=F32, gdt_b=F32):
    u = sv["u"]
    dz, dg2, db2 = _ln_bwd_call("mix", dx2, sv["z"], w["ln2_g"], S)

    def dm_epi(accs, y0, y1, y2, g0, g1, g2):
        dm = accs[0]
        outs_p, outs_g = [], []
        for yp, gl in ((y0, g0), (y1, g1), (y2, g2)):
            sg = _sigmoid(gl)
            outs_p.append(dm * sg)
            outs_g.append(dm * yp.astype(F32) * sg * (1.0 - sg))
        return outs_p + outs_g

    yp = sv["yp"]
    dyp0, dyp1, dyp2, dgl0, dgl1, dgl2 = _mm(
        "dmerged", "nt", [dz], [w["w_out"]], [(0, 0, 0)], 1, dm_epi,
        [(yp[0], "mn", 0), (yp[1], "mn", 0), (yp[2], "mn", 0), (u, "mn", 0), (u, "mn", 1), (u, "mn", 2)],
        [BF16] * 6, S, D_MODEL, D_MODEL, tm=256)
    dw_out = _mm1("dw_out", "tn", sv["merged"], dz, D_MODEL, D_MODEL, S, out_dtype=gdt_b)
    wb = w["w_branchT"]
    dys, dwbs = [], []
    for j, (yj, dyp) in enumerate(((sv["y_a"], dyp0), (sv["y_b"], dyp1), (sv["y_c"], dyp2))):
        dys.append(_mm1("dy_branch%d" % j, "nn", dyp, wb[j], S, BRANCH, D_MODEL))
        dwbs.append(_mm1("dw_branch%d" % j, "tn", dyp, yj, D_MODEL, BRANCH, S, out_dtype=gdt_b))
    day, dxc, dwa, dwx, dba, dbx, dlam = _lru_bwd(dys[0], u, sv, w["lru"], S)
    dax, (dcw0, dcw1, dcw2, dcw3, dcb) = _conv_bwd(dxc, u, w["lru"], S)
    dbq, dbk, dbv, dbr, dglow, dwg2p, dbg, dng = _gla_bwd(dys[1], u, sv["oraw"], sv["states"], w["gla"], S)
    doh = _to_heads(dys[2], S)
    dw_branchT = jnp.stack(dwbs)
    side = make_side(dict(w_out=dw_out, w_branchT=dw_branchT)) if make_side is not None else None
    (dqh, dfq, dkh, dfk, dvh), side_out = _fox_bwd(sv["qT"], sv["qa"], sv["ka"], sv["kT"], sv["vh"], doh,
                                                   jnp.swapaxes(doh, 1, 2), sv["oT"], sv["lse"], S, side)
    dfc = jnp.transpose(dfq[:, FOX_ONES, :] - dfk[:, 0, :])
    dfc = jnp.pad(dfc, ((0, 0), (0, LANES - FOX_HEADS)))
    dcf, dbf = _fox_gate_bwd(dfc, u, w["bfp"], S)
    du = jnp.concatenate(
        [dgl0, dgl1, dgl2, dax, day, dbq, dbk, dbv, dbr, _from_heads(dqh, S).astype(BF16),
         _from_heads(dkh, S).astype(BF16), _from_heads(dvh, S).astype(BF16), dglow, dcf,
         jnp.zeros((S, U_WIDTH - U_CF - LANES), BF16)], axis=1)
    dw_inT_p = _mm1("dw_in", "tn", du, sv["x"], U_WIDTH, D_MODEL, S, tm=1536, out_dtype=gdt_a)

    def dx_epi(accs, dzr):
        return [accs[0] + ALPHA * dzr]

    (dx1,) = _mm("dx_mix", "nn", [du], [w["w_inT_p"]], [(0, 0, 0)], 1, dx_epi, [(dz, "mn", 0)], [F32], S, D_MODEL,
                 U_WIDTH, tm=1024, tk=1536)
    pieces = sorted(W_IN_SEGMENTS)
    dw_inT = jnp.concatenate([dw_inT_p[dst:dst + width] for _, width, dst in pieces], axis=0)
    eye = jnp.eye(LRU_BLOCKS, dtype=F32)
    dwa_b = jnp.einsum("ncmd,nm->ncd", dwa.reshape(LRU_BLOCKS, 64, LRU_BLOCKS, 64), eye)
    dwx_b = jnp.einsum("ncmd,nm->ncd", dwx.reshape(LRU_BLOCKS, 64, LRU_BLOCKS, 64), eye)
    grads = dict(
        w_inT=dw_inT, w_out=dw_out, w_branchT=dw_branchT, ln2_g=dg2, ln2_b=db2,
        conv_w=jnp.concatenate([dcw0, dcw1, dcw2, dcw3], axis=0).astype(gdt_a), conv_b=dcb, lru_wa=dwa_b, lru_wx=dwx_b,
        lru_ba=dba, lru_bx=dbx, lru_lambda=dlam, gla_w_g2=dwg2p[:GLA_LOWRANK].astype(gdt_a), gla_b_g=dbg, gla_norm_g=dng,
        fox_b_f=dbf[:, :FOX_HEADS])
    return dx1, grads, side_out


def _ple_fwd(x3, p_i, w, S):
    pe = _mm1("ple_proj", "nt", p_i, w["ple_w_projT"], S, D_MODEL, PLE_DIM)

    def epi(accs, xr, per, bg, gg, bb):
        sg = _sigmoid(accs[0] + bg)
        z = ALPHA * xr + sg * per
        return [sg, z, _ln_fwd(z, gg, bb)]

    sg, z4, x4 = _mm("ple_gate", "nn", [x3], [w["ple_w_gate"]], [(0, 0, 0)], 1, epi,
                     [(x3, "mn", 0), (pe, "mn", 0), (w["ple_b_gate"], "n"), (w["ln4_g"], "n"), (w["ln4_b"], "n")],
                     [F32, F32, F32], S, D_MODEL, D_MODEL)
    return x4, dict(x=x3, p=p_i, pe=pe, sg=sg, z=z4)


def _ple_bwd(dx4, sv, w, S, gdt=F32):
    def fn(dy_t, z_t, pe_t, sg_t, g_t):
        dz, xhat = _ln_bwd(dy_t, z_t, g_t)
        dgl = dz * pe_t * sg_t * (1.0 - sg_t)
        return [dz, dz * sg_t, dgl], [_colsum(dy_t * xhat), _colsum(dy_t), _colsum(dgl)]

    (dz, dpe, dgl), (dg4, db4, dbg) = _rowwise(
        "ple_bwd", fn, [dx4, sv["z"], sv["pe"], sv["sg"]], [w["ln4_g"]],
        [(D_MODEL, F32), (D_MODEL, BF16), (D_MODEL, BF16)], [D_MODEL] * 3, S)
    dwpT = _mm1("dw_ple_proj", "tn", dpe, sv["p"], D_MODEL, PLE_DIM, S, out_dtype=gdt)
    dwg = _mm1("dw_ple_gate", "tn", sv["x"], dgl, D_MODEL, D_MODEL, S, out_dtype=gdt)

    def dx_epi(accs, dzr):
        return [accs[0] + ALPHA * dzr]

    (dx3,) = _mm("dx_ple", "nt", [dgl], [w["ple_w_gate"]], [(0, 0, 0)], 1, dx_epi, [(dz, "mn", 0)], [F32], S,
                 D_MODEL, D_MODEL)
    return dx3, dict(ple_w_projT=dwpT, ple_w_gate=dwg, ple_b_gate=dbg, ln4_g=dg4, ln4_b=db4)


def _rows_of_all(g):
    return g.reshape((g.shape[0] * g.shape[1],) + g.shape[2:])


EARLY_WEIGHTS = ("ffn1_w_up", "ffn1_w_down", "w_in", "conv_w", "gla_w_g2")


def _ffn_weights(gathered, tag):
    upT = _rows_of_all(gathered[tag + "_w_up"])
    return upT[:D_FF], upT[D_FF:], _rows_of_all(gathered[tag + "_w_down"])


def _late_weights(gathered):
    return dict(ffn2=_ffn_weights(gathered, "ffn2"),
                w_branchT=jnp.moveaxis(gathered["w_branch"], 0, 1).reshape(3, D_MODEL, BRANCH),
                w_out=_rows_of_all(gathered["w_out"]),
                ple_w_projT=_rows_of_all(gathered["ple_w_proj"]),
                ple_w_gate=_rows_of_all(gathered["ple_w_gate"]))


def _w_in_operand(gathered_w_in):
    w_inT = _rows_of_all(gathered_w_in)
    placed = sorted((dst, src, width) for src, width, dst in W_IN_SEGMENTS)
    parts, pos = [], 0
    for dst, src, width in placed:
        if dst > pos:
            parts.append(jnp.zeros((dst - pos, D_MODEL), w_inT.dtype))
        parts.append(w_inT[src:src + width])
        pos = dst + width
    parts.append(jnp.zeros((U_WIDTH - pos, D_MODEL), w_inT.dtype))
    return jnp.concatenate(parts, axis=0)


def _layer_weights(gathered, full, i):
    w = _late_weights(gathered) if "w_out" in gathered else {}
    w["ffn1"] = _ffn_weights(gathered, "ffn1")
    if "w_in" in gathered:
        w["w_inT_p"] = _w_in_operand(gathered["w_in"])
    eye = jnp.eye(LRU_BLOCKS, dtype=F32)

    def dense(blocks):
        return jnp.einsum("ncd,nm->ncmd", blocks, eye).reshape(BRANCH, BRANCH).astype(BF16)

    def vec(name):
        return full[name][i].reshape(1, -1)

    cw = jnp.moveaxis(gathered["conv_w"], 0, 1).reshape(4, BRANCH)
    w_g2 = jnp.moveaxis(gathered["gla_w_g2"], 0, 1).reshape(GLA_LOWRANK, GLA_QK)
    w["lru"] = dict(cw0=cw[0:1], cw1=cw[1:2], cw2=cw[2:3], cw3=cw[3:4], conv_b=vec("conv_b"),
                    wa=dense(full["lru_wa"][i]), wx=dense(full["lru_wx"][i]), ba=vec("lru_ba"), bx=vec("lru_bx"),
                    lam=vec("lru_lambda"))
    hq = jnp.arange(GLA_QK) // GLA_DK
    hv = jnp.arange(GLA_V) // GLA_DV
    w["gla"] = dict(wg2=jnp.pad(w_g2, ((0, LANES - GLA_LOWRANK), (0, 0))).astype(BF16),
                    bg=vec("gla_b_g"), ng=vec("gla_norm_g"), bd=(hv[:, None] == hq[None, :]).astype(F32))
    w["bfp"] = jnp.pad(vec("fox_b_f"), ((0, 0), (0, LANES - FOX_HEADS)))
    for name in ("ln1_g", "ln1_b", "ln2_g", "ln2_b", "ln3_g", "ln3_b", "ln4_g", "ln4_b", "ple_b_gate"):
        w[name] = vec(name)
    return w


def _layer_fwd(x0, p_i, w, S, side=None, on_side=None, first_side=None, on_first=None):
    x1, s1, first_out = _ffn_fwd("ffn1", x0, *w["ffn1"], w["ln1_g"], w["ln1_b"], S, first_side)
    if on_first is not None:
        on_first(first_out)
    x2, s2, side_out = _mixer_fwd(x1, w, S, side, on_side)
    x3, s3, _ = _ffn_fwd("ffn2", x2, *w["ffn2"], w["ln3_g"], w["ln3_b"], S)
    x4, s4 = _ple_fwd(x3, p_i, w, S)
    return x4, (s1, s2, s3, s4), side_out


def _layer_bwd(dx4, saved, w, S, make_side=None, gdt_a=F32, gdt_b=F32, make_first=None, make_last=None):
    s1, s2, s3, s4 = saved
    dx3, g4 = _ple_bwd(dx4, s4, w, S, gdt_b)
    dx2, g3, _ = _ffn_bwd("ffn2", dx3, s3, *w["ffn2"], w["ln3_g"], S, gdt_b)
    late = dict(ffn2_w_upT=g3["w_upT"], ffn2_w_down=g3["w_down"], ple_w_projT=g4["ple_w_projT"],
                ple_w_gate=g4["ple_w_gate"])
    mixer_side = None if make_side is None else (lambda mix: make_side({**late, **mix}))
    dx1, g2, side_out = _mixer_bwd(dx2, s2, w, S, mixer_side, gdt_a, gdt_b)
    last = None if make_last is None else (
        lambda g: make_last({"ffn1_w_upT": g["w_upT"], "ffn1_w_down": g["w_down"]}))
    first = None if make_first is None else make_first(g2)
    dx0, g1, last_out = _ffn_bwd("ffn1", dx1, s1, *w["ffn1"], w["ln1_g"], S, gdt_a, last, first)
    grads = dict(g2)
    grads.update(g4)
    grads.update(late)
    grads.update(ffn1_w_upT=g1["w_upT"], ffn1_w_down=g1["w_down"], ln1_g=g1["ln_g"], ln1_b=g1["ln_b"],
                 ln3_g=g3["ln_g"], ln3_b=g3["ln_b"])
    return dx0, grads, side_out, last_out


def _travel_grads(grads, names):
    return [_dest_pieces(n, grads[n + "T" if n in COLUMN_SHARDED else n]) for n in names]


def _local_step(x, p, target, gathered0, rest, full, overlap):
    S = x.shape[0]
    names = [n for n, _ in SHARDED]
    early = [n for n in names if n in EARLY_WEIGHTS]
    late = [n for n in names if n not in EARLY_WEIGHTS]
    w0 = _layer_weights(gathered0, full, 0)
    if not overlap:
        h, saved0, _ = _layer_fwd(x, p[0], w0, S)
        w1 = _layer_weights(rest, full, 1)
        h, saved1, _ = _layer_fwd(h, p[1], w1, S)
    else:
        w_in0, late0, early1, late1 = rest
        h, saved0, got = _layer_fwd(x, p[0], w0, S, _gather_job(list(late0) + list(early1)),
                                    lambda got: w0.update(_late_weights(dict(zip(late, got[:len(late)])))),
                                    _gather_job([w_in0]), lambda got: w0.update(w_inT_p=_w_in_operand(got[0])))
        w1 = _layer_weights(dict(zip(early, got[len(late):])), full, 1)
        h, saved1, _ = _layer_fwd(h, p[1], w1, S, _gather_job(list(late1)),
                                  lambda got: w1.update(_late_weights(dict(zip(late, got)))))

    def loss_fn(y, t):
        err = y - t
        return [err * (1.0 / D_MODEL)], [_colsum(err * err) * (0.5 / D_MODEL)]

    (dy,), (lsum,) = _rowwise("loss", loss_fn, [h, target], [], [(D_MODEL, F32)], [D_MODEL], S)
    loss = jnp.sum(lsum)
    if not overlap:
        dy, g1, _, _ = _layer_bwd(dy, saved1, w1, S)
        dy, g0, _, _ = _layer_bwd(dy, saved0, w0, S)
        return loss, dy, [g0, g1], {}

    ffn_early = [n for n in early if n.startswith("ffn1")]
    mix_early = [n for n in early if not n.startswith("ffn1")]

    def two_stage(tag, which):
        def make(g):
            dest = _travel_grads(g, which)
            got = _sibling_swap_multi("grad_sibling_swap_" + tag, dest)
            core = lax.axis_index("c").astype(jnp.int32).reshape(1)
            return _chip_job([_pair_add("grad_pair_add_" + n, core, _as_rows(d, 2), _as_rows(a, 1))
                              for n, d, a in zip(which, dest, got)])
        return make

    dy, g1, late1_pieces, _ = _layer_bwd(dy, saved1, w1, S, lambda g: _scatter_job(_travel_grads(g, late)),
                                         BF16, BF16)
    dy, g0, pieces, (mix0_pieces, ffn0_pieces) = _layer_bwd(
        dy, saved0, w0, S, lambda g: _scatter_job(_travel_grads(g1, early) + _travel_grads(g, late)), F32, BF16,
        two_stage("mixer", mix_early), two_stage("ffn", ffn_early))
    exchanged = {(1, n): a for n, a in zip(late, late1_pieces)}
    exchanged.update({(1, n): a for n, a in zip(early, pieces[:len(early)])})
    exchanged.update({(0, n): a for n, a in zip(late, pieces[len(early):])})
    exchanged.update({(0, n): a for n, a in zip(mix_early, mix0_pieces)})
    exchanged.update({(0, n): a for n, a in zip(ffn_early, ffn0_pieces)})
    return loss, dy, [g0, g1], exchanged


def kernel(x, p, ffn1_w_up, ffn1_w_down, ln1_g, ln1_b, w_in, conv_w, conv_b, lru_wa, lru_ba, lru_wx, lru_bx, lru_lambda, gla_w_g2, gla_b_g, gla_norm_g, fox_b_f, w_branch, w_out, ln2_g, ln2_b, ffn2_w_up, ffn2_w_down, ln3_g, ln3_b, ple_w_proj, ple_w_gate, ple_b_gate, ln4_g, ln4_b, loss_target, m_ffn1_w_up, m_ffn1_w_down, m_ln1_g, m_ln1_b, m_w_in, m_conv_w, m_conv_b, m_lru_wa, m_lru_ba, m_lru_wx, m_lru_bx, m_lru_lambda, m_gla_w_g2, m_gla_b_g, m_gla_norm_g, m_fox_b_f, m_w_branch, m_w_out, m_ln2_g, m_ln2_b, m_ffn2_w_up, m_ffn2_w_down, m_ln3_g, m_ln3_b, m_ple_w_proj, m_ple_w_gate, m_ple_b_gate, m_ln4_g, m_ln4_b, v_ffn1_w_up, v_ffn1_w_down, v_ln1_g, v_ln1_b, v_w_in, v_conv_w, v_conv_b, v_lru_wa, v_lru_ba, v_lru_wx, v_lru_bx, v_lru_lambda, v_gla_w_g2, v_gla_b_g, v_gla_norm_g, v_fox_b_f, v_w_branch, v_w_out, v_ln2_g, v_ln2_b, v_ffn2_w_up, v_ffn2_w_down, v_ln3_g, v_ln3_b, v_ple_w_proj, v_ple_w_gate, v_ple_b_gate, v_ln4_g, v_ln4_b):
    env = dict(locals())
    wts = {n: env[n] for n in WEIGHTS}
    ms = {n: env["m_" + n] for n in WEIGHTS}
    vs = {n: env["v_" + n] for n in WEIGHTS}
    sharded = [n for n, _ in SHARDED]

    def travel(n, a):
        return jnp.swapaxes(a, -1, -2) if n in COLUMN_SHARDED else a

    shards = {(i, n): travel(n, wts[n][i]) if n in SHARDED_F32_GATHER else travel(n, wts[n][i]).astype(BF16)
              for i in range(DEPTH) for n in sharded}
    early = [n for n in sharded if n in EARLY_WEIGHTS]
    late = [n for n in sharded if n not in EARLY_WEIGHTS]
    first = [n for n in early if n != "w_in"]
    gathered0 = dict(zip(first, _allgather_multi("gather_weights", [shards[0, n] for n in first])))
    rest = (shards[0, "w_in"], [shards[0, n] for n in late], [shards[1, n] for n in early],
            [shards[1, n] for n in late])
    full = {n: wts[n] for n in REPLICATED}

    loss_part, grad_x, layer_grads, pieces = _local_step(x[0], p[:, 0], loss_target[0], gathered0, rest, full, True)
    loss = lax.psum(loss_part, MESH_AXES)

    rep = list(REPLICATED)
    rep_grads = [jnp.stack([layer_grads[i][n] for i in range(DEPTH)]).reshape(wts[n].shape) for n in rep]
    (gr,) = _allgather_multi("grad_gather_replicated", [_pack(rep_grads, F32)])

    kinds = ("grad", "delta", "new_m", "new_v")
    out = {}
    for n in sharded:
        local = [_as_rows(travel(n, pieces[i, n].reshape((-1,) + shards[i, n].shape)), 1) for i in range(DEPTH)]
        res = _adamw("adamw_" + n, local, _as_rows(wts[n], 1), _as_rows(ms[n], 1), _as_rows(vs[n], 1))
        for kind, arr in zip(kinds, res):
            out[kind + "_" + n] = arr.reshape(wts[n].shape)
    res = _adamw("adamw_replicated", [gr], _pack([wts[n] for n in rep], F32)[None],
                 _pack([ms[n] for n in rep], F32)[None], _pack([vs[n] for n in rep], F32)[None])
    shapes = [wts[n].shape for n in rep]
    for kind, buf in zip(kinds, res):
        for n, arr in zip(rep, _unpack(buf[0], shapes)):
            out[kind + "_" + n] = arr
    return (loss, grad_x[None], *[out["grad_" + n] for n in WEIGHTS], *[out["delta_" + n] for n in WEIGHTS],
            *[out["new_m_" + n] for n in WEIGHTS], *[out["new_v_" + n] for n in WEIGHTS])
```

```python
import functools
import math

import jax
import jax.numpy as jnp
from jax import lax
from jax.experimental import pallas as pl
from jax.experimental.pallas import tpu as pltpu

F32 = jnp.float32
BF16 = jnp.bfloat16

N_DEV = 8
MESH_AXES = ("x", "y", "c")
DEPTH = 2
D_MODEL = 1024
D_FF = 2816
BRANCH = 512
CHUNK = 64
GLA_HEADS = 4
GLA_DK = 64
GLA_DV = 128
GLA_LOWRANK = 16
GLA_TAU = 16.0
FOX_HEADS = 8
FOX_DH = 64
PLE_DIM = 256
LRU_C = 8.0
LRU_BLOCKS = 8
LN_EPS = 1e-5
RMS_EPS = 1e-6
ALPHA = (2 * DEPTH) ** 0.25
LANES = 128
NEG_BIG = -1e30

ADAM_LR = 0.001
ADAM_B1 = 0.9
ADAM_B2 = 0.999
ADAM_EPS = 1e-08
ADAM_WD = 0.01
ADAM_STEP = 10

VMEM_LIMIT_BYTES = 56 * 1024 * 1024

U_GATES = 0
U_AX = 3072
U_AY = 3584
U_BQ = 4096
U_BK = 4352
U_BV = 4608
U_BR = 5120
U_CQ = 5632
U_CK = 6144
U_CV = 6656
U_BLOW = 7168
U_CF = 7296
U_WIDTH = 7680
W_IN_SEGMENTS = (
    (0, 512, U_AX), (512, 512, U_AY), (1024, 256, U_BQ), (1280, 256, U_BK), (1536, 512, U_BV),
    (2048, 16, U_BLOW), (2064, 512, U_BR), (2576, 512, U_CQ), (3088, 512, U_CK), (3600, 512, U_CV),
    (4112, 8, U_CF), (4120, 3072, U_GATES),
)

SHARDED = (
    ("ffn1_w_up", 2), ("ffn1_w_down", 1), ("w_in", 2), ("conv_w", 2), ("gla_w_g2", 2), ("w_branch", 3),
    ("w_out", 1), ("ffn2_w_up", 2), ("ffn2_w_down", 1), ("ple_w_proj", 2), ("ple_w_gate", 1),
)
SHARDED_F32_GATHER = ("conv_w", "gla_w_g2")
COLUMN_SHARDED = ("ffn1_w_up", "ffn2_w_up", "w_in", "w_branch", "ple_w_proj")
REPLICATED = ("ln1_g", "ln1_b", "conv_b", "lru_wa", "lru_ba", "lru_wx", "lru_bx", "lru_lambda", "gla_b_g",
              "gla_norm_g", "fox_b_f", "ln2_g", "ln2_b", "ln3_g", "ln3_b", "ple_b_gate", "ln4_g", "ln4_b")
WEIGHTS = ("ffn1_w_up", "ffn1_w_down", "ln1_g", "ln1_b", "w_in", "conv_w", "conv_b", "lru_wa", "lru_ba", "lru_wx",
           "lru_bx", "lru_lambda", "gla_w_g2", "gla_b_g", "gla_norm_g", "fox_b_f", "w_branch", "w_out", "ln2_g",
           "ln2_b", "ffn2_w_up", "ffn2_w_down", "ln3_g", "ln3_b", "ple_w_proj", "ple_w_gate", "ple_b_gate", "ln4_g",
           "ln4_b")


def _sigmoid(x):
    return 1.0 / (1.0 + jnp.exp(-x))


def _log1p_pos(e):
    return jnp.where(e < 1e-4, e * (1.0 - 0.5 * e), jnp.log(1.0 + e))


def _softplus(x):
    return jnp.maximum(x, 0.0) + _log1p_pos(jnp.exp(-jnp.abs(x)))


def _log_sigmoid(x):
    return -_softplus(-x)


def _neg_expm1(y):
    series = -y * (1.0 + y * (0.5 + y * (1.0 / 6.0 + y * (1.0 / 24.0 + y * (1.0 / 120.0)))))
    return jnp.where(y > -0.1, series, 1.0 - jnp.exp(y))


def _silu_and_grad(x):
    s = _sigmoid(x)
    return x * s, s * (1.0 + x * (1.0 - s))


_GELU_C = math.sqrt(2.0 / math.pi)


def _gelu_and_grad(x):
    inner = _GELU_C * (x + 0.044715 * x * x * x)
    t = jnp.tanh(inner)
    g = 0.5 * x * (1.0 + t)
    dg = 0.5 * (1.0 + t) + 0.5 * x * (1.0 - t * t) * _GELU_C * (1.0 + 3.0 * 0.044715 * x * x)
    return g, dg


def _ln_stats(z):
    mu = jnp.mean(z, axis=-1, keepdims=True)
    zc = z - mu
    var = jnp.mean(zc * zc, axis=-1, keepdims=True)
    rstd = lax.rsqrt(var + LN_EPS)
    return zc * rstd, rstd


def _ln_fwd(z, g, b):
    xhat, _ = _ln_stats(z)
    return xhat * g + b


def _ln_bwd(dy, z, g):
    xhat, rstd = _ln_stats(z)
    dxh = dy * g
    m1 = jnp.mean(dxh, axis=-1, keepdims=True)
    m2 = jnp.mean(dxh * xhat, axis=-1, keepdims=True)
    return rstd * (dxh - m1 - xhat * m2), xhat


def _colsum(x):
    return jnp.sum(x, axis=0, keepdims=True)


def _dot(a, b, dims):
    dn = {"nn": (((1,), (0,)), ((), ())), "nt": (((1,), (1,)), ((), ())), "tn": (((0,), (0,)), ((), ()))}[dims]
    return lax.dot_general(a.astype(BF16), b.astype(BF16), dn, preferred_element_type=F32)


def _scan_rows(a, b, length, reverse=False, seg=None):
    rows = lax.broadcasted_iota(jnp.int32, b.shape, 0)
    span = seg if seg else length
    pos = rows % span if seg else rows
    d = 1
    while d < span:
        shift = (length - d) if reverse else d
        valid = (pos < span - d) if reverse else (pos >= d)
        sb = jnp.where(valid, pltpu.roll(b, shift, 0), 0.0)
        if a is None:
            b = b + sb
        else:
            b = b + a * sb
            a = a * jnp.where(valid, pltpu.roll(a, shift, 0), 1.0)
        d *= 2
    return a, b


def _tile(dim, pref):
    if dim <= pref:
        return dim
    best = None
    t = LANES
    while t <= pref:
        if dim % t == 0:
            best = t
        t += LANES
    assert best is not None, (dim, pref)
    return best


def _full_spec(arr):
    nd = arr.ndim
    return pl.BlockSpec(arr.shape, lambda *_: (0,) * nd)


def _mm(name, dims, a_ops, b_ops, terms, n_acc, epilogue, extras, out_dtypes, M, N, K, tm=512, tn=1024, tk=1024,
        side=None):
    tm, tn, tk = _tile(M, tm), _tile(N, tn), _tile(K, tk)
    gm, gn, gk = M // tm, N // tn, K // tk
    a_bytes = sum(a.size * a.dtype.itemsize for a in a_ops)
    b_bytes = sum(b.size * b.dtype.itemsize for b in b_ops)
    n_outer = gk == 1 and b_bytes + a_bytes * gn < a_bytes + b_bytes * gm

    def spec(shape, fn):
        if n_outer:
            return pl.BlockSpec(shape, lambda j, i, k: fn(i, j, k))
        return pl.BlockSpec(shape, fn)

    if dims == "tn":
        a_spec = spec((tk, tm), lambda i, j, k: (k, i))
    else:
        a_spec = spec((tm, tk), lambda i, j, k: (i, k))
    if dims == "nt":
        b_spec = spec((tn, tk), lambda i, j, k: (j, k))
    else:
        b_spec = spec((tk, tn), lambda i, j, k: (k, j))
    e_specs, e_arrays = [], []
    for ex in extras:
        if ex[1] == "mn":
            e_specs.append(spec((tm, tn), functools.partial(lambda i, j, k, off: (i, j + off), off=ex[2])))
        else:
            e_specs.append(spec((1, tn), lambda i, j, k: (0, j)))
        e_arrays.append(ex[0])
    na, nb, ne, no = len(a_ops), len(b_ops), len(extras), len(out_dtypes)
    grid = (gn, gm, gk) if n_outer else (gm, gn, gk)
    s_in, s_out, s_sems = _side_specs(side) if side is not None else ([], [], [])
    nsi, nso = len(s_in), len(s_out)
    total = grid[0] * grid[1] * grid[2]

    def body(*refs):
        a_refs = refs[:na]
        b_refs = refs[na:na + nb]
        e_refs = refs[na + nb:na + nb + ne]
        pos = na + nb + ne
        s_ins, o_refs = refs[pos:pos + nsi], refs[pos + nsi:pos + nsi + no]
        pos += nsi + no
        s_outs, acc_refs, sems = refs[pos:pos + nso], refs[pos + nso:pos + nso + n_acc], refs[pos + nso + n_acc:]
        k = pl.program_id(2)
        if side is not None:
            phases = side["phases"]
            triggers = [0, total - 1] if len(phases) == 2 else [0, total * 7 // 10, total - 1]
            flat = (pl.program_id(0) * grid[1] + pl.program_id(1)) * grid[2] + k
            for trigger, phase in zip(triggers[:-1], phases[:-1]):
                @pl.when(flat == trigger)
                def _(phase=phase):
                    phase(s_ins, s_outs, *sems)

        @pl.when(k == 0)
        def _():
            for acc in acc_refs:
                acc[...] = jnp.zeros_like(acc)

        for r, ai, bi in terms:
            acc_refs[r][...] += _dot(a_refs[ai][...], b_refs[bi][...], dims)

        @pl.when(k == gk - 1)
        def _():
            res = epilogue([acc[...] for acc in acc_refs], *[e[...] for e in e_refs])
            for o, val in zip(o_refs, res):
                o[...] = val.astype(o.dtype)

        if side is not None:
            @pl.when(flat == triggers[-1])
            def _():
                phases[-1](s_ins, s_outs, *sems)

    outs = pl.pallas_call(
        body,
        name=name,
        grid=grid,
        in_specs=[a_spec] * na + [b_spec] * nb + e_specs + s_in,
        out_specs=[spec((tm, tn), lambda i, j, k: (i, j))] * no + s_out,
        out_shape=[jax.ShapeDtypeStruct((M, N), dt) for dt in out_dtypes] + (side["out_shapes"] if side else []),
        scratch_shapes=[pltpu.VMEM((tm, tn), F32)] * n_acc + s_sems,
        compiler_params=pltpu.CompilerParams(
            dimension_semantics=("arbitrary",) * 3 if side is not None else ("parallel", "parallel", "arbitrary"),
            vmem_limit_bytes=VMEM_LIMIT_BYTES),
    )(*a_ops, *b_ops, *e_arrays, *(side["arrs"] if side else []))
    return outs if side is None else (outs[:no], outs[no:])


def _mm1(name, dims, a, b, M, N, K, out_dtype=F32, scale=None, **kw):
    def epi(accs):
        return [accs[0] if scale is None else accs[0] * scale]
    return _mm(name, dims, [a], [b], [(0, 0, 0)], 1, epi, [], [out_dtype], M, N, K, **kw)[0]


def _rowwise(name, fn, row_ins, vec_ins, row_outs, sum_outs, S, tr=512, reverse=False):
    tr = min(tr, S)
    g = S // tr
    rmap = (lambda i: (g - 1 - i)) if reverse else (lambda i: i)
    in_specs, arrays = [], []
    for r in row_ins:
        if isinstance(r, tuple):
            arr, width, blk = r
            in_specs.append(pl.BlockSpec((tr, width), functools.partial(lambda i, blk: (rmap(i), blk), blk=blk)))
        else:
            arr = r
            in_specs.append(pl.BlockSpec((tr, arr.shape[1]), lambda i: (rmap(i), 0)))
        arrays.append(arr)
    for v in vec_ins:
        in_specs.append(_full_spec(v))
        arrays.append(v)
    nr, nv, no, ns = len(row_ins), len(vec_ins), len(row_outs), len(sum_outs)

    def body(*refs):
        ins = [r[...] for r in refs[:nr + nv]]
        o_refs = refs[nr + nv:nr + nv + no]
        s_refs = refs[nr + nv + no:]
        outs, sums = fn(*ins)
        for o, val in zip(o_refs, outs):
            o[...] = val.astype(o.dtype)
        if ns:
            i = pl.program_id(0)

            @pl.when(i == 0)
            def _():
                for s, val in zip(s_refs, sums):
                    s[...] = val

            @pl.when(i > 0)
            def _():
                for s, val in zip(s_refs, sums):
                    s[...] += val

    res = pl.pallas_call(
        body,
        name=name,
        grid=(g,),
        in_specs=in_specs,
        out_specs=[pl.BlockSpec((tr, c), lambda i: (rmap(i), 0)) for c, _ in row_outs]
        + [pl.BlockSpec((1, c), lambda i: (0, 0)) for c in sum_outs],
        out_shape=[jax.ShapeDtypeStruct((S, c), dt) for c, dt in row_outs]
        + [jax.ShapeDtypeStruct((1, c), F32) for c in sum_outs],
        compiler_params=pltpu.CompilerParams(
            dimension_semantics=("arbitrary",), vmem_limit_bytes=VMEM_LIMIT_BYTES),
    )(*arrays)
    return res[:no], res[no:]


MESH_ID = pl.DeviceIdType.MESH


def _remote(src, dst, send_sem, recv_sem, to):
    return pltpu.make_async_remote_copy(src_ref=src, dst_ref=dst, send_sem=send_sem, recv_sem=recv_sem,
                                        device_id=to, device_id_type=MESH_ID)


def _hbm_call(name, body, arrs, out_shapes, n_send, n_recv, n_local):
    return pl.pallas_call(
        body,
        name=name,
        in_specs=[pl.BlockSpec(memory_space=pltpu.HBM)] * len(arrs),
        out_specs=[pl.BlockSpec(memory_space=pltpu.HBM)] * len(out_shapes),
        out_shape=out_shapes,
        scratch_shapes=[pltpu.SemaphoreType.DMA((n_send,)), pltpu.SemaphoreType.DMA((n_recv,)),
                        pltpu.SemaphoreType.DMA((n_local,))],
        compiler_params=pltpu.CompilerParams(has_side_effects=True),
    )(*arrs)


def _side_job(arrs, out_shapes, n_send, n_recv, n_local, phases):
    return dict(arrs=list(arrs), out_shapes=list(out_shapes), sems=(n_send, n_recv, n_local), phases=phases)


def _side_specs(side):
    hbm = pl.BlockSpec(memory_space=pltpu.HBM)
    sems = [pltpu.SemaphoreType.DMA((k,)) for k in side["sems"]]
    return [hbm] * len(side["arrs"]), [hbm] * len(side["out_shapes"]), sems


def _gather_job(arrs):
    n = len(arrs)

    def plan(ins, outs, send_sems, recv_sems, local_sems):
        x, y, c = lax.axis_index("x"), lax.axis_index("y"), lax.axis_index("c")
        me, sibling = (x, y, c), (x, y, 1 - c)
        chips = [(1 - x, y), (x, 1 - y), (1 - x, 1 - y)]

        def slot(i, dev):
            return outs[i].at[4 * dev[0] + 2 * dev[1] + dev[2]]

        def copy(i, k, block, to, src=None):
            dst = slot(i, block)
            return _remote(dst if src is None else src, dst, send_sems.at[7 * i + k], recv_sems.at[7 * i + k], to)

        mine = [pltpu.make_async_copy(ins[i], slot(i, me), local_sems.at[i]) for i in range(n)]
        first = []
        for i in range(n):
            first.append(copy(i, 0, me, sibling, src=ins[i]))
            first += [copy(i, 1 + j, me, (*chip, c), src=ins[i]) for j, chip in enumerate(chips)]
        arrive = [[copy(i, 1 + j, (*chip, c), me) for i in range(n)] for j, chip in enumerate(chips)]
        passed = [[copy(i, 4 + j, (*chip, c), sibling) for i in range(n)] for j, chip in enumerate(chips)]
        last = [copy(i, 0, sibling, me) for i in range(n)]
        last += [copy(i, 4 + j, (*chip, 1 - c), me) for i in range(n) for j, chip in enumerate(chips)]
        return mine, first, arrive, passed, last

    def start(*refs):
        mine, first, _, _, _ = plan(*refs)
        for cp in mine + first:
            cp.start()

    def forward(*refs):
        _, _, arrive, passed, _ = plan(*refs)
        for came, onward in zip(arrive, passed):
            for a, p in zip(came, onward):
                a.wait_recv()
                p.start()

    def finish(*refs):
        mine, first, _, passed, last = plan(*refs)
        for cp in last:
            cp.wait_recv()
        for cp in first + [p for onward in passed for p in onward]:
            cp.wait_send()
        for cp in mine:
            cp.wait()

    outs = [jax.ShapeDtypeStruct((N_DEV,) + a.shape, a.dtype) for a in arrs]
    return _side_job(arrs, outs, 7 * n, 7 * n, n, [start, forward, finish])


def _scatter_job(arrs):
    n = len(arrs)

    def plan(ins, outs, send_sems, recv_sems, local_sems):
        x, y, c = lax.axis_index("x"), lax.axis_index("y"), lax.axis_index("c")
        here = 2 * x + y
        local = [pltpu.make_async_copy(ins[i].at[here, c], outs[i].at[here, c], local_sems.at[i]) for i in range(n)]
        sends, recvs = [], []
        for i in range(n):
            for k in range(1, N_DEV):
                px = 1 - x if k & 4 else x
                py = 1 - y if k & 2 else y
                pc = 1 - c if k & 1 else c
                sems = (send_sems.at[7 * i + k - 1], recv_sems.at[7 * i + k - 1], (px, py, pc))
                sends.append(_remote(ins[i].at[2 * px + py, pc], outs[i].at[here, c], *sems))
                recvs.append(_remote(ins[i].at[2 * px + py, pc], outs[i].at[2 * px + py, pc], *sems))
        return local, sends, recvs

    def start(*refs):
        local, sends, _ = plan(*refs)
        for cp in local + sends:
            cp.start()

    def finish(*refs):
        local, sends, recvs = plan(*refs)
        for cp in recvs:
            cp.wait_recv()
        for cp in sends:
            cp.wait_send()
        for cp in local:
            cp.wait()

    outs = [jax.ShapeDtypeStruct(a.shape, a.dtype) for a in arrs]
    return _side_job(arrs, outs, 7 * n, 7 * n, n, [start, finish])


def _run_job(name, job):
    na, no = len(job["arrs"]), len(job["out_shapes"])

    def body(*refs):
        ins, outs, sems = refs[:na], refs[na:na + no], refs[na + no:]
        for phase in job["phases"]:
            phase(ins, outs, *sems)

    return _hbm_call(name, body, job["arrs"], job["out_shapes"], *job["sems"])


def _allgather_multi(name, arrs):
    return _run_job(name, _gather_job(arrs))


def _sibling_swap_multi(name, arrs):
    n = len(arrs)
    per = 4

    def body(*refs):
        ins, got = refs[:n], refs[n:2 * n]
        send_sems, recv_sems, _ = refs[2 * n:]
        x, y, c = lax.axis_index("x"), lax.axis_index("y"), lax.axis_index("c")
        sibling = (x, y, 1 - c)
        sends = []
        for i in range(n):
            for a in range(4):
                k = per * i + a
                sends.append(_remote(ins[i].at[a, 1 - c], got[i].at[a], send_sems.at[k], recv_sems.at[k], sibling))
        for cp in sends:
            cp.start()
        for cp in sends:
            cp.wait_recv()
        for cp in sends:
            cp.wait_send()

    outs = [jax.ShapeDtypeStruct((4,) + a.shape[2:], a.dtype) for a in arrs]
    return _hbm_call(name, body, arrs, outs, per * n, per * n, 1)


def _chip_job(arrs):
    n = len(arrs)

    def plan(ins, outs, send_sems, recv_sems, local_sems):
        x, y, c = lax.axis_index("x"), lax.axis_index("y"), lax.axis_index("c")
        mine = 2 * x + y
        chips = [(1 - x, y), (x, 1 - y), (1 - x, 1 - y)]
        local = [pltpu.make_async_copy(ins[i].at[mine], outs[i].at[mine], local_sems.at[i]) for i in range(n)]
        sends, recvs = [], []
        for i in range(n):
            for j, (px, py) in enumerate(chips):
                peer = 2 * px + py
                sems = (send_sems.at[3 * i + j], recv_sems.at[3 * i + j], (px, py, c))
                sends.append(_remote(ins[i].at[peer], outs[i].at[mine], *sems))
                recvs.append(_remote(ins[i].at[peer], outs[i].at[peer], *sems))
        return local, sends, recvs

    def start(*refs):
        local, sends, _ = plan(*refs)
        for cp in local + sends:
            cp.start()

    def finish(*refs):
        local, sends, recvs = plan(*refs)
        for cp in recvs:
            cp.wait_recv()
        for cp in sends:
            cp.wait_send()
        for cp in local:
            cp.wait()

    outs = [jax.ShapeDtypeStruct(a.shape, a.dtype) for a in arrs]
    return _side_job(arrs, outs, 3 * n, 3 * n, n, [start, finish])


def _as_rows(a, lead):
    return a.reshape(a.shape[:lead] + (-1, a.shape[-1]))


def _row_tile(rows, cols, parts):
    budget = 4 * 1024 * 1024 // (4 * max(cols, LANES) * parts)
    return _tile_rows(rows, max(8, min(512, budget // 8 * 8)))


def _pair_add(name, core, both, got):
    _, rows, cols = got.shape
    tr = _row_tile(rows, cols, 2)

    def body(c_ref, a_ref, b_ref, o_ref):
        o_ref[...] = (a_ref[...] + b_ref[...]).astype(o_ref.dtype)

    blk = pl.BlockSpec((1, tr, cols), lambda ch, i, c_ref: (ch, i, 0))
    return pl.pallas_call(
        body, name=name,
        grid_spec=pltpu.PrefetchScalarGridSpec(
            num_scalar_prefetch=1, grid=(4, rows // tr),
            in_specs=[pl.BlockSpec((1, None, tr, cols), lambda ch, i, c_ref: (ch, c_ref[0], i, 0)), blk],
            out_specs=blk),
        out_shape=jax.ShapeDtypeStruct(got.shape, BF16),
        compiler_params=pltpu.CompilerParams(dimension_semantics=("parallel", "parallel"),
                                             vmem_limit_bytes=VMEM_LIMIT_BYTES),
    )(core, both, got)


def _adamw(name, gparts, w, m, v):
    layers = len(gparts)
    _, rows, cols = gparts[0].shape
    tr = _row_tile(rows, cols, sum(gp.shape[0] for gp in gparts))
    c1 = 1.0 / (1.0 - ADAM_B1 ** ADAM_STEP)
    c2 = 1.0 / (1.0 - ADAM_B2 ** ADAM_STEP)

    def body(*refs):
        gp_refs = refs[:layers]
        w_ref, m_ref, v_ref, g_ref, d_ref, nm_ref, nv_ref = refs[layers:]
        layer = pl.program_id(0)
        g = None
        for k, gp_ref in enumerate(gp_refs):
            gk = gp_ref[0].astype(F32)
            for i in range(1, gp_ref.shape[0]):
                gk = gk + gp_ref[i].astype(F32)
            g = gk if g is None else jnp.where(layer == k, gk, g)
        nm = ADAM_B1 * m_ref[...] + (1.0 - ADAM_B1) * g
        nv = ADAM_B2 * v_ref[...] + (1.0 - ADAM_B2) * (g * g)
        m_hat = nm * c1
        v_hat = nv * c2
        g_ref[...] = g
        nm_ref[...] = nm
        nv_ref[...] = nv
        d_ref[...] = -ADAM_LR * (m_hat / (jnp.sqrt(v_hat) + ADAM_EPS) + ADAM_WD * w_ref[...])

    row = pl.BlockSpec((None, tr, cols), lambda l, i: (l, i, 0))
    return pl.pallas_call(
        body,
        name=name,
        grid=(layers, rows // tr),
        in_specs=[pl.BlockSpec((gp.shape[0], tr, cols), lambda l, i: (0, i, 0)) for gp in gparts] + [row, row, row],
        out_specs=[row] * 4,
        out_shape=[jax.ShapeDtypeStruct((layers, rows, cols), F32)] * 4,
        compiler_params=pltpu.CompilerParams(dimension_semantics=("parallel", "parallel"),
                                             vmem_limit_bytes=VMEM_LIMIT_BYTES),
    )(*gparts, w, m, v)


def _tile_rows(rows, pref):
    t = min(pref, rows) // 8 * 8
    while t >= 8 and rows % t:
        t -= 8
    return t if t >= 8 else rows


PACK_ROWS = 512


def _pack(arrs, dtype):
    flat = jnp.concatenate([a.astype(dtype).reshape(-1) for a in arrs])
    quantum = PACK_ROWS * LANES
    padded = -(-flat.shape[0] // quantum) * quantum
    return jnp.pad(flat, (0, padded - flat.shape[0])).reshape(-1, LANES)


def _unpack(buf, shapes, lead=()):
    flat = buf.reshape(lead + (-1,))
    out, off = [], 0
    for shp in shapes:
        n = math.prod(shp)
        out.append(flat[..., off:off + n].reshape(lead + tuple(shp)))
        off += n
    return out


def _dest_pieces(name, g):
    if name == "w_branch":
        return jnp.moveaxis(g.reshape(3, 4, 2, D_MODEL // N_DEV, BRANCH), 0, 2)
    if name in SHARDED_F32_GATHER:
        return jnp.moveaxis(g.reshape(g.shape[0], 4, 2, -1), 0, 2)
    return g.reshape((4, 2, g.shape[0] // N_DEV) + g.shape[1:])


HALO = 8
LRU_TILE = 256
CONV_TILE = 512


def _rows_down(x, prev, k):
    xs = pltpu.roll(x, k, 0)
    row = lax.broadcasted_iota(jnp.int32, prev.shape, 0)
    top = jnp.where(row < k, pltpu.roll(prev, k, 0), xs[:HALO])
    return jnp.concatenate([top, xs[HALO:]], axis=0)


def _rows_up(x, nxt, k):
    rows = x.shape[0]
    xs = pltpu.roll(x, rows - k, 0)
    row = lax.broadcasted_iota(jnp.int32, nxt.shape, 0)
    bottom = jnp.where(row >= HALO - k, pltpu.roll(nxt, HALO - k, 0), xs[rows - HALO:])
    return jnp.concatenate([xs[:rows - HALO], bottom], axis=0)


def _halo_before(T, block_of, col=0):
    per = T // HALO
    return pl.BlockSpec((HALO, BRANCH), lambda t: (jnp.maximum(block_of(t) * per - 1, 0), col))


def _halo_after(T, block_of, S, col=0):
    per = T // HALO
    return pl.BlockSpec((HALO, BRANCH), lambda t: (jnp.minimum((block_of(t) + 1) * per, S // HALO - 1), col))


def _lru_fwd(u, lw, S):
    T = min(LRU_TILE, S)
    nb = S // T
    row = pl.BlockSpec((T, BRANCH), lambda t: (t, 0))
    vecs = [lw["cw0"], lw["cw1"], lw["cw2"], lw["cw3"], lw["conv_b"], lw["wa"], lw["wx"], lw["ba"], lw["bx"],
            lw["lam"]]

    def body(ax, ax_before, ay, cw0, cw1, cw2, cw3, cb, wa, wx, ba, bx, lam, xc_o, r_o, i_o, a_o, h_o, ya_o, hc):
        t = pl.program_id(0)

        @pl.when(t == 0)
        def _():
            hc[...] = jnp.zeros_like(hc)

        x = ax[...]
        before = jnp.where(t == 0, 0.0, ax_before[...])
        xc = (cw3[...] * x + cw2[...] * _rows_down(x, before, 1) + cw1[...] * _rows_down(x, before, 2)
              + cw0[...] * _rows_down(x, before, 3) + cb[...])
        r = _sigmoid(_dot(xc, wa[...], "nn") + ba[...])
        gi = _sigmoid(_dot(xc, wx[...], "nn") + bx[...])
        sp = _softplus(-lam[...])
        la = -LRU_C * r * sp
        a = jnp.exp(la)
        mult = jnp.sqrt(_neg_expm1(2.0 * la))
        A, B = _scan_rows(a, mult * gi * xc, T)
        h = B + A * hc[...]
        h_o[...] = h
        hc[...] = h_o[pl.ds(T - 1, 1), :]
        xc_o[...] = xc
        r_o[...] = r
        i_o[...] = gi
        a_o[...] = a
        gy, _ = _gelu_and_grad(ay[...])
        ya_o[...] = (gy * h).astype(ya_o.dtype)

    outs = pl.pallas_call(
        body,
        name="lru_fwd",
        grid=(nb,),
        in_specs=[pl.BlockSpec((T, BRANCH), lambda t: (t, U_AX // BRANCH)),
                  _halo_before(T, lambda t: t, U_AX // BRANCH),
                  pl.BlockSpec((T, BRANCH), lambda t: (t, U_AY // BRANCH))] + [_full_spec(v) for v in vecs],
        out_specs=[row] * 6,
        out_shape=[jax.ShapeDtypeStruct((S, BRANCH), F32)] * 5 + [jax.ShapeDtypeStruct((S, BRANCH), BF16)],
        scratch_shapes=[pltpu.VMEM((1, BRANCH), F32)],
        compiler_params=pltpu.CompilerParams(dimension_semantics=("arbitrary",), vmem_limit_bytes=VMEM_LIMIT_BYTES),
    )(u, u, u, *vecs)
    return outs


def _lru_bwd(dya, u, sv, lw, S):
    T = min(LRU_TILE, S)
    nb = S // T
    rrow = pl.BlockSpec((T, BRANCH), lambda t: (nb - 1 - t, 0))
    sq = pl.BlockSpec((BRANCH, BRANCH), lambda t: (0, 0))
    vrow = pl.BlockSpec((1, BRANCH), lambda t: (0, 0))

    def block(t):
        return nb - 1 - t

    def body(dya_r, ay, h, h_before, xc_r, r_r, i_r, a_r, a_after, wa, wx, lam,
             day_o, dxc_o, dwa_o, dwx_o, dba_o, dbx_o, dlam_o, lcar, tmp):
        t = pl.program_id(0)
        h_prev = _rows_down(h[...], jnp.where(t == nb - 1, 0.0, h_before[...]), 1)
        a_next = _rows_up(a_r[...], jnp.where(t == 0, 0.0, a_after[...]), 1)

        @pl.when(t == 0)
        def _():
            lcar[...] = jnp.zeros_like(lcar)
            dwa_o[...] = jnp.zeros_like(dwa_o)
            dwx_o[...] = jnp.zeros_like(dwx_o)
            dba_o[...] = jnp.zeros_like(dba_o)
            dbx_o[...] = jnp.zeros_like(dbx_o)
            dlam_o[...] = jnp.zeros_like(dlam_o)

        gy, dgy = _gelu_and_grad(ay[...])
        dy = dya_r[...]
        day_o[...] = (dy * h[...] * dgy).astype(day_o.dtype)
        A, B = _scan_rows(a_next, dy * gy, T, reverse=True)
        lmb = B + A * lcar[...]
        tmp[...] = lmb
        lcar[...] = tmp[pl.ds(0, 1), :]
        xc, r, gi, a = xc_r[...], r_r[...], i_r[...], a_r[...]
        sp = _softplus(-lam[...])
        la = -LRU_C * r * sp
        mult = jnp.sqrt(_neg_expm1(2.0 * la))
        da = lmb * h_prev
        dmult = lmb * gi * xc
        di = lmb * mult * xc
        dxc = lmb * mult * gi
        dla = da * a - dmult * a * a / mult
        dr = dla * (-LRU_C * sp)
        dlam_o[...] += _colsum(dla * (LRU_C * r)) * _sigmoid(-lam[...])
        dpr = dr * r * (1.0 - r)
        dpi = di * gi * (1.0 - gi)
        dba_o[...] += _colsum(dpr)
        dbx_o[...] += _colsum(dpi)
        dxc_o[...] = dxc + _dot(dpr, wa[...], "nt") + _dot(dpi, wx[...], "nt")
        dwa_o[...] += _dot(xc, dpr, "tn")
        dwx_o[...] += _dot(xc, dpi, "tn")

    outs = pl.pallas_call(
        body,
        name="lru_bwd",
        grid=(nb,),
        in_specs=[rrow, pl.BlockSpec((T, BRANCH), lambda t: (nb - 1 - t, U_AY // BRANCH)), rrow,
                  _halo_before(T, block), rrow, rrow, rrow, rrow, _halo_after(T, block, S), sq, sq, vrow],
        out_specs=[rrow, rrow, sq, sq, vrow, vrow, vrow],
        out_shape=[jax.ShapeDtypeStruct((S, BRANCH), BF16), jax.ShapeDtypeStruct((S, BRANCH), F32),
                   jax.ShapeDtypeStruct((BRANCH, BRANCH), F32), jax.ShapeDtypeStruct((BRANCH, BRANCH), F32),
                   jax.ShapeDtypeStruct((1, BRANCH), F32), jax.ShapeDtypeStruct((1, BRANCH), F32),
                   jax.ShapeDtypeStruct((1, BRANCH), F32)],
        scratch_shapes=[pltpu.VMEM((1, BRANCH), F32), pltpu.VMEM((T, BRANCH), F32)],
        compiler_params=pltpu.CompilerParams(dimension_semantics=("arbitrary",), vmem_limit_bytes=VMEM_LIMIT_BYTES),
    )(dya, u, sv["h"], sv["h"], sv["xc"], sv["r"], sv["i"], sv["a"], sv["a"], lw["wa"], lw["wx"], lw["lam"])
    return outs


def _conv_bwd(dxc, u, lw, S):
    T = min(CONV_TILE, S)
    nb = S // T
    vecs = [lw["cw0"], lw["cw1"], lw["cw2"], lw["cw3"]]
    vrow = pl.BlockSpec((1, BRANCH), lambda t: (0, 0))

    def body(d_r, d_after, ax, ax_before, cw0, cw1, cw2, cw3, dax_o, dcw0_o, dcw1_o, dcw2_o, dcw3_o, dcb_o):
        t = pl.program_id(0)
        d = d_r[...]
        after = jnp.where(t == nb - 1, 0.0, d_after[...])
        x = ax[...]
        before = jnp.where(t == 0, 0.0, ax_before[...])
        dax = (cw3[...] * d + cw2[...] * _rows_up(d, after, 1) + cw1[...] * _rows_up(d, after, 2)
               + cw0[...] * _rows_up(d, after, 3))
        dax_o[...] = dax.astype(dax_o.dtype)
        sums = [_colsum(d * _rows_down(x, before, 3)), _colsum(d * _rows_down(x, before, 2)),
                _colsum(d * _rows_down(x, before, 1)), _colsum(d * x), _colsum(d)]
        outs = [dcw0_o, dcw1_o, dcw2_o, dcw3_o, dcb_o]

        @pl.when(t == 0)
        def _():
            for o, val in zip(outs, sums):
                o[...] = val

        @pl.when(t > 0)
        def _():
            for o, val in zip(outs, sums):
                o[...] += val

    res = pl.pallas_call(
        body,
        name="conv_bwd",
        grid=(nb,),
        in_specs=[pl.BlockSpec((T, BRANCH), lambda t: (t, 0)), _halo_after(T, lambda t: t, S),
                  pl.BlockSpec((T, BRANCH), lambda t: (t, U_AX // BRANCH)),
                  _halo_before(T, lambda t: t, U_AX // BRANCH)] + [_full_spec(v) for v in vecs],
        out_specs=[pl.BlockSpec((T, BRANCH), lambda t: (t, 0))] + [vrow] * 5,
        out_shape=[jax.ShapeDtypeStruct((S, BRANCH), BF16)] + [jax.ShapeDtypeStruct((1, BRANCH), F32)] * 5,
        compiler_params=pltpu.CompilerParams(dimension_semantics=("arbitrary",), vmem_limit_bytes=VMEM_LIMIT_BYTES),
    )(dxc, dxc, u, u, *vecs)
    return res[0], res[1:]


GLA_QK = GLA_HEADS * GLA_DK
GLA_V = GLA_HEADS * GLA_DV
GLA_SCALE = GLA_DK ** -0.5


def _gla_specs(TB, rev_nb=None):
    def rmap(t):
        return t if rev_nb is None else rev_nb - 1 - t
    return [
        pl.BlockSpec((TB, GLA_QK), lambda t: (rmap(t), U_BQ // GLA_QK)),
        pl.BlockSpec((TB, GLA_QK), lambda t: (rmap(t), U_BK // GLA_QK)),
        pl.BlockSpec((TB, GLA_V), lambda t: (rmap(t), U_BV // GLA_V)),
        pl.BlockSpec((TB, GLA_V), lambda t: (rmap(t), U_BR // GLA_V)),
        pl.BlockSpec((TB, LANES), lambda t: (rmap(t), U_BLOW // LANES)),
    ]


def _gla_gates(gl, wg2, bg, TB):
    pre = _dot(gl, wg2, "nn") + bg
    la = _log_sigmoid(pre) * (1.0 / GLA_TAU)
    _, gc = _scan_rows(None, la, TB, seg=CHUNK)
    return pre, la, gc


def _gla_fwd(u, gw, S):
    TB = min(512, S)
    nb = S // TB
    cpb = TB // CHUNK
    vecs = [gw["wg2"], gw["bg"], gw["ng"], gw["bd"]]

    def body(q_r, k_r, v_r, br_r, gl_r, wg2, bg, ng, bd, yb_o, oraw_o, st_o, st):
        t = pl.program_id(0)

        @pl.when(t == 0)
        def _():
            st[...] = jnp.zeros_like(st)

        _, la, gc = _gla_gates(gl_r[...], wg2[...], bg[...], TB)
        for c in range(cpb):
            sl = slice(c * CHUNK, (c + 1) * CHUNK)
            gt = _colsum(la[sl])
            kdec = k_r[sl, :] * jnp.exp(gt - gc[sl])
            d_t = _dot(v_r[sl, :], kdec, "tn") * bd[...]
            s_new = st[...] * jnp.exp(gt) + d_t
            st[...] = s_new
            st_o[c] = s_new
            oraw_o[sl, :] = _dot(q_r[sl, :] * GLA_SCALE, s_new, "nt")
        for h in range(GLA_HEADS):
            hs = slice(h * GLA_DV, (h + 1) * GLA_DV)
            oh = oraw_o[:, hs]
            on = oh * lax.rsqrt(jnp.mean(oh * oh, axis=-1, keepdims=True) + RMS_EPS)
            sil, _ = _silu_and_grad(br_r[:, hs])
            yb_o[:, hs] = (on * ng[:, hs] * sil).astype(yb_o.dtype)

    return pl.pallas_call(
        body,
        name="gla_fwd",
        grid=(nb,),
        in_specs=_gla_specs(TB) + [_full_spec(v) for v in vecs],
        out_specs=[pl.BlockSpec((TB, GLA_V), lambda t: (t, 0)), pl.BlockSpec((TB, GLA_V), lambda t: (t, 0)),
                   pl.BlockSpec((cpb, GLA_V, GLA_QK), lambda t: (t, 0, 0))],
        out_shape=[jax.ShapeDtypeStruct((S, GLA_V), BF16), jax.ShapeDtypeStruct((S, GLA_V), F32),
                   jax.ShapeDtypeStruct((S // CHUNK, GLA_V, GLA_QK), F32)],
        scratch_shapes=[pltpu.VMEM((GLA_V, GLA_QK), F32)],
        compiler_params=pltpu.CompilerParams(dimension_semantics=("arbitrary",), vmem_limit_bytes=VMEM_LIMIT_BYTES),
    )(u, u, u, u, u, *vecs)


def _gla_bwd(dyb, u, oraw, states, gw, S):
    TB = min(512, S)
    nb = S // TB
    cpb = TB // CHUNK
    vecs = [gw["wg2"], gw["bg"], gw["ng"], gw["bd"]]

    def rrow(width):
        return pl.BlockSpec((TB, width), lambda t: (nb - 1 - t, 0))

    def body(dyb_r, oraw_r, q_r, k_r, v_r, br_r, gl_r, st_r, sp_r, wg2, bg, ng, bd,
             dq_o, dk_o, dv_o, dbr_o, dgl_o, dwg2_o, dbg_o, dng_o, dcar, do_buf, dla_buf):
        t = pl.program_id(0)
        blk = nb - 1 - t

        @pl.when(t == 0)
        def _():
            dcar[...] = jnp.zeros_like(dcar)
            dwg2_o[...] = jnp.zeros_like(dwg2_o)
            dbg_o[...] = jnp.zeros_like(dbg_o)
            dng_o[...] = jnp.zeros_like(dng_o)

        pre, la, gc = _gla_gates(gl_r[...], wg2[...], bg[...], TB)
        for h in range(GLA_HEADS):
            hs = slice(h * GLA_DV, (h + 1) * GLA_DV)
            oh = oraw_r[:, hs]
            rs = lax.rsqrt(jnp.mean(oh * oh, axis=-1, keepdims=True) + RMS_EPS)
            on = oh * rs
            sil, dsil = _silu_and_grad(br_r[:, hs])
            dy = dyb_r[:, hs]
            dbr_o[:, hs] = (dy * on * ng[:, hs] * dsil).astype(dbr_o.dtype)
            don = dy * ng[:, hs] * sil
            dng_o[:, hs] += _colsum(dy * on * sil)
            do_buf[:, hs] = rs * (don - on * jnp.mean(don * on, axis=-1, keepdims=True))
        first = jnp.where(blk == 0, 0.0, 1.0)
        for c in reversed(range(cpb)):
            sl = slice(c * CHUNK, (c + 1) * CHUNK)
            s_n = st_r[c]
            s_prev = st_r[c - 1] if c > 0 else sp_r[0] * first
            gt = _colsum(la[sl])
            w = jnp.exp(gt - gc[sl])
            k_c = k_r[sl, :]
            kdec = k_c * w
            qs = q_r[sl, :] * GLA_SCALE
            do_c = do_buf[sl, :]
            dq_o[sl, :] = (_dot(do_c, s_n, "nn") * GLA_SCALE).astype(dq_o.dtype)
            d_n = _dot(do_c, qs, "tn") * bd[...] + dcar[...]
            dv_o[sl, :] = _dot(kdec, d_n, "nt").astype(dv_o.dtype)
            dkdec = _dot(v_r[sl, :], d_n, "nn")
            dk_o[sl, :] = (dkdec * w).astype(dk_o.dtype)
            tt = dkdec * kdec
            e = jnp.exp(gt)
            dgt = _colsum(tt) + _colsum(d_n * s_prev) * e
            _, rc = _scan_rows(None, -tt, CHUNK, reverse=True)
            dla_buf[sl, :] = rc + dgt
            dcar[...] = d_n * e
        dpre = dla_buf[...] * _sigmoid(-pre) * (1.0 / GLA_TAU)
        dbg_o[...] += _colsum(dpre)
        dgl_o[...] = _dot(dpre, wg2[...], "nt").astype(dgl_o.dtype)
        dwg2_o[...] += _dot(gl_r[...], dpre, "tn")

    return pl.pallas_call(
        body,
        name="gla_bwd",
        grid=(nb,),
        in_specs=[rrow(GLA_V), rrow(GLA_V)] + _gla_specs(TB, rev_nb=nb)
        + [pl.BlockSpec((cpb, GLA_V, GLA_QK), lambda t: (nb - 1 - t, 0, 0)),
           pl.BlockSpec((1, GLA_V, GLA_QK), lambda t: (jnp.maximum((nb - 1 - t) * cpb - 1, 0), 0, 0))]
        + [_full_spec(v) for v in vecs],
        out_specs=[rrow(GLA_QK), rrow(GLA_QK), rrow(GLA_V), rrow(GLA_V), rrow(LANES),
                   pl.BlockSpec((LANES, GLA_QK), lambda t: (0, 0)), pl.BlockSpec((1, GLA_QK), lambda t: (0, 0)),
                   pl.BlockSpec((1, GLA_V), lambda t: (0, 0))],
        out_shape=[jax.ShapeDtypeStruct((S, GLA_QK), BF16), jax.ShapeDtypeStruct((S, GLA_QK), BF16),
                   jax.ShapeDtypeStruct((S, GLA_V), BF16), jax.ShapeDtypeStruct((S, GLA_V), BF16),
                   jax.ShapeDtypeStruct((S, LANES), BF16), jax.ShapeDtypeStruct((LANES, GLA_QK), F32),
                   jax.ShapeDtypeStruct((1, GLA_QK), F32), jax.ShapeDtypeStruct((1, GLA_V), F32)],
        scratch_shapes=[pltpu.VMEM((GLA_V, GLA_QK), F32), pltpu.VMEM((TB, GLA_V), F32),
                        pltpu.VMEM((TB, GLA_QK), F32)],
        compiler_params=pltpu.CompilerParams(dimension_semantics=("arbitrary",), vmem_limit_bytes=VMEM_LIMIT_BYTES),
    )(dyb, oraw, u, u, u, u, u, states, states, *vecs)


FOX_SCALE = FOX_DH ** -0.5


def _fox_gate_fwd(u, bfp, S):
    T = min(512, S)

    def body(f_r, b_r, fc_o, car):
        t = pl.program_id(0)

        @pl.when(t == 0)
        def _():
            car[...] = jnp.zeros_like(car)

        _, cs = _scan_rows(None, _log_sigmoid(f_r[...] + b_r[...]), T)
        fc_o[...] = cs + car[...]
        car[...] = fc_o[pl.ds(T - 1, 1), :]

    return pl.pallas_call(
        body,
        name="fox_gate_fwd",
        grid=(S // T,),
        in_specs=[pl.BlockSpec((T, LANES), lambda t: (t, U_CF // LANES)), _full_spec(bfp)],
        out_specs=pl.BlockSpec((T, LANES), lambda t: (t, 0)),
        out_shape=jax.ShapeDtypeStruct((S, LANES), F32),
        scratch_shapes=[pltpu.VMEM((1, LANES), F32)],
        compiler_params=pltpu.CompilerParams(dimension_semantics=("arbitrary",), vmem_limit_bytes=VMEM_LIMIT_BYTES),
    )(u, bfp)


def _fox_gate_bwd(dfc, u, bfp, S):
    T = min(512, S)
    nb = S // T

    def body(d_r, f_r, b_r, df_o, db_o, car, tmp):
        t = pl.program_id(0)

        @pl.when(t == 0)
        def _():
            car[...] = jnp.zeros_like(car)
            db_o[...] = jnp.zeros_like(db_o)

        _, rc = _scan_rows(None, d_r[...], T, reverse=True)
        tmp[...] = rc + car[...]
        car[...] = tmp[pl.ds(0, 1), :]
        df = tmp[...] * _sigmoid(-(f_r[...] + b_r[...]))
        df_o[...] = df.astype(df_o.dtype)
        db_o[...] += _colsum(df)

    return pl.pallas_call(
        body,
        name="fox_gate_bwd",
        grid=(nb,),
        in_specs=[pl.BlockSpec((T, LANES), lambda t: (nb - 1 - t, 0)),
                  pl.BlockSpec((T, LANES), lambda t: (nb - 1 - t, U_CF // LANES)), _full_spec(bfp)],
        out_specs=[pl.BlockSpec((T, LANES), lambda t: (nb - 1 - t, 0)), pl.BlockSpec((1, LANES), lambda t: (0, 0))],
        out_shape=[jax.ShapeDtypeStruct((S, LANES), BF16), jax.ShapeDtypeStruct((1, LANES), F32)],
        scratch_shapes=[pltpu.VMEM((1, LANES), F32), pltpu.VMEM((T, LANES), F32)],
        compiler_params=pltpu.CompilerParams(dimension_semantics=("arbitrary",), vmem_limit_bytes=VMEM_LIMIT_BYTES),
    )(dfc, u, bfp)


def _fox_call(name, body, tables, grid, in_specs, out_specs, out_shape, scratch, args, side):
    n_in, n_out, n_scr = len(in_specs), len(out_specs), len(scratch)
    semantics = ("parallel", "arbitrary")
    if side is not None:
        total = grid[0] * grid[1]
        phases = side["phases"]
        triggers = [0, total - 1] if len(phases) == 2 else [0, total * 7 // 10, total - 1]
        na, no = len(side["arrs"]), len(side["out_shapes"])
        s_in, s_out, s_sems = _side_specs(side)
        kernel_body = body

        def body(*refs):
            tabs, rest = refs[:len(tables)], refs[len(tables):]
            ins, s_ins = rest[:n_in], rest[n_in:n_in + na]
            rest = rest[n_in + na:]
            outs, s_outs = rest[:n_out], rest[n_out:n_out + no]
            rest = rest[n_out + no:]
            scr, sems = rest[:n_scr], rest[n_scr:]
            flat = pl.program_id(0) * grid[1] + pl.program_id(1)
            for trigger, phase in zip(triggers[:-1], phases[:-1]):
                @pl.when(flat == trigger)
                def _(phase=phase):
                    phase(s_ins, s_outs, *sems)
            kernel_body(*tabs, *ins, *outs, *scr)

            @pl.when(flat == triggers[-1])
            def _():
                phases[-1](s_ins, s_outs, *sems)

        in_specs, out_specs = in_specs + s_in, out_specs + s_out
        out_shape, scratch = out_shape + side["out_shapes"], scratch + s_sems
        args = list(args) + side["arrs"]
        semantics = ("arbitrary", "arbitrary")
    res = pl.pallas_call(
        body,
        name=name,
        grid_spec=pltpu.PrefetchScalarGridSpec(num_scalar_prefetch=len(tables), grid=grid, in_specs=in_specs,
                                               out_specs=out_specs, scratch_shapes=scratch),
        out_shape=out_shape,
        compiler_params=pltpu.CompilerParams(dimension_semantics=semantics, vmem_limit_bytes=VMEM_LIMIT_BYTES),
    )(*tables, *args)
    return res[:n_out], res[n_out:]


FOX_TILE = 1024
FOX_GROUP = 2
FOX_GROUP_FWD = 8
FOX_AUG = 128
FOX_ONES = 3


def _fox_pairs(n, by_key):
    pairs = [(qi, ki) for qi in range(n) for ki in range(qi + 1)]
    if by_key:
        pairs.sort(key=lambda qk: (qk[1], qk[0]))
    qs = jnp.asarray([qk[0] for qk in pairs], jnp.int32)
    ks = jnp.asarray([qk[1] for qk in pairs], jnp.int32)
    return qs, ks


def _fox_causal(sT):
    keys = lax.broadcasted_iota(jnp.int32, sT.shape, 0)
    queries = lax.broadcasted_iota(jnp.int32, sT.shape, 1)
    return jnp.where(keys <= queries, sT, NEG_BIG)


def _fox_fwd(qT, ka, vT, S, side=None):
    t = min(FOX_TILE, S)
    n = S // t
    qi_tab, ki_tab = _fox_pairs(n, by_key=False)

    G = FOX_GROUP_FWD

    def body(qi_ref, ki_ref, qT_r, ka_r, vT_r, oT_o, lse_o, m_s, l_s, acc):
        step = pl.program_id(1)
        qi, ki = qi_ref[step], ki_ref[step]

        @pl.when(ki == 0)
        def _():
            m_s[...] = jnp.full_like(m_s, NEG_BIG)
            l_s[...] = jnp.zeros_like(l_s)
            acc[...] = jnp.zeros_like(acc)

        def update(g, masked):
            sT = _dot(ka_r[g], qT_r[g], "nn")
            if masked:
                sT = _fox_causal(sT)
            m_new = jnp.maximum(m_s[g], jnp.max(sT, axis=0, keepdims=True))
            p = jnp.exp(sT - m_new)
            alpha = jnp.exp(m_s[g] - m_new)
            l_s[g] = alpha * l_s[g] + jnp.sum(p, axis=0, keepdims=True)
            acc[g] = alpha * acc[g] + _dot(vT_r[g], p, "nn")
            m_s[g] = m_new

        @pl.when(ki < qi)
        def _():
            for g in range(G):
                update(g, False)

        @pl.when(ki == qi)
        def _():
            for g in range(G):
                update(g, True)
                oT_o[g] = acc[g] / l_s[g]
                lse_o[g] = m_s[g] + jnp.log(l_s[g])

    return _fox_call(
        "fox_fwd", body, (qi_tab, ki_tab), (FOX_HEADS // G, int(qi_tab.shape[0])),
        [pl.BlockSpec((G, FOX_AUG, t), lambda h, s, qt, kt: (h, 0, qt[s])),
         pl.BlockSpec((G, t, FOX_AUG), lambda h, s, qt, kt: (h, kt[s], 0)),
         pl.BlockSpec((G, FOX_DH, t), lambda h, s, qt, kt: (h, 0, kt[s]))],
        [pl.BlockSpec((G, FOX_DH, t), lambda h, s, qt, kt: (h, 0, qt[s])),
         pl.BlockSpec((G, 1, t), lambda h, s, qt, kt: (h, 0, qt[s]))],
        [jax.ShapeDtypeStruct((FOX_HEADS, FOX_DH, S), F32), jax.ShapeDtypeStruct((FOX_HEADS, 1, S), F32)],
        [pltpu.VMEM((G, 1, t), F32), pltpu.VMEM((G, 1, t), F32), pltpu.VMEM((G, FOX_DH, t), F32)],
        (qT, ka, vT), side)


FOX_BIAS_ROWS = 8


def _fox_bwd(qT, qa, ka, kT, v, do, doT, oT, lse, S, side=None):
    t = min(FOX_TILE, S)
    n = S // t
    qi_tab, ki_tab = _fox_pairs(n, by_key=True)
    n_steps = int(qi_tab.shape[0])
    slab = slice(FOX_DH, FOX_DH + FOX_BIAS_ROWS)

    G = FOX_GROUP

    def body(qi_ref, ki_ref, qT_r, qa_r, ka_r, kT_r, v_r, do_r, doT_r, oT_r, lse_r,
             dq_o, dfq_o, dk_o, dfk_o, dv_o, dq_acc, dk_acc, dv_acc):
        step = pl.program_id(1)
        qi, ki = qi_ref[step], ki_ref[step]

        @pl.when(step == 0)
        def _():
            dq_acc[...] = jnp.zeros_like(dq_acc)

        @pl.when(qi == ki)
        def _():
            dk_acc[...] = jnp.zeros_like(dk_acc)
            dv_acc[...] = jnp.zeros_like(dv_acc)

        def update(g, masked):
            sT = _dot(ka_r[g], qT_r[g], "nn")
            if masked:
                sT = _fox_causal(sT)
            pT = jnp.exp(sT - lse_r[g])
            delta = jnp.sum(oT_r[g] * doT_r[g], axis=0, keepdims=True)
            dsT = pT * (_dot(v_r[g], doT_r[g], "nn") - delta)
            dv_acc[g] += _dot(pT, do_r[g], "nn")
            dk_acc[g] += _dot(dsT, qa_r[g], "nn")
            dq_acc[g, qi] += _dot(kT_r[g], dsT, "nn")

        @pl.when(qi > ki)
        def _():
            for g in range(G):
                update(g, False)

        @pl.when(qi == ki)
        def _():
            for g in range(G):
                update(g, True)

        @pl.when(qi == n - 1)
        def _():
            for g in range(G):
                dk = dk_acc[g]
                dk_o[g] = dk[:, :FOX_DH].astype(dk_o.dtype)
                dfk_o[g] = dk.T[slab]
                dv_o[g] = dv_acc[g].astype(dv_o.dtype)

        @pl.when(step == n_steps - 1)
        def _():
            for g in range(G):
                for j in range(n):
                    dqT = dq_acc[g, j]
                    dq_o[g, j * t:(j + 1) * t, :] = (dqT.T[:, :FOX_DH] * FOX_SCALE).astype(dq_o.dtype)
                    dfq_o[g, :, j * t:(j + 1) * t] = dqT[slab]

    def qlane(rows):
        return pl.BlockSpec((G, rows, t), lambda h, s, qt, kt: (h, 0, qt[s]))

    def qrow(cols):
        return pl.BlockSpec((G, t, cols), lambda h, s, qt, kt: (h, qt[s], 0))

    def krow(cols):
        return pl.BlockSpec((G, t, cols), lambda h, s, qt, kt: (h, kt[s], 0))

    def klane(rows):
        return pl.BlockSpec((G, rows, t), lambda h, s, qt, kt: (h, 0, kt[s]))

    def head(rows, cols):
        return pl.BlockSpec((G, rows, cols), lambda h, s, qt, kt: (h, 0, 0))

    return _fox_call(
        "fox_bwd", body, (qi_tab, ki_tab), (FOX_HEADS // G, n_steps),
        [qlane(FOX_AUG), qrow(FOX_AUG), krow(FOX_AUG), klane(FOX_AUG), krow(FOX_DH), qrow(FOX_DH), qlane(FOX_DH),
         qlane(FOX_DH), qlane(1)],
        [head(S, FOX_DH), head(FOX_BIAS_ROWS, S), krow(FOX_DH), klane(FOX_BIAS_ROWS), krow(FOX_DH)],
        [jax.ShapeDtypeStruct((FOX_HEADS, S, FOX_DH), BF16), jax.ShapeDtypeStruct((FOX_HEADS, FOX_BIAS_ROWS, S), F32),
         jax.ShapeDtypeStruct((FOX_HEADS, S, FOX_DH), BF16), jax.ShapeDtypeStruct((FOX_HEADS, FOX_BIAS_ROWS, S), F32),
         jax.ShapeDtypeStruct((FOX_HEADS, S, FOX_DH), BF16)],
        [pltpu.VMEM((G, n, FOX_AUG, t), F32), pltpu.VMEM((G, t, FOX_AUG), F32), pltpu.VMEM((G, t, FOX_DH), F32)],
        (qT, qa, ka, kT, v, do, doT, oT, lse), side)


def _fox_prep(u, fcum, S):
    T = min(512, S)
    head_of = jnp.arange(BRANCH) // FOX_DH
    dim_of = jnp.arange(BRANCH) % FOX_DH
    heads = jnp.arange(FOX_HEADS)[:, None, None]
    sel = (head_of[None, :, None] == heads) & (dim_of[None, :, None] == jnp.arange(FOX_AUG)[None, None, :])
    sel_q = (sel * FOX_SCALE).astype(BF16)
    sel_k = sel.astype(BF16)
    sel_vT = jnp.swapaxes(sel[:, :, :FOX_DH], 1, 2).astype(BF16)
    piece = jnp.arange(FOX_ONES * LANES) // LANES
    lane = jnp.arange(FOX_ONES * LANES) % LANES
    col = jnp.arange(FOX_AUG)[None, None, :]
    at_q = (lane[None, :, None] == heads) & (col == FOX_DH + FOX_ONES + piece[None, :, None])
    at_k = (lane[None, :, None] == heads) & (col == FOX_DH + piece[None, :, None])
    bias_q = at_q.astype(BF16)
    bias_k = (-at_k.astype(F32)).astype(BF16)
    cols = jnp.arange(FOX_AUG)[None, :]
    ones_q = ((cols >= FOX_DH) & (cols < FOX_DH + FOX_ONES)).astype(F32)
    ones_k = ((cols >= FOX_DH + FOX_ONES) & (cols < FOX_DH + 2 * FOX_ONES)).astype(F32)
    consts = [sel_q, sel_k, sel_vT, bias_q, bias_k, ones_q, ones_k]

    def body(cq, ck, cv, fc, sq, sk, svT, bq, bk, oq, ok, qa_o, ka_o, qT_o, kT_o, vh_o, vT_o):
        f = fc[...]
        hi = f.astype(BF16).astype(F32)
        mid = (f - hi).astype(BF16).astype(F32)
        lo = (f - hi - mid).astype(BF16).astype(F32)
        pieces = jnp.concatenate([hi, mid, lo], axis=1)
        q, k, v = cq[...], ck[...], cv[...]
        for h in range(FOX_HEADS):
            qa = _dot(q, sq[h], "nn") + _dot(pieces, bq[h], "nn") + oq[...]
            ka = _dot(k, sk[h], "nn") + _dot(pieces, bk[h], "nn") + ok[...]
            qa_o[h] = qa.astype(qa_o.dtype)
            ka_o[h] = ka.astype(ka_o.dtype)
            qT_o[h] = qa.T.astype(qT_o.dtype)
            kT_o[h] = ka.T.astype(kT_o.dtype)
            vT_o[h] = _dot(svT[h], v, "nt").astype(vT_o.dtype)
            vh_o[h] = _dot(v, svT[h], "nt").astype(vh_o.dtype)

    def win(off):
        return pl.BlockSpec((T, BRANCH), functools.partial(lambda i, blk: (i, blk), blk=off // BRANCH))

    def rows(c):
        return pl.BlockSpec((FOX_HEADS, T, c), lambda i: (0, i, 0))

    def lanes(r):
        return pl.BlockSpec((FOX_HEADS, r, T), lambda i: (0, 0, i))

    bf = lambda *shape: jax.ShapeDtypeStruct((FOX_HEADS,) + shape, BF16)
    return pl.pallas_call(
        body,
        name="fox_prep",
        grid=(S // T,),
        in_specs=[win(U_CQ), win(U_CK), win(U_CV), pl.BlockSpec((T, LANES), lambda i: (i, 0))]
        + [_full_spec(c) for c in consts],
        out_specs=[rows(FOX_AUG), rows(FOX_AUG), lanes(FOX_AUG), lanes(FOX_AUG), rows(FOX_DH), lanes(FOX_DH)],
        out_shape=[bf(S, FOX_AUG), bf(S, FOX_AUG), bf(FOX_AUG, S), bf(FOX_AUG, S), bf(S, FOX_DH), bf(FOX_DH, S)],
        compiler_params=pltpu.CompilerParams(dimension_semantics=("parallel",), vmem_limit_bytes=VMEM_LIMIT_BYTES),
    )(u, u, u, fcum, *consts)


def _to_heads(x2d, S):
    return jnp.transpose(x2d.reshape(S, FOX_HEADS, FOX_DH), (1, 0, 2))


def _from_heads(xh, S):
    return jnp.transpose(xh, (1, 0, 2)).reshape(S, FOX_HEADS * FOX_DH)


def _ffn_fwd(tag, x, wgT, wuT, wd, g, b, S, side=None):
    def up_epi(accs):
        gate, up = accs
        sil, _ = _silu_and_grad(gate)
        return [gate, up, sil * up]

    res = _mm(tag + "_up", "nt", [x], [wgT, wuT], [(0, 0, 0), (1, 0, 1)], 2, up_epi, [],
              [BF16, BF16, BF16], S, D_FF, D_MODEL, tn=1408, side=side)
    (gate, up, act), side_out = res if side is not None else (res, None)

    def down_epi(accs, xr, gg, bb):
        z = ALPHA * xr + 0.5 * accs[0]
        return [z, _ln_fwd(z, gg, bb)]

    z, xn = _mm(tag + "_down", "nn", [act], [wd], [(0, 0, 0)], 1, down_epi, [(x, "mn", 0), (g, "n"), (b, "n")],
                [F32, F32], S, D_MODEL, D_FF, tk=D_FF)
    return xn, dict(x=x, gate=gate, up=up, act=act, z=z), side_out


def _ln_bwd_call(tag, dy, z, g, S):
    def fn(dy_t, z_t, g_t):
        dz, xhat = _ln_bwd(dy_t, z_t, g_t)
        return [dz], [_colsum(dy_t * xhat), _colsum(dy_t)]

    (dz,), (dg, db) = _rowwise(tag + "_ln_bwd", fn, [dy, z], [g], [(D_MODEL, F32)], [D_MODEL, D_MODEL], S)
    return dz, dg, db


def _ffn_bwd(tag, dxn, sv, wgT, wuT, wd, g, S, gdt=F32, make_side=None, first_side=None):
    dz, dg, db = _ln_bwd_call(tag, dxn, sv["z"], g, S)

    def act_epi(accs, gate, up):
        da = 0.5 * accs[0]
        sil, dsil = _silu_and_grad(gate.astype(F32))
        return [da * up.astype(F32) * dsil, da * sil]

    res = _mm(tag + "_dact", "nt", [dz], [wd], [(0, 0, 0)], 1, act_epi,
              [(sv["gate"], "mn", 0), (sv["up"], "mn", 0)], [BF16, BF16], S, D_FF, D_MODEL, tn=1408, side=first_side)
    (dgate, dup), first_out = res if first_side is not None else (res, None)
    dwd = _mm1(tag + "_dwd", "tn", sv["act"], dz, D_FF, D_MODEL, S, scale=0.5, tm=1408, out_dtype=gdt)

    def two(accs):
        return [accs[0], accs[1]]

    dwgT, dwuT = _mm(tag + "_dwup", "tn", [dgate, dup], [sv["x"]], [(0, 0, 0), (1, 1, 0)], 2, two, [], [gdt, gdt],
                     D_FF, D_MODEL, S, tm=1408, tk=512)

    def dx_epi(accs, dzr):
        return [accs[0] + ALPHA * dzr]

    grads = dict(w_upT=jnp.concatenate([dwgT, dwuT], axis=0), w_down=dwd, ln_g=dg, ln_b=db)
    side = make_side(grads) if make_side is not None else None
    res = _mm(tag + "_dx", "nn", [dgate, dup], [wgT, wuT], [(0, 0, 0), (0, 1, 1)], 1, dx_epi, [(dz, "mn", 0)],
              [F32], S, D_MODEL, D_FF, tm=1024, tk=1408, side=side)
    (dx,), side_out = res if side is not None else (res, None)
    return dx, grads, (first_out, side_out)


def _mixer_fwd(x1, w, S, side=None, on_side=None):
    u = _mm1("w_in", "nt", x1, w["w_inT_p"], S, U_WIDTH, D_MODEL, tm=1024, tn=1536)
    xc, r, gi, a, h, y_a = _lru_fwd(u, w["lru"], S)
    y_b, oraw, states = _gla_fwd(u, w["gla"], S)
    fcum = _fox_gate_fwd(u, w["bfp"], S)
    qa, ka, qT, kT, vh, vT = _fox_prep(u, fcum, S)
    (oT, lse), side_out = _fox_fwd(qT, ka, vT, S, side)
    if on_side is not None:
        on_side(side_out)
    y_c = jnp.transpose(oT, (2, 0, 1)).reshape(S, BRANCH).astype(BF16)

    def merge_epi(accs, g0, g1, g2):
        merged = _sigmoid(g0) * accs[0] + _sigmoid(g1) * accs[1] + _sigmoid(g2) * accs[2]
        return [accs[0], accs[1], accs[2], merged]

    wb = w["w_branchT"]
    yp0, yp1, yp2, merged = _mm(
        "merge", "nt", [y_a, y_b, y_c], [wb[0], wb[1], wb[2]], [(0, 0, 0), (1, 1, 1), (2, 2, 2)], 3, merge_epi,
        [(u, "mn", 0), (u, "mn", 1), (u, "mn", 2)], [BF16, BF16, BF16, BF16], S, D_MODEL, BRANCH, tm=256)

    def out_epi(accs, xr, gg, bb):
        z = ALPHA * xr + accs[0]
        return [z, _ln_fwd(z, gg, bb)]

    z2, x2 = _mm("w_out", "nn", [merged], [w["w_out"]], [(0, 0, 0)], 1, out_epi,
                 [(x1, "mn", 0), (w["ln2_g"], "n"), (w["ln2_b"], "n")], [F32, F32], S, D_MODEL, D_MODEL)
    sv = dict(x=x1, u=u, xc=xc, r=r, i=gi, a=a, h=h, y_a=y_a, y_b=y_b, y_c=y_c, oraw=oraw,
              states=states, qT=qT, qa=qa, ka=ka, kT=kT, vh=vh, oT=oT, lse=lse, yp=(yp0, yp1, yp2), merged=merged,
              z=z2)
    return x2, sv, side_out


def _mixer_bwd(dx2, sv, w, S, make_side=None, gdt_a=F32, gdt_b=F32):
    u = sv["u"]
    dz, dg2, db2 = _ln_bwd_call("mix", dx2, sv["z"], w["ln2_g"], S)

    def dm_epi(accs, y0, y1, y2, g0, g1, g2):
        dm = accs[0]
        outs_p, outs_g = [], []
        for yp, gl in ((y0, g0), (y1, g1), (y2, g2)):
            sg = _sigmoid(gl)
            outs_p.append(dm * sg)
            outs_g.append(dm * yp.astype(F32) * sg * (1.0 - sg))
        return outs_p + outs_g

    yp = sv["yp"]
    dyp0, dyp1, dyp2, dgl0, dgl1, dgl2 = _mm(
        "dmerged", "nt", [dz], [w["w_out"]], [(0, 0, 0)], 1, dm_epi,
        [(yp[0], "mn", 0), (yp[1], "mn", 0), (yp[2], "mn", 0), (u, "mn", 0), (u, "mn", 1), (u, "mn", 2)],
        [BF16] * 6, S, D_MODEL, D_MODEL, tm=256)
    dw_out = _mm1("dw_out", "tn", sv["merged"], dz, D_MODEL, D_MODEL, S, out_dtype=gdt_b)
    wb = w["w_branchT"]
    dys, dwbs = [], []
    for j, (yj, dyp) in enumerate(((sv["y_a"], dyp0), (sv["y_b"], dyp1), (sv["y_c"], dyp2))):
        dys.append(_mm1("dy_branch%d" % j, "nn", dyp, wb[j], S, BRANCH, D_MODEL))
        dwbs.append(_mm1("dw_branch%d" % j, "tn", dyp, yj, D_MODEL, BRANCH, S, out_dtype=gdt_b))
    day, dxc, dwa, dwx, dba, dbx, dlam = _lru_bwd(dys[0], u, sv, w["lru"], S)
    dax, (dcw0, dcw1, dcw2, dcw3, dcb) = _conv_bwd(dxc, u, w["lru"], S)
    dbq, dbk, dbv, dbr, dglow, dwg2p, dbg, dng = _gla_bwd(dys[1], u, sv["oraw"], sv["states"], w["gla"], S)
    doh = _to_heads(dys[2], S)
    dw_branchT = jnp.stack(dwbs)
    side = make_side(dict(w_out=dw_out, w_branchT=dw_branchT)) if make_side is not None else None
    (dqh, dfq, dkh, dfk, dvh), side_out = _fox_bwd(sv["qT"], sv["qa"], sv["ka"], sv["kT"], sv["vh"], doh,
                                                   jnp.swapaxes(doh, 1, 2), sv["oT"], sv["lse"], S, side)
    dfc = jnp.transpose(dfq[:, FOX_ONES, :] - dfk[:, 0, :])
    dfc = jnp.pad(dfc, ((0, 0), (0, LANES - FOX_HEADS)))
    dcf, dbf = _fox_gate_bwd(dfc, u, w["bfp"], S)
    du = jnp.concatenate(
        [dgl0, dgl1, dgl2, dax, day, dbq, dbk, dbv, dbr, _from_heads(dqh, S).astype(BF16),
         _from_heads(dkh, S).astype(BF16), _from_heads(dvh, S).astype(BF16), dglow, dcf,
         jnp.zeros((S, U_WIDTH - U_CF - LANES), BF16)], axis=1)
    dw_inT_p = _mm1("dw_in", "tn", du, sv["x"], U_WIDTH, D_MODEL, S, tm=1536, out_dtype=gdt_a)

    def dx_epi(accs, dzr):
        return [accs[0] + ALPHA * dzr]

    (dx1,) = _mm("dx_mix", "nn", [du], [w["w_inT_p"]], [(0, 0, 0)], 1, dx_epi, [(dz, "mn", 0)], [F32], S, D_MODEL,
                 U_WIDTH, tm=1024, tk=1536)
    pieces = sorted(W_IN_SEGMENTS)
    dw_inT = jnp.concatenate([dw_inT_p[dst:dst + width] for _, width, dst in pieces], axis=0)
    eye = jnp.eye(LRU_BLOCKS, dtype=F32)
    dwa_b = jnp.einsum("ncmd,nm->ncd", dwa.reshape(LRU_BLOCKS, 64, LRU_BLOCKS, 64), eye)
    dwx_b = jnp.einsum("ncmd,nm->ncd", dwx.reshape(LRU_BLOCKS, 64, LRU_BLOCKS, 64), eye)
    grads = dict(
        w_inT=dw_inT, w_out=dw_out, w_branchT=dw_branchT, ln2_g=dg2, ln2_b=db2,
        conv_w=jnp.concatenate([dcw0, dcw1, dcw2, dcw3], axis=0).astype(gdt_a), conv_b=dcb, lru_wa=dwa_b, lru_wx=dwx_b,
        lru_ba=dba, lru_bx=dbx, lru_lambda=dlam, gla_w_g2=dwg2p[:GLA_LOWRANK].astype(gdt_a), gla_b_g=dbg, gla_norm_g=dng,
        fox_b_f=dbf[:, :FOX_HEADS])
    return dx1, grads, side_out


def _ple_fwd(x3, p_i, w, S):
    pe = _mm1("ple_proj", "nt", p_i, w["ple_w_projT"], S, D_MODEL, PLE_DIM)

    def epi(accs, xr, per, bg, gg, bb):
        sg = _sigmoid(accs[0] + bg)
        z = ALPHA * xr + sg * per
        return [sg, z, _ln_fwd(z, gg, bb)]

    sg, z4, x4 = _mm("ple_gate", "nn", [x3], [w["ple_w_gate"]], [(0, 0, 0)], 1, epi,
                     [(x3, "mn", 0), (pe, "mn", 0), (w["ple_b_gate"], "n"), (w["ln4_g"], "n"), (w["ln4_b"], "n")],
                     [F32, F32, F32], S, D_MODEL, D_MODEL)
    return x4, dict(x=x3, p=p_i, pe=pe, sg=sg, z=z4)


def _ple_bwd(dx4, sv, w, S, gdt=F32):
    def fn(dy_t, z_t, pe_t, sg_t, g_t):
        dz, xhat = _ln_bwd(dy_t, z_t, g_t)
        dgl = dz * pe_t * sg_t * (1.0 - sg_t)
        return [dz, dz * sg_t, dgl], [_colsum(dy_t * xhat), _colsum(dy_t), _colsum(dgl)]

    (dz, dpe, dgl), (dg4, db4, dbg) = _rowwise(
        "ple_bwd", fn, [dx4, sv["z"], sv["pe"], sv["sg"]], [w["ln4_g"]],
        [(D_MODEL, F32), (D_MODEL, BF16), (D_MODEL, BF16)], [D_MODEL] * 3, S)
    dwpT = _mm1("dw_ple_proj", "tn", dpe, sv["p"], D_MODEL, PLE_DIM, S, out_dtype=gdt)
    dwg = _mm1("dw_ple_gate", "tn", sv["x"], dgl, D_MODEL, D_MODEL, S, out_dtype=gdt)

    def dx_epi(accs, dzr):
        return [accs[0] + ALPHA * dzr]

    (dx3,) = _mm("dx_ple", "nt", [dgl], [w["ple_w_gate"]], [(0, 0, 0)], 1, dx_epi, [(dz, "mn", 0)], [F32], S,
                 D_MODEL, D_MODEL)
    return dx3, dict(ple_w_projT=dwpT, ple_w_gate=dwg, ple_b_gate=dbg, ln4_g=dg4, ln4_b=db4)


def _rows_of_all(g):
    return g.reshape((g.shape[0] * g.shape[1],) + g.shape[2:])


EARLY_WEIGHTS = ("ffn1_w_up", "ffn1_w_down", "w_in", "conv_w", "gla_w_g2")


def _ffn_weights(gathered, tag):
    upT = _rows_of_all(gathered[tag + "_w_up"])
    return upT[:D_FF], upT[D_FF:], _rows_of_all(gathered[tag + "_w_down"])


def _late_weights(gathered):
    return dict(ffn2=_ffn_weights(gathered, "ffn2"),
                w_branchT=jnp.moveaxis(gathered["w_branch"], 0, 1).reshape(3, D_MODEL, BRANCH),
                w_out=_rows_of_all(gathered["w_out"]),
                ple_w_projT=_rows_of_all(gathered["ple_w_proj"]),
                ple_w_gate=_rows_of_all(gathered["ple_w_gate"]))


def _w_in_operand(gathered_w_in):
    w_inT = _rows_of_all(gathered_w_in)
    placed = sorted((dst, src, width) for src, width, dst in W_IN_SEGMENTS)
    parts, pos = [], 0
    for dst, src, width in placed:
        if dst > pos:
            parts.append(jnp.zeros((dst - pos, D_MODEL), w_inT.dtype))
        parts.append(w_inT[src:src + width])
        pos = dst + width
    parts.append(jnp.zeros((U_WIDTH - pos, D_MODEL), w_inT.dtype))
    return jnp.concatenate(parts, axis=0)


def _layer_weights(gathered, full, i):
    w = _late_weights(gathered) if "w_out" in gathered else {}
    w["ffn1"] = _ffn_weights(gathered, "ffn1")
    if "w_in" in gathered:
        w["w_inT_p"] = _w_in_operand(gathered["w_in"])
    eye = jnp.eye(LRU_BLOCKS, dtype=F32)

    def dense(blocks):
        return jnp.einsum("ncd,nm->ncmd", blocks, eye).reshape(BRANCH, BRANCH).astype(BF16)

    def vec(name):
        return full[name][i].reshape(1, -1)

    cw = jnp.moveaxis(gathered["conv_w"], 0, 1).reshape(4, BRANCH)
    w_g2 = jnp.moveaxis(gathered["gla_w_g2"], 0, 1).reshape(GLA_LOWRANK, GLA_QK)
    w["lru"] = dict(cw0=cw[0:1], cw1=cw[1:2], cw2=cw[2:3], cw3=cw[3:4], conv_b=vec("conv_b"),
                    wa=dense(full["lru_wa"][i]), wx=dense(full["lru_wx"][i]), ba=vec("lru_ba"), bx=vec("lru_bx"),
                    lam=vec("lru_lambda"))
    hq = jnp.arange(GLA_QK) // GLA_DK
    hv = jnp.arange(GLA_V) // GLA_DV
    w["gla"] = dict(wg2=jnp.pad(w_g2, ((0, LANES - GLA_LOWRANK), (0, 0))).astype(BF16),
                    bg=vec("gla_b_g"), ng=vec("gla_norm_g"), bd=(hv[:, None] == hq[None, :]).astype(F32))
    w["bfp"] = jnp.pad(vec("fox_b_f"), ((0, 0), (0, LANES - FOX_HEADS)))
    for name in ("ln1_g", "ln1_b", "ln2_g", "ln2_b", "ln3_g", "ln3_b", "ln4_g", "ln4_b", "ple_b_gate"):
        w[name] = vec(name)
    return w


def _layer_fwd(x0, p_i, w, S, side=None, on_side=None, first_side=None, on_first=None):
    x1, s1, first_out = _ffn_fwd("ffn1", x0, *w["ffn1"], w["ln1_g"], w["ln1_b"], S, first_side)
    if on_first is not None:
        on_first(first_out)
    x2, s2, side_out = _mixer_fwd(x1, w, S, side, on_side)
    x3, s3, _ = _ffn_fwd("ffn2", x2, *w["ffn2"], w["ln3_g"], w["ln3_b"], S)
    x4, s4 = _ple_fwd(x3, p_i, w, S)
    return x4, (s1, s2, s3, s4), side_out


def _layer_bwd(dx4, saved, w, S, make_side=None, gdt_a=F32, gdt_b=F32, make_first=None, make_last=None):
    s1, s2, s3, s4 = saved
    dx3, g4 = _ple_bwd(dx4, s4, w, S, gdt_b)
    dx2, g3, _ = _ffn_bwd("ffn2", dx3, s3, *w["ffn2"], w["ln3_g"], S, gdt_b)
    late = dict(ffn2_w_upT=g3["w_upT"], ffn2_w_down=g3["w_down"], ple_w_projT=g4["ple_w_projT"],
                ple_w_gate=g4["ple_w_gate"])
    mixer_side = None if make_side is None else (lambda mix: make_side({**late, **mix}))
    dx1, g2, side_out = _mixer_bwd(dx2, s2, w, S, mixer_side, gdt_a, gdt_b)
    last = None if make_last is None else (
        lambda g: make_last({"ffn1_w_upT": g["w_upT"], "ffn1_w_down": g["w_down"]}))
    first = None if make_first is None else make_first(g2)
    dx0, g1, last_out = _ffn_bwd("ffn1", dx1, s1, *w["ffn1"], w["ln1_g"], S, gdt_a, last, first)
    grads = dict(g2)
    grads.update(g4)
    grads.update(late)
    grads.update(ffn1_w_upT=g1["w_upT"], ffn1_w_down=g1["w_down"], ln1_g=g1["ln_g"], ln1_b=g1["ln_b"],
                 ln3_g=g3["ln_g"], ln3_b=g3["ln_b"])
    return dx0, grads, side_out, last_out


def _travel_grads(grads, names):
    return [_dest_pieces(n, grads[n + "T" if n in COLUMN_SHARDED else n]) for n in names]


def _local_step(x, p, target, gathered0, rest, full, overlap):
    S = x.shape[0]
    names = [n for n, _ in SHARDED]
    early = [n for n in names if n in EARLY_WEIGHTS]
    late = [n for n in names if n not in EARLY_WEIGHTS]
    w0 = _layer_weights(gathered0, full, 0)
    if not overlap:
        h, saved0, _ = _layer_fwd(x, p[0], w0, S)
        w1 = _layer_weights(rest, full, 1)
        h, saved1, _ = _layer_fwd(h, p[1], w1, S)
    else:
        w_in0, late0, early1, late1 = rest
        h, saved0, got = _layer_fwd(x, p[0], w0, S, _gather_job(list(late0) + list(early1)),
                                    lambda got: w0.update(_late_weights(dict(zip(late, got[:len(late)])))),
                                    _gather_job([w_in0]), lambda got: w0.update(w_inT_p=_w_in_operand(got[0])))
        w1 = _layer_weights(dict(zip(early, got[len(late):])), full, 1)
        h, saved1, _ = _layer_fwd(h, p[1], w1, S, _gather_job(list(late1)),
                                  lambda got: w1.update(_late_weights(dict(zip(late, got)))))

    def loss_fn(y, t):
        err = y - t
        return [err * (1.0 / D_MODEL)], [_colsum(err * err) * (0.5 / D_MODEL)]

    (dy,), (lsum,) = _rowwise("loss", loss_fn, [h, target], [], [(D_MODEL, F32)], [D_MODEL], S)
    loss = jnp.sum(lsum)
    if not overlap:
        dy, g1, _, _ = _layer_bwd(dy, saved1, w1, S)
        dy, g0, _, _ = _layer_bwd(dy, saved0, w0, S)
        return loss, dy, [g0, g1], {}

    ffn_early = [n for n in early if n.startswith("ffn1")]
    mix_early = [n for n in early if not n.startswith("ffn1")]

    def two_stage(tag, which):
        def make(g):
            dest = _travel_grads(g, which)
            got = _sibling_swap_multi("grad_sibling_swap_" + tag, dest)
            core = lax.axis_index("c").astype(jnp.int32).reshape(1)
            return _chip_job([_pair_add("grad_pair_add_" + n, core, _as_rows(d, 2), _as_rows(a, 1))
                              for n, d, a in zip(which, dest, got)])
        return make

    dy, g1, late1_pieces, _ = _layer_bwd(dy, saved1, w1, S, lambda g: _scatter_job(_travel_grads(g, late)),
                                         BF16, BF16)
    dy, g0, pieces, (mix0_pieces, ffn0_pieces) = _layer_bwd(
        dy, saved0, w0, S, lambda g: _scatter_job(_travel_grads(g1, early) + _travel_grads(g, late)), F32, BF16,
        two_stage("mixer", mix_early), two_stage("ffn", ffn_early))
    exchanged = {(1, n): a for n, a in zip(late, late1_pieces)}
    exchanged.update({(1, n): a for n, a in zip(early, pieces[:len(early)])})
    exchanged.update({(0, n): a for n, a in zip(late, pieces[len(early):])})
    exchanged.update({(0, n): a for n, a in zip(mix_early, mix0_pieces)})
    exchanged.update({(0, n): a for n, a in zip(ffn_early, ffn0_pieces)})
    return loss, dy, [g0, g1], exchanged


def kernel(x, p, ffn1_w_up, ffn1_w_down, ln1_g, ln1_b, w_in, conv_w, conv_b, lru_wa, lru_ba, lru_wx, lru_bx, lru_lambda, gla_w_g2, gla_b_g, gla_norm_g, fox_b_f, w_branch, w_out, ln2_g, ln2_b, ffn2_w_up, ffn2_w_down, ln3_g, ln3_b, ple_w_proj, ple_w_gate, ple_b_gate, ln4_g, ln4_b, loss_target, m_ffn1_w_up, m_ffn1_w_down, m_ln1_g, m_ln1_b, m_w_in, m_conv_w, m_conv_b, m_lru_wa, m_lru_ba, m_lru_wx, m_lru_bx, m_lru_lambda, m_gla_w_g2, m_gla_b_g, m_gla_norm_g, m_fox_b_f, m_w_branch, m_w_out, m_ln2_g, m_ln2_b, m_ffn2_w_up, m_ffn2_w_down, m_ln3_g, m_ln3_b, m_ple_w_proj, m_ple_w_gate, m_ple_b_gate, m_ln4_g, m_ln4_b, v_ffn1_w_up, v_ffn1_w_down, v_ln1_g, v_ln1_b, v_w_in, v_conv_w, v_conv_b, v_lru_wa, v_lru_ba, v_lru_wx, v_lru_bx, v_lru_lambda, v_gla_w_g2, v_gla_b_g, v_gla_norm_g, v_fox_b_f, v_w_branch, v_w_out, v_ln2_g, v_ln2_b, v_ffn2_w_up, v_ffn2_w_down, v_ln3_g, v_ln3_b, v_ple_w_proj, v_ple_w_gate, v_ple_b_gate, v_ln4_g, v_ln4_b):
    env = dict(locals())
    wts = {n: env[n] for n in WEIGHTS}
    ms = {n: env["m_" + n] for n in WEIGHTS}
    vs = {n: env["v_" + n] for n in WEIGHTS}
    sharded = [n for n, _ in SHARDED]

    def travel(n, a):
        return jnp.swapaxes(a, -1, -2) if n in COLUMN_SHARDED else a

    shards = {(i, n): travel(n, wts[n][i]) if n in SHARDED_F32_GATHER else travel(n, wts[n][i]).astype(BF16)
              for i in range(DEPTH) for n in sharded}
    early = [n for n in sharded if n in EARLY_WEIGHTS]
    late = [n for n in sharded if n not in EARLY_WEIGHTS]
    first = [n for n in early if n != "w_in"]
    gathered0 = dict(zip(first, _allgather_multi("gather_weights", [shards[0, n] for n in first])))
    rest = (shards[0, "w_in"], [shards[0, n] for n in late], [shards[1, n] for n in early],
            [shards[1, n] for n in late])
    full = {n: wts[n] for n in REPLICATED}

    loss_part, grad_x, layer_grads, pieces = _local_step(x[0], p[:, 0], loss_target[0], gathered0, rest, full, True)
    loss = lax.psum(loss_part, MESH_AXES)

    rep = list(REPLICATED)
    rep_grads = [jnp.stack([layer_grads[i][n] for i in range(DEPTH)]).reshape(wts[n].shape) for n in rep]
    (gr,) = _allgather_multi("grad_gather_replicated", [_pack(rep_grads, F32)])

    kinds = ("grad", "delta", "new_m", "new_v")
    out = {}
    for n in sharded:
        local = [_as_rows(travel(n, pieces[i, n].reshape((-1,) + shards[i, n].shape)), 1) for i in range(DEPTH)]
        res = _adamw("adamw_" + n, local, _as_rows(wts[n], 1), _as_rows(ms[n], 1), _as_rows(vs[n], 1))
        for kind, arr in zip(kinds, res):
            out[kind + "_" + n] = arr.reshape(wts[n].shape)
    res = _adamw("adamw_replicated", [gr], _pack([wts[n] for n in rep], F32)[None],
                 _pack([ms[n] for n in rep], F32)[None], _pack([vs[n] for n in rep], F32)[None])
    shapes = [wts[n].shape for n in rep]
    for kind, buf in zip(kinds, res):
        for n, arr in zip(rep, _unpack(buf[0], shapes)):
            out[kind + "_" + n] = arr
    return (loss, grad_x[None], *[out["grad_" + n] for n in WEIGHTS], *[out["delta_" + n] for n in WEIGHTS],
            *[out["new_m_" + n] for n in WEIGHTS], *[out["new_v_" + n] for n in WEIGHTS])
```

```python
import functools
import math

import jax
import jax.numpy as jnp
from jax import lax
from jax.experimental import pallas as pl
from jax.experimental.pallas import tpu as pltpu

F32 = jnp.float32
BF16 = jnp.bfloat16

N_DEV = 8
MESH_AXES = ("x", "y", "c")
DEPTH = 2
D_MODEL = 1024
D_FF = 2816
BRANCH = 512
CHUNK = 64
GLA_HEADS = 4
GLA_DK = 64
GLA_DV = 128
GLA_LOWRANK = 16
GLA_TAU = 16.0
FOX_HEADS = 8
FOX_DH = 64
PLE_DIM = 256
LRU_C = 8.0
LRU_BLOCKS = 8
LN_EPS = 1e-5
RMS_EPS = 1e-6
ALPHA = (2 * DEPTH) ** 0.25
LANES = 128
NEG_BIG = -1e30

ADAM_LR = 0.001
ADAM_B1 = 0.9
ADAM_B2 = 0.999
ADAM_EPS = 1e-08
ADAM_WD = 0.01
ADAM_STEP = 10

VMEM_LIMIT_BYTES = 56 * 1024 * 1024

U_GATES = 0
U_AX = 3072
U_AY = 3584
U_BQ = 4096
U_BK = 4352
U_BV = 4608
U_BR = 5120
U_CQ = 5632
U_CK = 6144
U_CV = 6656
U_BLOW = 7168
U_CF = 7296
U_WIDTH = 7680
W_IN_SEGMENTS = (
    (0, 512, U_AX), (512, 512, U_AY), (1024, 256, U_BQ), (1280, 256, U_BK), (1536, 512, U_BV),
    (2048, 16, U_BLOW), (2064, 512, U_BR), (2576, 512, U_CQ), (3088, 512, U_CK), (3600, 512, U_CV),
    (4112, 8, U_CF), (4120, 3072, U_GATES),
)

SHARDED = (
    ("ffn1_w_up", 2), ("ffn1_w_down", 1), ("w_in", 2), ("conv_w", 2), ("gla_w_g2", 2), ("w_branch", 3),
    ("w_out", 1), ("ffn2_w_up", 2), ("ffn2_w_down", 1), ("ple_w_proj", 2), ("ple_w_gate", 1),
)
SHARDED_F32_GATHER = ("conv_w", "gla_w_g2")
COLUMN_SHARDED = ("ffn1_w_up", "ffn2_w_up", "w_in", "w_branch", "ple_w_proj")
REPLICATED = ("ln1_g", "ln1_b", "conv_b", "lru_wa", "lru_ba", "lru_wx", "lru_bx", "lru_lambda", "gla_b_g",
              "gla_norm_g", "fox_b_f", "ln2_g", "ln2_b", "ln3_g", "ln3_b", "ple_b_gate", "ln4_g", "ln4_b")
WEIGHTS = ("ffn1_w_up", "ffn1_w_down", "ln1_g", "ln1_b", "w_in", "conv_w", "conv_b", "lru_wa", "lru_ba", "lru_wx",
           "lru_bx", "lru_lambda", "gla_w_g2", "gla_b_g", "gla_norm_g", "fox_b_f", "w_branch", "w_out", "ln2_g",
           "ln2_b", "ffn2_w_up", "ffn2_w_down", "ln3_g", "ln3_b", "ple_w_proj", "ple_w_gate", "ple_b_gate", "ln4_g",
           "ln4_b")


def _sigmoid(x):
    return 1.0 / (1.0 + jnp.exp(-x))


def _log1p_pos(e):
    return jnp.where(e < 1e-4, e * (1.0 - 0.5 * e), jnp.log(1.0 + e))


def _softplus(x):
    return jnp.maximum(x, 0.0) + _log1p_pos(jnp.exp(-jnp.abs(x)))


def _log_sigmoid(x):
    return -_softplus(-x)


def _neg_expm1(y):
    series = -y * (1.0 + y * (0.5 + y * (1.0 / 6.0 + y * (1.0 / 24.0 + y * (1.0 / 120.0)))))
    return jnp.where(y > -0.1, series, 1.0 - jnp.exp(y))


def _silu_and_grad(x):
    s = _sigmoid(x)
    return x * s, s * (1.0 + x * (1.0 - s))


_GELU_C = math.sqrt(2.0 / math.pi)


def _gelu_and_grad(x):
    inner = _GELU_C * (x + 0.044715 * x * x * x)
    t = jnp.tanh(inner)
    g = 0.5 * x * (1.0 + t)
    dg = 0.5 * (1.0 + t) + 0.5 * x * (1.0 - t * t) * _GELU_C * (1.0 + 3.0 * 0.044715 * x * x)
    return g, dg


def _ln_stats(z):
    mu = jnp.mean(z, axis=-1, keepdims=True)
    zc = z - mu
    var = jnp.mean(zc * zc, axis=-1, keepdims=True)
    rstd = lax.rsqrt(var + LN_EPS)
    return zc * rstd, rstd


def _ln_fwd(z, g, b):
    xhat, _ = _ln_stats(z)
    return xhat * g + b


def _ln_bwd(dy, z, g):
    xhat, rstd = _ln_stats(z)
    dxh = dy * g
    m1 = jnp.mean(dxh, axis=-1, keepdims=True)
    m2 = jnp.mean(dxh * xhat, axis=-1, keepdims=True)
    return rstd * (dxh - m1 - xhat * m2), xhat


def _colsum(x):
    return jnp.sum(x, axis=0, keepdims=True)


def _dot(a, b, dims):
    dn = {"nn": (((1,), (0,)), ((), ())), "nt": (((1,), (1,)), ((), ())), "tn": (((0,), (0,)), ((), ()))}[dims]
    return lax.dot_general(a.astype(BF16), b.astype(BF16), dn, preferred_element_type=F32)


def _scan_rows(a, b, length, reverse=False, seg=None):
    rows = lax.broadcasted_iota(jnp.int32, b.shape, 0)
    span = seg if seg else length
    pos = rows % span if seg else rows
    d = 1
    while d < span:
        shift = (length - d) if reverse else d
        valid = (pos < span - d) if reverse else (pos >= d)
        sb = jnp.where(valid, pltpu.roll(b, shift, 0), 0.0)
        if a is None:
            b = b + sb
        else:
            b = b + a * sb
            a = a * jnp.where(valid, pltpu.roll(a, shift, 0), 1.0)
        d *= 2
    return a, b


def _tile(dim, pref):
    if dim <= pref:
        return dim
    best = None
    t = LANES
    while t <= pref:
        if dim % t == 0:
            best = t
        t += LANES
    assert best is not None, (dim, pref)
    return best


def _full_spec(arr):
    nd = arr.ndim
    return pl.BlockSpec(arr.shape, lambda *_: (0,) * nd)


def _mm(name, dims, a_ops, b_ops, terms, n_acc, epilogue, extras, out_dtypes, M, N, K, tm=512, tn=1024, tk=1024,
        side=None):
    tm, tn, tk = _tile(M, tm), _tile(N, tn), _tile(K, tk)
    gm, gn, gk = M // tm, N // tn, K // tk
    a_bytes = sum(a.size * a.dtype.itemsize for a in a_ops)
    b_bytes = sum(b.size * b.dtype.itemsize for b in b_ops)
    n_outer = gk == 1 and b_bytes + a_bytes * gn < a_bytes + b_bytes * gm

    def spec(shape, fn):
        if n_outer:
            return pl.BlockSpec(shape, lambda j, i, k: fn(i, j, k))
        return pl.BlockSpec(shape, fn)

    if dims == "tn":
        a_spec = spec((tk, tm), lambda i, j, k: (k, i))
    else:
        a_spec = spec((tm, tk), lambda i, j, k: (i, k))
    if dims == "nt":
        b_spec = spec((tn, tk), lambda i, j, k: (j, k))
    else:
        b_spec = spec((tk, tn), lambda i, j, k: (k, j))
    e_specs, e_arrays = [], []
    for ex in extras:
        if ex[1] == "mn":
            e_specs.append(spec((tm, tn), functools.partial(lambda i, j, k, off: (i, j + off), off=ex[2])))
        else:
            e_specs.append(spec((1, tn), lambda i, j, k: (0, j)))
        e_arrays.append(ex[0])
    na, nb, ne, no = len(a_ops), len(b_ops), len(extras), len(out_dtypes)
    grid = (gn, gm, gk) if n_outer else (gm, gn, gk)
    s_in, s_out, s_sems = _side_specs(side) if side is not None else ([], [], [])
    nsi, nso = len(s_in), len(s_out)
    total = grid[0] * grid[1] * grid[2]

    def body(*refs):
        a_refs = refs[:na]
        b_refs = refs[na:na + nb]
        e_refs = refs[na + nb:na + nb + ne]
        pos = na + nb + ne
        s_ins, o_refs = refs[pos:pos + nsi], refs[pos + nsi:pos + nsi + no]
        pos += nsi + no
        s_outs, acc_refs, sems = refs[pos:pos + nso], refs[pos + nso:pos + nso + n_acc], refs[pos + nso + n_acc:]
        k = pl.program_id(2)
        if side is not None:
            phases = side["phases"]
            triggers = [0, total - 1] if len(phases) == 2 else [0, total * 7 // 10, total - 1]
            flat = (pl.program_id(0) * grid[1] + pl.program_id(1)) * grid[2] + k
            for trigger, phase in zip(triggers[:-1], phases[:-1]):
                @pl.when(flat == trigger)
                def _(phase=phase):
                    phase(s_ins, s_outs, *sems)

        @pl.when(k == 0)
        def _():
            for acc in acc_refs:
                acc[...] = jnp.zeros_like(acc)

        for r, ai, bi in terms:
            acc_refs[r][...] += _dot(a_refs[ai][...], b_refs[bi][...], dims)

        @pl.when(k == gk - 1)
        def _():
            res = epilogue([acc[...] for acc in acc_refs], *[e[...] for e in e_refs])
            for o, val in zip(o_refs, res):
                o[...] = val.astype(o.dtype)

        if side is not None:
            @pl.when(flat == triggers[-1])
            def _():
                phases[-1](s_ins, s_outs, *sems)

    outs = pl.pallas_call(
        body,
        name=name,
        grid=grid,
        in_specs=[a_spec] * na + [b_spec] * nb + e_specs + s_in,
        out_specs=[spec((tm, tn), lambda i, j, k: (i, j))] * no + s_out,
        out_shape=[jax.ShapeDtypeStruct((M, N), dt) for dt in out_dtypes] + (side["out_shapes"] if side else []),
        scratch_shapes=[pltpu.VMEM((tm, tn), F32)] * n_acc + s_sems,
        compiler_params=pltpu.CompilerParams(
            dimension_semantics=("arbitrary",) * 3 if side is not None else ("parallel", "parallel", "arbitrary"),
            vmem_limit_bytes=VMEM_LIMIT_BYTES),
    )(*a_ops, *b_ops, *e_arrays, *(side["arrs"] if side else []))
    return outs if side is None else (outs[:no], outs[no:])


def _mm1(name, dims, a, b, M, N, K, out_dtype=F32, scale=None, **kw):
    def epi(accs):
        return [accs[0] if scale is None else accs[0] * scale]
    return _mm(name, dims, [a], [b], [(0, 0, 0)], 1, epi, [], [out_dtype], M, N, K, **kw)[0]


def _rowwise(name, fn, row_ins, vec_ins, row_outs, sum_outs, S, tr=512, reverse=False):
    tr = min(tr, S)
    g = S // tr
    rmap = (lambda i: (g - 1 - i)) if reverse else (lambda i: i)
    in_specs, arrays = [], []
    for r in row_ins:
        if isinstance(r, tuple):
            arr, width, blk = r
            in_specs.append(pl.BlockSpec((tr, width), functools.partial(lambda i, blk: (rmap(i), blk), blk=blk)))
        else:
            arr = r
            in_specs.append(pl.BlockSpec((tr, arr.shape[1]), lambda i: (rmap(i), 0)))
        arrays.append(arr)
    for v in vec_ins:
        in_specs.append(_full_spec(v))
        arrays.append(v)
    nr, nv, no, ns = len(row_ins), len(vec_ins), len(row_outs), len(sum_outs)

    def body(*refs):
        ins = [r[...] for r in refs[:nr + nv]]
        o_refs = refs[nr + nv:nr + nv + no]
        s_refs = refs[nr + nv + no:]
        outs, sums = fn(*ins)
        for o, val in zip(o_refs, outs):
            o[...] = val.astype(o.dtype)
        if ns:
            i = pl.program_id(0)

            @pl.when(i == 0)
            def _():
                for s, val in zip(s_refs, sums):
                    s[...] = val

            @pl.when(i > 0)
            def _():
                for s, val in zip(s_refs, sums):
                    s[...] += val

    res = pl.pallas_call(
        body,
        name=name,
        grid=(g,),
        in_specs=in_specs,
        out_specs=[pl.BlockSpec((tr, c), lambda i: (rmap(i), 0)) for c, _ in row_outs]
        + [pl.BlockSpec((1, c), lambda i: (0, 0)) for c in sum_outs],
        out_shape=[jax.ShapeDtypeStruct((S, c), dt) for c, dt in row_outs]
        + [jax.ShapeDtypeStruct((1, c), F32) for c in sum_outs],
        compiler_params=pltpu.CompilerParams(
            dimension_semantics=("arbitrary",), vmem_limit_bytes=VMEM_LIMIT_BYTES),
    )(*arrays)
    return res[:no], res[no:]


MESH_ID = pl.DeviceIdType.MESH


def _remote(src, dst, send_sem, recv_sem, to):
    return pltpu.make_async_remote_copy(src_ref=src, dst_ref=dst, send_sem=send_sem, recv_sem=recv_sem,
                                        device_id=to, device_id_type=MESH_ID)


def _hbm_call(name, body, arrs, out_shapes, n_send, n_recv, n_local):
    return pl.pallas_call(
        body,
        name=name,
        in_specs=[pl.BlockSpec(memory_space=pltpu.HBM)] * len(arrs),
        out_specs=[pl.BlockSpec(memory_space=pltpu.HBM)] * len(out_shapes),
        out_shape=out_shapes,
        scratch_shapes=[pltpu.SemaphoreType.DMA((n_send,)), pltpu.SemaphoreType.DMA((n_recv,)),
                        pltpu.SemaphoreType.DMA((n_local,))],
        compiler_params=pltpu.CompilerParams(has_side_effects=True),
    )(*arrs)


def _side_job(arrs, out_shapes, n_send, n_recv, n_local, phases):
    return dict(arrs=list(arrs), out_shapes=list(out_shapes), sems=(n_send, n_recv, n_local), phases=phases)


def _side_specs(side):
    hbm = pl.BlockSpec(memory_space=pltpu.HBM)
    sems = [pltpu.SemaphoreType.DMA((k,)) for k in side["sems"]]
    return [hbm] * len(side["arrs"]), [hbm] * len(side["out_shapes"]), sems


def _gather_job(arrs):
    n = len(arrs)

    def plan(ins, outs, send_sems, recv_sems, local_sems):
        x, y, c = lax.axis_index("x"), lax.axis_index("y"), lax.axis_index("c")
        me, sibling = (x, y, c), (x, y, 1 - c)
        chips = [(1 - x, y), (x, 1 - y), (1 - x, 1 - y)]

        def slot(i, dev):
            return outs[i].at[4 * dev[0] + 2 * dev[1] + dev[2]]

        def copy(i, k, block, to, src=None):
            dst = slot(i, block)
            return _remote(dst if src is None else src, dst, send_sems.at[7 * i + k], recv_sems.at[7 * i + k], to)

        mine = [pltpu.make_async_copy(ins[i], slot(i, me), local_sems.at[i]) for i in range(n)]
        first = []
        for i in range(n):
            first.append(copy(i, 0, me, sibling, src=ins[i]))
            first += [copy(i, 1 + j, me, (*chip, c), src=ins[i]) for j, chip in enumerate(chips)]
        arrive = [[copy(i, 1 + j, (*chip, c), me) for i in range(n)] for j, chip in enumerate(chips)]
        passed = [[copy(i, 4 + j, (*chip, c), sibling) for i in range(n)] for j, chip in enumerate(chips)]
        last = [copy(i, 0, sibling, me) for i in range(n)]
        last += [copy(i, 4 + j, (*chip, 1 - c), me) for i in range(n) for j, chip in enumerate(chips)]
        return mine, first, arrive, passed, last

    def start(*refs):
        mine, first, _, _, _ = plan(*refs)
        for cp in mine + first:
            cp.start()

    def forward(*refs):
        _, _, arrive, passed, _ = plan(*refs)
        for came, onward in zip(arrive, passed):
            for a, p in zip(came, onward):
                a.wait_recv()
                p.start()

    def finish(*refs):
        mine, first, _, passed, last = plan(*refs)
        for cp in last:
            cp.wait_recv()
        for cp in first + [p for onward in passed for p in onward]:
            cp.wait_send()
        for cp in mine:
            cp.wait()

    outs = [jax.ShapeDtypeStruct((N_DEV,) + a.shape, a.dtype) for a in arrs]
    return _side_job(arrs, outs, 7 * n, 7 * n, n, [start, forward, finish])


def _scatter_job(arrs):
    n = len(arrs)

    def plan(ins, outs, send_sems, recv_sems, local_sems):
        x, y, c = lax.axis_index("x"), lax.axis_index("y"), lax.axis_index("c")
        here = 2 * x + y
        local = [pltpu.make_async_copy(ins[i].at[here, c], outs[i].at[here, c], local_sems.at[i]) for i in range(n)]
        sends, recvs = [], []
        for i in range(n):
            for k in range(1, N_DEV):
                px = 1 - x if k & 4 else x
                py = 1 - y if k & 2 else y
                pc = 1 - c if k & 1 else c
                sems = (send_sems.at[7 * i + k - 1], recv_sems.at[7 * i + k - 1], (px, py, pc))
                sends.append(_remote(ins[i].at[2 * px + py, pc], outs[i].at[here, c], *sems))
                recvs.append(_remote(ins[i].at[2 * px + py, pc], outs[i].at[2 * px + py, pc], *sems))
        return local, sends, recvs

    def start(*refs):
        local, sends, _ = plan(*refs)
        for cp in local + sends:
            cp.start()

    def finish(*refs):
        local, sends, recvs = plan(*refs)
        for cp in recvs:
            cp.wait_recv()
        for cp in sends:
            cp.wait_send()
        for cp in local:
            cp.wait()

    outs = [jax.ShapeDtypeStruct(a.shape, a.dtype) for a in arrs]
    return _side_job(arrs, outs, 7 * n, 7 * n, n, [start, finish])


def _run_job(name, job):
    na, no = len(job["arrs"]), len(job["out_shapes"])

    def body(*refs):
        ins, outs, sems = refs[:na], refs[na:na + no], refs[na + no:]
        for phase in job["phases"]:
            phase(ins, outs, *sems)

    return _hbm_call(name, body, job["arrs"], job["out_shapes"], *job["sems"])


def _allgather_multi(name, arrs):
    return _run_job(name, _gather_job(arrs))


def _sibling_swap_multi(name, arrs):
    n = len(arrs)
    per = 4

    def body(*refs):
        ins, got = refs[:n], refs[n:2 * n]
        send_sems, recv_sems, _ = refs[2 * n:]
        x, y, c = lax.axis_index("x"), lax.axis_index("y"), lax.axis_index("c")
        sibling = (x, y, 1 - c)
        sends = []
        for i in range(n):
            for a in range(4):
                k = per * i + a
                sends.append(_remote(ins[i].at[a, 1 - c], got[i].at[a], send_sems.at[k], recv_sems.at[k], sibling))
        for cp in sends:
            cp.start()
        for cp in sends:
            cp.wait_recv()
        for cp in sends:
            cp.wait_send()

    outs = [jax.ShapeDtypeStruct((4,) + a.shape[2:], a.dtype) for a in arrs]
    return _hbm_call(name, body, arrs, outs, per * n, per * n, 1)


def _chip_job(arrs):
    n = len(arrs)

    def plan(ins, outs, send_sems, recv_sems, local_sems):
        x, y, c = lax.axis_index("x"), lax.axis_index("y"), lax.axis_index("c")
        mine = 2 * x + y
        chips = [(1 - x, y), (x, 1 - y), (1 - x, 1 - y)]
        local = [pltpu.make_async_copy(ins[i].at[mine], outs[i].at[mine], local_sems.at[i]) for i in range(n)]
        sends, recvs = [], []
        for i in range(n):
            for j, (px, py) in enumerate(chips):
                peer = 2 * px + py
                sems = (send_sems.at[3 * i + j], recv_sems.at[3 * i + j], (px, py, c))
                sends.append(_remote(ins[i].at[peer], outs[i].at[mine], *sems))
                recvs.append(_remote(ins[i].at[peer], outs[i].at[peer], *sems))
        return local, sends, recvs

    def start(*refs):
        local, sends, _ = plan(*refs)
        for cp in local + sends:
            cp.start()

    def finish(*refs):
        local, sends, recvs = plan(*refs)
        for cp in recvs:
            cp.wait_recv()
        for cp in sends:
            cp.wait_send()
        for cp in local:
            cp.wait()

    outs = [jax.ShapeDtypeStruct(a.shape, a.dtype) for a in arrs]
    return _side_job(arrs, outs, 3 * n, 3 * n, n, [start, finish])


def _as_rows(a, lead):
    return a.reshape(a.shape[:lead] + (-1, a.shape[-1]))


def _row_tile(rows, cols, parts):
    budget = 4 * 1024 * 1024 // (4 * max(cols, LANES) * parts)
    return _tile_rows(rows, max(8, min(512, budget // 8 * 8)))


def _pair_add(name, core, both, got):
    _, rows, cols = got.shape
    tr = _row_tile(rows, cols, 2)

    def body(c_ref, a_ref, b_ref, o_ref):
        o_ref[...] = (a_ref[...] + b_ref[...]).astype(o_ref.dtype)

    blk = pl.BlockSpec((1, tr, cols), lambda ch, i, c_ref: (ch, i, 0))
    return pl.pallas_call(
        body, name=name,
        grid_spec=pltpu.PrefetchScalarGridSpec(
            num_scalar_prefetch=1, grid=(4, rows // tr),
            in_specs=[pl.BlockSpec((1, None, tr, cols), lambda ch, i, c_ref: (ch, c_ref[0], i, 0)), blk],
            out_specs=blk),
        out_shape=jax.ShapeDtypeStruct(got.shape, BF16),
        compiler_params=pltpu.CompilerParams(dimension_semantics=("parallel", "parallel"),
                                             vmem_limit_bytes=VMEM_LIMIT_BYTES),
    )(core, both, got)


def _adamw(name, gparts, w, m, v):
    layers = len(gparts)
    _, rows, cols = gparts[0].shape
    tr = _row_tile(rows, cols, sum(gp.shape[0] for gp in gparts))
    c1 = 1.0 / (1.0 - ADAM_B1 ** ADAM_STEP)
    c2 = 1.0 / (1.0 - ADAM_B2 ** ADAM_STEP)

    def body(*refs):
        gp_refs = refs[:layers]
        w_ref, m_ref, v_ref, g_ref, d_ref, nm_ref, nv_ref = refs[layers:]
        layer = pl.program_id(0)
        g = None
        for k, gp_ref in enumerate(gp_refs):
            gk = gp_ref[0].astype(F32)
            for i in range(1, gp_ref.shape[0]):
                gk = gk + gp_ref[i].astype(F32)
            g = gk if g is None else jnp.where(layer == k, gk, g)
        nm = ADAM_B1 * m_ref[...] + (1.0 - ADAM_B1) * g
        nv = ADAM_B2 * v_ref[...] + (1.0 - ADAM_B2) * (g * g)
        m_hat = nm * c1
        v_hat = nv * c2
        g_ref[...] = g
        nm_ref[...] = nm
        nv_ref[...] = nv
        d_ref[...] = -ADAM_LR * (m_hat / (jnp.sqrt(v_hat) + ADAM_EPS) + ADAM_WD * w_ref[...])

    row = pl.BlockSpec((None, tr, cols), lambda l, i: (l, i, 0))
    return pl.pallas_call(
        body,
        name=name,
        grid=(layers, rows // tr),
        in_specs=[pl.BlockSpec((gp.shape[0], tr, cols), lambda l, i: (0, i, 0)) for gp in gparts] + [row, row, row],
        out_specs=[row] * 4,
        out_shape=[jax.ShapeDtypeStruct((layers, rows, cols), F32)] * 4,
        compiler_params=pltpu.CompilerParams(dimension_semantics=("parallel", "parallel"),
                                             vmem_limit_bytes=VMEM_LIMIT_BYTES),
    )(*gparts, w, m, v)


def _tile_rows(rows, pref):
    t = min(pref, rows) // 8 * 8
    while t >= 8 and rows % t:
        t -= 8
    return t if t >= 8 else rows


PACK_ROWS = 512


def _pack(arrs, dtype):
    flat = jnp.concatenate([a.astype(dtype).reshape(-1) for a in arrs])
    quantum = PACK_ROWS * LANES
    padded = -(-flat.shape[0] // quantum) * quantum
    return jnp.pad(flat, (0, padded - flat.shape[0])).reshape(-1, LANES)


def _unpack(buf, shapes, lead=()):
    flat = buf.reshape(lead + (-1,))
    out, off = [], 0
    for shp in shapes:
        n = math.prod(shp)
        out.append(flat[..., off:off + n].reshape(lead + tuple(shp)))
        off += n
    return out


def _dest_pieces(name, g):
    if name == "w_branch":
        return jnp.moveaxis(g.reshape(3, 4, 2, D_MODEL // N_DEV, BRANCH), 0, 2)
    if name in SHARDED_F32_GATHER:
        return jnp.moveaxis(g.reshape(g.shape[0], 4, 2, -1), 0, 2)
    return g.reshape((4, 2, g.shape[0] // N_DEV) + g.shape[1:])


HALO = 8
LRU_TILE = 256
CONV_TILE = 512


def _rows_down(x, prev, k):
    xs = pltpu.roll(x, k, 0)
    row = lax.broadcasted_iota(jnp.int32, prev.shape, 0)
    top = jnp.where(row < k, pltpu.roll(prev, k, 0), xs[:HALO])
    return jnp.concatenate([top, xs[HALO:]], axis=0)


def _rows_up(x, nxt, k):
    rows = x.shape[0]
    xs = pltpu.roll(x, rows - k, 0)
    row = lax.broadcasted_iota(jnp.int32, nxt.shape, 0)
    bottom = jnp.where(row >= HALO - k, pltpu.roll(nxt, HALO - k, 0), xs[rows - HALO:])
    return jnp.concatenate([xs[:rows - HALO], bottom], axis=0)


def _halo_before(T, block_of, col=0):
    per = T // HALO
    return pl.BlockSpec((HALO, BRANCH), lambda t: (jnp.maximum(block_of(t) * per - 1, 0), col))


def _halo_after(T, block_of, S, col=0):
    per = T // HALO
    return pl.BlockSpec((HALO, BRANCH), lambda t: (jnp.minimum((block_of(t) + 1) * per, S // HALO - 1), col))


def _lru_fwd(u, lw, S):
    T = min(LRU_TILE, S)
    nb = S // T
    row = pl.BlockSpec((T, BRANCH), lambda t: (t, 0))
    vecs = [lw["cw0"], lw["cw1"], lw["cw2"], lw["cw3"], lw["conv_b"], lw["wa"], lw["wx"], lw["ba"], lw["bx"],
            lw["lam"]]

    def body(ax, ax_before, ay, cw0, cw1, cw2, cw3, cb, wa, wx, ba, bx, lam, xc_o, r_o, i_o, a_o, h_o, ya_o, hc):
        t = pl.program_id(0)

        @pl.when(t == 0)
        def _():
            hc[...] = jnp.zeros_like(hc)

        x = ax[...]
        before = jnp.where(t == 0, 0.0, ax_before[...])
        xc = (cw3[...] * x + cw2[...] * _rows_down(x, before, 1) + cw1[...] * _rows_down(x, before, 2)
              + cw0[...] * _rows_down(x, before, 3) + cb[...])
        r = _sigmoid(_dot(xc, wa[...], "nn") + ba[...])
        gi = _sigmoid(_dot(xc, wx[...], "nn") + bx[...])
        sp = _softplus(-lam[...])
        la = -LRU_C * r * sp
        a = jnp.exp(la)
        mult = jnp.sqrt(_neg_expm1(2.0 * la))
        A, B = _scan_rows(a, mult * gi * xc, T)
        h = B + A * hc[...]
        h_o[...] = h
        hc[...] = h_o[pl.ds(T - 1, 1), :]
        xc_o[...] = xc
        r_o[...] = r
        i_o[...] = gi
        a_o[...] = a
        gy, _ = _gelu_and_grad(ay[...])
        ya_o[...] = (gy * h).astype(ya_o.dtype)

    outs = pl.pallas_call(
        body,
        name="lru_fwd",
        grid=(nb,),
        in_specs=[pl.BlockSpec((T, BRANCH), lambda t: (t, U_AX // BRANCH)),
                  _halo_before(T, lambda t: t, U_AX // BRANCH),
                  pl.BlockSpec((T, BRANCH), lambda t: (t, U_AY // BRANCH))] + [_full_spec(v) for v in vecs],
        out_specs=[row] * 6,
        out_shape=[jax.ShapeDtypeStruct((S, BRANCH), F32)] * 5 + [jax.ShapeDtypeStruct((S, BRANCH), BF16)],
        scratch_shapes=[pltpu.VMEM((1, BRANCH), F32)],
        compiler_params=pltpu.CompilerParams(dimension_semantics=("arbitrary",), vmem_limit_bytes=VMEM_LIMIT_BYTES),
    )(u, u, u, *vecs)
    return outs


def _lru_bwd(dya, u, sv, lw, S):
    T = min(LRU_TILE, S)
    nb = S // T
    rrow = pl.BlockSpec((T, BRANCH), lambda t: (nb - 1 - t, 0))
    sq = pl.BlockSpec((BRANCH, BRANCH), lambda t: (0, 0))
    vrow = pl.BlockSpec((1, BRANCH), lambda t: (0, 0))

    def block(t):
        return nb - 1 - t

    def body(dya_r, ay, h, h_before, xc_r, r_r, i_r, a_r, a_after, wa, wx, lam,
             day_o, dxc_o, dwa_o, dwx_o, dba_o, dbx_o, dlam_o, lcar, tmp):
        t = pl.program_id(0)
        h_prev = _rows_down(h[...], jnp.where(t == nb - 1, 0.0, h_before[...]), 1)
        a_next = _rows_up(a_r[...], jnp.where(t == 0, 0.0, a_after[...]), 1)

        @pl.when(t == 0)
        def _():
            lcar[...] = jnp.zeros_like(lcar)
            dwa_o[...] = jnp.zeros_like(dwa_o)
            dwx_o[...] = jnp.zeros_like(dwx_o)
            dba_o[...] = jnp.zeros_like(dba_o)
            dbx_o[...] = jnp.zeros_like(dbx_o)
            dlam_o[...] = jnp.zeros_like(dlam_o)

        gy, dgy = _gelu_and_grad(ay[...])
        dy = dya_r[...]
        day_o[...] = (dy * h[...] * dgy).astype(day_o.dtype)
        A, B = _scan_rows(a_next, dy * gy, T, reverse=True)
        lmb = B + A * lcar[...]
        tmp[...] = lmb
        lcar[...] = tmp[pl.ds(0, 1), :]
        xc, r, gi, a = xc_r[...], r_r[...], i_r[...], a_r[...]
        sp = _softplus(-lam[...])
        la = -LRU_C * r * sp
        mult = jnp.sqrt(_neg_expm1(2.0 * la))
        da = lmb * h_prev
        dmult = lmb * gi * xc
        di = lmb * mult * xc
        dxc = lmb * mult * gi
        dla = da * a - dmult * a * a / mult
        dr = dla * (-LRU_C * sp)
        dlam_o[...] += _colsum(dla * (LRU_C * r)) * _sigmoid(-lam[...])
        dpr = dr * r * (1.0 - r)
        dpi = di * gi * (1.0 - gi)
        dba_o[...] += _colsum(dpr)
        dbx_o[...] += _colsum(dpi)
        dxc_o[...] = dxc + _dot(dpr, wa[...], "nt") + _dot(dpi, wx[...], "nt")
        dwa_o[...] += _dot(xc, dpr, "tn")
        dwx_o[...] += _dot(xc, dpi, "tn")

    outs = pl.pallas_call(
        body,
        name="lru_bwd",
        grid=(nb,),
        in_specs=[rrow, pl.BlockSpec((T, BRANCH), lambda t: (nb - 1 - t, U_AY // BRANCH)), rrow,
                  _halo_before(T, block), rrow, rrow, rrow, rrow, _halo_after(T, block, S), sq, sq, vrow],
        out_specs=[rrow, rrow, sq, sq, vrow, vrow, vrow],
        out_shape=[jax.ShapeDtypeStruct((S, BRANCH), BF16), jax.ShapeDtypeStruct((S, BRANCH), F32),
                   jax.ShapeDtypeStruct((BRANCH, BRANCH), F32), jax.ShapeDtypeStruct((BRANCH, BRANCH), F32),
                   jax.ShapeDtypeStruct((1, BRANCH), F32), jax.ShapeDtypeStruct((1, BRANCH), F32),
                   jax.ShapeDtypeStruct((1, BRANCH), F32)],
        scratch_shapes=[pltpu.VMEM((1, BRANCH), F32), pltpu.VMEM((T, BRANCH), F32)],
        compiler_params=pltpu.CompilerParams(dimension_semantics=("arbitrary",), vmem_limit_bytes=VMEM_LIMIT_BYTES),
    )(dya, u, sv["h"], sv["h"], sv["xc"], sv["r"], sv["i"], sv["a"], sv["a"], lw["wa"], lw["wx"], lw["lam"])
    return outs


def _conv_bwd(dxc, u, lw, S):
    T = min(CONV_TILE, S)
    nb = S // T
    vecs = [lw["cw0"], lw["cw1"], lw["cw2"], lw["cw3"]]
    vrow = pl.BlockSpec((1, BRANCH), lambda t: (0, 0))

    def body(d_r, d_after, ax, ax_before, cw0, cw1, cw2, cw3, dax_o, dcw0_o, dcw1_o, dcw2_o, dcw3_o, dcb_o):
        t = pl.program_id(0)
        d = d_r[...]
        after = jnp.where(t == nb - 1, 0.0, d_after[...])
        x = ax[...]
        before = jnp.where(t == 0, 0.0, ax_before[...])
        dax = (cw3[...] * d + cw2[...] * _rows_up(d, after, 1) + cw1[...] * _rows_up(d, after, 2)
               + cw0[...] * _rows_up(d, after, 3))
        dax_o[...] = dax.astype(dax_o.dtype)
        sums = [_colsum(d * _rows_down(x, before, 3)), _colsum(d * _rows_down(x, before, 2)),
                _colsum(d * _rows_down(x, before, 1)), _colsum(d * x), _colsum(d)]
        outs = [dcw0_o, dcw1_o, dcw2_o, dcw3_o, dcb_o]

        @pl.when(t == 0)
        def _():
            for o, val in zip(outs, sums):
                o[...] = val

        @pl.when(t > 0)
        def _():
            for o, val in zip(outs, sums):
                o[...] += val

    res = pl.pallas_call(
        body,
        name="conv_bwd",
        grid=(nb,),
        in_specs=[pl.BlockSpec((T, BRANCH), lambda t: (t, 0)), _halo_after(T, lambda t: t, S),
                  pl.BlockSpec((T, BRANCH), lambda t: (t, U_AX // BRANCH)),
                  _halo_before(T, lambda t: t, U_AX // BRANCH)] + [_full_spec(v) for v in vecs],
        out_specs=[pl.BlockSpec((T, BRANCH), lambda t: (t, 0))] + [vrow] * 5,
        out_shape=[jax.ShapeDtypeStruct((S, BRANCH), BF16)] + [jax.ShapeDtypeStruct((1, BRANCH), F32)] * 5,
        compiler_params=pltpu.CompilerParams(dimension_semantics=("arbitrary",), vmem_limit_bytes=VMEM_LIMIT_BYTES),
    )(dxc, dxc, u, u, *vecs)
    return res[0], res[1:]


GLA_QK = GLA_HEADS * GLA_DK
GLA_V = GLA_HEADS * GLA_DV
GLA_SCALE = GLA_DK ** -0.5


def _gla_specs(TB, rev_nb=None):
    def rmap(t):
        return t if rev_nb is None else rev_nb - 1 - t
    return [
        pl.BlockSpec((TB, GLA_QK), lambda t: (rmap(t), U_BQ // GLA_QK)),
        pl.BlockSpec((TB, GLA_QK), lambda t: (rmap(t), U_BK // GLA_QK)),
        pl.BlockSpec((TB, GLA_V), lambda t: (rmap(t), U_BV // GLA_V)),
        pl.BlockSpec((TB, GLA_V), lambda t: (rmap(t), U_BR // GLA_V)),
        pl.BlockSpec((TB, LANES), lambda t: (rmap(t), U_BLOW // LANES)),
    ]


def _gla_gates(gl, wg2, bg, TB):
    pre = _dot(gl, wg2, "nn") + bg
    la = _log_sigmoid(pre) * (1.0 / GLA_TAU)
    _, gc = _scan_rows(None, la, TB, seg=CHUNK)
    return pre, la, gc


def _gla_fwd(u, gw, S):
    TB = min(512, S)
    nb = S // TB
    cpb = TB // CHUNK
    vecs = [gw["wg2"], gw["bg"], gw["ng"], gw["bd"]]

    def body(q_r, k_r, v_r, br_r, gl_r, wg2, bg, ng, bd, yb_o, oraw_o, st_o, st):
        t = pl.program_id(0)

        @pl.when(t == 0)
        def _():
            st[...] = jnp.zeros_like(st)

        _, la, gc = _gla_gates(gl_r[...], wg2[...], bg[...], TB)
        for c in range(cpb):
            sl = slice(c * CHUNK, (c + 1) * CHUNK)
            gt = _colsum(la[sl])
            kdec = k_r[sl, :] * jnp.exp(gt - gc[sl])
            d_t = _dot(v_r[sl, :], kdec, "tn") * bd[...]
            s_new = st[...] * jnp.exp(gt) + d_t
            st[...] = s_new
            st_o[c] = s_new
            oraw_o[sl, :] = _dot(q_r[sl, :] * GLA_SCALE, s_new, "nt")
        for h in range(GLA_HEADS):
            hs = slice(h * GLA_DV, (h + 1) * GLA_DV)
            oh = oraw_o[:, hs]
            on = oh * lax.rsqrt(jnp.mean(oh * oh, axis=-1, keepdims=True) + RMS_EPS)
            sil, _ = _silu_and_grad(br_r[:, hs])
            yb_o[:, hs] = (on * ng[:, hs] * sil).astype(yb_o.dtype)

    return pl.pallas_call(
        body,
        name="gla_fwd",
        grid=(nb,),
        in_specs=_gla_specs(TB) + [_full_spec(v) for v in vecs],
        out_specs=[pl.BlockSpec((TB, GLA_V), lambda t: (t, 0)), pl.BlockSpec((TB, GLA_V), lambda t: (t, 0)),
                   pl.BlockSpec((cpb, GLA_V, GLA_QK), lambda t: (t, 0, 0))],
        out_shape=[jax.ShapeDtypeStruct((S, GLA_V), BF16), jax.ShapeDtypeStruct((S, GLA_V), F32),
                   jax.ShapeDtypeStruct((S // CHUNK, GLA_V, GLA_QK), F32)],
        scratch_shapes=[pltpu.VMEM((GLA_V, GLA_QK), F32)],
        compiler_params=pltpu.CompilerParams(dimension_semantics=("arbitrary",), vmem_limit_bytes=VMEM_LIMIT_BYTES),
    )(u, u, u, u, u, *vecs)


def _gla_bwd(dyb, u, oraw, states, gw, S):
    TB = min(512, S)
    nb = S // TB
    cpb = TB // CHUNK
    vecs = [gw["wg2"], gw["bg"], gw["ng"], gw["bd"]]

    def rrow(width):
        return pl.BlockSpec((TB, width), lambda t: (nb - 1 - t, 0))

    def body(dyb_r, oraw_r, q_r, k_r, v_r, br_r, gl_r, st_r, sp_r, wg2, bg, ng, bd,
             dq_o, dk_o, dv_o, dbr_o, dgl_o, dwg2_o, dbg_o, dng_o, dcar, do_buf, dla_buf):
        t = pl.program_id(0)
        blk = nb - 1 - t

        @pl.when(t == 0)
        def _():
            dcar[...] = jnp.zeros_like(dcar)
            dwg2_o[...] = jnp.zeros_like(dwg2_o)
            dbg_o[...] = jnp.zeros_like(dbg_o)
            dng_o[...] = jnp.zeros_like(dng_o)

        pre, la, gc = _gla_gates(gl_r[...], wg2[...], bg[...], TB)
        for h in range(GLA_HEADS):
            hs = slice(h * GLA_DV, (h + 1) * GLA_DV)
            oh = oraw_r[:, hs]
            rs = lax.rsqrt(jnp.mean(oh * oh, axis=-1, keepdims=True) + RMS_EPS)
            on = oh * rs
            sil, dsil = _silu_and_grad(br_r[:, hs])
            dy = dyb_r[:, hs]
            dbr_o[:, hs] = (dy * on * ng[:, hs] * dsil).astype(dbr_o.dtype)
            don = dy * ng[:, hs] * sil
            dng_o[:, hs] += _colsum(dy * on * sil)
            do_buf[:, hs] = rs * (don - on * jnp.mean(don * on, axis=-1, keepdims=True))
        first = jnp.where(blk == 0, 0.0, 1.0)
        for c in reversed(range(cpb)):
            sl = slice(c * CHUNK, (c + 1) * CHUNK)
            s_n = st_r[c]
            s_prev = st_r[c - 1] if c > 0 else sp_r[0] * first
            gt = _colsum(la[sl])
            w = jnp.exp(gt - gc[sl])
            k_c = k_r[sl, :]
            kdec = k_c * w
            qs = q_r[sl, :] * GLA_SCALE
            do_c = do_buf[sl, :]
            dq_o[sl, :] = (_dot(do_c, s_n, "nn") * GLA_SCALE).astype(dq_o.dtype)
            d_n = _dot(do_c, qs, "tn") * bd[...] + dcar[...]
            dv_o[sl, :] = _dot(kdec, d_n, "nt").astype(dv_o.dtype)
            dkdec = _dot(v_r[sl, :], d_n, "nn")
            dk_o[sl, :] = (dkdec * w).astype(dk_o.dtype)
            tt = dkdec * kdec
            e = jnp.exp(gt)
            dgt = _colsum(tt) + _colsum(d_n * s_prev) * e
            _, rc = _scan_rows(None, -tt, CHUNK, reverse=True)
            dla_buf[sl, :] = rc + dgt
            dcar[...] = d_n * e
        dpre = dla_buf[...] * _sigmoid(-pre) * (1.0 / GLA_TAU)
        dbg_o[...] += _colsum(dpre)
        dgl_o[...] = _dot(dpre, wg2[...], "nt").astype(dgl_o.dtype)
        dwg2_o[...] += _dot(gl_r[...], dpre, "tn")

    return pl.pallas_call(
        body,
        name="gla_bwd",
        grid=(nb,),
        in_specs=[rrow(GLA_V), rrow(GLA_V)] + _gla_specs(TB, rev_nb=nb)
        + [pl.BlockSpec((cpb, GLA_V, GLA_QK), lambda t: (nb - 1 - t, 0, 0)),
           pl.BlockSpec((1, GLA_V, GLA_QK), lambda t: (jnp.maximum((nb - 1 - t) * cpb - 1, 0), 0, 0))]
        + [_full_spec(v) for v in vecs],
        out_specs=[rrow(GLA_QK), rrow(GLA_QK), rrow(GLA_V), rrow(GLA_V), rrow(LANES),
                   pl.BlockSpec((LANES, GLA_QK), lambda t: (0, 0)), pl.BlockSpec((1, GLA_QK), lambda t: (0, 0)),
                   pl.BlockSpec((1, GLA_V), lambda t: (0, 0))],
        out_shape=[jax.ShapeDtypeStruct((S, GLA_QK), BF16), jax.ShapeDtypeStruct((S, GLA_QK), BF16),
                   jax.ShapeDtypeStruct((S, GLA_V), BF16), jax.ShapeDtypeStruct((S, GLA_V), BF16),
                   jax.ShapeDtypeStruct((S, LANES), BF16), jax.ShapeDtypeStruct((LANES, GLA_QK), F32),
                   jax.ShapeDtypeStruct((1, GLA_QK), F32), jax.ShapeDtypeStruct((1, GLA_V), F32)],
        scratch_shapes=[pltpu.VMEM((GLA_V, GLA_QK), F32), pltpu.VMEM((TB, GLA_V), F32),
                        pltpu.VMEM((TB, GLA_QK), F32)],
        compiler_params=pltpu.CompilerParams(dimension_semantics=("arbitrary",), vmem_limit_bytes=VMEM_LIMIT_BYTES),
    )(dyb, oraw, u, u, u, u, u, states, states, *vecs)


FOX_SCALE = FOX_DH ** -0.5


def _fox_gate_fwd(u, bfp, S):
    T = min(512, S)

    def body(f_r, b_r, fc_o, car):
        t = pl.program_id(0)

        @pl.when(t == 0)
        def _():
            car[...] = jnp.zeros_like(car)

        _, cs = _scan_rows(None, _log_sigmoid(f_r[...] + b_r[...]), T)
        fc_o[...] = cs + car[...]
        car[...] = fc_o[pl.ds(T - 1, 1), :]

    return pl.pallas_call(
        body,
        name="fox_gate_fwd",
        grid=(S // T,),
        in_specs=[pl.BlockSpec((T, LANES), lambda t: (t, U_CF // LANES)), _full_spec(bfp)],
        out_specs=pl.BlockSpec((T, LANES), lambda t: (t, 0)),
        out_shape=jax.ShapeDtypeStruct((S, LANES), F32),
        scratch_shapes=[pltpu.VMEM((1, LANES), F32)],
        compiler_params=pltpu.CompilerParams(dimension_semantics=("arbitrary",), vmem_limit_bytes=VMEM_LIMIT_BYTES),
    )(u, bfp)


def _fox_gate_bwd(dfc, u, bfp, S):
    T = min(512, S)
    nb = S // T

    def body(d_r, f_r, b_r, df_o, db_o, car, tmp):
        t = pl.program_id(0)

        @pl.when(t == 0)
        def _():
            car[...] = jnp.zeros_like(car)
            db_o[...] = jnp.zeros_like(db_o)

        _, rc = _scan_rows(None, d_r[...], T, reverse=True)
        tmp[...] = rc + car[...]
        car[...] = tmp[pl.ds(0, 1), :]
        df = tmp[...] * _sigmoid(-(f_r[...] + b_r[...]))
        df_o[...] = df.astype(df_o.dtype)
        db_o[...] += _colsum(df)

    return pl.pallas_call(
        body,
        name="fox_gate_bwd",
        grid=(nb,),
        in_specs=[pl.BlockSpec((T, LANES), lambda t: (nb - 1 - t, 0)),
                  pl.BlockSpec((T, LANES), lambda t: (nb - 1 - t, U_CF // LANES)), _full_spec(bfp)],
        out_specs=[pl.BlockSpec((T, LANES), lambda t: (nb - 1 - t, 0)), pl.BlockSpec((1, LANES), lambda t: (0, 0))],
        out_shape=[jax.ShapeDtypeStruct((S, LANES), BF16), jax.ShapeDtypeStruct((1, LANES), F32)],
        scratch_shapes=[pltpu.VMEM((1, LANES), F32), pltpu.VMEM((T, LANES), F32)],
        compiler_params=pltpu.CompilerParams(dimension_semantics=("arbitrary",), vmem_limit_bytes=VMEM_LIMIT_BYTES),
    )(dfc, u, bfp)


def _fox_call(name, body, tables, grid, in_specs, out_specs, out_shape, scratch, args, side):
    n_in, n_out, n_scr = len(in_specs), len(out_specs), len(scratch)
    semantics = ("parallel", "arbitrary")
    if side is not None:
        total = grid[0] * grid[1]
        phases = side["phases"]
        triggers = [0, total - 1] if len(phases) == 2 else [0, total * 7 // 10, total - 1]
        na, no = len(side["arrs"]), len(side["out_shapes"])
        s_in, s_out, s_sems = _side_specs(side)
        kernel_body = body

        def body(*refs):
            tabs, rest = refs[:len(tables)], refs[len(tables):]
            ins, s_ins = rest[:n_in], rest[n_in:n_in + na]
            rest = rest[n_in + na:]
            outs, s_outs = rest[:n_out], rest[n_out:n_out + no]
            rest = rest[n_out + no:]
            scr, sems = rest[:n_scr], rest[n_scr:]
            flat = pl.program_id(0) * grid[1] + pl.program_id(1)
            for trigger, phase in zip(triggers[:-1], phases[:-1]):
                @pl.when(flat == trigger)
                def _(phase=phase):
                    phase(s_ins, s_outs, *sems)
            kernel_body(*tabs, *ins, *outs, *scr)

            @pl.when(flat == triggers[-1])
            def _():
                phases[-1](s_ins, s_outs, *sems)

        in_specs, out_specs = in_specs + s_in, out_specs + s_out
        out_shape, scratch = out_shape + side["out_shapes"], scratch + s_sems
        args = list(args) + side["arrs"]
        semantics = ("arbitrary", "arbitrary")
    res = pl.pallas_call(
        body,
        name=name,
        grid_spec=pltpu.PrefetchScalarGridSpec(num_scalar_prefetch=len(tables), grid=grid, in_specs=in_specs,
                                               out_specs=out_specs, scratch_shapes=scratch),
        out_shape=out_shape,
        compiler_params=pltpu.CompilerParams(dimension_semantics=semantics, vmem_limit_bytes=VMEM_LIMIT_BYTES),
    )(*tables, *args)
    return res[:n_out], res[n_out:]


FOX_TILE = 1024
FOX_GROUP = 2
FOX_GROUP_FWD = 8
FOX_AUG = 128
FOX_ONES = 3


def _fox_pairs(n, by_key):
    pairs = [(qi, ki) for qi in range(n) for ki in range(qi + 1)]
    if by_key:
        pairs.sort(key=lambda qk: (qk[1], qk[0]))
    qs = jnp.asarray([qk[0] for qk in pairs], jnp.int32)
    ks = jnp.asarray([qk[1] for qk in pairs], jnp.int32)
    return qs, ks


def _fox_causal(sT):
    keys = lax.broadcasted_iota(jnp.int32, sT.shape, 0)
    queries = lax.broadcasted_iota(jnp.int32, sT.shape, 1)
    return jnp.where(keys <= queries, sT, NEG_BIG)


def _fox_fwd(qT, ka, vT, S, side=None):
    t = min(FOX_TILE, S)
    n = S // t
    qi_tab, ki_tab = _fox_pairs(n, by_key=False)

    G = FOX_GROUP_FWD

    def body(qi_ref, ki_ref, qT_r, ka_r, vT_r, oT_o, lse_o, m_s, l_s, acc):
        step = pl.program_id(1)
        qi, ki = qi_ref[step], ki_ref[step]

        @pl.when(ki == 0)
        def _():
            m_s[...] = jnp.full_like(m_s, NEG_BIG)
            l_s[...] = jnp.zeros_like(l_s)
            acc[...] = jnp.zeros_like(acc)

        def update(g, masked):
            sT = _dot(ka_r[g], qT_r[g], "nn")
            if masked:
                sT = _fox_causal(sT)
            m_new = jnp.maximum(m_s[g], jnp.max(sT, axis=0, keepdims=True))
            p = jnp.exp(sT - m_new)
            alpha = jnp.exp(m_s[g] - m_new)
            l_s[g] = alpha * l_s[g] + jnp.sum(p, axis=0, keepdims=True)
            acc[g] = alpha * acc[g] + _dot(vT_r[g], p, "nn")
            m_s[g] = m_new

        @pl.when(ki < qi)
        def _():
            for g in range(G):
                update(g, False)

        @pl.when(ki == qi)
        def _():
            for g in range(G):
                update(g, True)
                oT_o[g] = acc[g] / l_s[g]
                lse_o[g] = m_s[g] + jnp.log(l_s[g])

    return _fox_call(
        "fox_fwd", body, (qi_tab, ki_tab), (FOX_HEADS // G, int(qi_tab.shape[0])),
        [pl.BlockSpec((G, FOX_AUG, t), lambda h, s, qt, kt: (h, 0, qt[s])),
         pl.BlockSpec((G, t, FOX_AUG), lambda h, s, qt, kt: (h, kt[s], 0)),
         pl.BlockSpec((G, FOX_DH, t), lambda h, s, qt, kt: (h, 0, kt[s]))],
        [pl.BlockSpec((G, FOX_DH, t), lambda h, s, qt, kt: (h, 0, qt[s])),
         pl.BlockSpec((G, 1, t), lambda h, s, qt, kt: (h, 0, qt[s]))],
        [jax.ShapeDtypeStruct((FOX_HEADS, FOX_DH, S), F32), jax.ShapeDtypeStruct((FOX_HEADS, 1, S), F32)],
        [pltpu.VMEM((G, 1, t), F32), pltpu.VMEM((G, 1, t), F32), pltpu.VMEM((G, FOX_DH, t), F32)],
        (qT, ka, vT), side)


FOX_BIAS_ROWS = 8


def _fox_bwd(qT, qa, ka, kT, v, do, doT, oT, lse, S, side=None):
    t = min(FOX_TILE, S)
    n = S // t
    qi_tab, ki_tab = _fox_pairs(n, by_key=True)
    n_steps = int(qi_tab.shape[0])
    slab = slice(FOX_DH, FOX_DH + FOX_BIAS_ROWS)

    G = FOX_GROUP

    def body(qi_ref, ki_ref, qT_r, qa_r, ka_r, kT_r, v_r, do_r, doT_r, oT_r, lse_r,
             dq_o, dfq_o, dk_o, dfk_o, dv_o, dq_acc, dk_acc, dv_acc):
        step = pl.program_id(1)
        qi, ki = qi_ref[step], ki_ref[step]

        @pl.when(step == 0)
        def _():
            dq_acc[...] = jnp.zeros_like(dq_acc)

        @pl.when(qi == ki)
        def _():
            dk_acc[...] = jnp.zeros_like(dk_acc)
            dv_acc[...] = jnp.zeros_like(dv_acc)

        def update(g, masked):
            sT = _dot(ka_r[g], qT_r[g], "nn")
            if masked:
                sT = _fox_causal(sT)
            pT = jnp.exp(sT - lse_r[g])
            delta = jnp.sum(oT_r[g] * doT_r[g], axis=0, keepdims=True)
            dsT = pT * (_dot(v_r[g], doT_r[g], "nn") - delta)
            dv_acc[g] += _dot(pT, do_r[g], "nn")
            dk_acc[g] += _dot(dsT, qa_r[g], "nn")
            dq_acc[g, qi] += _dot(kT_r[g], dsT, "nn")

        @pl.when(qi > ki)
        def _():
            for g in range(G):
                update(g, False)

        @pl.when(qi == ki)
        def _():
            for g in range(G):
                update(g, True)

        @pl.when(qi == n - 1)
        def _():
            for g in range(G):
                dk = dk_acc[g]
                dk_o[g] = dk[:, :FOX_DH].astype(dk_o.dtype)
                dfk_o[g] = dk.T[slab]
                dv_o[g] = dv_acc[g].astype(dv_o.dtype)

        @pl.when(step == n_steps - 1)
        def _():
            for g in range(G):
                for j in range(n):
                    dqT = dq_acc[g, j]
                    dq_o[g, j * t:(j + 1) * t, :] = (dqT.T[:, :FOX_DH] * FOX_SCALE).astype(dq_o.dtype)
                    dfq_o[g, :, j * t:(j + 1) * t] = dqT[slab]

    def qlane(rows):
        return pl.BlockSpec((G, rows, t), lambda h, s, qt, kt: (h, 0, qt[s]))

    def qrow(cols):
        return pl.BlockSpec((G, t, cols), lambda h, s, qt, kt: (h, qt[s], 0))

    def krow(cols):
        return pl.BlockSpec((G, t, cols), lambda h, s, qt, kt: (h, kt[s], 0))

    def klane(rows):
        return pl.BlockSpec((G, rows, t), lambda h, s, qt, kt: (h, 0, kt[s]))

    def head(rows, cols):
        return pl.BlockSpec((G, rows, cols), lambda h, s, qt, kt: (h, 0, 0))

    return _fox_call(
        "fox_bwd", body, (qi_tab, ki_tab), (FOX_HEADS // G, n_steps),
        [qlane(FOX_AUG), qrow(FOX_AUG), krow(FOX_AUG), klane(FOX_AUG), krow(FOX_DH), qrow(FOX_DH), qlane(FOX_DH),
         qlane(FOX_DH), qlane(1)],
        [head(S, FOX_DH), head(FOX_BIAS_ROWS, S), krow(FOX_DH), klane(FOX_BIAS_ROWS), krow(FOX_DH)],
        [jax.ShapeDtypeStruct((FOX_HEADS, S, FOX_DH), BF16), jax.ShapeDtypeStruct((FOX_HEADS, FOX_BIAS_ROWS, S), F32),
         jax.ShapeDtypeStruct((FOX_HEADS, S, FOX_DH), BF16), jax.ShapeDtypeStruct((FOX_HEADS, FOX_BIAS_ROWS, S), F32),
         jax.ShapeDtypeStruct((FOX_HEADS, S, FOX_DH), BF16)],
        [pltpu.VMEM((G, n, FOX_AUG, t), F32), pltpu.VMEM((G, t, FOX_AUG), F32), pltpu.VMEM((G, t, FOX_DH), F32)],
        (qT, qa, ka, kT, v, do, doT, oT, lse), side)


def _fox_prep(u, fcum, S):
    T = min(512, S)
    head_of = jnp.arange(BRANCH) // FOX_DH
    dim_of = jnp.arange(BRANCH) % FOX_DH
    heads = jnp.arange(FOX_HEADS)[:, None, None]
    sel = (head_of[None, :, None] == heads) & (dim_of[None, :, None] == jnp.arange(FOX_AUG)[None, None, :])
    sel_q = (sel * FOX_SCALE).astype(BF16)
    sel_k = sel.astype(BF16)
    sel_vT = jnp.swapaxes(sel[:, :, :FOX_DH], 1, 2).astype(BF16)
    piece = jnp.arange(FOX_ONES * LANES) // LANES
    lane = jnp.arange(FOX_ONES * LANES) % LANES
    col = jnp.arange(FOX_AUG)[None, None, :]
    at_q = (lane[None, :, None] == heads) & (col == FOX_DH + FOX_ONES + piece[None, :, None])
    at_k = (lane[None, :, None] == heads) & (col == FOX_DH + piece[None, :, None])
    bias_q = at_q.astype(BF16)
    bias_k = (-at_k.astype(F32)).astype(BF16)
    cols = jnp.arange(FOX_AUG)[None, :]
    ones_q = ((cols >= FOX_DH) & (cols < FOX_DH + FOX_ONES)).astype(F32)
    ones_k = ((cols >= FOX_DH + FOX_ONES) & (cols < FOX_DH + 2 * FOX_ONES)).astype(F32)
    consts = [sel_q, sel_k, sel_vT, bias_q, bias_k, ones_q, ones_k]

    def body(cq, ck, cv, fc, sq, sk, svT, bq, bk, oq, ok, qa_o, ka_o, qT_o, kT_o, vh_o, vT_o):
        f = fc[...]
        hi = f.astype(BF16).astype(F32)
        mid = (f - hi).astype(BF16).astype(F32)
        lo = (f - hi - mid).astype(BF16).astype(F32)
        pieces = jnp.concatenate([hi, mid, lo], axis=1)
        q, k, v = cq[...], ck[...], cv[...]
        for h in range(FOX_HEADS):
            qa = _dot(q, sq[h], "nn") + _dot(pieces, bq[h], "nn") + oq[...]
            ka = _dot(k, sk[h], "nn") + _dot(pieces, bk[h], "nn") + ok[...]
            qa_o[h] = qa.astype(qa_o.dtype)
            ka_o[h] = ka.astype(ka_o.dtype)
            qT_o[h] = qa.T.astype(qT_o.dtype)
            kT_o[h] = ka.T.astype(kT_o.dtype)
            vT_o[h] = _dot(svT[h], v, "nt").astype(vT_o.dtype)
            vh_o[h] = _dot(v, svT[h], "nt").astype(vh_o.dtype)

    def win(off):
        return pl.BlockSpec((T, BRANCH), functools.partial(lambda i, blk: (i, blk), blk=off // BRANCH))

    def rows(c):
        return pl.BlockSpec((FOX_HEADS, T, c), lambda i: (0, i, 0))

    def lanes(r):
        return pl.BlockSpec((FOX_HEADS, r, T), lambda i: (0, 0, i))

    bf = lambda *shape: jax.ShapeDtypeStruct((FOX_HEADS,) + shape, BF16)
    return pl.pallas_call(
        body,
        name="fox_prep",
        grid=(S // T,),
        in_specs=[win(U_CQ), win(U_CK), win(U_CV), pl.BlockSpec((T, LANES), lambda i: (i, 0))]
        + [_full_spec(c) for c in consts],
        out_specs=[rows(FOX_AUG), rows(FOX_AUG), lanes(FOX_AUG), lanes(FOX_AUG), rows(FOX_DH), lanes(FOX_DH)],
        out_shape=[bf(S, FOX_AUG), bf(S, FOX_AUG), bf(FOX_AUG, S), bf(FOX_AUG, S), bf(S, FOX_DH), bf(FOX_DH, S)],
        compiler_params=pltpu.CompilerParams(dimension_semantics=("parallel",), vmem_limit_bytes=VMEM_LIMIT_BYTES),
    )(u, u, u, fcum, *consts)


def _to_heads(x2d, S):
    return jnp.transpose(x2d.reshape(S, FOX_HEADS, FOX_DH), (1, 0, 2))


def _from_heads(xh, S):
    return jnp.transpose(xh, (1, 0, 2)).reshape(S, FOX_HEADS * FOX_DH)


def _ffn_fwd(tag, x, wgT, wuT, wd, g, b, S, side=None):
    def up_epi(accs):
        gate, up = accs
        sil, _ = _silu_and_grad(gate)
        return [gate, up, sil * up]

    res = _mm(tag + "_up", "nt", [x], [wgT, wuT], [(0, 0, 0), (1, 0, 1)], 2, up_epi, [],
              [BF16, BF16, BF16], S, D_FF, D_MODEL, tn=1408, side=side)
    (gate, up, act), side_out = res if side is not None else (res, None)

    def down_epi(accs, xr, gg, bb):
        z = ALPHA * xr + 0.5 * accs[0]
        return [z, _ln_fwd(z, gg, bb)]

    z, xn = _mm(tag + "_down", "nn", [act], [wd], [(0, 0, 0)], 1, down_epi, [(x, "mn", 0), (g, "n"), (b, "n")],
                [F32, F32], S, D_MODEL, D_FF, tk=D_FF)
    return xn, dict(x=x, gate=gate, up=up, act=act, z=z), side_out


def _ln_bwd_call(tag, dy, z, g, S):
    def fn(dy_t, z_t, g_t):
        dz, xhat = _ln_bwd(dy_t, z_t, g_t)
        return [dz], [_colsum(dy_t * xhat), _colsum(dy_t)]

    (dz,), (dg, db) = _rowwise(tag + "_ln_bwd", fn, [dy, z], [g], [(D_MODEL, F32)], [D_MODEL, D_MODEL], S)
    return dz, dg, db


def _ffn_bwd(tag, dxn, sv, wgT, wuT, wd, g, S, gdt=F32, make_side=None, first_side=None):
    dz, dg, db = _ln_bwd_call(tag, dxn, sv["z"], g, S)

    def act_epi(accs, gate, up):
        da = 0.5 * accs[0]
        sil, dsil = _silu_and_grad(gate.astype(F32))
        return [da * up.astype(F32) * dsil, da * sil]

    res = _mm(tag + "_dact", "nt", [dz], [wd], [(0, 0, 0)], 1, act_epi,
              [(sv["gate"], "mn", 0), (sv["up"], "mn", 0)], [BF16, BF16], S, D_FF, D_MODEL, tn=1408, side=first_side)
    (dgate, dup), first_out = res if first_side is not None else (res, None)
    dwd = _mm1(tag + "_dwd", "tn", sv["act"], dz, D_FF, D_MODEL, S, scale=0.5, tm=1408, out_dtype=gdt)

    def two(accs):
        return [accs[0], accs[1]]

    dwgT, dwuT = _mm(tag + "_dwup", "tn", [dgate, dup], [sv["x"]], [(0, 0, 0), (1, 1, 0)], 2, two, [], [gdt, gdt],
                     D_FF, D_MODEL, S, tm=1408, tk=512)

    def dx_epi(accs, dzr):
        return [accs[0] + ALPHA * dzr]

    grads = dict(w_upT=jnp.concatenate([dwgT, dwuT], axis=0), w_down=dwd, ln_g=dg, ln_b=db)
    side = make_side(grads) if make_side is not None else None
    res = _mm(tag + "_dx", "nn", [dgate, dup], [wgT, wuT], [(0, 0, 0), (0, 1, 1)], 1, dx_epi, [(dz, "mn", 0)],
              [F32], S, D_MODEL, D_FF, tm=1024, tk=1408, side=side)
    (dx,), side_out = res if side is not None else (res, None)
    return dx, grads, (first_out, side_out)


def _mixer_fwd(x1, w, S, side=None, on_side=None):
    u = _mm1("w_in", "nt", x1, w["w_inT_p"], S, U_WIDTH, D_MODEL, tm=1024, tn=1536)
    xc, r, gi, a, h, y_a = _lru_fwd(u, w["lru"], S)
    y_b, oraw, states = _gla_fwd(u, w["gla"], S)
    fcum = _fox_gate_fwd(u, w["bfp"], S)
    qa, ka, qT, kT, vh, vT = _fox_prep(u, fcum, S)
    (oT, lse), side_out = _fox_fwd(qT, ka, vT, S, side)
    if on_side is not None:
        on_side(side_out)
    y_c = jnp.transpose(oT, (2, 0, 1)).reshape(S, BRANCH).astype(BF16)

    def merge_epi(accs, g0, g1, g2):
        merged = _sigmoid(g0) * accs[0] + _sigmoid(g1) * accs[1] + _sigmoid(g2) * accs[2]
        return [accs[0], accs[1], accs[2], merged]

    wb = w["w_branchT"]
    yp0, yp1, yp2, merged = _mm(
        "merge", "nt", [y_a, y_b, y_c], [wb[0], wb[1], wb[2]], [(0, 0, 0), (1, 1, 1), (2, 2, 2)], 3, merge_epi,
        [(u, "mn", 0), (u, "mn", 1), (u, "mn", 2)], [BF16, BF16, BF16, BF16], S, D_MODEL, BRANCH, tm=512)

    def out_epi(accs, xr, gg, bb):
        z = ALPHA * xr + accs[0]
        return [z, _ln_fwd(z, gg, bb)]

    z2, x2 = _mm("w_out", "nn", [merged], [w["w_out"]], [(0, 0, 0)], 1, out_epi,
                 [(x1, "mn", 0), (w["ln2_g"], "n"), (w["ln2_b"], "n")], [F32, F32], S, D_MODEL, D_MODEL)
    sv = dict(x=x1, u=u, xc=xc, r=r, i=gi, a=a, h=h, y_a=y_a, y_b=y_b, y_c=y_c, oraw=oraw,
              states=states, qT=qT, qa=qa, ka=ka, kT=kT, vh=vh, oT=oT, lse=lse, yp=(yp0, yp1, yp2), merged=merged,
              z=z2)
    return x2, sv, side_out


def _mixer_bwd(dx2, sv, w, S, make_side=None, gdt_a=F32, gdt_b=F32):
    u = sv["u"]
    dz, dg2, db2 = _ln_bwd_call("mix", dx2, sv["z"], w["ln2_g"], S)

    def dm_epi(accs, y0, y1, y2, g0, g1, g2):
        dm = accs[0]
        outs_p, outs_g = [], []
        for yp, gl in ((y0, g0), (y1, g1), (y2, g2)):
            sg = _sigmoid(gl)
            outs_p.append(dm * sg)
            outs_g.append(dm * yp.astype(F32) * sg * (1.0 - sg))
        return outs_p + outs_g

    yp = sv["yp"]
    dyp0, dyp1, dyp2, dgl0, dgl1, dgl2 = _mm(
        "dmerged", "nt", [dz], [w["w_out"]], [(0, 0, 0)], 1, dm_epi,
        [(yp[0], "mn", 0), (yp[1], "mn", 0), (yp[2], "mn", 0), (u, "mn", 0), (u, "mn", 1), (u, "mn", 2)],
        [BF16] * 6, S, D_MODEL, D_MODEL, tm=256)
    dw_out = _mm1("dw_out", "tn", sv["merged"], dz, D_MODEL, D_MODEL, S, out_dtype=gdt_b)
    wb = w["w_branchT"]
    dys, dwbs = [], []
    for j, (yj, dyp) in enumerate(((sv["y_a"], dyp0), (sv["y_b"], dyp1), (sv["y_c"], dyp2))):
        dys.append(_mm1("dy_branch%d" % j, "nn", dyp, wb[j], S, BRANCH, D_MODEL))
        dwbs.append(_mm1("dw_branch%d" % j, "tn", dyp, yj, D_MODEL, BRANCH, S, out_dtype=gdt_b))
    day, dxc, dwa, dwx, dba, dbx, dlam = _lru_bwd(dys[0], u, sv, w["lru"], S)
    dax, (dcw0, dcw1, dcw2, dcw3, dcb) = _conv_bwd(dxc, u, w["lru"], S)
    dbq, dbk, dbv, dbr, dglow, dwg2p, dbg, dng = _gla_bwd(dys[1], u, sv["oraw"], sv["states"], w["gla"], S)
    doh = _to_heads(dys[2], S)
    dw_branchT = jnp.stack(dwbs)
    side = make_side(dict(w_out=dw_out, w_branchT=dw_branchT)) if make_side is not None else None
    (dqh, dfq, dkh, dfk, dvh), side_out = _fox_bwd(sv["qT"], sv["qa"], sv["ka"], sv["kT"], sv["vh"], doh,
                                                   jnp.swapaxes(doh, 1, 2), sv["oT"], sv["lse"], S, side)
    dfc = jnp.transpose(dfq[:, FOX_ONES, :] - dfk[:, 0, :])
    dfc = jnp.pad(dfc, ((0, 0), (0, LANES - FOX_HEADS)))
    dcf, dbf = _fox_gate_bwd(dfc, u, w["bfp"], S)
    du = jnp.concatenate(
        [dgl0, dgl1, dgl2, dax, day, dbq, dbk, dbv, dbr, _from_heads(dqh, S).astype(BF16),
         _from_heads(dkh, S).astype(BF16), _from_heads(dvh, S).astype(BF16), dglow, dcf,
         jnp.zeros((S, U_WIDTH - U_CF - LANES), BF16)], axis=1)
    dw_inT_p = _mm1("dw_in", "tn", du, sv["x"], U_WIDTH, D_MODEL, S, tm=1536, out_dtype=gdt_a)

    def dx_epi(accs, dzr):
        return [accs[0] + ALPHA * dzr]

    (dx1,) = _mm("dx_mix", "nn", [du], [w["w_inT_p"]], [(0, 0, 0)], 1, dx_epi, [(dz, "mn", 0)], [F32], S, D_MODEL,
                 U_WIDTH, tm=1024, tk=1536)
    pieces = sorted(W_IN_SEGMENTS)
    dw_inT = jnp.concatenate([dw_inT_p[dst:dst + width] for _, width, dst in pieces], axis=0)
    eye = jnp.eye(LRU_BLOCKS, dtype=F32)
    dwa_b = jnp.einsum("ncmd,nm->ncd", dwa.reshape(LRU_BLOCKS, 64, LRU_BLOCKS, 64), eye)
    dwx_b = jnp.einsum("ncmd,nm->ncd", dwx.reshape(LRU_BLOCKS, 64, LRU_BLOCKS, 64), eye)
    grads = dict(
        w_inT=dw_inT, w_out=dw_out, w_branchT=dw_branchT, ln2_g=dg2, ln2_b=db2,
        conv_w=jnp.concatenate([dcw0, dcw1, dcw2, dcw3], axis=0).astype(gdt_a), conv_b=dcb, lru_wa=dwa_b, lru_wx=dwx_b,
        lru_ba=dba, lru_bx=dbx, lru_lambda=dlam, gla_w_g2=dwg2p[:GLA_LOWRANK].astype(gdt_a), gla_b_g=dbg, gla_norm_g=dng,
        fox_b_f=dbf[:, :FOX_HEADS])
    return dx1, grads, side_out


def _ple_fwd(x3, p_i, w, S):
    pe = _mm1("ple_proj", "nt", p_i, w["ple_w_projT"], S, D_MODEL, PLE_DIM)

    def epi(accs, xr, per, bg, gg, bb):
        sg = _sigmoid(accs[0] + bg)
        z = ALPHA * xr + sg * per
        return [sg, z, _ln_fwd(z, gg, bb)]

    sg, z4, x4 = _mm("ple_gate", "nn", [x3], [w["ple_w_gate"]], [(0, 0, 0)], 1, epi,
                     [(x3, "mn", 0), (pe, "mn", 0), (w["ple_b_gate"], "n"), (w["ln4_g"], "n"), (w["ln4_b"], "n")],
                     [F32, F32, F32], S, D_MODEL, D_MODEL)
    return x4, dict(x=x3, p=p_i, pe=pe, sg=sg, z=z4)


def _ple_bwd(dx4, sv, w, S, gdt=F32):
    def fn(dy_t, z_t, pe_t, sg_t, g_t):
        dz, xhat = _ln_bwd(dy_t, z_t, g_t)
        dgl = dz * pe_t * sg_t * (1.0 - sg_t)
        return [dz, dz * sg_t, dgl], [_colsum(dy_t * xhat), _colsum(dy_t), _colsum(dgl)]

    (dz, dpe, dgl), (dg4, db4, dbg) = _rowwise(
        "ple_bwd", fn, [dx4, sv["z"], sv["pe"], sv["sg"]], [w["ln4_g"]],
        [(D_MODEL, F32), (D_MODEL, BF16), (D_MODEL, BF16)], [D_MODEL] * 3, S)
    dwpT = _mm1("dw_ple_proj", "tn", dpe, sv["p"], D_MODEL, PLE_DIM, S, out_dtype=gdt)
    dwg = _mm1("dw_ple_gate", "tn", sv["x"], dgl, D_MODEL, D_MODEL, S, out_dtype=gdt)

    def dx_epi(accs, dzr):
        return [accs[0] + ALPHA * dzr]

    (dx3,) = _mm("dx_ple", "nt", [dgl], [w["ple_w_gate"]], [(0, 0, 0)], 1, dx_epi, [(dz, "mn", 0)], [F32], S,
                 D_MODEL, D_MODEL)
    return dx3, dict(ple_w_projT=dwpT, ple_w_gate=dwg, ple_b_gate=dbg, ln4_g=dg4, ln4_b=db4)


def _rows_of_all(g):
    return g.reshape((g.shape[0] * g.shape[1],) + g.shape[2:])


EARLY_WEIGHTS = ("ffn1_w_up", "ffn1_w_down", "w_in", "conv_w", "gla_w_g2")


def _ffn_weights(gathered, tag):
    upT = _rows_of_all(gathered[tag + "_w_up"])
    return upT[:D_FF], upT[D_FF:], _rows_of_all(gathered[tag + "_w_down"])


def _late_weights(gathered):
    return dict(ffn2=_ffn_weights(gathered, "ffn2"),
                w_branchT=jnp.moveaxis(gathered["w_branch"], 0, 1).reshape(3, D_MODEL, BRANCH),
                w_out=_rows_of_all(gathered["w_out"]),
                ple_w_projT=_rows_of_all(gathered["ple_w_proj"]),
                ple_w_gate=_rows_of_all(gathered["ple_w_gate"]))


def _w_in_operand(gathered_w_in):
    w_inT = _rows_of_all(gathered_w_in)
    placed = sorted((dst, src, width) for src, width, dst in W_IN_SEGMENTS)
    parts, pos = [], 0
    for dst, src, width in placed:
        if dst > pos:
            parts.append(jnp.zeros((dst - pos, D_MODEL), w_inT.dtype))
        parts.append(w_inT[src:src + width])
        pos = dst + width
    parts.append(jnp.zeros((U_WIDTH - pos, D_MODEL), w_inT.dtype))
    return jnp.concatenate(parts, axis=0)


def _layer_weights(gathered, full, i):
    w = _late_weights(gathered) if "w_out" in gathered else {}
    w["ffn1"] = _ffn_weights(gathered, "ffn1")
    if "w_in" in gathered:
        w["w_inT_p"] = _w_in_operand(gathered["w_in"])
    eye = jnp.eye(LRU_BLOCKS, dtype=F32)

    def dense(blocks):
        return jnp.einsum("ncd,nm->ncmd", blocks, eye).reshape(BRANCH, BRANCH).astype(BF16)

    def vec(name):
        return full[name][i].reshape(1, -1)

    cw = jnp.moveaxis(gathered["conv_w"], 0, 1).reshape(4, BRANCH)
    w_g2 = jnp.moveaxis(gathered["gla_w_g2"], 0, 1).reshape(GLA_LOWRANK, GLA_QK)
    w["lru"] = dict(cw0=cw[0:1], cw1=cw[1:2], cw2=cw[2:3], cw3=cw[3:4], conv_b=vec("conv_b"),
                    wa=dense(full["lru_wa"][i]), wx=dense(full["lru_wx"][i]), ba=vec("lru_ba"), bx=vec("lru_bx"),
                    lam=vec("lru_lambda"))
    hq = jnp.arange(GLA_QK) // GLA_DK
    hv = jnp.arange(GLA_V) // GLA_DV
    w["gla"] = dict(wg2=jnp.pad(w_g2, ((0, LANES - GLA_LOWRANK), (0, 0))).astype(BF16),
                    bg=vec("gla_b_g"), ng=vec("gla_norm_g"), bd=(hv[:, None] == hq[None, :]).astype(F32))
    w["bfp"] = jnp.pad(vec("fox_b_f"), ((0, 0), (0, LANES - FOX_HEADS)))
    for name in ("ln1_g", "ln1_b", "ln2_g", "ln2_b", "ln3_g", "ln3_b", "ln4_g", "ln4_b", "ple_b_gate"):
        w[name] = vec(name)
    return w


def _layer_fwd(x0, p_i, w, S, side=None, on_side=None, first_side=None, on_first=None):
    x1, s1, first_out = _ffn_fwd("ffn1", x0, *w["ffn1"], w["ln1_g"], w["ln1_b"], S, first_side)
    if on_first is not None:
        on_first(first_out)
    x2, s2, side_out = _mixer_fwd(x1, w, S, side, on_side)
    x3, s3, _ = _ffn_fwd("ffn2", x2, *w["ffn2"], w["ln3_g"], w["ln3_b"], S)
    x4, s4 = _ple_fwd(x3, p_i, w, S)
    return x4, (s1, s2, s3, s4), side_out


def _layer_bwd(dx4, saved, w, S, make_side=None, gdt_a=F32, gdt_b=F32, make_first=None, make_last=None):
    s1, s2, s3, s4 = saved
    dx3, g4 = _ple_bwd(dx4, s4, w, S, gdt_b)
    dx2, g3, _ = _ffn_bwd("ffn2", dx3, s3, *w["ffn2"], w["ln3_g"], S, gdt_b)
    late = dict(ffn2_w_upT=g3["w_upT"], ffn2_w_down=g3["w_down"], ple_w_projT=g4["ple_w_projT"],
                ple_w_gate=g4["ple_w_gate"])
    mixer_side = None if make_side is None else (lambda mix: make_side({**late, **mix}))
    dx1, g2, side_out = _mixer_bwd(dx2, s2, w, S, mixer_side, gdt_a, gdt_b)
    last = None if make_last is None else (
        lambda g: make_last({"ffn1_w_upT": g["w_upT"], "ffn1_w_down": g["w_down"]}))
    first = None if make_first is None else make_first(g2)
    dx0, g1, last_out = _ffn_bwd("ffn1", dx1, s1, *w["ffn1"], w["ln1_g"], S, gdt_a, last, first)
    grads = dict(g2)
    grads.update(g4)
    grads.update(late)
    grads.update(ffn1_w_upT=g1["w_upT"], ffn1_w_down=g1["w_down"], ln1_g=g1["ln_g"], ln1_b=g1["ln_b"],
                 ln3_g=g3["ln_g"], ln3_b=g3["ln_b"])
    return dx0, grads, side_out, last_out


def _travel_grads(grads, names):
    return [_dest_pieces(n, grads[n + "T" if n in COLUMN_SHARDED else n]) for n in names]


def _local_step(x, p, target, gathered0, rest, full, overlap):
    S = x.shape[0]
    names = [n for n, _ in SHARDED]
    early = [n for n in names if n in EARLY_WEIGHTS]
    late = [n for n in names if n not in EARLY_WEIGHTS]
    w0 = _layer_weights(gathered0, full, 0)
    if not overlap:
        h, saved0, _ = _layer_fwd(x, p[0], w0, S)
        w1 = _layer_weights(rest, full, 1)
        h, saved1, _ = _layer_fwd(h, p[1], w1, S)
    else:
        w_in0, late0, early1, late1 = rest
        h, saved0, got = _layer_fwd(x, p[0], w0, S, _gather_job(list(late0) + list(early1)),
                                    lambda got: w0.update(_late_weights(dict(zip(late, got[:len(late)])))),
                                    _gather_job([w_in0]), lambda got: w0.update(w_inT_p=_w_in_operand(got[0])))
        w1 = _layer_weights(dict(zip(early, got[len(late):])), full, 1)
        h, saved1, _ = _layer_fwd(h, p[1], w1, S, _gather_job(list(late1)),
                                  lambda got: w1.update(_late_weights(dict(zip(late, got)))))

    def loss_fn(y, t):
        err = y - t
        return [err * (1.0 / D_MODEL)], [_colsum(err * err) * (0.5 / D_MODEL)]

    (dy,), (lsum,) = _rowwise("loss", loss_fn, [h, target], [], [(D_MODEL, F32)], [D_MODEL], S)
    loss = jnp.sum(lsum)
    if not overlap:
        dy, g1, _, _ = _layer_bwd(dy, saved1, w1, S)
        dy, g0, _, _ = _layer_bwd(dy, saved0, w0, S)
        return loss, dy, [g0, g1], {}

    ffn_early = [n for n in early if n.startswith("ffn1")]
    mix_early = [n for n in early if not n.startswith("ffn1")]

    def two_stage(tag, which):
        def make(g):
            dest = _travel_grads(g, which)
            got = _sibling_swap_multi("grad_sibling_swap_" + tag, dest)
            core = lax.axis_index("c").astype(jnp.int32).reshape(1)
            return _chip_job([_pair_add("grad_pair_add_" + n, core, _as_rows(d, 2), _as_rows(a, 1))
                              for n, d, a in zip(which, dest, got)])
        return make

    dy, g1, late1_pieces, _ = _layer_bwd(dy, saved1, w1, S, lambda g: _scatter_job(_travel_grads(g, late)),
                                         BF16, BF16)
    dy, g0, pieces, (mix0_pieces, ffn0_pieces) = _layer_bwd(
        dy, saved0, w0, S, lambda g: _scatter_job(_travel_grads(g1, early) + _travel_grads(g, late)), F32, BF16,
        two_stage("mixer", mix_early), two_stage("ffn", ffn_early))
    exchanged = {(1, n): a for n, a in zip(late, late1_pieces)}
    exchanged.update({(1, n): a for n, a in zip(early, pieces[:len(early)])})
    exchanged.update({(0, n): a for n, a in zip(late, pieces[len(early):])})
    exchanged.update({(0, n): a for n, a in zip(mix_early, mix0_pieces)})
    exchanged.update({(0, n): a for n, a in zip(ffn_early, ffn0_pieces)})
    return loss, dy, [g0, g1], exchanged


def kernel(x, p, ffn1_w_up, ffn1_w_down, ln1_g, ln1_b, w_in, conv_w, conv_b, lru_wa, lru_ba, lru_wx, lru_bx, lru_lambda, gla_w_g2, gla_b_g, gla_norm_g, fox_b_f, w_branch, w_out, ln2_g, ln2_b, ffn2_w_up, ffn2_w_down, ln3_g, ln3_b, ple_w_proj, ple_w_gate, ple_b_gate, ln4_g, ln4_b, loss_target, m_ffn1_w_up, m_ffn1_w_down, m_ln1_g, m_ln1_b, m_w_in, m_conv_w, m_conv_b, m_lru_wa, m_lru_ba, m_lru_wx, m_lru_bx, m_lru_lambda, m_gla_w_g2, m_gla_b_g, m_gla_norm_g, m_fox_b_f, m_w_branch, m_w_out, m_ln2_g, m_ln2_b, m_ffn2_w_up, m_ffn2_w_down, m_ln3_g, m_ln3_b, m_ple_w_proj, m_ple_w_gate, m_ple_b_gate, m_ln4_g, m_ln4_b, v_ffn1_w_up, v_ffn1_w_down, v_ln1_g, v_ln1_b, v_w_in, v_conv_w, v_conv_b, v_lru_wa, v_lru_ba, v_lru_wx, v_lru_bx, v_lru_lambda, v_gla_w_g2, v_gla_b_g, v_gla_norm_g, v_fox_b_f, v_w_branch, v_w_out, v_ln2_g, v_ln2_b, v_ffn2_w_up, v_ffn2_w_down, v_ln3_g, v_ln3_b, v_ple_w_proj, v_ple_w_gate, v_ple_b_gate, v_ln4_g, v_ln4_b):
    env = dict(locals())
    wts = {n: env[n] for n in WEIGHTS}
    ms = {n: env["m_" + n] for n in WEIGHTS}
    vs = {n: env["v_" + n] for n in WEIGHTS}
    sharded = [n for n, _ in SHARDED]

    def travel(n, a):
        return jnp.swapaxes(a, -1, -2) if n in COLUMN_SHARDED else a

    shards = {(i, n): travel(n, wts[n][i]) if n in SHARDED_F32_GATHER else travel(n, wts[n][i]).astype(BF16)
              for i in range(DEPTH) for n in sharded}
    early = [n for n in sharded if n in EARLY_WEIGHTS]
    late = [n for n in sharded if n not in EARLY_WEIGHTS]
    first = [n for n in early if n != "w_in"]
    gathered0 = dict(zip(first, _allgather_multi("gather_weights", [shards[0, n] for n in first])))
    rest = (shards[0, "w_in"], [shards[0, n] for n in late], [shards[1, n] for n in early],
            [shards[1, n] for n in late])
    full = {n: wts[n] for n in REPLICATED}

    loss_part, grad_x, layer_grads, pieces = _local_step(x[0], p[:, 0], loss_target[0], gathered0, rest, full, True)
    loss = lax.psum(loss_part, MESH_AXES)

    rep = list(REPLICATED)
    rep_grads = [jnp.stack([layer_grads[i][n] for i in range(DEPTH)]).reshape(wts[n].shape) for n in rep]
    (gr,) = _allgather_multi("grad_gather_replicated", [_pack(rep_grads, F32)])

    kinds = ("grad", "delta", "new_m", "new_v")
    out = {}
    for n in sharded:
        local = [_as_rows(travel(n, pieces[i, n].reshape((-1,) + shards[i, n].shape)), 1) for i in range(DEPTH)]
        res = _adamw("adamw_" + n, local, _as_rows(wts[n], 1), _as_rows(ms[n], 1), _as_rows(vs[n], 1))
        for kind, arr in zip(kinds, res):
            out[kind + "_" + n] = arr.reshape(wts[n].shape)
    res = _adamw("adamw_replicated", [gr], _pack([wts[n] for n in rep], F32)[None],
                 _pack([ms[n] for n in rep], F32)[None], _pack([vs[n] for n in rep], F32)[None])
    shapes = [wts[n].shape for n in rep]
    for kind, buf in zip(kinds, res):
        for n, arr in zip(rep, _unpack(buf[0], shapes)):
            out[kind + "_" + n] = arr
    return (loss, grad_x[None], *[out["grad_" + n] for n in WEIGHTS], *[out["delta_" + n] for n in WEIGHTS],
            *[out["new_m_" + n] for n in WEIGHTS], *[out["new_v_" + n] for n in WEIGHTS])
```
